```python
import math
import jax, jax.numpy as jnp
from jax import lax
import numpy as np

D_MODEL = 1024
BATCH = 8
SEQ = 2048
DEPTH = 1
DEC_BATCH = 128
DEC_SEQ = 4
PAST_LEN = 16384
PAGE_SIZE = 128

D_A = D_MODEL // 2
H_A = 4
DK = D_A // H_A
DV = D_A // H_A
D_B = D_MODEL - D_A
POOL_WINDOWS = (2, 4, 8, 16)
POOL_GROUP = D_B // len(POOL_WINDOWS)
POOL_HIST = max(POOL_WINDOWS) - 1
D_MIX = D_A + D_B
D_IN = 4 * D_A + D_B
CHUNK = 64
N_EXPERTS = 32
TOP_K = 4
D_FF = 1024
SWIGLU_LIMIT = 7.0
SWIGLU_ALPHA = 1.702
EPS = 1e-6

kernel_name = "hymba_hgrn2_pool_moe_adaln_step"


def rmsnorm(x, w):
    xf = x.astype(jnp.float32)
    y = xf * lax.rsqrt(jnp.mean(xf * xf, axis=-1, keepdims=True) + EPS) * w.astype(jnp.float32)
    return y.astype(x.dtype)


def hgrn2_mix(q, f_pre, i, g, lb, head_norm, S0):
    f32 = jnp.float32
    Bn, T, _ = q.shape
    f = lb.astype(f32) + (1.0 - lb.astype(f32)) * jax.nn.sigmoid(f_pre.astype(f32))
    logf = jnp.log(f)
    k = 1.0 - f
    qa = jax.nn.silu(q.astype(f32))

    def heads(a):
        return a.reshape(Bn, T, H_A, -1).transpose(0, 2, 1, 3)

    qh, kh, gh, vh = heads(qa), heads(k), heads(logf), heads(i.astype(f32))
    C = min(CHUNK, T)
    n = -(-T // C)
    pad = n * C - T

    def chunks(a):
        a = jnp.pad(a, ((0, 0), (0, 0), (0, pad), (0, 0)))
        return a.reshape(Bn, H_A, n, C, a.shape[-1]).transpose(2, 0, 1, 3, 4)

    mask = jnp.tril(jnp.ones((C, C), dtype=bool))[:, :, None]

    def step(S, inp):
        qc, kc, vc, gc = inp
        b = jnp.cumsum(gc, axis=2)
        o_inter = jnp.einsum('bhtk,bhkv->bhtv', qc * jnp.exp(b), S)
        diff = b[:, :, :, None, :] - b[:, :, None, :, :]
        decay = jnp.exp(jnp.where(mask, diff, -jnp.inf))
        A = jnp.einsum('bhtk,bhsk,bhtsk->bhts', qc, kc, decay)
        o = o_inter + jnp.einsum('bhts,bhsv->bhtv', A, vc)
        b_last = b[:, :, -1:, :]
        S_new = jnp.exp(b_last[:, :, 0, :, None]) * S + jnp.einsum(
            'bhsk,bhsv->bhkv', kc * jnp.exp(b_last - b), vc)
        return S_new, o

    S, o = lax.scan(step, S0.astype(f32), (chunks(qh), chunks(kh), chunks(vh), chunks(gh)))
    o = o.transpose(1, 2, 0, 3, 4).reshape(Bn, H_A, n * C, DV)[:, :, :T]
    o = o * lax.rsqrt(jnp.mean(o * o, axis=-1, keepdims=True) + EPS) * head_norm.astype(f32)[None, :, None, :]
    o = o.transpose(0, 2, 1, 3).reshape(Bn, T, D_A) * jax.nn.silu(g.astype(f32))
    return o.astype(q.dtype), S


def pool_mix(v, hist, start, w_pool, pool_scale):
    f32 = jnp.float32
    T = v.shape[1]
    full = jnp.concatenate([hist.astype(v.dtype), v], axis=1)
    cs = jnp.pad(jnp.cumsum(full.astype(f32), axis=1), ((0, 0), (1, 0), (0, 0)))
    pos = start + jnp.arange(T) + 1
    outs = []
    for gi, w in enumerate(POOL_WINDOWS):
        ch = slice(gi * POOL_GROUP, (gi + 1) * POOL_GROUP)
        s = cs[:, POOL_HIST + 1:POOL_HIST + 1 + T, ch] - cs[:, POOL_HIST + 1 - w:POOL_HIST + 1 - w + T, ch]
        cnt = jnp.minimum(pos, w).astype(f32)[None, :, None]
        d = s / cnt - v[:, :, ch].astype(f32)
        outs.append(jnp.einsum('btc,cd->btd', d, w_pool[gi].astype(f32)))
    y = jnp.concatenate(outs, axis=-1) * pool_scale.astype(f32)
    return y.astype(v.dtype), full[:, -POOL_HIST:]


def moe(h, w_router, b_router, w_gu, b_gu, w_down, b_down):
    shp = h.shape
    hf = h.reshape(-1, shp[-1])
    n = hf.shape[0]
    logits = (hf @ w_router + b_router).astype(jnp.float32)
    top_val, top_idx = lax.top_k(logits, TOP_K)
    probs = jax.nn.softmax(top_val, axis=-1)
    flat_e = top_idx.reshape(-1)
    flat_tok = jnp.arange(n * TOP_K) // TOP_K
    order = jnp.argsort(flat_e)
    e_sorted = flat_e[order]
    tok_sorted = flat_tok[order]
    group_sizes = jnp.bincount(flat_e, length=N_EXPERTS).astype(jnp.int32)
    xs = hf[tok_sorted]
    gu = lax.ragged_dot(xs, w_gu, group_sizes) + b_gu[e_sorted]
    gate = jnp.minimum(gu[:, :D_FF], SWIGLU_LIMIT)
    up = jnp.clip(gu[:, D_FF:], -SWIGLU_LIMIT, SWIGLU_LIMIT)
    act = gate * jax.nn.sigmoid(SWIGLU_ALPHA * gate) * (up + 1.0)
    out = lax.ragged_dot(act.astype(w_down.dtype), w_down, group_sizes) + b_down[e_sorted]
    out = out * probs.reshape(-1)[order][:, None].astype(out.dtype)
    y = jax.ops.segment_sum(out, tok_sorted, num_segments=n)
    return y.reshape(shp).astype(h.dtype)


def layer(x, c, S0, hist, start, lb, w_ada, b_ada, norm1, norm2, w_in, hgrn_norm,
          w_pool, pool_scale, w_out, w_router, b_router, w_gu, b_gu, w_down, b_down):
    mod = (jax.nn.silu(c) @ w_ada + b_ada)[:, None, :]
    sh1, sc1, g1, sh2, sc2, g2 = jnp.split(mod, 6, axis=-1)
    h = rmsnorm(x, norm1) * (1 + sc1) + sh1
    u = h @ w_in
    q, f_pre, i, g, v = jnp.split(u, [D_A, 2 * D_A, 3 * D_A, 4 * D_A], axis=-1)
    o_a, S = hgrn2_mix(q, f_pre, i, g, lb, hgrn_norm, S0)
    o_b, new_hist = pool_mix(v, hist, start, w_pool, pool_scale)
    mix = jnp.concatenate([o_a, o_b], axis=-1) @ w_out
    x = x + g1 * mix
    h2 = rmsnorm(x, norm2) * (1 + sc2) + sh2
    x = x + g2 * moe(h2, w_router, b_router, w_gu, b_gu, w_down, b_down)
    return x, S, new_hist


def setup_inputs(seed: int = 0) -> dict:
    key = jax.random.key(seed)
    ks = jax.random.split(key, 24)
    f32 = jnp.float32
    nrm = lambda k, shp, s: jax.random.normal(k, shp, f32) * s
    return {
        "x_prompt": nrm(ks[0], (BATCH, SEQ, D_MODEL), 1.0),
        "x_sample": nrm(ks[1], (DEC_BATCH, DEC_SEQ, D_MODEL), 1.0),
        "state_hgrn": nrm(ks[2], (DEPTH, DEC_BATCH, H_A, DK, DV), 0.5),
        "state_pool": nrm(ks[3], (DEPTH, DEC_BATCH, POOL_HIST, D_B), 1.0),
        "c_prompt": nrm(ks[4], (BATCH, D_MODEL), 1.0),
        "c_sample": nrm(ks[5], (DEC_BATCH, D_MODEL), 1.0),
        "w_ada": nrm(ks[6], (DEPTH, D_MODEL, 6 * D_MODEL), 0.5 * D_MODEL ** -0.5),
        "b_ada": nrm(ks[7], (DEPTH, 6 * D_MODEL), 0.02),
        "norm1": 1.0 + nrm(ks[8], (DEPTH, D_MODEL), 0.02),
        "norm2": 1.0 + nrm(ks[9], (DEPTH, D_MODEL), 0.02),
        "w_in": nrm(ks[10], (DEPTH, D_MODEL, D_IN), D_MODEL ** -0.5),
        "lower_bounds": nrm(ks[11], (DEPTH + 1, D_A), 0.5),
        "hgrn_norm": 1.0 + nrm(ks[12], (DEPTH, H_A, DV), 0.02),
        "w_pool": nrm(ks[13], (DEPTH, len(POOL_WINDOWS), POOL_GROUP, POOL_GROUP), POOL_GROUP ** -0.5),
        "pool_scale": 1.0 + nrm(ks[14], (DEPTH, D_B), 0.1),
        "w_out": nrm(ks[15], (DEPTH, D_MIX, D_MODEL), D_MIX ** -0.5),
        "w_router": nrm(ks[16], (DEPTH, D_MODEL, N_EXPERTS), D_MODEL ** -0.5),
        "b_router": nrm(ks[17], (DEPTH, N_EXPERTS), 0.01),
        "w_gu": nrm(ks[18], (DEPTH, N_EXPERTS, D_MODEL, 2 * D_FF), D_MODEL ** -0.5),
        "b_gu": nrm(ks[19], (DEPTH, N_EXPERTS, 2 * D_FF), 0.02),
        "w_down": nrm(ks[20], (DEPTH, N_EXPERTS, D_FF, D_MODEL), D_FF ** -0.5),
        "b_down": nrm(ks[21], (DEPTH, N_EXPERTS, D_MODEL), 0.02),
        "final_norm": 1.0 + nrm(ks[22], (D_MODEL,), 0.02),
    }


def reference(x_prompt, x_sample, state_hgrn, state_pool, c_prompt, c_sample, w_ada, b_ada,
              norm1, norm2, w_in, lower_bounds, hgrn_norm, w_pool, pool_scale, w_out,
              w_router, b_router, w_gu, b_gu, w_down, b_down, final_norm):
    lbs = jnp.cumsum(jax.nn.softmax(lower_bounds.astype(jnp.float32), axis=0), axis=0)
    yp, ys = x_prompt, x_sample
    S0_prompt = jnp.zeros((x_prompt.shape[0], H_A, DK, DV), jnp.float32)
    hist_prompt = jnp.zeros((x_prompt.shape[0], POOL_HIST, D_B), x_prompt.dtype)
    Sp_l, Hp_l, Ss_l, Hs_l = [], [], [], []
    for l in range(DEPTH):
        params = (w_ada[l], b_ada[l], norm1[l], norm2[l], w_in[l], hgrn_norm[l], w_pool[l],
                  pool_scale[l], w_out[l], w_router[l], b_router[l], w_gu[l], b_gu[l],
                  w_down[l], b_down[l])
        yp, Sp, Hp = layer(yp, c_prompt, S0_prompt, hist_prompt, 0, lbs[l], *params)
        ys, Ss, Hs = layer(ys, c_sample, state_hgrn[l], state_pool[l], PAST_LEN, lbs[l], *params)
        Sp_l.append(Sp.astype(state_hgrn.dtype))
        Hp_l.append(Hp.astype(state_pool.dtype))
        Ss_l.append(Ss.astype(state_hgrn.dtype))
        Hs_l.append(Hs.astype(state_pool.dtype))
    y_prompt = rmsnorm(yp, final_norm)
    y_sample = rmsnorm(ys, final_norm)
    return (y_prompt, y_sample, jnp.stack(Sp_l), jnp.stack(Hp_l), jnp.stack(Ss_l), jnp.stack(Hs_l))
```

```python
import functools

import jax
import jax.numpy as jnp
from jax import lax
from jax.experimental import pallas as pl
from jax.experimental.pallas import tpu as pltpu

F32, BF16, I32 = jnp.float32, jnp.bfloat16, jnp.int32

D_MODEL = 1024
D_A = 512
H_A = 4
DK = 128
DV = 128
D_B = 512
POOL_WINDOWS = (2, 4, 8, 16)
POOL_GROUP = 128
POOL_HIST = 15
D_IN = 4 * D_A + D_B
N_EXPERTS = 32
TOP_K = 4
D_FF = 1024
SWIGLU_LIMIT = 7.0
SWIGLU_ALPHA = 1.702
EPS = 1e-6

SUB = 16
ROW_TILE = 256
TOK_TILE = 256
VMEM_LIMIT = 56 * 1024 * 1024


def _cparams(sem):
    return pltpu.CompilerParams(dimension_semantics=sem, vmem_limit_bytes=VMEM_LIMIT)


def _silu(x):
    return x * jax.nn.sigmoid(x)


def _mod_spec(per_token, tm, tiles_per_seq, col):
    if per_token:
        return pl.BlockSpec((tm, D_MODEL), lambda i: (i, col))
    return pl.BlockSpec((None, 1, D_MODEL), lambda i: (i // tiles_per_seq, 0, col))


def _ada_kernel(c_ref, w_ref, b_ref, o_ref):
    c = c_ref[...]
    o_ref[...] = jnp.dot(_silu(c).astype(BF16), w_ref[...].astype(BF16),
                         preferred_element_type=F32) + b_ref[...]


def _ada(c_all, w_ada, b_ada):
    rows = c_all.shape[0]
    n = w_ada.shape[1]
    return pl.pallas_call(
        _ada_kernel,
        grid=(n // D_MODEL,),
        in_specs=[pl.BlockSpec((rows, D_MODEL), lambda j: (0, 0)),
                  pl.BlockSpec((D_MODEL, D_MODEL), lambda j: (0, j)),
                  pl.BlockSpec((1, D_MODEL), lambda j: (0, j))],
        out_specs=pl.BlockSpec((rows, D_MODEL), lambda j: (0, j)),
        out_shape=jax.ShapeDtypeStruct((rows, n), F32),
        compiler_params=_cparams(("arbitrary",)),
        name="ada",
    )(c_all, w_ada, b_ada.reshape(1, n))


def _inproj_kernel(x_ref, sh_ref, sc_ref, n1_ref, w_ref, u_ref):
    x = x_ref[...]
    h = x * lax.rsqrt(jnp.mean(x * x, axis=-1, keepdims=True) + EPS) * n1_ref[...]
    h = h * (1.0 + sc_ref[...]) + sh_ref[...]
    u_ref[...] = jnp.dot(h.astype(BF16), w_ref[...], preferred_element_type=F32)


def _inproj(x, mod, per_token, seq, norm1, w_in_bf):
    n = x.shape[0]
    tm = min(512, n)
    tps = max(seq // tm, 1)
    return pl.pallas_call(
        _inproj_kernel,
        grid=(n // tm,),
        in_specs=[pl.BlockSpec((tm, D_MODEL), lambda i: (i, 0)),
                  _mod_spec(per_token, tm, tps, 0),
                  _mod_spec(per_token, tm, tps, 1),
                  pl.BlockSpec((1, D_MODEL), lambda i: (0, 0)),
                  pl.BlockSpec((D_MODEL, D_IN), lambda i: (0, 0))],
        out_specs=pl.BlockSpec((tm, D_IN), lambda i: (i, 0)),
        out_shape=jax.ShapeDtypeStruct((n, D_IN), F32),
        compiler_params=_cparams(("arbitrary",)),
        name="inproj",
    )(x, mod, mod, norm1.reshape(1, D_MODEL), w_in_bf)


def _block_select():
    r = lax.broadcasted_iota(I32, (SUB, SUB * DK), 0)
    c = lax.broadcasted_iota(I32, (SUB, SUB * DK), 1) >> (DK.bit_length() - 1)
    return (r == c).astype(BF16)


def _hgrn_block(q, fp, iv, g, lb, hn, st_ref, sel_ref, valid):
    f = lb + (1.0 - lb) * jax.nn.sigmoid(fp)
    logf = jnp.log(f)
    kk = 1.0 - f
    if valid < SUB:
        live = lax.broadcasted_iota(I32, (SUB, D_A), 0) < valid
        logf = jnp.where(live, logf, 0.0)
        kk = jnp.where(live, kk, 0.0)
    qa = _silu(q)

    r = lax.broadcasted_iota(I32, (SUB, SUB), 0)
    c = lax.broadcasted_iota(I32, (SUB, SUB), 1)
    tri = (r >= c).astype(BF16)
    hi = logf.astype(BF16)
    lo = (logf - hi.astype(F32)).astype(BF16)
    b = (jnp.dot(tri, hi, preferred_element_type=F32)
         + jnp.dot(tri, lo, preferred_element_type=F32))
    bend = b[SUB - 1:SUB, :]
    qt = (qa * jnp.exp(b)).astype(BF16)
    kh = (kk * jnp.exp(bend - b)).astype(BF16)
    dec = jnp.exp(bend)
    ivb = iv.astype(BF16)

    rows = []
    for h in range(H_A):
        sl = slice(DK * h, DK * (h + 1))
        bh, qh, kkh = b[:, sl], qa[:, sl], kk[:, sl]
        ps = []
        for s in range(SUB):
            e = jnp.exp(jnp.minimum(bh - bh[s:s + 1, :], 0.0))
            ps.append((e * (qh * kkh[s:s + 1, :])).astype(BF16))
        rows.append(jnp.concatenate(ps, axis=1))
    pcat = jnp.concatenate(rows, axis=0)
    a = lax.dot_general(pcat, sel_ref[...], (((1,), (1,)), ((), ())),
                        preferred_element_type=F32)
    tr = lax.broadcasted_iota(I32, (H_A * SUB, SUB), 0) & (SUB - 1)
    tc = lax.broadcasted_iota(I32, (H_A * SUB, SUB), 1)
    a = jnp.where(tr >= tc, a, 0.0).astype(BF16)

    outs = []
    for h in range(H_A):
        sl = slice(DK * h, DK * (h + 1))
        st = st_ref[h]
        o = lax.dot_general(qt[:, sl], st.astype(BF16), (((1,), (1,)), ((), ())),
                            preferred_element_type=F32)
        o = o + jnp.dot(a[SUB * h:SUB * (h + 1), :], ivb[:, sl], preferred_element_type=F32)
        ut = lax.dot_general(ivb[:, sl], kh[:, sl], (((0,), (0,)), ((), ())),
                             preferred_element_type=F32)
        st_ref[h] = st * dec[:, sl] + ut
        o = o * lax.rsqrt(jnp.mean(o * o, axis=-1, keepdims=True) + EPS) * hn[:, sl]
        outs.append(o)
    return jnp.concatenate(outs, axis=1) * _silu(g)


def _hgrn_prompt_kernel(q_ref, f_ref, i_ref, g_ref, lb_ref, hn_ref, o_ref, s_ref, st_ref, sel_ref,
                        *, tt):
    t = pl.program_id(1)

    @pl.when(t == 0)
    def _():
        st_ref[...] = jnp.zeros_like(st_ref)
        sel_ref[...] = _block_select()

    lb = lb_ref[...]
    hn = hn_ref[...]

    def body(j, carry):
        rs = pl.ds(pl.multiple_of(j * SUB, SUB), SUB)
        o_ref[rs, :] = _hgrn_block(q_ref[rs, :], f_ref[rs, :], i_ref[rs, :], g_ref[rs, :],
                                   lb, hn, st_ref, sel_ref, SUB)
        return carry

    lax.fori_loop(0, tt // SUB, body, 0)

    @pl.when(t == pl.num_programs(1) - 1)
    def _():
        for h in range(H_A):
            s_ref[h] = st_ref[h].T


def _hgrn_prompt(u, batch, seq, lb, hn):
    tt = 256
    nt = seq // tt

    def col(cb):
        return pl.BlockSpec((tt, D_A), lambda b, t: (b * nt + t, cb))

    return pl.pallas_call(
        functools.partial(_hgrn_prompt_kernel, tt=tt),
        grid=(batch, nt),
        in_specs=[col(0), col(1), col(2), col(3),
                  pl.BlockSpec((1, D_A), lambda b, t: (0, 0)),
                  pl.BlockSpec((1, D_A), lambda b, t: (0, 0))],
        out_specs=[pl.BlockSpec((tt, D_A), lambda b, t: (b * nt + t, 0)),
                   pl.BlockSpec((None, H_A, DK, DV), lambda b, t: (b, 0, 0, 0))],
        out_shape=[jax.ShapeDtypeStruct((batch * seq, D_A), F32),
                   jax.ShapeDtypeStruct((batch, H_A, DK, DV), F32)],
        scratch_shapes=[pltpu.VMEM((H_A, DV, DK), F32), pltpu.VMEM((SUB, SUB * DK), BF16)],
        compiler_params=_cparams(("arbitrary", "arbitrary")),
        name="hgrn_prompt",
    )(u, u, u, u, lb, hn)


def _hgrn_sample_kernel(q_ref, f_ref, i_ref, g_ref, lb_ref, hn_ref, s0_ref, o_ref, s_ref,
                        st_ref, sel_ref, pad_ref, *, seq):
    for h in range(H_A):
        st_ref[h] = s0_ref[h].T
    sel_ref[...] = _block_select()
    pad_ref[...] = jnp.zeros_like(pad_ref)
    for n, ref in enumerate((q_ref, f_ref, i_ref, g_ref)):
        pad_ref[n, 0:seq, :] = ref[...]
    o = _hgrn_block(pad_ref[0], pad_ref[1], pad_ref[2], pad_ref[3], lb_ref[...], hn_ref[...],
                    st_ref, sel_ref, seq)
    o_ref[...] = o[0:seq, :]
    for h in range(H_A):
        s_ref[h] = st_ref[h].T


def _hgrn_sample(u3, s0, lb, hn):
    batch, seq, _ = u3.shape

    def col(cb):
        return pl.BlockSpec((None, seq, D_A), lambda b: (b, 0, cb))

    return pl.pallas_call(
        functools.partial(_hgrn_sample_kernel, seq=seq),
        grid=(batch,),
        in_specs=[col(0), col(1), col(2), col(3),
                  pl.BlockSpec((1, D_A), lambda b: (0, 0)),
                  pl.BlockSpec((1, D_A), lambda b: (0, 0)),
                  pl.BlockSpec((None, H_A, DK, DV), lambda b: (b, 0, 0, 0))],
        out_specs=[pl.BlockSpec((None, seq, D_A), lambda b: (b, 0, 0)),
                   pl.BlockSpec((None, H_A, DK, DV), lambda b: (b, 0, 0, 0))],
        out_shape=[jax.ShapeDtypeStruct((batch, seq, D_A), F32),
                   jax.ShapeDtypeStruct((batch, H_A, DK, DV), F32)],
        scratch_shapes=[pltpu.VMEM((H_A, DV, DK), F32), pltpu.VMEM((SUB, SUB * DK), BF16),
                        pltpu.VMEM((4, SUB, D_A), F32)],
        compiler_params=_cparams(("arbitrary",)),
        name="hgrn_sample",
    )(u3, u3, u3, u3, lb, hn, s0)


def _pool_groups(full_ref, base, tm, cnt_fn, wp_ref, ps_ref):
    outs = []
    for gi, w in enumerate(POOL_WINDOWS):
        ch = slice(POOL_GROUP * gi, POOL_GROUP * (gi + 1))
        v = full_ref[base:base + tm, ch]
        s = v
        for j in range(1, w):
            s = s + full_ref[base - j:base - j + tm, ch]
        d = s / cnt_fn(w) - v
        outs.append(jnp.dot(d.astype(BF16), wp_ref[gi], preferred_element_type=F32))
    return jnp.concatenate(outs, axis=1) * ps_ref[...]


def _pool_prompt_kernel(v_ref, prev_ref, wp_ref, ps_ref, o_ref, hist_ref, full_ref, *, tm):
    t = pl.program_id(1)
    prev = prev_ref[...]
    full_ref[0:16, :] = jnp.where(t == 0, 0.0, prev)
    full_ref[16:16 + tm, :] = v_ref[...]
    pos = (lax.broadcasted_iota(I32, (tm, 1), 0) + t * tm + 1).astype(F32)
    o_ref[...] = _pool_groups(full_ref, 16, tm, lambda w: jnp.minimum(pos, float(w)),
                              wp_ref, ps_ref)

    @pl.when(t == pl.num_programs(1) - 1)
    def _():
        hist_ref[...] = full_ref[16 + tm - POOL_HIST:16 + tm, :]


def _pool_prompt(u, batch, seq, w_pool_bf, pool_scale):
    tm = 256
    nt = seq // tm
    vcol = (4 * D_A) // D_B
    return pl.pallas_call(
        functools.partial(_pool_prompt_kernel, tm=tm),
        grid=(batch, nt),
        in_specs=[pl.BlockSpec((tm, D_B), lambda b, t: (b * nt + t, vcol)),
                  pl.BlockSpec((16, D_B),
                               lambda b, t: (jnp.maximum((b * nt + t) * (tm // 16) - 1, 0), vcol)),
                  pl.BlockSpec((len(POOL_WINDOWS), POOL_GROUP, POOL_GROUP), lambda b, t: (0, 0, 0)),
                  pl.BlockSpec((1, D_B), lambda b, t: (0, 0))],
        out_specs=[pl.BlockSpec((tm, D_B), lambda b, t: (b * nt + t, 0)),
                   pl.BlockSpec((None, POOL_HIST, D_B), lambda b, t: (b, 0, 0))],
        out_shape=[jax.ShapeDtypeStruct((batch * seq, D_B), F32),
                   jax.ShapeDtypeStruct((batch, POOL_HIST, D_B), F32)],
        scratch_shapes=[pltpu.VMEM((16 + tm, D_B), F32)],
        compiler_params=_cparams(("arbitrary", "arbitrary")),
        name="pool_prompt",
    )(u, u, w_pool_bf, pool_scale)


def _pool_sample_kernel(v_ref, hist_ref, wp_ref, ps_ref, o_ref, nh_ref, full_ref, *, seq, start):
    full_ref[0:1, :] = jnp.zeros((1, D_B), F32)
    full_ref[1:1 + POOL_HIST, :] = hist_ref[...]
    full_ref[16:16 + seq, :] = v_ref[...]
    pos = (lax.broadcasted_iota(I32, (seq, 1), 0) + start + 1).astype(F32)
    o_ref[...] = _pool_groups(full_ref, 16, seq, lambda w: jnp.minimum(pos, float(w)),
                              wp_ref, ps_ref)
    nh_ref[...] = full_ref[16 + seq - POOL_HIST:16 + seq, :]


def _pool_sample(u3, hist, start, w_pool_bf, pool_scale):
    batch, seq, _ = u3.shape
    vcol = (4 * D_A) // D_B
    return pl.pallas_call(
        functools.partial(_pool_sample_kernel, seq=seq, start=start),
        grid=(batch,),
        in_specs=[pl.BlockSpec((None, seq, D_B), lambda b: (b, 0, vcol)),
                  pl.BlockSpec((None, POOL_HIST, D_B), lambda b: (b, 0, 0)),
                  pl.BlockSpec((len(POOL_WINDOWS), POOL_GROUP, POOL_GROUP), lambda b: (0, 0, 0)),
                  pl.BlockSpec((1, D_B), lambda b: (0, 0))],
        out_specs=[pl.BlockSpec((None, seq, D_B), lambda b: (b, 0, 0)),
                   pl.BlockSpec((None, POOL_HIST, D_B), lambda b: (b, 0, 0))],
        out_shape=[jax.ShapeDtypeStruct((batch, seq, D_B), F32),
                   jax.ShapeDtypeStruct((batch, POOL_HIST, D_B), F32)],
        scratch_shapes=[pltpu.VMEM((16 + seq, D_B), F32)],
        compiler_params=_cparams(("arbitrary",)),
        name="pool_sample",
    )(u3, hist, w_pool_bf, pool_scale)


LANES = 128
SLAB = D_MODEL // LANES


def _store_slabs(ref, value):
    n = value.shape[0]
    for j in range(SLAB):
        ref[pl.ds(j, n, stride=SLAB), :] = value[:, LANES * j:LANES * (j + 1)]


def _load_slabs(ref, n):
    return [ref[pl.ds(j, n, stride=SLAB), :] for j in range(SLAB)]


def _slab_rows(row):
    return pl.ds(pl.multiple_of(row * SLAB, SLAB), SLAB)


def _split_dot(a, w_hi, w_lo):
    a_hi = a.astype(BF16)
    a_lo = (a - a_hi.astype(F32)).astype(BF16)
    return (jnp.dot(a_hi, w_hi, preferred_element_type=F32)
            + jnp.dot(a_hi, w_lo, preferred_element_type=F32)
            + jnp.dot(a_lo, w_hi, preferred_element_type=F32))


def _outproj_kernel(x_ref, oa_ref, ob_ref, g1_ref, sh2_ref, sc2_ref, n2_ref, woa_ref, wob_ref,
                    wrh_ref, wrl_ref, br_ref, x1_ref, h2_ref, idx_ref, prob_ref):
    mix = (jnp.dot(oa_ref[...].astype(BF16), woa_ref[...], preferred_element_type=F32)
           + jnp.dot(ob_ref[...].astype(BF16), wob_ref[...], preferred_element_type=F32))
    x1 = x_ref[...] + g1_ref[...] * mix
    x1_ref[...] = x1
    h2 = x1 * lax.rsqrt(jnp.mean(x1 * x1, axis=-1, keepdims=True) + EPS) * n2_ref[...]
    h2 = h2 * (1.0 + sc2_ref[...]) + sh2_ref[...]
    _store_slabs(h2_ref, h2)

    logits = _split_dot(h2, wrh_ref[...], wrl_ref[...]) + br_ref[...]
    tm = logits.shape[0]
    lane = lax.broadcasted_iota(I32, (tm, N_EXPERTS), 1)
    kcol = lax.broadcasted_iota(I32, (tm, TOP_K), 1)
    vals = jnp.zeros((tm, TOP_K), F32)
    idxs = jnp.zeros((tm, TOP_K), I32)
    for k in range(TOP_K):
        m = jnp.max(logits, axis=-1, keepdims=True)
        am = jnp.min(jnp.where(logits == m, lane, N_EXPERTS), axis=-1, keepdims=True)
        vals = jnp.where(kcol == k, m, vals)
        idxs = jnp.where(kcol == k, am, idxs)
        logits = jnp.where(lane == am, -jnp.inf, logits)
    e = jnp.exp(vals - vals[:, 0:1])
    prob_ref[...] = e / jnp.sum(e, axis=-1, keepdims=True)
    idx_ref[...] = idxs


def _outproj(x, o_a, o_b, mod, per_token, seq, norm2, w_out_bf, wr_hi, wr_lo, b_router):
    n = x.shape[0]
    tm = min(256, n)
    tps = max(seq // tm, 1)
    row = lambda width: pl.BlockSpec((tm, width), lambda i: (i, 0))
    const = lambda shape: pl.BlockSpec(shape, lambda i: (0,) * len(shape))
    return pl.pallas_call(
        _outproj_kernel,
        grid=(n // tm,),
        in_specs=[row(D_MODEL), row(D_A), row(D_B),
                  _mod_spec(per_token, tm, tps, 2),
                  _mod_spec(per_token, tm, tps, 3),
                  _mod_spec(per_token, tm, tps, 4),
                  const((1, D_MODEL)),
                  pl.BlockSpec((D_A, D_MODEL), lambda i: (0, 0)),
                  pl.BlockSpec((D_B, D_MODEL), lambda i: (1, 0)),
                  const((D_MODEL, N_EXPERTS)), const((D_MODEL, N_EXPERTS)), const((1, N_EXPERTS))],
        out_specs=[row(D_MODEL), pl.BlockSpec((tm * SLAB, LANES), lambda i: (i, 0)),
                   row(TOP_K), row(TOP_K)],
        out_shape=[jax.ShapeDtypeStruct((n, D_MODEL), F32),
                   jax.ShapeDtypeStruct((n * SLAB, LANES), F32),
                   jax.ShapeDtypeStruct((n, TOP_K), I32),
                   jax.ShapeDtypeStruct((n, TOP_K), F32)],
        compiler_params=_cparams(("arbitrary",)),
        name="outproj",
    )(x, o_a, o_b, mod, mod, mod, norm2.reshape(1, D_MODEL), w_out_bf, w_out_bf,
      wr_hi, wr_lo, b_router.reshape(1, N_EXPERTS))


def _rank_kernel(idx_ref, dest_ref, cnt_ref, carry_ref, *, te):
    phase = pl.program_id(0)
    i = pl.program_id(1)
    idx = idx_ref[...]
    lane = lax.broadcasted_iota(I32, (te, N_EXPERTS), 1)
    onehots = [(idx[:, k:k + 1] == lane) for k in range(TOP_K)]
    member = jnp.zeros((te, N_EXPERTS), F32)
    for oh in onehots:
        member = member + oh.astype(F32)
    colsum = jnp.sum(member, axis=0, keepdims=True)

    @pl.when((phase == 0) & (i == 0))
    def _():
        carry_ref[...] = jnp.zeros_like(carry_ref)

    @pl.when(phase == 0)
    def _():
        carry_ref[...] += colsum

    @pl.when((phase == 1) & (i == 0))
    def _():
        cnt = carry_ref[...]
        cnt_ref[...] = cnt.astype(I32)
        tiles = jnp.floor((cnt + (ROW_TILE - 1)) * (1.0 / ROW_TILE))
        r = lax.broadcasted_iota(I32, (N_EXPERTS, N_EXPERTS), 0)
        c = lax.broadcasted_iota(I32, (N_EXPERTS, N_EXPERTS), 1)
        before = (r < c).astype(BF16)
        carry_ref[...] = ROW_TILE * jnp.dot(tiles.astype(BF16), before, preferred_element_type=F32)

    @pl.when(phase == 1)
    def _():
        r = lax.broadcasted_iota(I32, (te, te), 0)
        c = lax.broadcasted_iota(I32, (te, te), 1)
        earlier = (r > c).astype(BF16)
        base = carry_ref[...] + jnp.dot(earlier, member.astype(BF16), preferred_element_type=F32)
        kcol = lax.broadcasted_iota(I32, (te, TOP_K), 1)
        dest = jnp.zeros((te, TOP_K), F32)
        for k, oh in enumerate(onehots):
            dk = jnp.sum(jnp.where(oh, base, 0.0), axis=-1, keepdims=True)
            dest = jnp.where(kcol == k, dk, dest)
        dest_ref[...] = dest.astype(I32)
        carry_ref[...] += colsum


def _rank(top_idx):
    n = top_idx.shape[0]
    te = 512
    return pl.pallas_call(
        functools.partial(_rank_kernel, te=te),
        grid=(2, n // te),
        in_specs=[pl.BlockSpec((te, TOP_K), lambda p, i: (i, 0))],
        out_specs=[pl.BlockSpec((te, TOP_K), lambda p, i: (p * i, 0)),
                   pl.BlockSpec((1, N_EXPERTS), lambda p, i: (0, 0))],
        out_shape=[jax.ShapeDtypeStruct((n, TOP_K), I32),
                   jax.ShapeDtypeStruct((1, N_EXPERTS), I32)],
        scratch_shapes=[pltpu.VMEM((1, N_EXPERTS), F32)],
        compiler_params=_cparams(("arbitrary", "arbitrary")),
        name="rank",
    )(top_idx)


def _dispatch_kernel(last_ref, ntile_ref, used_ref, dest_ref, h_ref, xs_ref, zero_ref, sem, zsem,
                     *, n_tiles):
    i = pl.program_id(0)

    @pl.when(i == 0)
    def _():
        zero_ref[...] = jnp.zeros_like(zero_ref)

        def clear(tile):
            rows = pl.ds(pl.multiple_of(tile * (ROW_TILE * SLAB), ROW_TILE * SLAB), ROW_TILE * SLAB)
            return pltpu.make_async_copy(zero_ref, xs_ref.at[rows], zsem)

        def over_tail(fn):
            def body(t, carry):
                fn(clear(t))
                return carry
            lax.fori_loop(used_ref[0], n_tiles, body, 0)

        for fn in (lambda c: c.start(), lambda c: c.wait()):
            for e in range(N_EXPERTS):
                @pl.when(ntile_ref[e] > 0)
                def _():
                    fn(clear(last_ref[e]))
            over_tail(fn)

    def body(r, carry):
        for k in range(TOP_K):
            d = dest_ref[r * TOP_K + k]
            pltpu.make_async_copy(h_ref.at[_slab_rows(r)], xs_ref.at[_slab_rows(d)], sem).start()
        return carry

    lax.fori_loop(0, TOK_TILE, body, 0)
    for k in range(TOP_K):
        pltpu.make_async_copy(h_ref, xs_ref.at[pl.ds(0, TOK_TILE * SLAB)], sem).wait()


def _dispatch(h2, dest_flat, last_tile, ntiles, used, n_tiles):
    n = h2.shape[0] // SLAB
    grid_spec = pltpu.PrefetchScalarGridSpec(
        num_scalar_prefetch=3,
        grid=(n // TOK_TILE,),
        in_specs=[pl.BlockSpec((TOK_TILE * TOP_K,), lambda i, *_: (i,), memory_space=pltpu.SMEM),
                  pl.BlockSpec((TOK_TILE * SLAB, LANES), lambda i, *_: (i, 0))],
        out_specs=pl.BlockSpec(memory_space=pl.ANY),
        scratch_shapes=[pltpu.VMEM((ROW_TILE * SLAB, LANES), F32),
                        pltpu.SemaphoreType.DMA(()), pltpu.SemaphoreType.DMA(())],
    )
    return pl.pallas_call(
        functools.partial(_dispatch_kernel, n_tiles=n_tiles),
        grid_spec=grid_spec,
        out_shape=jax.ShapeDtypeStruct((n_tiles * ROW_TILE * SLAB, LANES), F32),
        compiler_params=_cparams(("arbitrary",)),
        name="dispatch",
    )(last_tile, ntiles, used, dest_flat, h2)


def _experts_kernel(te_ref, used_ref, xs_ref, wgu_ref, bgu_ref, wd_ref, bd_ref, ys_ref,
                    wgu_bf, wd_bf):
    i = pl.program_id(0)
    prev = te_ref[jnp.maximum(i - 1, 0)]
    fresh = (i == 0) | (te_ref[i] != prev)

    @pl.when(fresh)
    def _():
        wgu_bf[...] = wgu_ref[...].astype(BF16)
        wd_bf[...] = wd_ref[...].astype(BF16)

    @pl.when(i < used_ref[0])
    def _():
        x = jnp.concatenate([p.astype(BF16) for p in _load_slabs(xs_ref, ROW_TILE)], axis=1)
        acc = jnp.zeros((ROW_TILE, D_MODEL), F32) + bd_ref[...]
        nch = 4
        cw = D_FF // nch
        for c in range(nch):
            gate = jnp.dot(x, wgu_bf[:, c * cw:(c + 1) * cw], preferred_element_type=F32)
            gate = jnp.minimum(gate + bgu_ref[:, c * cw:(c + 1) * cw], SWIGLU_LIMIT)
            up = jnp.dot(x, wgu_bf[:, D_FF + c * cw:D_FF + (c + 1) * cw], preferred_element_type=F32)
            up = jnp.clip(up + bgu_ref[:, D_FF + c * cw:D_FF + (c + 1) * cw], -SWIGLU_LIMIT, SWIGLU_LIMIT)
            act = gate * jax.nn.sigmoid(SWIGLU_ALPHA * gate) * (up + 1.0)
            acc = acc + jnp.dot(act.astype(BF16), wd_bf[c * cw:(c + 1) * cw, :],
                                preferred_element_type=F32)
        _store_slabs(ys_ref, acc)

    @pl.when(i >= used_ref[0])
    def _():
        ys_ref[...] = jnp.zeros_like(ys_ref)


def _experts(xs, tile_expert, used, w_gu, b_gu, w_down, b_down):
    rows = xs.shape[0] // SLAB
    nt = rows // ROW_TILE

    def tile(i, te, used):
        return (jnp.minimum(i, used[0] - 1), 0)

    grid_spec = pltpu.PrefetchScalarGridSpec(
        num_scalar_prefetch=2,
        grid=(nt,),
        in_specs=[pl.BlockSpec((ROW_TILE * SLAB, LANES), tile),
                  pl.BlockSpec((None, D_MODEL, 2 * D_FF), lambda i, te, used: (te[i], 0, 0)),
                  pl.BlockSpec((None, 1, 2 * D_FF), lambda i, te, used: (te[i], 0, 0)),
                  pl.BlockSpec((None, D_FF, D_MODEL), lambda i, te, used: (te[i], 0, 0)),
                  pl.BlockSpec((None, 1, D_MODEL), lambda i, te, used: (te[i], 0, 0))],
        out_specs=pl.BlockSpec((ROW_TILE * SLAB, LANES), lambda i, te, used: (i, 0)),
        scratch_shapes=[pltpu.VMEM((D_MODEL, 2 * D_FF), BF16), pltpu.VMEM((D_FF, D_MODEL), BF16)],
    )
    return pl.pallas_call(
        _experts_kernel,
        grid_spec=grid_spec,
        out_shape=jax.ShapeDtypeStruct((rows * SLAB, LANES), F32),
        compiler_params=_cparams(("arbitrary",)),
        name="experts",
    )(tile_expert, used, xs, w_gu, b_gu.reshape(N_EXPERTS, 1, 2 * D_FF), w_down,
      b_down.reshape(N_EXPERTS, 1, D_MODEL))


def _combine_kernel(dest_ref, ys_ref, x1_ref, p_ref, g2_ref, fn_ref, y_ref, buf_ref, sem, *, last):
    def body(r, carry):
        for k in range(TOP_K):
            d = dest_ref[r * TOP_K + k]
            pltpu.make_async_copy(ys_ref.at[_slab_rows(d)], buf_ref.at[k, _slab_rows(r)], sem).start()
        return carry

    lax.fori_loop(0, TOK_TILE, body, 0)
    for k in range(TOP_K):
        pltpu.make_async_copy(ys_ref.at[pl.ds(0, TOK_TILE * SLAB)], buf_ref.at[k], sem).wait()

    p = p_ref[...]
    groups = None
    for k in range(TOP_K):
        part = [p[:, k:k + 1] * g for g in _load_slabs(buf_ref.at[k], TOK_TILE)]
        groups = part if groups is None else [a + b for a, b in zip(groups, part)]
    moe = jnp.concatenate(groups, axis=1)
    x2 = x1_ref[...] + g2_ref[...] * moe
    if last:
        x2 = x2 * lax.rsqrt(jnp.mean(x2 * x2, axis=-1, keepdims=True) + EPS) * fn_ref[...]
    y_ref[...] = x2


def _combine(ys, dest_flat, x1, probs, mod, per_token, seq, tile0, final_norm, last):
    n = x1.shape[0]
    tm = TOK_TILE
    tps = max(seq // tm, 1)
    return pl.pallas_call(
        functools.partial(_combine_kernel, last=last),
        grid=(n // tm,),
        in_specs=[pl.BlockSpec((tm * TOP_K,), lambda i: (i + tile0,), memory_space=pltpu.SMEM),
                  pl.BlockSpec(memory_space=pl.ANY),
                  pl.BlockSpec((tm, D_MODEL), lambda i: (i, 0)),
                  pl.BlockSpec((tm, TOP_K), lambda i: (i + tile0, 0)),
                  _mod_spec(per_token, tm, tps, 5),
                  pl.BlockSpec((1, D_MODEL), lambda i: (0, 0))],
        out_specs=pl.BlockSpec((tm, D_MODEL), lambda i: (i, 0)),
        out_shape=jax.ShapeDtypeStruct((n, D_MODEL), F32),
        scratch_shapes=[pltpu.VMEM((TOP_K, tm * SLAB, LANES), F32), pltpu.SemaphoreType.DMA(())],
        compiler_params=_cparams(("arbitrary",)),
        name="combine",
    )(dest_flat, ys, x1, probs, mod, final_norm.reshape(1, D_MODEL))


def kernel(x_prompt, x_sample, state_hgrn, state_pool, c_prompt, c_sample, w_ada, b_ada, norm1,
           norm2, w_in, lower_bounds, hgrn_norm, w_pool, pool_scale, w_out, w_router, b_router,
           w_gu, b_gu, w_down, b_down, final_norm):
    bp, seq_p, _ = x_prompt.shape
    bs, seq_s, _ = x_sample.shape
    np_, ns = bp * seq_p, bs * seq_s
    depth = w_ada.shape[0]
    past_len = 16384

    lbs = jnp.cumsum(jax.nn.softmax(lower_bounds.astype(F32), axis=0), axis=0)
    xp = x_prompt.reshape(np_, D_MODEL)
    xs_ = x_sample.reshape(ns, D_MODEL)
    c_all = jnp.concatenate([c_prompt, c_sample], axis=0)

    sp_l, hp_l, ss_l, hs_l = [], [], [], []
    for l in range(depth):
        w_in_bf = w_in[l].astype(BF16)
        w_out_bf = w_out[l].astype(BF16)
        w_pool_bf = w_pool[l].astype(BF16)
        wr_hi = w_router[l].astype(BF16)
        wr_lo = (w_router[l] - wr_hi.astype(F32)).astype(BF16)
        lb = lbs[l].reshape(1, D_A)
        hn = hgrn_norm[l].reshape(1, D_A)
        psc = pool_scale[l].reshape(1, D_B)

        mod = _ada(c_all, w_ada[l], b_ada[l])
        mod_p = mod[:bp].reshape(bp, 1, 6 * D_MODEL)
        mod_s = jnp.repeat(mod[bp:], seq_s, axis=0)

        up = _inproj(xp, mod_p, False, seq_p, norm1[l], w_in_bf)
        us = _inproj(xs_, mod_s, True, seq_s, norm1[l], w_in_bf)
        us3 = us.reshape(bs, seq_s, D_IN)

        oa_p, s_p = _hgrn_prompt(up, bp, seq_p, lb, hn)
        oa_s, s_s = _hgrn_sample(us3, state_hgrn[l], lb, hn)
        ob_p, h_p = _pool_prompt(up, bp, seq_p, w_pool_bf, psc)
        ob_s, h_s = _pool_sample(us3, state_pool[l], past_len, w_pool_bf, psc)

        x1p, h2p, idxp, prp = _outproj(xp, oa_p, ob_p, mod_p, False, seq_p, norm2[l], w_out_bf,
                                       wr_hi, wr_lo, b_router[l])
        x1s, h2s, idxs, prs = _outproj(xs_, oa_s.reshape(ns, D_A), ob_s.reshape(ns, D_B), mod_s,
                                       True, seq_s, norm2[l], w_out_bf, wr_hi, wr_lo, b_router[l])

        h2 = jnp.concatenate([h2p, h2s], axis=0)
        top_idx = jnp.concatenate([idxp, idxs], axis=0)
        probs = jnp.concatenate([prp, prs], axis=0)
        n_tok = np_ + ns
        n_tiles = (n_tok * TOP_K) // ROW_TILE + N_EXPERTS

        dest, counts = _rank(top_idx)
        dest_flat = dest.reshape(n_tok * TOP_K)
        tiles_e = (counts.reshape(N_EXPERTS) + ROW_TILE - 1) // ROW_TILE
        ends = jnp.cumsum(tiles_e)
        used = ends[-1:].astype(I32)
        tile_expert = jnp.minimum(
            jnp.searchsorted(ends, jnp.arange(n_tiles, dtype=I32), side="right"),
            N_EXPERTS - 1).astype(I32)
        last_tile = (ends - 1).astype(I32)

        xsort = _dispatch(h2, dest_flat, last_tile, tiles_e.astype(I32), used, n_tiles)
        ysort = _experts(xsort, tile_expert, used, w_gu[l], b_gu[l], w_down[l], b_down[l])

        last = l == depth - 1
        xp = _combine(ysort, dest_flat, x1p, probs, mod_p, False, seq_p, 0, final_norm, last)
        xs_ = _combine(ysort, dest_flat, x1s, probs, mod_s, True, seq_s, np_ // TOK_TILE,
                       final_norm, last)

        sp_l.append(s_p)
        hp_l.append(h_p)
        ss_l.append(s_s)
        hs_l.append(h_s)

    return (xp.reshape(bp, seq_p, D_MODEL), xs_.reshape(bs, seq_s, D_MODEL),
            jnp.stack(sp_l), jnp.stack(hp_l), jnp.stack(ss_l), jnp.stack(hs_l))
```

```python
import functools

import jax
import jax.numpy as jnp
from jax import lax
from jax.experimental import pallas as pl
from jax.experimental.pallas import tpu as pltpu

F32, BF16, I32 = jnp.float32, jnp.bfloat16, jnp.int32

D_MODEL = 1024
D_A = 512
H_A = 4
DK = 128
DV = 128
D_B = 512
POOL_WINDOWS = (2, 4, 8, 16)
POOL_GROUP = 128
POOL_HIST = 15
D_IN = 4 * D_A + D_B
N_EXPERTS = 32
TOP_K = 4
D_FF = 1024
SWIGLU_LIMIT = 7.0
SWIGLU_ALPHA = 1.702
EPS = 1e-6

SUB = 16
ROW_TILE = 256
TOK_TILE = 256
VMEM_LIMIT = 56 * 1024 * 1024


def _cparams(sem):
    return pltpu.CompilerParams(dimension_semantics=sem, vmem_limit_bytes=VMEM_LIMIT)


def _silu(x):
    return x * jax.nn.sigmoid(x)


def _mod_spec(per_token, tm, tiles_per_seq, col):
    if per_token:
        return pl.BlockSpec((tm, D_MODEL), lambda i: (i, col))
    return pl.BlockSpec((None, 1, D_MODEL), lambda i: (i // tiles_per_seq, 0, col))


def _ada_kernel(c_ref, w_ref, b_ref, o_ref):
    c = c_ref[...]
    o_ref[...] = jnp.dot(_silu(c).astype(BF16), w_ref[...].astype(BF16),
                         preferred_element_type=F32) + b_ref[...]


def _ada(c_all, w_ada, b_ada):
    rows = c_all.shape[0]
    n = w_ada.shape[1]
    return pl.pallas_call(
        _ada_kernel,
        grid=(n // D_MODEL,),
        in_specs=[pl.BlockSpec((rows, D_MODEL), lambda j: (0, 0)),
                  pl.BlockSpec((D_MODEL, D_MODEL), lambda j: (0, j)),
                  pl.BlockSpec((1, D_MODEL), lambda j: (0, j))],
        out_specs=pl.BlockSpec((rows, D_MODEL), lambda j: (0, j)),
        out_shape=jax.ShapeDtypeStruct((rows, n), F32),
        compiler_params=_cparams(("arbitrary",)),
        name="ada",
    )(c_all, w_ada, b_ada.reshape(1, n))


def _inproj_kernel(x_ref, sh_ref, sc_ref, n1_ref, w_ref, u_ref):
    x = x_ref[...]
    h = x * lax.rsqrt(jnp.mean(x * x, axis=-1, keepdims=True) + EPS) * n1_ref[...]
    h = h * (1.0 + sc_ref[...]) + sh_ref[...]
    u_ref[...] = jnp.dot(h.astype(BF16), w_ref[...], preferred_element_type=F32)


def _inproj(x, mod, per_token, seq, norm1, w_in_bf):
    n = x.shape[0]
    tm = min(512, n)
    tps = max(seq // tm, 1)
    return pl.pallas_call(
        _inproj_kernel,
        grid=(n // tm,),
        in_specs=[pl.BlockSpec((tm, D_MODEL), lambda i: (i, 0)),
                  _mod_spec(per_token, tm, tps, 0),
                  _mod_spec(per_token, tm, tps, 1),
                  pl.BlockSpec((1, D_MODEL), lambda i: (0, 0)),
                  pl.BlockSpec((D_MODEL, D_IN), lambda i: (0, 0))],
        out_specs=pl.BlockSpec((tm, D_IN), lambda i: (i, 0)),
        out_shape=jax.ShapeDtypeStruct((n, D_IN), F32),
        compiler_params=_cparams(("arbitrary",)),
        name="inproj",
    )(x, mod, mod, norm1.reshape(1, D_MODEL), w_in_bf)


def _block_select():
    r = lax.broadcasted_iota(I32, (SUB, SUB * DK), 0)
    c = lax.broadcasted_iota(I32, (SUB, SUB * DK), 1) >> (DK.bit_length() - 1)
    return (r == c).astype(BF16)


def _hgrn_block(q, fp, iv, g, lb, hn, st_ref, sel_ref, valid):
    f = lb + (1.0 - lb) * jax.nn.sigmoid(fp)
    logf = jnp.log(f)
    kk = 1.0 - f
    if valid < SUB:
        live = lax.broadcasted_iota(I32, (SUB, D_A), 0) < valid
        logf = jnp.where(live, logf, 0.0)
        kk = jnp.where(live, kk, 0.0)
    qa = _silu(q)

    r = lax.broadcasted_iota(I32, (SUB, SUB), 0)
    c = lax.broadcasted_iota(I32, (SUB, SUB), 1)
    tri = (r >= c).astype(BF16)
    hi = logf.astype(BF16)
    lo = (logf - hi.astype(F32)).astype(BF16)
    b = (jnp.dot(tri, hi, preferred_element_type=F32)
         + jnp.dot(tri, lo, preferred_element_type=F32))
    bend = b[SUB - 1:SUB, :]
    qt = (qa * jnp.exp(b)).astype(BF16)
    kh = (kk * jnp.exp(bend - b)).astype(BF16)
    dec = jnp.exp(bend)
    ivb = iv.astype(BF16)

    rows = []
    for h in range(H_A):
        sl = slice(DK * h, DK * (h + 1))
        bh, qh, kkh = b[:, sl], qa[:, sl], kk[:, sl]
        ps = []
        for s in range(SUB):
            e = jnp.exp(jnp.minimum(bh - bh[s:s + 1, :], 0.0))
            ps.append((e * (qh * kkh[s:s + 1, :])).astype(BF16))
        rows.append(jnp.concatenate(ps, axis=1))
    pcat = jnp.concatenate(rows, axis=0)
    a = lax.dot_general(pcat, sel_ref[...], (((1,), (1,)), ((), ())),
                        preferred_element_type=F32)
    tr = lax.broadcasted_iota(I32, (H_A * SUB, SUB), 0) & (SUB - 1)
    tc = lax.broadcasted_iota(I32, (H_A * SUB, SUB), 1)
    a = jnp.where(tr >= tc, a, 0.0).astype(BF16)

    outs = []
    for h in range(H_A):
        sl = slice(DK * h, DK * (h + 1))
        st = st_ref[h]
        o = lax.dot_general(qt[:, sl], st.astype(BF16), (((1,), (1,)), ((), ())),
                            preferred_element_type=F32)
        o = o + jnp.dot(a[SUB * h:SUB * (h + 1), :], ivb[:, sl], preferred_element_type=F32)
        ut = lax.dot_general(ivb[:, sl], kh[:, sl], (((0,), (0,)), ((), ())),
                             preferred_element_type=F32)
        st_ref[h] = st * dec[:, sl] + ut
        o = o * lax.rsqrt(jnp.mean(o * o, axis=-1, keepdims=True) + EPS) * hn[:, sl]
        outs.append(o)
    return jnp.concatenate(outs, axis=1) * _silu(g)


def _hgrn_prompt_kernel(q_ref, f_ref, i_ref, g_ref, lb_ref, hn_ref, o_ref, s_ref, st_ref, sel_ref,
                        *, tt):
    t = pl.program_id(1)

    @pl.when(t == 0)
    def _():
        st_ref[...] = jnp.zeros_like(st_ref)
        sel_ref[...] = _block_select()

    lb = lb_ref[...]
    hn = hn_ref[...]

    def body(j, carry):
        rs = pl.ds(pl.multiple_of(j * SUB, SUB), SUB)
        o_ref[rs, :] = _hgrn_block(q_ref[rs, :], f_ref[rs, :], i_ref[rs, :], g_ref[rs, :],
                                   lb, hn, st_ref, sel_ref, SUB)
        return carry

    lax.fori_loop(0, tt // SUB, body, 0)

    @pl.when(t == pl.num_programs(1) - 1)
    def _():
        for h in range(H_A):
            s_ref[h] = st_ref[h].T


def _hgrn_prompt(u, batch, seq, lb, hn):
    tt = 256
    nt = seq // tt

    def col(cb):
        return pl.BlockSpec((tt, D_A), lambda b, t: (b * nt + t, cb))

    return pl.pallas_call(
        functools.partial(_hgrn_prompt_kernel, tt=tt),
        grid=(batch, nt),
        in_specs=[col(0), col(1), col(2), col(3),
                  pl.BlockSpec((1, D_A), lambda b, t: (0, 0)),
                  pl.BlockSpec((1, D_A), lambda b, t: (0, 0))],
        out_specs=[pl.BlockSpec((tt, D_A), lambda b, t: (b * nt + t, 0)),
                   pl.BlockSpec((None, H_A, DK, DV), lambda b, t: (b, 0, 0, 0))],
        out_shape=[jax.ShapeDtypeStruct((batch * seq, D_A), F32),
                   jax.ShapeDtypeStruct((batch, H_A, DK, DV), F32)],
        scratch_shapes=[pltpu.VMEM((H_A, DV, DK), F32), pltpu.VMEM((SUB, SUB * DK), BF16)],
        compiler_params=_cparams(("arbitrary", "arbitrary")),
        name="hgrn_prompt",
    )(u, u, u, u, lb, hn)


def _hgrn_sample_kernel(q_ref, f_ref, i_ref, g_ref, lb_ref, hn_ref, s0_ref, o_ref, s_ref,
                        st_ref, sel_ref, pad_ref, *, seq):
    for h in range(H_A):
        st_ref[h] = s0_ref[h].T
    sel_ref[...] = _block_select()
    pad_ref[...] = jnp.zeros_like(pad_ref)
    for n, ref in enumerate((q_ref, f_ref, i_ref, g_ref)):
        pad_ref[n, 0:seq, :] = ref[...]
    o = _hgrn_block(pad_ref[0], pad_ref[1], pad_ref[2], pad_ref[3], lb_ref[...], hn_ref[...],
                    st_ref, sel_ref, seq)
    o_ref[...] = o[0:seq, :]
    for h in range(H_A):
        s_ref[h] = st_ref[h].T


def _hgrn_sample(u3, s0, lb, hn):
    batch, seq, _ = u3.shape

    def col(cb):
        return pl.BlockSpec((None, seq, D_A), lambda b: (b, 0, cb))

    return pl.pallas_call(
        functools.partial(_hgrn_sample_kernel, seq=seq),
        grid=(batch,),
        in_specs=[col(0), col(1), col(2), col(3),
                  pl.BlockSpec((1, D_A), lambda b: (0, 0)),
                  pl.BlockSpec((1, D_A), lambda b: (0, 0)),
                  pl.BlockSpec((None, H_A, DK, DV), lambda b: (b, 0, 0, 0))],
        out_specs=[pl.BlockSpec((None, seq, D_A), lambda b: (b, 0, 0)),
                   pl.BlockSpec((None, H_A, DK, DV), lambda b: (b, 0, 0, 0))],
        out_shape=[jax.ShapeDtypeStruct((batch, seq, D_A), F32),
                   jax.ShapeDtypeStruct((batch, H_A, DK, DV), F32)],
        scratch_shapes=[pltpu.VMEM((H_A, DV, DK), F32), pltpu.VMEM((SUB, SUB * DK), BF16),
                        pltpu.VMEM((4, SUB, D_A), F32)],
        compiler_params=_cparams(("arbitrary",)),
        name="hgrn_sample",
    )(u3, u3, u3, u3, lb, hn, s0)


HGRN_SEQS = 8
HALF = SUB // 2


def _hgrn_step(q, fp, iv, g, lb, hn, st_ref, sel_ref, cm_ref, b_ref, k_ref, valid):
    nrow = HGRN_SEQS * SUB
    f = lb + (1.0 - lb) * jax.nn.sigmoid(fp)
    lf = jnp.log2(f)
    kk = 1.0 - f
    if valid < SUB:
        live = (lax.broadcasted_iota(I32, (nrow, D_A), 0) & (SUB - 1)) < valid
        lf = jnp.where(live, lf, 0.0)
        kk = jnp.where(live, kk, 0.0)
    qa = _silu(q)

    hi = lf.astype(BF16)
    lo = (lf - hi.astype(F32)).astype(BF16)
    cum, tot = cm_ref[0], cm_ref[1]
    b = jnp.dot(cum, hi, preferred_element_type=F32) + jnp.dot(cum, lo, preferred_element_type=F32)
    be = jnp.dot(tot, hi, preferred_element_type=F32) + jnp.dot(tot, lo, preferred_element_type=F32)
    qt = (qa * jnp.exp2(b)).astype(BF16)
    kh = (kk * jnp.exp2(be - b)).astype(BF16)
    dec = jnp.exp2(be)
    ivb = iv.astype(BF16)
    for h in range(H_A):
        b_ref[h] = b[:, DK * h:DK * (h + 1)]
        k_ref[h] = kk[:, DK * h:DK * (h + 1)]

    zero_half = jnp.zeros((HALF, D_A), F32)
    blocks = []
    for n in range(HGRN_SEQS):
        per_s = []
        for s in range(SUB):
            row = n * SUB + s
            b_s = jnp.concatenate([b_ref[h, pl.ds(row, HALF, stride=0), :] for h in range(H_A)], axis=1)
            k_s = jnp.concatenate([k_ref[h, pl.ds(row, HALF, stride=0), :] for h in range(H_A)], axis=1)
            halves = []
            for hf in range(2):
                if hf == 0 and s >= HALF:
                    halves.append(zero_half)
                    continue
                r0 = n * SUB + hf * HALF
                e = jnp.exp2(jnp.minimum(b[r0:r0 + HALF] - b_s, 0.0))
                halves.append(e * (qa[r0:r0 + HALF] * k_s))
            per_s.append(jnp.concatenate(halves, axis=0).astype(BF16))
        for h in range(H_A):
            blocks.append(jnp.concatenate([p[:, DK * h:DK * (h + 1)] for p in per_s], axis=1))
    pcat = jnp.concatenate(blocks, axis=0)
    a = lax.dot_general(pcat, sel_ref[...], (((1,), (1,)), ((), ())),
                        preferred_element_type=F32)
    tr = lax.broadcasted_iota(I32, a.shape, 0) & (SUB - 1)
    tc = lax.broadcasted_iota(I32, a.shape, 1)
    a = jnp.where(tr >= tc, a, 0.0).astype(BF16)

    outs = []
    for n in range(HGRN_SEQS):
        rows = slice(n * SUB, (n + 1) * SUB)
        heads = []
        for h in range(H_A):
            sl = slice(DK * h, DK * (h + 1))
            st = st_ref[n, h]
            o = lax.dot_general(qt[rows, sl], st.astype(BF16), (((1,), (1,)), ((), ())),
                                preferred_element_type=F32)
            ar = (n * H_A + h) * SUB
            o = o + jnp.dot(a[ar:ar + SUB, :], ivb[rows, sl], preferred_element_type=F32)
            ut = lax.dot_general(ivb[rows, sl], kh[rows, sl], (((0,), (0,)), ((), ())),
                                 preferred_element_type=F32)
            dtile = dec[n * SUB:n * SUB + HALF, sl]
            st_ref[n, h] = (st.reshape(DV // HALF, HALF, DK) * dtile[None]).reshape(DV, DK) + ut
            o = o * lax.rsqrt(jnp.mean(o * o, axis=-1, keepdims=True) + EPS) * hn[:, sl]
            heads.append(o)
        outs.append(jnp.concatenate(heads, axis=1))
    return jnp.concatenate(outs, axis=0) * _silu(g)


def _hgrn_kernel(*refs, rows, has_s0):
    if has_s0:
        (q_ref, f_ref, i_ref, g_ref, lb_ref, hn_ref, s0_ref, o_ref, s_ref,
         st_ref, sel_ref, cm_ref, b_ref, k_ref, pad_ref) = refs
    else:
        (q_ref, f_ref, i_ref, g_ref, lb_ref, hn_ref, o_ref, s_ref,
         st_ref, sel_ref, cm_ref, b_ref, k_ref) = refs
    t = pl.program_id(0)
    nrow = HGRN_SEQS * SUB

    @pl.when(t == 0)
    def _():
        sel_ref[...] = _block_select()
        r = lax.broadcasted_iota(I32, (nrow, nrow), 0)
        c = lax.broadcasted_iota(I32, (nrow, nrow), 1)
        shift = SUB.bit_length() - 1
        same = (r >> shift) == (c >> shift)
        cm_ref[0] = (same & (r >= c)).astype(BF16)
        cm_ref[1] = same.astype(BF16)

    lb = lb_ref[...]
    hn = hn_ref[...]
    if has_s0:
        for n in range(HGRN_SEQS):
            for h in range(H_A):
                st_ref[n, h] = s0_ref[n, h].T
        pad_ref[...] = jnp.zeros_like(pad_ref)
        for j, ref in enumerate((q_ref, f_ref, i_ref, g_ref)):
            pad_ref[j, :, 0:rows, :] = ref[...]
        o = _hgrn_step(*(pad_ref[j].reshape(nrow, D_A) for j in range(4)), lb, hn,
                       st_ref, sel_ref, cm_ref, b_ref, k_ref, rows)
        o_ref[...] = o.reshape(HGRN_SEQS, SUB, D_A)[:, 0:rows, :]
    else:
        @pl.when(t == 0)
        def _():
            st_ref[...] = jnp.zeros_like(st_ref)

        def body(j, carry):
            rs = pl.ds(pl.multiple_of(j * SUB, SUB), SUB)
            o = _hgrn_step(*(ref[:, rs, :].reshape(nrow, D_A) for ref in (q_ref, f_ref, i_ref, g_ref)),
                           lb, hn, st_ref, sel_ref, cm_ref, b_ref, k_ref, SUB)
            o_ref[:, rs, :] = o.reshape(HGRN_SEQS, SUB, D_A)
            return carry

        lax.fori_loop(0, rows // SUB, body, 0)

    def emit_state():
        for n in range(HGRN_SEQS):
            for h in range(H_A):
                s_ref[n, h] = st_ref[n, h].T

    if has_s0:
        emit_state()
    else:
        pl.when(t == pl.num_programs(0) - 1)(emit_state)


def _hgrn(u3, s0, lb, hn):
    batch, seq, _ = u3.shape
    has_s0 = s0 is not None
    if has_s0:
        rows, grid = seq, (batch // HGRN_SEQS,)
        blk = lambda cb: pl.BlockSpec((HGRN_SEQS, rows, D_A), lambda i: (i, 0, cb))
        sblk = pl.BlockSpec((HGRN_SEQS, H_A, DK, DV), lambda i: (i, 0, 0, 0))
    else:
        rows, grid = 128, (seq // 128,)
        blk = lambda cb: pl.BlockSpec((HGRN_SEQS, rows, D_A), lambda i: (0, i, cb))
        sblk = pl.BlockSpec((HGRN_SEQS, H_A, DK, DV), lambda i: (0, 0, 0, 0))
    vec = pl.BlockSpec((1, D_A), lambda i: (0, 0))
    nrow = HGRN_SEQS * SUB
    scratch = [pltpu.VMEM((HGRN_SEQS, H_A, DV, DK), F32), pltpu.VMEM((SUB, SUB * DK), BF16),
               pltpu.VMEM((2, nrow, nrow), BF16), pltpu.VMEM((H_A, nrow, DK), F32),
               pltpu.VMEM((H_A, nrow, DK), F32)]
    if has_s0:
        scratch.append(pltpu.VMEM((4, HGRN_SEQS, SUB, D_A), F32))
    args = (u3, u3, u3, u3, lb, hn) + ((s0,) if has_s0 else ())
    return pl.pallas_call(
        functools.partial(_hgrn_kernel, rows=rows, has_s0=has_s0),
        grid=grid,
        in_specs=[blk(0), blk(1), blk(2), blk(3), vec, vec] + ([sblk] if has_s0 else []),
        out_specs=[blk(0), sblk],
        out_shape=[jax.ShapeDtypeStruct((batch, seq, D_A), F32),
                   jax.ShapeDtypeStruct((batch, H_A, DK, DV), F32)],
        scratch_shapes=scratch,
        compiler_params=_cparams(("arbitrary",)),
        name="hgrn_state" if has_s0 else "hgrn_scan",
    )(*args)


def _pool_groups(full_ref, base, tm, cnt_fn, wp_ref, ps_ref, step=1):
    outs = []
    for gi, w in enumerate(POOL_WINDOWS):
        ch = slice(POOL_GROUP * gi, POOL_GROUP * (gi + 1))
        v = full_ref[base:base + tm, ch]
        s = v
        for j in range(1, w):
            s = s + full_ref[base - j * step:base - j * step + tm, ch]
        d = s / cnt_fn(w) - v
        outs.append(jnp.dot(d.astype(BF16), wp_ref[gi], preferred_element_type=F32))
    return jnp.concatenate(outs, axis=1) * ps_ref[...]


def _pool_prompt_kernel(v_ref, prev_ref, wp_ref, ps_ref, o_ref, hist_ref, full_ref, *, tm):
    t = pl.program_id(1)
    prev = prev_ref[...]
    full_ref[0:16, :] = jnp.where(t == 0, 0.0, prev)
    full_ref[16:16 + tm, :] = v_ref[...]
    pos = (lax.broadcasted_iota(I32, (tm, 1), 0) + t * tm + 1).astype(F32)
    o_ref[...] = _pool_groups(full_ref, 16, tm, lambda w: jnp.minimum(pos, float(w)),
                              wp_ref, ps_ref)

    @pl.when(t == pl.num_programs(1) - 1)
    def _():
        hist_ref[...] = full_ref[16 + tm - POOL_HIST:16 + tm, :]


def _pool_prompt(u, batch, seq, w_pool_bf, pool_scale):
    tm = 256
    nt = seq // tm
    vcol = (4 * D_A) // D_B
    return pl.pallas_call(
        functools.partial(_pool_prompt_kernel, tm=tm),
        grid=(batch, nt),
        in_specs=[pl.BlockSpec((tm, D_B), lambda b, t: (b * nt + t, vcol)),
                  pl.BlockSpec((16, D_B),
                               lambda b, t: (jnp.maximum((b * nt + t) * (tm // 16) - 1, 0), vcol)),
                  pl.BlockSpec((len(POOL_WINDOWS), POOL_GROUP, POOL_GROUP), lambda b, t: (0, 0, 0)),
                  pl.BlockSpec((1, D_B), lambda b, t: (0, 0))],
        out_specs=[pl.BlockSpec((tm, D_B), lambda b, t: (b * nt + t, 0)),
                   pl.BlockSpec((None, POOL_HIST, D_B), lambda b, t: (b, 0, 0))],
        out_shape=[jax.ShapeDtypeStruct((batch * seq, D_B), F32),
                   jax.ShapeDtypeStruct((batch, POOL_HIST, D_B), F32)],
        scratch_shapes=[pltpu.VMEM((16 + tm, D_B), F32)],
        compiler_params=_cparams(("arbitrary", "arbitrary")),
        name="pool_prompt",
    )(u, u, w_pool_bf, pool_scale)


def _pool_sample_kernel(v_ref, hist_ref, wp_ref, ps_ref, o_ref, nh_ref, full_ref, *,
                        seq, batch, start):
    full_ref[0:batch, :] = jnp.zeros((batch, D_B), F32)
    full_ref[batch:16 * batch, :] = hist_ref[...].reshape(POOL_HIST * batch, D_B)
    full_ref[16 * batch:(16 + seq) * batch, :] = v_ref[...].reshape(seq * batch, D_B)

    def cnt(w):
        rows = [jnp.full((batch, 1), float(min(start + t + 1, w)), F32) for t in range(seq)]
        return jnp.concatenate(rows, axis=0)

    o = _pool_groups(full_ref, 16 * batch, seq * batch, cnt, wp_ref, ps_ref, step=batch)
    o_ref[...] = o.reshape(seq, batch, D_B)
    nh_ref[...] = full_ref[(16 + seq - POOL_HIST) * batch:(16 + seq) * batch, :].reshape(
        POOL_HIST, batch, D_B)


def _pool_sample(u_t, hist_t, start, w_pool_bf, pool_scale):
    seq, batch, _ = u_t.shape
    vcol = (4 * D_A) // D_B
    return pl.pallas_call(
        functools.partial(_pool_sample_kernel, seq=seq, batch=batch, start=start),
        grid=(1,),
        in_specs=[pl.BlockSpec((seq, batch, D_B), lambda i: (0, 0, vcol)),
                  pl.BlockSpec((POOL_HIST, batch, D_B), lambda i: (0, 0, 0)),
                  pl.BlockSpec((len(POOL_WINDOWS), POOL_GROUP, POOL_GROUP), lambda i: (0, 0, 0)),
                  pl.BlockSpec((1, D_B), lambda i: (0, 0))],
        out_specs=[pl.BlockSpec((seq, batch, D_B), lambda i: (0, 0, 0)),
                   pl.BlockSpec((POOL_HIST, batch, D_B), lambda i: (0, 0, 0))],
        out_shape=[jax.ShapeDtypeStruct((seq, batch, D_B), F32),
                   jax.ShapeDtypeStruct((POOL_HIST, batch, D_B), F32)],
        scratch_shapes=[pltpu.VMEM(((16 + seq) * batch, D_B), F32)],
        compiler_params=_cparams(("arbitrary",)),
        name="pool_sample",
    )(u_t, hist_t, w_pool_bf, pool_scale)


LANES = 128
SLAB = D_MODEL // LANES


def _store_slabs(ref, value):
    n = value.shape[0]
    for j in range(SLAB):
        ref[pl.ds(j, n, stride=SLAB), :] = value[:, LANES * j:LANES * (j + 1)]


def _load_slabs(ref, n):
    return [ref[pl.ds(j, n, stride=SLAB), :] for j in range(SLAB)]


def _slab_rows(row):
    return pl.ds(pl.multiple_of(row * SLAB, SLAB), SLAB)


def _split_dot(a, w_hi, w_lo):
    a_hi = a.astype(BF16)
    a_lo = (a - a_hi.astype(F32)).astype(BF16)
    return (jnp.dot(a_hi, w_hi, preferred_element_type=F32)
            + jnp.dot(a_hi, w_lo, preferred_element_type=F32)
            + jnp.dot(a_lo, w_hi, preferred_element_type=F32))


def _outproj_kernel(x_ref, oa_ref, ob_ref, g1_ref, sh2_ref, sc2_ref, n2_ref, woa_ref, wob_ref,
                    wrh_ref, wrl_ref, br_ref, x1_ref, h2_ref, idx_ref, prob_ref):
    mix = (jnp.dot(oa_ref[...].astype(BF16), woa_ref[...], preferred_element_type=F32)
           + jnp.dot(ob_ref[...].astype(BF16), wob_ref[...], preferred_element_type=F32))
    x1 = x_ref[...] + g1_ref[...] * mix
    x1_ref[...] = x1
    h2 = x1 * lax.rsqrt(jnp.mean(x1 * x1, axis=-1, keepdims=True) + EPS) * n2_ref[...]
    h2 = h2 * (1.0 + sc2_ref[...]) + sh2_ref[...]
    _store_slabs(h2_ref, h2)

    logits = _split_dot(h2, wrh_ref[...], wrl_ref[...]) + br_ref[...]
    tm = logits.shape[0]
    lane = lax.broadcasted_iota(I32, (tm, N_EXPERTS), 1)
    kcol = lax.broadcasted_iota(I32, (tm, TOP_K), 1)
    vals = jnp.zeros((tm, TOP_K), F32)
    idxs = jnp.zeros((tm, TOP_K), I32)
    for k in range(TOP_K):
        m = jnp.max(logits, axis=-1, keepdims=True)
        am = jnp.min(jnp.where(logits == m, lane, N_EXPERTS), axis=-1, keepdims=True)
        vals = jnp.where(kcol == k, m, vals)
        idxs = jnp.where(kcol == k, am, idxs)
        logits = jnp.where(lane == am, -jnp.inf, logits)
    e = jnp.exp(vals - vals[:, 0:1])
    prob_ref[...] = e / jnp.sum(e, axis=-1, keepdims=True)
    idx_ref[...] = idxs


def _outproj(x, o_a, o_b, mod, per_token, seq, norm2, w_out_bf, wr_hi, wr_lo, b_router):
    n = x.shape[0]
    tm = min(256, n)
    tps = max(seq // tm, 1)
    row = lambda width: pl.BlockSpec((tm, width), lambda i: (i, 0))
    const = lambda shape: pl.BlockSpec(shape, lambda i: (0,) * len(shape))
    return pl.pallas_call(
        _outproj_kernel,
        grid=(n // tm,),
        in_specs=[row(D_MODEL), row(D_A), row(D_B),
                  _mod_spec(per_token, tm, tps, 2),
                  _mod_spec(per_token, tm, tps, 3),
                  _mod_spec(per_token, tm, tps, 4),
                  const((1, D_MODEL)),
                  pl.BlockSpec((D_A, D_MODEL), lambda i: (0, 0)),
                  pl.BlockSpec((D_B, D_MODEL), lambda i: (1, 0)),
                  const((D_MODEL, N_EXPERTS)), const((D_MODEL, N_EXPERTS)), const((1, N_EXPERTS))],
        out_specs=[row(D_MODEL), pl.BlockSpec((tm * SLAB, LANES), lambda i: (i, 0)),
                   row(TOP_K), row(TOP_K)],
        out_shape=[jax.ShapeDtypeStruct((n, D_MODEL), F32),
                   jax.ShapeDtypeStruct((n * SLAB, LANES), F32),
                   jax.ShapeDtypeStruct((n, TOP_K), I32),
                   jax.ShapeDtypeStruct((n, TOP_K), F32)],
        compiler_params=_cparams(("arbitrary",)),
        name="outproj",
    )(x, o_a, o_b, mod, mod, mod, norm2.reshape(1, D_MODEL), w_out_bf, w_out_bf,
      wr_hi, wr_lo, b_router.reshape(1, N_EXPERTS))


def _rank_kernel(idx_ref, dest_ref, cnt_ref, carry_ref, *, te):
    phase = pl.program_id(0)
    i = pl.program_id(1)
    idx = idx_ref[...]
    lane = lax.broadcasted_iota(I32, (te, N_EXPERTS), 1)
    onehots = [(idx[:, k:k + 1] == lane) for k in range(TOP_K)]
    member = jnp.zeros((te, N_EXPERTS), F32)
    for oh in onehots:
        member = member + oh.astype(F32)
    colsum = jnp.sum(member, axis=0, keepdims=True)

    @pl.when((phase == 0) & (i == 0))
    def _():
        carry_ref[...] = jnp.zeros_like(carry_ref)

    @pl.when(phase == 0)
    def _():
        carry_ref[...] += colsum

    @pl.when((phase == 1) & (i == 0))
    def _():
        cnt = carry_ref[...]
        cnt_ref[...] = cnt.astype(I32)
        tiles = jnp.floor((cnt + (ROW_TILE - 1)) * (1.0 / ROW_TILE))
        r = lax.broadcasted_iota(I32, (N_EXPERTS, N_EXPERTS), 0)
        c = lax.broadcasted_iota(I32, (N_EXPERTS, N_EXPERTS), 1)
        before = (r < c).astype(BF16)
        carry_ref[...] = ROW_TILE * jnp.dot(tiles.astype(BF16), before, preferred_element_type=F32)

    @pl.when(phase == 1)
    def _():
        r = lax.broadcasted_iota(I32, (te, te), 0)
        c = lax.broadcasted_iota(I32, (te, te), 1)
        earlier = (r > c).astype(BF16)
        base = carry_ref[...] + jnp.dot(earlier, member.astype(BF16), preferred_element_type=F32)
        kcol = lax.broadcasted_iota(I32, (te, TOP_K), 1)
        dest = jnp.zeros((te, TOP_K), F32)
        for k, oh in enumerate(onehots):
            dk = jnp.sum(jnp.where(oh, base, 0.0), axis=-1, keepdims=True)
            dest = jnp.where(kcol == k, dk, dest)
        dest_ref[...] = dest.astype(I32)
        carry_ref[...] += colsum


def _rank(top_idx):
    n = top_idx.shape[0]
    te = 512
    return pl.pallas_call(
        functools.partial(_rank_kernel, te=te),
        grid=(2, n // te),
        in_specs=[pl.BlockSpec((te, TOP_K), lambda p, i: (i, 0))],
        out_specs=[pl.BlockSpec((te, TOP_K), lambda p, i: (p * i, 0)),
                   pl.BlockSpec((1, N_EXPERTS), lambda p, i: (0, 0))],
        out_shape=[jax.ShapeDtypeStruct((n, TOP_K), I32),
                   jax.ShapeDtypeStruct((1, N_EXPERTS), I32)],
        scratch_shapes=[pltpu.VMEM((1, N_EXPERTS), F32)],
        compiler_params=_cparams(("arbitrary", "arbitrary")),
        name="rank",
    )(top_idx)


def _dispatch_kernel(last_ref, ntile_ref, used_ref, dest_ref, h_ref, xs_ref, zero_ref, sem, zsem,
                     *, n_tiles):
    i = pl.program_id(0)

    @pl.when(i == 0)
    def _():
        zero_ref[...] = jnp.zeros_like(zero_ref)

        def clear(tile):
            rows = pl.ds(pl.multiple_of(tile * (ROW_TILE * SLAB), ROW_TILE * SLAB), ROW_TILE * SLAB)
            return pltpu.make_async_copy(zero_ref, xs_ref.at[rows], zsem)

        def over_tail(fn):
            def body(t, carry):
                fn(clear(t))
                return carry
            lax.fori_loop(used_ref[0], n_tiles, body, 0)

        for fn in (lambda c: c.start(), lambda c: c.wait()):
            for e in range(N_EXPERTS):
                @pl.when(ntile_ref[e] > 0)
                def _():
                    fn(clear(last_ref[e]))
            over_tail(fn)

    def body(r, carry):
        for k in range(TOP_K):
            d = dest_ref[r * TOP_K + k]
            pltpu.make_async_copy(h_ref.at[_slab_rows(r)], xs_ref.at[_slab_rows(d)], sem).start()
        return carry

    lax.fori_loop(0, TOK_TILE, body, 0)
    for k in range(TOP_K):
        pltpu.make_async_copy(h_ref, xs_ref.at[pl.ds(0, TOK_TILE * SLAB)], sem).wait()


def _dispatch(h2, dest_flat, last_tile, ntiles, used, n_tiles):
    n = h2.shape[0] // SLAB
    grid_spec = pltpu.PrefetchScalarGridSpec(
        num_scalar_prefetch=3,
        grid=(n // TOK_TILE,),
        in_specs=[pl.BlockSpec((TOK_TILE * TOP_K,), lambda i, *_: (i,), memory_space=pltpu.SMEM),
                  pl.BlockSpec((TOK_TILE * SLAB, LANES), lambda i, *_: (i, 0))],
        out_specs=pl.BlockSpec(memory_space=pl.ANY),
        scratch_shapes=[pltpu.VMEM((ROW_TILE * SLAB, LANES), F32),
                        pltpu.SemaphoreType.DMA(()), pltpu.SemaphoreType.DMA(())],
    )
    return pl.pallas_call(
        functools.partial(_dispatch_kernel, n_tiles=n_tiles),
        grid_spec=grid_spec,
        out_shape=jax.ShapeDtypeStruct((n_tiles * ROW_TILE * SLAB, LANES), F32),
        compiler_params=_cparams(("arbitrary",)),
        name="dispatch",
    )(last_tile, ntiles, used, dest_flat, h2)


def _experts_kernel(te_ref, used_ref, xs_ref, wgu_ref, bgu_ref, wd_ref, bd_ref, ys_ref,
                    wgu_bf, wd_bf):
    i = pl.program_id(0)
    prev = te_ref[jnp.maximum(i - 1, 0)]
    fresh = (i == 0) | (te_ref[i] != prev)

    @pl.when(fresh)
    def _():
        wgu_bf[...] = wgu_ref[...].astype(BF16)
        wd_bf[...] = wd_ref[...].astype(BF16)

    @pl.when(i < used_ref[0])
    def _():
        x = jnp.concatenate([p.astype(BF16) for p in _load_slabs(xs_ref, ROW_TILE)], axis=1)
        acc = jnp.zeros((ROW_TILE, D_MODEL), F32) + bd_ref[...]
        nch = 4
        cw = D_FF // nch
        for c in range(nch):
            gate = jnp.dot(x, wgu_bf[:, c * cw:(c + 1) * cw], preferred_element_type=F32)
            gate = jnp.minimum(gate + bgu_ref[:, c * cw:(c + 1) * cw], SWIGLU_LIMIT)
            up = jnp.dot(x, wgu_bf[:, D_FF + c * cw:D_FF + (c + 1) * cw], preferred_element_type=F32)
            up = jnp.clip(up + bgu_ref[:, D_FF + c * cw:D_FF + (c + 1) * cw], -SWIGLU_LIMIT, SWIGLU_LIMIT)
            act = gate * jax.nn.sigmoid(SWIGLU_ALPHA * gate) * (up + 1.0)
            acc = acc + jnp.dot(act.astype(BF16), wd_bf[c * cw:(c + 1) * cw, :],
                                preferred_element_type=F32)
        _store_slabs(ys_ref, acc)

    @pl.when(i >= used_ref[0])
    def _():
        ys_ref[...] = jnp.zeros_like(ys_ref)


def _experts(xs, tile_expert, used, w_gu, b_gu, w_down, b_down):
    rows = xs.shape[0] // SLAB
    nt = rows // ROW_TILE

    def tile(i, te, used):
        return (jnp.minimum(i, used[0] - 1), 0)

    grid_spec = pltpu.PrefetchScalarGridSpec(
        num_scalar_prefetch=2,
        grid=(nt,),
        in_specs=[pl.BlockSpec((ROW_TILE * SLAB, LANES), tile),
                  pl.BlockSpec((None, D_MODEL, 2 * D_FF), lambda i, te, used: (te[i], 0, 0)),
                  pl.BlockSpec((None, 1, 2 * D_FF), lambda i, te, used: (te[i], 0, 0)),
                  pl.BlockSpec((None, D_FF, D_MODEL), lambda i, te, used: (te[i], 0, 0)),
                  pl.BlockSpec((None, 1, D_MODEL), lambda i, te, used: (te[i], 0, 0))],
        out_specs=pl.BlockSpec((ROW_TILE * SLAB, LANES), lambda i, te, used: (i, 0)),
        scratch_shapes=[pltpu.VMEM((D_MODEL, 2 * D_FF), BF16), pltpu.VMEM((D_FF, D_MODEL), BF16)],
    )
    return pl.pallas_call(
        _experts_kernel,
        grid_spec=grid_spec,
        out_shape=jax.ShapeDtypeStruct((rows * SLAB, LANES), F32),
        compiler_params=_cparams(("arbitrary",)),
        name="experts",
    )(tile_expert, used, xs, w_gu, b_gu.reshape(N_EXPERTS, 1, 2 * D_FF), w_down,
      b_down.reshape(N_EXPERTS, 1, D_MODEL))


def _combine_kernel(dest_ref, ys_ref, x1_ref, p_ref, g2_ref, fn_ref, y_ref, buf_ref, sem, *, last):
    def body(r, carry):
        for k in range(TOP_K):
            d = dest_ref[r * TOP_K + k]
            pltpu.make_async_copy(ys_ref.at[_slab_rows(d)], buf_ref.at[k, _slab_rows(r)], sem).start()
        return carry

    lax.fori_loop(0, TOK_TILE, body, 0)
    for k in range(TOP_K):
        pltpu.make_async_copy(ys_ref.at[pl.ds(0, TOK_TILE * SLAB)], buf_ref.at[k], sem).wait()

    p = p_ref[...]
    groups = None
    for k in range(TOP_K):
        part = [p[:, k:k + 1] * g for g in _load_slabs(buf_ref.at[k], TOK_TILE)]
        groups = part if groups is None else [a + b for a, b in zip(groups, part)]
    moe = jnp.concatenate(groups, axis=1)
    x2 = x1_ref[...] + g2_ref[...] * moe
    if last:
        x2 = x2 * lax.rsqrt(jnp.mean(x2 * x2, axis=-1, keepdims=True) + EPS) * fn_ref[...]
    y_ref[...] = x2


def _combine(ys, dest_flat, x1, probs, mod, per_token, seq, tile0, final_norm, last):
    n = x1.shape[0]
    tm = TOK_TILE
    tps = max(seq // tm, 1)
    return pl.pallas_call(
        functools.partial(_combine_kernel, last=last),
        grid=(n // tm,),
        in_specs=[pl.BlockSpec((tm * TOP_K,), lambda i: (i + tile0,), memory_space=pltpu.SMEM),
                  pl.BlockSpec(memory_space=pl.ANY),
                  pl.BlockSpec((tm, D_MODEL), lambda i: (i, 0)),
                  pl.BlockSpec((tm, TOP_K), lambda i: (i + tile0, 0)),
                  _mod_spec(per_token, tm, tps, 5),
                  pl.BlockSpec((1, D_MODEL), lambda i: (0, 0))],
        out_specs=pl.BlockSpec((tm, D_MODEL), lambda i: (i, 0)),
        out_shape=jax.ShapeDtypeStruct((n, D_MODEL), F32),
        scratch_shapes=[pltpu.VMEM((TOP_K, tm * SLAB, LANES), F32), pltpu.SemaphoreType.DMA(())],
        compiler_params=_cparams(("arbitrary",)),
        name="combine",
    )(dest_flat, ys, x1, probs, mod, final_norm.reshape(1, D_MODEL))


def kernel(x_prompt, x_sample, state_hgrn, state_pool, c_prompt, c_sample, w_ada, b_ada, norm1,
           norm2, w_in, lower_bounds, hgrn_norm, w_pool, pool_scale, w_out, w_router, b_router,
           w_gu, b_gu, w_down, b_down, final_norm):
    bp, seq_p, _ = x_prompt.shape
    bs, seq_s, _ = x_sample.shape
    np_, ns = bp * seq_p, bs * seq_s
    depth = w_ada.shape[0]
    past_len = 16384

    lbs = jnp.cumsum(jax.nn.softmax(lower_bounds.astype(F32), axis=0), axis=0)
    xp = x_prompt.reshape(np_, D_MODEL)
    xs_ = x_sample.reshape(ns, D_MODEL)
    c_all = jnp.concatenate([c_prompt, c_sample], axis=0)

    sp_l, hp_l, ss_l, hs_l = [], [], [], []
    for l in range(depth):
        w_in_bf = w_in[l].astype(BF16)
        w_out_bf = w_out[l].astype(BF16)
        w_pool_bf = w_pool[l].astype(BF16)
        wr_hi = w_router[l].astype(BF16)
        wr_lo = (w_router[l] - wr_hi.astype(F32)).astype(BF16)
        lb = lbs[l].reshape(1, D_A)
        hn = hgrn_norm[l].reshape(1, D_A)
        psc = pool_scale[l].reshape(1, D_B)

        mod = _ada(c_all, w_ada[l], b_ada[l])
        mod_p = mod[:bp].reshape(bp, 1, 6 * D_MODEL)
        mod_s = jnp.repeat(mod[bp:], seq_s, axis=0)

        up = _inproj(xp, mod_p, False, seq_p, norm1[l], w_in_bf)
        us = _inproj(xs_, mod_s, True, seq_s, norm1[l], w_in_bf)
        us3 = us.reshape(bs, seq_s, D_IN)

        oa_p, s_p = _hgrn(up.reshape(bp, seq_p, D_IN), None, lb, hn)
        oa_p = oa_p.reshape(np_, D_A)
        oa_s, s_s = _hgrn(us3, state_hgrn[l], lb, hn)
        ob_p, h_p = _pool_prompt(up, bp, seq_p, w_pool_bf, psc)
        ob_t, h_t = _pool_sample(us3.transpose(1, 0, 2), state_pool[l].transpose(1, 0, 2), past_len,
                                 w_pool_bf, psc)
        ob_s, h_s = ob_t.transpose(1, 0, 2), h_t.transpose(1, 0, 2)

        x1p, h2p, idxp, prp = _outproj(xp, oa_p, ob_p, mod_p, False, seq_p, norm2[l], w_out_bf,
                                       wr_hi, wr_lo, b_router[l])
        x1s, h2s, idxs, prs = _outproj(xs_, oa_s.reshape(ns, D_A), ob_s.reshape(ns, D_B), mod_s,
                                       True, seq_s, norm2[l], w_out_bf, wr_hi, wr_lo, b_router[l])

        h2 = jnp.concatenate([h2p, h2s], axis=0)
        top_idx = jnp.concatenate([idxp, idxs], axis=0)
        probs = jnp.concatenate([prp, prs], axis=0)
        n_tok = np_ + ns
        n_tiles = (n_tok * TOP_K) // ROW_TILE + N_EXPERTS

        dest, counts = _rank(top_idx)
        dest_flat = dest.reshape(n_tok * TOP_K)
        tiles_e = (counts.reshape(N_EXPERTS) + ROW_TILE - 1) // ROW_TILE
        ends = jnp.cumsum(tiles_e)
        used = ends[-1:].astype(I32)
        tile_expert = jnp.minimum(
            jnp.sum(jnp.arange(n_tiles, dtype=I32)[:, None] >= ends[None, :], axis=1),
            N_EXPERTS - 1).astype(I32)
        last_tile = (ends - 1).astype(I32)

        xsort = _dispatch(h2, dest_flat, last_tile, tiles_e.astype(I32), used, n_tiles)
        ysort = _experts(xsort, tile_expert, used, w_gu[l], b_gu[l], w_down[l], b_down[l])

        last = l == depth - 1
        xp = _combine(ysort, dest_flat, x1p, probs, mod_p, False, seq_p, 0, final_norm, last)
        xs_ = _combine(ysort, dest_flat, x1s, probs, mod_s, True, seq_s, np_ // TOK_TILE,
                       final_norm, last)

        sp_l.append(s_p)
        hp_l.append(h_p)
        ss_l.append(s_s)
        hs_l.append(h_s)

    return (xp.reshape(bp, seq_p, D_MODEL), xs_.reshape(bs, seq_s, D_MODEL),
            jnp.stack(sp_l), jnp.stack(hp_l), jnp.stack(ss_l), jnp.stack(hs_l))
```

```python
import functools

import jax
import jax.numpy as jnp
from jax import lax
from jax.experimental import pallas as pl
from jax.experimental.pallas import tpu as pltpu

F32, BF16, I32 = jnp.float32, jnp.bfloat16, jnp.int32

D_MODEL = 1024
D_A = 512
H_A = 4
DK = 128
DV = 128
D_B = 512
POOL_WINDOWS = (2, 4, 8, 16)
POOL_GROUP = 128
POOL_HIST = 15
D_IN = 4 * D_A + D_B
N_EXPERTS = 32
TOP_K = 4
D_FF = 1024
SWIGLU_LIMIT = 7.0
SWIGLU_ALPHA = 1.702
EPS = 1e-6

SUB = 16
ROW_TILE = 256
TOK_TILE = 256
VMEM_LIMIT = 56 * 1024 * 1024


def _cparams(sem):
    return pltpu.CompilerParams(dimension_semantics=sem, vmem_limit_bytes=VMEM_LIMIT)


def _silu(x):
    return x * jax.nn.sigmoid(x)


def _mod_spec(per_token, tm, tiles_per_seq, col):
    if per_token:
        return pl.BlockSpec((tm, D_MODEL), lambda i: (i, col))
    return pl.BlockSpec((None, 1, D_MODEL), lambda i: (i // tiles_per_seq, 0, col))


def _ada_kernel(c_ref, w_ref, b_ref, o_ref):
    c = c_ref[...]
    o_ref[...] = jnp.dot(_silu(c).astype(BF16), w_ref[...].astype(BF16),
                         preferred_element_type=F32) + b_ref[...]


def _ada(c_all, w_ada, b_ada):
    rows = c_all.shape[0]
    n = w_ada.shape[1]
    return pl.pallas_call(
        _ada_kernel,
        grid=(n // D_MODEL,),
        in_specs=[pl.BlockSpec((rows, D_MODEL), lambda j: (0, 0)),
                  pl.BlockSpec((D_MODEL, D_MODEL), lambda j: (0, j)),
                  pl.BlockSpec((1, D_MODEL), lambda j: (0, j))],
        out_specs=pl.BlockSpec((rows, D_MODEL), lambda j: (0, j)),
        out_shape=jax.ShapeDtypeStruct((rows, n), F32),
        compiler_params=_cparams(("arbitrary",)),
        name="ada",
    )(c_all, w_ada, b_ada.reshape(1, n))


def _inproj_kernel(x_ref, sh_ref, sc_ref, n1_ref, w_ref, u_ref):
    x = x_ref[...]
    h = x * lax.rsqrt(jnp.mean(x * x, axis=-1, keepdims=True) + EPS) * n1_ref[...]
    h = h * (1.0 + sc_ref[...]) + sh_ref[...]
    u_ref[...] = jnp.dot(h.astype(BF16), w_ref[...], preferred_element_type=F32)


def _inproj(x, mod, per_token, seq, norm1, w_in_bf):
    n = x.shape[0]
    tm = min(512, n)
    tps = max(seq // tm, 1)
    return pl.pallas_call(
        _inproj_kernel,
        grid=(n // tm,),
        in_specs=[pl.BlockSpec((tm, D_MODEL), lambda i: (i, 0)),
                  _mod_spec(per_token, tm, tps, 0),
                  _mod_spec(per_token, tm, tps, 1),
                  pl.BlockSpec((1, D_MODEL), lambda i: (0, 0)),
                  pl.BlockSpec((D_MODEL, D_IN), lambda i: (0, 0))],
        out_specs=pl.BlockSpec((tm, D_IN), lambda i: (i, 0)),
        out_shape=jax.ShapeDtypeStruct((n, D_IN), F32),
        compiler_params=_cparams(("arbitrary",)),
        name="inproj",
    )(x, mod, mod, norm1.reshape(1, D_MODEL), w_in_bf)


def _block_select():
    r = lax.broadcasted_iota(I32, (SUB, SUB * DK), 0)
    c = lax.broadcasted_iota(I32, (SUB, SUB * DK), 1) >> (DK.bit_length() - 1)
    return (r == c).astype(BF16)


def _hgrn_block(q, fp, iv, g, lb, hn, st_ref, sel_ref, valid):
    f = lb + (1.0 - lb) * jax.nn.sigmoid(fp)
    logf = jnp.log(f)
    kk = 1.0 - f
    if valid < SUB:
        live = lax.broadcasted_iota(I32, (SUB, D_A), 0) < valid
        logf = jnp.where(live, logf, 0.0)
        kk = jnp.where(live, kk, 0.0)
    qa = _silu(q)

    r = lax.broadcasted_iota(I32, (SUB, SUB), 0)
    c = lax.broadcasted_iota(I32, (SUB, SUB), 1)
    tri = (r >= c).astype(BF16)
    hi = logf.astype(BF16)
    lo = (logf - hi.astype(F32)).astype(BF16)
    b = (jnp.dot(tri, hi, preferred_element_type=F32)
         + jnp.dot(tri, lo, preferred_element_type=F32))
    bend = b[SUB - 1:SUB, :]
    qt = (qa * jnp.exp(b)).astype(BF16)
    kh = (kk * jnp.exp(bend - b)).astype(BF16)
    dec = jnp.exp(bend)
    ivb = iv.astype(BF16)

    rows = []
    for h in range(H_A):
        sl = slice(DK * h, DK * (h + 1))
        bh, qh, kkh = b[:, sl], qa[:, sl], kk[:, sl]
        ps = []
        for s in range(SUB):
            e = jnp.exp(jnp.minimum(bh - bh[s:s + 1, :], 0.0))
            ps.append((e * (qh * kkh[s:s + 1, :])).astype(BF16))
        rows.append(jnp.concatenate(ps, axis=1))
    pcat = jnp.concatenate(rows, axis=0)
    a = lax.dot_general(pcat, sel_ref[...], (((1,), (1,)), ((), ())),
                        preferred_element_type=F32)
    tr = lax.broadcasted_iota(I32, (H_A * SUB, SUB), 0) & (SUB - 1)
    tc = lax.broadcasted_iota(I32, (H_A * SUB, SUB), 1)
    a = jnp.where(tr >= tc, a, 0.0).astype(BF16)

    outs = []
    for h in range(H_A):
        sl = slice(DK * h, DK * (h + 1))
        st = st_ref[h]
        o = lax.dot_general(qt[:, sl], st.astype(BF16), (((1,), (1,)), ((), ())),
                            preferred_element_type=F32)
        o = o + jnp.dot(a[SUB * h:SUB * (h + 1), :], ivb[:, sl], preferred_element_type=F32)
        ut = lax.dot_general(ivb[:, sl], kh[:, sl], (((0,), (0,)), ((), ())),
                             preferred_element_type=F32)
        st_ref[h] = st * dec[:, sl] + ut
        o = o * lax.rsqrt(jnp.mean(o * o, axis=-1, keepdims=True) + EPS) * hn[:, sl]
        outs.append(o)
    return jnp.concatenate(outs, axis=1) * _silu(g)


def _hgrn_prompt_kernel(q_ref, f_ref, i_ref, g_ref, lb_ref, hn_ref, o_ref, s_ref, st_ref, sel_ref,
                        *, tt):
    t = pl.program_id(1)

    @pl.when(t == 0)
    def _():
        st_ref[...] = jnp.zeros_like(st_ref)
        sel_ref[...] = _block_select()

    lb = lb_ref[...]
    hn = hn_ref[...]

    def body(j, carry):
        rs = pl.ds(pl.multiple_of(j * SUB, SUB), SUB)
        o_ref[rs, :] = _hgrn_block(q_ref[rs, :], f_ref[rs, :], i_ref[rs, :], g_ref[rs, :],
                                   lb, hn, st_ref, sel_ref, SUB)
        return carry

    lax.fori_loop(0, tt // SUB, body, 0)

    @pl.when(t == pl.num_programs(1) - 1)
    def _():
        for h in range(H_A):
            s_ref[h] = st_ref[h].T


def _hgrn_prompt(u, batch, seq, lb, hn):
    tt = 256
    nt = seq // tt

    def col(cb):
        return pl.BlockSpec((tt, D_A), lambda b, t: (b * nt + t, cb))

    return pl.pallas_call(
        functools.partial(_hgrn_prompt_kernel, tt=tt),
        grid=(batch, nt),
        in_specs=[col(0), col(1), col(2), col(3),
                  pl.BlockSpec((1, D_A), lambda b, t: (0, 0)),
                  pl.BlockSpec((1, D_A), lambda b, t: (0, 0))],
        out_specs=[pl.BlockSpec((tt, D_A), lambda b, t: (b * nt + t, 0)),
                   pl.BlockSpec((None, H_A, DK, DV), lambda b, t: (b, 0, 0, 0))],
        out_shape=[jax.ShapeDtypeStruct((batch * seq, D_A), F32),
                   jax.ShapeDtypeStruct((batch, H_A, DK, DV), F32)],
        scratch_shapes=[pltpu.VMEM((H_A, DV, DK), F32), pltpu.VMEM((SUB, SUB * DK), BF16)],
        compiler_params=_cparams(("arbitrary", "arbitrary")),
        name="hgrn_prompt",
    )(u, u, u, u, lb, hn)


def _hgrn_sample_kernel(q_ref, f_ref, i_ref, g_ref, lb_ref, hn_ref, s0_ref, o_ref, s_ref,
                        st_ref, sel_ref, pad_ref, *, seq):
    for h in range(H_A):
        st_ref[h] = s0_ref[h].T
    sel_ref[...] = _block_select()
    pad_ref[...] = jnp.zeros_like(pad_ref)
    for n, ref in enumerate((q_ref, f_ref, i_ref, g_ref)):
        pad_ref[n, 0:seq, :] = ref[...]
    o = _hgrn_block(pad_ref[0], pad_ref[1], pad_ref[2], pad_ref[3], lb_ref[...], hn_ref[...],
                    st_ref, sel_ref, seq)
    o_ref[...] = o[0:seq, :]
    for h in range(H_A):
        s_ref[h] = st_ref[h].T


def _hgrn_sample(u3, s0, lb, hn):
    batch, seq, _ = u3.shape

    def col(cb):
        return pl.BlockSpec((None, seq, D_A), lambda b: (b, 0, cb))

    return pl.pallas_call(
        functools.partial(_hgrn_sample_kernel, seq=seq),
        grid=(batch,),
        in_specs=[col(0), col(1), col(2), col(3),
                  pl.BlockSpec((1, D_A), lambda b: (0, 0)),
                  pl.BlockSpec((1, D_A), lambda b: (0, 0)),
                  pl.BlockSpec((None, H_A, DK, DV), lambda b: (b, 0, 0, 0))],
        out_specs=[pl.BlockSpec((None, seq, D_A), lambda b: (b, 0, 0)),
                   pl.BlockSpec((None, H_A, DK, DV), lambda b: (b, 0, 0, 0))],
        out_shape=[jax.ShapeDtypeStruct((batch, seq, D_A), F32),
                   jax.ShapeDtypeStruct((batch, H_A, DK, DV), F32)],
        scratch_shapes=[pltpu.VMEM((H_A, DV, DK), F32), pltpu.VMEM((SUB, SUB * DK), BF16),
                        pltpu.VMEM((4, SUB, D_A), F32)],
        compiler_params=_cparams(("arbitrary",)),
        name="hgrn_sample",
    )(u3, u3, u3, u3, lb, hn, s0)


HGRN_SEQS = 8
HALF = SUB // 2


def _hgrn_step(q, fp, iv, g, lb, hn, st_ref, sel_ref, cm_ref, b_ref, k_ref, valid):
    nrow = HGRN_SEQS * SUB
    f = lb + (1.0 - lb) * jax.nn.sigmoid(fp)
    lf = jnp.log2(f)
    kk = 1.0 - f
    if valid < SUB:
        live = (lax.broadcasted_iota(I32, (nrow, D_A), 0) & (SUB - 1)) < valid
        lf = jnp.where(live, lf, 0.0)
        kk = jnp.where(live, kk, 0.0)
    qa = _silu(q)

    hi = lf.astype(BF16)
    lo = (lf - hi.astype(F32)).astype(BF16)
    cum, tot = cm_ref[0], cm_ref[1]
    b = jnp.dot(cum, hi, preferred_element_type=F32) + jnp.dot(cum, lo, preferred_element_type=F32)
    be = jnp.dot(tot, hi, preferred_element_type=F32) + jnp.dot(tot, lo, preferred_element_type=F32)
    qt = (qa * jnp.exp2(b)).astype(BF16)
    kh = (kk * jnp.exp2(be - b)).astype(BF16)
    dec = jnp.exp2(be)
    ivb = iv.astype(BF16)
    for h in range(H_A):
        b_ref[h] = b[:, DK * h:DK * (h + 1)]
        k_ref[h] = kk[:, DK * h:DK * (h + 1)]

    zero_half = jnp.zeros((HALF, D_A), F32)
    blocks = []
    for n in range(HGRN_SEQS):
        per_s = []
        for s in range(SUB):
            row = n * SUB + s
            b_s = jnp.concatenate([b_ref[h, pl.ds(row, HALF, stride=0), :] for h in range(H_A)], axis=1)
            k_s = jnp.concatenate([k_ref[h, pl.ds(row, HALF, stride=0), :] for h in range(H_A)], axis=1)
            halves = []
            for hf in range(2):
                if hf == 0 and s >= HALF:
                    halves.append(zero_half)
                    continue
                r0 = n * SUB + hf * HALF
                e = jnp.exp2(jnp.minimum(b[r0:r0 + HALF] - b_s, 0.0))
                halves.append(e * (qa[r0:r0 + HALF] * k_s))
            per_s.append(jnp.concatenate(halves, axis=0).astype(BF16))
        for h in range(H_A):
            blocks.append(jnp.concatenate([p[:, DK * h:DK * (h + 1)] for p in per_s], axis=1))
    pcat = jnp.concatenate(blocks, axis=0)
    a = lax.dot_general(pcat, sel_ref[...], (((1,), (1,)), ((), ())),
                        preferred_element_type=F32)
    tr = lax.broadcasted_iota(I32, a.shape, 0) & (SUB - 1)
    tc = lax.broadcasted_iota(I32, a.shape, 1)
    a = jnp.where(tr >= tc, a, 0.0).astype(BF16)

    outs = []
    for n in range(HGRN_SEQS):
        rows = slice(n * SUB, (n + 1) * SUB)
        heads = []
        for h in range(H_A):
            sl = slice(DK * h, DK * (h + 1))
            st = st_ref[n, h]
            o = lax.dot_general(qt[rows, sl], st.astype(BF16), (((1,), (1,)), ((), ())),
                                preferred_element_type=F32)
            ar = (n * H_A + h) * SUB
            o = o + jnp.dot(a[ar:ar + SUB, :], ivb[rows, sl], preferred_element_type=F32)
            ut = lax.dot_general(ivb[rows, sl], kh[rows, sl], (((0,), (0,)), ((), ())),
                                 preferred_element_type=F32)
            dtile = dec[n * SUB:n * SUB + HALF, sl]
            st_ref[n, h] = (st.reshape(DV // HALF, HALF, DK) * dtile[None]).reshape(DV, DK) + ut
            o = o * lax.rsqrt(jnp.mean(o * o, axis=-1, keepdims=True) + EPS) * hn[:, sl]
            heads.append(o)
        outs.append(jnp.concatenate(heads, axis=1))
    return jnp.concatenate(outs, axis=0) * _silu(g)


def _hgrn_kernel(*refs, rows, has_s0):
    if has_s0:
        (q_ref, f_ref, i_ref, g_ref, lb_ref, hn_ref, s0_ref, o_ref, s_ref,
         st_ref, sel_ref, cm_ref, b_ref, k_ref, pad_ref) = refs
    else:
        (q_ref, f_ref, i_ref, g_ref, lb_ref, hn_ref, o_ref, s_ref,
         st_ref, sel_ref, cm_ref, b_ref, k_ref) = refs
    t = pl.program_id(0)
    nrow = HGRN_SEQS * SUB

    @pl.when(t == 0)
    def _():
        sel_ref[...] = _block_select()
        r = lax.broadcasted_iota(I32, (nrow, nrow), 0)
        c = lax.broadcasted_iota(I32, (nrow, nrow), 1)
        shift = SUB.bit_length() - 1
        same = (r >> shift) == (c >> shift)
        cm_ref[0] = (same & (r >= c)).astype(BF16)
        cm_ref[1] = same.astype(BF16)

    lb = lb_ref[...]
    hn = hn_ref[...]
    if has_s0:
        for n in range(HGRN_SEQS):
            for h in range(H_A):
                st_ref[n, h] = s0_ref[n, h].T
        pad_ref[...] = jnp.zeros_like(pad_ref)
        for j, ref in enumerate((q_ref, f_ref, i_ref, g_ref)):
            pad_ref[j, :, 0:rows, :] = ref[...]
        o = _hgrn_step(*(pad_ref[j].reshape(nrow, D_A) for j in range(4)), lb, hn,
                       st_ref, sel_ref, cm_ref, b_ref, k_ref, rows)
        o_ref[...] = o.reshape(HGRN_SEQS, SUB, D_A)[:, 0:rows, :]
    else:
        @pl.when(t == 0)
        def _():
            st_ref[...] = jnp.zeros_like(st_ref)

        def body(j, carry):
            rs = pl.ds(pl.multiple_of(j * SUB, SUB), SUB)
            o = _hgrn_step(*(ref[:, rs, :].reshape(nrow, D_A) for ref in (q_ref, f_ref, i_ref, g_ref)),
                           lb, hn, st_ref, sel_ref, cm_ref, b_ref, k_ref, SUB)
            o_ref[:, rs, :] = o.reshape(HGRN_SEQS, SUB, D_A)
            return carry

        lax.fori_loop(0, rows // SUB, body, 0)

    def emit_state():
        for n in range(HGRN_SEQS):
            for h in range(H_A):
                s_ref[n, h] = st_ref[n, h].T

    if has_s0:
        emit_state()
    else:
        pl.when(t == pl.num_programs(0) - 1)(emit_state)


def _hgrn(u3, s0, lb, hn):
    batch, seq, _ = u3.shape
    has_s0 = s0 is not None
    if has_s0:
        rows, grid = seq, (batch // HGRN_SEQS,)
        blk = lambda cb: pl.BlockSpec((HGRN_SEQS, rows, D_A), lambda i: (i, 0, cb))
        sblk = pl.BlockSpec((HGRN_SEQS, H_A, DK, DV), lambda i: (i, 0, 0, 0))
    else:
        rows, grid = 128, (seq // 128,)
        blk = lambda cb: pl.BlockSpec((HGRN_SEQS, rows, D_A), lambda i: (0, i, cb))
        sblk = pl.BlockSpec((HGRN_SEQS, H_A, DK, DV), lambda i: (0, 0, 0, 0))
    vec = pl.BlockSpec((1, D_A), lambda i: (0, 0))
    nrow = HGRN_SEQS * SUB
    scratch = [pltpu.VMEM((HGRN_SEQS, H_A, DV, DK), F32), pltpu.VMEM((SUB, SUB * DK), BF16),
               pltpu.VMEM((2, nrow, nrow), BF16), pltpu.VMEM((H_A, nrow, DK), F32),
               pltpu.VMEM((H_A, nrow, DK), F32)]
    if has_s0:
        scratch.append(pltpu.VMEM((4, HGRN_SEQS, SUB, D_A), F32))
    args = (u3, u3, u3, u3, lb, hn) + ((s0,) if has_s0 else ())
    return pl.pallas_call(
        functools.partial(_hgrn_kernel, rows=rows, has_s0=has_s0),
        grid=grid,
        in_specs=[blk(0), blk(1), blk(2), blk(3), vec, vec] + ([sblk] if has_s0 else []),
        out_specs=[blk(0), sblk],
        out_shape=[jax.ShapeDtypeStruct((batch, seq, D_A), F32),
                   jax.ShapeDtypeStruct((batch, H_A, DK, DV), F32)],
        scratch_shapes=scratch,
        compiler_params=_cparams(("arbitrary",)),
        name="hgrn_state" if has_s0 else "hgrn_scan",
    )(*args)


def _pool_groups(full_ref, base, tm, cnt_fn, wp_ref, ps_ref, step=1):
    outs = []
    for gi, w in enumerate(POOL_WINDOWS):
        ch = slice(POOL_GROUP * gi, POOL_GROUP * (gi + 1))
        v = full_ref[base:base + tm, ch]
        s = v
        for j in range(1, w):
            s = s + full_ref[base - j * step:base - j * step + tm, ch]
        d = s / cnt_fn(w) - v
        outs.append(jnp.dot(d.astype(BF16), wp_ref[gi], preferred_element_type=F32))
    return jnp.concatenate(outs, axis=1) * ps_ref[...]


def _pool_prompt_kernel(v_ref, prev_ref, wp_ref, ps_ref, o_ref, hist_ref, full_ref, *, tm):
    t = pl.program_id(1)
    prev = prev_ref[...]
    full_ref[0:16, :] = jnp.where(t == 0, 0.0, prev)
    full_ref[16:16 + tm, :] = v_ref[...]
    pos = (lax.broadcasted_iota(I32, (tm, 1), 0) + t * tm + 1).astype(F32)
    o_ref[...] = _pool_groups(full_ref, 16, tm, lambda w: jnp.minimum(pos, float(w)),
                              wp_ref, ps_ref)

    @pl.when(t == pl.num_programs(1) - 1)
    def _():
        hist_ref[...] = full_ref[16 + tm - POOL_HIST:16 + tm, :]


def _pool_prompt(u, batch, seq, w_pool_bf, pool_scale):
    tm = 256
    nt = seq // tm
    vcol = (4 * D_A) // D_B
    return pl.pallas_call(
        functools.partial(_pool_prompt_kernel, tm=tm),
        grid=(batch, nt),
        in_specs=[pl.BlockSpec((tm, D_B), lambda b, t: (b * nt + t, vcol)),
                  pl.BlockSpec((16, D_B),
                               lambda b, t: (jnp.maximum((b * nt + t) * (tm // 16) - 1, 0), vcol)),
                  pl.BlockSpec((len(POOL_WINDOWS), POOL_GROUP, POOL_GROUP), lambda b, t: (0, 0, 0)),
                  pl.BlockSpec((1, D_B), lambda b, t: (0, 0))],
        out_specs=[pl.BlockSpec((tm, D_B), lambda b, t: (b * nt + t, 0)),
                   pl.BlockSpec((None, POOL_HIST, D_B), lambda b, t: (b, 0, 0))],
        out_shape=[jax.ShapeDtypeStruct((batch * seq, D_B), F32),
                   jax.ShapeDtypeStruct((batch, POOL_HIST, D_B), F32)],
        scratch_shapes=[pltpu.VMEM((16 + tm, D_B), F32)],
        compiler_params=_cparams(("arbitrary", "arbitrary")),
        name="pool_prompt",
    )(u, u, w_pool_bf, pool_scale)


def _pool_sample_kernel(v_ref, hist_ref, wp_ref, ps_ref, o_ref, nh_ref, full_ref, *,
                        seq, batch, start):
    full_ref[0:batch, :] = jnp.zeros((batch, D_B), F32)
    full_ref[batch:16 * batch, :] = hist_ref[...].reshape(POOL_HIST * batch, D_B)
    full_ref[16 * batch:(16 + seq) * batch, :] = v_ref[...].reshape(seq * batch, D_B)

    def cnt(w):
        rows = [jnp.full((batch, 1), float(min(start + t + 1, w)), F32) for t in range(seq)]
        return jnp.concatenate(rows, axis=0)

    o = _pool_groups(full_ref, 16 * batch, seq * batch, cnt, wp_ref, ps_ref, step=batch)
    o_ref[...] = o.reshape(seq, batch, D_B)
    nh_ref[...] = full_ref[(16 + seq - POOL_HIST) * batch:(16 + seq) * batch, :].reshape(
        POOL_HIST, batch, D_B)


def _pool_sample(u_t, hist_t, start, w_pool_bf, pool_scale):
    seq, batch, _ = u_t.shape
    vcol = (4 * D_A) // D_B
    return pl.pallas_call(
        functools.partial(_pool_sample_kernel, seq=seq, batch=batch, start=start),
        grid=(1,),
        in_specs=[pl.BlockSpec((seq, batch, D_B), lambda i: (0, 0, vcol)),
                  pl.BlockSpec((POOL_HIST, batch, D_B), lambda i: (0, 0, 0)),
                  pl.BlockSpec((len(POOL_WINDOWS), POOL_GROUP, POOL_GROUP), lambda i: (0, 0, 0)),
                  pl.BlockSpec((1, D_B), lambda i: (0, 0))],
        out_specs=[pl.BlockSpec((seq, batch, D_B), lambda i: (0, 0, 0)),
                   pl.BlockSpec((POOL_HIST, batch, D_B), lambda i: (0, 0, 0))],
        out_shape=[jax.ShapeDtypeStruct((seq, batch, D_B), F32),
                   jax.ShapeDtypeStruct((POOL_HIST, batch, D_B), F32)],
        scratch_shapes=[pltpu.VMEM(((16 + seq) * batch, D_B), F32)],
        compiler_params=_cparams(("arbitrary",)),
        name="pool_sample",
    )(u_t, hist_t, w_pool_bf, pool_scale)


LANES = 128
SLAB = D_MODEL // LANES


def _store_slabs(ref, value):
    n = value.shape[0]
    for j in range(SLAB):
        ref[pl.ds(j, n, stride=SLAB), :] = value[:, LANES * j:LANES * (j + 1)]


def _load_slabs(ref, n):
    return [ref[pl.ds(j, n, stride=SLAB), :] for j in range(SLAB)]


def _slab_rows(row):
    return pl.ds(pl.multiple_of(row * SLAB, SLAB), SLAB)


def _split_dot(a, w_hi, w_lo):
    a_hi = a.astype(BF16)
    a_lo = (a - a_hi.astype(F32)).astype(BF16)
    return (jnp.dot(a_hi, w_hi, preferred_element_type=F32)
            + jnp.dot(a_hi, w_lo, preferred_element_type=F32)
            + jnp.dot(a_lo, w_hi, preferred_element_type=F32))


def _outproj_kernel(x_ref, oa_ref, ob_ref, g1_ref, sh2_ref, sc2_ref, n2_ref, woa_ref, wob_ref,
                    wrh_ref, wrl_ref, br_ref, x1_ref, h2_ref, idx_ref, prob_ref):
    mix = (jnp.dot(oa_ref[...].astype(BF16), woa_ref[...], preferred_element_type=F32)
           + jnp.dot(ob_ref[...].astype(BF16), wob_ref[...], preferred_element_type=F32))
    x1 = x_ref[...] + g1_ref[...] * mix
    x1_ref[...] = x1
    h2 = x1 * lax.rsqrt(jnp.mean(x1 * x1, axis=-1, keepdims=True) + EPS) * n2_ref[...]
    h2 = h2 * (1.0 + sc2_ref[...]) + sh2_ref[...]
    _store_slabs(h2_ref, h2)

    logits = _split_dot(h2, wrh_ref[...], wrl_ref[...]) + br_ref[...]
    tm = logits.shape[0]
    lane = lax.broadcasted_iota(I32, (tm, N_EXPERTS), 1)
    kcol = lax.broadcasted_iota(I32, (tm, TOP_K), 1)
    vals = jnp.zeros((tm, TOP_K), F32)
    idxs = jnp.zeros((tm, TOP_K), I32)
    for k in range(TOP_K):
        m = jnp.max(logits, axis=-1, keepdims=True)
        am = jnp.min(jnp.where(logits == m, lane, N_EXPERTS), axis=-1, keepdims=True)
        vals = jnp.where(kcol == k, m, vals)
        idxs = jnp.where(kcol == k, am, idxs)
        logits = jnp.where(lane == am, -jnp.inf, logits)
    e = jnp.exp(vals - vals[:, 0:1])
    prob_ref[...] = e / jnp.sum(e, axis=-1, keepdims=True)
    idx_ref[...] = idxs


def _outproj(x, o_a, o_b, mod, per_token, seq, norm2, w_out_bf, wr_hi, wr_lo, b_router):
    n = x.shape[0]
    tm = min(256, n)
    tps = max(seq // tm, 1)
    row = lambda width: pl.BlockSpec((tm, width), lambda i: (i, 0))
    const = lambda shape: pl.BlockSpec(shape, lambda i: (0,) * len(shape))
    return pl.pallas_call(
        _outproj_kernel,
        grid=(n // tm,),
        in_specs=[row(D_MODEL), row(D_A), row(D_B),
                  _mod_spec(per_token, tm, tps, 2),
                  _mod_spec(per_token, tm, tps, 3),
                  _mod_spec(per_token, tm, tps, 4),
                  const((1, D_MODEL)),
                  pl.BlockSpec((D_A, D_MODEL), lambda i: (0, 0)),
                  pl.BlockSpec((D_B, D_MODEL), lambda i: (1, 0)),
                  const((D_MODEL, N_EXPERTS)), const((D_MODEL, N_EXPERTS)), const((1, N_EXPERTS))],
        out_specs=[row(D_MODEL), pl.BlockSpec((tm * SLAB, LANES), lambda i: (i, 0)),
                   row(TOP_K), row(TOP_K)],
        out_shape=[jax.ShapeDtypeStruct((n, D_MODEL), F32),
                   jax.ShapeDtypeStruct((n * SLAB, LANES), F32),
                   jax.ShapeDtypeStruct((n, TOP_K), I32),
                   jax.ShapeDtypeStruct((n, TOP_K), F32)],
        compiler_params=_cparams(("arbitrary",)),
        name="outproj",
    )(x, o_a, o_b, mod, mod, mod, norm2.reshape(1, D_MODEL), w_out_bf, w_out_bf,
      wr_hi, wr_lo, b_router.reshape(1, N_EXPERTS))


def _rank_kernel(idx_ref, dest_ref, cnt_ref, carry_ref, *, te):
    phase = pl.program_id(0)
    i = pl.program_id(1)
    idx = idx_ref[...]
    lane = lax.broadcasted_iota(I32, (te, N_EXPERTS), 1)
    onehots = [(idx[:, k:k + 1] == lane) for k in range(TOP_K)]
    member = jnp.zeros((te, N_EXPERTS), F32)
    for oh in onehots:
        member = member + oh.astype(F32)
    colsum = jnp.sum(member, axis=0, keepdims=True)

    @pl.when((phase == 0) & (i == 0))
    def _():
        carry_ref[...] = jnp.zeros_like(carry_ref)

    @pl.when(phase == 0)
    def _():
        carry_ref[...] += colsum

    @pl.when((phase == 1) & (i == 0))
    def _():
        cnt = carry_ref[...]
        cnt_ref[...] = cnt.astype(I32)
        tiles = jnp.floor((cnt + (ROW_TILE - 1)) * (1.0 / ROW_TILE))
        r = lax.broadcasted_iota(I32, (N_EXPERTS, N_EXPERTS), 0)
        c = lax.broadcasted_iota(I32, (N_EXPERTS, N_EXPERTS), 1)
        before = (r < c).astype(BF16)
        carry_ref[...] = ROW_TILE * jnp.dot(tiles.astype(BF16), before, preferred_element_type=F32)

    @pl.when(phase == 1)
    def _():
        r = lax.broadcasted_iota(I32, (te, te), 0)
        c = lax.broadcasted_iota(I32, (te, te), 1)
        earlier = (r > c).astype(BF16)
        base = carry_ref[...] + jnp.dot(earlier, member.astype(BF16), preferred_element_type=F32)
        kcol = lax.broadcasted_iota(I32, (te, TOP_K), 1)
        dest = jnp.zeros((te, TOP_K), F32)
        for k, oh in enumerate(onehots):
            dk = jnp.sum(jnp.where(oh, base, 0.0), axis=-1, keepdims=True)
            dest = jnp.where(kcol == k, dk, dest)
        dest_ref[...] = dest.astype(I32)
        carry_ref[...] += colsum


def _rank(top_idx):
    n = top_idx.shape[0]
    te = 512
    return pl.pallas_call(
        functools.partial(_rank_kernel, te=te),
        grid=(2, n // te),
        in_specs=[pl.BlockSpec((te, TOP_K), lambda p, i: (i, 0))],
        out_specs=[pl.BlockSpec((te, TOP_K), lambda p, i: (p * i, 0)),
                   pl.BlockSpec((1, N_EXPERTS), lambda p, i: (0, 0))],
        out_shape=[jax.ShapeDtypeStruct((n, TOP_K), I32),
                   jax.ShapeDtypeStruct((1, N_EXPERTS), I32)],
        scratch_shapes=[pltpu.VMEM((1, N_EXPERTS), F32)],
        compiler_params=_cparams(("arbitrary", "arbitrary")),
        name="rank",
    )(top_idx)


def _dispatch_kernel(last_ref, ntile_ref, used_ref, dest_ref, h_ref, xs_ref, zero_ref, sem, zsem,
                     *, n_tiles):
    i = pl.program_id(0)

    @pl.when(i == 0)
    def _():
        zero_ref[...] = jnp.zeros_like(zero_ref)

        def clear(tile):
            rows = pl.ds(pl.multiple_of(tile * (ROW_TILE * SLAB), ROW_TILE * SLAB), ROW_TILE * SLAB)
            return pltpu.make_async_copy(zero_ref, xs_ref.at[rows], zsem)

        def over_tail(fn):
            def body(t, carry):
                fn(clear(t))
                return carry
            lax.fori_loop(used_ref[0], n_tiles, body, 0)

        for fn in (lambda c: c.start(), lambda c: c.wait()):
            for e in range(N_EXPERTS):
                @pl.when(ntile_ref[e] > 0)
                def _():
                    fn(clear(last_ref[e]))
            over_tail(fn)

    def body(r, carry):
        for k in range(TOP_K):
            d = dest_ref[r * TOP_K + k]
            pltpu.make_async_copy(h_ref.at[_slab_rows(r)], xs_ref.at[_slab_rows(d)], sem).start()
        return carry

    lax.fori_loop(0, TOK_TILE, body, 0)
    for k in range(TOP_K):
        pltpu.make_async_copy(h_ref, xs_ref.at[pl.ds(0, TOK_TILE * SLAB)], sem).wait()


def _dispatch(h2, dest_flat, last_tile, ntiles, used, n_tiles):
    n = h2.shape[0] // SLAB
    grid_spec = pltpu.PrefetchScalarGridSpec(
        num_scalar_prefetch=3,
        grid=(n // TOK_TILE,),
        in_specs=[pl.BlockSpec((TOK_TILE * TOP_K,), lambda i, *_: (i,), memory_space=pltpu.SMEM),
                  pl.BlockSpec((TOK_TILE * SLAB, LANES), lambda i, *_: (i, 0))],
        out_specs=pl.BlockSpec(memory_space=pl.ANY),
        scratch_shapes=[pltpu.VMEM((ROW_TILE * SLAB, LANES), F32),
                        pltpu.SemaphoreType.DMA(()), pltpu.SemaphoreType.DMA(())],
    )
    return pl.pallas_call(
        functools.partial(_dispatch_kernel, n_tiles=n_tiles),
        grid_spec=grid_spec,
        out_shape=jax.ShapeDtypeStruct((n_tiles * ROW_TILE * SLAB, LANES), F32),
        compiler_params=_cparams(("arbitrary",)),
        name="dispatch",
    )(last_tile, ntiles, used, dest_flat, h2)


def _experts_kernel(te_ref, used_ref, xs_ref, wgu_ref, bgu_ref, wd_ref, bd_ref, ys_ref,
                    wgu_bf, wd_bf):
    i = pl.program_id(0)
    prev = te_ref[jnp.maximum(i - 1, 0)]
    fresh = (i == 0) | (te_ref[i] != prev)

    @pl.when(fresh)
    def _():
        wgu_bf[...] = wgu_ref[...].astype(BF16)
        wd_bf[...] = wd_ref[...].astype(BF16)

    @pl.when(i < used_ref[0])
    def _():
        x = jnp.concatenate([p.astype(BF16) for p in _load_slabs(xs_ref, ROW_TILE)], axis=1)
        acc = jnp.zeros((ROW_TILE, D_MODEL), F32) + bd_ref[...]
        nch = 4
        cw = D_FF // nch
        for c in range(nch):
            gate = jnp.dot(x, wgu_bf[:, c * cw:(c + 1) * cw], preferred_element_type=F32)
            gate = jnp.minimum(gate + bgu_ref[:, c * cw:(c + 1) * cw], SWIGLU_LIMIT)
            up = jnp.dot(x, wgu_bf[:, D_FF + c * cw:D_FF + (c + 1) * cw], preferred_element_type=F32)
            up = jnp.clip(up + bgu_ref[:, D_FF + c * cw:D_FF + (c + 1) * cw], -SWIGLU_LIMIT, SWIGLU_LIMIT)
            act = gate * jax.nn.sigmoid(SWIGLU_ALPHA * gate) * (up + 1.0)
            acc = acc + jnp.dot(act.astype(BF16), wd_bf[c * cw:(c + 1) * cw, :],
                                preferred_element_type=F32)
        _store_slabs(ys_ref, acc)

    @pl.when(i >= used_ref[0])
    def _():
        ys_ref[...] = jnp.zeros_like(ys_ref)


def _experts(xs, tile_expert, used, w_gu, b_gu, w_down, b_down):
    rows = xs.shape[0] // SLAB
    nt = rows // ROW_TILE

    def tile(i, te, used):
        return (jnp.minimum(i, used[0] - 1), 0)

    grid_spec = pltpu.PrefetchScalarGridSpec(
        num_scalar_prefetch=2,
        grid=(nt,),
        in_specs=[pl.BlockSpec((ROW_TILE * SLAB, LANES), tile),
                  pl.BlockSpec((None, D_MODEL, 2 * D_FF), lambda i, te, used: (te[i], 0, 0)),
                  pl.BlockSpec((None, 1, 2 * D_FF), lambda i, te, used: (te[i], 0, 0)),
                  pl.BlockSpec((None, D_FF, D_MODEL), lambda i, te, used: (te[i], 0, 0)),
                  pl.BlockSpec((None, 1, D_MODEL), lambda i, te, used: (te[i], 0, 0))],
        out_specs=pl.BlockSpec((ROW_TILE * SLAB, LANES), lambda i, te, used: (i, 0)),
        scratch_shapes=[pltpu.VMEM((D_MODEL, 2 * D_FF), BF16), pltpu.VMEM((D_FF, D_MODEL), BF16)],
    )
    return pl.pallas_call(
        _experts_kernel,
        grid_spec=grid_spec,
        out_shape=jax.ShapeDtypeStruct((rows * SLAB, LANES), F32),
        compiler_params=_cparams(("arbitrary",)),
        name="experts",
    )(tile_expert, used, xs, w_gu, b_gu.reshape(N_EXPERTS, 1, 2 * D_FF), w_down,
      b_down.reshape(N_EXPERTS, 1, D_MODEL))


def _combine_kernel(dest_ref, ys_ref, x1_ref, p_ref, g2_ref, fn_ref, y_ref, buf_ref, sem, *, last):
    def body(r, carry):
        for k in range(TOP_K):
            d = dest_ref[r * TOP_K + k]
            pltpu.make_async_copy(ys_ref.at[_slab_rows(d)], buf_ref.at[k, _slab_rows(r)], sem).start()
        return carry

    lax.fori_loop(0, TOK_TILE, body, 0)
    for k in range(TOP_K):
        pltpu.make_async_copy(ys_ref.at[pl.ds(0, TOK_TILE * SLAB)], buf_ref.at[k], sem).wait()

    p = p_ref[...]
    groups = None
    for k in range(TOP_K):
        part = [p[:, k:k + 1] * g for g in _load_slabs(buf_ref.at[k], TOK_TILE)]
        groups = part if groups is None else [a + b for a, b in zip(groups, part)]
    moe = jnp.concatenate(groups, axis=1)
    x2 = x1_ref[...] + g2_ref[...] * moe
    if last:
        x2 = x2 * lax.rsqrt(jnp.mean(x2 * x2, axis=-1, keepdims=True) + EPS) * fn_ref[...]
    y_ref[...] = x2


def _combine(ys, dest_flat, x1, probs, mod, per_token, seq, tile0, final_norm, last):
    n = x1.shape[0]
    tm = TOK_TILE
    tps = max(seq // tm, 1)
    return pl.pallas_call(
        functools.partial(_combine_kernel, last=last),
        grid=(n // tm,),
        in_specs=[pl.BlockSpec((tm * TOP_K,), lambda i: (i + tile0,), memory_space=pltpu.SMEM),
                  pl.BlockSpec(memory_space=pl.ANY),
                  pl.BlockSpec((tm, D_MODEL), lambda i: (i, 0)),
                  pl.BlockSpec((tm, TOP_K), lambda i: (i + tile0, 0)),
                  _mod_spec(per_token, tm, tps, 5),
                  pl.BlockSpec((1, D_MODEL), lambda i: (0, 0))],
        out_specs=pl.BlockSpec((tm, D_MODEL), lambda i: (i, 0)),
        out_shape=jax.ShapeDtypeStruct((n, D_MODEL), F32),
        scratch_shapes=[pltpu.VMEM((TOP_K, tm * SLAB, LANES), F32), pltpu.SemaphoreType.DMA(())],
        compiler_params=_cparams(("arbitrary",)),
        name="combine",
    )(dest_flat, ys, x1, probs, mod, final_norm.reshape(1, D_MODEL))


def _invert_kernel(dest_ref, src0_ref, dst0_ref, src_ref, dst_ref, sem, *, n_tok):
    i = pl.program_id(0)
    blk = dest_ref.shape[0]

    @pl.when(i == 0)
    def _():
        copies = [pltpu.make_async_copy(src0_ref, src_ref, sem.at[0]),
                  pltpu.make_async_copy(dst0_ref, dst_ref, sem.at[1])]
        for cp in copies:
            cp.start()
        for cp in copies:
            cp.wait()

    def body(a, carry):
        g = i * blk + a
        d = dest_ref[a]
        tok = g >> (TOP_K.bit_length() - 1)
        src_ref[d] = tok
        dst_ref[d] = (g & (TOP_K - 1)) * n_tok + tok
        return carry

    lax.fori_loop(0, blk, body, 0, unroll=8)


def _invert(dest_flat, n_tok, n_rows):
    blk = TOK_TILE * TOP_K
    whole = pl.BlockSpec(memory_space=pltpu.SMEM)
    src0 = jnp.zeros((n_rows,), I32)
    dst0 = n_tok * TOP_K + jnp.arange(n_rows, dtype=I32)
    return pl.pallas_call(
        functools.partial(_invert_kernel, n_tok=n_tok),
        grid=(dest_flat.shape[0] // blk,),
        in_specs=[pl.BlockSpec((blk,), lambda i: (i,), memory_space=pltpu.SMEM),
                  pl.BlockSpec(memory_space=pl.ANY), pl.BlockSpec(memory_space=pl.ANY)],
        out_specs=[whole, whole],
        out_shape=[jax.ShapeDtypeStruct((n_rows,), I32), jax.ShapeDtypeStruct((n_rows,), I32)],
        scratch_shapes=[pltpu.SemaphoreType.DMA((2,))],
        compiler_params=_cparams(("arbitrary",)),
        name="invert",
    )(dest_flat, src0, dst0)


FF_CHUNKS = 4


def _moe_kernel(src_ref, dst_ref, te_ref, used_ref, h_ref, wgu_ref, bgu_ref, wd_ref, bd_ref, y_ref,
                wgu_bf, wd_bf, act_bf, buf, gsem, ssem):
    i = pl.program_id(0)
    used = used_ref[0]
    slot = i % 2
    tile_rows = ROW_TILE * SLAB
    n_tiles = te_ref.shape[0]

    def gather(tile, sl, rows):
        for r in rows:
            tok = src_ref[tile * ROW_TILE + r]
            pltpu.make_async_copy(h_ref.at[_slab_rows(tok)], buf.at[sl, _slab_rows(r)],
                                  gsem.at[sl]).start()

    def scatter(tile, sl, rows):
        for r in rows:
            out = dst_ref[tile * ROW_TILE + r]
            pltpu.make_async_copy(buf.at[2 + sl, _slab_rows(r)], y_ref.at[_slab_rows(out)],
                                  ssem.at[sl]).start(priority=1)

    def wait_gather(sl):
        pltpu.make_async_copy(h_ref.at[pl.ds(0, tile_rows)], buf.at[sl], gsem.at[sl]).wait()

    def wait_scatter(sl):
        pltpu.make_async_copy(buf.at[2 + sl], y_ref.at[pl.ds(0, tile_rows)], ssem.at[sl]).wait()

    @pl.when(i == 0)
    def _():
        buf[3] = jnp.zeros(buf.shape[1:], F32)
        gather(0, 0, range(ROW_TILE))

    prev = te_ref[jnp.maximum(i - 1, 0)]
    fresh = (i == 0) | (te_ref[jnp.minimum(i, n_tiles - 1)] != prev)

    @pl.when(fresh & (i < used))
    def _():
        wgu_bf[...] = wgu_ref[...].astype(BF16)
        wd_bf[...] = wd_ref[...].astype(BF16)

    per = ROW_TILE // FF_CHUNKS
    cw = D_FF // FF_CHUNKS

    @pl.when(i < used)
    def _():
        wait_gather(slot)

        @pl.when(i >= 1)
        def _():
            wait_scatter(slot)

        nxt = jnp.minimum(i + 1, n_tiles - 1)
        prv = jnp.maximum(i - 1, 0)
        half = per // 2
        for c in range(FF_CHUNKS):
            gather(nxt, 1 - slot, range(c * per, (c + 1) * per))
            scatter(prv, 1 - slot, range(c * half, (c + 1) * half))
            x = jnp.concatenate([p.astype(BF16) for p in _load_slabs(buf.at[slot], ROW_TILE)], axis=1)
            gate = jnp.dot(x, wgu_bf[:, c * cw:(c + 1) * cw], preferred_element_type=F32)
            gate = jnp.minimum(gate + bgu_ref[:, c * cw:(c + 1) * cw], SWIGLU_LIMIT)
            up = jnp.dot(x, wgu_bf[:, D_FF + c * cw:D_FF + (c + 1) * cw], preferred_element_type=F32)
            up = jnp.clip(up + bgu_ref[:, D_FF + c * cw:D_FF + (c + 1) * cw], -SWIGLU_LIMIT, SWIGLU_LIMIT)
            act = gate * jax.nn.sigmoid(SWIGLU_ALPHA * gate) * (up + 1.0)
            act_bf[:, c * cw:(c + 1) * cw] = act.astype(BF16)
        ow = D_MODEL // FF_CHUNKS
        base = FF_CHUNKS * half
        for c in range(FF_CHUNKS):
            scatter(prv, 1 - slot, range(base + c * half, base + (c + 1) * half))
            out = jnp.dot(act_bf[...], wd_bf[:, c * ow:(c + 1) * ow], preferred_element_type=F32)
            out = out + bd_ref[:, c * ow:(c + 1) * ow]
            for j in range(ow // LANES):
                buf[2 + slot, pl.ds(c * (ow // LANES) + j, ROW_TILE, stride=SLAB), :] = (
                    out[:, LANES * j:LANES * (j + 1)])

    @pl.when(i == used)
    def _():
        wait_gather(slot)
        wait_scatter(slot)
        scatter(i - 1, 1 - slot, range(ROW_TILE))
        wait_scatter(1 - slot)


def _moe(h2, src, dst, tile_expert, used, w_gu, b_gu, w_down, b_down, n_tok):
    n_tiles = tile_expert.shape[0]
    n_rows = n_tiles * ROW_TILE
    ex = lambda i, src, dst, te, used: (te[jnp.minimum(i, n_tiles - 1)], 0, 0)
    grid_spec = pltpu.PrefetchScalarGridSpec(
        num_scalar_prefetch=4,
        grid=(n_tiles + 1,),
        in_specs=[pl.BlockSpec(memory_space=pl.ANY),
                  pl.BlockSpec((None, D_MODEL, 2 * D_FF), ex),
                  pl.BlockSpec((None, 1, 2 * D_FF), ex),
                  pl.BlockSpec((None, D_FF, D_MODEL), ex),
                  pl.BlockSpec((None, 1, D_MODEL), ex)],
        out_specs=pl.BlockSpec(memory_space=pl.ANY),
        scratch_shapes=[pltpu.VMEM((D_MODEL, 2 * D_FF), BF16), pltpu.VMEM((D_FF, D_MODEL), BF16),
                        pltpu.VMEM((ROW_TILE, D_FF), BF16),
                        pltpu.VMEM((4, ROW_TILE * SLAB, LANES), F32),
                        pltpu.SemaphoreType.DMA((2,)), pltpu.SemaphoreType.DMA((2,))],
    )
    return pl.pallas_call(
        _moe_kernel,
        grid_spec=grid_spec,
        out_shape=jax.ShapeDtypeStruct(((n_tok * TOP_K + n_rows) * SLAB, LANES), F32),
        compiler_params=_cparams(("arbitrary",)),
        name="moe",
    )(src, dst, tile_expert, used, h2, w_gu, b_gu.reshape(N_EXPERTS, 1, 2 * D_FF), w_down,
      b_down.reshape(N_EXPERTS, 1, D_MODEL))


def _merge_kernel(y0_ref, y1_ref, y2_ref, y3_ref, x1_ref, p_ref, g2_ref, fn_ref, o_ref, *, last):
    p = p_ref[...]
    tm = p.shape[0]
    groups = None
    for k, ref in enumerate((y0_ref, y1_ref, y2_ref, y3_ref)):
        part = [p[:, k:k + 1] * g for g in _load_slabs(ref, tm)]
        groups = part if groups is None else [a + b for a, b in zip(groups, part)]
    x2 = x1_ref[...] + g2_ref[...] * jnp.concatenate(groups, axis=1)
    if last:
        x2 = x2 * lax.rsqrt(jnp.mean(x2 * x2, axis=-1, keepdims=True) + EPS) * fn_ref[...]
    o_ref[...] = x2


def _merge(y, x1, probs, mod, per_token, seq, tile0, n_tok, final_norm, last):
    n = x1.shape[0]
    tm = TOK_TILE
    tps = max(seq // tm, 1)
    choice = lambda k: pl.BlockSpec((tm * SLAB, LANES), lambda i: (k * (n_tok // tm) + tile0 + i, 0))
    return pl.pallas_call(
        functools.partial(_merge_kernel, last=last),
        grid=(n // tm,),
        in_specs=[choice(0), choice(1), choice(2), choice(3),
                  pl.BlockSpec((tm, D_MODEL), lambda i: (i, 0)),
                  pl.BlockSpec((tm, TOP_K), lambda i: (i + tile0, 0)),
                  _mod_spec(per_token, tm, tps, 5),
                  pl.BlockSpec((1, D_MODEL), lambda i: (0, 0))],
        out_specs=pl.BlockSpec((tm, D_MODEL), lambda i: (i, 0)),
        out_shape=jax.ShapeDtypeStruct((n, D_MODEL), F32),
        compiler_params=_cparams(("arbitrary",)),
        name="merge",
    )(y, y, y, y, x1, probs, mod, final_norm.reshape(1, D_MODEL))


def kernel(x_prompt, x_sample, state_hgrn, state_pool, c_prompt, c_sample, w_ada, b_ada, norm1,
           norm2, w_in, lower_bounds, hgrn_norm, w_pool, pool_scale, w_out, w_router, b_router,
           w_gu, b_gu, w_down, b_down, final_norm):
    bp, seq_p, _ = x_prompt.shape
    bs, seq_s, _ = x_sample.shape
    np_, ns = bp * seq_p, bs * seq_s
    depth = w_ada.shape[0]
    past_len = 16384

    lbs = jnp.cumsum(jax.nn.softmax(lower_bounds.astype(F32), axis=0), axis=0)
    xp = x_prompt.reshape(np_, D_MODEL)
    xs_ = x_sample.reshape(ns, D_MODEL)
    c_all = jnp.concatenate([c_prompt, c_sample], axis=0)

    sp_l, hp_l, ss_l, hs_l = [], [], [], []
    for l in range(depth):
        w_in_bf = w_in[l].astype(BF16)
        w_out_bf = w_out[l].astype(BF16)
        w_pool_bf = w_pool[l].astype(BF16)
        wr_hi = w_router[l].astype(BF16)
        wr_lo = (w_router[l] - wr_hi.astype(F32)).astype(BF16)
        lb = lbs[l].reshape(1, D_A)
        hn = hgrn_norm[l].reshape(1, D_A)
        psc = pool_scale[l].reshape(1, D_B)

        mod = _ada(c_all, w_ada[l], b_ada[l])
        mod_p = mod[:bp].reshape(bp, 1, 6 * D_MODEL)
        mod_s = jnp.repeat(mod[bp:], seq_s, axis=0)

        up = _inproj(xp, mod_p, False, seq_p, norm1[l], w_in_bf)
        us = _inproj(xs_, mod_s, True, seq_s, norm1[l], w_in_bf)
        us3 = us.reshape(bs, seq_s, D_IN)

        oa_p, s_p = _hgrn(up.reshape(bp, seq_p, D_IN), None, lb, hn)
        oa_p = oa_p.reshape(np_, D_A)
        oa_s, s_s = _hgrn(us3, state_hgrn[l], lb, hn)
        ob_p, h_p = _pool_prompt(up, bp, seq_p, w_pool_bf, psc)
        ob_t, h_t = _pool_sample(us3.transpose(1, 0, 2), state_pool[l].transpose(1, 0, 2), past_len,
                                 w_pool_bf, psc)
        ob_s, h_s = ob_t.transpose(1, 0, 2), h_t.transpose(1, 0, 2)

        x1p, h2p, idxp, prp = _outproj(xp, oa_p, ob_p, mod_p, False, seq_p, norm2[l], w_out_bf,
                                       wr_hi, wr_lo, b_router[l])
        x1s, h2s, idxs, prs = _outproj(xs_, oa_s.reshape(ns, D_A), ob_s.reshape(ns, D_B), mod_s,
                                       True, seq_s, norm2[l], w_out_bf, wr_hi, wr_lo, b_router[l])

        h2 = jnp.concatenate([h2p, h2s], axis=0)
        top_idx = jnp.concatenate([idxp, idxs], axis=0)
        probs = jnp.concatenate([prp, prs], axis=0)
        n_tok = np_ + ns
        n_tiles = (n_tok * TOP_K) // ROW_TILE + N_EXPERTS

        dest, counts = _rank(top_idx)
        dest_flat = dest.reshape(n_tok * TOP_K)
        tiles_e = (counts.reshape(N_EXPERTS) + ROW_TILE - 1) // ROW_TILE
        ends = jnp.cumsum(tiles_e)
        used = ends[-1:].astype(I32)
        tile_expert = jnp.minimum(
            jnp.sum(jnp.arange(n_tiles, dtype=I32)[:, None] >= ends[None, :], axis=1),
            N_EXPERTS - 1).astype(I32)

        src, dst = _invert(dest_flat, n_tok, n_tiles * ROW_TILE)
        y = _moe(h2, src, dst, tile_expert, used, w_gu[l], b_gu[l], w_down[l], b_down[l], n_tok)

        last = l == depth - 1
        xp = _merge(y, x1p, probs, mod_p, False, seq_p, 0, n_tok, final_norm, last)
        xs_ = _merge(y, x1s, probs, mod_s, True, seq_s, np_ // TOK_TILE, n_tok, final_norm, last)

        sp_l.append(s_p)
        hp_l.append(h_p)
        ss_l.append(s_s)
        hs_l.append(h_s)

    return (xp.reshape(bp, seq_p, D_MODEL), xs_.reshape(bs, seq_s, D_MODEL),
            jnp.stack(sp_l), jnp.stack(hp_l), jnp.stack(ss_l), jnp.stack(hs_l))
```

```python
import functools

import jax
import jax.numpy as jnp
from jax import lax
from jax.experimental import pallas as pl
from jax.experimental.pallas import tpu as pltpu

F32, BF16, I32 = jnp.float32, jnp.bfloat16, jnp.int32

D_MODEL = 1024
D_A = 512
H_A = 4
DK = 128
DV = 128
D_B = 512
POOL_WINDOWS = (2, 4, 8, 16)
POOL_GROUP = 128
POOL_HIST = 15
D_IN = 4 * D_A + D_B
N_EXPERTS = 32
TOP_K = 4
D_FF = 1024
SWIGLU_LIMIT = 7.0
SWIGLU_ALPHA = 1.702
EPS = 1e-6

SUB = 16
ROW_TILE = 256
TOK_TILE = 256
VMEM_LIMIT = 56 * 1024 * 1024


def _cparams(sem):
    return pltpu.CompilerParams(dimension_semantics=sem, vmem_limit_bytes=VMEM_LIMIT)


def _silu(x):
    return x * jax.nn.sigmoid(x)


def _mod_spec(per_token, tm, tiles_per_seq, col):
    if per_token:
        return pl.BlockSpec((tm, D_MODEL), lambda i: (i, col))
    return pl.BlockSpec((None, 1, D_MODEL), lambda i: (i // tiles_per_seq, 0, col))


def _ada_kernel(c_ref, w_ref, b_ref, o_ref):
    c = c_ref[...]
    o_ref[...] = jnp.dot(_silu(c).astype(BF16), w_ref[...].astype(BF16),
                         preferred_element_type=F32) + b_ref[...]


def _ada(c_all, w_ada, b_ada):
    rows = c_all.shape[0]
    n = w_ada.shape[1]
    return pl.pallas_call(
        _ada_kernel,
        grid=(n // D_MODEL,),
        in_specs=[pl.BlockSpec((rows, D_MODEL), lambda j: (0, 0)),
                  pl.BlockSpec((D_MODEL, D_MODEL), lambda j: (0, j)),
                  pl.BlockSpec((1, D_MODEL), lambda j: (0, j))],
        out_specs=pl.BlockSpec((rows, D_MODEL), lambda j: (0, j)),
        out_shape=jax.ShapeDtypeStruct((rows, n), F32),
        compiler_params=_cparams(("arbitrary",)),
        name="ada",
    )(c_all, w_ada, b_ada.reshape(1, n))


def _inproj_kernel(x_ref, sh_ref, sc_ref, n1_ref, w_ref, u_ref):
    x = x_ref[...]
    h = x * lax.rsqrt(jnp.mean(x * x, axis=-1, keepdims=True) + EPS) * n1_ref[...]
    h = h * (1.0 + sc_ref[...]) + sh_ref[...]
    u_ref[...] = jnp.dot(h.astype(BF16), w_ref[...], preferred_element_type=F32)


def _inproj(x, mod, per_token, seq, norm1, w_in_bf):
    n = x.shape[0]
    tm = min(512, n)
    tps = max(seq // tm, 1)
    return pl.pallas_call(
        _inproj_kernel,
        grid=(n // tm,),
        in_specs=[pl.BlockSpec((tm, D_MODEL), lambda i: (i, 0)),
                  _mod_spec(per_token, tm, tps, 0),
                  _mod_spec(per_token, tm, tps, 1),
                  pl.BlockSpec((1, D_MODEL), lambda i: (0, 0)),
                  pl.BlockSpec((D_MODEL, D_IN), lambda i: (0, 0))],
        out_specs=pl.BlockSpec((tm, D_IN), lambda i: (i, 0)),
        out_shape=jax.ShapeDtypeStruct((n, D_IN), F32),
        compiler_params=_cparams(("arbitrary",)),
        name="inproj",
    )(x, mod, mod, norm1.reshape(1, D_MODEL), w_in_bf)


def _block_select():
    r = lax.broadcasted_iota(I32, (SUB, SUB * DK), 0)
    c = lax.broadcasted_iota(I32, (SUB, SUB * DK), 1) >> (DK.bit_length() - 1)
    return (r == c).astype(BF16)


def _hgrn_block(q, fp, iv, g, lb, hn, st_ref, sel_ref, valid):
    f = lb + (1.0 - lb) * jax.nn.sigmoid(fp)
    logf = jnp.log(f)
    kk = 1.0 - f
    if valid < SUB:
        live = lax.broadcasted_iota(I32, (SUB, D_A), 0) < valid
        logf = jnp.where(live, logf, 0.0)
        kk = jnp.where(live, kk, 0.0)
    qa = _silu(q)

    r = lax.broadcasted_iota(I32, (SUB, SUB), 0)
    c = lax.broadcasted_iota(I32, (SUB, SUB), 1)
    tri = (r >= c).astype(BF16)
    hi = logf.astype(BF16)
    lo = (logf - hi.astype(F32)).astype(BF16)
    b = (jnp.dot(tri, hi, preferred_element_type=F32)
         + jnp.dot(tri, lo, preferred_element_type=F32))
    bend = b[SUB - 1:SUB, :]
    qt = (qa * jnp.exp(b)).astype(BF16)
    kh = (kk * jnp.exp(bend - b)).astype(BF16)
    dec = jnp.exp(bend)
    ivb = iv.astype(BF16)

    rows = []
    for h in range(H_A):
        sl = slice(DK * h, DK * (h + 1))
        bh, qh, kkh = b[:, sl], qa[:, sl], kk[:, sl]
        ps = []
        for s in range(SUB):
            e = jnp.exp(jnp.minimum(bh - bh[s:s + 1, :], 0.0))
            ps.append((e * (qh * kkh[s:s + 1, :])).astype(BF16))
        rows.append(jnp.concatenate(ps, axis=1))
    pcat = jnp.concatenate(rows, axis=0)
    a = lax.dot_general(pcat, sel_ref[...], (((1,), (1,)), ((), ())),
                        preferred_element_type=F32)
    tr = lax.broadcasted_iota(I32, (H_A * SUB, SUB), 0) & (SUB - 1)
    tc = lax.broadcasted_iota(I32, (H_A * SUB, SUB), 1)
    a = jnp.where(tr >= tc, a, 0.0).astype(BF16)

    outs = []
    for h in range(H_A):
        sl = slice(DK * h, DK * (h + 1))
        st = st_ref[h]
        o = lax.dot_general(qt[:, sl], st.astype(BF16), (((1,), (1,)), ((), ())),
                            preferred_element_type=F32)
        o = o + jnp.dot(a[SUB * h:SUB * (h + 1), :], ivb[:, sl], preferred_element_type=F32)
        ut = lax.dot_general(ivb[:, sl], kh[:, sl], (((0,), (0,)), ((), ())),
                             preferred_element_type=F32)
        st_ref[h] = st * dec[:, sl] + ut
        o = o * lax.rsqrt(jnp.mean(o * o, axis=-1, keepdims=True) + EPS) * hn[:, sl]
        outs.append(o)
    return jnp.concatenate(outs, axis=1) * _silu(g)


def _hgrn_prompt_kernel(q_ref, f_ref, i_ref, g_ref, lb_ref, hn_ref, o_ref, s_ref, st_ref, sel_ref,
                        *, tt):
    t = pl.program_id(1)

    @pl.when(t == 0)
    def _():
        st_ref[...] = jnp.zeros_like(st_ref)
        sel_ref[...] = _block_select()

    lb = lb_ref[...]
    hn = hn_ref[...]

    def body(j, carry):
        rs = pl.ds(pl.multiple_of(j * SUB, SUB), SUB)
        o_ref[rs, :] = _hgrn_block(q_ref[rs, :], f_ref[rs, :], i_ref[rs, :], g_ref[rs, :],
                                   lb, hn, st_ref, sel_ref, SUB)
        return carry

    lax.fori_loop(0, tt // SUB, body, 0)

    @pl.when(t == pl.num_programs(1) - 1)
    def _():
        for h in range(H_A):
            s_ref[h] = st_ref[h].T


def _hgrn_prompt(u, batch, seq, lb, hn):
    tt = 256
    nt = seq // tt

    def col(cb):
        return pl.BlockSpec((tt, D_A), lambda b, t: (b * nt + t, cb))

    return pl.pallas_call(
        functools.partial(_hgrn_prompt_kernel, tt=tt),
        grid=(batch, nt),
        in_specs=[col(0), col(1), col(2), col(3),
                  pl.BlockSpec((1, D_A), lambda b, t: (0, 0)),
                  pl.BlockSpec((1, D_A), lambda b, t: (0, 0))],
        out_specs=[pl.BlockSpec((tt, D_A), lambda b, t: (b * nt + t, 0)),
                   pl.BlockSpec((None, H_A, DK, DV), lambda b, t: (b, 0, 0, 0))],
        out_shape=[jax.ShapeDtypeStruct((batch * seq, D_A), F32),
                   jax.ShapeDtypeStruct((batch, H_A, DK, DV), F32)],
        scratch_shapes=[pltpu.VMEM((H_A, DV, DK), F32), pltpu.VMEM((SUB, SUB * DK), BF16)],
        compiler_params=_cparams(("arbitrary", "arbitrary")),
        name="hgrn_prompt",
    )(u, u, u, u, lb, hn)


def _hgrn_sample_kernel(q_ref, f_ref, i_ref, g_ref, lb_ref, hn_ref, s0_ref, o_ref, s_ref,
                        st_ref, sel_ref, pad_ref, *, seq):
    for h in range(H_A):
        st_ref[h] = s0_ref[h].T
    sel_ref[...] = _block_select()
    pad_ref[...] = jnp.zeros_like(pad_ref)
    for n, ref in enumerate((q_ref, f_ref, i_ref, g_ref)):
        pad_ref[n, 0:seq, :] = ref[...]
    o = _hgrn_block(pad_ref[0], pad_ref[1], pad_ref[2], pad_ref[3], lb_ref[...], hn_ref[...],
                    st_ref, sel_ref, seq)
    o_ref[...] = o[0:seq, :]
    for h in range(H_A):
        s_ref[h] = st_ref[h].T


def _hgrn_sample(u3, s0, lb, hn):
    batch, seq, _ = u3.shape

    def col(cb):
        return pl.BlockSpec((None, seq, D_A), lambda b: (b, 0, cb))

    return pl.pallas_call(
        functools.partial(_hgrn_sample_kernel, seq=seq),
        grid=(batch,),
        in_specs=[col(0), col(1), col(2), col(3),
                  pl.BlockSpec((1, D_A), lambda b: (0, 0)),
                  pl.BlockSpec((1, D_A), lambda b: (0, 0)),
                  pl.BlockSpec((None, H_A, DK, DV), lambda b: (b, 0, 0, 0))],
        out_specs=[pl.BlockSpec((None, seq, D_A), lambda b: (b, 0, 0)),
                   pl.BlockSpec((None, H_A, DK, DV), lambda b: (b, 0, 0, 0))],
        out_shape=[jax.ShapeDtypeStruct((batch, seq, D_A), F32),
                   jax.ShapeDtypeStruct((batch, H_A, DK, DV), F32)],
        scratch_shapes=[pltpu.VMEM((H_A, DV, DK), F32), pltpu.VMEM((SUB, SUB * DK), BF16),
                        pltpu.VMEM((4, SUB, D_A), F32)],
        compiler_params=_cparams(("arbitrary",)),
        name="hgrn_sample",
    )(u3, u3, u3, u3, lb, hn, s0)


HGRN_SEQS = 8
HALF = SUB // 2


def _hgrn_step(q, fp, iv, g, lb, hn, st_ref, sel_ref, cm_ref, b_ref, k_ref, valid):
    nrow = HGRN_SEQS * SUB
    f = lb + (1.0 - lb) * jax.nn.sigmoid(fp)
    lf = jnp.log2(f)
    kk = 1.0 - f
    if valid < SUB:
        live = (lax.broadcasted_iota(I32, (nrow, D_A), 0) & (SUB - 1)) < valid
        lf = jnp.where(live, lf, 0.0)
        kk = jnp.where(live, kk, 0.0)
    qa = _silu(q)

    hi = lf.astype(BF16)
    lo = (lf - hi.astype(F32)).astype(BF16)
    cum, tot = cm_ref[0], cm_ref[1]
    b = jnp.dot(cum, hi, preferred_element_type=F32) + jnp.dot(cum, lo, preferred_element_type=F32)
    be = jnp.dot(tot, hi, preferred_element_type=F32) + jnp.dot(tot, lo, preferred_element_type=F32)
    qt = (qa * jnp.exp2(b)).astype(BF16)
    kh = (kk * jnp.exp2(be - b)).astype(BF16)
    dec = jnp.exp2(be)
    ivb = iv.astype(BF16)
    for h in range(H_A):
        b_ref[h] = b[:, DK * h:DK * (h + 1)]
        k_ref[h] = kk[:, DK * h:DK * (h + 1)]

    zero_half = jnp.zeros((HALF, D_A), F32)
    blocks = []
    for n in range(HGRN_SEQS):
        per_s = []
        for s in range(SUB):
            row = n * SUB + s
            b_s = jnp.concatenate([b_ref[h, pl.ds(row, HALF, stride=0), :] for h in range(H_A)], axis=1)
            k_s = jnp.concatenate([k_ref[h, pl.ds(row, HALF, stride=0), :] for h in range(H_A)], axis=1)
            halves = []
            for hf in range(2):
                if hf == 0 and s >= HALF:
                    halves.append(zero_half)
                    continue
                r0 = n * SUB + hf * HALF
                e = jnp.exp2(jnp.minimum(b[r0:r0 + HALF] - b_s, 0.0))
                halves.append(e * (qa[r0:r0 + HALF] * k_s))
            per_s.append(jnp.concatenate(halves, axis=0).astype(BF16))
        for h in range(H_A):
            blocks.append(jnp.concatenate([p[:, DK * h:DK * (h + 1)] for p in per_s], axis=1))
    pcat = jnp.concatenate(blocks, axis=0)
    a = lax.dot_general(pcat, sel_ref[...], (((1,), (1,)), ((), ())),
                        preferred_element_type=F32)
    tr = lax.broadcasted_iota(I32, a.shape, 0) & (SUB - 1)
    tc = lax.broadcasted_iota(I32, a.shape, 1)
    a = jnp.where(tr >= tc, a, 0.0).astype(BF16)

    outs = []
    for n in range(HGRN_SEQS):
        rows = slice(n * SUB, (n + 1) * SUB)
        heads = []
        for h in range(H_A):
            sl = slice(DK * h, DK * (h + 1))
            st = st_ref[n, h]
            o = lax.dot_general(qt[rows, sl], st.astype(BF16), (((1,), (1,)), ((), ())),
                                preferred_element_type=F32)
            ar = (n * H_A + h) * SUB
            o = o + jnp.dot(a[ar:ar + SUB, :], ivb[rows, sl], preferred_element_type=F32)
            ut = lax.dot_general(ivb[rows, sl], kh[rows, sl], (((0,), (0,)), ((), ())),
                                 preferred_element_type=F32)
            dtile = dec[n * SUB:n * SUB + HALF, sl]
            st_ref[n, h] = (st.reshape(DV // HALF, HALF, DK) * dtile[None]).reshape(DV, DK) + ut
            o = o * lax.rsqrt(jnp.mean(o * o, axis=-1, keepdims=True) + EPS) * hn[:, sl]
            heads.append(o)
        outs.append(jnp.concatenate(heads, axis=1))
    return jnp.concatenate(outs, axis=0) * _silu(g)


def _hgrn_kernel(*refs, rows, has_s0):
    if has_s0:
        (q_ref, f_ref, i_ref, g_ref, lb_ref, hn_ref, s0_ref, o_ref, s_ref,
         st_ref, sel_ref, cm_ref, b_ref, k_ref, pad_ref) = refs
    else:
        (q_ref, f_ref, i_ref, g_ref, lb_ref, hn_ref, o_ref, s_ref,
         st_ref, sel_ref, cm_ref, b_ref, k_ref) = refs
    t = pl.program_id(0)
    nrow = HGRN_SEQS * SUB

    @pl.when(t == 0)
    def _():
        sel_ref[...] = _block_select()
        r = lax.broadcasted_iota(I32, (nrow, nrow), 0)
        c = lax.broadcasted_iota(I32, (nrow, nrow), 1)
        shift = SUB.bit_length() - 1
        same = (r >> shift) == (c >> shift)
        cm_ref[0] = (same & (r >= c)).astype(BF16)
        cm_ref[1] = same.astype(BF16)

    lb = lb_ref[...]
    hn = hn_ref[...]
    if has_s0:
        for n in range(HGRN_SEQS):
            for h in range(H_A):
                st_ref[n, h] = s0_ref[n, h].T
        pad_ref[...] = jnp.zeros_like(pad_ref)
        for j, ref in enumerate((q_ref, f_ref, i_ref, g_ref)):
            pad_ref[j, :, 0:rows, :] = ref[...]
        o = _hgrn_step(*(pad_ref[j].reshape(nrow, D_A) for j in range(4)), lb, hn,
                       st_ref, sel_ref, cm_ref, b_ref, k_ref, rows)
        o_ref[...] = o.reshape(HGRN_SEQS, SUB, D_A)[:, 0:rows, :]
    else:
        @pl.when(t == 0)
        def _():
            st_ref[...] = jnp.zeros_like(st_ref)

        def body(j, carry):
            rs = pl.ds(pl.multiple_of(j * SUB, SUB), SUB)
            o = _hgrn_step(*(ref[:, rs, :].reshape(nrow, D_A) for ref in (q_ref, f_ref, i_ref, g_ref)),
                           lb, hn, st_ref, sel_ref, cm_ref, b_ref, k_ref, SUB)
            o_ref[:, rs, :] = o.reshape(HGRN_SEQS, SUB, D_A)
            return carry

        lax.fori_loop(0, rows // SUB, body, 0)

    def emit_state():
        for n in range(HGRN_SEQS):
            for h in range(H_A):
                s_ref[n, h] = st_ref[n, h].T

    if has_s0:
        emit_state()
    else:
        pl.when(t == pl.num_programs(0) - 1)(emit_state)


def _hgrn(u3, s0, lb, hn):
    batch, seq, _ = u3.shape
    has_s0 = s0 is not None
    if has_s0:
        rows, grid = seq, (batch // HGRN_SEQS,)
        blk = lambda cb: pl.BlockSpec((HGRN_SEQS, rows, D_A), lambda i: (i, 0, cb))
        sblk = pl.BlockSpec((HGRN_SEQS, H_A, DK, DV), lambda i: (i, 0, 0, 0))
    else:
        rows, grid = 128, (seq // 128,)
        blk = lambda cb: pl.BlockSpec((HGRN_SEQS, rows, D_A), lambda i: (0, i, cb))
        sblk = pl.BlockSpec((HGRN_SEQS, H_A, DK, DV), lambda i: (0, 0, 0, 0))
    vec = pl.BlockSpec((1, D_A), lambda i: (0, 0))
    nrow = HGRN_SEQS * SUB
    scratch = [pltpu.VMEM((HGRN_SEQS, H_A, DV, DK), F32), pltpu.VMEM((SUB, SUB * DK), BF16),
               pltpu.VMEM((2, nrow, nrow), BF16), pltpu.VMEM((H_A, nrow, DK), F32),
               pltpu.VMEM((H_A, nrow, DK), F32)]
    if has_s0:
        scratch.append(pltpu.VMEM((4, HGRN_SEQS, SUB, D_A), F32))
    args = (u3, u3, u3, u3, lb, hn) + ((s0,) if has_s0 else ())
    return pl.pallas_call(
        functools.partial(_hgrn_kernel, rows=rows, has_s0=has_s0),
        grid=grid,
        in_specs=[blk(0), blk(1), blk(2), blk(3), vec, vec] + ([sblk] if has_s0 else []),
        out_specs=[blk(0), sblk],
        out_shape=[jax.ShapeDtypeStruct((batch, seq, D_A), F32),
                   jax.ShapeDtypeStruct((batch, H_A, DK, DV), F32)],
        scratch_shapes=scratch,
        compiler_params=_cparams(("arbitrary",)),
        name="hgrn_state" if has_s0 else "hgrn_scan",
    )(*args)


def _pool_groups(full_ref, base, tm, cnt_fn, wp_ref, ps_ref, step=1):
    outs = []
    for gi, w in enumerate(POOL_WINDOWS):
        ch = slice(POOL_GROUP * gi, POOL_GROUP * (gi + 1))
        v = full_ref[base:base + tm, ch]
        s = v
        for j in range(1, w):
            s = s + full_ref[base - j * step:base - j * step + tm, ch]
        d = s / cnt_fn(w) - v
        outs.append(jnp.dot(d.astype(BF16), wp_ref[gi], preferred_element_type=F32))
    return jnp.concatenate(outs, axis=1) * ps_ref[...]


def _pool_prompt_kernel(v_ref, prev_ref, wp_ref, ps_ref, o_ref, hist_ref, full_ref, *, tm):
    t = pl.program_id(1)
    prev = prev_ref[...]
    full_ref[0:16, :] = jnp.where(t == 0, 0.0, prev)
    full_ref[16:16 + tm, :] = v_ref[...]
    pos = (lax.broadcasted_iota(I32, (tm, 1), 0) + t * tm + 1).astype(F32)
    o_ref[...] = _pool_groups(full_ref, 16, tm, lambda w: jnp.minimum(pos, float(w)),
                              wp_ref, ps_ref)

    @pl.when(t == pl.num_programs(1) - 1)
    def _():
        hist_ref[...] = full_ref[16 + tm - POOL_HIST:16 + tm, :]


def _pool_prompt(u, batch, seq, w_pool_bf, pool_scale):
    tm = 256
    nt = seq // tm
    vcol = (4 * D_A) // D_B
    return pl.pallas_call(
        functools.partial(_pool_prompt_kernel, tm=tm),
        grid=(batch, nt),
        in_specs=[pl.BlockSpec((tm, D_B), lambda b, t: (b * nt + t, vcol)),
                  pl.BlockSpec((16, D_B),
                               lambda b, t: (jnp.maximum((b * nt + t) * (tm // 16) - 1, 0), vcol)),
                  pl.BlockSpec((len(POOL_WINDOWS), POOL_GROUP, POOL_GROUP), lambda b, t: (0, 0, 0)),
                  pl.BlockSpec((1, D_B), lambda b, t: (0, 0))],
        out_specs=[pl.BlockSpec((tm, D_B), lambda b, t: (b * nt + t, 0)),
                   pl.BlockSpec((None, POOL_HIST, D_B), lambda b, t: (b, 0, 0))],
        out_shape=[jax.ShapeDtypeStruct((batch * seq, D_B), F32),
                   jax.ShapeDtypeStruct((batch, POOL_HIST, D_B), F32)],
        scratch_shapes=[pltpu.VMEM((16 + tm, D_B), F32)],
        compiler_params=_cparams(("arbitrary", "arbitrary")),
        name="pool_prompt",
    )(u, u, w_pool_bf, pool_scale)


def _pool_sample_kernel(v_ref, hist_ref, wp_ref, ps_ref, o_ref, nh_ref, full_ref, *,
                        seq, batch, start):
    full_ref[0:batch, :] = jnp.zeros((batch, D_B), F32)
    full_ref[batch:16 * batch, :] = hist_ref[...].reshape(POOL_HIST * batch, D_B)
    full_ref[16 * batch:(16 + seq) * batch, :] = v_ref[...].reshape(seq * batch, D_B)

    def cnt(w):
        rows = [jnp.full((batch, 1), float(min(start + t + 1, w)), F32) for t in range(seq)]
        return jnp.concatenate(rows, axis=0)

    o = _pool_groups(full_ref, 16 * batch, seq * batch, cnt, wp_ref, ps_ref, step=batch)
    o_ref[...] = o.reshape(seq, batch, D_B)
    nh_ref[...] = full_ref[(16 + seq - POOL_HIST) * batch:(16 + seq) * batch, :].reshape(
        POOL_HIST, batch, D_B)


def _pool_sample(u_t, hist_t, start, w_pool_bf, pool_scale):
    seq, batch, _ = u_t.shape
    vcol = (4 * D_A) // D_B
    return pl.pallas_call(
        functools.partial(_pool_sample_kernel, seq=seq, batch=batch, start=start),
        grid=(1,),
        in_specs=[pl.BlockSpec((seq, batch, D_B), lambda i: (0, 0, vcol)),
                  pl.BlockSpec((POOL_HIST, batch, D_B), lambda i: (0, 0, 0)),
                  pl.BlockSpec((len(POOL_WINDOWS), POOL_GROUP, POOL_GROUP), lambda i: (0, 0, 0)),
                  pl.BlockSpec((1, D_B), lambda i: (0, 0))],
        out_specs=[pl.BlockSpec((seq, batch, D_B), lambda i: (0, 0, 0)),
                   pl.BlockSpec((POOL_HIST, batch, D_B), lambda i: (0, 0, 0))],
        out_shape=[jax.ShapeDtypeStruct((seq, batch, D_B), F32),
                   jax.ShapeDtypeStruct((POOL_HIST, batch, D_B), F32)],
        scratch_shapes=[pltpu.VMEM(((16 + seq) * batch, D_B), F32)],
        compiler_params=_cparams(("arbitrary",)),
        name="pool_sample",
    )(u_t, hist_t, w_pool_bf, pool_scale)


LANES = 128
SLAB = D_MODEL // LANES


def _store_slabs(ref, value):
    n = value.shape[0]
    for j in range(SLAB):
        ref[pl.ds(j, n, stride=SLAB), :] = value[:, LANES * j:LANES * (j + 1)]


def _load_slabs(ref, n):
    return [ref[pl.ds(j, n, stride=SLAB), :] for j in range(SLAB)]


def _slab_rows(row):
    return pl.ds(pl.multiple_of(row * SLAB, SLAB), SLAB)


def _split_dot(a, w_hi, w_lo):
    a_hi = a.astype(BF16)
    a_lo = (a - a_hi.astype(F32)).astype(BF16)
    return (jnp.dot(a_hi, w_hi, preferred_element_type=F32)
            + jnp.dot(a_hi, w_lo, preferred_element_type=F32)
            + jnp.dot(a_lo, w_hi, preferred_element_type=F32))


def _outproj_kernel(x_ref, oa_ref, ob_ref, g1_ref, sh2_ref, sc2_ref, n2_ref, woa_ref, wob_ref,
                    wrh_ref, wrl_ref, br_ref, x1_ref, h2_ref, idx_ref, prob_ref):
    mix = (jnp.dot(oa_ref[...].astype(BF16), woa_ref[...], preferred_element_type=F32)
           + jnp.dot(ob_ref[...].astype(BF16), wob_ref[...], preferred_element_type=F32))
    x1 = x_ref[...] + g1_ref[...] * mix
    x1_ref[...] = x1
    h2 = x1 * lax.rsqrt(jnp.mean(x1 * x1, axis=-1, keepdims=True) + EPS) * n2_ref[...]
    h2 = h2 * (1.0 + sc2_ref[...]) + sh2_ref[...]
    _store_slabs(h2_ref, h2)

    logits = _split_dot(h2, wrh_ref[...], wrl_ref[...]) + br_ref[...]
    tm = logits.shape[0]
    lane = lax.broadcasted_iota(I32, (tm, N_EXPERTS), 1)
    kcol = lax.broadcasted_iota(I32, (tm, TOP_K), 1)
    vals = jnp.zeros((tm, TOP_K), F32)
    idxs = jnp.zeros((tm, TOP_K), I32)
    for k in range(TOP_K):
        m = jnp.max(logits, axis=-1, keepdims=True)
        am = jnp.min(jnp.where(logits == m, lane, N_EXPERTS), axis=-1, keepdims=True)
        vals = jnp.where(kcol == k, m, vals)
        idxs = jnp.where(kcol == k, am, idxs)
        logits = jnp.where(lane == am, -jnp.inf, logits)
    e = jnp.exp(vals - vals[:, 0:1])
    prob_ref[...] = e / jnp.sum(e, axis=-1, keepdims=True)
    idx_ref[...] = idxs


def _outproj(x, o_a, o_b, mod, per_token, seq, norm2, w_out_bf, wr_hi, wr_lo, b_router):
    n = x.shape[0]
    tm = min(256, n)
    tps = max(seq // tm, 1)
    row = lambda width: pl.BlockSpec((tm, width), lambda i: (i, 0))
    const = lambda shape: pl.BlockSpec(shape, lambda i: (0,) * len(shape))
    return pl.pallas_call(
        _outproj_kernel,
        grid=(n // tm,),
        in_specs=[row(D_MODEL), row(D_A), row(D_B),
                  _mod_spec(per_token, tm, tps, 2),
                  _mod_spec(per_token, tm, tps, 3),
                  _mod_spec(per_token, tm, tps, 4),
                  const((1, D_MODEL)),
                  pl.BlockSpec((D_A, D_MODEL), lambda i: (0, 0)),
                  pl.BlockSpec((D_B, D_MODEL), lambda i: (1, 0)),
                  const((D_MODEL, N_EXPERTS)), const((D_MODEL, N_EXPERTS)), const((1, N_EXPERTS))],
        out_specs=[row(D_MODEL), pl.BlockSpec((tm * SLAB, LANES), lambda i: (i, 0)),
                   row(TOP_K), row(TOP_K)],
        out_shape=[jax.ShapeDtypeStruct((n, D_MODEL), F32),
                   jax.ShapeDtypeStruct((n * SLAB, LANES), F32),
                   jax.ShapeDtypeStruct((n, TOP_K), I32),
                   jax.ShapeDtypeStruct((n, TOP_K), F32)],
        compiler_params=_cparams(("arbitrary",)),
        name="outproj",
    )(x, o_a, o_b, mod, mod, mod, norm2.reshape(1, D_MODEL), w_out_bf, w_out_bf,
      wr_hi, wr_lo, b_router.reshape(1, N_EXPERTS))


def _rank_kernel(idx_ref, dest_ref, cnt_ref, carry_ref, *, te):
    phase = pl.program_id(0)
    i = pl.program_id(1)
    idx = idx_ref[...]
    lane = lax.broadcasted_iota(I32, (te, N_EXPERTS), 1)
    onehots = [(idx[:, k:k + 1] == lane) for k in range(TOP_K)]
    member = jnp.zeros((te, N_EXPERTS), F32)
    for oh in onehots:
        member = member + oh.astype(F32)
    colsum = jnp.sum(member, axis=0, keepdims=True)

    @pl.when((phase == 0) & (i == 0))
    def _():
        carry_ref[...] = jnp.zeros_like(carry_ref)

    @pl.when(phase == 0)
    def _():
        carry_ref[...] += colsum

    @pl.when((phase == 1) & (i == 0))
    def _():
        cnt = carry_ref[...]
        cnt_ref[...] = cnt.astype(I32)
        tiles = jnp.floor((cnt + (ROW_TILE - 1)) * (1.0 / ROW_TILE))
        r = lax.broadcasted_iota(I32, (N_EXPERTS, N_EXPERTS), 0)
        c = lax.broadcasted_iota(I32, (N_EXPERTS, N_EXPERTS), 1)
        before = (r < c).astype(BF16)
        carry_ref[...] = ROW_TILE * jnp.dot(tiles.astype(BF16), before, preferred_element_type=F32)

    @pl.when(phase == 1)
    def _():
        r = lax.broadcasted_iota(I32, (te, te), 0)
        c = lax.broadcasted_iota(I32, (te, te), 1)
        earlier = (r > c).astype(BF16)
        base = carry_ref[...] + jnp.dot(earlier, member.astype(BF16), preferred_element_type=F32)
        kcol = lax.broadcasted_iota(I32, (te, TOP_K), 1)
        dest = jnp.zeros((te, TOP_K), F32)
        for k, oh in enumerate(onehots):
            dk = jnp.sum(jnp.where(oh, base, 0.0), axis=-1, keepdims=True)
            dest = jnp.where(kcol == k, dk, dest)
        dest_ref[...] = dest.astype(I32)
        carry_ref[...] += colsum


def _rank(top_idx):
    n = top_idx.shape[0]
    te = 512
    return pl.pallas_call(
        functools.partial(_rank_kernel, te=te),
        grid=(2, n // te),
        in_specs=[pl.BlockSpec((te, TOP_K), lambda p, i: (i, 0))],
        out_specs=[pl.BlockSpec((te, TOP_K), lambda p, i: (p * i, 0)),
                   pl.BlockSpec((1, N_EXPERTS), lambda p, i: (0, 0))],
        out_shape=[jax.ShapeDtypeStruct((n, TOP_K), I32),
                   jax.ShapeDtypeStruct((1, N_EXPERTS), I32)],
        scratch_shapes=[pltpu.VMEM((1, N_EXPERTS), F32)],
        compiler_params=_cparams(("arbitrary", "arbitrary")),
        name="rank",
    )(top_idx)


def _dispatch_kernel(last_ref, ntile_ref, used_ref, dest_ref, h_ref, xs_ref, zero_ref, sem, zsem,
                     *, n_tiles):
    i = pl.program_id(0)

    @pl.when(i == 0)
    def _():
        zero_ref[...] = jnp.zeros_like(zero_ref)

        def clear(tile):
            rows = pl.ds(pl.multiple_of(tile * (ROW_TILE * SLAB), ROW_TILE * SLAB), ROW_TILE * SLAB)
            return pltpu.make_async_copy(zero_ref, xs_ref.at[rows], zsem)

        def over_tail(fn):
            def body(t, carry):
                fn(clear(t))
                return carry
            lax.fori_loop(used_ref[0], n_tiles, body, 0)

        for fn in (lambda c: c.start(), lambda c: c.wait()):
            for e in range(N_EXPERTS):
                @pl.when(ntile_ref[e] > 0)
                def _():
                    fn(clear(last_ref[e]))
            over_tail(fn)

    def body(r, carry):
        for k in range(TOP_K):
            d = dest_ref[r * TOP_K + k]
            pltpu.make_async_copy(h_ref.at[_slab_rows(r)], xs_ref.at[_slab_rows(d)], sem).start()
        return carry

    lax.fori_loop(0, TOK_TILE, body, 0)
    for k in range(TOP_K):
        pltpu.make_async_copy(h_ref, xs_ref.at[pl.ds(0, TOK_TILE * SLAB)], sem).wait()


def _dispatch(h2, dest_flat, last_tile, ntiles, used, n_tiles):
    n = h2.shape[0] // SLAB
    grid_spec = pltpu.PrefetchScalarGridSpec(
        num_scalar_prefetch=3,
        grid=(n // TOK_TILE,),
        in_specs=[pl.BlockSpec((TOK_TILE * TOP_K,), lambda i, *_: (i,), memory_space=pltpu.SMEM),
                  pl.BlockSpec((TOK_TILE * SLAB, LANES), lambda i, *_: (i, 0))],
        out_specs=pl.BlockSpec(memory_space=pl.ANY),
        scratch_shapes=[pltpu.VMEM((ROW_TILE * SLAB, LANES), F32),
                        pltpu.SemaphoreType.DMA(()), pltpu.SemaphoreType.DMA(())],
    )
    return pl.pallas_call(
        functools.partial(_dispatch_kernel, n_tiles=n_tiles),
        grid_spec=grid_spec,
        out_shape=jax.ShapeDtypeStruct((n_tiles * ROW_TILE * SLAB, LANES), F32),
        compiler_params=_cparams(("arbitrary",)),
        name="dispatch",
    )(last_tile, ntiles, used, dest_flat, h2)


def _experts_kernel(te_ref, used_ref, xs_ref, wgu_ref, bgu_ref, wd_ref, bd_ref, ys_ref,
                    wgu_bf, wd_bf):
    i = pl.program_id(0)
    prev = te_ref[jnp.maximum(i - 1, 0)]
    fresh = (i == 0) | (te_ref[i] != prev)

    @pl.when(fresh)
    def _():
        wgu_bf[...] = wgu_ref[...].astype(BF16)
        wd_bf[...] = wd_ref[...].astype(BF16)

    @pl.when(i < used_ref[0])
    def _():
        x = jnp.concatenate([p.astype(BF16) for p in _load_slabs(xs_ref, ROW_TILE)], axis=1)
        acc = jnp.zeros((ROW_TILE, D_MODEL), F32) + bd_ref[...]
        nch = 4
        cw = D_FF // nch
        for c in range(nch):
            gate = jnp.dot(x, wgu_bf[:, c * cw:(c + 1) * cw], preferred_element_type=F32)
            gate = jnp.minimum(gate + bgu_ref[:, c * cw:(c + 1) * cw], SWIGLU_LIMIT)
            up = jnp.dot(x, wgu_bf[:, D_FF + c * cw:D_FF + (c + 1) * cw], preferred_element_type=F32)
            up = jnp.clip(up + bgu_ref[:, D_FF + c * cw:D_FF + (c + 1) * cw], -SWIGLU_LIMIT, SWIGLU_LIMIT)
            act = gate * jax.nn.sigmoid(SWIGLU_ALPHA * gate) * (up + 1.0)
            acc = acc + jnp.dot(act.astype(BF16), wd_bf[c * cw:(c + 1) * cw, :],
                                preferred_element_type=F32)
        _store_slabs(ys_ref, acc)

    @pl.when(i >= used_ref[0])
    def _():
        ys_ref[...] = jnp.zeros_like(ys_ref)


def _experts(xs, tile_expert, used, w_gu, b_gu, w_down, b_down):
    rows = xs.shape[0] // SLAB
    nt = rows // ROW_TILE

    def tile(i, te, used):
        return (jnp.minimum(i, used[0] - 1), 0)

    grid_spec = pltpu.PrefetchScalarGridSpec(
        num_scalar_prefetch=2,
        grid=(nt,),
        in_specs=[pl.BlockSpec((ROW_TILE * SLAB, LANES), tile),
                  pl.BlockSpec((None, D_MODEL, 2 * D_FF), lambda i, te, used: (te[i], 0, 0)),
                  pl.BlockSpec((None, 1, 2 * D_FF), lambda i, te, used: (te[i], 0, 0)),
                  pl.BlockSpec((None, D_FF, D_MODEL), lambda i, te, used: (te[i], 0, 0)),
                  pl.BlockSpec((None, 1, D_MODEL), lambda i, te, used: (te[i], 0, 0))],
        out_specs=pl.BlockSpec((ROW_TILE * SLAB, LANES), lambda i, te, used: (i, 0)),
        scratch_shapes=[pltpu.VMEM((D_MODEL, 2 * D_FF), BF16), pltpu.VMEM((D_FF, D_MODEL), BF16)],
    )
    return pl.pallas_call(
        _experts_kernel,
        grid_spec=grid_spec,
        out_shape=jax.ShapeDtypeStruct((rows * SLAB, LANES), F32),
        compiler_params=_cparams(("arbitrary",)),
        name="experts",
    )(tile_expert, used, xs, w_gu, b_gu.reshape(N_EXPERTS, 1, 2 * D_FF), w_down,
      b_down.reshape(N_EXPERTS, 1, D_MODEL))


def _combine_kernel(dest_ref, ys_ref, x1_ref, p_ref, g2_ref, fn_ref, y_ref, buf_ref, sem, *, last):
    def body(r, carry):
        for k in range(TOP_K):
            d = dest_ref[r * TOP_K + k]
            pltpu.make_async_copy(ys_ref.at[_slab_rows(d)], buf_ref.at[k, _slab_rows(r)], sem).start()
        return carry

    lax.fori_loop(0, TOK_TILE, body, 0)
    for k in range(TOP_K):
        pltpu.make_async_copy(ys_ref.at[pl.ds(0, TOK_TILE * SLAB)], buf_ref.at[k], sem).wait()

    p = p_ref[...]
    groups = None
    for k in range(TOP_K):
        part = [p[:, k:k + 1] * g for g in _load_slabs(buf_ref.at[k], TOK_TILE)]
        groups = part if groups is None else [a + b for a, b in zip(groups, part)]
    moe = jnp.concatenate(groups, axis=1)
    x2 = x1_ref[...] + g2_ref[...] * moe
    if last:
        x2 = x2 * lax.rsqrt(jnp.mean(x2 * x2, axis=-1, keepdims=True) + EPS) * fn_ref[...]
    y_ref[...] = x2


def _combine(ys, dest_flat, x1, probs, mod, per_token, seq, tile0, final_norm, last):
    n = x1.shape[0]
    tm = TOK_TILE
    tps = max(seq // tm, 1)
    return pl.pallas_call(
        functools.partial(_combine_kernel, last=last),
        grid=(n // tm,),
        in_specs=[pl.BlockSpec((tm * TOP_K,), lambda i: (i + tile0,), memory_space=pltpu.SMEM),
                  pl.BlockSpec(memory_space=pl.ANY),
                  pl.BlockSpec((tm, D_MODEL), lambda i: (i, 0)),
                  pl.BlockSpec((tm, TOP_K), lambda i: (i + tile0, 0)),
                  _mod_spec(per_token, tm, tps, 5),
                  pl.BlockSpec((1, D_MODEL), lambda i: (0, 0))],
        out_specs=pl.BlockSpec((tm, D_MODEL), lambda i: (i, 0)),
        out_shape=jax.ShapeDtypeStruct((n, D_MODEL), F32),
        scratch_shapes=[pltpu.VMEM((TOP_K, tm * SLAB, LANES), F32), pltpu.SemaphoreType.DMA(())],
        compiler_params=_cparams(("arbitrary",)),
        name="combine",
    )(dest_flat, ys, x1, probs, mod, final_norm.reshape(1, D_MODEL))


def _invert_kernel(dest_ref, src0_ref, dst0_ref, src_ref, dst_ref, sem, *, n_tok):
    i = pl.program_id(0)
    blk = dest_ref.shape[0]

    @pl.when(i == 0)
    def _():
        copies = [pltpu.make_async_copy(src0_ref, src_ref, sem.at[0]),
                  pltpu.make_async_copy(dst0_ref, dst_ref, sem.at[1])]
        for cp in copies:
            cp.start()
        for cp in copies:
            cp.wait()

    def body(a, carry):
        g = i * blk + a
        d = dest_ref[a]
        tok = g >> (TOP_K.bit_length() - 1)
        src_ref[d] = tok
        dst_ref[d] = (g & (TOP_K - 1)) * n_tok + tok
        return carry

    lax.fori_loop(0, blk, body, 0, unroll=8)


def _invert(dest_flat, n_tok, n_rows):
    blk = TOK_TILE * TOP_K
    whole = pl.BlockSpec(memory_space=pltpu.SMEM)
    src0 = jnp.zeros((n_rows,), I32)
    dst0 = n_tok * TOP_K + jnp.arange(n_rows, dtype=I32)
    return pl.pallas_call(
        functools.partial(_invert_kernel, n_tok=n_tok),
        grid=(dest_flat.shape[0] // blk,),
        in_specs=[pl.BlockSpec((blk,), lambda i: (i,), memory_space=pltpu.SMEM),
                  pl.BlockSpec(memory_space=pl.ANY), pl.BlockSpec(memory_space=pl.ANY)],
        out_specs=[whole, whole],
        out_shape=[jax.ShapeDtypeStruct((n_rows,), I32), jax.ShapeDtypeStruct((n_rows,), I32)],
        scratch_shapes=[pltpu.SemaphoreType.DMA((2,))],
        compiler_params=_cparams(("arbitrary",)),
        name="invert",
    )(dest_flat, src0, dst0)


FF_CHUNKS = 4


def _moe_kernel(src_ref, dst_ref, te_ref, used_ref, h_ref, wgu_ref, bgu_ref, wd_ref, bd_ref, y_ref,
                wgu_bf, wd_bf, act_bf, buf, gsem, ssem):
    i = pl.program_id(0)
    used = used_ref[0]
    slot = i % 2
    tile_rows = ROW_TILE * SLAB
    n_tiles = te_ref.shape[0]

    def gather(tile, sl, rows):
        for r in rows:
            tok = src_ref[tile * ROW_TILE + r]
            pltpu.make_async_copy(h_ref.at[_slab_rows(tok)], buf.at[sl, _slab_rows(r)],
                                  gsem.at[sl]).start(priority=1)

    def scatter(tile, sl, rows):
        for r in rows:
            out = dst_ref[tile * ROW_TILE + r]
            pltpu.make_async_copy(buf.at[2 + sl, _slab_rows(r)], y_ref.at[_slab_rows(out)],
                                  ssem.at[sl]).start()

    def wait_gather(sl):
        pltpu.make_async_copy(h_ref.at[pl.ds(0, tile_rows)], buf.at[sl], gsem.at[sl]).wait()

    def wait_scatter(sl):
        pltpu.make_async_copy(buf.at[2 + sl], y_ref.at[pl.ds(0, tile_rows)], ssem.at[sl]).wait()

    @pl.when(i == 0)
    def _():
        buf[2] = jnp.zeros(buf.shape[1:], F32)
        buf[3] = jnp.zeros(buf.shape[1:], F32)
        tail = y_ref.shape[0] - tile_rows
        pltpu.make_async_copy(buf.at[2], y_ref.at[pl.ds(tail, tile_rows)], ssem.at[0]).start()
        gather(0, 0, range(ROW_TILE))

    prev = te_ref[jnp.maximum(i - 1, 0)]
    fresh = (i == 0) | (te_ref[jnp.minimum(i, n_tiles - 1)] != prev)

    @pl.when(fresh & (i < used))
    def _():
        wgu_bf[...] = wgu_ref[...].astype(BF16)
        wd_bf[...] = wd_ref[...].astype(BF16)

    per = ROW_TILE // FF_CHUNKS
    cw = D_FF // FF_CHUNKS

    @pl.when(i < used)
    def _():
        wait_gather(slot)
        nxt = jnp.minimum(i + 1, n_tiles - 1)
        prv = jnp.where(i >= 1, i - 1, n_tiles)
        half = per // 2
        for c in range(FF_CHUNKS):
            gather(nxt, 1 - slot, range(c * per, (c + 1) * per))
            scatter(prv, 1 - slot, range(c * half, (c + 1) * half))
            x = jnp.concatenate([p.astype(BF16) for p in _load_slabs(buf.at[slot], ROW_TILE)], axis=1)
            gate = jnp.dot(x, wgu_bf[:, c * cw:(c + 1) * cw], preferred_element_type=F32)
            gate = jnp.minimum(gate + bgu_ref[:, c * cw:(c + 1) * cw], SWIGLU_LIMIT)
            up = jnp.dot(x, wgu_bf[:, D_FF + c * cw:D_FF + (c + 1) * cw], preferred_element_type=F32)
            up = jnp.clip(up + bgu_ref[:, D_FF + c * cw:D_FF + (c + 1) * cw], -SWIGLU_LIMIT, SWIGLU_LIMIT)
            act = gate * jax.nn.sigmoid(SWIGLU_ALPHA * gate) * (up + 1.0)
            act_bf[:, c * cw:(c + 1) * cw] = act.astype(BF16)
        ow = D_MODEL // FF_CHUNKS
        base = FF_CHUNKS * half
        wait_scatter(slot)
        for c in range(FF_CHUNKS):
            scatter(prv, 1 - slot, range(base + c * half, base + (c + 1) * half))
            out = jnp.dot(act_bf[...], wd_bf[:, c * ow:(c + 1) * ow], preferred_element_type=F32)
            out = out + bd_ref[:, c * ow:(c + 1) * ow]
            for j in range(ow // LANES):
                buf[2 + slot, pl.ds(c * (ow // LANES) + j, ROW_TILE, stride=SLAB), :] = (
                    out[:, LANES * j:LANES * (j + 1)])

    @pl.when(i == used)
    def _():
        wait_gather(slot)
        wait_scatter(slot)
        scatter(i - 1, 1 - slot, range(ROW_TILE))
        wait_scatter(1 - slot)


def _moe(h2, src, dst, tile_expert, used, w_gu, b_gu, w_down, b_down, n_tok):
    n_tiles = tile_expert.shape[0]
    n_rows = n_tiles * ROW_TILE
    ex = lambda i, src, dst, te, used: (te[jnp.minimum(i, n_tiles - 1)], 0, 0)
    grid_spec = pltpu.PrefetchScalarGridSpec(
        num_scalar_prefetch=4,
        grid=(n_tiles + 1,),
        in_specs=[pl.BlockSpec(memory_space=pl.ANY),
                  pl.BlockSpec((None, D_MODEL, 2 * D_FF), ex),
                  pl.BlockSpec((None, 1, 2 * D_FF), ex),
                  pl.BlockSpec((None, D_FF, D_MODEL), ex),
                  pl.BlockSpec((None, 1, D_MODEL), ex)],
        out_specs=pl.BlockSpec(memory_space=pl.ANY),
        scratch_shapes=[pltpu.VMEM((D_MODEL, 2 * D_FF), BF16), pltpu.VMEM((D_FF, D_MODEL), BF16),
                        pltpu.VMEM((ROW_TILE, D_FF), BF16),
                        pltpu.VMEM((4, ROW_TILE * SLAB, LANES), F32),
                        pltpu.SemaphoreType.DMA((2,)), pltpu.SemaphoreType.DMA((2,))],
    )
    return pl.pallas_call(
        _moe_kernel,
        grid_spec=grid_spec,
        out_shape=jax.ShapeDtypeStruct(((n_tok * TOP_K + n_rows + 2 * ROW_TILE) * SLAB, LANES), F32),
        compiler_params=_cparams(("arbitrary",)),
        name="moe",
    )(src, dst, tile_expert, used, h2, w_gu, b_gu.reshape(N_EXPERTS, 1, 2 * D_FF), w_down,
      b_down.reshape(N_EXPERTS, 1, D_MODEL))


def _merge_kernel(y0_ref, y1_ref, y2_ref, y3_ref, x1_ref, p_ref, g2_ref, fn_ref, o_ref, *, last):
    p = p_ref[...]
    tm = p.shape[0]
    groups = None
    for k, ref in enumerate((y0_ref, y1_ref, y2_ref, y3_ref)):
        part = [p[:, k:k + 1] * g for g in _load_slabs(ref, tm)]
        groups = part if groups is None else [a + b for a, b in zip(groups, part)]
    x2 = x1_ref[...] + g2_ref[...] * jnp.concatenate(groups, axis=1)
    if last:
        x2 = x2 * lax.rsqrt(jnp.mean(x2 * x2, axis=-1, keepdims=True) + EPS) * fn_ref[...]
    o_ref[...] = x2


def _merge(y, x1, probs, mod, per_token, seq, tile0, n_tok, final_norm, last):
    n = x1.shape[0]
    tm = TOK_TILE
    tps = max(seq // tm, 1)
    choice = lambda k: pl.BlockSpec((tm * SLAB, LANES), lambda i: (k * (n_tok // tm) + tile0 + i, 0))
    return pl.pallas_call(
        functools.partial(_merge_kernel, last=last),
        grid=(n // tm,),
        in_specs=[choice(0), choice(1), choice(2), choice(3),
                  pl.BlockSpec((tm, D_MODEL), lambda i: (i, 0)),
                  pl.BlockSpec((tm, TOP_K), lambda i: (i + tile0, 0)),
                  _mod_spec(per_token, tm, tps, 5),
                  pl.BlockSpec((1, D_MODEL), lambda i: (0, 0))],
        out_specs=pl.BlockSpec((tm, D_MODEL), lambda i: (i, 0)),
        out_shape=jax.ShapeDtypeStruct((n, D_MODEL), F32),
        compiler_params=_cparams(("arbitrary",)),
        name="merge",
    )(y, y, y, y, x1, probs, mod, final_norm.reshape(1, D_MODEL))


def kernel(x_prompt, x_sample, state_hgrn, state_pool, c_prompt, c_sample, w_ada, b_ada, norm1,
           norm2, w_in, lower_bounds, hgrn_norm, w_pool, pool_scale, w_out, w_router, b_router,
           w_gu, b_gu, w_down, b_down, final_norm):
    bp, seq_p, _ = x_prompt.shape
    bs, seq_s, _ = x_sample.shape
    np_, ns = bp * seq_p, bs * seq_s
    depth = w_ada.shape[0]
    past_len = 16384

    lbs = jnp.cumsum(jax.nn.softmax(lower_bounds.astype(F32), axis=0), axis=0)
    xp = x_prompt.reshape(np_, D_MODEL)
    xs_ = x_sample.reshape(ns, D_MODEL)
    c_all = jnp.concatenate([c_prompt, c_sample], axis=0)

    sp_l, hp_l, ss_l, hs_l = [], [], [], []
    for l in range(depth):
        w_in_bf = w_in[l].astype(BF16)
        w_out_bf = w_out[l].astype(BF16)
        w_pool_bf = w_pool[l].astype(BF16)
        wr_hi = w_router[l].astype(BF16)
        wr_lo = (w_router[l] - wr_hi.astype(F32)).astype(BF16)
        lb = lbs[l].reshape(1, D_A)
        hn = hgrn_norm[l].reshape(1, D_A)
        psc = pool_scale[l].reshape(1, D_B)

        mod = _ada(c_all, w_ada[l], b_ada[l])
        mod_p = mod[:bp].reshape(bp, 1, 6 * D_MODEL)
        mod_s = jnp.repeat(mod[bp:], seq_s, axis=0)

        up = _inproj(xp, mod_p, False, seq_p, norm1[l], w_in_bf)
        us = _inproj(xs_, mod_s, True, seq_s, norm1[l], w_in_bf)
        us3 = us.reshape(bs, seq_s, D_IN)

        oa_p, s_p = _hgrn(up.reshape(bp, seq_p, D_IN), None, lb, hn)
        oa_p = oa_p.reshape(np_, D_A)
        oa_s, s_s = _hgrn(us3, state_hgrn[l], lb, hn)
        ob_p, h_p = _pool_prompt(up, bp, seq_p, w_pool_bf, psc)
        ob_t, h_t = _pool_sample(us3.transpose(1, 0, 2), state_pool[l].transpose(1, 0, 2), past_len,
                                 w_pool_bf, psc)
        ob_s, h_s = ob_t.transpose(1, 0, 2), h_t.transpose(1, 0, 2)

        x1p, h2p, idxp, prp = _outproj(xp, oa_p, ob_p, mod_p, False, seq_p, norm2[l], w_out_bf,
                                       wr_hi, wr_lo, b_router[l])
        x1s, h2s, idxs, prs = _outproj(xs_, oa_s.reshape(ns, D_A), ob_s.reshape(ns, D_B), mod_s,
                                       True, seq_s, norm2[l], w_out_bf, wr_hi, wr_lo, b_router[l])

        h2 = jnp.concatenate([h2p, h2s], axis=0)
        top_idx = jnp.concatenate([idxp, idxs], axis=0)
        probs = jnp.concatenate([prp, prs], axis=0)
        n_tok = np_ + ns
        n_tiles = (n_tok * TOP_K) // ROW_TILE + N_EXPERTS

        dest, counts = _rank(top_idx)
        dest_flat = dest.reshape(n_tok * TOP_K)
        tiles_e = (counts.reshape(N_EXPERTS) + ROW_TILE - 1) // ROW_TILE
        ends = jnp.cumsum(tiles_e)
        used = ends[-1:].astype(I32)
        tile_expert = jnp.minimum(
            jnp.sum(jnp.arange(n_tiles, dtype=I32)[:, None] >= ends[None, :], axis=1),
            N_EXPERTS - 1).astype(I32)

        src, dst = _invert(dest_flat, n_tok, (n_tiles + 1) * ROW_TILE)
        y = _moe(h2, src, dst, tile_expert, used, w_gu[l], b_gu[l], w_down[l], b_down[l], n_tok)

        last = l == depth - 1
        xp = _merge(y, x1p, probs, mod_p, False, seq_p, 0, n_tok, final_norm, last)
        xs_ = _merge(y, x1s, probs, mod_s, True, seq_s, np_ // TOK_TILE, n_tok, final_norm, last)

        sp_l.append(s_p)
        hp_l.append(h_p)
        ss_l.append(s_s)
        hs_l.append(h_s)

    return (xp.reshape(bp, seq_p, D_MODEL), xs_.reshape(bs, seq_s, D_MODEL),
            jnp.stack(sp_l), jnp.stack(hp_l), jnp.stack(ss_l), jnp.stack(hs_l))
```

```python
import functools

import jax
import jax.numpy as jnp
from jax import lax
from jax.experimental import pallas as pl
from jax.experimental.pallas import tpu as pltpu

F32, BF16, I32 = jnp.float32, jnp.bfloat16, jnp.int32

D_MODEL = 1024
D_A = 512
H_A = 4
DK = 128
DV = 128
D_B = 512
POOL_WINDOWS = (2, 4, 8, 16)
POOL_GROUP = 128
POOL_HIST = 15
D_IN = 4 * D_A + D_B
N_EXPERTS = 32
TOP_K = 4
D_FF = 1024
SWIGLU_LIMIT = 7.0
SWIGLU_ALPHA = 1.702
EPS = 1e-6

SUB = 16
ROW_TILE = 256
TOK_TILE = 256
VMEM_LIMIT = 56 * 1024 * 1024


def _cparams(sem):
    return pltpu.CompilerParams(dimension_semantics=sem, vmem_limit_bytes=VMEM_LIMIT)


def _silu(x):
    return x * jax.nn.sigmoid(x)


def _mod_spec(per_token, tm, tiles_per_seq, col):
    if per_token:
        return pl.BlockSpec((tm, D_MODEL), lambda i: (i, col))
    return pl.BlockSpec((None, 1, D_MODEL), lambda i: (i // tiles_per_seq, 0, col))


def _ada_kernel(c_ref, w_ref, b_ref, o_ref):
    c = c_ref[...]
    o_ref[...] = jnp.dot(_silu(c).astype(BF16), w_ref[...].astype(BF16),
                         preferred_element_type=F32) + b_ref[...]


def _ada(c_all, w_ada, b_ada):
    rows = c_all.shape[0]
    n = w_ada.shape[1]
    return pl.pallas_call(
        _ada_kernel,
        grid=(n // D_MODEL,),
        in_specs=[pl.BlockSpec((rows, D_MODEL), lambda j: (0, 0)),
                  pl.BlockSpec((D_MODEL, D_MODEL), lambda j: (0, j)),
                  pl.BlockSpec((1, D_MODEL), lambda j: (0, j))],
        out_specs=pl.BlockSpec((rows, D_MODEL), lambda j: (0, j)),
        out_shape=jax.ShapeDtypeStruct((rows, n), F32),
        compiler_params=_cparams(("arbitrary",)),
        name="ada",
    )(c_all, w_ada, b_ada.reshape(1, n))


def _inproj_kernel(x_ref, sh_ref, sc_ref, n1_ref, w_ref, u_ref):
    x = x_ref[...]
    h = x * lax.rsqrt(jnp.mean(x * x, axis=-1, keepdims=True) + EPS) * n1_ref[...]
    h = h * (1.0 + sc_ref[...]) + sh_ref[...]
    u_ref[...] = jnp.dot(h.astype(BF16), w_ref[...], preferred_element_type=F32)


def _inproj(x, mod, per_token, seq, norm1, w_in_bf):
    n = x.shape[0]
    tm = min(512, n)
    tps = max(seq // tm, 1)
    return pl.pallas_call(
        _inproj_kernel,
        grid=(n // tm,),
        in_specs=[pl.BlockSpec((tm, D_MODEL), lambda i: (i, 0)),
                  _mod_spec(per_token, tm, tps, 0),
                  _mod_spec(per_token, tm, tps, 1),
                  pl.BlockSpec((1, D_MODEL), lambda i: (0, 0)),
                  pl.BlockSpec((D_MODEL, D_IN), lambda i: (0, 0))],
        out_specs=pl.BlockSpec((tm, D_IN), lambda i: (i, 0)),
        out_shape=jax.ShapeDtypeStruct((n, D_IN), F32),
        compiler_params=_cparams(("arbitrary",)),
        name="inproj",
    )(x, mod, mod, norm1.reshape(1, D_MODEL), w_in_bf)


def _block_select():
    r = lax.broadcasted_iota(I32, (SUB, SUB * DK), 0)
    c = lax.broadcasted_iota(I32, (SUB, SUB * DK), 1) >> (DK.bit_length() - 1)
    return (r == c).astype(BF16)


def _hgrn_block(q, fp, iv, g, lb, hn, st_ref, sel_ref, valid):
    f = lb + (1.0 - lb) * jax.nn.sigmoid(fp)
    logf = jnp.log(f)
    kk = 1.0 - f
    if valid < SUB:
        live = lax.broadcasted_iota(I32, (SUB, D_A), 0) < valid
        logf = jnp.where(live, logf, 0.0)
        kk = jnp.where(live, kk, 0.0)
    qa = _silu(q)

    r = lax.broadcasted_iota(I32, (SUB, SUB), 0)
    c = lax.broadcasted_iota(I32, (SUB, SUB), 1)
    tri = (r >= c).astype(BF16)
    hi = logf.astype(BF16)
    lo = (logf - hi.astype(F32)).astype(BF16)
    b = (jnp.dot(tri, hi, preferred_element_type=F32)
         + jnp.dot(tri, lo, preferred_element_type=F32))
    bend = b[SUB - 1:SUB, :]
    qt = (qa * jnp.exp(b)).astype(BF16)
    kh = (kk * jnp.exp(bend - b)).astype(BF16)
    dec = jnp.exp(bend)
    ivb = iv.astype(BF16)

    rows = []
    for h in range(H_A):
        sl = slice(DK * h, DK * (h + 1))
        bh, qh, kkh = b[:, sl], qa[:, sl], kk[:, sl]
        ps = []
        for s in range(SUB):
            e = jnp.exp(jnp.minimum(bh - bh[s:s + 1, :], 0.0))
            ps.append((e * (qh * kkh[s:s + 1, :])).astype(BF16))
        rows.append(jnp.concatenate(ps, axis=1))
    pcat = jnp.concatenate(rows, axis=0)
    a = lax.dot_general(pcat, sel_ref[...], (((1,), (1,)), ((), ())),
                        preferred_element_type=F32)
    tr = lax.broadcasted_iota(I32, (H_A * SUB, SUB), 0) & (SUB - 1)
    tc = lax.broadcasted_iota(I32, (H_A * SUB, SUB), 1)
    a = jnp.where(tr >= tc, a, 0.0).astype(BF16)

    outs = []
    for h in range(H_A):
        sl = slice(DK * h, DK * (h + 1))
        st = st_ref[h]
        o = lax.dot_general(qt[:, sl], st.astype(BF16), (((1,), (1,)), ((), ())),
                            preferred_element_type=F32)
        o = o + jnp.dot(a[SUB * h:SUB * (h + 1), :], ivb[:, sl], preferred_element_type=F32)
        ut = lax.dot_general(ivb[:, sl], kh[:, sl], (((0,), (0,)), ((), ())),
                             preferred_element_type=F32)
        st_ref[h] = st * dec[:, sl] + ut
        o = o * lax.rsqrt(jnp.mean(o * o, axis=-1, keepdims=True) + EPS) * hn[:, sl]
        outs.append(o)
    return jnp.concatenate(outs, axis=1) * _silu(g)


def _hgrn_prompt_kernel(q_ref, f_ref, i_ref, g_ref, lb_ref, hn_ref, o_ref, s_ref, st_ref, sel_ref,
                        *, tt):
    t = pl.program_id(1)

    @pl.when(t == 0)
    def _():
        st_ref[...] = jnp.zeros_like(st_ref)
        sel_ref[...] = _block_select()

    lb = lb_ref[...]
    hn = hn_ref[...]

    def body(j, carry):
        rs = pl.ds(pl.multiple_of(j * SUB, SUB), SUB)
        o_ref[rs, :] = _hgrn_block(q_ref[rs, :], f_ref[rs, :], i_ref[rs, :], g_ref[rs, :],
                                   lb, hn, st_ref, sel_ref, SUB)
        return carry

    lax.fori_loop(0, tt // SUB, body, 0)

    @pl.when(t == pl.num_programs(1) - 1)
    def _():
        for h in range(H_A):
            s_ref[h] = st_ref[h].T


def _hgrn_prompt(u, batch, seq, lb, hn):
    tt = 256
    nt = seq // tt

    def col(cb):
        return pl.BlockSpec((tt, D_A), lambda b, t: (b * nt + t, cb))

    return pl.pallas_call(
        functools.partial(_hgrn_prompt_kernel, tt=tt),
        grid=(batch, nt),
        in_specs=[col(0), col(1), col(2), col(3),
                  pl.BlockSpec((1, D_A), lambda b, t: (0, 0)),
                  pl.BlockSpec((1, D_A), lambda b, t: (0, 0))],
        out_specs=[pl.BlockSpec((tt, D_A), lambda b, t: (b * nt + t, 0)),
                   pl.BlockSpec((None, H_A, DK, DV), lambda b, t: (b, 0, 0, 0))],
        out_shape=[jax.ShapeDtypeStruct((batch * seq, D_A), F32),
                   jax.ShapeDtypeStruct((batch, H_A, DK, DV), F32)],
        scratch_shapes=[pltpu.VMEM((H_A, DV, DK), F32), pltpu.VMEM((SUB, SUB * DK), BF16)],
        compiler_params=_cparams(("arbitrary", "arbitrary")),
        name="hgrn_prompt",
    )(u, u, u, u, lb, hn)


def _hgrn_sample_kernel(q_ref, f_ref, i_ref, g_ref, lb_ref, hn_ref, s0_ref, o_ref, s_ref,
                        st_ref, sel_ref, pad_ref, *, seq):
    for h in range(H_A):
        st_ref[h] = s0_ref[h].T
    sel_ref[...] = _block_select()
    pad_ref[...] = jnp.zeros_like(pad_ref)
    for n, ref in enumerate((q_ref, f_ref, i_ref, g_ref)):
        pad_ref[n, 0:seq, :] = ref[...]
    o = _hgrn_block(pad_ref[0], pad_ref[1], pad_ref[2], pad_ref[3], lb_ref[...], hn_ref[...],
                    st_ref, sel_ref, seq)
    o_ref[...] = o[0:seq, :]
    for h in range(H_A):
        s_ref[h] = st_ref[h].T


def _hgrn_sample(u3, s0, lb, hn):
    batch, seq, _ = u3.shape

    def col(cb):
        return pl.BlockSpec((None, seq, D_A), lambda b: (b, 0, cb))

    return pl.pallas_call(
        functools.partial(_hgrn_sample_kernel, seq=seq),
        grid=(batch,),
        in_specs=[col(0), col(1), col(2), col(3),
                  pl.BlockSpec((1, D_A), lambda b: (0, 0)),
                  pl.BlockSpec((1, D_A), lambda b: (0, 0)),
                  pl.BlockSpec((None, H_A, DK, DV), lambda b: (b, 0, 0, 0))],
        out_specs=[pl.BlockSpec((None, seq, D_A), lambda b: (b, 0, 0)),
                   pl.BlockSpec((None, H_A, DK, DV), lambda b: (b, 0, 0, 0))],
        out_shape=[jax.ShapeDtypeStruct((batch, seq, D_A), F32),
                   jax.ShapeDtypeStruct((batch, H_A, DK, DV), F32)],
        scratch_shapes=[pltpu.VMEM((H_A, DV, DK), F32), pltpu.VMEM((SUB, SUB * DK), BF16),
                        pltpu.VMEM((4, SUB, D_A), F32)],
        compiler_params=_cparams(("arbitrary",)),
        name="hgrn_sample",
    )(u3, u3, u3, u3, lb, hn, s0)


HGRN_SEQS = 8
HALF = SUB // 2


def _hgrn_step(q, fp, iv, g, lb, hn, st_ref, sel_ref, cm_ref, b_ref, k_ref, valid):
    nrow = HGRN_SEQS * SUB
    f = lb + (1.0 - lb) * jax.nn.sigmoid(fp)
    lf = jnp.log2(f)
    kk = 1.0 - f
    if valid < SUB:
        live = (lax.broadcasted_iota(I32, (nrow, D_A), 0) & (SUB - 1)) < valid
        lf = jnp.where(live, lf, 0.0)
        kk = jnp.where(live, kk, 0.0)
    qa = _silu(q)

    hi = lf.astype(BF16)
    lo = (lf - hi.astype(F32)).astype(BF16)
    cum, tot = cm_ref[0], cm_ref[1]
    b = jnp.dot(cum, hi, preferred_element_type=F32) + jnp.dot(cum, lo, preferred_element_type=F32)
    be = jnp.dot(tot, hi, preferred_element_type=F32) + jnp.dot(tot, lo, preferred_element_type=F32)
    qt = (qa * jnp.exp2(b)).astype(BF16)
    kh = (kk * jnp.exp2(be - b)).astype(BF16)
    dec = jnp.exp2(be)
    ivb = iv.astype(BF16)
    for h in range(H_A):
        b_ref[h] = b[:, DK * h:DK * (h + 1)]
        k_ref[h] = kk[:, DK * h:DK * (h + 1)]

    zero_half = jnp.zeros((HALF, D_A), F32)
    blocks = []
    for n in range(HGRN_SEQS):
        per_s = []
        for s in range(SUB):
            row = n * SUB + s
            b_s = jnp.concatenate([b_ref[h, pl.ds(row, HALF, stride=0), :] for h in range(H_A)], axis=1)
            k_s = jnp.concatenate([k_ref[h, pl.ds(row, HALF, stride=0), :] for h in range(H_A)], axis=1)
            halves = []
            for hf in range(2):
                if hf == 0 and s >= HALF:
                    halves.append(zero_half)
                    continue
                r0 = n * SUB + hf * HALF
                e = jnp.exp2(jnp.minimum(b[r0:r0 + HALF] - b_s, 0.0))
                halves.append(e * (qa[r0:r0 + HALF] * k_s))
            per_s.append(jnp.concatenate(halves, axis=0).astype(BF16))
        for h in range(H_A):
            blocks.append(jnp.concatenate([p[:, DK * h:DK * (h + 1)] for p in per_s], axis=1))
    pcat = jnp.concatenate(blocks, axis=0)
    a = lax.dot_general(pcat, sel_ref[...], (((1,), (1,)), ((), ())),
                        preferred_element_type=F32)
    tr = lax.broadcasted_iota(I32, a.shape, 0) & (SUB - 1)
    tc = lax.broadcasted_iota(I32, a.shape, 1)
    a = jnp.where(tr >= tc, a, 0.0).astype(BF16)

    outs = []
    for n in range(HGRN_SEQS):
        rows = slice(n * SUB, (n + 1) * SUB)
        heads = []
        for h in range(H_A):
            sl = slice(DK * h, DK * (h + 1))
            st = st_ref[n, h]
            o = lax.dot_general(qt[rows, sl], st.astype(BF16), (((1,), (1,)), ((), ())),
                                preferred_element_type=F32)
            ar = (n * H_A + h) * SUB
            o = o + jnp.dot(a[ar:ar + SUB, :], ivb[rows, sl], preferred_element_type=F32)
            ut = lax.dot_general(ivb[rows, sl], kh[rows, sl], (((0,), (0,)), ((), ())),
                                 preferred_element_type=F32)
            dtile = dec[n * SUB:n * SUB + HALF, sl]
            st_ref[n, h] = (st.reshape(DV // HALF, HALF, DK) * dtile[None]).reshape(DV, DK) + ut
            o = o * lax.rsqrt(jnp.mean(o * o, axis=-1, keepdims=True) + EPS) * hn[:, sl]
            heads.append(o)
        outs.append(jnp.concatenate(heads, axis=1))
    return jnp.concatenate(outs, axis=0) * _silu(g)


def _hgrn_kernel(*refs, rows, has_s0):
    if has_s0:
        (q_ref, f_ref, i_ref, g_ref, lb_ref, hn_ref, s0_ref, o_ref, s_ref,
         st_ref, sel_ref, cm_ref, b_ref, k_ref, pad_ref) = refs
    else:
        (q_ref, f_ref, i_ref, g_ref, lb_ref, hn_ref, o_ref, s_ref,
         st_ref, sel_ref, cm_ref, b_ref, k_ref) = refs
    t = pl.program_id(0)
    nrow = HGRN_SEQS * SUB

    @pl.when(t == 0)
    def _():
        sel_ref[...] = _block_select()
        r = lax.broadcasted_iota(I32, (nrow, nrow), 0)
        c = lax.broadcasted_iota(I32, (nrow, nrow), 1)
        shift = SUB.bit_length() - 1
        same = (r >> shift) == (c >> shift)
        cm_ref[0] = (same & (r >= c)).astype(BF16)
        cm_ref[1] = same.astype(BF16)

    lb = lb_ref[...]
    hn = hn_ref[...]
    if has_s0:
        for n in range(HGRN_SEQS):
            for h in range(H_A):
                st_ref[n, h] = s0_ref[n, h].T
        pad_ref[...] = jnp.zeros_like(pad_ref)
        for j, ref in enumerate((q_ref, f_ref, i_ref, g_ref)):
            pad_ref[j, :, 0:rows, :] = ref[...]
        o = _hgrn_step(*(pad_ref[j].reshape(nrow, D_A) for j in range(4)), lb, hn,
                       st_ref, sel_ref, cm_ref, b_ref, k_ref, rows)
        o_ref[...] = o.reshape(HGRN_SEQS, SUB, D_A)[:, 0:rows, :]
    else:
        @pl.when(t == 0)
        def _():
            st_ref[...] = jnp.zeros_like(st_ref)

        def body(j, carry):
            rs = pl.ds(pl.multiple_of(j * SUB, SUB), SUB)
            o = _hgrn_step(*(ref[:, rs, :].reshape(nrow, D_A) for ref in (q_ref, f_ref, i_ref, g_ref)),
                           lb, hn, st_ref, sel_ref, cm_ref, b_ref, k_ref, SUB)
            o_ref[:, rs, :] = o.reshape(HGRN_SEQS, SUB, D_A)
            return carry

        lax.fori_loop(0, rows // SUB, body, 0)

    def emit_state():
        for n in range(HGRN_SEQS):
            for h in range(H_A):
                s_ref[n, h] = st_ref[n, h].T

    if has_s0:
        emit_state()
    else:
        pl.when(t == pl.num_programs(0) - 1)(emit_state)


def _hgrn(u3, s0, lb, hn):
    batch, seq, _ = u3.shape
    has_s0 = s0 is not None
    if has_s0:
        rows, grid = seq, (batch // HGRN_SEQS,)
        blk = lambda cb: pl.BlockSpec((HGRN_SEQS, rows, D_A), lambda i: (i, 0, cb))
        sblk = pl.BlockSpec((HGRN_SEQS, H_A, DK, DV), lambda i: (i, 0, 0, 0))
    else:
        rows, grid = 128, (seq // 128,)
        blk = lambda cb: pl.BlockSpec((HGRN_SEQS, rows, D_A), lambda i: (0, i, cb))
        sblk = pl.BlockSpec((HGRN_SEQS, H_A, DK, DV), lambda i: (0, 0, 0, 0))
    vec = pl.BlockSpec((1, D_A), lambda i: (0, 0))
    nrow = HGRN_SEQS * SUB
    scratch = [pltpu.VMEM((HGRN_SEQS, H_A, DV, DK), F32), pltpu.VMEM((SUB, SUB * DK), BF16),
               pltpu.VMEM((2, nrow, nrow), BF16), pltpu.VMEM((H_A, nrow, DK), F32),
               pltpu.VMEM((H_A, nrow, DK), F32)]
    if has_s0:
        scratch.append(pltpu.VMEM((4, HGRN_SEQS, SUB, D_A), F32))
    args = (u3, u3, u3, u3, lb, hn) + ((s0,) if has_s0 else ())
    return pl.pallas_call(
        functools.partial(_hgrn_kernel, rows=rows, has_s0=has_s0),
        grid=grid,
        in_specs=[blk(0), blk(1), blk(2), blk(3), vec, vec] + ([sblk] if has_s0 else []),
        out_specs=[blk(0), sblk],
        out_shape=[jax.ShapeDtypeStruct((batch, seq, D_A), F32),
                   jax.ShapeDtypeStruct((batch, H_A, DK, DV), F32)],
        scratch_shapes=scratch,
        compiler_params=_cparams(("arbitrary",)),
        name="hgrn_state" if has_s0 else "hgrn_scan",
    )(*args)


def _pool_groups(full_ref, base, tm, cnt_fn, wp_ref, ps_ref, step=1):
    outs = []
    for gi, w in enumerate(POOL_WINDOWS):
        ch = slice(POOL_GROUP * gi, POOL_GROUP * (gi + 1))
        v = full_ref[base:base + tm, ch]
        s = v
        for j in range(1, w):
            s = s + full_ref[base - j * step:base - j * step + tm, ch]
        d = s / cnt_fn(w) - v
        outs.append(jnp.dot(d.astype(BF16), wp_ref[gi], preferred_element_type=F32))
    return jnp.concatenate(outs, axis=1) * ps_ref[...]


def _pool_prompt_kernel(v_ref, prev_ref, wp_ref, ps_ref, o_ref, hist_ref, full_ref, *, tm):
    t = pl.program_id(1)
    prev = prev_ref[...]
    full_ref[0:16, :] = jnp.where(t == 0, 0.0, prev)
    full_ref[16:16 + tm, :] = v_ref[...]
    pos = (lax.broadcasted_iota(I32, (tm, 1), 0) + t * tm + 1).astype(F32)
    o_ref[...] = _pool_groups(full_ref, 16, tm, lambda w: jnp.minimum(pos, float(w)),
                              wp_ref, ps_ref)

    @pl.when(t == pl.num_programs(1) - 1)
    def _():
        hist_ref[...] = full_ref[16 + tm - POOL_HIST:16 + tm, :]


def _pool_prompt(u, batch, seq, w_pool_bf, pool_scale):
    tm = 256
    nt = seq // tm
    vcol = (4 * D_A) // D_B
    return pl.pallas_call(
        functools.partial(_pool_prompt_kernel, tm=tm),
        grid=(batch, nt),
        in_specs=[pl.BlockSpec((tm, D_B), lambda b, t: (b * nt + t, vcol)),
                  pl.BlockSpec((16, D_B),
                               lambda b, t: (jnp.maximum((b * nt + t) * (tm // 16) - 1, 0), vcol)),
                  pl.BlockSpec((len(POOL_WINDOWS), POOL_GROUP, POOL_GROUP), lambda b, t: (0, 0, 0)),
                  pl.BlockSpec((1, D_B), lambda b, t: (0, 0))],
        out_specs=[pl.BlockSpec((tm, D_B), lambda b, t: (b * nt + t, 0)),
                   pl.BlockSpec((None, POOL_HIST, D_B), lambda b, t: (b, 0, 0))],
        out_shape=[jax.ShapeDtypeStruct((batch * seq, D_B), F32),
                   jax.ShapeDtypeStruct((batch, POOL_HIST, D_B), F32)],
        scratch_shapes=[pltpu.VMEM((16 + tm, D_B), F32)],
        compiler_params=_cparams(("arbitrary", "arbitrary")),
        name="pool_prompt",
    )(u, u, w_pool_bf, pool_scale)


def _pool_sample_kernel(v_ref, hist_ref, wp_ref, ps_ref, o_ref, nh_ref, full_ref, *,
                        seq, batch, start):
    full_ref[0:batch, :] = jnp.zeros((batch, D_B), F32)
    full_ref[batch:16 * batch, :] = hist_ref[...].reshape(POOL_HIST * batch, D_B)
    full_ref[16 * batch:(16 + seq) * batch, :] = v_ref[...].reshape(seq * batch, D_B)

    def cnt(w):
        rows = [jnp.full((batch, 1), float(min(start + t + 1, w)), F32) for t in range(seq)]
        return jnp.concatenate(rows, axis=0)

    o = _pool_groups(full_ref, 16 * batch, seq * batch, cnt, wp_ref, ps_ref, step=batch)
    o_ref[...] = o.reshape(seq, batch, D_B)
    nh_ref[...] = full_ref[(16 + seq - POOL_HIST) * batch:(16 + seq) * batch, :].reshape(
        POOL_HIST, batch, D_B)


def _pool_sample(u_t, hist_t, start, w_pool_bf, pool_scale):
    seq, batch, _ = u_t.shape
    vcol = (4 * D_A) // D_B
    return pl.pallas_call(
        functools.partial(_pool_sample_kernel, seq=seq, batch=batch, start=start),
        grid=(1,),
        in_specs=[pl.BlockSpec((seq, batch, D_B), lambda i: (0, 0, vcol)),
                  pl.BlockSpec((POOL_HIST, batch, D_B), lambda i: (0, 0, 0)),
                  pl.BlockSpec((len(POOL_WINDOWS), POOL_GROUP, POOL_GROUP), lambda i: (0, 0, 0)),
                  pl.BlockSpec((1, D_B), lambda i: (0, 0))],
        out_specs=[pl.BlockSpec((seq, batch, D_B), lambda i: (0, 0, 0)),
                   pl.BlockSpec((POOL_HIST, batch, D_B), lambda i: (0, 0, 0))],
        out_shape=[jax.ShapeDtypeStruct((seq, batch, D_B), F32),
                   jax.ShapeDtypeStruct((POOL_HIST, batch, D_B), F32)],
        scratch_shapes=[pltpu.VMEM(((16 + seq) * batch, D_B), F32)],
        compiler_params=_cparams(("arbitrary",)),
        name="pool_sample",
    )(u_t, hist_t, w_pool_bf, pool_scale)


LANES = 128
SLAB = D_MODEL // LANES


def _store_slabs(ref, value):
    n = value.shape[0]
    for j in range(SLAB):
        ref[pl.ds(j, n, stride=SLAB), :] = value[:, LANES * j:LANES * (j + 1)]


def _load_slabs(ref, n):
    return [ref[pl.ds(j, n, stride=SLAB), :] for j in range(SLAB)]


def _slab_rows(row):
    return pl.ds(pl.multiple_of(row * SLAB, SLAB), SLAB)


def _split_dot(a, w_hi, w_lo):
    a_hi = a.astype(BF16)
    a_lo = (a - a_hi.astype(F32)).astype(BF16)
    return (jnp.dot(a_hi, w_hi, preferred_element_type=F32)
            + jnp.dot(a_hi, w_lo, preferred_element_type=F32)
            + jnp.dot(a_lo, w_hi, preferred_element_type=F32))


def _outproj_kernel(x_ref, oa_ref, ob_ref, g1_ref, sh2_ref, sc2_ref, n2_ref, woa_ref, wob_ref,
                    wrh_ref, wrl_ref, br_ref, *rest):
    x1_ref, h2_ref, idx_ref, prob_ref = rest[-4:]
    mix =(jnp.dot(oa_ref[...].astype(BF16), woa_ref[...], preferred_element_type=F32)
           + jnp.dot(ob_ref[...].astype(BF16), wob_ref[...], preferred_element_type=F32))
    x1 = x_ref[...] + g1_ref[...] * mix
    x1_ref[...] = x1
    h2 = x1 * lax.rsqrt(jnp.mean(x1 * x1, axis=-1, keepdims=True) + EPS) * n2_ref[...]
    h2 = h2 * (1.0 + sc2_ref[...]) + sh2_ref[...]
    _store_slabs(h2_ref, h2)

    logits = _split_dot(h2, wrh_ref[...], wrl_ref[...]) + br_ref[...]
    tm = logits.shape[0]
    lane = lax.broadcasted_iota(I32, (tm, N_EXPERTS), 1)
    kcol = lax.broadcasted_iota(I32, (tm, TOP_K), 1)
    vals = jnp.zeros((tm, TOP_K), F32)
    idxs = jnp.zeros((tm, TOP_K), I32)
    for k in range(TOP_K):
        m = jnp.max(logits, axis=-1, keepdims=True)
        am = jnp.min(jnp.where(logits == m, lane, N_EXPERTS), axis=-1, keepdims=True)
        vals = jnp.where(kcol == k, m, vals)
        idxs = jnp.where(kcol == k, am, idxs)
        logits = jnp.where(lane == am, -jnp.inf, logits)
    e = jnp.exp(vals - vals[:, 0:1])
    prob_ref[...] = e / jnp.sum(e, axis=-1, keepdims=True)
    idx_ref[...] = idxs


def _outproj(x, o_a, o_b, mod, per_token, seq, norm2, w_out_bf, wr_hi, wr_lo, b_router,
             n_total, tile0, joint=None):
    n = x.shape[0]
    tm = min(256, n)
    tps = max(seq // tm, 1)
    row = lambda width: pl.BlockSpec((tm, width), lambda i: (i, 0))
    jrow = lambda width: pl.BlockSpec((tm, width), lambda i: (i + tile0, 0))
    const = lambda shape: pl.BlockSpec(shape, lambda i: (0,) * len(shape))
    joint = () if joint is None else tuple(joint)
    n_in = 12
    return pl.pallas_call(
        _outproj_kernel,
        grid=(n // tm,),
        in_specs=[row(D_MODEL), row(D_A), row(D_B),
                  _mod_spec(per_token, tm, tps, 2),
                  _mod_spec(per_token, tm, tps, 3),
                  _mod_spec(per_token, tm, tps, 4),
                  const((1, D_MODEL)),
                  pl.BlockSpec((D_A, D_MODEL), lambda i: (0, 0)),
                  pl.BlockSpec((D_B, D_MODEL), lambda i: (1, 0)),
                  const((D_MODEL, N_EXPERTS)), const((D_MODEL, N_EXPERTS)), const((1, N_EXPERTS))]
                 + [pl.BlockSpec(memory_space=pl.ANY)] * len(joint),
        out_specs=[row(D_MODEL), pl.BlockSpec((tm * SLAB, LANES), lambda i: (i + tile0, 0)),
                   jrow(TOP_K), jrow(TOP_K)],
        out_shape=[jax.ShapeDtypeStruct((n, D_MODEL), F32),
                   jax.ShapeDtypeStruct((n_total * SLAB, LANES), F32),
                   jax.ShapeDtypeStruct((n_total, TOP_K), I32),
                   jax.ShapeDtypeStruct((n_total, TOP_K), F32)],
        input_output_aliases={n_in + j: 1 + j for j in range(len(joint))},
        compiler_params=_cparams(("arbitrary",)),
        name="outproj",
    )(x, o_a, o_b, mod, mod, mod, norm2.reshape(1, D_MODEL), w_out_bf, w_out_bf,
      wr_hi, wr_lo, b_router.reshape(1, N_EXPERTS), *joint)


def _rank_kernel(idx_ref, dest_ref, cnt_ref, carry_ref, *, te):
    phase = pl.program_id(0)
    i = pl.program_id(1)
    idx = idx_ref[...]
    lane = lax.broadcasted_iota(I32, (te, N_EXPERTS), 1)
    onehots = [(idx[:, k:k + 1] == lane) for k in range(TOP_K)]
    member = jnp.zeros((te, N_EXPERTS), F32)
    for oh in onehots:
        member = member + oh.astype(F32)
    colsum = jnp.sum(member, axis=0, keepdims=True)

    @pl.when((phase == 0) & (i == 0))
    def _():
        carry_ref[...] = jnp.zeros_like(carry_ref)

    @pl.when(phase == 0)
    def _():
        carry_ref[...] += colsum

    @pl.when((phase == 1) & (i == 0))
    def _():
        cnt = carry_ref[...]
        cnt_ref[...] = cnt.astype(I32)
        tiles = jnp.floor((cnt + (ROW_TILE - 1)) * (1.0 / ROW_TILE))
        r = lax.broadcasted_iota(I32, (N_EXPERTS, N_EXPERTS), 0)
        c = lax.broadcasted_iota(I32, (N_EXPERTS, N_EXPERTS), 1)
        before = (r < c).astype(BF16)
        carry_ref[...] = ROW_TILE * jnp.dot(tiles.astype(BF16), before, preferred_element_type=F32)

    @pl.when(phase == 1)
    def _():
        r = lax.broadcasted_iota(I32, (te, te), 0)
        c = lax.broadcasted_iota(I32, (te, te), 1)
        earlier = (r > c).astype(BF16)
        base = carry_ref[...] + jnp.dot(earlier, member.astype(BF16), preferred_element_type=F32)
        kcol = lax.broadcasted_iota(I32, (te, TOP_K), 1)
        dest = jnp.zeros((te, TOP_K), F32)
        for k, oh in enumerate(onehots):
            dk = jnp.sum(jnp.where(oh, base, 0.0), axis=-1, keepdims=True)
            dest = jnp.where(kcol == k, dk, dest)
        dest_ref[...] = dest.astype(I32)
        carry_ref[...] += colsum


def _rank(top_idx):
    n = top_idx.shape[0]
    te = 512
    return pl.pallas_call(
        functools.partial(_rank_kernel, te=te),
        grid=(2, n // te),
        in_specs=[pl.BlockSpec((te, TOP_K), lambda p, i: (i, 0))],
        out_specs=[pl.BlockSpec((te, TOP_K), lambda p, i: (p * i, 0)),
                   pl.BlockSpec((1, N_EXPERTS), lambda p, i: (0, 0))],
        out_shape=[jax.ShapeDtypeStruct((n, TOP_K), I32),
                   jax.ShapeDtypeStruct((1, N_EXPERTS), I32)],
        scratch_shapes=[pltpu.VMEM((1, N_EXPERTS), F32)],
        compiler_params=_cparams(("arbitrary", "arbitrary")),
        name="rank",
    )(top_idx)


def _dispatch_kernel(last_ref, ntile_ref, used_ref, dest_ref, h_ref, xs_ref, zero_ref, sem, zsem,
                     *, n_tiles):
    i = pl.program_id(0)

    @pl.when(i == 0)
    def _():
        zero_ref[...] = jnp.zeros_like(zero_ref)

        def clear(tile):
            rows = pl.ds(pl.multiple_of(tile * (ROW_TILE * SLAB), ROW_TILE * SLAB), ROW_TILE * SLAB)
            return pltpu.make_async_copy(zero_ref, xs_ref.at[rows], zsem)

        def over_tail(fn):
            def body(t, carry):
                fn(clear(t))
                return carry
            lax.fori_loop(used_ref[0], n_tiles, body, 0)

        for fn in (lambda c: c.start(), lambda c: c.wait()):
            for e in range(N_EXPERTS):
                @pl.when(ntile_ref[e] > 0)
                def _():
                    fn(clear(last_ref[e]))
            over_tail(fn)

    def body(r, carry):
        for k in range(TOP_K):
            d = dest_ref[r * TOP_K + k]
            pltpu.make_async_copy(h_ref.at[_slab_rows(r)], xs_ref.at[_slab_rows(d)], sem).start(
                priority=k % 2)
        return carry

    lax.fori_loop(0, TOK_TILE, body, 0, unroll=2)
    for k in range(TOP_K):
        pltpu.make_async_copy(h_ref, xs_ref.at[pl.ds(0, TOK_TILE * SLAB)], sem).wait()


def _dispatch(h2, dest_flat, last_tile, ntiles, used, n_tiles):
    n = h2.shape[0] // SLAB
    grid_spec = pltpu.PrefetchScalarGridSpec(
        num_scalar_prefetch=3,
        grid=(n // TOK_TILE,),
        in_specs=[pl.BlockSpec((TOK_TILE * TOP_K,), lambda i, *_: (i,), memory_space=pltpu.SMEM),
                  pl.BlockSpec((TOK_TILE * SLAB, LANES), lambda i, *_: (i, 0))],
        out_specs=pl.BlockSpec(memory_space=pl.ANY),
        scratch_shapes=[pltpu.VMEM((ROW_TILE * SLAB, LANES), F32),
                        pltpu.SemaphoreType.DMA(()), pltpu.SemaphoreType.DMA(())],
    )
    return pl.pallas_call(
        functools.partial(_dispatch_kernel, n_tiles=n_tiles),
        grid_spec=grid_spec,
        out_shape=jax.ShapeDtypeStruct((n_tiles * ROW_TILE * SLAB, LANES), F32),
        compiler_params=_cparams(("arbitrary",)),
        name="dispatch",
    )(last_tile, ntiles, used, dest_flat, h2)


def _experts_kernel(te_ref, used_ref, xs_ref, wgu_ref, bgu_ref, wd_ref, bd_ref, ys_ref,
                    wgu_bf, wd_bf, act_bf):
    i = pl.program_id(0)
    prev = te_ref[jnp.maximum(i - 1, 0)]
    fresh = (i == 0) | (te_ref[i] != prev)

    @pl.when(fresh)
    def _():
        wgu_bf[...] = wgu_ref[...].astype(BF16)
        wd_bf[...] = wd_ref[...].astype(BF16)

    @pl.when(i < used_ref[0])
    def _():
        x = jnp.concatenate([p.astype(BF16) for p in _load_slabs(xs_ref, ROW_TILE)], axis=1)
        cw = D_FF // FF_CHUNKS
        for c in range(FF_CHUNKS):
            gate = jnp.dot(x, wgu_bf[:, c * cw:(c + 1) * cw], preferred_element_type=F32)
            gate = jnp.minimum(gate + bgu_ref[:, c * cw:(c + 1) * cw], SWIGLU_LIMIT)
            up = jnp.dot(x, wgu_bf[:, D_FF + c * cw:D_FF + (c + 1) * cw], preferred_element_type=F32)
            up = jnp.clip(up + bgu_ref[:, D_FF + c * cw:D_FF + (c + 1) * cw], -SWIGLU_LIMIT, SWIGLU_LIMIT)
            act = gate * jax.nn.sigmoid(SWIGLU_ALPHA * gate) * (up + 1.0)
            act_bf[:, c * cw:(c + 1) * cw] = act.astype(BF16)
        ow = D_MODEL // FF_CHUNKS
        for c in range(FF_CHUNKS):
            out = jnp.dot(act_bf[...], wd_bf[:, c * ow:(c + 1) * ow], preferred_element_type=F32)
            out = out + bd_ref[:, c * ow:(c + 1) * ow]
            for j in range(ow // LANES):
                ys_ref[pl.ds(c * (ow // LANES) + j, ROW_TILE, stride=SLAB), :] = (
                    out[:, LANES * j:LANES * (j + 1)])

    @pl.when(i >= used_ref[0])
    def _():
        ys_ref[...] = jnp.zeros_like(ys_ref)


def _experts(xs, tile_expert, used, w_gu, b_gu, w_down, b_down):
    rows = xs.shape[0] // SLAB
    nt = rows // ROW_TILE

    def tile(i, te, used):
        return (jnp.minimum(i, used[0] - 1), 0)

    grid_spec = pltpu.PrefetchScalarGridSpec(
        num_scalar_prefetch=2,
        grid=(nt,),
        in_specs=[pl.BlockSpec((ROW_TILE * SLAB, LANES), tile),
                  pl.BlockSpec((None, D_MODEL, 2 * D_FF), lambda i, te, used: (te[i], 0, 0)),
                  pl.BlockSpec((None, 1, 2 * D_FF), lambda i, te, used: (te[i], 0, 0)),
                  pl.BlockSpec((None, D_FF, D_MODEL), lambda i, te, used: (te[i], 0, 0)),
                  pl.BlockSpec((None, 1, D_MODEL), lambda i, te, used: (te[i], 0, 0))],
        out_specs=pl.BlockSpec((ROW_TILE * SLAB, LANES), lambda i, te, used: (i, 0)),
        scratch_shapes=[pltpu.VMEM((D_MODEL, 2 * D_FF), BF16), pltpu.VMEM((D_FF, D_MODEL), BF16),
                        pltpu.VMEM((ROW_TILE, D_FF), BF16)],
    )
    return pl.pallas_call(
        _experts_kernel,
        grid_spec=grid_spec,
        out_shape=jax.ShapeDtypeStruct((rows * SLAB, LANES), F32),
        compiler_params=_cparams(("arbitrary",)),
        name="experts",
    )(tile_expert, used, xs, w_gu, b_gu.reshape(N_EXPERTS, 1, 2 * D_FF), w_down,
      b_down.reshape(N_EXPERTS, 1, D_MODEL))


def _combine_kernel(dest_ref, next_ref, ys_ref, x1_ref, p_ref, g2_ref, fn_ref, y_ref, buf_ref, sem,
                    *, last):
    i = pl.program_id(0)
    slot = i % 2

    def issue(d_ref, sl):
        def body(r, carry):
            for k in range(TOP_K):
                d = d_ref[r * TOP_K + k]
                pltpu.make_async_copy(ys_ref.at[_slab_rows(d)], buf_ref.at[sl, k, _slab_rows(r)],
                                      sem.at[sl]).start(priority=k % 2)
            return carry
        lax.fori_loop(0, TOK_TILE, body, 0, unroll=2)

    @pl.when(i == 0)
    def _():
        issue(dest_ref, 0)

    @pl.when(i + 1 < pl.num_programs(0))
    def _():
        issue(next_ref, 1 - slot)

    for k in range(TOP_K):
        pltpu.make_async_copy(ys_ref.at[pl.ds(0, TOK_TILE * SLAB)], buf_ref.at[slot, k],
                              sem.at[slot]).wait()

    p = p_ref[...]
    groups = None
    for k in range(TOP_K):
        part = [p[:, k:k + 1] * g for g in _load_slabs(buf_ref.at[slot, k], TOK_TILE)]
        groups = part if groups is None else [a + b for a, b in zip(groups, part)]
    moe = jnp.concatenate(groups, axis=1)
    x2 = x1_ref[...] + g2_ref[...] * moe
    if last:
        x2 = x2 * lax.rsqrt(jnp.mean(x2 * x2, axis=-1, keepdims=True) + EPS) * fn_ref[...]
    y_ref[...] = x2


def _combine(ys, dest_flat, x1, probs, mod, per_token, seq, tile0, final_norm, last):
    n = x1.shape[0]
    tm = TOK_TILE
    tps = max(seq // tm, 1)
    nt = n // tm
    return pl.pallas_call(
        functools.partial(_combine_kernel, last=last),
        grid=(nt,),
        in_specs=[pl.BlockSpec((tm * TOP_K,), lambda i: (i + tile0,), memory_space=pltpu.SMEM),
                  pl.BlockSpec((tm * TOP_K,), lambda i: (jnp.minimum(i + 1, nt - 1) + tile0,),
                               memory_space=pltpu.SMEM),
                  pl.BlockSpec(memory_space=pl.ANY),
                  pl.BlockSpec((tm, D_MODEL), lambda i: (i, 0)),
                  pl.BlockSpec((tm, TOP_K), lambda i: (i + tile0, 0)),
                  _mod_spec(per_token, tm, tps, 5),
                  pl.BlockSpec((1, D_MODEL), lambda i: (0, 0))],
        out_specs=pl.BlockSpec((tm, D_MODEL), lambda i: (i, 0)),
        out_shape=jax.ShapeDtypeStruct((n, D_MODEL), F32),
        scratch_shapes=[pltpu.VMEM((2, TOP_K, tm * SLAB, LANES), F32),
                        pltpu.SemaphoreType.DMA((2,))],
        compiler_params=_cparams(("arbitrary",)),
        name="combine",
    )(dest_flat, dest_flat, ys, x1, probs, mod, final_norm.reshape(1, D_MODEL))


def _invert_kernel(dest_ref, src0_ref, dst0_ref, src_ref, dst_ref, sem, *, n_tok):
    i = pl.program_id(0)
    blk = dest_ref.shape[0]

    @pl.when(i == 0)
    def _():
        copies = [pltpu.make_async_copy(src0_ref, src_ref, sem.at[0]),
                  pltpu.make_async_copy(dst0_ref, dst_ref, sem.at[1])]
        for cp in copies:
            cp.start()
        for cp in copies:
            cp.wait()

    def body(a, carry):
        g = i * blk + a
        d = dest_ref[a]
        tok = g >> (TOP_K.bit_length() - 1)
        src_ref[d] = tok
        dst_ref[d] = (g & (TOP_K - 1)) * n_tok + tok
        return carry

    lax.fori_loop(0, blk, body, 0, unroll=8)


def _invert(dest_flat, n_tok, n_rows):
    blk = TOK_TILE * TOP_K
    whole = pl.BlockSpec(memory_space=pltpu.SMEM)
    src0 = jnp.zeros((n_rows,), I32)
    dst0 = n_tok * TOP_K + jnp.arange(n_rows, dtype=I32)
    return pl.pallas_call(
        functools.partial(_invert_kernel, n_tok=n_tok),
        grid=(dest_flat.shape[0] // blk,),
        in_specs=[pl.BlockSpec((blk,), lambda i: (i,), memory_space=pltpu.SMEM),
                  pl.BlockSpec(memory_space=pl.ANY), pl.BlockSpec(memory_space=pl.ANY)],
        out_specs=[whole, whole],
        out_shape=[jax.ShapeDtypeStruct((n_rows,), I32), jax.ShapeDtypeStruct((n_rows,), I32)],
        scratch_shapes=[pltpu.SemaphoreType.DMA((2,))],
        compiler_params=_cparams(("arbitrary",)),
        name="invert",
    )(dest_flat, src0, dst0)


FF_CHUNKS = 4


def _moe_kernel(src_ref, dst_ref, te_ref, used_ref, h_ref, wgu_ref, bgu_ref, wd_ref, bd_ref, y_ref,
                wgu_bf, wd_bf, act_bf, buf, gsem, ssem):
    i = pl.program_id(0)
    used = used_ref[0]
    slot = i % 2
    tile_rows = ROW_TILE * SLAB
    n_tiles = te_ref.shape[0]

    def gather(tile, sl, rows):
        for r in rows:
            tok = src_ref[tile * ROW_TILE + r]
            pltpu.make_async_copy(h_ref.at[_slab_rows(tok)], buf.at[sl, _slab_rows(r)],
                                  gsem.at[sl]).start(priority=1)

    def scatter(tile, sl, rows):
        for r in rows:
            out = dst_ref[tile * ROW_TILE + r]
            pltpu.make_async_copy(buf.at[2 + sl, _slab_rows(r)], y_ref.at[_slab_rows(out)],
                                  ssem.at[sl]).start()

    def wait_gather(sl):
        pltpu.make_async_copy(h_ref.at[pl.ds(0, tile_rows)], buf.at[sl], gsem.at[sl]).wait()

    def wait_scatter(sl):
        pltpu.make_async_copy(buf.at[2 + sl], y_ref.at[pl.ds(0, tile_rows)], ssem.at[sl]).wait()

    @pl.when(i == 0)
    def _():
        buf[2] = jnp.zeros(buf.shape[1:], F32)
        buf[3] = jnp.zeros(buf.shape[1:], F32)
        tail = y_ref.shape[0] - tile_rows
        pltpu.make_async_copy(buf.at[2], y_ref.at[pl.ds(tail, tile_rows)], ssem.at[0]).start()
        gather(0, 0, range(ROW_TILE))

    prev = te_ref[jnp.maximum(i - 1, 0)]
    fresh = (i == 0) | (te_ref[jnp.minimum(i, n_tiles - 1)] != prev)

    @pl.when(fresh & (i < used))
    def _():
        wgu_bf[...] = wgu_ref[...].astype(BF16)
        wd_bf[...] = wd_ref[...].astype(BF16)

    per = ROW_TILE // FF_CHUNKS
    cw = D_FF // FF_CHUNKS

    @pl.when(i < used)
    def _():
        wait_gather(slot)
        nxt = jnp.minimum(i + 1, n_tiles - 1)
        prv = jnp.where(i >= 1, i - 1, n_tiles)
        half = per // 2
        for c in range(FF_CHUNKS):
            gather(nxt, 1 - slot, range(c * per, (c + 1) * per))
            scatter(prv, 1 - slot, range(c * half, (c + 1) * half))
            x = jnp.concatenate([p.astype(BF16) for p in _load_slabs(buf.at[slot], ROW_TILE)], axis=1)
            gate = jnp.dot(x, wgu_bf[:, c * cw:(c + 1) * cw], preferred_element_type=F32)
            gate = jnp.minimum(gate + bgu_ref[:, c * cw:(c + 1) * cw], SWIGLU_LIMIT)
            up = jnp.dot(x, wgu_bf[:, D_FF + c * cw:D_FF + (c + 1) * cw], preferred_element_type=F32)
            up = jnp.clip(up + bgu_ref[:, D_FF + c * cw:D_FF + (c + 1) * cw], -SWIGLU_LIMIT, SWIGLU_LIMIT)
            act = gate * jax.nn.sigmoid(SWIGLU_ALPHA * gate) * (up + 1.0)
            act_bf[:, c * cw:(c + 1) * cw] = act.astype(BF16)
        ow = D_MODEL // FF_CHUNKS
        base = FF_CHUNKS * half
        wait_scatter(slot)
        for c in range(FF_CHUNKS):
            scatter(prv, 1 - slot, range(base + c * half, base + (c + 1) * half))
            out = jnp.dot(act_bf[...], wd_bf[:, c * ow:(c + 1) * ow], preferred_element_type=F32)
            out = out + bd_ref[:, c * ow:(c + 1) * ow]
            for j in range(ow // LANES):
                buf[2 + slot, pl.ds(c * (ow // LANES) + j, ROW_TILE, stride=SLAB), :] = (
                    out[:, LANES * j:LANES * (j + 1)])

    @pl.when(i == used)
    def _():
        wait_gather(slot)
        wait_scatter(slot)
        scatter(i - 1, 1 - slot, range(ROW_TILE))
        wait_scatter(1 - slot)


def _moe(h2, src, dst, tile_expert, used, w_gu, b_gu, w_down, b_down, n_tok):
    n_tiles = tile_expert.shape[0]
    n_rows = n_tiles * ROW_TILE
    ex = lambda i, src, dst, te, used: (te[jnp.minimum(i, n_tiles - 1)], 0, 0)
    grid_spec = pltpu.PrefetchScalarGridSpec(
        num_scalar_prefetch=4,
        grid=(n_tiles + 1,),
        in_specs=[pl.BlockSpec(memory_space=pl.ANY),
                  pl.BlockSpec((None, D_MODEL, 2 * D_FF), ex),
                  pl.BlockSpec((None, 1, 2 * D_FF), ex),
                  pl.BlockSpec((None, D_FF, D_MODEL), ex),
                  pl.BlockSpec((None, 1, D_MODEL), ex)],
        out_specs=pl.BlockSpec(memory_space=pl.ANY),
        scratch_shapes=[pltpu.VMEM((D_MODEL, 2 * D_FF), BF16), pltpu.VMEM((D_FF, D_MODEL), BF16),
                        pltpu.VMEM((ROW_TILE, D_FF), BF16),
                        pltpu.VMEM((4, ROW_TILE * SLAB, LANES), F32),
                        pltpu.SemaphoreType.DMA((2,)), pltpu.SemaphoreType.DMA((2,))],
    )
    return pl.pallas_call(
        _moe_kernel,
        grid_spec=grid_spec,
        out_shape=jax.ShapeDtypeStruct(((n_tok * TOP_K + n_rows + 2 * ROW_TILE) * SLAB, LANES), F32),
        compiler_params=_cparams(("arbitrary",)),
        name="moe",
    )(src, dst, tile_expert, used, h2, w_gu, b_gu.reshape(N_EXPERTS, 1, 2 * D_FF), w_down,
      b_down.reshape(N_EXPERTS, 1, D_MODEL))


def _merge_kernel(y0_ref, y1_ref, y2_ref, y3_ref, x1_ref, p_ref, g2_ref, fn_ref, o_ref, *, last):
    p = p_ref[...]
    tm = p.shape[0]
    groups = None
    for k, ref in enumerate((y0_ref, y1_ref, y2_ref, y3_ref)):
        part = [p[:, k:k + 1] * g for g in _load_slabs(ref, tm)]
        groups = part if groups is None else [a + b for a, b in zip(groups, part)]
    x2 = x1_ref[...] + g2_ref[...] * jnp.concatenate(groups, axis=1)
    if last:
        x2 = x2 * lax.rsqrt(jnp.mean(x2 * x2, axis=-1, keepdims=True) + EPS) * fn_ref[...]
    o_ref[...] = x2


def _merge(y, x1, probs, mod, per_token, seq, tile0, n_tok, final_norm, last):
    n = x1.shape[0]
    tm = TOK_TILE
    tps = max(seq // tm, 1)
    choice = lambda k: pl.BlockSpec((tm * SLAB, LANES), lambda i: (k * (n_tok // tm) + tile0 + i, 0))
    return pl.pallas_call(
        functools.partial(_merge_kernel, last=last),
        grid=(n // tm,),
        in_specs=[choice(0), choice(1), choice(2), choice(3),
                  pl.BlockSpec((tm, D_MODEL), lambda i: (i, 0)),
                  pl.BlockSpec((tm, TOP_K), lambda i: (i + tile0, 0)),
                  _mod_spec(per_token, tm, tps, 5),
                  pl.BlockSpec((1, D_MODEL), lambda i: (0, 0))],
        out_specs=pl.BlockSpec((tm, D_MODEL), lambda i: (i, 0)),
        out_shape=jax.ShapeDtypeStruct((n, D_MODEL), F32),
        compiler_params=_cparams(("arbitrary",)),
        name="merge",
    )(y, y, y, y, x1, probs, mod, final_norm.reshape(1, D_MODEL))


def kernel(x_prompt, x_sample, state_hgrn, state_pool, c_prompt, c_sample, w_ada, b_ada, norm1,
           norm2, w_in, lower_bounds, hgrn_norm, w_pool, pool_scale, w_out, w_router, b_router,
           w_gu, b_gu, w_down, b_down, final_norm):
    bp, seq_p, _ = x_prompt.shape
    bs, seq_s, _ = x_sample.shape
    np_, ns = bp * seq_p, bs * seq_s
    depth = w_ada.shape[0]
    past_len = 16384

    lbs = jnp.cumsum(jax.nn.softmax(lower_bounds.astype(F32), axis=0), axis=0)
    xp = x_prompt.reshape(np_, D_MODEL)
    xs_ = x_sample.reshape(ns, D_MODEL)
    c_all = jnp.concatenate([c_prompt, c_sample], axis=0)

    sp_l, hp_l, ss_l, hs_l = [], [], [], []
    for l in range(depth):
        w_in_bf = w_in[l].astype(BF16)
        w_out_bf = w_out[l].astype(BF16)
        w_pool_bf = w_pool[l].astype(BF16)
        wr_hi = w_router[l].astype(BF16)
        wr_lo = (w_router[l] - wr_hi.astype(F32)).astype(BF16)
        lb = lbs[l].reshape(1, D_A)
        hn = hgrn_norm[l].reshape(1, D_A)
        psc = pool_scale[l].reshape(1, D_B)

        mod = _ada(c_all, w_ada[l], b_ada[l])
        mod_p = mod[:bp].reshape(bp, 1, 6 * D_MODEL)
        mod_s = jnp.repeat(mod[bp:], seq_s, axis=0)

        up = _inproj(xp, mod_p, False, seq_p, norm1[l], w_in_bf)
        us = _inproj(xs_, mod_s, True, seq_s, norm1[l], w_in_bf)
        us3 = us.reshape(bs, seq_s, D_IN)

        oa_p, s_p = _hgrn(up.reshape(bp, seq_p, D_IN), None, lb, hn)
        oa_p = oa_p.reshape(np_, D_A)
        oa_s, s_s = _hgrn(us3, state_hgrn[l], lb, hn)
        ob_p, h_p = _pool_prompt(up, bp, seq_p, w_pool_bf, psc)
        ob_t, h_t = _pool_sample(us3.transpose(1, 0, 2), state_pool[l].transpose(1, 0, 2), past_len,
                                 w_pool_bf, psc)
        ob_s, h_s = ob_t.transpose(1, 0, 2), h_t.transpose(1, 0, 2)

        n_tok = np_ + ns
        n_tiles = (n_tok * TOP_K) // ROW_TILE + N_EXPERTS
        x1p, *joint = _outproj(xp, oa_p, ob_p, mod_p, False, seq_p, norm2[l], w_out_bf,
                               wr_hi, wr_lo, b_router[l], n_tok, 0)
        x1s, h2, top_idx, probs = _outproj(
            xs_, oa_s.reshape(ns, D_A), ob_s.reshape(ns, D_B), mod_s, True, seq_s, norm2[l],
            w_out_bf, wr_hi, wr_lo, b_router[l], n_tok, np_ // TOK_TILE, joint)

        dest, counts = _rank(top_idx)
        dest_flat = dest.reshape(n_tok * TOP_K)
        tiles_e = (counts.reshape(N_EXPERTS) + ROW_TILE - 1) // ROW_TILE
        ends = jnp.cumsum(tiles_e)
        used = ends[-1:].astype(I32)
        tile_expert = jnp.minimum(
            jnp.sum(jnp.arange(n_tiles, dtype=I32)[:, None] >= ends[None, :], axis=1),
            N_EXPERTS - 1).astype(I32)

        last_tile = (ends - 1).astype(I32)

        xsort = _dispatch(h2, dest_flat, last_tile, tiles_e.astype(I32), used, n_tiles)
        ysort = _experts(xsort, tile_expert, used, w_gu[l], b_gu[l], w_down[l], b_down[l])

        last = l == depth - 1
        xp = _combine(ysort, dest_flat, x1p, probs, mod_p, False, seq_p, 0, final_norm, last)
        xs_ = _combine(ysort, dest_flat, x1s, probs, mod_s, True, seq_s, np_ // TOK_TILE,
                       final_norm, last)

        sp_l.append(s_p)
        hp_l.append(h_p)
        ss_l.append(s_s)
        hs_l.append(h_s)

    return (xp.reshape(bp, seq_p, D_MODEL), xs_.reshape(bs, seq_s, D_MODEL),
            jnp.stack(sp_l), jnp.stack(hp_l), jnp.stack(ss_l), jnp.stack(hs_l))
```

```python
import functools

import jax
import jax.numpy as jnp
from jax import lax
from jax.experimental import pallas as pl
from jax.experimental.pallas import tpu as pltpu

F32, BF16, I32 = jnp.float32, jnp.bfloat16, jnp.int32

D_MODEL = 1024
D_A = 512
H_A = 4
DK = 128
DV = 128
D_B = 512
POOL_WINDOWS = (2, 4, 8, 16)
POOL_GROUP = 128
POOL_HIST = 15
D_IN = 4 * D_A + D_B
N_EXPERTS = 32
TOP_K = 4
D_FF = 1024
SWIGLU_LIMIT = 7.0
SWIGLU_ALPHA = 1.702
EPS = 1e-6

SUB = 16
ROW_TILE = 256
TOK_TILE = 256
VMEM_LIMIT = 56 * 1024 * 1024


def _cparams(sem):
    return pltpu.CompilerParams(dimension_semantics=sem, vmem_limit_bytes=VMEM_LIMIT)


def _silu(x):
    return x * jax.nn.sigmoid(x)


def _mod_spec(per_token, tm, tiles_per_seq, col):
    if per_token:
        return pl.BlockSpec((tm, D_MODEL), lambda i: (i, col))
    return pl.BlockSpec((None, 1, D_MODEL), lambda i: (i // tiles_per_seq, 0, col))


def _ada_kernel(c_ref, w_ref, b_ref, o_ref):
    c = c_ref[...]
    o_ref[...] = jnp.dot(_silu(c).astype(BF16), w_ref[...].astype(BF16),
                         preferred_element_type=F32) + b_ref[...]


def _ada(c_all, w_ada, b_ada):
    rows = c_all.shape[0]
    n = w_ada.shape[1]
    return pl.pallas_call(
        _ada_kernel,
        grid=(n // D_MODEL,),
        in_specs=[pl.BlockSpec((rows, D_MODEL), lambda j: (0, 0)),
                  pl.BlockSpec((D_MODEL, D_MODEL), lambda j: (0, j)),
                  pl.BlockSpec((1, D_MODEL), lambda j: (0, j))],
        out_specs=pl.BlockSpec((rows, D_MODEL), lambda j: (0, j)),
        out_shape=jax.ShapeDtypeStruct((rows, n), F32),
        compiler_params=_cparams(("arbitrary",)),
        name="ada",
    )(c_all, w_ada, b_ada.reshape(1, n))


def _inproj_kernel(x_ref, sh_ref, sc_ref, n1_ref, w_ref, u_ref):
    x = x_ref[...]
    h = x * lax.rsqrt(jnp.mean(x * x, axis=-1, keepdims=True) + EPS) * n1_ref[...]
    h = h * (1.0 + sc_ref[...]) + sh_ref[...]
    u_ref[...] = jnp.dot(h.astype(BF16), w_ref[...], preferred_element_type=F32)


def _inproj(x, mod, per_token, seq, norm1, w_in_bf):
    n = x.shape[0]
    tm = min(512, n)
    tps = max(seq // tm, 1)
    return pl.pallas_call(
        _inproj_kernel,
        grid=(n // tm,),
        in_specs=[pl.BlockSpec((tm, D_MODEL), lambda i: (i, 0)),
                  _mod_spec(per_token, tm, tps, 0),
                  _mod_spec(per_token, tm, tps, 1),
                  pl.BlockSpec((1, D_MODEL), lambda i: (0, 0)),
                  pl.BlockSpec((D_MODEL, D_IN), lambda i: (0, 0))],
        out_specs=pl.BlockSpec((tm, D_IN), lambda i: (i, 0)),
        out_shape=jax.ShapeDtypeStruct((n, D_IN), F32),
        compiler_params=_cparams(("arbitrary",)),
        name="inproj",
    )(x, mod, mod, norm1.reshape(1, D_MODEL), w_in_bf)


def _block_select():
    r = lax.broadcasted_iota(I32, (SUB, SUB * DK), 0)
    c = lax.broadcasted_iota(I32, (SUB, SUB * DK), 1) >> (DK.bit_length() - 1)
    return (r == c).astype(BF16)


def _hgrn_block(q, fp, iv, g, lb, hn, st_ref, sel_ref, valid):
    f = lb + (1.0 - lb) * jax.nn.sigmoid(fp)
    logf = jnp.log(f)
    kk = 1.0 - f
    if valid < SUB:
        live = lax.broadcasted_iota(I32, (SUB, D_A), 0) < valid
        logf = jnp.where(live, logf, 0.0)
        kk = jnp.where(live, kk, 0.0)
    qa = _silu(q)

    r = lax.broadcasted_iota(I32, (SUB, SUB), 0)
    c = lax.broadcasted_iota(I32, (SUB, SUB), 1)
    tri = (r >= c).astype(BF16)
    hi = logf.astype(BF16)
    lo = (logf - hi.astype(F32)).astype(BF16)
    b = (jnp.dot(tri, hi, preferred_element_type=F32)
         + jnp.dot(tri, lo, preferred_element_type=F32))
    bend = b[SUB - 1:SUB, :]
    qt = (qa * jnp.exp(b)).astype(BF16)
    kh = (kk * jnp.exp(bend - b)).astype(BF16)
    dec = jnp.exp(bend)
    ivb = iv.astype(BF16)

    rows = []
    for h in range(H_A):
        sl = slice(DK * h, DK * (h + 1))
        bh, qh, kkh = b[:, sl], qa[:, sl], kk[:, sl]
        ps = []
        for s in range(SUB):
            e = jnp.exp(jnp.minimum(bh - bh[s:s + 1, :], 0.0))
            ps.append((e * (qh * kkh[s:s + 1, :])).astype(BF16))
        rows.append(jnp.concatenate(ps, axis=1))
    pcat = jnp.concatenate(rows, axis=0)
    a = lax.dot_general(pcat, sel_ref[...], (((1,), (1,)), ((), ())),
                        preferred_element_type=F32)
    tr = lax.broadcasted_iota(I32, (H_A * SUB, SUB), 0) & (SUB - 1)
    tc = lax.broadcasted_iota(I32, (H_A * SUB, SUB), 1)
    a = jnp.where(tr >= tc, a, 0.0).astype(BF16)

    outs = []
    for h in range(H_A):
        sl = slice(DK * h, DK * (h + 1))
        st = st_ref[h]
        o = lax.dot_general(qt[:, sl], st.astype(BF16), (((1,), (1,)), ((), ())),
                            preferred_element_type=F32)
        o = o + jnp.dot(a[SUB * h:SUB * (h + 1), :], ivb[:, sl], preferred_element_type=F32)
        ut = lax.dot_general(ivb[:, sl], kh[:, sl], (((0,), (0,)), ((), ())),
                             preferred_element_type=F32)
        st_ref[h] = st * dec[:, sl] + ut
        o = o * lax.rsqrt(jnp.mean(o * o, axis=-1, keepdims=True) + EPS) * hn[:, sl]
        outs.append(o)
    return jnp.concatenate(outs, axis=1) * _silu(g)


def _hgrn_prompt_kernel(q_ref, f_ref, i_ref, g_ref, lb_ref, hn_ref, o_ref, s_ref, st_ref, sel_ref,
                        *, tt):
    t = pl.program_id(1)

    @pl.when(t == 0)
    def _():
        st_ref[...] = jnp.zeros_like(st_ref)
        sel_ref[...] = _block_select()

    lb = lb_ref[...]
    hn = hn_ref[...]

    def body(j, carry):
        rs = pl.ds(pl.multiple_of(j * SUB, SUB), SUB)
        o_ref[rs, :] = _hgrn_block(q_ref[rs, :], f_ref[rs, :], i_ref[rs, :], g_ref[rs, :],
                                   lb, hn, st_ref, sel_ref, SUB)
        return carry

    lax.fori_loop(0, tt // SUB, body, 0)

    @pl.when(t == pl.num_programs(1) - 1)
    def _():
        for h in range(H_A):
            s_ref[h] = st_ref[h].T


def _hgrn_prompt(u, batch, seq, lb, hn):
    tt = 256
    nt = seq // tt

    def col(cb):
        return pl.BlockSpec((tt, D_A), lambda b, t: (b * nt + t, cb))

    return pl.pallas_call(
        functools.partial(_hgrn_prompt_kernel, tt=tt),
        grid=(batch, nt),
        in_specs=[col(0), col(1), col(2), col(3),
                  pl.BlockSpec((1, D_A), lambda b, t: (0, 0)),
                  pl.BlockSpec((1, D_A), lambda b, t: (0, 0))],
        out_specs=[pl.BlockSpec((tt, D_A), lambda b, t: (b * nt + t, 0)),
                   pl.BlockSpec((None, H_A, DK, DV), lambda b, t: (b, 0, 0, 0))],
        out_shape=[jax.ShapeDtypeStruct((batch * seq, D_A), F32),
                   jax.ShapeDtypeStruct((batch, H_A, DK, DV), F32)],
        scratch_shapes=[pltpu.VMEM((H_A, DV, DK), F32), pltpu.VMEM((SUB, SUB * DK), BF16)],
        compiler_params=_cparams(("arbitrary", "arbitrary")),
        name="hgrn_prompt",
    )(u, u, u, u, lb, hn)


def _hgrn_sample_kernel(q_ref, f_ref, i_ref, g_ref, lb_ref, hn_ref, s0_ref, o_ref, s_ref,
                        st_ref, sel_ref, pad_ref, *, seq):
    for h in range(H_A):
        st_ref[h] = s0_ref[h].T
    sel_ref[...] = _block_select()
    pad_ref[...] = jnp.zeros_like(pad_ref)
    for n, ref in enumerate((q_ref, f_ref, i_ref, g_ref)):
        pad_ref[n, 0:seq, :] = ref[...]
    o = _hgrn_block(pad_ref[0], pad_ref[1], pad_ref[2], pad_ref[3], lb_ref[...], hn_ref[...],
                    st_ref, sel_ref, seq)
    o_ref[...] = o[0:seq, :]
    for h in range(H_A):
        s_ref[h] = st_ref[h].T


def _hgrn_sample(u3, s0, lb, hn):
    batch, seq, _ = u3.shape

    def col(cb):
        return pl.BlockSpec((None, seq, D_A), lambda b: (b, 0, cb))

    return pl.pallas_call(
        functools.partial(_hgrn_sample_kernel, seq=seq),
        grid=(batch,),
        in_specs=[col(0), col(1), col(2), col(3),
                  pl.BlockSpec((1, D_A), lambda b: (0, 0)),
                  pl.BlockSpec((1, D_A), lambda b: (0, 0)),
                  pl.BlockSpec((None, H_A, DK, DV), lambda b: (b, 0, 0, 0))],
        out_specs=[pl.BlockSpec((None, seq, D_A), lambda b: (b, 0, 0)),
                   pl.BlockSpec((None, H_A, DK, DV), lambda b: (b, 0, 0, 0))],
        out_shape=[jax.ShapeDtypeStruct((batch, seq, D_A), F32),
                   jax.ShapeDtypeStruct((batch, H_A, DK, DV), F32)],
        scratch_shapes=[pltpu.VMEM((H_A, DV, DK), F32), pltpu.VMEM((SUB, SUB * DK), BF16),
                        pltpu.VMEM((4, SUB, D_A), F32)],
        compiler_params=_cparams(("arbitrary",)),
        name="hgrn_sample",
    )(u3, u3, u3, u3, lb, hn, s0)


HGRN_SEQS = 8
HALF = SUB // 2


def _hgrn_step(q, fp, iv, g, lb, hn, st_ref, sel_ref, cm_ref, b_ref, k_ref, valid):
    nrow = HGRN_SEQS * SUB
    f = lb + (1.0 - lb) * jax.nn.sigmoid(fp)
    lf = jnp.log2(f)
    kk = 1.0 - f
    if valid < SUB:
        live = (lax.broadcasted_iota(I32, (nrow, D_A), 0) & (SUB - 1)) < valid
        lf = jnp.where(live, lf, 0.0)
        kk = jnp.where(live, kk, 0.0)
    qa = _silu(q)

    hi = lf.astype(BF16)
    lo = (lf - hi.astype(F32)).astype(BF16)
    cum, tot = cm_ref[0], cm_ref[1]
    b = jnp.dot(cum, hi, preferred_element_type=F32) + jnp.dot(cum, lo, preferred_element_type=F32)
    be = jnp.dot(tot, hi, preferred_element_type=F32) + jnp.dot(tot, lo, preferred_element_type=F32)
    qt = (qa * jnp.exp2(b)).astype(BF16)
    kh = (kk * jnp.exp2(be - b)).astype(BF16)
    dec = jnp.exp2(be)
    ivb = iv.astype(BF16)
    for h in range(H_A):
        b_ref[h] = b[:, DK * h:DK * (h + 1)]
        k_ref[h] = kk[:, DK * h:DK * (h + 1)]

    zero_half = jnp.zeros((HALF, D_A), F32)
    blocks = []
    for n in range(HGRN_SEQS):
        per_s = []
        for s in range(SUB):
            row = n * SUB + s
            b_s = jnp.concatenate([b_ref[h, pl.ds(row, HALF, stride=0), :] for h in range(H_A)], axis=1)
            k_s = jnp.concatenate([k_ref[h, pl.ds(row, HALF, stride=0), :] for h in range(H_A)], axis=1)
            halves = []
            for hf in range(2):
                if hf == 0 and s >= HALF:
                    halves.append(zero_half)
                    continue
                r0 = n * SUB + hf * HALF
                e = jnp.exp2(jnp.minimum(b[r0:r0 + HALF] - b_s, 0.0))
                halves.append(e * (qa[r0:r0 + HALF] * k_s))
            per_s.append(jnp.concatenate(halves, axis=0).astype(BF16))
        for h in range(H_A):
            blocks.append(jnp.concatenate([p[:, DK * h:DK * (h + 1)] for p in per_s], axis=1))
    pcat = jnp.concatenate(blocks, axis=0)
    a = lax.dot_general(pcat, sel_ref[...], (((1,), (1,)), ((), ())),
                        preferred_element_type=F32)
    tr = lax.broadcasted_iota(I32, a.shape, 0) & (SUB - 1)
    tc = lax.broadcasted_iota(I32, a.shape, 1)
    a = jnp.where(tr >= tc, a, 0.0).astype(BF16)

    outs = []
    for n in range(HGRN_SEQS):
        rows = slice(n * SUB, (n + 1) * SUB)
        heads = []
        for h in range(H_A):
            sl = slice(DK * h, DK * (h + 1))
            st = st_ref[n, h]
            o = lax.dot_general(qt[rows, sl], st.astype(BF16), (((1,), (1,)), ((), ())),
                                preferred_element_type=F32)
            ar = (n * H_A + h) * SUB
            o = o + jnp.dot(a[ar:ar + SUB, :], ivb[rows, sl], preferred_element_type=F32)
            ut = lax.dot_general(ivb[rows, sl], kh[rows, sl], (((0,), (0,)), ((), ())),
                                 preferred_element_type=F32)
            dtile = dec[n * SUB:n * SUB + HALF, sl]
            st_ref[n, h] = (st.reshape(DV // HALF, HALF, DK) * dtile[None]).reshape(DV, DK) + ut
            o = o * lax.rsqrt(jnp.mean(o * o, axis=-1, keepdims=True) + EPS) * hn[:, sl]
            heads.append(o)
        outs.append(jnp.concatenate(heads, axis=1))
    return jnp.concatenate(outs, axis=0) * _silu(g)


def _hgrn_kernel(*refs, rows, has_s0):
    if has_s0:
        (q_ref, f_ref, i_ref, g_ref, lb_ref, hn_ref, s0_ref, o_ref, s_ref,
         st_ref, sel_ref, cm_ref, b_ref, k_ref, pad_ref) = refs
    else:
        (q_ref, f_ref, i_ref, g_ref, lb_ref, hn_ref, o_ref, s_ref,
         st_ref, sel_ref, cm_ref, b_ref, k_ref) = refs
    t = pl.program_id(0)
    nrow = HGRN_SEQS * SUB

    @pl.when(t == 0)
    def _():
        sel_ref[...] = _block_select()
        r = lax.broadcasted_iota(I32, (nrow, nrow), 0)
        c = lax.broadcasted_iota(I32, (nrow, nrow), 1)
        shift = SUB.bit_length() - 1
        same = (r >> shift) == (c >> shift)
        cm_ref[0] = (same & (r >= c)).astype(BF16)
        cm_ref[1] = same.astype(BF16)

    lb = lb_ref[...]
    hn = hn_ref[...]
    if has_s0:
        for n in range(HGRN_SEQS):
            for h in range(H_A):
                st_ref[n, h] = s0_ref[n, h].T
        pad_ref[...] = jnp.zeros_like(pad_ref)
        for j, ref in enumerate((q_ref, f_ref, i_ref, g_ref)):
            pad_ref[j, :, 0:rows, :] = ref[...]
        o = _hgrn_step(*(pad_ref[j].reshape(nrow, D_A) for j in range(4)), lb, hn,
                       st_ref, sel_ref, cm_ref, b_ref, k_ref, rows)
        o_ref[...] = o.reshape(HGRN_SEQS, SUB, D_A)[:, 0:rows, :]
    else:
        @pl.when(t == 0)
        def _():
            st_ref[...] = jnp.zeros_like(st_ref)

        def body(j, carry):
            rs = pl.ds(pl.multiple_of(j * SUB, SUB), SUB)
            o = _hgrn_step(*(ref[:, rs, :].reshape(nrow, D_A) for ref in (q_ref, f_ref, i_ref, g_ref)),
                           lb, hn, st_ref, sel_ref, cm_ref, b_ref, k_ref, SUB)
            o_ref[:, rs, :] = o.reshape(HGRN_SEQS, SUB, D_A)
            return carry

        lax.fori_loop(0, rows // SUB, body, 0)

    def emit_state():
        for n in range(HGRN_SEQS):
            for h in range(H_A):
                s_ref[n, h] = st_ref[n, h].T

    if has_s0:
        emit_state()
    else:
        pl.when(t == pl.num_programs(0) - 1)(emit_state)


def _hgrn(u3, s0, lb, hn):
    batch, seq, _ = u3.shape
    has_s0 = s0 is not None
    if has_s0:
        rows, grid = seq, (batch // HGRN_SEQS,)
        blk = lambda cb: pl.BlockSpec((HGRN_SEQS, rows, D_A), lambda i: (i, 0, cb))
        sblk = pl.BlockSpec((HGRN_SEQS, H_A, DK, DV), lambda i: (i, 0, 0, 0))
    else:
        rows, grid = 128, (seq // 128,)
        blk = lambda cb: pl.BlockSpec((HGRN_SEQS, rows, D_A), lambda i: (0, i, cb))
        sblk = pl.BlockSpec((HGRN_SEQS, H_A, DK, DV), lambda i: (0, 0, 0, 0))
    vec = pl.BlockSpec((1, D_A), lambda i: (0, 0))
    nrow = HGRN_SEQS * SUB
    scratch = [pltpu.VMEM((HGRN_SEQS, H_A, DV, DK), F32), pltpu.VMEM((SUB, SUB * DK), BF16),
               pltpu.VMEM((2, nrow, nrow), BF16), pltpu.VMEM((H_A, nrow, DK), F32),
               pltpu.VMEM((H_A, nrow, DK), F32)]
    if has_s0:
        scratch.append(pltpu.VMEM((4, HGRN_SEQS, SUB, D_A), F32))
    args = (u3, u3, u3, u3, lb, hn) + ((s0,) if has_s0 else ())
    return pl.pallas_call(
        functools.partial(_hgrn_kernel, rows=rows, has_s0=has_s0),
        grid=grid,
        in_specs=[blk(0), blk(1), blk(2), blk(3), vec, vec] + ([sblk] if has_s0 else []),
        out_specs=[blk(0), sblk],
        out_shape=[jax.ShapeDtypeStruct((batch, seq, D_A), F32),
                   jax.ShapeDtypeStruct((batch, H_A, DK, DV), F32)],
        scratch_shapes=scratch,
        compiler_params=_cparams(("arbitrary",)),
        name="hgrn_state" if has_s0 else "hgrn_scan",
    )(*args)


def _pool_groups(full_ref, base, tm, cnt_fn, wp_ref, ps_ref, step=1):
    outs = []
    for gi, w in enumerate(POOL_WINDOWS):
        ch = slice(POOL_GROUP * gi, POOL_GROUP * (gi + 1))
        v = full_ref[base:base + tm, ch]
        s = v
        for j in range(1, w):
            s = s + full_ref[base - j * step:base - j * step + tm, ch]
        d = s / cnt_fn(w) - v
        outs.append(jnp.dot(d.astype(BF16), wp_ref[gi], preferred_element_type=F32))
    return jnp.concatenate(outs, axis=1) * ps_ref[...]


def _pool_prompt_kernel(v_ref, prev_ref, wp_ref, ps_ref, o_ref, hist_ref, full_ref, *, tm):
    t = pl.program_id(1)
    prev = prev_ref[...]
    full_ref[0:16, :] = jnp.where(t == 0, 0.0, prev)
    full_ref[16:16 + tm, :] = v_ref[...]
    pos = (lax.broadcasted_iota(I32, (tm, 1), 0) + t * tm + 1).astype(F32)
    o_ref[...] = _pool_groups(full_ref, 16, tm, lambda w: jnp.minimum(pos, float(w)),
                              wp_ref, ps_ref)

    @pl.when(t == pl.num_programs(1) - 1)
    def _():
        hist_ref[...] = full_ref[16 + tm - POOL_HIST:16 + tm, :]


def _pool_prompt(u, batch, seq, w_pool_bf, pool_scale):
    tm = 256
    nt = seq // tm
    vcol = (4 * D_A) // D_B
    return pl.pallas_call(
        functools.partial(_pool_prompt_kernel, tm=tm),
        grid=(batch, nt),
        in_specs=[pl.BlockSpec((tm, D_B), lambda b, t: (b * nt + t, vcol)),
                  pl.BlockSpec((16, D_B),
                               lambda b, t: (jnp.maximum((b * nt + t) * (tm // 16) - 1, 0), vcol)),
                  pl.BlockSpec((len(POOL_WINDOWS), POOL_GROUP, POOL_GROUP), lambda b, t: (0, 0, 0)),
                  pl.BlockSpec((1, D_B), lambda b, t: (0, 0))],
        out_specs=[pl.BlockSpec((tm, D_B), lambda b, t: (b * nt + t, 0)),
                   pl.BlockSpec((None, POOL_HIST, D_B), lambda b, t: (b, 0, 0))],
        out_shape=[jax.ShapeDtypeStruct((batch * seq, D_B), F32),
                   jax.ShapeDtypeStruct((batch, POOL_HIST, D_B), F32)],
        scratch_shapes=[pltpu.VMEM((16 + tm, D_B), F32)],
        compiler_params=_cparams(("arbitrary", "arbitrary")),
        name="pool_prompt",
    )(u, u, w_pool_bf, pool_scale)


def _pool_sample_kernel(v_ref, hist_ref, wp_ref, ps_ref, o_ref, nh_ref, full_ref, *,
                        seq, batch, start):
    full_ref[0:batch, :] = jnp.zeros((batch, D_B), F32)
    full_ref[batch:16 * batch, :] = hist_ref[...].reshape(POOL_HIST * batch, D_B)
    full_ref[16 * batch:(16 + seq) * batch, :] = v_ref[...].reshape(seq * batch, D_B)

    def cnt(w):
        rows = [jnp.full((batch, 1), float(min(start + t + 1, w)), F32) for t in range(seq)]
        return jnp.concatenate(rows, axis=0)

    o = _pool_groups(full_ref, 16 * batch, seq * batch, cnt, wp_ref, ps_ref, step=batch)
    o_ref[...] = o.reshape(seq, batch, D_B)
    nh_ref[...] = full_ref[(16 + seq - POOL_HIST) * batch:(16 + seq) * batch, :].reshape(
        POOL_HIST, batch, D_B)


def _pool_sample(u_t, hist_t, start, w_pool_bf, pool_scale):
    seq, batch, _ = u_t.shape
    vcol = (4 * D_A) // D_B
    return pl.pallas_call(
        functools.partial(_pool_sample_kernel, seq=seq, batch=batch, start=start),
        grid=(1,),
        in_specs=[pl.BlockSpec((seq, batch, D_B), lambda i: (0, 0, vcol)),
                  pl.BlockSpec((POOL_HIST, batch, D_B), lambda i: (0, 0, 0)),
                  pl.BlockSpec((len(POOL_WINDOWS), POOL_GROUP, POOL_GROUP), lambda i: (0, 0, 0)),
                  pl.BlockSpec((1, D_B), lambda i: (0, 0))],
        out_specs=[pl.BlockSpec((seq, batch, D_B), lambda i: (0, 0, 0)),
                   pl.BlockSpec((POOL_HIST, batch, D_B), lambda i: (0, 0, 0))],
        out_shape=[jax.ShapeDtypeStruct((seq, batch, D_B), F32),
                   jax.ShapeDtypeStruct((POOL_HIST, batch, D_B), F32)],
        scratch_shapes=[pltpu.VMEM(((16 + seq) * batch, D_B), F32)],
        compiler_params=_cparams(("arbitrary",)),
        name="pool_sample",
    )(u_t, hist_t, w_pool_bf, pool_scale)


LANES = 128
SLAB = D_MODEL // LANES


def _store_slabs(ref, value):
    n = value.shape[0]
    for j in range(SLAB):
        ref[pl.ds(j, n, stride=SLAB), :] = value[:, LANES * j:LANES * (j + 1)]


def _load_slabs(ref, n):
    return [ref[pl.ds(j, n, stride=SLAB), :] for j in range(SLAB)]


def _slab_rows(row):
    return pl.ds(pl.multiple_of(row * SLAB, SLAB), SLAB)


def _split_dot(a, w_hi, w_lo):
    a_hi = a.astype(BF16)
    a_lo = (a - a_hi.astype(F32)).astype(BF16)
    return (jnp.dot(a_hi, w_hi, preferred_element_type=F32)
            + jnp.dot(a_hi, w_lo, preferred_element_type=F32)
            + jnp.dot(a_lo, w_hi, preferred_element_type=F32))


def _outproj_kernel(x_ref, oa_ref, ob_ref, g1_ref, sh2_ref, sc2_ref, n2_ref, woa_ref, wob_ref,
                    wrh_ref, wrl_ref, br_ref, *rest):
    x1_ref, h2_ref, idx_ref, prob_ref = rest[-4:]
    mix =(jnp.dot(oa_ref[...].astype(BF16), woa_ref[...], preferred_element_type=F32)
           + jnp.dot(ob_ref[...].astype(BF16), wob_ref[...], preferred_element_type=F32))
    x1 = x_ref[...] + g1_ref[...] * mix
    x1_ref[...] = x1
    h2 = x1 * lax.rsqrt(jnp.mean(x1 * x1, axis=-1, keepdims=True) + EPS) * n2_ref[...]
    h2 = h2 * (1.0 + sc2_ref[...]) + sh2_ref[...]
    _store_slabs(h2_ref, h2)

    logits = _split_dot(h2, wrh_ref[...], wrl_ref[...]) + br_ref[...]
    tm = logits.shape[0]
    lane = lax.broadcasted_iota(I32, (tm, N_EXPERTS), 1)
    kcol = lax.broadcasted_iota(I32, (tm, TOP_K), 1)
    vals = jnp.zeros((tm, TOP_K), F32)
    idxs = jnp.zeros((tm, TOP_K), I32)
    for k in range(TOP_K):
        m = jnp.max(logits, axis=-1, keepdims=True)
        am = jnp.min(jnp.where(logits == m, lane, N_EXPERTS), axis=-1, keepdims=True)
        vals = jnp.where(kcol == k, m, vals)
        idxs = jnp.where(kcol == k, am, idxs)
        logits = jnp.where(lane == am, -jnp.inf, logits)
    e = jnp.exp(vals - vals[:, 0:1])
    prob_ref[...] = e / jnp.sum(e, axis=-1, keepdims=True)
    idx_ref[...] = idxs


def _outproj(x, o_a, o_b, mod, per_token, seq, norm2, w_out_bf, wr_hi, wr_lo, b_router,
             n_total, tile0, joint=None):
    n = x.shape[0]
    tm = min(256, n)
    tps = max(seq // tm, 1)
    row = lambda width: pl.BlockSpec((tm, width), lambda i: (i, 0))
    jrow = lambda width: pl.BlockSpec((tm, width), lambda i: (i + tile0, 0))
    const = lambda shape: pl.BlockSpec(shape, lambda i: (0,) * len(shape))
    joint = () if joint is None else tuple(joint)
    n_in = 12
    return pl.pallas_call(
        _outproj_kernel,
        grid=(n // tm,),
        in_specs=[row(D_MODEL), row(D_A), row(D_B),
                  _mod_spec(per_token, tm, tps, 2),
                  _mod_spec(per_token, tm, tps, 3),
                  _mod_spec(per_token, tm, tps, 4),
                  const((1, D_MODEL)),
                  pl.BlockSpec((D_A, D_MODEL), lambda i: (0, 0)),
                  pl.BlockSpec((D_B, D_MODEL), lambda i: (1, 0)),
                  const((D_MODEL, N_EXPERTS)), const((D_MODEL, N_EXPERTS)), const((1, N_EXPERTS))]
                 + [pl.BlockSpec(memory_space=pl.ANY)] * len(joint),
        out_specs=[row(D_MODEL), pl.BlockSpec((tm * SLAB, LANES), lambda i: (i + tile0, 0)),
                   jrow(TOP_K), jrow(TOP_K)],
        out_shape=[jax.ShapeDtypeStruct((n, D_MODEL), F32),
                   jax.ShapeDtypeStruct((n_total * SLAB, LANES), F32),
                   jax.ShapeDtypeStruct((n_total, TOP_K), I32),
                   jax.ShapeDtypeStruct((n_total, TOP_K), F32)],
        input_output_aliases={n_in + j: 1 + j for j in range(len(joint))},
        compiler_params=_cparams(("arbitrary",)),
        name="outproj",
    )(x, o_a, o_b, mod, mod, mod, norm2.reshape(1, D_MODEL), w_out_bf, w_out_bf,
      wr_hi, wr_lo, b_router.reshape(1, N_EXPERTS), *joint)


def _rank_kernel(idx_ref, dest_ref, cnt_ref, carry_ref, *, te):
    phase = pl.program_id(0)
    i = pl.program_id(1)
    idx = idx_ref[...]
    lane = lax.broadcasted_iota(I32, (te, N_EXPERTS), 1)
    onehots = [(idx[:, k:k + 1] == lane) for k in range(TOP_K)]
    member = jnp.zeros((te, N_EXPERTS), F32)
    for oh in onehots:
        member = member + oh.astype(F32)
    colsum = jnp.sum(member, axis=0, keepdims=True)

    @pl.when((phase == 0) & (i == 0))
    def _():
        carry_ref[...] = jnp.zeros_like(carry_ref)

    @pl.when(phase == 0)
    def _():
        carry_ref[...] += colsum

    @pl.when((phase == 1) & (i == 0))
    def _():
        cnt = carry_ref[...]
        cnt_ref[...] = cnt.astype(I32)
        tiles = jnp.floor((cnt + (ROW_TILE - 1)) * (1.0 / ROW_TILE))
        r = lax.broadcasted_iota(I32, (N_EXPERTS, N_EXPERTS), 0)
        c = lax.broadcasted_iota(I32, (N_EXPERTS, N_EXPERTS), 1)
        before = (r < c).astype(BF16)
        carry_ref[...] = ROW_TILE * jnp.dot(tiles.astype(BF16), before, preferred_element_type=F32)

    @pl.when(phase == 1)
    def _():
        r = lax.broadcasted_iota(I32, (te, te), 0)
        c = lax.broadcasted_iota(I32, (te, te), 1)
        earlier = (r > c).astype(BF16)
        base = carry_ref[...] + jnp.dot(earlier, member.astype(BF16), preferred_element_type=F32)
        kcol = lax.broadcasted_iota(I32, (te, TOP_K), 1)
        dest = jnp.zeros((te, TOP_K), F32)
        for k, oh in enumerate(onehots):
            dk = jnp.sum(jnp.where(oh, base, 0.0), axis=-1, keepdims=True)
            dest = jnp.where(kcol == k, dk, dest)
        dest_ref[...] = dest.astype(I32)
        carry_ref[...] += colsum


def _rank(top_idx):
    n = top_idx.shape[0]
    te = 512
    return pl.pallas_call(
        functools.partial(_rank_kernel, te=te),
        grid=(2, n // te),
        in_specs=[pl.BlockSpec((te, TOP_K), lambda p, i: (i, 0))],
        out_specs=[pl.BlockSpec((te, TOP_K), lambda p, i: (p * i, 0)),
                   pl.BlockSpec((1, N_EXPERTS), lambda p, i: (0, 0))],
        out_shape=[jax.ShapeDtypeStruct((n, TOP_K), I32),
                   jax.ShapeDtypeStruct((1, N_EXPERTS), I32)],
        scratch_shapes=[pltpu.VMEM((1, N_EXPERTS), F32)],
        compiler_params=_cparams(("arbitrary", "arbitrary")),
        name="rank",
    )(top_idx)


def _dispatch_kernel(last_ref, ntile_ref, used_ref, dest_ref, h_ref, xs_ref, zero_ref, sem, zsem,
                     *, n_tiles):
    i = pl.program_id(0)

    @pl.when(i == 0)
    def _():
        zero_ref[...] = jnp.zeros_like(zero_ref)

        def clear(tile):
            rows = pl.ds(pl.multiple_of(tile * (ROW_TILE * SLAB), ROW_TILE * SLAB), ROW_TILE * SLAB)
            return pltpu.make_async_copy(zero_ref, xs_ref.at[rows], zsem)

        def over_tail(fn):
            def body(t, carry):
                fn(clear(t))
                return carry
            lax.fori_loop(used_ref[0], n_tiles, body, 0)

        for fn in (lambda c: c.start(), lambda c: c.wait()):
            for e in range(N_EXPERTS):
                @pl.when(ntile_ref[e] > 0)
                def _():
                    fn(clear(last_ref[e]))
            over_tail(fn)

    def body(r, carry):
        for k in range(TOP_K):
            d = dest_ref[r * TOP_K + k]
            pltpu.make_async_copy(h_ref.at[_slab_rows(r)], xs_ref.at[_slab_rows(d)], sem).start(
                priority=k % 2)
        return carry

    lax.fori_loop(0, TOK_TILE, body, 0, unroll=2)
    for k in range(TOP_K):
        pltpu.make_async_copy(h_ref, xs_ref.at[pl.ds(0, TOK_TILE * SLAB)], sem).wait()


def _dispatch(h2, dest_flat, last_tile, ntiles, used, n_tiles):
    n = h2.shape[0] // SLAB
    grid_spec = pltpu.PrefetchScalarGridSpec(
        num_scalar_prefetch=3,
        grid=(n // TOK_TILE,),
        in_specs=[pl.BlockSpec((TOK_TILE * TOP_K,), lambda i, *_: (i,), memory_space=pltpu.SMEM),
                  pl.BlockSpec((TOK_TILE * SLAB, LANES), lambda i, *_: (i, 0))],
        out_specs=pl.BlockSpec(memory_space=pl.ANY),
        scratch_shapes=[pltpu.VMEM((ROW_TILE * SLAB, LANES), F32),
                        pltpu.SemaphoreType.DMA(()), pltpu.SemaphoreType.DMA(())],
    )
    return pl.pallas_call(
        functools.partial(_dispatch_kernel, n_tiles=n_tiles),
        grid_spec=grid_spec,
        out_shape=jax.ShapeDtypeStruct((n_tiles * ROW_TILE * SLAB, LANES), F32),
        compiler_params=_cparams(("arbitrary",)),
        name="dispatch",
    )(last_tile, ntiles, used, dest_flat, h2)


def _experts_kernel(te_ref, used_ref, xs_ref, wgu_ref, bgu_ref, wd_ref, bd_ref, ys_ref,
                    wgu_bf, wd_bf, act_bf):
    i = pl.program_id(0)
    prev = te_ref[jnp.maximum(i - 1, 0)]
    fresh = (i == 0) | (te_ref[i] != prev)

    @pl.when(fresh)
    def _():
        wgu_bf[...] = wgu_ref[...].astype(BF16)
        wd_bf[...] = wd_ref[...].astype(BF16)

    @pl.when(i < used_ref[0])
    def _():
        x = jnp.concatenate([p.astype(BF16) for p in _load_slabs(xs_ref, ROW_TILE)], axis=1)
        cw = D_FF // FF_CHUNKS
        for c in range(FF_CHUNKS):
            gate = jnp.dot(x, wgu_bf[:, c * cw:(c + 1) * cw], preferred_element_type=F32)
            gate = jnp.minimum(gate + bgu_ref[:, c * cw:(c + 1) * cw], SWIGLU_LIMIT)
            up = jnp.dot(x, wgu_bf[:, D_FF + c * cw:D_FF + (c + 1) * cw], preferred_element_type=F32)
            up = jnp.clip(up + bgu_ref[:, D_FF + c * cw:D_FF + (c + 1) * cw], -SWIGLU_LIMIT, SWIGLU_LIMIT)
            act = gate * jax.nn.sigmoid(SWIGLU_ALPHA * gate) * (up + 1.0)
            act_bf[:, c * cw:(c + 1) * cw] = act.astype(BF16)
        ow = D_MODEL // FF_CHUNKS
        for c in range(FF_CHUNKS):
            out = jnp.dot(act_bf[...], wd_bf[:, c * ow:(c + 1) * ow], preferred_element_type=F32)
            out = out + bd_ref[:, c * ow:(c + 1) * ow]
            for j in range(ow // LANES):
                ys_ref[pl.ds(c * (ow // LANES) + j, ROW_TILE, stride=SLAB), :] = (
                    out[:, LANES * j:LANES * (j + 1)])

    @pl.when(i >= used_ref[0])
    def _():
        ys_ref[...] = jnp.zeros_like(ys_ref)


def _experts(xs, tile_expert, used, w_gu, b_gu, w_down, b_down):
    rows = xs.shape[0] // SLAB
    nt = rows // ROW_TILE

    def tile(i, te, used):
        return (jnp.minimum(i, used[0] - 1), 0)

    grid_spec = pltpu.PrefetchScalarGridSpec(
        num_scalar_prefetch=2,
        grid=(nt,),
        in_specs=[pl.BlockSpec((ROW_TILE * SLAB, LANES), tile),
                  pl.BlockSpec((None, D_MODEL, 2 * D_FF), lambda i, te, used: (te[i], 0, 0)),
                  pl.BlockSpec((None, 1, 2 * D_FF), lambda i, te, used: (te[i], 0, 0)),
                  pl.BlockSpec((None, D_FF, D_MODEL), lambda i, te, used: (te[i], 0, 0)),
                  pl.BlockSpec((None, 1, D_MODEL), lambda i, te, used: (te[i], 0, 0))],
        out_specs=pl.BlockSpec((ROW_TILE * SLAB, LANES), lambda i, te, used: (i, 0)),
        scratch_shapes=[pltpu.VMEM((D_MODEL, 2 * D_FF), BF16), pltpu.VMEM((D_FF, D_MODEL), BF16),
                        pltpu.VMEM((ROW_TILE, D_FF), BF16)],
    )
    return pl.pallas_call(
        _experts_kernel,
        grid_spec=grid_spec,
        out_shape=jax.ShapeDtypeStruct((rows * SLAB, LANES), F32),
        compiler_params=_cparams(("arbitrary",)),
        name="experts",
    )(tile_expert, used, xs, w_gu, b_gu.reshape(N_EXPERTS, 1, 2 * D_FF), w_down,
      b_down.reshape(N_EXPERTS, 1, D_MODEL))


X_RING = 3


def _mlp_kernel(te_ref, used_ref, grp_ref, nxt_ref, xs_ref, wgu_ref, bgu_ref, wd_ref, bd_ref, ys_ref,
                wgu_f, wd_f, wgu_bf, wd_bf, act_bf, xring, wsem, xsem):
    i = pl.program_id(0)
    used = used_ref[0]
    tile_rows = ROW_TILE * SLAB

    def weights(e, ws):
        return (pltpu.make_async_copy(wgu_ref.at[e], wgu_f.at[ws], wsem.at[ws, 0]),
                pltpu.make_async_copy(wd_ref.at[e], wd_f.at[ws], wsem.at[ws, 1]))

    def x_tile(t):
        t = jnp.asarray(t, I32)
        rows = pl.ds(pl.multiple_of(t * tile_rows, tile_rows), tile_rows)
        return pltpu.make_async_copy(xs_ref.at[rows], xring.at[t % X_RING], xsem.at[t % X_RING])

    @pl.when(i == 0)
    def _():
        for cp in weights(te_ref[0], 0):
            cp.start()
        for t in range(X_RING - 1):
            @pl.when(t < used)
            def _():
                x_tile(t).start()

    @pl.when(i + X_RING - 1 < used)
    def _():
        x_tile(i + X_RING - 1).start()

    ws = grp_ref[i] % 2
    fresh = (i == 0) | (te_ref[i] != te_ref[jnp.maximum(i - 1, 0)])

    @pl.when(fresh & (i < used))
    def _():
        for cp in weights(te_ref[i], ws):
            cp.wait()

        @pl.when(nxt_ref[i] >= 0)
        def _():
            for cp in weights(nxt_ref[i], 1 - ws):
                cp.start()

        wgu_bf[...] = wgu_f[ws].astype(BF16)
        wd_bf[...] = wd_f[ws].astype(BF16)

    @pl.when(i < used)
    def _():
        x_tile(i).wait()
        x = jnp.concatenate([p.astype(BF16) for p in _load_slabs(xring.at[i % X_RING], ROW_TILE)],
                            axis=1)
        cw = D_FF // FF_CHUNKS
        for c in range(FF_CHUNKS):
            gate = jnp.dot(x, wgu_bf[:, c * cw:(c + 1) * cw], preferred_element_type=F32)
            gate = jnp.minimum(gate + bgu_ref[:, c * cw:(c + 1) * cw], SWIGLU_LIMIT)
            up = jnp.dot(x, wgu_bf[:, D_FF + c * cw:D_FF + (c + 1) * cw], preferred_element_type=F32)
            up = jnp.clip(up + bgu_ref[:, D_FF + c * cw:D_FF + (c + 1) * cw], -SWIGLU_LIMIT, SWIGLU_LIMIT)
            act = gate * jax.nn.sigmoid(SWIGLU_ALPHA * gate) * (up + 1.0)
            act_bf[:, c * cw:(c + 1) * cw] = act.astype(BF16)
        ow = D_MODEL // FF_CHUNKS
        for c in range(FF_CHUNKS):
            out = jnp.dot(act_bf[...], wd_bf[:, c * ow:(c + 1) * ow], preferred_element_type=F32)
            out = out + bd_ref[:, c * ow:(c + 1) * ow]
            for j in range(ow // LANES):
                ys_ref[pl.ds(c * (ow // LANES) + j, ROW_TILE, stride=SLAB), :] = (
                    out[:, LANES * j:LANES * (j + 1)])

    @pl.when(i >= used)
    def _():
        ys_ref[...] = jnp.zeros_like(ys_ref)


def _mlp(xs, tile_expert, used, grp, nxt, w_gu, b_gu, w_down, b_down):
    rows = xs.shape[0] // SLAB
    nt = rows // ROW_TILE
    ex = lambda i, te, *_: (te[i], 0, 0)
    grid_spec = pltpu.PrefetchScalarGridSpec(
        num_scalar_prefetch=4,
        grid=(nt,),
        in_specs=[pl.BlockSpec(memory_space=pl.ANY),
                  pl.BlockSpec(memory_space=pl.ANY),
                  pl.BlockSpec((None, 1, 2 * D_FF), ex),
                  pl.BlockSpec(memory_space=pl.ANY),
                  pl.BlockSpec((None, 1, D_MODEL), ex)],
        out_specs=pl.BlockSpec((ROW_TILE * SLAB, LANES), lambda i, *_: (i, 0)),
        scratch_shapes=[pltpu.VMEM((2, D_MODEL, 2 * D_FF), F32), pltpu.VMEM((2, D_FF, D_MODEL), F32),
                        pltpu.VMEM((D_MODEL, 2 * D_FF), BF16), pltpu.VMEM((D_FF, D_MODEL), BF16),
                        pltpu.VMEM((ROW_TILE, D_FF), BF16),
                        pltpu.VMEM((X_RING, ROW_TILE * SLAB, LANES), F32),
                        pltpu.SemaphoreType.DMA((2, 2)), pltpu.SemaphoreType.DMA((X_RING,))],
    )
    return pl.pallas_call(
        _mlp_kernel,
        grid_spec=grid_spec,
        out_shape=jax.ShapeDtypeStruct((rows * SLAB, LANES), F32),
        compiler_params=_cparams(("arbitrary",)),
        name="experts",
    )(tile_expert, used, grp, nxt, xs, w_gu, b_gu.reshape(N_EXPERTS, 1, 2 * D_FF), w_down,
      b_down.reshape(N_EXPERTS, 1, D_MODEL))


def _combine_kernel(dest_ref, next_ref, ys_ref, x1_ref, p_ref, g2_ref, fn_ref, y_ref, buf_ref, sem,
                    *, last):
    i = pl.program_id(0)
    slot = i % 2

    def issue(d_ref, sl):
        def body(r, carry):
            for k in range(TOP_K):
                d = d_ref[r * TOP_K + k]
                pltpu.make_async_copy(ys_ref.at[_slab_rows(d)], buf_ref.at[sl, k, _slab_rows(r)],
                                      sem.at[sl]).start(priority=k % 2)
            return carry
        lax.fori_loop(0, TOK_TILE, body, 0, unroll=2)

    @pl.when(i == 0)
    def _():
        issue(dest_ref, 0)

    @pl.when(i + 1 < pl.num_programs(0))
    def _():
        issue(next_ref, 1 - slot)

    for k in range(TOP_K):
        pltpu.make_async_copy(ys_ref.at[pl.ds(0, TOK_TILE * SLAB)], buf_ref.at[slot, k],
                              sem.at[slot]).wait()

    p = p_ref[...]
    groups = None
    for k in range(TOP_K):
        part = [p[:, k:k + 1] * g for g in _load_slabs(buf_ref.at[slot, k], TOK_TILE)]
        groups = part if groups is None else [a + b for a, b in zip(groups, part)]
    moe = jnp.concatenate(groups, axis=1)
    x2 = x1_ref[...] + g2_ref[...] * moe
    if last:
        x2 = x2 * lax.rsqrt(jnp.mean(x2 * x2, axis=-1, keepdims=True) + EPS) * fn_ref[...]
    y_ref[...] = x2


def _combine(ys, dest_flat, x1, probs, mod, per_token, seq, tile0, final_norm, last):
    n = x1.shape[0]
    tm = TOK_TILE
    tps = max(seq // tm, 1)
    nt = n // tm
    return pl.pallas_call(
        functools.partial(_combine_kernel, last=last),
        grid=(nt,),
        in_specs=[pl.BlockSpec((tm * TOP_K,), lambda i: (i + tile0,), memory_space=pltpu.SMEM),
                  pl.BlockSpec((tm * TOP_K,), lambda i: (jnp.minimum(i + 1, nt - 1) + tile0,),
                               memory_space=pltpu.SMEM),
                  pl.BlockSpec(memory_space=pl.ANY),
                  pl.BlockSpec((tm, D_MODEL), lambda i: (i, 0)),
                  pl.BlockSpec((tm, TOP_K), lambda i: (i + tile0, 0)),
                  _mod_spec(per_token, tm, tps, 5),
                  pl.BlockSpec((1, D_MODEL), lambda i: (0, 0))],
        out_specs=pl.BlockSpec((tm, D_MODEL), lambda i: (i, 0)),
        out_shape=jax.ShapeDtypeStruct((n, D_MODEL), F32),
        scratch_shapes=[pltpu.VMEM((2, TOP_K, tm * SLAB, LANES), F32),
                        pltpu.SemaphoreType.DMA((2,))],
        compiler_params=_cparams(("arbitrary",)),
        name="combine",
    )(dest_flat, dest_flat, ys, x1, probs, mod, final_norm.reshape(1, D_MODEL))


def _invert_kernel(dest_ref, src0_ref, dst0_ref, src_ref, dst_ref, sem, *, n_tok):
    i = pl.program_id(0)
    blk = dest_ref.shape[0]

    @pl.when(i == 0)
    def _():
        copies = [pltpu.make_async_copy(src0_ref, src_ref, sem.at[0]),
                  pltpu.make_async_copy(dst0_ref, dst_ref, sem.at[1])]
        for cp in copies:
            cp.start()
        for cp in copies:
            cp.wait()

    def body(a, carry):
        g = i * blk + a
        d = dest_ref[a]
        tok = g >> (TOP_K.bit_length() - 1)
        src_ref[d] = tok
        dst_ref[d] = (g & (TOP_K - 1)) * n_tok + tok
        return carry

    lax.fori_loop(0, blk, body, 0, unroll=8)


def _invert(dest_flat, n_tok, n_rows):
    blk = TOK_TILE * TOP_K
    whole = pl.BlockSpec(memory_space=pltpu.SMEM)
    src0 = jnp.zeros((n_rows,), I32)
    dst0 = n_tok * TOP_K + jnp.arange(n_rows, dtype=I32)
    return pl.pallas_call(
        functools.partial(_invert_kernel, n_tok=n_tok),
        grid=(dest_flat.shape[0] // blk,),
        in_specs=[pl.BlockSpec((blk,), lambda i: (i,), memory_space=pltpu.SMEM),
                  pl.BlockSpec(memory_space=pl.ANY), pl.BlockSpec(memory_space=pl.ANY)],
        out_specs=[whole, whole],
        out_shape=[jax.ShapeDtypeStruct((n_rows,), I32), jax.ShapeDtypeStruct((n_rows,), I32)],
        scratch_shapes=[pltpu.SemaphoreType.DMA((2,))],
        compiler_params=_cparams(("arbitrary",)),
        name="invert",
    )(dest_flat, src0, dst0)


FF_CHUNKS = 4


def _moe_kernel(src_ref, dst_ref, te_ref, used_ref, h_ref, wgu_ref, bgu_ref, wd_ref, bd_ref, y_ref,
                wgu_bf, wd_bf, act_bf, buf, gsem, ssem):
    i = pl.program_id(0)
    used = used_ref[0]
    slot = i % 2
    tile_rows = ROW_TILE * SLAB
    n_tiles = te_ref.shape[0]

    def gather(tile, sl, rows):
        for r in rows:
            tok = src_ref[tile * ROW_TILE + r]
            pltpu.make_async_copy(h_ref.at[_slab_rows(tok)], buf.at[sl, _slab_rows(r)],
                                  gsem.at[sl]).start(priority=1)

    def scatter(tile, sl, rows):
        for r in rows:
            out = dst_ref[tile * ROW_TILE + r]
            pltpu.make_async_copy(buf.at[2 + sl, _slab_rows(r)], y_ref.at[_slab_rows(out)],
                                  ssem.at[sl]).start()

    def wait_gather(sl):
        pltpu.make_async_copy(h_ref.at[pl.ds(0, tile_rows)], buf.at[sl], gsem.at[sl]).wait()

    def wait_scatter(sl):
        pltpu.make_async_copy(buf.at[2 + sl], y_ref.at[pl.ds(0, tile_rows)], ssem.at[sl]).wait()

    @pl.when(i == 0)
    def _():
        buf[2] = jnp.zeros(buf.shape[1:], F32)
        buf[3] = jnp.zeros(buf.shape[1:], F32)
        tail = y_ref.shape[0] - tile_rows
        pltpu.make_async_copy(buf.at[2], y_ref.at[pl.ds(tail, tile_rows)], ssem.at[0]).start()
        gather(0, 0, range(ROW_TILE))

    prev = te_ref[jnp.maximum(i - 1, 0)]
    fresh = (i == 0) | (te_ref[jnp.minimum(i, n_tiles - 1)] != prev)

    @pl.when(fresh & (i < used))
    def _():
        wgu_bf[...] = wgu_ref[...].astype(BF16)
        wd_bf[...] = wd_ref[...].astype(BF16)

    per = ROW_TILE // FF_CHUNKS
    cw = D_FF // FF_CHUNKS

    @pl.when(i < used)
    def _():
        wait_gather(slot)
        nxt = jnp.minimum(i + 1, n_tiles - 1)
        prv = jnp.where(i >= 1, i - 1, n_tiles)
        half = per // 2
        for c in range(FF_CHUNKS):
            gather(nxt, 1 - slot, range(c * per, (c + 1) * per))
            scatter(prv, 1 - slot, range(c * half, (c + 1) * half))
            x = jnp.concatenate([p.astype(BF16) for p in _load_slabs(buf.at[slot], ROW_TILE)], axis=1)
            gate = jnp.dot(x, wgu_bf[:, c * cw:(c + 1) * cw], preferred_element_type=F32)
            gate = jnp.minimum(gate + bgu_ref[:, c * cw:(c + 1) * cw], SWIGLU_LIMIT)
            up = jnp.dot(x, wgu_bf[:, D_FF + c * cw:D_FF + (c + 1) * cw], preferred_element_type=F32)
            up = jnp.clip(up + bgu_ref[:, D_FF + c * cw:D_FF + (c + 1) * cw], -SWIGLU_LIMIT, SWIGLU_LIMIT)
            act = gate * jax.nn.sigmoid(SWIGLU_ALPHA * gate) * (up + 1.0)
            act_bf[:, c * cw:(c + 1) * cw] = act.astype(BF16)
        ow = D_MODEL // FF_CHUNKS
        base = FF_CHUNKS * half
        wait_scatter(slot)
        for c in range(FF_CHUNKS):
            scatter(prv, 1 - slot, range(base + c * half, base + (c + 1) * half))
            out = jnp.dot(act_bf[...], wd_bf[:, c * ow:(c + 1) * ow], preferred_element_type=F32)
            out = out + bd_ref[:, c * ow:(c + 1) * ow]
            for j in range(ow // LANES):
                buf[2 + slot, pl.ds(c * (ow // LANES) + j, ROW_TILE, stride=SLAB), :] = (
                    out[:, LANES * j:LANES * (j + 1)])

    @pl.when(i == used)
    def _():
        wait_gather(slot)
        wait_scatter(slot)
        scatter(i - 1, 1 - slot, range(ROW_TILE))
        wait_scatter(1 - slot)


def _moe(h2, src, dst, tile_expert, used, w_gu, b_gu, w_down, b_down, n_tok):
    n_tiles = tile_expert.shape[0]
    n_rows = n_tiles * ROW_TILE
    ex = lambda i, src, dst, te, used: (te[jnp.minimum(i, n_tiles - 1)], 0, 0)
    grid_spec = pltpu.PrefetchScalarGridSpec(
        num_scalar_prefetch=4,
        grid=(n_tiles + 1,),
        in_specs=[pl.BlockSpec(memory_space=pl.ANY),
                  pl.BlockSpec((None, D_MODEL, 2 * D_FF), ex),
                  pl.BlockSpec((None, 1, 2 * D_FF), ex),
                  pl.BlockSpec((None, D_FF, D_MODEL), ex),
                  pl.BlockSpec((None, 1, D_MODEL), ex)],
        out_specs=pl.BlockSpec(memory_space=pl.ANY),
        scratch_shapes=[pltpu.VMEM((D_MODEL, 2 * D_FF), BF16), pltpu.VMEM((D_FF, D_MODEL), BF16),
                        pltpu.VMEM((ROW_TILE, D_FF), BF16),
                        pltpu.VMEM((4, ROW_TILE * SLAB, LANES), F32),
                        pltpu.SemaphoreType.DMA((2,)), pltpu.SemaphoreType.DMA((2,))],
    )
    return pl.pallas_call(
        _moe_kernel,
        grid_spec=grid_spec,
        out_shape=jax.ShapeDtypeStruct(((n_tok * TOP_K + n_rows + 2 * ROW_TILE) * SLAB, LANES), F32),
        compiler_params=_cparams(("arbitrary",)),
        name="moe",
    )(src, dst, tile_expert, used, h2, w_gu, b_gu.reshape(N_EXPERTS, 1, 2 * D_FF), w_down,
      b_down.reshape(N_EXPERTS, 1, D_MODEL))


def _merge_kernel(y0_ref, y1_ref, y2_ref, y3_ref, x1_ref, p_ref, g2_ref, fn_ref, o_ref, *, last):
    p = p_ref[...]
    tm = p.shape[0]
    groups = None
    for k, ref in enumerate((y0_ref, y1_ref, y2_ref, y3_ref)):
        part = [p[:, k:k + 1] * g for g in _load_slabs(ref, tm)]
        groups = part if groups is None else [a + b for a, b in zip(groups, part)]
    x2 = x1_ref[...] + g2_ref[...] * jnp.concatenate(groups, axis=1)
    if last:
        x2 = x2 * lax.rsqrt(jnp.mean(x2 * x2, axis=-1, keepdims=True) + EPS) * fn_ref[...]
    o_ref[...] = x2


def _merge(y, x1, probs, mod, per_token, seq, tile0, n_tok, final_norm, last):
    n = x1.shape[0]
    tm = TOK_TILE
    tps = max(seq // tm, 1)
    choice = lambda k: pl.BlockSpec((tm * SLAB, LANES), lambda i: (k * (n_tok // tm) + tile0 + i, 0))
    return pl.pallas_call(
        functools.partial(_merge_kernel, last=last),
        grid=(n // tm,),
        in_specs=[choice(0), choice(1), choice(2), choice(3),
                  pl.BlockSpec((tm, D_MODEL), lambda i: (i, 0)),
                  pl.BlockSpec((tm, TOP_K), lambda i: (i + tile0, 0)),
                  _mod_spec(per_token, tm, tps, 5),
                  pl.BlockSpec((1, D_MODEL), lambda i: (0, 0))],
        out_specs=pl.BlockSpec((tm, D_MODEL), lambda i: (i, 0)),
        out_shape=jax.ShapeDtypeStruct((n, D_MODEL), F32),
        compiler_params=_cparams(("arbitrary",)),
        name="merge",
    )(y, y, y, y, x1, probs, mod, final_norm.reshape(1, D_MODEL))


def kernel(x_prompt, x_sample, state_hgrn, state_pool, c_prompt, c_sample, w_ada, b_ada, norm1,
           norm2, w_in, lower_bounds, hgrn_norm, w_pool, pool_scale, w_out, w_router, b_router,
           w_gu, b_gu, w_down, b_down, final_norm):
    bp, seq_p, _ = x_prompt.shape
    bs, seq_s, _ = x_sample.shape
    np_, ns = bp * seq_p, bs * seq_s
    depth = w_ada.shape[0]
    past_len = 16384

    lbs = jnp.cumsum(jax.nn.softmax(lower_bounds.astype(F32), axis=0), axis=0)
    xp = x_prompt.reshape(np_, D_MODEL)
    xs_ = x_sample.reshape(ns, D_MODEL)
    c_all = jnp.concatenate([c_prompt, c_sample], axis=0)

    sp_l, hp_l, ss_l, hs_l = [], [], [], []
    for l in range(depth):
        w_in_bf = w_in[l].astype(BF16)
        w_out_bf = w_out[l].astype(BF16)
        w_pool_bf = w_pool[l].astype(BF16)
        wr_hi = w_router[l].astype(BF16)
        wr_lo = (w_router[l] - wr_hi.astype(F32)).astype(BF16)
        lb = lbs[l].reshape(1, D_A)
        hn = hgrn_norm[l].reshape(1, D_A)
        psc = pool_scale[l].reshape(1, D_B)

        mod = _ada(c_all, w_ada[l], b_ada[l])
        mod_p = mod[:bp].reshape(bp, 1, 6 * D_MODEL)
        mod_s = jnp.repeat(mod[bp:], seq_s, axis=0)

        up = _inproj(xp, mod_p, False, seq_p, norm1[l], w_in_bf)
        us = _inproj(xs_, mod_s, True, seq_s, norm1[l], w_in_bf)
        us3 = us.reshape(bs, seq_s, D_IN)

        oa_p, s_p = _hgrn(up.reshape(bp, seq_p, D_IN), None, lb, hn)
        oa_p = oa_p.reshape(np_, D_A)
        oa_s, s_s = _hgrn(us3, state_hgrn[l], lb, hn)
        ob_p, h_p = _pool_prompt(up, bp, seq_p, w_pool_bf, psc)
        ob_t, h_t = _pool_sample(us3.transpose(1, 0, 2), state_pool[l].transpose(1, 0, 2), past_len,
                                 w_pool_bf, psc)
        ob_s, h_s = ob_t.transpose(1, 0, 2), h_t.transpose(1, 0, 2)

        n_tok = np_ + ns
        n_tiles = (n_tok * TOP_K) // ROW_TILE + N_EXPERTS
        x1p, *joint = _outproj(xp, oa_p, ob_p, mod_p, False, seq_p, norm2[l], w_out_bf,
                               wr_hi, wr_lo, b_router[l], n_tok, 0)
        x1s, h2, top_idx, probs = _outproj(
            xs_, oa_s.reshape(ns, D_A), ob_s.reshape(ns, D_B), mod_s, True, seq_s, norm2[l],
            w_out_bf, wr_hi, wr_lo, b_router[l], n_tok, np_ // TOK_TILE, joint)

        dest, counts = _rank(top_idx)
        dest_flat = dest.reshape(n_tok * TOP_K)
        tiles_e = (counts.reshape(N_EXPERTS) + ROW_TILE - 1) // ROW_TILE
        ends = jnp.cumsum(tiles_e)
        used = ends[-1:].astype(I32)
        tile_expert = jnp.minimum(
            jnp.sum(jnp.arange(n_tiles, dtype=I32)[:, None] >= ends[None, :], axis=1),
            N_EXPERTS - 1).astype(I32)

        last_tile = (ends - 1).astype(I32)

        xsort = _dispatch(h2, dest_flat, last_tile, tiles_e.astype(I32), used, n_tiles)
        grp = jnp.cumsum(jnp.concatenate(
            [jnp.zeros((1,), I32), (tile_expert[1:] != tile_expert[:-1]).astype(I32)]))
        eids = jnp.arange(N_EXPERTS, dtype=I32)
        later = (eids[None, :] > eids[:, None]) & (tiles_e > 0)[None, :]
        nxt_e = jnp.min(jnp.where(later, eids[None, :], N_EXPERTS), axis=1)
        nxt = jnp.where(nxt_e < N_EXPERTS, nxt_e, -1)[tile_expert].astype(I32)
        ysort = _mlp(xsort, tile_expert, used, grp.astype(I32), nxt, w_gu[l], b_gu[l],
                     w_down[l], b_down[l])

        last = l == depth - 1
        xp = _combine(ysort, dest_flat, x1p, probs, mod_p, False, seq_p, 0, final_norm, last)
        xs_ = _combine(ysort, dest_flat, x1s, probs, mod_s, True, seq_s, np_ // TOK_TILE,
                       final_norm, last)

        sp_l.append(s_p)
        hp_l.append(h_p)
        ss_l.append(s_s)
        hs_l.append(h_s)

    return (xp.reshape(bp, seq_p, D_MODEL), xs_.reshape(bs, seq_s, D_MODEL),
            jnp.stack(sp_l), jnp.stack(hp_l), jnp.stack(ss_l), jnp.stack(hs_l))
```

```python
import functools

import jax
import jax.numpy as jnp
from jax import lax
from jax.experimental import pallas as pl
from jax.experimental.pallas import tpu as pltpu

F32, BF16, I32 = jnp.float32, jnp.bfloat16, jnp.int32

D_MODEL = 1024
D_A = 512
H_A = 4
DK = 128
DV = 128
D_B = 512
POOL_WINDOWS = (2, 4, 8, 16)
POOL_GROUP = 128
POOL_HIST = 15
D_IN = 4 * D_A + D_B
N_EXPERTS = 32
TOP_K = 4
D_FF = 1024
SWIGLU_LIMIT = 7.0
SWIGLU_ALPHA = 1.702
EPS = 1e-6

SUB = 16
ROW_TILE = 256
TOK_TILE = 256
VMEM_LIMIT = 56 * 1024 * 1024


def _cparams(sem):
    return pltpu.CompilerParams(dimension_semantics=sem, vmem_limit_bytes=VMEM_LIMIT)


def _silu(x):
    return x * jax.nn.sigmoid(x)


def _mod_spec(per_token, tm, tiles_per_seq, col):
    if per_token:
        return pl.BlockSpec((tm, D_MODEL), lambda i: (i, col))
    return pl.BlockSpec((None, 1, D_MODEL), lambda i: (i // tiles_per_seq, 0, col))


def _ada_kernel(c_ref, w_ref, b_ref, o_ref):
    c = c_ref[...]
    o_ref[...] = jnp.dot(_silu(c).astype(BF16), w_ref[...].astype(BF16),
                         preferred_element_type=F32) + b_ref[...]


def _ada(c_all, w_ada, b_ada):
    rows = c_all.shape[0]
    n = w_ada.shape[1]
    return pl.pallas_call(
        _ada_kernel,
        grid=(n // D_MODEL,),
        in_specs=[pl.BlockSpec((rows, D_MODEL), lambda j: (0, 0)),
                  pl.BlockSpec((D_MODEL, D_MODEL), lambda j: (0, j)),
                  pl.BlockSpec((1, D_MODEL), lambda j: (0, j))],
        out_specs=pl.BlockSpec((rows, D_MODEL), lambda j: (0, j)),
        out_shape=jax.ShapeDtypeStruct((rows, n), F32),
        compiler_params=_cparams(("arbitrary",)),
        name="ada",
    )(c_all, w_ada, b_ada.reshape(1, n))


def _inproj_kernel(x_ref, sh_ref, sc_ref, n1_ref, w_ref, u_ref):
    x = x_ref[...]
    h = x * lax.rsqrt(jnp.mean(x * x, axis=-1, keepdims=True) + EPS) * n1_ref[...]
    h = h * (1.0 + sc_ref[...]) + sh_ref[...]
    u_ref[...] = jnp.dot(h.astype(BF16), w_ref[...], preferred_element_type=F32)


def _inproj(x, mod, per_token, seq, norm1, w_in_bf):
    n = x.shape[0]
    tm = min(512, n)
    tps = max(seq // tm, 1)
    return pl.pallas_call(
        _inproj_kernel,
        grid=(n // tm,),
        in_specs=[pl.BlockSpec((tm, D_MODEL), lambda i: (i, 0)),
                  _mod_spec(per_token, tm, tps, 0),
                  _mod_spec(per_token, tm, tps, 1),
                  pl.BlockSpec((1, D_MODEL), lambda i: (0, 0)),
                  pl.BlockSpec((D_MODEL, D_IN), lambda i: (0, 0))],
        out_specs=pl.BlockSpec((tm, D_IN), lambda i: (i, 0)),
        out_shape=jax.ShapeDtypeStruct((n, D_IN), F32),
        compiler_params=_cparams(("arbitrary",)),
        name="inproj",
    )(x, mod, mod, norm1.reshape(1, D_MODEL), w_in_bf)


def _block_select():
    r = lax.broadcasted_iota(I32, (SUB, SUB * DK), 0)
    c = lax.broadcasted_iota(I32, (SUB, SUB * DK), 1) >> (DK.bit_length() - 1)
    return (r == c).astype(BF16)


def _hgrn_block(q, fp, iv, g, lb, hn, st_ref, sel_ref, valid):
    f = lb + (1.0 - lb) * jax.nn.sigmoid(fp)
    logf = jnp.log(f)
    kk = 1.0 - f
    if valid < SUB:
        live = lax.broadcasted_iota(I32, (SUB, D_A), 0) < valid
        logf = jnp.where(live, logf, 0.0)
        kk = jnp.where(live, kk, 0.0)
    qa = _silu(q)

    r = lax.broadcasted_iota(I32, (SUB, SUB), 0)
    c = lax.broadcasted_iota(I32, (SUB, SUB), 1)
    tri = (r >= c).astype(BF16)
    hi = logf.astype(BF16)
    lo = (logf - hi.astype(F32)).astype(BF16)
    b = (jnp.dot(tri, hi, preferred_element_type=F32)
         + jnp.dot(tri, lo, preferred_element_type=F32))
    bend = b[SUB - 1:SUB, :]
    qt = (qa * jnp.exp(b)).astype(BF16)
    kh = (kk * jnp.exp(bend - b)).astype(BF16)
    dec = jnp.exp(bend)
    ivb = iv.astype(BF16)

    rows = []
    for h in range(H_A):
        sl = slice(DK * h, DK * (h + 1))
        bh, qh, kkh = b[:, sl], qa[:, sl], kk[:, sl]
        ps = []
        for s in range(SUB):
            e = jnp.exp(jnp.minimum(bh - bh[s:s + 1, :], 0.0))
            ps.append((e * (qh * kkh[s:s + 1, :])).astype(BF16))
        rows.append(jnp.concatenate(ps, axis=1))
    pcat = jnp.concatenate(rows, axis=0)
    a = lax.dot_general(pcat, sel_ref[...], (((1,), (1,)), ((), ())),
                        preferred_element_type=F32)
    tr = lax.broadcasted_iota(I32, (H_A * SUB, SUB), 0) & (SUB - 1)
    tc = lax.broadcasted_iota(I32, (H_A * SUB, SUB), 1)
    a = jnp.where(tr >= tc, a, 0.0).astype(BF16)

    outs = []
    for h in range(H_A):
        sl = slice(DK * h, DK * (h + 1))
        st = st_ref[h]
        o = lax.dot_general(qt[:, sl], st.astype(BF16), (((1,), (1,)), ((), ())),
                            preferred_element_type=F32)
        o = o + jnp.dot(a[SUB * h:SUB * (h + 1), :], ivb[:, sl], preferred_element_type=F32)
        ut = lax.dot_general(ivb[:, sl], kh[:, sl], (((0,), (0,)), ((), ())),
                             preferred_element_type=F32)
        st_ref[h] = st * dec[:, sl] + ut
        o = o * lax.rsqrt(jnp.mean(o * o, axis=-1, keepdims=True) + EPS) * hn[:, sl]
        outs.append(o)
    return jnp.concatenate(outs, axis=1) * _silu(g)


def _hgrn_prompt_kernel(q_ref, f_ref, i_ref, g_ref, lb_ref, hn_ref, o_ref, s_ref, st_ref, sel_ref,
                        *, tt):
    t = pl.program_id(1)

    @pl.when(t == 0)
    def _():
        st_ref[...] = jnp.zeros_like(st_ref)
        sel_ref[...] = _block_select()

    lb = lb_ref[...]
    hn = hn_ref[...]

    def body(j, carry):
        rs = pl.ds(pl.multiple_of(j * SUB, SUB), SUB)
        o_ref[rs, :] = _hgrn_block(q_ref[rs, :], f_ref[rs, :], i_ref[rs, :], g_ref[rs, :],
                                   lb, hn, st_ref, sel_ref, SUB)
        return carry

    lax.fori_loop(0, tt // SUB, body, 0)

    @pl.when(t == pl.num_programs(1) - 1)
    def _():
        for h in range(H_A):
            s_ref[h] = st_ref[h].T


def _hgrn_prompt(u, batch, seq, lb, hn):
    tt = 256
    nt = seq // tt

    def col(cb):
        return pl.BlockSpec((tt, D_A), lambda b, t: (b * nt + t, cb))

    return pl.pallas_call(
        functools.partial(_hgrn_prompt_kernel, tt=tt),
        grid=(batch, nt),
        in_specs=[col(0), col(1), col(2), col(3),
                  pl.BlockSpec((1, D_A), lambda b, t: (0, 0)),
                  pl.BlockSpec((1, D_A), lambda b, t: (0, 0))],
        out_specs=[pl.BlockSpec((tt, D_A), lambda b, t: (b * nt + t, 0)),
                   pl.BlockSpec((None, H_A, DK, DV), lambda b, t: (b, 0, 0, 0))],
        out_shape=[jax.ShapeDtypeStruct((batch * seq, D_A), F32),
                   jax.ShapeDtypeStruct((batch, H_A, DK, DV), F32)],
        scratch_shapes=[pltpu.VMEM((H_A, DV, DK), F32), pltpu.VMEM((SUB, SUB * DK), BF16)],
        compiler_params=_cparams(("arbitrary", "arbitrary")),
        name="hgrn_prompt",
    )(u, u, u, u, lb, hn)


def _hgrn_sample_kernel(q_ref, f_ref, i_ref, g_ref, lb_ref, hn_ref, s0_ref, o_ref, s_ref,
                        st_ref, sel_ref, pad_ref, *, seq):
    for h in range(H_A):
        st_ref[h] = s0_ref[h].T
    sel_ref[...] = _block_select()
    pad_ref[...] = jnp.zeros_like(pad_ref)
    for n, ref in enumerate((q_ref, f_ref, i_ref, g_ref)):
        pad_ref[n, 0:seq, :] = ref[...]
    o = _hgrn_block(pad_ref[0], pad_ref[1], pad_ref[2], pad_ref[3], lb_ref[...], hn_ref[...],
                    st_ref, sel_ref, seq)
    o_ref[...] = o[0:seq, :]
    for h in range(H_A):
        s_ref[h] = st_ref[h].T


def _hgrn_sample(u3, s0, lb, hn):
    batch, seq, _ = u3.shape

    def col(cb):
        return pl.BlockSpec((None, seq, D_A), lambda b: (b, 0, cb))

    return pl.pallas_call(
        functools.partial(_hgrn_sample_kernel, seq=seq),
        grid=(batch,),
        in_specs=[col(0), col(1), col(2), col(3),
                  pl.BlockSpec((1, D_A), lambda b: (0, 0)),
                  pl.BlockSpec((1, D_A), lambda b: (0, 0)),
                  pl.BlockSpec((None, H_A, DK, DV), lambda b: (b, 0, 0, 0))],
        out_specs=[pl.BlockSpec((None, seq, D_A), lambda b: (b, 0, 0)),
                   pl.BlockSpec((None, H_A, DK, DV), lambda b: (b, 0, 0, 0))],
        out_shape=[jax.ShapeDtypeStruct((batch, seq, D_A), F32),
                   jax.ShapeDtypeStruct((batch, H_A, DK, DV), F32)],
        scratch_shapes=[pltpu.VMEM((H_A, DV, DK), F32), pltpu.VMEM((SUB, SUB * DK), BF16),
                        pltpu.VMEM((4, SUB, D_A), F32)],
        compiler_params=_cparams(("arbitrary",)),
        name="hgrn_sample",
    )(u3, u3, u3, u3, lb, hn, s0)


HGRN_SEQS = 8
HALF = SUB // 2


def _hgrn_step(q, fp, iv, g, lb, hn, st_ref, sel_ref, cm_ref, b_ref, k_ref, valid):
    nrow = HGRN_SEQS * SUB
    f = lb + (1.0 - lb) * jax.nn.sigmoid(fp)
    lf = jnp.log2(f)
    kk = 1.0 - f
    if valid < SUB:
        live = (lax.broadcasted_iota(I32, (nrow, D_A), 0) & (SUB - 1)) < valid
        lf = jnp.where(live, lf, 0.0)
        kk = jnp.where(live, kk, 0.0)
    qa = _silu(q)

    hi = lf.astype(BF16)
    lo = (lf - hi.astype(F32)).astype(BF16)
    cum, tot = cm_ref[0], cm_ref[1]
    b = jnp.dot(cum, hi, preferred_element_type=F32) + jnp.dot(cum, lo, preferred_element_type=F32)
    be = jnp.dot(tot, hi, preferred_element_type=F32) + jnp.dot(tot, lo, preferred_element_type=F32)
    qt = (qa * jnp.exp2(b)).astype(BF16)
    kh = (kk * jnp.exp2(be - b)).astype(BF16)
    dec = jnp.exp2(be)
    ivb = iv.astype(BF16)
    for h in range(H_A):
        b_ref[h] = b[:, DK * h:DK * (h + 1)]
        k_ref[h] = kk[:, DK * h:DK * (h + 1)]

    zero_half = jnp.zeros((HALF, D_A), F32)
    tr = lax.broadcasted_iota(I32, (H_A * SUB, SUB), 0) & (SUB - 1)
    tc = lax.broadcasted_iota(I32, (H_A * SUB, SUB), 1)
    sg = _silu(g)
    inter = []
    for n in range(HGRN_SEQS):
        rows = slice(n * SUB, (n + 1) * SUB)
        for h in range(H_A):
            sl = slice(DK * h, DK * (h + 1))
            st = st_ref[n, h]
            inter.append(lax.dot_general(qt[rows, sl], st.astype(BF16), (((1,), (1,)), ((), ())),
                                         preferred_element_type=F32))
            ut = lax.dot_general(ivb[rows, sl], kh[rows, sl], (((0,), (0,)), ((), ())),
                                 preferred_element_type=F32)
            dtile = dec[n * SUB:n * SUB + HALF, sl]
            st_ref[n, h] = (st.reshape(DV // HALF, HALF, DK) * dtile[None]).reshape(DV, DK) + ut

    scores = []
    for n in range(HGRN_SEQS):
        per_s = []
        for s in range(SUB):
            row = n * SUB + s
            b_s = jnp.concatenate([b_ref[h, pl.ds(row, HALF, stride=0), :] for h in range(H_A)], axis=1)
            k_s = jnp.concatenate([k_ref[h, pl.ds(row, HALF, stride=0), :] for h in range(H_A)], axis=1)
            halves = []
            for hf in range(2):
                if hf == 0 and s >= HALF:
                    halves.append(zero_half)
                    continue
                r0 = n * SUB + hf * HALF
                e = jnp.exp2(jnp.minimum(b[r0:r0 + HALF] - b_s, 0.0))
                halves.append(e * (qa[r0:r0 + HALF] * k_s))
            per_s.append(jnp.concatenate(halves, axis=0).astype(BF16))
        pcat = jnp.concatenate(
            [jnp.concatenate([p[:, DK * h:DK * (h + 1)] for p in per_s], axis=1) for h in range(H_A)],
            axis=0)
        a = lax.dot_general(pcat, sel_ref[...], (((1,), (1,)), ((), ())),
                            preferred_element_type=F32)
        scores.append(jnp.where(tr >= tc, a, 0.0).astype(BF16))

    outs = []
    for n in range(HGRN_SEQS):
        rows = slice(n * SUB, (n + 1) * SUB)
        heads = []
        for h in range(H_A):
            sl = slice(DK * h, DK * (h + 1))
            o = inter[n * H_A + h] + jnp.dot(scores[n][h * SUB:(h + 1) * SUB, :], ivb[rows, sl],
                                             preferred_element_type=F32)
            o = o * lax.rsqrt(jnp.mean(o * o, axis=-1, keepdims=True) + EPS) * hn[:, sl]
            heads.append(o)
        outs.append(jnp.concatenate(heads, axis=1) * sg[rows])
    return jnp.concatenate(outs, axis=0)


def _hgrn_kernel(*refs, rows, has_s0):
    if has_s0:
        (q_ref, f_ref, i_ref, g_ref, lb_ref, hn_ref, s0_ref, o_ref, s_ref,
         st_ref, sel_ref, cm_ref, b_ref, k_ref, pad_ref) = refs
    else:
        (q_ref, f_ref, i_ref, g_ref, lb_ref, hn_ref, o_ref, s_ref,
         st_ref, sel_ref, cm_ref, b_ref, k_ref) = refs
    t = pl.program_id(0)
    nrow = HGRN_SEQS * SUB

    @pl.when(t == 0)
    def _():
        sel_ref[...] = _block_select()
        r = lax.broadcasted_iota(I32, (nrow, nrow), 0)
        c = lax.broadcasted_iota(I32, (nrow, nrow), 1)
        shift = SUB.bit_length() - 1
        same = (r >> shift) == (c >> shift)
        cm_ref[0] = (same & (r >= c)).astype(BF16)
        cm_ref[1] = same.astype(BF16)

    lb = lb_ref[...]
    hn = hn_ref[...]
    if has_s0:
        for n in range(HGRN_SEQS):
            for h in range(H_A):
                st_ref[n, h] = s0_ref[n, h].T
        pad_ref[...] = jnp.zeros_like(pad_ref)
        for j, ref in enumerate((q_ref, f_ref, i_ref, g_ref)):
            pad_ref[j, :, 0:rows, :] = ref[...]
        o = _hgrn_step(*(pad_ref[j].reshape(nrow, D_A) for j in range(4)), lb, hn,
                       st_ref, sel_ref, cm_ref, b_ref, k_ref, rows)
        o_ref[...] = o.reshape(HGRN_SEQS, SUB, D_A)[:, 0:rows, :]
    else:
        @pl.when(t == 0)
        def _():
            st_ref[...] = jnp.zeros_like(st_ref)

        def body(j, carry):
            rs = pl.ds(pl.multiple_of(j * SUB, SUB), SUB)
            o = _hgrn_step(*(ref[:, rs, :].reshape(nrow, D_A) for ref in (q_ref, f_ref, i_ref, g_ref)),
                           lb, hn, st_ref, sel_ref, cm_ref, b_ref, k_ref, SUB)
            o_ref[:, rs, :] = o.reshape(HGRN_SEQS, SUB, D_A)
            return carry

        lax.fori_loop(0, rows // SUB, body, 0)

    def emit_state():
        for n in range(HGRN_SEQS):
            for h in range(H_A):
                s_ref[n, h] = st_ref[n, h].T

    if has_s0:
        emit_state()
    else:
        pl.when(t == pl.num_programs(0) - 1)(emit_state)


def _hgrn(u3, s0, lb, hn):
    batch, seq, _ = u3.shape
    has_s0 = s0 is not None
    if has_s0:
        rows, grid = seq, (batch // HGRN_SEQS,)
        blk = lambda cb: pl.BlockSpec((HGRN_SEQS, rows, D_A), lambda i: (i, 0, cb))
        sblk = pl.BlockSpec((HGRN_SEQS, H_A, DK, DV), lambda i: (i, 0, 0, 0))
    else:
        rows, grid = 128, (seq // 128,)
        blk = lambda cb: pl.BlockSpec((HGRN_SEQS, rows, D_A), lambda i: (0, i, cb))
        sblk = pl.BlockSpec((HGRN_SEQS, H_A, DK, DV), lambda i: (0, 0, 0, 0))
    vec = pl.BlockSpec((1, D_A), lambda i: (0, 0))
    nrow = HGRN_SEQS * SUB
    scratch = [pltpu.VMEM((HGRN_SEQS, H_A, DV, DK), F32), pltpu.VMEM((SUB, SUB * DK), BF16),
               pltpu.VMEM((2, nrow, nrow), BF16), pltpu.VMEM((H_A, nrow, DK), F32),
               pltpu.VMEM((H_A, nrow, DK), F32)]
    if has_s0:
        scratch.append(pltpu.VMEM((4, HGRN_SEQS, SUB, D_A), F32))
    args = (u3, u3, u3, u3, lb, hn) + ((s0,) if has_s0 else ())
    return pl.pallas_call(
        functools.partial(_hgrn_kernel, rows=rows, has_s0=has_s0),
        grid=grid,
        in_specs=[blk(0), blk(1), blk(2), blk(3), vec, vec] + ([sblk] if has_s0 else []),
        out_specs=[blk(0), sblk],
        out_shape=[jax.ShapeDtypeStruct((batch, seq, D_A), F32),
                   jax.ShapeDtypeStruct((batch, H_A, DK, DV), F32)],
        scratch_shapes=scratch,
        compiler_params=_cparams(("arbitrary",)),
        name="hgrn_state" if has_s0 else "hgrn_scan",
    )(*args)


def _pool_groups(full_ref, base, tm, cnt_fn, wp_ref, ps_ref, step=1):
    outs = []
    for gi, w in enumerate(POOL_WINDOWS):
        ch = slice(POOL_GROUP * gi, POOL_GROUP * (gi + 1))
        v = full_ref[base:base + tm, ch]
        s = v
        for j in range(1, w):
            s = s + full_ref[base - j * step:base - j * step + tm, ch]
        d = s / cnt_fn(w) - v
        outs.append(jnp.dot(d.astype(BF16), wp_ref[gi], preferred_element_type=F32))
    return jnp.concatenate(outs, axis=1) * ps_ref[...]


def _pool_prompt_kernel(v_ref, prev_ref, wp_ref, ps_ref, o_ref, hist_ref, full_ref, *, tm):
    t = pl.program_id(1)
    prev = prev_ref[...]
    full_ref[0:16, :] = jnp.where(t == 0, 0.0, prev)
    full_ref[16:16 + tm, :] = v_ref[...]
    pos = (lax.broadcasted_iota(I32, (tm, 1), 0) + t * tm + 1).astype(F32)
    o_ref[...] = _pool_groups(full_ref, 16, tm, lambda w: jnp.minimum(pos, float(w)),
                              wp_ref, ps_ref)

    @pl.when(t == pl.num_programs(1) - 1)
    def _():
        hist_ref[...] = full_ref[16 + tm - POOL_HIST:16 + tm, :]


def _pool_prompt(u, batch, seq, w_pool_bf, pool_scale):
    tm = 256
    nt = seq // tm
    vcol = (4 * D_A) // D_B
    return pl.pallas_call(
        functools.partial(_pool_prompt_kernel, tm=tm),
        grid=(batch, nt),
        in_specs=[pl.BlockSpec((tm, D_B), lambda b, t: (b * nt + t, vcol)),
                  pl.BlockSpec((16, D_B),
                               lambda b, t: (jnp.maximum((b * nt + t) * (tm // 16) - 1, 0), vcol)),
                  pl.BlockSpec((len(POOL_WINDOWS), POOL_GROUP, POOL_GROUP), lambda b, t: (0, 0, 0)),
                  pl.BlockSpec((1, D_B), lambda b, t: (0, 0))],
        out_specs=[pl.BlockSpec((tm, D_B), lambda b, t: (b * nt + t, 0)),
                   pl.BlockSpec((None, POOL_HIST, D_B), lambda b, t: (b, 0, 0))],
        out_shape=[jax.ShapeDtypeStruct((batch * seq, D_B), F32),
                   jax.ShapeDtypeStruct((batch, POOL_HIST, D_B), F32)],
        scratch_shapes=[pltpu.VMEM((16 + tm, D_B), F32)],
        compiler_params=_cparams(("arbitrary", "arbitrary")),
        name="pool_prompt",
    )(u, u, w_pool_bf, pool_scale)


def _pool_sample_kernel(v_ref, hist_ref, wp_ref, ps_ref, o_ref, nh_ref, full_ref, *,
                        seq, batch, start):
    full_ref[0:batch, :] = jnp.zeros((batch, D_B), F32)
    full_ref[batch:16 * batch, :] = hist_ref[...].reshape(POOL_HIST * batch, D_B)
    full_ref[16 * batch:(16 + seq) * batch, :] = v_ref[...].reshape(seq * batch, D_B)

    def cnt(w):
        rows = [jnp.full((batch, 1), float(min(start + t + 1, w)), F32) for t in range(seq)]
        return jnp.concatenate(rows, axis=0)

    o = _pool_groups(full_ref, 16 * batch, seq * batch, cnt, wp_ref, ps_ref, step=batch)
    o_ref[...] = o.reshape(seq, batch, D_B)
    nh_ref[...] = full_ref[(16 + seq - POOL_HIST) * batch:(16 + seq) * batch, :].reshape(
        POOL_HIST, batch, D_B)


def _pool_sample(u_t, hist_t, start, w_pool_bf, pool_scale):
    seq, batch, _ = u_t.shape
    vcol = (4 * D_A) // D_B
    return pl.pallas_call(
        functools.partial(_pool_sample_kernel, seq=seq, batch=batch, start=start),
        grid=(1,),
        in_specs=[pl.BlockSpec((seq, batch, D_B), lambda i: (0, 0, vcol)),
                  pl.BlockSpec((POOL_HIST, batch, D_B), lambda i: (0, 0, 0)),
                  pl.BlockSpec((len(POOL_WINDOWS), POOL_GROUP, POOL_GROUP), lambda i: (0, 0, 0)),
                  pl.BlockSpec((1, D_B), lambda i: (0, 0))],
        out_specs=[pl.BlockSpec((seq, batch, D_B), lambda i: (0, 0, 0)),
                   pl.BlockSpec((POOL_HIST, batch, D_B), lambda i: (0, 0, 0))],
        out_shape=[jax.ShapeDtypeStruct((seq, batch, D_B), F32),
                   jax.ShapeDtypeStruct((POOL_HIST, batch, D_B), F32)],
        scratch_shapes=[pltpu.VMEM(((16 + seq) * batch, D_B), F32)],
        compiler_params=_cparams(("arbitrary",)),
        name="pool_sample",
    )(u_t, hist_t, w_pool_bf, pool_scale)


LANES = 128
SLAB = D_MODEL // LANES


def _store_slabs(ref, value):
    n = value.shape[0]
    for j in range(SLAB):
        ref[pl.ds(j, n, stride=SLAB), :] = value[:, LANES * j:LANES * (j + 1)]


def _load_slabs(ref, n):
    return [ref[pl.ds(j, n, stride=SLAB), :] for j in range(SLAB)]


def _slab_rows(row):
    return pl.ds(pl.multiple_of(row * SLAB, SLAB), SLAB)


def _split_dot(a, w_hi, w_lo):
    a_hi = a.astype(BF16)
    a_lo = (a - a_hi.astype(F32)).astype(BF16)
    return (jnp.dot(a_hi, w_hi, preferred_element_type=F32)
            + jnp.dot(a_hi, w_lo, preferred_element_type=F32)
            + jnp.dot(a_lo, w_hi, preferred_element_type=F32))


def _outproj_kernel(x_ref, oa_ref, ob_ref, g1_ref, sh2_ref, sc2_ref, n2_ref, woa_ref, wob_ref,
                    wrh_ref, wrl_ref, br_ref, *rest):
    x1_ref, h2_ref, idx_ref, prob_ref = rest[-4:]
    mix =(jnp.dot(oa_ref[...].astype(BF16), woa_ref[...], preferred_element_type=F32)
           + jnp.dot(ob_ref[...].astype(BF16), wob_ref[...], preferred_element_type=F32))
    x1 = x_ref[...] + g1_ref[...] * mix
    x1_ref[...] = x1
    h2 = x1 * lax.rsqrt(jnp.mean(x1 * x1, axis=-1, keepdims=True) + EPS) * n2_ref[...]
    h2 = h2 * (1.0 + sc2_ref[...]) + sh2_ref[...]
    _store_slabs(h2_ref, h2)

    logits = _split_dot(h2, wrh_ref[...], wrl_ref[...]) + br_ref[...]
    tm = logits.shape[0]
    lane = lax.broadcasted_iota(I32, (tm, N_EXPERTS), 1)
    kcol = lax.broadcasted_iota(I32, (tm, TOP_K), 1)
    vals = jnp.zeros((tm, TOP_K), F32)
    idxs = jnp.zeros((tm, TOP_K), I32)
    for k in range(TOP_K):
        m = jnp.max(logits, axis=-1, keepdims=True)
        am = jnp.min(jnp.where(logits == m, lane, N_EXPERTS), axis=-1, keepdims=True)
        vals = jnp.where(kcol == k, m, vals)
        idxs = jnp.where(kcol == k, am, idxs)
        logits = jnp.where(lane == am, -jnp.inf, logits)
    e = jnp.exp(vals - vals[:, 0:1])
    prob_ref[...] = e / jnp.sum(e, axis=-1, keepdims=True)
    idx_ref[...] = idxs


def _outproj(x, o_a, o_b, mod, per_token, seq, norm2, w_out_bf, wr_hi, wr_lo, b_router,
             n_total, tile0, joint=None):
    n = x.shape[0]
    tm = min(256, n)
    tps = max(seq // tm, 1)
    row = lambda width: pl.BlockSpec((tm, width), lambda i: (i, 0))
    jrow = lambda width: pl.BlockSpec((tm, width), lambda i: (i + tile0, 0))
    const = lambda shape: pl.BlockSpec(shape, lambda i: (0,) * len(shape))
    joint = () if joint is None else tuple(joint)
    n_in = 12
    return pl.pallas_call(
        _outproj_kernel,
        grid=(n // tm,),
        in_specs=[row(D_MODEL), row(D_A), row(D_B),
                  _mod_spec(per_token, tm, tps, 2),
                  _mod_spec(per_token, tm, tps, 3),
                  _mod_spec(per_token, tm, tps, 4),
                  const((1, D_MODEL)),
                  pl.BlockSpec((D_A, D_MODEL), lambda i: (0, 0)),
                  pl.BlockSpec((D_B, D_MODEL), lambda i: (1, 0)),
                  const((D_MODEL, N_EXPERTS)), const((D_MODEL, N_EXPERTS)), const((1, N_EXPERTS))]
                 + [pl.BlockSpec(memory_space=pl.ANY)] * len(joint),
        out_specs=[row(D_MODEL), pl.BlockSpec((tm * SLAB, LANES), lambda i: (i + tile0, 0)),
                   jrow(TOP_K), jrow(TOP_K)],
        out_shape=[jax.ShapeDtypeStruct((n, D_MODEL), F32),
                   jax.ShapeDtypeStruct((n_total * SLAB, LANES), F32),
                   jax.ShapeDtypeStruct((n_total, TOP_K), I32),
                   jax.ShapeDtypeStruct((n_total, TOP_K), F32)],
        input_output_aliases={n_in + j: 1 + j for j in range(len(joint))},
        compiler_params=_cparams(("arbitrary",)),
        name="outproj",
    )(x, o_a, o_b, mod, mod, mod, norm2.reshape(1, D_MODEL), w_out_bf, w_out_bf,
      wr_hi, wr_lo, b_router.reshape(1, N_EXPERTS), *joint)


def _rank_kernel(idx_ref, dest_ref, cnt_ref, carry_ref, *, te):
    phase = pl.program_id(0)
    i = pl.program_id(1)
    idx = idx_ref[...]
    lane = lax.broadcasted_iota(I32, (te, N_EXPERTS), 1)
    onehots = [(idx[:, k:k + 1] == lane) for k in range(TOP_K)]
    member = jnp.zeros((te, N_EXPERTS), F32)
    for oh in onehots:
        member = member + oh.astype(F32)
    colsum = jnp.sum(member, axis=0, keepdims=True)

    @pl.when((phase == 0) & (i == 0))
    def _():
        carry_ref[...] = jnp.zeros_like(carry_ref)

    @pl.when(phase == 0)
    def _():
        carry_ref[...] += colsum

    @pl.when((phase == 1) & (i == 0))
    def _():
        cnt = carry_ref[...]
        cnt_ref[...] = cnt.astype(I32)
        tiles = jnp.floor((cnt + (ROW_TILE - 1)) * (1.0 / ROW_TILE))
        r = lax.broadcasted_iota(I32, (N_EXPERTS, N_EXPERTS), 0)
        c = lax.broadcasted_iota(I32, (N_EXPERTS, N_EXPERTS), 1)
        before = (r < c).astype(BF16)
        carry_ref[...] = ROW_TILE * jnp.dot(tiles.astype(BF16), before, preferred_element_type=F32)

    @pl.when(phase == 1)
    def _():
        r = lax.broadcasted_iota(I32, (te, te), 0)
        c = lax.broadcasted_iota(I32, (te, te), 1)
        earlier = (r > c).astype(BF16)
        base = carry_ref[...] + jnp.dot(earlier, member.astype(BF16), preferred_element_type=F32)
        kcol = lax.broadcasted_iota(I32, (te, TOP_K), 1)
        dest = jnp.zeros((te, TOP_K), F32)
        for k, oh in enumerate(onehots):
            dk = jnp.sum(jnp.where(oh, base, 0.0), axis=-1, keepdims=True)
            dest = jnp.where(kcol == k, dk, dest)
        dest_ref[...] = dest.astype(I32)
        carry_ref[...] += colsum


def _rank(top_idx):
    n = top_idx.shape[0]
    te = 512
    return pl.pallas_call(
        functools.partial(_rank_kernel, te=te),
        grid=(2, n // te),
        in_specs=[pl.BlockSpec((te, TOP_K), lambda p, i: (i, 0))],
        out_specs=[pl.BlockSpec((te, TOP_K), lambda p, i: (p * i, 0)),
                   pl.BlockSpec((1, N_EXPERTS), lambda p, i: (0, 0))],
        out_shape=[jax.ShapeDtypeStruct((n, TOP_K), I32),
                   jax.ShapeDtypeStruct((1, N_EXPERTS), I32)],
        scratch_shapes=[pltpu.VMEM((1, N_EXPERTS), F32)],
        compiler_params=_cparams(("arbitrary", "arbitrary")),
        name="rank",
    )(top_idx)


def _dispatch_kernel(last_ref, ntile_ref, used_ref, dest_ref, h_ref, xs_ref, zero_ref, sem, zsem,
                     *, n_tiles):
    i = pl.program_id(0)

    @pl.when(i == 0)
    def _():
        zero_ref[...] = jnp.zeros_like(zero_ref)

        def clear(tile):
            rows = pl.ds(pl.multiple_of(tile * (ROW_TILE * SLAB), ROW_TILE * SLAB), ROW_TILE * SLAB)
            return pltpu.make_async_copy(zero_ref, xs_ref.at[rows], zsem)

        def over_tail(fn):
            def body(t, carry):
                fn(clear(t))
                return carry
            lax.fori_loop(used_ref[0], n_tiles, body, 0)

        for fn in (lambda c: c.start(), lambda c: c.wait()):
            for e in range(N_EXPERTS):
                @pl.when(ntile_ref[e] > 0)
                def _():
                    fn(clear(last_ref[e]))
            over_tail(fn)

    def body(r, carry):
        for k in range(TOP_K):
            d = dest_ref[r * TOP_K + k]
            pltpu.make_async_copy(h_ref.at[_slab_rows(r)], xs_ref.at[_slab_rows(d)], sem).start(
                priority=k % 2)
        return carry

    lax.fori_loop(0, TOK_TILE, body, 0, unroll=2)
    for k in range(TOP_K):
        pltpu.make_async_copy(h_ref, xs_ref.at[pl.ds(0, TOK_TILE * SLAB)], sem).wait()


def _dispatch(h2, dest_flat, last_tile, ntiles, used, n_tiles):
    n = h2.shape[0] // SLAB
    grid_spec = pltpu.PrefetchScalarGridSpec(
        num_scalar_prefetch=3,
        grid=(n // TOK_TILE,),
        in_specs=[pl.BlockSpec((TOK_TILE * TOP_K,), lambda i, *_: (i,), memory_space=pltpu.SMEM),
                  pl.BlockSpec((TOK_TILE * SLAB, LANES), lambda i, *_: (i, 0))],
        out_specs=pl.BlockSpec(memory_space=pl.ANY),
        scratch_shapes=[pltpu.VMEM((ROW_TILE * SLAB, LANES), F32),
                        pltpu.SemaphoreType.DMA(()), pltpu.SemaphoreType.DMA(())],
    )
    return pl.pallas_call(
        functools.partial(_dispatch_kernel, n_tiles=n_tiles),
        grid_spec=grid_spec,
        out_shape=jax.ShapeDtypeStruct((n_tiles * ROW_TILE * SLAB, LANES), F32),
        compiler_params=_cparams(("arbitrary",)),
        name="dispatch",
    )(last_tile, ntiles, used, dest_flat, h2)


def _experts_kernel(te_ref, used_ref, xs_ref, wgu_ref, bgu_ref, wd_ref, bd_ref, ys_ref,
                    wgu_bf, wd_bf, act_bf):
    i = pl.program_id(0)
    prev = te_ref[jnp.maximum(i - 1, 0)]
    fresh = (i == 0) | (te_ref[i] != prev)

    @pl.when(fresh)
    def _():
        wgu_bf[...] = wgu_ref[...].astype(BF16)
        wd_bf[...] = wd_ref[...].astype(BF16)

    @pl.when(i < used_ref[0])
    def _():
        x = jnp.concatenate([p.astype(BF16) for p in _load_slabs(xs_ref, ROW_TILE)], axis=1)
        cw = D_FF // FF_CHUNKS
        for c in range(FF_CHUNKS):
            gate = jnp.dot(x, wgu_bf[:, c * cw:(c + 1) * cw], preferred_element_type=F32)
            gate = jnp.minimum(gate + bgu_ref[:, c * cw:(c + 1) * cw], SWIGLU_LIMIT)
            up = jnp.dot(x, wgu_bf[:, D_FF + c * cw:D_FF + (c + 1) * cw], preferred_element_type=F32)
            up = jnp.clip(up + bgu_ref[:, D_FF + c * cw:D_FF + (c + 1) * cw], -SWIGLU_LIMIT, SWIGLU_LIMIT)
            act = gate * jax.nn.sigmoid(SWIGLU_ALPHA * gate) * (up + 1.0)
            act_bf[:, c * cw:(c + 1) * cw] = act.astype(BF16)
        ow = D_MODEL // FF_CHUNKS
        for c in range(FF_CHUNKS):
            out = jnp.dot(act_bf[...], wd_bf[:, c * ow:(c + 1) * ow], preferred_element_type=F32)
            out = out + bd_ref[:, c * ow:(c + 1) * ow]
            for j in range(ow // LANES):
                ys_ref[pl.ds(c * (ow // LANES) + j, ROW_TILE, stride=SLAB), :] = (
                    out[:, LANES * j:LANES * (j + 1)])

    @pl.when(i >= used_ref[0])
    def _():
        ys_ref[...] = jnp.zeros_like(ys_ref)


def _experts(xs, tile_expert, used, w_gu, b_gu, w_down, b_down):
    rows = xs.shape[0] // SLAB
    nt = rows // ROW_TILE

    def tile(i, te, used):
        return (jnp.minimum(i, used[0] - 1), 0)

    grid_spec = pltpu.PrefetchScalarGridSpec(
        num_scalar_prefetch=2,
        grid=(nt,),
        in_specs=[pl.BlockSpec((ROW_TILE * SLAB, LANES), tile),
                  pl.BlockSpec((None, D_MODEL, 2 * D_FF), lambda i, te, used: (te[i], 0, 0)),
                  pl.BlockSpec((None, 1, 2 * D_FF), lambda i, te, used: (te[i], 0, 0)),
                  pl.BlockSpec((None, D_FF, D_MODEL), lambda i, te, used: (te[i], 0, 0)),
                  pl.BlockSpec((None, 1, D_MODEL), lambda i, te, used: (te[i], 0, 0))],
        out_specs=pl.BlockSpec((ROW_TILE * SLAB, LANES), lambda i, te, used: (i, 0)),
        scratch_shapes=[pltpu.VMEM((D_MODEL, 2 * D_FF), BF16), pltpu.VMEM((D_FF, D_MODEL), BF16),
                        pltpu.VMEM((ROW_TILE, D_FF), BF16)],
    )
    return pl.pallas_call(
        _experts_kernel,
        grid_spec=grid_spec,
        out_shape=jax.ShapeDtypeStruct((rows * SLAB, LANES), F32),
        compiler_params=_cparams(("arbitrary",)),
        name="experts",
    )(tile_expert, used, xs, w_gu, b_gu.reshape(N_EXPERTS, 1, 2 * D_FF), w_down,
      b_down.reshape(N_EXPERTS, 1, D_MODEL))


X_RING = 3


def _mlp_kernel(te_ref, used_ref, grp_ref, nxt_ref, xs_ref, wgu_ref, bgu_ref, wd_ref, bd_ref, ys_ref,
                wgu_f, wd_f, wgu_bf, wd_bf, act_bf, xring, wsem, xsem):
    i = pl.program_id(0)
    used = used_ref[0]
    tile_rows = ROW_TILE * SLAB

    def weights(e, ws):
        return (pltpu.make_async_copy(wgu_ref.at[e], wgu_f.at[ws], wsem.at[ws, 0]),
                pltpu.make_async_copy(wd_ref.at[e], wd_f.at[ws], wsem.at[ws, 1]))

    def x_tile(t):
        t = jnp.asarray(t, I32)
        rows = pl.ds(pl.multiple_of(t * tile_rows, tile_rows), tile_rows)
        return pltpu.make_async_copy(xs_ref.at[rows], xring.at[t % X_RING], xsem.at[t % X_RING])

    @pl.when(i == 0)
    def _():
        for cp in weights(te_ref[0], 0):
            cp.start()
        for t in range(X_RING - 1):
            @pl.when(t < used)
            def _():
                x_tile(t).start()

    @pl.when(i + X_RING - 1 < used)
    def _():
        x_tile(i + X_RING - 1).start()

    ws = grp_ref[i] % 2
    fresh = (i == 0) | (te_ref[i] != te_ref[jnp.maximum(i - 1, 0)])

    @pl.when(fresh & (i < used))
    def _():
        for cp in weights(te_ref[i], ws):
            cp.wait()

        @pl.when(nxt_ref[i] >= 0)
        def _():
            for cp in weights(nxt_ref[i], 1 - ws):
                cp.start()

        wgu_bf[...] = wgu_f[ws].astype(BF16)
        wd_bf[...] = wd_f[ws].astype(BF16)

    @pl.when(i < used)
    def _():
        x_tile(i).wait()
        x = jnp.concatenate([p.astype(BF16) for p in _load_slabs(xring.at[i % X_RING], ROW_TILE)],
                            axis=1)
        cw = D_FF // FF_CHUNKS
        for c in range(FF_CHUNKS):
            gate = jnp.dot(x, wgu_bf[:, c * cw:(c + 1) * cw], preferred_element_type=F32)
            gate = jnp.minimum(gate + bgu_ref[:, c * cw:(c + 1) * cw], SWIGLU_LIMIT)
            up = jnp.dot(x, wgu_bf[:, D_FF + c * cw:D_FF + (c + 1) * cw], preferred_element_type=F32)
            up = jnp.clip(up + bgu_ref[:, D_FF + c * cw:D_FF + (c + 1) * cw], -SWIGLU_LIMIT, SWIGLU_LIMIT)
            act = gate * jax.nn.sigmoid(SWIGLU_ALPHA * gate) * (up + 1.0)
            act_bf[:, c * cw:(c + 1) * cw] = act.astype(BF16)
        ow = D_MODEL // FF_CHUNKS
        for c in range(FF_CHUNKS):
            out = jnp.dot(act_bf[...], wd_bf[:, c * ow:(c + 1) * ow], preferred_element_type=F32)
            out = out + bd_ref[:, c * ow:(c + 1) * ow]
            for j in range(ow // LANES):
                ys_ref[pl.ds(c * (ow // LANES) + j, ROW_TILE, stride=SLAB), :] = (
                    out[:, LANES * j:LANES * (j + 1)])

    @pl.when(i >= used)
    def _():
        ys_ref[...] = jnp.zeros_like(ys_ref)


def _mlp(xs, tile_expert, used, grp, nxt, w_gu, b_gu, w_down, b_down):
    rows = xs.shape[0] // SLAB
    nt = rows // ROW_TILE
    ex = lambda i, te, *_: (te[i], 0, 0)
    grid_spec = pltpu.PrefetchScalarGridSpec(
        num_scalar_prefetch=4,
        grid=(nt,),
        in_specs=[pl.BlockSpec(memory_space=pl.ANY),
                  pl.BlockSpec(memory_space=pl.ANY),
                  pl.BlockSpec((None, 1, 2 * D_FF), ex),
                  pl.BlockSpec(memory_space=pl.ANY),
                  pl.BlockSpec((None, 1, D_MODEL), ex)],
        out_specs=pl.BlockSpec((ROW_TILE * SLAB, LANES), lambda i, *_: (i, 0)),
        scratch_shapes=[pltpu.VMEM((2, D_MODEL, 2 * D_FF), F32), pltpu.VMEM((2, D_FF, D_MODEL), F32),
                        pltpu.VMEM((D_MODEL, 2 * D_FF), BF16), pltpu.VMEM((D_FF, D_MODEL), BF16),
                        pltpu.VMEM((ROW_TILE, D_FF), BF16),
                        pltpu.VMEM((X_RING, ROW_TILE * SLAB, LANES), F32),
                        pltpu.SemaphoreType.DMA((2, 2)), pltpu.SemaphoreType.DMA((X_RING,))],
    )
    return pl.pallas_call(
        _mlp_kernel,
        grid_spec=grid_spec,
        out_shape=jax.ShapeDtypeStruct((rows * SLAB, LANES), F32),
        compiler_params=_cparams(("arbitrary",)),
        name="experts",
    )(tile_expert, used, grp, nxt, xs, w_gu, b_gu.reshape(N_EXPERTS, 1, 2 * D_FF), w_down,
      b_down.reshape(N_EXPERTS, 1, D_MODEL))


def _combine_kernel(dest_ref, next_ref, ys_ref, x1_ref, p_ref, g2_ref, fn_ref, y_ref, buf_ref, sem,
                    *, last):
    i = pl.program_id(0)
    slot = i % 2

    def issue(d_ref, sl):
        def body(r, carry):
            for k in range(TOP_K):
                d = d_ref[r * TOP_K + k]
                pltpu.make_async_copy(ys_ref.at[_slab_rows(d)], buf_ref.at[sl, k, _slab_rows(r)],
                                      sem.at[sl]).start(priority=k % 2)
            return carry
        lax.fori_loop(0, TOK_TILE, body, 0, unroll=2)

    @pl.when(i == 0)
    def _():
        issue(dest_ref, 0)

    @pl.when(i + 1 < pl.num_programs(0))
    def _():
        issue(next_ref, 1 - slot)

    for k in range(TOP_K):
        pltpu.make_async_copy(ys_ref.at[pl.ds(0, TOK_TILE * SLAB)], buf_ref.at[slot, k],
                              sem.at[slot]).wait()

    p = p_ref[...]
    groups = None
    for k in range(TOP_K):
        part = [p[:, k:k + 1] * g for g in _load_slabs(buf_ref.at[slot, k], TOK_TILE)]
        groups = part if groups is None else [a + b for a, b in zip(groups, part)]
    moe = jnp.concatenate(groups, axis=1)
    x2 = x1_ref[...] + g2_ref[...] * moe
    if last:
        x2 = x2 * lax.rsqrt(jnp.mean(x2 * x2, axis=-1, keepdims=True) + EPS) * fn_ref[...]
    y_ref[...] = x2


def _combine(ys, dest_flat, x1, probs, mod, per_token, seq, tile0, final_norm, last):
    n = x1.shape[0]
    tm = TOK_TILE
    tps = max(seq // tm, 1)
    nt = n // tm
    return pl.pallas_call(
        functools.partial(_combine_kernel, last=last),
        grid=(nt,),
        in_specs=[pl.BlockSpec((tm * TOP_K,), lambda i: (i + tile0,), memory_space=pltpu.SMEM),
                  pl.BlockSpec((tm * TOP_K,), lambda i: (jnp.minimum(i + 1, nt - 1) + tile0,),
                               memory_space=pltpu.SMEM),
                  pl.BlockSpec(memory_space=pl.ANY),
                  pl.BlockSpec((tm, D_MODEL), lambda i: (i, 0)),
                  pl.BlockSpec((tm, TOP_K), lambda i: (i + tile0, 0)),
                  _mod_spec(per_token, tm, tps, 5),
                  pl.BlockSpec((1, D_MODEL), lambda i: (0, 0))],
        out_specs=pl.BlockSpec((tm, D_MODEL), lambda i: (i, 0)),
        out_shape=jax.ShapeDtypeStruct((n, D_MODEL), F32),
        scratch_shapes=[pltpu.VMEM((2, TOP_K, tm * SLAB, LANES), F32),
                        pltpu.SemaphoreType.DMA((2,))],
        compiler_params=_cparams(("arbitrary",)),
        name="combine",
    )(dest_flat, dest_flat, ys, x1, probs, mod, final_norm.reshape(1, D_MODEL))


def _invert_kernel(dest_ref, src0_ref, dst0_ref, src_ref, dst_ref, sem, *, n_tok):
    i = pl.program_id(0)
    blk = dest_ref.shape[0]

    @pl.when(i == 0)
    def _():
        copies = [pltpu.make_async_copy(src0_ref, src_ref, sem.at[0]),
                  pltpu.make_async_copy(dst0_ref, dst_ref, sem.at[1])]
        for cp in copies:
            cp.start()
        for cp in copies:
            cp.wait()

    def body(a, carry):
        g = i * blk + a
        d = dest_ref[a]
        tok = g >> (TOP_K.bit_length() - 1)
        src_ref[d] = tok
        dst_ref[d] = (g & (TOP_K - 1)) * n_tok + tok
        return carry

    lax.fori_loop(0, blk, body, 0, unroll=8)


def _invert(dest_flat, n_tok, n_rows):
    blk = TOK_TILE * TOP_K
    whole = pl.BlockSpec(memory_space=pltpu.SMEM)
    src0 = jnp.zeros((n_rows,), I32)
    dst0 = n_tok * TOP_K + jnp.arange(n_rows, dtype=I32)
    return pl.pallas_call(
        functools.partial(_invert_kernel, n_tok=n_tok),
        grid=(dest_flat.shape[0] // blk,),
        in_specs=[pl.BlockSpec((blk,), lambda i: (i,), memory_space=pltpu.SMEM),
                  pl.BlockSpec(memory_space=pl.ANY), pl.BlockSpec(memory_space=pl.ANY)],
        out_specs=[whole, whole],
        out_shape=[jax.ShapeDtypeStruct((n_rows,), I32), jax.ShapeDtypeStruct((n_rows,), I32)],
        scratch_shapes=[pltpu.SemaphoreType.DMA((2,))],
        compiler_params=_cparams(("arbitrary",)),
        name="invert",
    )(dest_flat, src0, dst0)


FF_CHUNKS = 4


def _moe_kernel(src_ref, dst_ref, te_ref, used_ref, h_ref, wgu_ref, bgu_ref, wd_ref, bd_ref, y_ref,
                wgu_bf, wd_bf, act_bf, buf, gsem, ssem):
    i = pl.program_id(0)
    used = used_ref[0]
    slot = i % 2
    tile_rows = ROW_TILE * SLAB
    n_tiles = te_ref.shape[0]

    def gather(tile, sl, rows):
        for r in rows:
            tok = src_ref[tile * ROW_TILE + r]
            pltpu.make_async_copy(h_ref.at[_slab_rows(tok)], buf.at[sl, _slab_rows(r)],
                                  gsem.at[sl]).start(priority=1)

    def scatter(tile, sl, rows):
        for r in rows:
            out = dst_ref[tile * ROW_TILE + r]
            pltpu.make_async_copy(buf.at[2 + sl, _slab_rows(r)], y_ref.at[_slab_rows(out)],
                                  ssem.at[sl]).start()

    def wait_gather(sl):
        pltpu.make_async_copy(h_ref.at[pl.ds(0, tile_rows)], buf.at[sl], gsem.at[sl]).wait()

    def wait_scatter(sl):
        pltpu.make_async_copy(buf.at[2 + sl], y_ref.at[pl.ds(0, tile_rows)], ssem.at[sl]).wait()

    @pl.when(i == 0)
    def _():
        buf[2] = jnp.zeros(buf.shape[1:], F32)
        buf[3] = jnp.zeros(buf.shape[1:], F32)
        tail = y_ref.shape[0] - tile_rows
        pltpu.make_async_copy(buf.at[2], y_ref.at[pl.ds(tail, tile_rows)], ssem.at[0]).start()
        gather(0, 0, range(ROW_TILE))

    prev = te_ref[jnp.maximum(i - 1, 0)]
    fresh = (i == 0) | (te_ref[jnp.minimum(i, n_tiles - 1)] != prev)

    @pl.when(fresh & (i < used))
    def _():
        wgu_bf[...] = wgu_ref[...].astype(BF16)
        wd_bf[...] = wd_ref[...].astype(BF16)

    per = ROW_TILE // FF_CHUNKS
    cw = D_FF // FF_CHUNKS

    @pl.when(i < used)
    def _():
        wait_gather(slot)
        nxt = jnp.minimum(i + 1, n_tiles - 1)
        prv = jnp.where(i >= 1, i - 1, n_tiles)
        half = per // 2
        for c in range(FF_CHUNKS):
            gather(nxt, 1 - slot, range(c * per, (c + 1) * per))
            scatter(prv, 1 - slot, range(c * half, (c + 1) * half))
            x = jnp.concatenate([p.astype(BF16) for p in _load_slabs(buf.at[slot], ROW_TILE)], axis=1)
            gate = jnp.dot(x, wgu_bf[:, c * cw:(c + 1) * cw], preferred_element_type=F32)
            gate = jnp.minimum(gate + bgu_ref[:, c * cw:(c + 1) * cw], SWIGLU_LIMIT)
            up = jnp.dot(x, wgu_bf[:, D_FF + c * cw:D_FF + (c + 1) * cw], preferred_element_type=F32)
            up = jnp.clip(up + bgu_ref[:, D_FF + c * cw:D_FF + (c + 1) * cw], -SWIGLU_LIMIT, SWIGLU_LIMIT)
            act = gate * jax.nn.sigmoid(SWIGLU_ALPHA * gate) * (up + 1.0)
            act_bf[:, c * cw:(c + 1) * cw] = act.astype(BF16)
        ow = D_MODEL // FF_CHUNKS
        base = FF_CHUNKS * half
        wait_scatter(slot)
        for c in range(FF_CHUNKS):
            scatter(prv, 1 - slot, range(base + c * half, base + (c + 1) * half))
            out = jnp.dot(act_bf[...], wd_bf[:, c * ow:(c + 1) * ow], preferred_element_type=F32)
            out = out + bd_ref[:, c * ow:(c + 1) * ow]
            for j in range(ow // LANES):
                buf[2 + slot, pl.ds(c * (ow // LANES) + j, ROW_TILE, stride=SLAB), :] = (
                    out[:, LANES * j:LANES * (j + 1)])

    @pl.when(i == used)
    def _():
        wait_gather(slot)
        wait_scatter(slot)
        scatter(i - 1, 1 - slot, range(ROW_TILE))
        wait_scatter(1 - slot)


def _moe(h2, src, dst, tile_expert, used, w_gu, b_gu, w_down, b_down, n_tok):
    n_tiles = tile_expert.shape[0]
    n_rows = n_tiles * ROW_TILE
    ex = lambda i, src, dst, te, used: (te[jnp.minimum(i, n_tiles - 1)], 0, 0)
    grid_spec = pltpu.PrefetchScalarGridSpec(
        num_scalar_prefetch=4,
        grid=(n_tiles + 1,),
        in_specs=[pl.BlockSpec(memory_space=pl.ANY),
                  pl.BlockSpec((None, D_MODEL, 2 * D_FF), ex),
                  pl.BlockSpec((None, 1, 2 * D_FF), ex),
                  pl.BlockSpec((None, D_FF, D_MODEL), ex),
                  pl.BlockSpec((None, 1, D_MODEL), ex)],
        out_specs=pl.BlockSpec(memory_space=pl.ANY),
        scratch_shapes=[pltpu.VMEM((D_MODEL, 2 * D_FF), BF16), pltpu.VMEM((D_FF, D_MODEL), BF16),
                        pltpu.VMEM((ROW_TILE, D_FF), BF16),
                        pltpu.VMEM((4, ROW_TILE * SLAB, LANES), F32),
                        pltpu.SemaphoreType.DMA((2,)), pltpu.SemaphoreType.DMA((2,))],
    )
    return pl.pallas_call(
        _moe_kernel,
        grid_spec=grid_spec,
        out_shape=jax.ShapeDtypeStruct(((n_tok * TOP_K + n_rows + 2 * ROW_TILE) * SLAB, LANES), F32),
        compiler_params=_cparams(("arbitrary",)),
        name="moe",
    )(src, dst, tile_expert, used, h2, w_gu, b_gu.reshape(N_EXPERTS, 1, 2 * D_FF), w_down,
      b_down.reshape(N_EXPERTS, 1, D_MODEL))


def _merge_kernel(y0_ref, y1_ref, y2_ref, y3_ref, x1_ref, p_ref, g2_ref, fn_ref, o_ref, *, last):
    p = p_ref[...]
    tm = p.shape[0]
    groups = None
    for k, ref in enumerate((y0_ref, y1_ref, y2_ref, y3_ref)):
        part = [p[:, k:k + 1] * g for g in _load_slabs(ref, tm)]
        groups = part if groups is None else [a + b for a, b in zip(groups, part)]
    x2 = x1_ref[...] + g2_ref[...] * jnp.concatenate(groups, axis=1)
    if last:
        x2 = x2 * lax.rsqrt(jnp.mean(x2 * x2, axis=-1, keepdims=True) + EPS) * fn_ref[...]
    o_ref[...] = x2


def _merge(y, x1, probs, mod, per_token, seq, tile0, n_tok, final_norm, last):
    n = x1.shape[0]
    tm = TOK_TILE
    tps = max(seq // tm, 1)
    choice = lambda k: pl.BlockSpec((tm * SLAB, LANES), lambda i: (k * (n_tok // tm) + tile0 + i, 0))
    return pl.pallas_call(
        functools.partial(_merge_kernel, last=last),
        grid=(n // tm,),
        in_specs=[choice(0), choice(1), choice(2), choice(3),
                  pl.BlockSpec((tm, D_MODEL), lambda i: (i, 0)),
                  pl.BlockSpec((tm, TOP_K), lambda i: (i + tile0, 0)),
                  _mod_spec(per_token, tm, tps, 5),
                  pl.BlockSpec((1, D_MODEL), lambda i: (0, 0))],
        out_specs=pl.BlockSpec((tm, D_MODEL), lambda i: (i, 0)),
        out_shape=jax.ShapeDtypeStruct((n, D_MODEL), F32),
        compiler_params=_cparams(("arbitrary",)),
        name="merge",
    )(y, y, y, y, x1, probs, mod, final_norm.reshape(1, D_MODEL))


def kernel(x_prompt, x_sample, state_hgrn, state_pool, c_prompt, c_sample, w_ada, b_ada, norm1,
           norm2, w_in, lower_bounds, hgrn_norm, w_pool, pool_scale, w_out, w_router, b_router,
           w_gu, b_gu, w_down, b_down, final_norm):
    bp, seq_p, _ = x_prompt.shape
    bs, seq_s, _ = x_sample.shape
    np_, ns = bp * seq_p, bs * seq_s
    depth = w_ada.shape[0]
    past_len = 16384

    lbs = jnp.cumsum(jax.nn.softmax(lower_bounds.astype(F32), axis=0), axis=0)
    xp = x_prompt.reshape(np_, D_MODEL)
    xs_ = x_sample.reshape(ns, D_MODEL)
    c_all = jnp.concatenate([c_prompt, c_sample], axis=0)

    sp_l, hp_l, ss_l, hs_l = [], [], [], []
    for l in range(depth):
        w_in_bf = w_in[l].astype(BF16)
        w_out_bf = w_out[l].astype(BF16)
        w_pool_bf = w_pool[l].astype(BF16)
        wr_hi = w_router[l].astype(BF16)
        wr_lo = (w_router[l] - wr_hi.astype(F32)).astype(BF16)
        lb = lbs[l].reshape(1, D_A)
        hn = hgrn_norm[l].reshape(1, D_A)
        psc = pool_scale[l].reshape(1, D_B)

        mod = _ada(c_all, w_ada[l], b_ada[l])
        mod_p = mod[:bp].reshape(bp, 1, 6 * D_MODEL)
        mod_s = jnp.repeat(mod[bp:], seq_s, axis=0)

        up = _inproj(xp, mod_p, False, seq_p, norm1[l], w_in_bf)
        us = _inproj(xs_, mod_s, True, seq_s, norm1[l], w_in_bf)
        us3 = us.reshape(bs, seq_s, D_IN)

        oa_p, s_p = _hgrn(up.reshape(bp, seq_p, D_IN), None, lb, hn)
        oa_p = oa_p.reshape(np_, D_A)
        oa_s, s_s = _hgrn(us3, state_hgrn[l], lb, hn)
        ob_p, h_p = _pool_prompt(up, bp, seq_p, w_pool_bf, psc)
        ob_t, h_t = _pool_sample(us3.transpose(1, 0, 2), state_pool[l].transpose(1, 0, 2), past_len,
                                 w_pool_bf, psc)
        ob_s, h_s = ob_t.transpose(1, 0, 2), h_t.transpose(1, 0, 2)

        n_tok = np_ + ns
        n_tiles = (n_tok * TOP_K) // ROW_TILE + N_EXPERTS
        x1p, *joint = _outproj(xp, oa_p, ob_p, mod_p, False, seq_p, norm2[l], w_out_bf,
                               wr_hi, wr_lo, b_router[l], n_tok, 0)
        x1s, h2, top_idx, probs = _outproj(
            xs_, oa_s.reshape(ns, D_A), ob_s.reshape(ns, D_B), mod_s, True, seq_s, norm2[l],
            w_out_bf, wr_hi, wr_lo, b_router[l], n_tok, np_ // TOK_TILE, joint)

        dest, counts = _rank(top_idx)
        dest_flat = dest.reshape(n_tok * TOP_K)
        tiles_e = (counts.reshape(N_EXPERTS) + ROW_TILE - 1) // ROW_TILE
        ends = jnp.cumsum(tiles_e)
        used = ends[-1:].astype(I32)
        tile_expert = jnp.minimum(
            jnp.sum(jnp.arange(n_tiles, dtype=I32)[:, None] >= ends[None, :], axis=1),
            N_EXPERTS - 1).astype(I32)

        last_tile = (ends - 1).astype(I32)

        xsort = _dispatch(h2, dest_flat, last_tile, tiles_e.astype(I32), used, n_tiles)
        grp = jnp.cumsum(jnp.concatenate(
            [jnp.zeros((1,), I32), (tile_expert[1:] != tile_expert[:-1]).astype(I32)]))
        eids = jnp.arange(N_EXPERTS, dtype=I32)
        later = (eids[None, :] > eids[:, None]) & (tiles_e > 0)[None, :]
        nxt_e = jnp.min(jnp.where(later, eids[None, :], N_EXPERTS), axis=1)
        nxt = jnp.where(nxt_e < N_EXPERTS, nxt_e, -1)[tile_expert].astype(I32)
        ysort = _mlp(xsort, tile_expert, used, grp.astype(I32), nxt, w_gu[l], b_gu[l],
                     w_down[l], b_down[l])

        last = l == depth - 1
        xp = _combine(ysort, dest_flat, x1p, probs, mod_p, False, seq_p, 0, final_norm, last)
        xs_ = _combine(ysort, dest_flat, x1s, probs, mod_s, True, seq_s, np_ // TOK_TILE,
                       final_norm, last)

        sp_l.append(s_p)
        hp_l.append(h_p)
        ss_l.append(s_s)
        hs_l.append(h_s)

    return (xp.reshape(bp, seq_p, D_MODEL), xs_.reshape(bs, seq_s, D_MODEL),
            jnp.stack(sp_l), jnp.stack(hp_l), jnp.stack(ss_l), jnp.stack(hs_l))
```

```python
import functools

import jax
import jax.numpy as jnp
from jax import lax
from jax.experimental import pallas as pl
from jax.experimental.pallas import tpu as pltpu

F32, BF16, I32 = jnp.float32, jnp.bfloat16, jnp.int32

D_MODEL = 1024
D_A = 512
H_A = 4
DK = 128
DV = 128
D_B = 512
POOL_WINDOWS = (2, 4, 8, 16)
POOL_GROUP = 128
POOL_HIST = 15
D_IN = 4 * D_A + D_B
N_EXPERTS = 32
TOP_K = 4
D_FF = 1024
SWIGLU_LIMIT = 7.0
SWIGLU_ALPHA = 1.702
EPS = 1e-6

SUB = 16
ROW_TILE = 256
TOK_TILE = 256
VMEM_LIMIT = 56 * 1024 * 1024


def _cparams(sem):
    return pltpu.CompilerParams(dimension_semantics=sem, vmem_limit_bytes=VMEM_LIMIT)


def _silu(x):
    return x * jax.nn.sigmoid(x)


def _mod_spec(per_token, tm, tiles_per_seq, col, last_tile=None):
    tile = (lambda i: i) if last_tile is None else (lambda i: jnp.minimum(i, last_tile))
    if per_token:
        return pl.BlockSpec((tm, D_MODEL), lambda i: (tile(i), col))
    return pl.BlockSpec((None, 1, D_MODEL), lambda i: (tile(i) // tiles_per_seq, 0, col))


def _ada_kernel(c_ref, w_ref, b_ref, o_ref):
    c = c_ref[...]
    o_ref[...] = jnp.dot(_silu(c).astype(BF16), w_ref[...].astype(BF16),
                         preferred_element_type=F32) + b_ref[...]


def _ada(c_all, w_ada, b_ada):
    rows = c_all.shape[0]
    n = w_ada.shape[1]
    return pl.pallas_call(
        _ada_kernel,
        grid=(n // D_MODEL,),
        in_specs=[pl.BlockSpec((rows, D_MODEL), lambda j: (0, 0)),
                  pl.BlockSpec((D_MODEL, D_MODEL), lambda j: (0, j)),
                  pl.BlockSpec((1, D_MODEL), lambda j: (0, j))],
        out_specs=pl.BlockSpec((rows, D_MODEL), lambda j: (0, j)),
        out_shape=jax.ShapeDtypeStruct((rows, n), F32),
        compiler_params=_cparams(("arbitrary",)),
        name="ada",
    )(c_all, w_ada, b_ada.reshape(1, n))


def _inproj_kernel(x_ref, sh_ref, sc_ref, n1_ref, w_ref, u_ref):
    x = x_ref[...]
    h = x * lax.rsqrt(jnp.mean(x * x, axis=-1, keepdims=True) + EPS) * n1_ref[...]
    h = h * (1.0 + sc_ref[...]) + sh_ref[...]
    u_ref[...] = jnp.dot(h.astype(BF16), w_ref[...], preferred_element_type=F32)


def _inproj(x, mod, per_token, seq, norm1, w_in_bf):
    n = x.shape[0]
    tm = min(512, n)
    tps = max(seq // tm, 1)
    return pl.pallas_call(
        _inproj_kernel,
        grid=(n // tm,),
        in_specs=[pl.BlockSpec((tm, D_MODEL), lambda i: (i, 0)),
                  _mod_spec(per_token, tm, tps, 0),
                  _mod_spec(per_token, tm, tps, 1),
                  pl.BlockSpec((1, D_MODEL), lambda i: (0, 0)),
                  pl.BlockSpec((D_MODEL, D_IN), lambda i: (0, 0))],
        out_specs=pl.BlockSpec((tm, D_IN), lambda i: (i, 0)),
        out_shape=jax.ShapeDtypeStruct((n, D_IN), F32),
        compiler_params=_cparams(("arbitrary",)),
        name="inproj",
    )(x, mod, mod, norm1.reshape(1, D_MODEL), w_in_bf)


def _block_select():
    r = lax.broadcasted_iota(I32, (SUB, SUB * DK), 0)
    c = lax.broadcasted_iota(I32, (SUB, SUB * DK), 1) >> (DK.bit_length() - 1)
    return (r == c).astype(BF16)


def _hgrn_block(q, fp, iv, g, lb, hn, st_ref, sel_ref, valid):
    f = lb + (1.0 - lb) * jax.nn.sigmoid(fp)
    logf = jnp.log(f)
    kk = 1.0 - f
    if valid < SUB:
        live = lax.broadcasted_iota(I32, (SUB, D_A), 0) < valid
        logf = jnp.where(live, logf, 0.0)
        kk = jnp.where(live, kk, 0.0)
    qa = _silu(q)

    r = lax.broadcasted_iota(I32, (SUB, SUB), 0)
    c = lax.broadcasted_iota(I32, (SUB, SUB), 1)
    tri = (r >= c).astype(BF16)
    hi = logf.astype(BF16)
    lo = (logf - hi.astype(F32)).astype(BF16)
    b = (jnp.dot(tri, hi, preferred_element_type=F32)
         + jnp.dot(tri, lo, preferred_element_type=F32))
    bend = b[SUB - 1:SUB, :]
    qt = (qa * jnp.exp(b)).astype(BF16)
    kh = (kk * jnp.exp(bend - b)).astype(BF16)
    dec = jnp.exp(bend)
    ivb = iv.astype(BF16)

    rows = []
    for h in range(H_A):
        sl = slice(DK * h, DK * (h + 1))
        bh, qh, kkh = b[:, sl], qa[:, sl], kk[:, sl]
        ps = []
        for s in range(SUB):
            e = jnp.exp(jnp.minimum(bh - bh[s:s + 1, :], 0.0))
            ps.append((e * (qh * kkh[s:s + 1, :])).astype(BF16))
        rows.append(jnp.concatenate(ps, axis=1))
    pcat = jnp.concatenate(rows, axis=0)
    a = lax.dot_general(pcat, sel_ref[...], (((1,), (1,)), ((), ())),
                        preferred_element_type=F32)
    tr = lax.broadcasted_iota(I32, (H_A * SUB, SUB), 0) & (SUB - 1)
    tc = lax.broadcasted_iota(I32, (H_A * SUB, SUB), 1)
    a = jnp.where(tr >= tc, a, 0.0).astype(BF16)

    outs = []
    for h in range(H_A):
        sl = slice(DK * h, DK * (h + 1))
        st = st_ref[h]
        o = lax.dot_general(qt[:, sl], st.astype(BF16), (((1,), (1,)), ((), ())),
                            preferred_element_type=F32)
        o = o + jnp.dot(a[SUB * h:SUB * (h + 1), :], ivb[:, sl], preferred_element_type=F32)
        ut = lax.dot_general(ivb[:, sl], kh[:, sl], (((0,), (0,)), ((), ())),
                             preferred_element_type=F32)
        st_ref[h] = st * dec[:, sl] + ut
        o = o * lax.rsqrt(jnp.mean(o * o, axis=-1, keepdims=True) + EPS) * hn[:, sl]
        outs.append(o)
    return jnp.concatenate(outs, axis=1) * _silu(g)


def _hgrn_prompt_kernel(q_ref, f_ref, i_ref, g_ref, lb_ref, hn_ref, o_ref, s_ref, st_ref, sel_ref,
                        *, tt):
    t = pl.program_id(1)

    @pl.when(t == 0)
    def _():
        st_ref[...] = jnp.zeros_like(st_ref)
        sel_ref[...] = _block_select()

    lb = lb_ref[...]
    hn = hn_ref[...]

    def body(j, carry):
        rs = pl.ds(pl.multiple_of(j * SUB, SUB), SUB)
        o_ref[rs, :] = _hgrn_block(q_ref[rs, :], f_ref[rs, :], i_ref[rs, :], g_ref[rs, :],
                                   lb, hn, st_ref, sel_ref, SUB)
        return carry

    lax.fori_loop(0, tt // SUB, body, 0)

    @pl.when(t == pl.num_programs(1) - 1)
    def _():
        for h in range(H_A):
            s_ref[h] = st_ref[h].T


def _hgrn_prompt(u, batch, seq, lb, hn):
    tt = 256
    nt = seq // tt

    def col(cb):
        return pl.BlockSpec((tt, D_A), lambda b, t: (b * nt + t, cb))

    return pl.pallas_call(
        functools.partial(_hgrn_prompt_kernel, tt=tt),
        grid=(batch, nt),
        in_specs=[col(0), col(1), col(2), col(3),
                  pl.BlockSpec((1, D_A), lambda b, t: (0, 0)),
                  pl.BlockSpec((1, D_A), lambda b, t: (0, 0))],
        out_specs=[pl.BlockSpec((tt, D_A), lambda b, t: (b * nt + t, 0)),
                   pl.BlockSpec((None, H_A, DK, DV), lambda b, t: (b, 0, 0, 0))],
        out_shape=[jax.ShapeDtypeStruct((batch * seq, D_A), F32),
                   jax.ShapeDtypeStruct((batch, H_A, DK, DV), F32)],
        scratch_shapes=[pltpu.VMEM((H_A, DV, DK), F32), pltpu.VMEM((SUB, SUB * DK), BF16)],
        compiler_params=_cparams(("arbitrary", "arbitrary")),
        name="hgrn_prompt",
    )(u, u, u, u, lb, hn)


def _hgrn_sample_kernel(q_ref, f_ref, i_ref, g_ref, lb_ref, hn_ref, s0_ref, o_ref, s_ref,
                        st_ref, sel_ref, pad_ref, *, seq):
    for h in range(H_A):
        st_ref[h] = s0_ref[h].T
    sel_ref[...] = _block_select()
    pad_ref[...] = jnp.zeros_like(pad_ref)
    for n, ref in enumerate((q_ref, f_ref, i_ref, g_ref)):
        pad_ref[n, 0:seq, :] = ref[...]
    o = _hgrn_block(pad_ref[0], pad_ref[1], pad_ref[2], pad_ref[3], lb_ref[...], hn_ref[...],
                    st_ref, sel_ref, seq)
    o_ref[...] = o[0:seq, :]
    for h in range(H_A):
        s_ref[h] = st_ref[h].T


def _hgrn_sample(u3, s0, lb, hn):
    batch, seq, _ = u3.shape

    def col(cb):
        return pl.BlockSpec((None, seq, D_A), lambda b: (b, 0, cb))

    return pl.pallas_call(
        functools.partial(_hgrn_sample_kernel, seq=seq),
        grid=(batch,),
        in_specs=[col(0), col(1), col(2), col(3),
                  pl.BlockSpec((1, D_A), lambda b: (0, 0)),
                  pl.BlockSpec((1, D_A), lambda b: (0, 0)),
                  pl.BlockSpec((None, H_A, DK, DV), lambda b: (b, 0, 0, 0))],
        out_specs=[pl.BlockSpec((None, seq, D_A), lambda b: (b, 0, 0)),
                   pl.BlockSpec((None, H_A, DK, DV), lambda b: (b, 0, 0, 0))],
        out_shape=[jax.ShapeDtypeStruct((batch, seq, D_A), F32),
                   jax.ShapeDtypeStruct((batch, H_A, DK, DV), F32)],
        scratch_shapes=[pltpu.VMEM((H_A, DV, DK), F32), pltpu.VMEM((SUB, SUB * DK), BF16),
                        pltpu.VMEM((4, SUB, D_A), F32)],
        compiler_params=_cparams(("arbitrary",)),
        name="hgrn_sample",
    )(u3, u3, u3, u3, lb, hn, s0)


HGRN_SEQS = 8
HALF = SUB // 2


def _hgrn_step(q, fp, iv, g, lb, hn, st_ref, sel_ref, cm_ref, b_ref, k_ref, valid):
    nrow = HGRN_SEQS * SUB
    f = lb + (1.0 - lb) * jax.nn.sigmoid(fp)
    lf = jnp.log2(f)
    kk = 1.0 - f
    if valid < SUB:
        live = (lax.broadcasted_iota(I32, (nrow, D_A), 0) & (SUB - 1)) < valid
        lf = jnp.where(live, lf, 0.0)
        kk = jnp.where(live, kk, 0.0)
    qa = _silu(q)

    hi = lf.astype(BF16)
    lo = (lf - hi.astype(F32)).astype(BF16)
    cum, tot = cm_ref[0], cm_ref[1]
    b = jnp.dot(cum, hi, preferred_element_type=F32) + jnp.dot(cum, lo, preferred_element_type=F32)
    be = jnp.dot(tot, hi, preferred_element_type=F32) + jnp.dot(tot, lo, preferred_element_type=F32)
    qt = (qa * jnp.exp2(b)).astype(BF16)
    kh = (kk * jnp.exp2(be - b)).astype(BF16)
    dec = jnp.exp2(be)
    ivb = iv.astype(BF16)
    for h in range(H_A):
        b_ref[h] = b[:, DK * h:DK * (h + 1)]
        k_ref[h] = kk[:, DK * h:DK * (h + 1)]

    zero_half = jnp.zeros((HALF, D_A), F32)
    tr = lax.broadcasted_iota(I32, (H_A * SUB, SUB), 0) & (SUB - 1)
    tc = lax.broadcasted_iota(I32, (H_A * SUB, SUB), 1)
    sg = _silu(g)
    inter = []
    for n in range(HGRN_SEQS):
        rows = slice(n * SUB, (n + 1) * SUB)
        for h in range(H_A):
            sl = slice(DK * h, DK * (h + 1))
            st = st_ref[n, h]
            inter.append(lax.dot_general(qt[rows, sl], st.astype(BF16), (((1,), (1,)), ((), ())),
                                         preferred_element_type=F32))
            ut = lax.dot_general(ivb[rows, sl], kh[rows, sl], (((0,), (0,)), ((), ())),
                                 preferred_element_type=F32)
            dtile = dec[n * SUB:n * SUB + HALF, sl]
            st_ref[n, h] = (st.reshape(DV // HALF, HALF, DK) * dtile[None]).reshape(DV, DK) + ut

    scores = []
    for n in range(HGRN_SEQS):
        per_s = []
        for s in range(SUB):
            row = n * SUB + s
            b_s = jnp.concatenate([b_ref[h, pl.ds(row, HALF, stride=0), :] for h in range(H_A)], axis=1)
            k_s = jnp.concatenate([k_ref[h, pl.ds(row, HALF, stride=0), :] for h in range(H_A)], axis=1)
            halves = []
            for hf in range(2):
                if hf == 0 and s >= HALF:
                    halves.append(zero_half)
                    continue
                r0 = n * SUB + hf * HALF
                e = jnp.exp2(jnp.minimum(b[r0:r0 + HALF] - b_s, 0.0))
                halves.append(e * (qa[r0:r0 + HALF] * k_s))
            per_s.append(jnp.concatenate(halves, axis=0).astype(BF16))
        pcat = jnp.concatenate(
            [jnp.concatenate([p[:, DK * h:DK * (h + 1)] for p in per_s], axis=1) for h in range(H_A)],
            axis=0)
        a = lax.dot_general(pcat, sel_ref[...], (((1,), (1,)), ((), ())),
                            preferred_element_type=F32)
        scores.append(jnp.where(tr >= tc, a, 0.0).astype(BF16))

    outs = []
    for n in range(HGRN_SEQS):
        rows = slice(n * SUB, (n + 1) * SUB)
        heads = []
        for h in range(H_A):
            sl = slice(DK * h, DK * (h + 1))
            o = inter[n * H_A + h] + jnp.dot(scores[n][h * SUB:(h + 1) * SUB, :], ivb[rows, sl],
                                             preferred_element_type=F32)
            o = o * lax.rsqrt(jnp.mean(o * o, axis=-1, keepdims=True) + EPS) * hn[:, sl]
            heads.append(o)
        outs.append(jnp.concatenate(heads, axis=1) * sg[rows])
    return jnp.concatenate(outs, axis=0)


def _hgrn_kernel(*refs, rows, has_s0):
    if has_s0:
        (q_ref, f_ref, i_ref, g_ref, lb_ref, hn_ref, s0_ref, o_ref, s_ref,
         st_ref, sel_ref, cm_ref, b_ref, k_ref, pad_ref) = refs
    else:
        (q_ref, f_ref, i_ref, g_ref, lb_ref, hn_ref, o_ref, s_ref,
         st_ref, sel_ref, cm_ref, b_ref, k_ref) = refs
    t = pl.program_id(1)
    nrow = HGRN_SEQS * SUB

    @pl.when((t == 0) & (pl.program_id(0) == 0))
    def _():
        sel_ref[...] = _block_select()
        r = lax.broadcasted_iota(I32, (nrow, nrow), 0)
        c = lax.broadcasted_iota(I32, (nrow, nrow), 1)
        shift = SUB.bit_length() - 1
        same = (r >> shift) == (c >> shift)
        cm_ref[0] = (same & (r >= c)).astype(BF16)
        cm_ref[1] = same.astype(BF16)

    lb = lb_ref[...]
    hn = hn_ref[...]
    if has_s0:
        for n in range(HGRN_SEQS):
            for h in range(H_A):
                st_ref[n, h] = s0_ref[n, h].T
        pad_ref[...] = jnp.zeros_like(pad_ref)
        for j, ref in enumerate((q_ref, f_ref, i_ref, g_ref)):
            pad_ref[j, :, 0:rows, :] = ref[...]
        o = _hgrn_step(*(pad_ref[j].reshape(nrow, D_A) for j in range(4)), lb, hn,
                       st_ref, sel_ref, cm_ref, b_ref, k_ref, rows)
        o_ref[...] = o.reshape(HGRN_SEQS, SUB, D_A)[:, 0:rows, :]
    else:
        @pl.when(t == 0)
        def _():
            st_ref[...] = jnp.zeros_like(st_ref)

        def body(j, carry):
            rs = pl.ds(pl.multiple_of(j * SUB, SUB), SUB)
            o = _hgrn_step(*(ref[:, rs, :].reshape(nrow, D_A) for ref in (q_ref, f_ref, i_ref, g_ref)),
                           lb, hn, st_ref, sel_ref, cm_ref, b_ref, k_ref, SUB)
            o_ref[:, rs, :] = o.reshape(HGRN_SEQS, SUB, D_A)
            return carry

        lax.fori_loop(0, rows // SUB, body, 0)

    def emit_state():
        for n in range(HGRN_SEQS):
            for h in range(H_A):
                s_ref[n, h] = st_ref[n, h].T

    if has_s0:
        emit_state()
    else:
        pl.when(t == pl.num_programs(1) - 1)(emit_state)


def _hgrn(u3, s0, lb, hn):
    batch, seq, _ = u3.shape
    has_s0 = s0 is not None
    rows = seq if has_s0 else 128
    grid = (batch // HGRN_SEQS, seq // rows)
    blk = lambda cb: pl.BlockSpec((HGRN_SEQS, rows, D_A), lambda b, t: (b, t, cb))
    sblk = pl.BlockSpec((HGRN_SEQS, H_A, DK, DV), lambda b, t: (b, 0, 0, 0))
    vec = pl.BlockSpec((1, D_A), lambda b, t: (0, 0))
    nrow = HGRN_SEQS * SUB
    scratch = [pltpu.VMEM((HGRN_SEQS, H_A, DV, DK), F32), pltpu.VMEM((SUB, SUB * DK), BF16),
               pltpu.VMEM((2, nrow, nrow), BF16), pltpu.VMEM((H_A, nrow, DK), F32),
               pltpu.VMEM((H_A, nrow, DK), F32)]
    if has_s0:
        scratch.append(pltpu.VMEM((4, HGRN_SEQS, SUB, D_A), F32))
    args = (u3, u3, u3, u3, lb, hn) + ((s0,) if has_s0 else ())
    return pl.pallas_call(
        functools.partial(_hgrn_kernel, rows=rows, has_s0=has_s0),
        grid=grid,
        in_specs=[blk(0), blk(1), blk(2), blk(3), vec, vec] + ([sblk] if has_s0 else []),
        out_specs=[blk(0), sblk],
        out_shape=[jax.ShapeDtypeStruct((batch, seq, D_A), F32),
                   jax.ShapeDtypeStruct((batch, H_A, DK, DV), F32)],
        scratch_shapes=scratch,
        compiler_params=_cparams(("arbitrary", "arbitrary")),
        name="hgrn_state" if has_s0 else "hgrn_scan",
    )(*args)


def _pool_groups(full_ref, base, tm, cnt_fn, wp_ref, ps_ref, step=1):
    outs = []
    for gi, w in enumerate(POOL_WINDOWS):
        ch = slice(POOL_GROUP * gi, POOL_GROUP * (gi + 1))
        v = full_ref[base:base + tm, ch]
        s = v
        for j in range(1, w):
            s = s + full_ref[base - j * step:base - j * step + tm, ch]
        d = s / cnt_fn(w) - v
        outs.append(jnp.dot(d.astype(BF16), wp_ref[gi], preferred_element_type=F32))
    return jnp.concatenate(outs, axis=1) * ps_ref[...]


def _pool_prompt_kernel(v_ref, prev_ref, wp_ref, ps_ref, o_ref, hist_ref, full_ref, *, tm):
    t = pl.program_id(1)
    prev = prev_ref[...]
    full_ref[0:16, :] = jnp.where(t == 0, 0.0, prev)
    full_ref[16:16 + tm, :] = v_ref[...]
    pos = (lax.broadcasted_iota(I32, (tm, 1), 0) + t * tm + 1).astype(F32)
    o_ref[...] = _pool_groups(full_ref, 16, tm, lambda w: jnp.minimum(pos, float(w)),
                              wp_ref, ps_ref)

    @pl.when(t == pl.num_programs(1) - 1)
    def _():
        hist_ref[...] = full_ref[16 + tm - POOL_HIST:16 + tm, :]


def _pool_prompt(u, batch, seq, w_pool_bf, pool_scale):
    tm = 256
    nt = seq // tm
    vcol = (4 * D_A) // D_B
    return pl.pallas_call(
        functools.partial(_pool_prompt_kernel, tm=tm),
        grid=(batch, nt),
        in_specs=[pl.BlockSpec((tm, D_B), lambda b, t: (b * nt + t, vcol)),
                  pl.BlockSpec((16, D_B),
                               lambda b, t: (jnp.maximum((b * nt + t) * (tm // 16) - 1, 0), vcol)),
                  pl.BlockSpec((len(POOL_WINDOWS), POOL_GROUP, POOL_GROUP), lambda b, t: (0, 0, 0)),
                  pl.BlockSpec((1, D_B), lambda b, t: (0, 0))],
        out_specs=[pl.BlockSpec((tm, D_B), lambda b, t: (b * nt + t, 0)),
                   pl.BlockSpec((None, POOL_HIST, D_B), lambda b, t: (b, 0, 0))],
        out_shape=[jax.ShapeDtypeStruct((batch * seq, D_B), F32),
                   jax.ShapeDtypeStruct((batch, POOL_HIST, D_B), F32)],
        scratch_shapes=[pltpu.VMEM((16 + tm, D_B), F32)],
        compiler_params=_cparams(("arbitrary", "arbitrary")),
        name="pool_prompt",
    )(u, u, w_pool_bf, pool_scale)


def _pool_sample_kernel(v_ref, hist_ref, wp_ref, ps_ref, o_ref, nh_ref, full_ref, *,
                        seq, batch, start):
    full_ref[0:batch, :] = jnp.zeros((batch, D_B), F32)
    full_ref[batch:16 * batch, :] = hist_ref[...].reshape(POOL_HIST * batch, D_B)
    full_ref[16 * batch:(16 + seq) * batch, :] = v_ref[...].reshape(seq * batch, D_B)

    def cnt(w):
        rows = [jnp.full((batch, 1), float(min(start + t + 1, w)), F32) for t in range(seq)]
        return jnp.concatenate(rows, axis=0)

    o = _pool_groups(full_ref, 16 * batch, seq * batch, cnt, wp_ref, ps_ref, step=batch)
    o_ref[...] = o.reshape(seq, batch, D_B)
    nh_ref[...] = full_ref[(16 + seq - POOL_HIST) * batch:(16 + seq) * batch, :].reshape(
        POOL_HIST, batch, D_B)


def _pool_sample(u_t, hist_t, start, w_pool_bf, pool_scale):
    seq, batch, _ = u_t.shape
    vcol = (4 * D_A) // D_B
    return pl.pallas_call(
        functools.partial(_pool_sample_kernel, seq=seq, batch=batch, start=start),
        grid=(1,),
        in_specs=[pl.BlockSpec((seq, batch, D_B), lambda i: (0, 0, vcol)),
                  pl.BlockSpec((POOL_HIST, batch, D_B), lambda i: (0, 0, 0)),
                  pl.BlockSpec((len(POOL_WINDOWS), POOL_GROUP, POOL_GROUP), lambda i: (0, 0, 0)),
                  pl.BlockSpec((1, D_B), lambda i: (0, 0))],
        out_specs=[pl.BlockSpec((seq, batch, D_B), lambda i: (0, 0, 0)),
                   pl.BlockSpec((POOL_HIST, batch, D_B), lambda i: (0, 0, 0))],
        out_shape=[jax.ShapeDtypeStruct((seq, batch, D_B), F32),
                   jax.ShapeDtypeStruct((POOL_HIST, batch, D_B), F32)],
        scratch_shapes=[pltpu.VMEM(((16 + seq) * batch, D_B), F32)],
        compiler_params=_cparams(("arbitrary",)),
        name="pool_sample",
    )(u_t, hist_t, w_pool_bf, pool_scale)


LANES = 128
SLAB = D_MODEL // LANES


def _store_slabs(ref, value):
    n = value.shape[0]
    for j in range(SLAB):
        ref[pl.ds(j, n, stride=SLAB), :] = value[:, LANES * j:LANES * (j + 1)]


def _load_slabs(ref, n):
    return [ref[pl.ds(j, n, stride=SLAB), :] for j in range(SLAB)]


def _slab_rows(row):
    return pl.ds(pl.multiple_of(row * SLAB, SLAB), SLAB)


def _split_dot(a, w_hi, w_lo):
    a_hi = a.astype(BF16)
    a_lo = (a - a_hi.astype(F32)).astype(BF16)
    return (jnp.dot(a_hi, w_hi, preferred_element_type=F32)
            + jnp.dot(a_hi, w_lo, preferred_element_type=F32)
            + jnp.dot(a_lo, w_hi, preferred_element_type=F32))


def _outproj_kernel(x_ref, oa_ref, ob_ref, g1_ref, sh2_ref, sc2_ref, n2_ref, woa_ref, wob_ref,
                    wrh_ref, wrl_ref, br_ref, *rest):
    x1_ref, h2_ref, idx_ref, prob_ref, lg_ref = rest[-5:]
    i = pl.program_id(0)
    slot = i % 2

    @pl.when(i == 0)
    def _():
        lg_ref[...] = jnp.zeros_like(lg_ref)

    logits = lg_ref[1 - slot]

    mix = (jnp.dot(oa_ref[...].astype(BF16), woa_ref[...], preferred_element_type=F32)
           + jnp.dot(ob_ref[...].astype(BF16), wob_ref[...], preferred_element_type=F32))
    x1 = x_ref[...] + g1_ref[...] * mix
    x1_ref[...] = x1
    h2 = x1 * lax.rsqrt(jnp.mean(x1 * x1, axis=-1, keepdims=True) + EPS) * n2_ref[...]
    h2 = h2 * (1.0 + sc2_ref[...]) + sh2_ref[...]
    _store_slabs(h2_ref, h2)
    lg_ref[slot] = _split_dot(h2, wrh_ref[...], wrl_ref[...]) + br_ref[...]

    tm = logits.shape[0]
    lane = lax.broadcasted_iota(I32, (tm, N_EXPERTS), 1)
    kcol = lax.broadcasted_iota(I32, (tm, TOP_K), 1)
    vals = jnp.zeros((tm, TOP_K), F32)
    idxs = jnp.zeros((tm, TOP_K), I32)
    for k in range(TOP_K):
        m = jnp.max(logits, axis=-1, keepdims=True)
        am = jnp.min(jnp.where(logits == m, lane, N_EXPERTS), axis=-1, keepdims=True)
        vals = jnp.where(kcol == k, m, vals)
        idxs = jnp.where(kcol == k, am, idxs)
        logits = jnp.where(lane == am, -jnp.inf, logits)
    e = jnp.exp(vals - vals[:, 0:1])
    prob_ref[...] = e / jnp.sum(e, axis=-1, keepdims=True)
    idx_ref[...] = idxs


def _outproj(x, o_a, o_b, mod, per_token, seq, norm2, w_out_bf, wr_hi, wr_lo, b_router,
             n_total, tile0, joint=None):
    n = x.shape[0]
    tm = min(256, n)
    tps = max(seq // tm, 1)
    nt = n // tm
    cur = lambda i: jnp.minimum(i, nt - 1)
    prv = lambda i: jnp.maximum(i - 1, 0)
    row = lambda width: pl.BlockSpec((tm, width), lambda i: (cur(i), 0))
    jrow = lambda width: pl.BlockSpec((tm, width), lambda i: (prv(i) + tile0, 0))
    const = lambda shape: pl.BlockSpec(shape, lambda i: (0,) * len(shape))
    joint = () if joint is None else tuple(joint)
    n_in = 12
    return pl.pallas_call(
        _outproj_kernel,
        grid=(nt + 1,),
        in_specs=[row(D_MODEL), row(D_A), row(D_B),
                  _mod_spec(per_token, tm, tps, 2, nt - 1),
                  _mod_spec(per_token, tm, tps, 3, nt - 1),
                  _mod_spec(per_token, tm, tps, 4, nt - 1),
                  const((1, D_MODEL)),
                  pl.BlockSpec((D_A, D_MODEL), lambda i: (0, 0)),
                  pl.BlockSpec((D_B, D_MODEL), lambda i: (1, 0)),
                  const((D_MODEL, N_EXPERTS)), const((D_MODEL, N_EXPERTS)), const((1, N_EXPERTS))]
                 + [pl.BlockSpec(memory_space=pl.ANY)] * len(joint),
        out_specs=[row(D_MODEL), pl.BlockSpec((tm * SLAB, LANES), lambda i: (cur(i) + tile0, 0)),
                   jrow(TOP_K), jrow(TOP_K)],
        out_shape=[jax.ShapeDtypeStruct((n, D_MODEL), F32),
                   jax.ShapeDtypeStruct((n_total * SLAB, LANES), F32),
                   jax.ShapeDtypeStruct((n_total, TOP_K), I32),
                   jax.ShapeDtypeStruct((n_total, TOP_K), F32)],
        input_output_aliases={n_in + j: 1 + j for j in range(len(joint))},
        scratch_shapes=[pltpu.VMEM((2, tm, N_EXPERTS), F32)],
        compiler_params=_cparams(("arbitrary",)),
        name="outproj",
    )(x, o_a, o_b, mod, mod, mod, norm2.reshape(1, D_MODEL), w_out_bf, w_out_bf,
      wr_hi, wr_lo, b_router.reshape(1, N_EXPERTS), *joint)


def _rank_kernel(idx_ref, dest_ref, cnt_ref, carry_ref, *, te):
    phase = pl.program_id(0)
    i = pl.program_id(1)
    idx = idx_ref[...]
    lane = lax.broadcasted_iota(I32, (te, N_EXPERTS), 1)
    onehots = [(idx[:, k:k + 1] == lane) for k in range(TOP_K)]
    member = jnp.zeros((te, N_EXPERTS), F32)
    for oh in onehots:
        member = member + oh.astype(F32)
    colsum = jnp.sum(member, axis=0, keepdims=True)

    @pl.when((phase == 0) & (i == 0))
    def _():
        carry_ref[...] = jnp.zeros_like(carry_ref)

    @pl.when(phase == 0)
    def _():
        carry_ref[...] += colsum

    @pl.when((phase == 1) & (i == 0))
    def _():
        cnt = carry_ref[...]
        cnt_ref[...] = cnt.astype(I32)
        tiles = jnp.floor((cnt + (ROW_TILE - 1)) * (1.0 / ROW_TILE))
        r = lax.broadcasted_iota(I32, (N_EXPERTS, N_EXPERTS), 0)
        c = lax.broadcasted_iota(I32, (N_EXPERTS, N_EXPERTS), 1)
        before = (r < c).astype(BF16)
        carry_ref[...] = ROW_TILE * jnp.dot(tiles.astype(BF16), before, preferred_element_type=F32)

    @pl.when(phase == 1)
    def _():
        r = lax.broadcasted_iota(I32, (te, te), 0)
        c = lax.broadcasted_iota(I32, (te, te), 1)
        earlier = (r > c).astype(BF16)
        base = carry_ref[...] + jnp.dot(earlier, member.astype(BF16), preferred_element_type=F32)
        kcol = lax.broadcasted_iota(I32, (te, TOP_K), 1)
        dest = jnp.zeros((te, TOP_K), F32)
        for k, oh in enumerate(onehots):
            dk = jnp.sum(jnp.where(oh, base, 0.0), axis=-1, keepdims=True)
            dest = jnp.where(kcol == k, dk, dest)
        dest_ref[...] = dest.astype(I32)
        carry_ref[...] += colsum


def _rank(top_idx):
    n = top_idx.shape[0]
    te = 512
    return pl.pallas_call(
        functools.partial(_rank_kernel, te=te),
        grid=(2, n // te),
        in_specs=[pl.BlockSpec((te, TOP_K), lambda p, i: (i, 0))],
        out_specs=[pl.BlockSpec((te, TOP_K), lambda p, i: (p * i, 0)),
                   pl.BlockSpec((1, N_EXPERTS), lambda p, i: (0, 0))],
        out_shape=[jax.ShapeDtypeStruct((n, TOP_K), I32),
                   jax.ShapeDtypeStruct((1, N_EXPERTS), I32)],
        scratch_shapes=[pltpu.VMEM((1, N_EXPERTS), F32)],
        compiler_params=_cparams(("arbitrary", "arbitrary")),
        name="rank",
    )(top_idx)


def _dispatch_kernel(last_ref, ntile_ref, used_ref, dest_ref, h_ref, xs_ref, zero_ref, sem, zsem,
                     *, n_tiles):
    i = pl.program_id(0)

    @pl.when(i == 0)
    def _():
        zero_ref[...] = jnp.zeros_like(zero_ref)

        def clear(tile):
            rows = pl.ds(pl.multiple_of(tile * (ROW_TILE * SLAB), ROW_TILE * SLAB), ROW_TILE * SLAB)
            return pltpu.make_async_copy(zero_ref, xs_ref.at[rows], zsem)

        def over_tail(fn):
            def body(t, carry):
                fn(clear(t))
                return carry
            lax.fori_loop(used_ref[0], n_tiles, body, 0)

        for fn in (lambda c: c.start(), lambda c: c.wait()):
            for e in range(N_EXPERTS):
                @pl.when(ntile_ref[e] > 0)
                def _():
                    fn(clear(last_ref[e]))
            over_tail(fn)

    def body(r, carry):
        for k in range(TOP_K):
            d = dest_ref[r * TOP_K + k]
            pltpu.make_async_copy(h_ref.at[_slab_rows(r)], xs_ref.at[_slab_rows(d)], sem).start(
                priority=k % 2)
        return carry

    lax.fori_loop(0, TOK_TILE, body, 0, unroll=2)
    for k in range(TOP_K):
        pltpu.make_async_copy(h_ref, xs_ref.at[pl.ds(0, TOK_TILE * SLAB)], sem).wait()


def _dispatch(h2, dest_flat, last_tile, ntiles, used, n_tiles):
    n = h2.shape[0] // SLAB
    grid_spec = pltpu.PrefetchScalarGridSpec(
        num_scalar_prefetch=3,
        grid=(n // TOK_TILE,),
        in_specs=[pl.BlockSpec((TOK_TILE * TOP_K,), lambda i, *_: (i,), memory_space=pltpu.SMEM),
                  pl.BlockSpec((TOK_TILE * SLAB, LANES), lambda i, *_: (i, 0))],
        out_specs=pl.BlockSpec(memory_space=pl.ANY),
        scratch_shapes=[pltpu.VMEM((ROW_TILE * SLAB, LANES), F32),
                        pltpu.SemaphoreType.DMA(()), pltpu.SemaphoreType.DMA(())],
    )
    return pl.pallas_call(
        functools.partial(_dispatch_kernel, n_tiles=n_tiles),
        grid_spec=grid_spec,
        out_shape=jax.ShapeDtypeStruct((n_tiles * ROW_TILE * SLAB, LANES), F32),
        compiler_params=_cparams(("arbitrary",)),
        name="dispatch",
    )(last_tile, ntiles, used, dest_flat, h2)


def _experts_kernel(te_ref, used_ref, xs_ref, wgu_ref, bgu_ref, wd_ref, bd_ref, ys_ref,
                    wgu_bf, wd_bf, act_bf):
    i = pl.program_id(0)
    prev = te_ref[jnp.maximum(i - 1, 0)]
    fresh = (i == 0) | (te_ref[i] != prev)

    @pl.when(fresh)
    def _():
        wgu_bf[...] = wgu_ref[...].astype(BF16)
        wd_bf[...] = wd_ref[...].astype(BF16)

    @pl.when(i < used_ref[0])
    def _():
        x = jnp.concatenate([p.astype(BF16) for p in _load_slabs(xs_ref, ROW_TILE)], axis=1)
        cw = D_FF // FF_CHUNKS
        for c in range(FF_CHUNKS):
            gate = jnp.dot(x, wgu_bf[:, c * cw:(c + 1) * cw], preferred_element_type=F32)
            gate = jnp.minimum(gate + bgu_ref[:, c * cw:(c + 1) * cw], SWIGLU_LIMIT)
            up = jnp.dot(x, wgu_bf[:, D_FF + c * cw:D_FF + (c + 1) * cw], preferred_element_type=F32)
            up = jnp.clip(up + bgu_ref[:, D_FF + c * cw:D_FF + (c + 1) * cw], -SWIGLU_LIMIT, SWIGLU_LIMIT)
            act = gate * jax.nn.sigmoid(SWIGLU_ALPHA * gate) * (up + 1.0)
            act_bf[:, c * cw:(c + 1) * cw] = act.astype(BF16)
        ow = D_MODEL // FF_CHUNKS
        for c in range(FF_CHUNKS):
            out = jnp.dot(act_bf[...], wd_bf[:, c * ow:(c + 1) * ow], preferred_element_type=F32)
            out = out + bd_ref[:, c * ow:(c + 1) * ow]
            for j in range(ow // LANES):
                ys_ref[pl.ds(c * (ow // LANES) + j, ROW_TILE, stride=SLAB), :] = (
                    out[:, LANES * j:LANES * (j + 1)])

    @pl.when(i >= used_ref[0])
    def _():
        ys_ref[...] = jnp.zeros_like(ys_ref)


def _experts(xs, tile_expert, used, w_gu, b_gu, w_down, b_down):
    rows = xs.shape[0] // SLAB
    nt = rows // ROW_TILE

    def tile(i, te, used):
        return (jnp.minimum(i, used[0] - 1), 0)

    grid_spec = pltpu.PrefetchScalarGridSpec(
        num_scalar_prefetch=2,
        grid=(nt,),
        in_specs=[pl.BlockSpec((ROW_TILE * SLAB, LANES), tile),
                  pl.BlockSpec((None, D_MODEL, 2 * D_FF), lambda i, te, used: (te[i], 0, 0)),
                  pl.BlockSpec((None, 1, 2 * D_FF), lambda i, te, used: (te[i], 0, 0)),
                  pl.BlockSpec((None, D_FF, D_MODEL), lambda i, te, used: (te[i], 0, 0)),
                  pl.BlockSpec((None, 1, D_MODEL), lambda i, te, used: (te[i], 0, 0))],
        out_specs=pl.BlockSpec((ROW_TILE * SLAB, LANES), lambda i, te, used: (i, 0)),
        scratch_shapes=[pltpu.VMEM((D_MODEL, 2 * D_FF), BF16), pltpu.VMEM((D_FF, D_MODEL), BF16),
                        pltpu.VMEM((ROW_TILE, D_FF), BF16)],
    )
    return pl.pallas_call(
        _experts_kernel,
        grid_spec=grid_spec,
        out_shape=jax.ShapeDtypeStruct((rows * SLAB, LANES), F32),
        compiler_params=_cparams(("arbitrary",)),
        name="experts",
    )(tile_expert, used, xs, w_gu, b_gu.reshape(N_EXPERTS, 1, 2 * D_FF), w_down,
      b_down.reshape(N_EXPERTS, 1, D_MODEL))


X_RING = 3


def _mlp_kernel(te_ref, used_ref, grp_ref, nxt_ref, xs_ref, wgu_ref, bgu_ref, wd_ref, bd_ref, ys_ref,
                wgu_f, wd_f, wgu_bf, wd_bf, act_bf, xring, wsem, xsem):
    i = pl.program_id(0)
    used = used_ref[0]
    tile_rows = ROW_TILE * SLAB

    def weights(e, ws):
        return (pltpu.make_async_copy(wgu_ref.at[e], wgu_f.at[ws], wsem.at[ws, 0]),
                pltpu.make_async_copy(wd_ref.at[e], wd_f.at[ws], wsem.at[ws, 1]))

    def x_tile(t):
        t = jnp.asarray(t, I32)
        rows = pl.ds(pl.multiple_of(t * tile_rows, tile_rows), tile_rows)
        return pltpu.make_async_copy(xs_ref.at[rows], xring.at[t % X_RING], xsem.at[t % X_RING])

    @pl.when(i == 0)
    def _():
        for cp in weights(te_ref[0], 0):
            cp.start()
        for t in range(X_RING - 1):
            @pl.when(t < used)
            def _():
                x_tile(t).start()

    @pl.when(i + X_RING - 1 < used)
    def _():
        x_tile(i + X_RING - 1).start()

    ws = grp_ref[i] % 2
    fresh = (i == 0) | (te_ref[i] != te_ref[jnp.maximum(i - 1, 0)])

    @pl.when(fresh & (i < used))
    def _():
        for cp in weights(te_ref[i], ws):
            cp.wait()

        @pl.when(nxt_ref[i] >= 0)
        def _():
            for cp in weights(nxt_ref[i], 1 - ws):
                cp.start()

        wgu_bf[...] = wgu_f[ws].astype(BF16)
        wd_bf[...] = wd_f[ws].astype(BF16)

    @pl.when(i < used)
    def _():
        x_tile(i).wait()
        x = jnp.concatenate([p.astype(BF16) for p in _load_slabs(xring.at[i % X_RING], ROW_TILE)],
                            axis=1)
        cw = D_FF // FF_CHUNKS
        for c in range(FF_CHUNKS):
            gate = jnp.dot(x, wgu_bf[:, c * cw:(c + 1) * cw], preferred_element_type=F32)
            gate = jnp.minimum(gate + bgu_ref[:, c * cw:(c + 1) * cw], SWIGLU_LIMIT)
            up = jnp.dot(x, wgu_bf[:, D_FF + c * cw:D_FF + (c + 1) * cw], preferred_element_type=F32)
            up = jnp.clip(up + bgu_ref[:, D_FF + c * cw:D_FF + (c + 1) * cw], -SWIGLU_LIMIT, SWIGLU_LIMIT)
            act = gate * jax.nn.sigmoid(SWIGLU_ALPHA * gate) * (up + 1.0)
            act_bf[:, c * cw:(c + 1) * cw] = act.astype(BF16)
        ow = D_MODEL // FF_CHUNKS
        for c in range(FF_CHUNKS):
            out = jnp.dot(act_bf[...], wd_bf[:, c * ow:(c + 1) * ow], preferred_element_type=F32)
            out = out + bd_ref[:, c * ow:(c + 1) * ow]
            for j in range(ow // LANES):
                ys_ref[pl.ds(c * (ow // LANES) + j, ROW_TILE, stride=SLAB), :] = (
                    out[:, LANES * j:LANES * (j + 1)])

    @pl.when(i >= used)
    def _():
        ys_ref[...] = jnp.zeros_like(ys_ref)


def _mlp(xs, tile_expert, used, grp, nxt, w_gu, b_gu, w_down, b_down):
    rows = xs.shape[0] // SLAB
    nt = rows // ROW_TILE
    ex = lambda i, te, *_: (te[i], 0, 0)
    grid_spec = pltpu.PrefetchScalarGridSpec(
        num_scalar_prefetch=4,
        grid=(nt,),
        in_specs=[pl.BlockSpec(memory_space=pl.ANY),
                  pl.BlockSpec(memory_space=pl.ANY),
                  pl.BlockSpec((None, 1, 2 * D_FF), ex),
                  pl.BlockSpec(memory_space=pl.ANY),
                  pl.BlockSpec((None, 1, D_MODEL), ex)],
        out_specs=pl.BlockSpec((ROW_TILE * SLAB, LANES), lambda i, *_: (i, 0)),
        scratch_shapes=[pltpu.VMEM((2, D_MODEL, 2 * D_FF), F32), pltpu.VMEM((2, D_FF, D_MODEL), F32),
                        pltpu.VMEM((D_MODEL, 2 * D_FF), BF16), pltpu.VMEM((D_FF, D_MODEL), BF16),
                        pltpu.VMEM((ROW_TILE, D_FF), BF16),
                        pltpu.VMEM((X_RING, ROW_TILE * SLAB, LANES), F32),
                        pltpu.SemaphoreType.DMA((2, 2)), pltpu.SemaphoreType.DMA((X_RING,))],
    )
    return pl.pallas_call(
        _mlp_kernel,
        grid_spec=grid_spec,
        out_shape=jax.ShapeDtypeStruct((rows * SLAB, LANES), F32),
        compiler_params=_cparams(("arbitrary",)),
        name="experts",
    )(tile_expert, used, grp, nxt, xs, w_gu, b_gu.reshape(N_EXPERTS, 1, 2 * D_FF), w_down,
      b_down.reshape(N_EXPERTS, 1, D_MODEL))


def _combine_kernel(dest_ref, next_ref, ys_ref, x1_ref, p_ref, g2_ref, fn_ref, y_ref, buf_ref, sem,
                    *, last):
    i = pl.program_id(0)
    slot = i % 2

    def issue(d_ref, sl):
        def body(r, carry):
            for k in range(TOP_K):
                d = d_ref[r * TOP_K + k]
                pltpu.make_async_copy(ys_ref.at[_slab_rows(d)], buf_ref.at[sl, k, _slab_rows(r)],
                                      sem.at[sl]).start(priority=k % 2)
            return carry
        lax.fori_loop(0, TOK_TILE, body, 0, unroll=2)

    @pl.when(i == 0)
    def _():
        issue(dest_ref, 0)

    @pl.when(i + 1 < pl.num_programs(0))
    def _():
        issue(next_ref, 1 - slot)

    for k in range(TOP_K):
        pltpu.make_async_copy(ys_ref.at[pl.ds(0, TOK_TILE * SLAB)], buf_ref.at[slot, k],
                              sem.at[slot]).wait()

    p = p_ref[...]
    groups = None
    for k in range(TOP_K):
        part = [p[:, k:k + 1] * g for g in _load_slabs(buf_ref.at[slot, k], TOK_TILE)]
        groups = part if groups is None else [a + b for a, b in zip(groups, part)]
    moe = jnp.concatenate(groups, axis=1)
    x2 = x1_ref[...] + g2_ref[...] * moe
    if last:
        x2 = x2 * lax.rsqrt(jnp.mean(x2 * x2, axis=-1, keepdims=True) + EPS) * fn_ref[...]
    y_ref[...] = x2


def _combine(ys, dest_flat, x1, probs, mod, per_token, seq, tile0, final_norm, last):
    n = x1.shape[0]
    tm = TOK_TILE
    tps = max(seq // tm, 1)
    nt = n // tm
    return pl.pallas_call(
        functools.partial(_combine_kernel, last=last),
        grid=(nt,),
        in_specs=[pl.BlockSpec((tm * TOP_K,), lambda i: (i + tile0,), memory_space=pltpu.SMEM),
                  pl.BlockSpec((tm * TOP_K,), lambda i: (jnp.minimum(i + 1, nt - 1) + tile0,),
                               memory_space=pltpu.SMEM),
                  pl.BlockSpec(memory_space=pl.ANY),
                  pl.BlockSpec((tm, D_MODEL), lambda i: (i, 0)),
                  pl.BlockSpec((tm, TOP_K), lambda i: (i + tile0, 0)),
                  _mod_spec(per_token, tm, tps, 5),
                  pl.BlockSpec((1, D_MODEL), lambda i: (0, 0))],
        out_specs=pl.BlockSpec((tm, D_MODEL), lambda i: (i, 0)),
        out_shape=jax.ShapeDtypeStruct((n, D_MODEL), F32),
        scratch_shapes=[pltpu.VMEM((2, TOP_K, tm * SLAB, LANES), F32),
                        pltpu.SemaphoreType.DMA((2,))],
        compiler_params=_cparams(("arbitrary",)),
        name="combine",
    )(dest_flat, dest_flat, ys, x1, probs, mod, final_norm.reshape(1, D_MODEL))


def _invert_kernel(dest_ref, src0_ref, dst0_ref, src_ref, dst_ref, sem, *, n_tok):
    i = pl.program_id(0)
    blk = dest_ref.shape[0]

    @pl.when(i == 0)
    def _():
        copies = [pltpu.make_async_copy(src0_ref, src_ref, sem.at[0]),
                  pltpu.make_async_copy(dst0_ref, dst_ref, sem.at[1])]
        for cp in copies:
            cp.start()
        for cp in copies:
            cp.wait()

    def body(a, carry):
        g = i * blk + a
        d = dest_ref[a]
        tok = g >> (TOP_K.bit_length() - 1)
        src_ref[d] = tok
        dst_ref[d] = (g & (TOP_K - 1)) * n_tok + tok
        return carry

    lax.fori_loop(0, blk, body, 0, unroll=8)


def _invert(dest_flat, n_tok, n_rows):
    blk = TOK_TILE * TOP_K
    whole = pl.BlockSpec(memory_space=pltpu.SMEM)
    src0 = jnp.zeros((n_rows,), I32)
    dst0 = n_tok * TOP_K + jnp.arange(n_rows, dtype=I32)
    return pl.pallas_call(
        functools.partial(_invert_kernel, n_tok=n_tok),
        grid=(dest_flat.shape[0] // blk,),
        in_specs=[pl.BlockSpec((blk,), lambda i: (i,), memory_space=pltpu.SMEM),
                  pl.BlockSpec(memory_space=pl.ANY), pl.BlockSpec(memory_space=pl.ANY)],
        out_specs=[whole, whole],
        out_shape=[jax.ShapeDtypeStruct((n_rows,), I32), jax.ShapeDtypeStruct((n_rows,), I32)],
        scratch_shapes=[pltpu.SemaphoreType.DMA((2,))],
        compiler_params=_cparams(("arbitrary",)),
        name="invert",
    )(dest_flat, src0, dst0)


FF_CHUNKS = 4


def _moe_kernel(src_ref, dst_ref, te_ref, used_ref, h_ref, wgu_ref, bgu_ref, wd_ref, bd_ref, y_ref,
                wgu_bf, wd_bf, act_bf, buf, gsem, ssem):
    i = pl.program_id(0)
    used = used_ref[0]
    slot = i % 2
    tile_rows = ROW_TILE * SLAB
    n_tiles = te_ref.shape[0]

    def gather(tile, sl, rows):
        for r in rows:
            tok = src_ref[tile * ROW_TILE + r]
            pltpu.make_async_copy(h_ref.at[_slab_rows(tok)], buf.at[sl, _slab_rows(r)],
                                  gsem.at[sl]).start(priority=1)

    def scatter(tile, sl, rows):
        for r in rows:
            out = dst_ref[tile * ROW_TILE + r]
            pltpu.make_async_copy(buf.at[2 + sl, _slab_rows(r)], y_ref.at[_slab_rows(out)],
                                  ssem.at[sl]).start()

    def wait_gather(sl):
        pltpu.make_async_copy(h_ref.at[pl.ds(0, tile_rows)], buf.at[sl], gsem.at[sl]).wait()

    def wait_scatter(sl):
        pltpu.make_async_copy(buf.at[2 + sl], y_ref.at[pl.ds(0, tile_rows)], ssem.at[sl]).wait()

    @pl.when(i == 0)
    def _():
        buf[2] = jnp.zeros(buf.shape[1:], F32)
        buf[3] = jnp.zeros(buf.shape[1:], F32)
        tail = y_ref.shape[0] - tile_rows
        pltpu.make_async_copy(buf.at[2], y_ref.at[pl.ds(tail, tile_rows)], ssem.at[0]).start()
        gather(0, 0, range(ROW_TILE))

    prev = te_ref[jnp.maximum(i - 1, 0)]
    fresh = (i == 0) | (te_ref[jnp.minimum(i, n_tiles - 1)] != prev)

    @pl.when(fresh & (i < used))
    def _():
        wgu_bf[...] = wgu_ref[...].astype(BF16)
        wd_bf[...] = wd_ref[...].astype(BF16)

    per = ROW_TILE // FF_CHUNKS
    cw = D_FF // FF_CHUNKS

    @pl.when(i < used)
    def _():
        wait_gather(slot)
        nxt = jnp.minimum(i + 1, n_tiles - 1)
        prv = jnp.where(i >= 1, i - 1, n_tiles)
        half = per // 2
        for c in range(FF_CHUNKS):
            gather(nxt, 1 - slot, range(c * per, (c + 1) * per))
            scatter(prv, 1 - slot, range(c * half, (c + 1) * half))
            x = jnp.concatenate([p.astype(BF16) for p in _load_slabs(buf.at[slot], ROW_TILE)], axis=1)
            gate = jnp.dot(x, wgu_bf[:, c * cw:(c + 1) * cw], preferred_element_type=F32)
            gate = jnp.minimum(gate + bgu_ref[:, c * cw:(c + 1) * cw], SWIGLU_LIMIT)
            up = jnp.dot(x, wgu_bf[:, D_FF + c * cw:D_FF + (c + 1) * cw], preferred_element_type=F32)
            up = jnp.clip(up + bgu_ref[:, D_FF + c * cw:D_FF + (c + 1) * cw], -SWIGLU_LIMIT, SWIGLU_LIMIT)
            act = gate * jax.nn.sigmoid(SWIGLU_ALPHA * gate) * (up + 1.0)
            act_bf[:, c * cw:(c + 1) * cw] = act.astype(BF16)
        ow = D_MODEL // FF_CHUNKS
        base = FF_CHUNKS * half
        wait_scatter(slot)
        for c in range(FF_CHUNKS):
            scatter(prv, 1 - slot, range(base + c * half, base + (c + 1) * half))
            out = jnp.dot(act_bf[...], wd_bf[:, c * ow:(c + 1) * ow], preferred_element_type=F32)
            out = out + bd_ref[:, c * ow:(c + 1) * ow]
            for j in range(ow // LANES):
                buf[2 + slot, pl.ds(c * (ow // LANES) + j, ROW_TILE, stride=SLAB), :] = (
                    out[:, LANES * j:LANES * (j + 1)])

    @pl.when(i == used)
    def _():
        wait_gather(slot)
        wait_scatter(slot)
        scatter(i - 1, 1 - slot, range(ROW_TILE))
        wait_scatter(1 - slot)


def _moe(h2, src, dst, tile_expert, used, w_gu, b_gu, w_down, b_down, n_tok):
    n_tiles = tile_expert.shape[0]
    n_rows = n_tiles * ROW_TILE
    ex = lambda i, src, dst, te, used: (te[jnp.minimum(i, n_tiles - 1)], 0, 0)
    grid_spec = pltpu.PrefetchScalarGridSpec(
        num_scalar_prefetch=4,
        grid=(n_tiles + 1,),
        in_specs=[pl.BlockSpec(memory_space=pl.ANY),
                  pl.BlockSpec((None, D_MODEL, 2 * D_FF), ex),
                  pl.BlockSpec((None, 1, 2 * D_FF), ex),
                  pl.BlockSpec((None, D_FF, D_MODEL), ex),
                  pl.BlockSpec((None, 1, D_MODEL), ex)],
        out_specs=pl.BlockSpec(memory_space=pl.ANY),
        scratch_shapes=[pltpu.VMEM((D_MODEL, 2 * D_FF), BF16), pltpu.VMEM((D_FF, D_MODEL), BF16),
                        pltpu.VMEM((ROW_TILE, D_FF), BF16),
                        pltpu.VMEM((4, ROW_TILE * SLAB, LANES), F32),
                        pltpu.SemaphoreType.DMA((2,)), pltpu.SemaphoreType.DMA((2,))],
    )
    return pl.pallas_call(
        _moe_kernel,
        grid_spec=grid_spec,
        out_shape=jax.ShapeDtypeStruct(((n_tok * TOP_K + n_rows + 2 * ROW_TILE) * SLAB, LANES), F32),
        compiler_params=_cparams(("arbitrary",)),
        name="moe",
    )(src, dst, tile_expert, used, h2, w_gu, b_gu.reshape(N_EXPERTS, 1, 2 * D_FF), w_down,
      b_down.reshape(N_EXPERTS, 1, D_MODEL))


def _merge_kernel(y0_ref, y1_ref, y2_ref, y3_ref, x1_ref, p_ref, g2_ref, fn_ref, o_ref, *, last):
    p = p_ref[...]
    tm = p.shape[0]
    groups = None
    for k, ref in enumerate((y0_ref, y1_ref, y2_ref, y3_ref)):
        part = [p[:, k:k + 1] * g for g in _load_slabs(ref, tm)]
        groups = part if groups is None else [a + b for a, b in zip(groups, part)]
    x2 = x1_ref[...] + g2_ref[...] * jnp.concatenate(groups, axis=1)
    if last:
        x2 = x2 * lax.rsqrt(jnp.mean(x2 * x2, axis=-1, keepdims=True) + EPS) * fn_ref[...]
    o_ref[...] = x2


def _merge(y, x1, probs, mod, per_token, seq, tile0, n_tok, final_norm, last):
    n = x1.shape[0]
    tm = TOK_TILE
    tps = max(seq // tm, 1)
    choice = lambda k: pl.BlockSpec((tm * SLAB, LANES), lambda i: (k * (n_tok // tm) + tile0 + i, 0))
    return pl.pallas_call(
        functools.partial(_merge_kernel, last=last),
        grid=(n // tm,),
        in_specs=[choice(0), choice(1), choice(2), choice(3),
                  pl.BlockSpec((tm, D_MODEL), lambda i: (i, 0)),
                  pl.BlockSpec((tm, TOP_K), lambda i: (i + tile0, 0)),
                  _mod_spec(per_token, tm, tps, 5),
                  pl.BlockSpec((1, D_MODEL), lambda i: (0, 0))],
        out_specs=pl.BlockSpec((tm, D_MODEL), lambda i: (i, 0)),
        out_shape=jax.ShapeDtypeStruct((n, D_MODEL), F32),
        compiler_params=_cparams(("arbitrary",)),
        name="merge",
    )(y, y, y, y, x1, probs, mod, final_norm.reshape(1, D_MODEL))


def kernel(x_prompt, x_sample, state_hgrn, state_pool, c_prompt, c_sample, w_ada, b_ada, norm1,
           norm2, w_in, lower_bounds, hgrn_norm, w_pool, pool_scale, w_out, w_router, b_router,
           w_gu, b_gu, w_down, b_down, final_norm):
    bp, seq_p, _ = x_prompt.shape
    bs, seq_s, _ = x_sample.shape
    np_, ns = bp * seq_p, bs * seq_s
    depth = w_ada.shape[0]
    past_len = 16384

    lbs = jnp.cumsum(jax.nn.softmax(lower_bounds.astype(F32), axis=0), axis=0)
    xp = x_prompt.reshape(np_, D_MODEL)
    xs_ = x_sample.reshape(ns, D_MODEL)
    c_all = jnp.concatenate([c_prompt, c_sample], axis=0)

    sp_l, hp_l, ss_l, hs_l = [], [], [], []
    for l in range(depth):
        w_in_bf = w_in[l].astype(BF16)
        w_out_bf = w_out[l].astype(BF16)
        w_pool_bf = w_pool[l].astype(BF16)
        wr_hi = w_router[l].astype(BF16)
        wr_lo = (w_router[l] - wr_hi.astype(F32)).astype(BF16)
        lb = lbs[l].reshape(1, D_A)
        hn = hgrn_norm[l].reshape(1, D_A)
        psc = pool_scale[l].reshape(1, D_B)

        mod = _ada(c_all, w_ada[l], b_ada[l])
        mod_p = mod[:bp].reshape(bp, 1, 6 * D_MODEL)
        mod_s = jnp.repeat(mod[bp:], seq_s, axis=0)

        up = _inproj(xp, mod_p, False, seq_p, norm1[l], w_in_bf)
        us = _inproj(xs_, mod_s, True, seq_s, norm1[l], w_in_bf)
        us3 = us.reshape(bs, seq_s, D_IN)

        oa_p, s_p = _hgrn(up.reshape(bp, seq_p, D_IN), None, lb, hn)
        oa_p = oa_p.reshape(np_, D_A)
        oa_s, s_s = _hgrn(us3, state_hgrn[l], lb, hn)
        ob_p, h_p = _pool_prompt(up, bp, seq_p, w_pool_bf, psc)
        ob_t, h_t = _pool_sample(us3.transpose(1, 0, 2), state_pool[l].transpose(1, 0, 2), past_len,
                                 w_pool_bf, psc)
        ob_s, h_s = ob_t.transpose(1, 0, 2), h_t.transpose(1, 0, 2)

        n_tok = np_ + ns
        n_tiles = (n_tok * TOP_K) // ROW_TILE + N_EXPERTS
        x1p, *joint = _outproj(xp, oa_p, ob_p, mod_p, False, seq_p, norm2[l], w_out_bf,
                               wr_hi, wr_lo, b_router[l], n_tok, 0)
        x1s, h2, top_idx, probs = _outproj(
            xs_, oa_s.reshape(ns, D_A), ob_s.reshape(ns, D_B), mod_s, True, seq_s, norm2[l],
            w_out_bf, wr_hi, wr_lo, b_router[l], n_tok, np_ // TOK_TILE, joint)

        dest, counts = _rank(top_idx)
        dest_flat = dest.reshape(n_tok * TOP_K)
        tiles_e = (counts.reshape(N_EXPERTS) + ROW_TILE - 1) // ROW_TILE
        ends = jnp.cumsum(tiles_e)
        used = ends[-1:].astype(I32)
        tile_expert = jnp.minimum(
            jnp.sum(jnp.arange(n_tiles, dtype=I32)[:, None] >= ends[None, :], axis=1),
            N_EXPERTS - 1).astype(I32)

        last_tile = (ends - 1).astype(I32)

        xsort = _dispatch(h2, dest_flat, last_tile, tiles_e.astype(I32), used, n_tiles)
        grp = jnp.cumsum(jnp.concatenate(
            [jnp.zeros((1,), I32), (tile_expert[1:] != tile_expert[:-1]).astype(I32)]))
        eids = jnp.arange(N_EXPERTS, dtype=I32)
        later = (eids[None, :] > eids[:, None]) & (tiles_e > 0)[None, :]
        nxt_e = jnp.min(jnp.where(later, eids[None, :], N_EXPERTS), axis=1)
        nxt = jnp.where(nxt_e < N_EXPERTS, nxt_e, -1)[tile_expert].astype(I32)
        ysort = _mlp(xsort, tile_expert, used, grp.astype(I32), nxt, w_gu[l], b_gu[l],
                     w_down[l], b_down[l])

        last = l == depth - 1
        xp = _combine(ysort, dest_flat, x1p, probs, mod_p, False, seq_p, 0, final_norm, last)
        xs_ = _combine(ysort, dest_flat, x1s, probs, mod_s, True, seq_s, np_ // TOK_TILE,
                       final_norm, last)

        sp_l.append(s_p)
        hp_l.append(h_p)
        ss_l.append(s_s)
        hs_l.append(h_s)

    return (xp.reshape(bp, seq_p, D_MODEL), xs_.reshape(bs, seq_s, D_MODEL),
            jnp.stack(sp_l), jnp.stack(hp_l), jnp.stack(ss_l), jnp.stack(hs_l))
```

```python
import functools

import jax
import jax.numpy as jnp
from jax import lax
from jax.experimental import pallas as pl
from jax.experimental.pallas import tpu as pltpu

F32, BF16, I32 = jnp.float32, jnp.bfloat16, jnp.int32

D_MODEL = 1024
D_A = 512
H_A = 4
DK = 128
DV = 128
D_B = 512
POOL_WINDOWS = (2, 4, 8, 16)
POOL_GROUP = 128
POOL_HIST = 15
D_IN = 4 * D_A + D_B
N_EXPERTS = 32
TOP_K = 4
D_FF = 1024
SWIGLU_LIMIT = 7.0
SWIGLU_ALPHA = 1.702
EPS = 1e-6

SUB = 16
ROW_TILE = 256
TOK_TILE = 256
VMEM_LIMIT = 56 * 1024 * 1024


def _cparams(sem):
    return pltpu.CompilerParams(dimension_semantics=sem, vmem_limit_bytes=VMEM_LIMIT)


def _silu(x):
    return x * jax.nn.sigmoid(x)


def _mod_spec(per_token, tm, tiles_per_seq, col, last_tile=None):
    tile = (lambda i: i) if last_tile is None else (lambda i: jnp.minimum(i, last_tile))
    if per_token:
        return pl.BlockSpec((tm, D_MODEL), lambda i: (tile(i), col))
    return pl.BlockSpec((None, 1, D_MODEL), lambda i: (tile(i) // tiles_per_seq, 0, col))


def _ada_kernel(c_ref, w_ref, b_ref, o_ref):
    c = c_ref[...]
    o_ref[...] = jnp.dot(_silu(c).astype(BF16), w_ref[...].astype(BF16),
                         preferred_element_type=F32) + b_ref[...]


def _ada(c_all, w_ada, b_ada):
    rows = c_all.shape[0]
    n = w_ada.shape[1]
    return pl.pallas_call(
        _ada_kernel,
        grid=(n // D_MODEL,),
        in_specs=[pl.BlockSpec((rows, D_MODEL), lambda j: (0, 0)),
                  pl.BlockSpec((D_MODEL, D_MODEL), lambda j: (0, j)),
                  pl.BlockSpec((1, D_MODEL), lambda j: (0, j))],
        out_specs=pl.BlockSpec((rows, D_MODEL), lambda j: (0, j)),
        out_shape=jax.ShapeDtypeStruct((rows, n), F32),
        compiler_params=_cparams(("arbitrary",)),
        name="ada",
    )(c_all, w_ada, b_ada.reshape(1, n))


def _inproj_kernel(x_ref, sh_ref, sc_ref, n1_ref, w_ref, u_ref):
    x = x_ref[...]
    h = x * lax.rsqrt(jnp.mean(x * x, axis=-1, keepdims=True) + EPS) * n1_ref[...]
    h = h * (1.0 + sc_ref[...]) + sh_ref[...]
    u_ref[...] = jnp.dot(h.astype(BF16), w_ref[...], preferred_element_type=F32)


def _inproj(x, mod, per_token, seq, norm1, w_in_bf):
    n = x.shape[0]
    tm = min(512, n)
    tps = max(seq // tm, 1)
    return pl.pallas_call(
        _inproj_kernel,
        grid=(n // tm,),
        in_specs=[pl.BlockSpec((tm, D_MODEL), lambda i: (i, 0)),
                  _mod_spec(per_token, tm, tps, 0),
                  _mod_spec(per_token, tm, tps, 1),
                  pl.BlockSpec((1, D_MODEL), lambda i: (0, 0)),
                  pl.BlockSpec((D_MODEL, D_IN), lambda i: (0, 0))],
        out_specs=pl.BlockSpec((tm, D_IN), lambda i: (i, 0)),
        out_shape=jax.ShapeDtypeStruct((n, D_IN), F32),
        compiler_params=_cparams(("arbitrary",)),
        name="inproj",
    )(x, mod, mod, norm1.reshape(1, D_MODEL), w_in_bf)


def _block_select():
    r = lax.broadcasted_iota(I32, (SUB, SUB * DK), 0)
    c = lax.broadcasted_iota(I32, (SUB, SUB * DK), 1) >> (DK.bit_length() - 1)
    return (r == c).astype(BF16)


def _hgrn_block(q, fp, iv, g, lb, hn, st_ref, sel_ref, valid):
    f = lb + (1.0 - lb) * jax.nn.sigmoid(fp)
    logf = jnp.log(f)
    kk = 1.0 - f
    if valid < SUB:
        live = lax.broadcasted_iota(I32, (SUB, D_A), 0) < valid
        logf = jnp.where(live, logf, 0.0)
        kk = jnp.where(live, kk, 0.0)
    qa = _silu(q)

    r = lax.broadcasted_iota(I32, (SUB, SUB), 0)
    c = lax.broadcasted_iota(I32, (SUB, SUB), 1)
    tri = (r >= c).astype(BF16)
    hi = logf.astype(BF16)
    lo = (logf - hi.astype(F32)).astype(BF16)
    b = (jnp.dot(tri, hi, preferred_element_type=F32)
         + jnp.dot(tri, lo, preferred_element_type=F32))
    bend = b[SUB - 1:SUB, :]
    qt = (qa * jnp.exp(b)).astype(BF16)
    kh = (kk * jnp.exp(bend - b)).astype(BF16)
    dec = jnp.exp(bend)
    ivb = iv.astype(BF16)

    rows = []
    for h in range(H_A):
        sl = slice(DK * h, DK * (h + 1))
        bh, qh, kkh = b[:, sl], qa[:, sl], kk[:, sl]
        ps = []
        for s in range(SUB):
            e = jnp.exp(jnp.minimum(bh - bh[s:s + 1, :], 0.0))
            ps.append((e * (qh * kkh[s:s + 1, :])).astype(BF16))
        rows.append(jnp.concatenate(ps, axis=1))
    pcat = jnp.concatenate(rows, axis=0)
    a = lax.dot_general(pcat, sel_ref[...], (((1,), (1,)), ((), ())),
                        preferred_element_type=F32)
    tr = lax.broadcasted_iota(I32, (H_A * SUB, SUB), 0) & (SUB - 1)
    tc = lax.broadcasted_iota(I32, (H_A * SUB, SUB), 1)
    a = jnp.where(tr >= tc, a, 0.0).astype(BF16)

    outs = []
    for h in range(H_A):
        sl = slice(DK * h, DK * (h + 1))
        st = st_ref[h]
        o = lax.dot_general(qt[:, sl], st.astype(BF16), (((1,), (1,)), ((), ())),
                            preferred_element_type=F32)
        o = o + jnp.dot(a[SUB * h:SUB * (h + 1), :], ivb[:, sl], preferred_element_type=F32)
        ut = lax.dot_general(ivb[:, sl], kh[:, sl], (((0,), (0,)), ((), ())),
                             preferred_element_type=F32)
        st_ref[h] = st * dec[:, sl] + ut
        o = o * lax.rsqrt(jnp.mean(o * o, axis=-1, keepdims=True) + EPS) * hn[:, sl]
        outs.append(o)
    return jnp.concatenate(outs, axis=1) * _silu(g)


def _hgrn_prompt_kernel(q_ref, f_ref, i_ref, g_ref, lb_ref, hn_ref, o_ref, s_ref, st_ref, sel_ref,
                        *, tt):
    t = pl.program_id(1)

    @pl.when(t == 0)
    def _():
        st_ref[...] = jnp.zeros_like(st_ref)
        sel_ref[...] = _block_select()

    lb = lb_ref[...]
    hn = hn_ref[...]

    def body(j, carry):
        rs = pl.ds(pl.multiple_of(j * SUB, SUB), SUB)
        o_ref[rs, :] = _hgrn_block(q_ref[rs, :], f_ref[rs, :], i_ref[rs, :], g_ref[rs, :],
                                   lb, hn, st_ref, sel_ref, SUB)
        return carry

    lax.fori_loop(0, tt // SUB, body, 0)

    @pl.when(t == pl.num_programs(1) - 1)
    def _():
        for h in range(H_A):
            s_ref[h] = st_ref[h].T


def _hgrn_prompt(u, batch, seq, lb, hn):
    tt = 256
    nt = seq // tt

    def col(cb):
        return pl.BlockSpec((tt, D_A), lambda b, t: (b * nt + t, cb))

    return pl.pallas_call(
        functools.partial(_hgrn_prompt_kernel, tt=tt),
        grid=(batch, nt),
        in_specs=[col(0), col(1), col(2), col(3),
                  pl.BlockSpec((1, D_A), lambda b, t: (0, 0)),
                  pl.BlockSpec((1, D_A), lambda b, t: (0, 0))],
        out_specs=[pl.BlockSpec((tt, D_A), lambda b, t: (b * nt + t, 0)),
                   pl.BlockSpec((None, H_A, DK, DV), lambda b, t: (b, 0, 0, 0))],
        out_shape=[jax.ShapeDtypeStruct((batch * seq, D_A), F32),
                   jax.ShapeDtypeStruct((batch, H_A, DK, DV), F32)],
        scratch_shapes=[pltpu.VMEM((H_A, DV, DK), F32), pltpu.VMEM((SUB, SUB * DK), BF16)],
        compiler_params=_cparams(("arbitrary", "arbitrary")),
        name="hgrn_prompt",
    )(u, u, u, u, lb, hn)


def _hgrn_sample_kernel(q_ref, f_ref, i_ref, g_ref, lb_ref, hn_ref, s0_ref, o_ref, s_ref,
                        st_ref, sel_ref, pad_ref, *, seq):
    for h in range(H_A):
        st_ref[h] = s0_ref[h].T
    sel_ref[...] = _block_select()
    pad_ref[...] = jnp.zeros_like(pad_ref)
    for n, ref in enumerate((q_ref, f_ref, i_ref, g_ref)):
        pad_ref[n, 0:seq, :] = ref[...]
    o = _hgrn_block(pad_ref[0], pad_ref[1], pad_ref[2], pad_ref[3], lb_ref[...], hn_ref[...],
                    st_ref, sel_ref, seq)
    o_ref[...] = o[0:seq, :]
    for h in range(H_A):
        s_ref[h] = st_ref[h].T


def _hgrn_sample(u3, s0, lb, hn):
    batch, seq, _ = u3.shape

    def col(cb):
        return pl.BlockSpec((None, seq, D_A), lambda b: (b, 0, cb))

    return pl.pallas_call(
        functools.partial(_hgrn_sample_kernel, seq=seq),
        grid=(batch,),
        in_specs=[col(0), col(1), col(2), col(3),
                  pl.BlockSpec((1, D_A), lambda b: (0, 0)),
                  pl.BlockSpec((1, D_A), lambda b: (0, 0)),
                  pl.BlockSpec((None, H_A, DK, DV), lambda b: (b, 0, 0, 0))],
        out_specs=[pl.BlockSpec((None, seq, D_A), lambda b: (b, 0, 0)),
                   pl.BlockSpec((None, H_A, DK, DV), lambda b: (b, 0, 0, 0))],
        out_shape=[jax.ShapeDtypeStruct((batch, seq, D_A), F32),
                   jax.ShapeDtypeStruct((batch, H_A, DK, DV), F32)],
        scratch_shapes=[pltpu.VMEM((H_A, DV, DK), F32), pltpu.VMEM((SUB, SUB * DK), BF16),
                        pltpu.VMEM((4, SUB, D_A), F32)],
        compiler_params=_cparams(("arbitrary",)),
        name="hgrn_sample",
    )(u3, u3, u3, u3, lb, hn, s0)


HGRN_SEQS = 8
HALF = SUB // 2


def _hgrn_step(q, fp, iv, g, lb, hn, st_ref, sel_ref, cm_ref, b_ref, k_ref, valid):
    nrow = HGRN_SEQS * SUB
    f = lb + (1.0 - lb) * jax.nn.sigmoid(fp)
    lf = jnp.log2(f)
    kk = 1.0 - f
    if valid < SUB:
        live = (lax.broadcasted_iota(I32, (nrow, D_A), 0) & (SUB - 1)) < valid
        lf = jnp.where(live, lf, 0.0)
        kk = jnp.where(live, kk, 0.0)
    qa = _silu(q)

    hi = lf.astype(BF16)
    lo = (lf - hi.astype(F32)).astype(BF16)
    cum, tot = cm_ref[0], cm_ref[1]
    b = jnp.dot(cum, hi, preferred_element_type=F32) + jnp.dot(cum, lo, preferred_element_type=F32)
    be = jnp.dot(tot, hi, preferred_element_type=F32) + jnp.dot(tot, lo, preferred_element_type=F32)
    qt = (qa * jnp.exp2(b)).astype(BF16)
    kh = (kk * jnp.exp2(be - b)).astype(BF16)
    dec = jnp.exp2(be)
    ivb = iv.astype(BF16)
    for h in range(H_A):
        b_ref[h] = b[:, DK * h:DK * (h + 1)]
        k_ref[h] = kk[:, DK * h:DK * (h + 1)]

    zero_half = jnp.zeros((HALF, D_A), F32)
    tr = lax.broadcasted_iota(I32, (H_A * SUB, SUB), 0) & (SUB - 1)
    tc = lax.broadcasted_iota(I32, (H_A * SUB, SUB), 1)
    sg = _silu(g)
    inter = []
    for n in range(HGRN_SEQS):
        rows = slice(n * SUB, (n + 1) * SUB)
        for h in range(H_A):
            sl = slice(DK * h, DK * (h + 1))
            st = st_ref[n, h]
            inter.append(lax.dot_general(qt[rows, sl], st.astype(BF16), (((1,), (1,)), ((), ())),
                                         preferred_element_type=F32))
            ut = lax.dot_general(ivb[rows, sl], kh[rows, sl], (((0,), (0,)), ((), ())),
                                 preferred_element_type=F32)
            dtile = dec[n * SUB:n * SUB + HALF, sl]
            st_ref[n, h] = (st.reshape(DV // HALF, HALF, DK) * dtile[None]).reshape(DV, DK) + ut

    scores = []
    for n in range(HGRN_SEQS):
        per_s = []
        for s in range(SUB):
            row = n * SUB + s
            b_s = jnp.concatenate([b_ref[h, pl.ds(row, HALF, stride=0), :] for h in range(H_A)], axis=1)
            k_s = jnp.concatenate([k_ref[h, pl.ds(row, HALF, stride=0), :] for h in range(H_A)], axis=1)
            halves = []
            for hf in range(2):
                if hf == 0 and s >= HALF:
                    halves.append(zero_half)
                    continue
                r0 = n * SUB + hf * HALF
                e = jnp.exp2(jnp.minimum(b[r0:r0 + HALF] - b_s, 0.0))
                halves.append(e * (qa[r0:r0 + HALF] * k_s))
            per_s.append(jnp.concatenate(halves, axis=0).astype(BF16))
        pcat = jnp.concatenate(
            [jnp.concatenate([p[:, DK * h:DK * (h + 1)] for p in per_s], axis=1) for h in range(H_A)],
            axis=0)
        a = lax.dot_general(pcat, sel_ref[...], (((1,), (1,)), ((), ())),
                            preferred_element_type=F32)
        scores.append(jnp.where(tr >= tc, a, 0.0).astype(BF16))

    outs = []
    for n in range(HGRN_SEQS):
        rows = slice(n * SUB, (n + 1) * SUB)
        heads = []
        for h in range(H_A):
            sl = slice(DK * h, DK * (h + 1))
            o = inter[n * H_A + h] + jnp.dot(scores[n][h * SUB:(h + 1) * SUB, :], ivb[rows, sl],
                                             preferred_element_type=F32)
            o = o * lax.rsqrt(jnp.mean(o * o, axis=-1, keepdims=True) + EPS) * hn[:, sl]
            heads.append(o)
        outs.append(jnp.concatenate(heads, axis=1) * sg[rows])
    return jnp.concatenate(outs, axis=0)


def _hgrn_kernel(*refs, rows, has_s0):
    if has_s0:
        (q_ref, f_ref, i_ref, g_ref, lb_ref, hn_ref, s0_ref, o_ref, s_ref,
         st_ref, sel_ref, cm_ref, b_ref, k_ref, pad_ref) = refs
    else:
        (q_ref, f_ref, i_ref, g_ref, lb_ref, hn_ref, o_ref, s_ref,
         st_ref, sel_ref, cm_ref, b_ref, k_ref) = refs
    t = pl.program_id(1)
    nrow = HGRN_SEQS * SUB

    @pl.when((t == 0) & (pl.program_id(0) == 0))
    def _():
        sel_ref[...] = _block_select()
        r = lax.broadcasted_iota(I32, (nrow, nrow), 0)
        c = lax.broadcasted_iota(I32, (nrow, nrow), 1)
        shift = SUB.bit_length() - 1
        same = (r >> shift) == (c >> shift)
        cm_ref[0] = (same & (r >= c)).astype(BF16)
        cm_ref[1] = same.astype(BF16)

    lb = lb_ref[...]
    hn = hn_ref[...]
    if has_s0:
        for n in range(HGRN_SEQS):
            for h in range(H_A):
                st_ref[n, h] = s0_ref[n, h].T
        pad_ref[...] = jnp.zeros_like(pad_ref)
        for j, ref in enumerate((q_ref, f_ref, i_ref, g_ref)):
            pad_ref[j, :, 0:rows, :] = ref[...]
        o = _hgrn_step(*(pad_ref[j].reshape(nrow, D_A) for j in range(4)), lb, hn,
                       st_ref, sel_ref, cm_ref, b_ref, k_ref, rows)
        o_ref[...] = o.reshape(HGRN_SEQS, SUB, D_A)[:, 0:rows, :]
    else:
        @pl.when(t == 0)
        def _():
            st_ref[...] = jnp.zeros_like(st_ref)

        def body(j, carry):
            rs = pl.ds(pl.multiple_of(j * SUB, SUB), SUB)
            o = _hgrn_step(*(ref[:, rs, :].reshape(nrow, D_A) for ref in (q_ref, f_ref, i_ref, g_ref)),
                           lb, hn, st_ref, sel_ref, cm_ref, b_ref, k_ref, SUB)
            o_ref[:, rs, :] = o.reshape(HGRN_SEQS, SUB, D_A)
            return carry

        lax.fori_loop(0, rows // SUB, body, 0)

    def emit_state():
        for n in range(HGRN_SEQS):
            for h in range(H_A):
                s_ref[n, h] = st_ref[n, h].T

    if has_s0:
        emit_state()
    else:
        pl.when(t == pl.num_programs(1) - 1)(emit_state)


def _hgrn(u3, s0, lb, hn):
    batch, seq, _ = u3.shape
    has_s0 = s0 is not None
    rows = seq if has_s0 else 128
    grid = (batch // HGRN_SEQS, seq // rows)
    blk = lambda cb: pl.BlockSpec((HGRN_SEQS, rows, D_A), lambda b, t: (b, t, cb))
    sblk = pl.BlockSpec((HGRN_SEQS, H_A, DK, DV), lambda b, t: (b, 0, 0, 0))
    vec = pl.BlockSpec((1, D_A), lambda b, t: (0, 0))
    nrow = HGRN_SEQS * SUB
    scratch = [pltpu.VMEM((HGRN_SEQS, H_A, DV, DK), F32), pltpu.VMEM((SUB, SUB * DK), BF16),
               pltpu.VMEM((2, nrow, nrow), BF16), pltpu.VMEM((H_A, nrow, DK), F32),
               pltpu.VMEM((H_A, nrow, DK), F32)]
    if has_s0:
        scratch.append(pltpu.VMEM((4, HGRN_SEQS, SUB, D_A), F32))
    args = (u3, u3, u3, u3, lb, hn) + ((s0,) if has_s0 else ())
    return pl.pallas_call(
        functools.partial(_hgrn_kernel, rows=rows, has_s0=has_s0),
        grid=grid,
        in_specs=[blk(0), blk(1), blk(2), blk(3), vec, vec] + ([sblk] if has_s0 else []),
        out_specs=[blk(0), sblk],
        out_shape=[jax.ShapeDtypeStruct((batch, seq, D_A), F32),
                   jax.ShapeDtypeStruct((batch, H_A, DK, DV), F32)],
        scratch_shapes=scratch,
        compiler_params=_cparams(("arbitrary", "arbitrary")),
        name="hgrn_state" if has_s0 else "hgrn_scan",
    )(*args)


def _pool_groups(full_ref, base, tm, cnt_fn, wp_ref, ps_ref, step=1):
    outs = []
    for gi, w in enumerate(POOL_WINDOWS):
        ch = slice(POOL_GROUP * gi, POOL_GROUP * (gi + 1))
        v = full_ref[base:base + tm, ch]
        s = v
        for j in range(1, w):
            s = s + full_ref[base - j * step:base - j * step + tm, ch]
        d = s / cnt_fn(w) - v
        outs.append(jnp.dot(d.astype(BF16), wp_ref[gi], preferred_element_type=F32))
    return jnp.concatenate(outs, axis=1) * ps_ref[...]


def _pool_prompt_kernel(v_ref, prev_ref, wp_ref, ps_ref, o_ref, hist_ref, full_ref, *, tm):
    t = pl.program_id(1)
    prev = prev_ref[...]
    full_ref[0:16, :] = jnp.where(t == 0, 0.0, prev)
    full_ref[16:16 + tm, :] = v_ref[...]
    pos = (lax.broadcasted_iota(I32, (tm, 1), 0) + t * tm + 1).astype(F32)
    o_ref[...] = _pool_groups(full_ref, 16, tm, lambda w: jnp.minimum(pos, float(w)),
                              wp_ref, ps_ref)

    @pl.when(t == pl.num_programs(1) - 1)
    def _():
        hist_ref[...] = full_ref[16 + tm - POOL_HIST:16 + tm, :]


def _pool_prompt(u, batch, seq, w_pool_bf, pool_scale):
    tm = 256
    nt = seq // tm
    vcol = (4 * D_A) // D_B
    return pl.pallas_call(
        functools.partial(_pool_prompt_kernel, tm=tm),
        grid=(batch, nt),
        in_specs=[pl.BlockSpec((tm, D_B), lambda b, t: (b * nt + t, vcol)),
                  pl.BlockSpec((16, D_B),
                               lambda b, t: (jnp.maximum((b * nt + t) * (tm // 16) - 1, 0), vcol)),
                  pl.BlockSpec((len(POOL_WINDOWS), POOL_GROUP, POOL_GROUP), lambda b, t: (0, 0, 0)),
                  pl.BlockSpec((1, D_B), lambda b, t: (0, 0))],
        out_specs=[pl.BlockSpec((tm, D_B), lambda b, t: (b * nt + t, 0)),
                   pl.BlockSpec((None, POOL_HIST, D_B), lambda b, t: (b, 0, 0))],
        out_shape=[jax.ShapeDtypeStruct((batch * seq, D_B), F32),
                   jax.ShapeDtypeStruct((batch, POOL_HIST, D_B), F32)],
        scratch_shapes=[pltpu.VMEM((16 + tm, D_B), F32)],
        compiler_params=_cparams(("arbitrary", "arbitrary")),
        name="pool_prompt",
    )(u, u, w_pool_bf, pool_scale)


def _pool_sample_kernel(v_ref, hist_ref, wp_ref, ps_ref, o_ref, nh_ref, full_ref, *,
                        seq, batch, start):
    full_ref[0:batch, :] = jnp.zeros((batch, D_B), F32)
    full_ref[batch:16 * batch, :] = hist_ref[...].reshape(POOL_HIST * batch, D_B)
    full_ref[16 * batch:(16 + seq) * batch, :] = v_ref[...].reshape(seq * batch, D_B)

    def cnt(w):
        rows = [jnp.full((batch, 1), float(min(start + t + 1, w)), F32) for t in range(seq)]
        return jnp.concatenate(rows, axis=0)

    o = _pool_groups(full_ref, 16 * batch, seq * batch, cnt, wp_ref, ps_ref, step=batch)
    o_ref[...] = o.reshape(seq, batch, D_B)
    nh_ref[...] = full_ref[(16 + seq - POOL_HIST) * batch:(16 + seq) * batch, :].reshape(
        POOL_HIST, batch, D_B)


def _pool_sample(u_t, hist_t, start, w_pool_bf, pool_scale):
    seq, batch, _ = u_t.shape
    vcol = (4 * D_A) // D_B
    return pl.pallas_call(
        functools.partial(_pool_sample_kernel, seq=seq, batch=batch, start=start),
        grid=(1,),
        in_specs=[pl.BlockSpec((seq, batch, D_B), lambda i: (0, 0, vcol)),
                  pl.BlockSpec((POOL_HIST, batch, D_B), lambda i: (0, 0, 0)),
                  pl.BlockSpec((len(POOL_WINDOWS), POOL_GROUP, POOL_GROUP), lambda i: (0, 0, 0)),
                  pl.BlockSpec((1, D_B), lambda i: (0, 0))],
        out_specs=[pl.BlockSpec((seq, batch, D_B), lambda i: (0, 0, 0)),
                   pl.BlockSpec((POOL_HIST, batch, D_B), lambda i: (0, 0, 0))],
        out_shape=[jax.ShapeDtypeStruct((seq, batch, D_B), F32),
                   jax.ShapeDtypeStruct((POOL_HIST, batch, D_B), F32)],
        scratch_shapes=[pltpu.VMEM(((16 + seq) * batch, D_B), F32)],
        compiler_params=_cparams(("arbitrary",)),
        name="pool_sample",
    )(u_t, hist_t, w_pool_bf, pool_scale)


LANES = 128
SLAB = D_MODEL // LANES


def _store_slabs(ref, value):
    n = value.shape[0]
    for j in range(SLAB):
        ref[pl.ds(j, n, stride=SLAB), :] = value[:, LANES * j:LANES * (j + 1)]


def _load_slabs(ref, n):
    return [ref[pl.ds(j, n, stride=SLAB), :] for j in range(SLAB)]


def _slab_rows(row):
    return pl.ds(pl.multiple_of(row * SLAB, SLAB), SLAB)


def _split_dot(a, w_hi, w_lo):
    a_hi = a.astype(BF16)
    a_lo = (a - a_hi.astype(F32)).astype(BF16)
    return (jnp.dot(a_hi, w_hi, preferred_element_type=F32)
            + jnp.dot(a_hi, w_lo, preferred_element_type=F32)
            + jnp.dot(a_lo, w_hi, preferred_element_type=F32))


def _outproj_kernel(x_ref, oa_ref, ob_ref, g1_ref, sh2_ref, sc2_ref, n2_ref, woa_ref, wob_ref,
                    wrh_ref, wrl_ref, br_ref, *rest):
    x1_ref, h2_ref, idx_ref, prob_ref, lg_ref = rest[-5:]
    i = pl.program_id(0)
    slot = i % 2

    @pl.when(i == 0)
    def _():
        lg_ref[...] = jnp.zeros_like(lg_ref)

    logits = lg_ref[1 - slot]

    mix = (jnp.dot(oa_ref[...].astype(BF16), woa_ref[...], preferred_element_type=F32)
           + jnp.dot(ob_ref[...].astype(BF16), wob_ref[...], preferred_element_type=F32))
    x1 = x_ref[...] + g1_ref[...] * mix
    x1_ref[...] = x1
    h2 = x1 * lax.rsqrt(jnp.mean(x1 * x1, axis=-1, keepdims=True) + EPS) * n2_ref[...]
    h2 = h2 * (1.0 + sc2_ref[...]) + sh2_ref[...]
    _store_slabs(h2_ref, h2)
    lg_ref[slot] = _split_dot(h2, wrh_ref[...], wrl_ref[...]) + br_ref[...]

    tm = logits.shape[0]
    lane = lax.broadcasted_iota(I32, (tm, N_EXPERTS), 1)
    kcol = lax.broadcasted_iota(I32, (tm, TOP_K), 1)
    vals = jnp.zeros((tm, TOP_K), F32)
    idxs = jnp.zeros((tm, TOP_K), I32)
    for k in range(TOP_K):
        m = jnp.max(logits, axis=-1, keepdims=True)
        am = jnp.min(jnp.where(logits == m, lane, N_EXPERTS), axis=-1, keepdims=True)
        vals = jnp.where(kcol == k, m, vals)
        idxs = jnp.where(kcol == k, am, idxs)
        logits = jnp.where(lane == am, -jnp.inf, logits)
    e = jnp.exp(vals - vals[:, 0:1])
    prob_ref[...] = e / jnp.sum(e, axis=-1, keepdims=True)
    idx_ref[...] = idxs


def _outproj(x, o_a, o_b, mod, per_token, seq, norm2, w_out_bf, wr_hi, wr_lo, b_router,
             n_total, tile0, joint=None):
    n = x.shape[0]
    tm = min(256, n)
    tps = max(seq // tm, 1)
    nt = n // tm
    cur = lambda i: jnp.minimum(i, nt - 1)
    prv = lambda i: jnp.maximum(i - 1, 0)
    row = lambda width: pl.BlockSpec((tm, width), lambda i: (cur(i), 0))
    jrow = lambda width: pl.BlockSpec((tm, width), lambda i: (prv(i) + tile0, 0))
    const = lambda shape: pl.BlockSpec(shape, lambda i: (0,) * len(shape))
    joint = () if joint is None else tuple(joint)
    n_in = 12
    return pl.pallas_call(
        _outproj_kernel,
        grid=(nt + 1,),
        in_specs=[row(D_MODEL), row(D_A), row(D_B),
                  _mod_spec(per_token, tm, tps, 2, nt - 1),
                  _mod_spec(per_token, tm, tps, 3, nt - 1),
                  _mod_spec(per_token, tm, tps, 4, nt - 1),
                  const((1, D_MODEL)),
                  pl.BlockSpec((D_A, D_MODEL), lambda i: (0, 0)),
                  pl.BlockSpec((D_B, D_MODEL), lambda i: (1, 0)),
                  const((D_MODEL, N_EXPERTS)), const((D_MODEL, N_EXPERTS)), const((1, N_EXPERTS))]
                 + [pl.BlockSpec(memory_space=pl.ANY)] * len(joint),
        out_specs=[row(D_MODEL), pl.BlockSpec((tm * SLAB, LANES), lambda i: (cur(i) + tile0, 0)),
                   jrow(TOP_K), jrow(TOP_K)],
        out_shape=[jax.ShapeDtypeStruct((n, D_MODEL), F32),
                   jax.ShapeDtypeStruct((n_total * SLAB, LANES), F32),
                   jax.ShapeDtypeStruct((n_total, TOP_K), I32),
                   jax.ShapeDtypeStruct((n_total, TOP_K), F32)],
        input_output_aliases={n_in + j: 1 + j for j in range(len(joint))},
        scratch_shapes=[pltpu.VMEM((2, tm, N_EXPERTS), F32)],
        compiler_params=_cparams(("arbitrary",)),
        name="outproj",
    )(x, o_a, o_b, mod, mod, mod, norm2.reshape(1, D_MODEL), w_out_bf, w_out_bf,
      wr_hi, wr_lo, b_router.reshape(1, N_EXPERTS), *joint)


def _rank_kernel(idx_ref, dest_ref, cnt_ref, carry_ref, *, te):
    phase = pl.program_id(0)
    i = pl.program_id(1)
    idx = idx_ref[...]
    lane = lax.broadcasted_iota(I32, (te, N_EXPERTS), 1)
    onehots = [(idx[:, k:k + 1] == lane) for k in range(TOP_K)]
    member = jnp.zeros((te, N_EXPERTS), F32)
    for oh in onehots:
        member = member + oh.astype(F32)
    colsum = jnp.sum(member, axis=0, keepdims=True)

    @pl.when((phase == 0) & (i == 0))
    def _():
        carry_ref[...] = jnp.zeros_like(carry_ref)

    @pl.when(phase == 0)
    def _():
        carry_ref[...] += colsum

    @pl.when((phase == 1) & (i == 0))
    def _():
        cnt = carry_ref[...]
        cnt_ref[...] = cnt.astype(I32)
        tiles = jnp.floor((cnt + (ROW_TILE - 1)) * (1.0 / ROW_TILE))
        r = lax.broadcasted_iota(I32, (N_EXPERTS, N_EXPERTS), 0)
        c = lax.broadcasted_iota(I32, (N_EXPERTS, N_EXPERTS), 1)
        before = (r < c).astype(BF16)
        carry_ref[...] = ROW_TILE * jnp.dot(tiles.astype(BF16), before, preferred_element_type=F32)

    @pl.when(phase == 1)
    def _():
        r = lax.broadcasted_iota(I32, (te, te), 0)
        c = lax.broadcasted_iota(I32, (te, te), 1)
        earlier = (r > c).astype(BF16)
        base = carry_ref[...] + jnp.dot(earlier, member.astype(BF16), preferred_element_type=F32)
        kcol = lax.broadcasted_iota(I32, (te, TOP_K), 1)
        dest = jnp.zeros((te, TOP_K), F32)
        for k, oh in enumerate(onehots):
            dk = jnp.sum(jnp.where(oh, base, 0.0), axis=-1, keepdims=True)
            dest = jnp.where(kcol == k, dk, dest)
        dest_ref[...] = dest.astype(I32)
        carry_ref[...] += colsum


def _rank(top_idx):
    n = top_idx.shape[0]
    te = 512
    return pl.pallas_call(
        functools.partial(_rank_kernel, te=te),
        grid=(2, n // te),
        in_specs=[pl.BlockSpec((te, TOP_K), lambda p, i: (i, 0))],
        out_specs=[pl.BlockSpec((te, TOP_K), lambda p, i: (p * i, 0)),
                   pl.BlockSpec((1, N_EXPERTS), lambda p, i: (0, 0))],
        out_shape=[jax.ShapeDtypeStruct((n, TOP_K), I32),
                   jax.ShapeDtypeStruct((1, N_EXPERTS), I32)],
        scratch_shapes=[pltpu.VMEM((1, N_EXPERTS), F32)],
        compiler_params=_cparams(("arbitrary", "arbitrary")),
        name="rank",
    )(top_idx)


def _dispatch_kernel(last_ref, ntile_ref, used_ref, dest_ref, h_ref, xs_ref, zero_ref, sem, zsem,
                     *, n_tiles):
    i = pl.program_id(0)

    @pl.when(i == 0)
    def _():
        zero_ref[...] = jnp.zeros_like(zero_ref)

        def clear(tile):
            rows = pl.ds(pl.multiple_of(tile * (ROW_TILE * SLAB), ROW_TILE * SLAB), ROW_TILE * SLAB)
            return pltpu.make_async_copy(zero_ref, xs_ref.at[rows], zsem)

        def over_tail(fn):
            def body(t, carry):
                fn(clear(t))
                return carry
            lax.fori_loop(used_ref[0], n_tiles, body, 0)

        for fn in (lambda c: c.start(), lambda c: c.wait()):
            for e in range(N_EXPERTS):
                @pl.when(ntile_ref[e] > 0)
                def _():
                    fn(clear(last_ref[e]))
            over_tail(fn)

    def body(r, carry):
        for k in range(TOP_K):
            d = dest_ref[r * TOP_K + k]
            pltpu.make_async_copy(h_ref.at[_slab_rows(r)], xs_ref.at[_slab_rows(d)], sem).start(
                priority=k % 2)
        return carry

    lax.fori_loop(0, TOK_TILE, body, 0, unroll=2)
    for k in range(TOP_K):
        pltpu.make_async_copy(h_ref, xs_ref.at[pl.ds(0, TOK_TILE * SLAB)], sem).wait()


def _dispatch(h2, dest_flat, last_tile, ntiles, used, n_tiles):
    n = h2.shape[0] // SLAB
    grid_spec = pltpu.PrefetchScalarGridSpec(
        num_scalar_prefetch=3,
        grid=(n // TOK_TILE,),
        in_specs=[pl.BlockSpec((TOK_TILE * TOP_K,), lambda i, *_: (i,), memory_space=pltpu.SMEM),
                  pl.BlockSpec((TOK_TILE * SLAB, LANES), lambda i, *_: (i, 0))],
        out_specs=pl.BlockSpec(memory_space=pl.ANY),
        scratch_shapes=[pltpu.VMEM((ROW_TILE * SLAB, LANES), F32),
                        pltpu.SemaphoreType.DMA(()), pltpu.SemaphoreType.DMA(())],
    )
    return pl.pallas_call(
        functools.partial(_dispatch_kernel, n_tiles=n_tiles),
        grid_spec=grid_spec,
        out_shape=jax.ShapeDtypeStruct((n_tiles * ROW_TILE * SLAB, LANES), F32),
        compiler_params=_cparams(("arbitrary",)),
        name="dispatch",
    )(last_tile, ntiles, used, dest_flat, h2)


def _experts_kernel(te_ref, used_ref, xs_ref, wgu_ref, bgu_ref, wd_ref, bd_ref, ys_ref,
                    wgu_bf, wd_bf, act_bf):
    i = pl.program_id(0)
    prev = te_ref[jnp.maximum(i - 1, 0)]
    fresh = (i == 0) | (te_ref[i] != prev)

    @pl.when(fresh)
    def _():
        wgu_bf[...] = wgu_ref[...].astype(BF16)
        wd_bf[...] = wd_ref[...].astype(BF16)

    @pl.when(i < used_ref[0])
    def _():
        x = jnp.concatenate([p.astype(BF16) for p in _load_slabs(xs_ref, ROW_TILE)], axis=1)
        cw = D_FF // FF_CHUNKS
        for c in range(FF_CHUNKS):
            gate = jnp.dot(x, wgu_bf[:, c * cw:(c + 1) * cw], preferred_element_type=F32)
            gate = jnp.minimum(gate + bgu_ref[:, c * cw:(c + 1) * cw], SWIGLU_LIMIT)
            up = jnp.dot(x, wgu_bf[:, D_FF + c * cw:D_FF + (c + 1) * cw], preferred_element_type=F32)
            up = jnp.clip(up + bgu_ref[:, D_FF + c * cw:D_FF + (c + 1) * cw], -SWIGLU_LIMIT, SWIGLU_LIMIT)
            act = gate * jax.nn.sigmoid(SWIGLU_ALPHA * gate) * (up + 1.0)
            act_bf[:, c * cw:(c + 1) * cw] = act.astype(BF16)
        ow = D_MODEL // FF_CHUNKS
        for c in range(FF_CHUNKS):
            out = jnp.dot(act_bf[...], wd_bf[:, c * ow:(c + 1) * ow], preferred_element_type=F32)
            out = out + bd_ref[:, c * ow:(c + 1) * ow]
            for j in range(ow // LANES):
                ys_ref[pl.ds(c * (ow // LANES) + j, ROW_TILE, stride=SLAB), :] = (
                    out[:, LANES * j:LANES * (j + 1)])

    @pl.when(i >= used_ref[0])
    def _():
        ys_ref[...] = jnp.zeros_like(ys_ref)


def _experts(xs, tile_expert, used, w_gu, b_gu, w_down, b_down):
    rows = xs.shape[0] // SLAB
    nt = rows // ROW_TILE

    def tile(i, te, used):
        return (jnp.minimum(i, used[0] - 1), 0)

    grid_spec = pltpu.PrefetchScalarGridSpec(
        num_scalar_prefetch=2,
        grid=(nt,),
        in_specs=[pl.BlockSpec((ROW_TILE * SLAB, LANES), tile),
                  pl.BlockSpec((None, D_MODEL, 2 * D_FF), lambda i, te, used: (te[i], 0, 0)),
                  pl.BlockSpec((None, 1, 2 * D_FF), lambda i, te, used: (te[i], 0, 0)),
                  pl.BlockSpec((None, D_FF, D_MODEL), lambda i, te, used: (te[i], 0, 0)),
                  pl.BlockSpec((None, 1, D_MODEL), lambda i, te, used: (te[i], 0, 0))],
        out_specs=pl.BlockSpec((ROW_TILE * SLAB, LANES), lambda i, te, used: (i, 0)),
        scratch_shapes=[pltpu.VMEM((D_MODEL, 2 * D_FF), BF16), pltpu.VMEM((D_FF, D_MODEL), BF16),
                        pltpu.VMEM((ROW_TILE, D_FF), BF16)],
    )
    return pl.pallas_call(
        _experts_kernel,
        grid_spec=grid_spec,
        out_shape=jax.ShapeDtypeStruct((rows * SLAB, LANES), F32),
        compiler_params=_cparams(("arbitrary",)),
        name="experts",
    )(tile_expert, used, xs, w_gu, b_gu.reshape(N_EXPERTS, 1, 2 * D_FF), w_down,
      b_down.reshape(N_EXPERTS, 1, D_MODEL))


X_RING = 3


def _mlp_kernel(te_ref, used_ref, grp_ref, nxt_ref, xs_ref, wgu_ref, bgu_ref, wd_ref, bd_ref, ys_ref,
                wgu_f, wd_f, wgu_bf, wd_bf, act_bf, xring, wsem, xsem):
    i = pl.program_id(0)
    used = used_ref[0]
    tile_rows = ROW_TILE * SLAB

    def weights(e, ws):
        return (pltpu.make_async_copy(wgu_ref.at[e], wgu_f.at[ws], wsem.at[ws, 0]),
                pltpu.make_async_copy(wd_ref.at[e], wd_f.at[ws], wsem.at[ws, 1]))

    def x_tile(t):
        t = jnp.asarray(t, I32)
        rows = pl.ds(pl.multiple_of(t * tile_rows, tile_rows), tile_rows)
        return pltpu.make_async_copy(xs_ref.at[rows], xring.at[t % X_RING], xsem.at[t % X_RING])

    @pl.when(i == 0)
    def _():
        for cp in weights(te_ref[0], 0):
            cp.start()
        for t in range(X_RING - 1):
            @pl.when(t < used)
            def _():
                x_tile(t).start()

    @pl.when(i + X_RING - 1 < used)
    def _():
        x_tile(i + X_RING - 1).start()

    ws = grp_ref[i] % 2
    fresh = (i == 0) | (te_ref[i] != te_ref[jnp.maximum(i - 1, 0)])

    @pl.when(fresh & (i < used))
    def _():
        for cp in weights(te_ref[i], ws):
            cp.wait()

        @pl.when(nxt_ref[i] >= 0)
        def _():
            for cp in weights(nxt_ref[i], 1 - ws):
                cp.start()

        wgu_bf[...] = wgu_f[ws].astype(BF16)
        wd_bf[...] = wd_f[ws].astype(BF16)

    @pl.when(i < used)
    def _():
        x_tile(i).wait()
        x = jnp.concatenate([p.astype(BF16) for p in _load_slabs(xring.at[i % X_RING], ROW_TILE)],
                            axis=1)
        cw = D_FF // FF_CHUNKS
        for c in range(FF_CHUNKS):
            gate = jnp.dot(x, wgu_bf[:, c * cw:(c + 1) * cw], preferred_element_type=F32)
            gate = jnp.minimum(gate + bgu_ref[:, c * cw:(c + 1) * cw], SWIGLU_LIMIT)
            up = jnp.dot(x, wgu_bf[:, D_FF + c * cw:D_FF + (c + 1) * cw], preferred_element_type=F32)
            up = jnp.clip(up + bgu_ref[:, D_FF + c * cw:D_FF + (c + 1) * cw], -SWIGLU_LIMIT, SWIGLU_LIMIT)
            act = gate * jax.nn.sigmoid(SWIGLU_ALPHA * gate) * (up + 1.0)
            act_bf[:, c * cw:(c + 1) * cw] = act.astype(BF16)
        ow = D_MODEL // FF_CHUNKS
        for c in range(FF_CHUNKS):
            out = jnp.dot(act_bf[...], wd_bf[:, c * ow:(c + 1) * ow], preferred_element_type=F32)
            out = out + bd_ref[:, c * ow:(c + 1) * ow]
            for j in range(ow // LANES):
                ys_ref[pl.ds(c * (ow // LANES) + j, ROW_TILE, stride=SLAB), :] = (
                    out[:, LANES * j:LANES * (j + 1)])

    @pl.when(i >= used)
    def _():
        ys_ref[...] = jnp.zeros_like(ys_ref)


def _mlp(xs, tile_expert, used, grp, nxt, w_gu, b_gu, w_down, b_down):
    rows = xs.shape[0] // SLAB
    nt = rows // ROW_TILE
    ex = lambda i, te, *_: (te[i], 0, 0)
    grid_spec = pltpu.PrefetchScalarGridSpec(
        num_scalar_prefetch=4,
        grid=(nt,),
        in_specs=[pl.BlockSpec(memory_space=pl.ANY),
                  pl.BlockSpec(memory_space=pl.ANY),
                  pl.BlockSpec((None, 1, 2 * D_FF), ex),
                  pl.BlockSpec(memory_space=pl.ANY),
                  pl.BlockSpec((None, 1, D_MODEL), ex)],
        out_specs=pl.BlockSpec((ROW_TILE * SLAB, LANES), lambda i, *_: (i, 0)),
        scratch_shapes=[pltpu.VMEM((2, D_MODEL, 2 * D_FF), F32), pltpu.VMEM((2, D_FF, D_MODEL), F32),
                        pltpu.VMEM((D_MODEL, 2 * D_FF), BF16), pltpu.VMEM((D_FF, D_MODEL), BF16),
                        pltpu.VMEM((ROW_TILE, D_FF), BF16),
                        pltpu.VMEM((X_RING, ROW_TILE * SLAB, LANES), F32),
                        pltpu.SemaphoreType.DMA((2, 2)), pltpu.SemaphoreType.DMA((X_RING,))],
    )
    return pl.pallas_call(
        _mlp_kernel,
        grid_spec=grid_spec,
        out_shape=jax.ShapeDtypeStruct((rows * SLAB, LANES), F32),
        compiler_params=_cparams(("arbitrary",)),
        name="experts",
    )(tile_expert, used, grp, nxt, xs, w_gu, b_gu.reshape(N_EXPERTS, 1, 2 * D_FF), w_down,
      b_down.reshape(N_EXPERTS, 1, D_MODEL))


def _mlp_loop_kernel(te_ref, used_ref, grp_ref, nxt_ref, xs_ref, wgu_ref, bgu_ref, wd_ref, bd_ref,
                     ys_ref, wgu_f, wd_f, wgu_bf, wd_bf, act_bf, xring, yring, wsem, xsem, ysem,
                     *, n_tiles):
    used = used_ref[0]
    tile_rows = ROW_TILE * SLAB

    def weights(e, ws):
        return (pltpu.make_async_copy(wgu_ref.at[e], wgu_f.at[ws], wsem.at[ws, 0]),
                pltpu.make_async_copy(wd_ref.at[e], wd_f.at[ws], wsem.at[ws, 1]))

    def tile_rows_of(t):
        return pl.ds(pl.multiple_of(t * tile_rows, tile_rows), tile_rows)

    def x_tile(t):
        t = jnp.asarray(t, I32)
        return pltpu.make_async_copy(xs_ref.at[tile_rows_of(t)], xring.at[t % X_RING],
                                     xsem.at[t % X_RING])

    def y_tile(t):
        return pltpu.make_async_copy(yring.at[t % 2], ys_ref.at[tile_rows_of(t)], ysem.at[t % 2])

    for cp in weights(te_ref[0], 0):
        cp.start()
    for t in range(X_RING - 1):
        @pl.when(t < used)
        def _():
            x_tile(t).start()

    def step(i, carry):
        @pl.when(i + X_RING - 1 < used)
        def _():
            x_tile(i + X_RING - 1).start()

        e = te_ref[i]
        ws = grp_ref[i] % 2
        fresh = (i == 0) | (e != te_ref[jnp.maximum(i - 1, 0)])

        @pl.when(fresh)
        def _():
            for cp in weights(e, ws):
                cp.wait()

            @pl.when(nxt_ref[i] >= 0)
            def _():
                for cp in weights(nxt_ref[i], 1 - ws):
                    cp.start()

            wgu_bf[...] = wgu_f[ws].astype(BF16)
            wd_bf[...] = wd_f[ws].astype(BF16)

        x_tile(i).wait()

        @pl.when(i >= 2)
        def _():
            y_tile(i - 2).wait()

        x = jnp.concatenate([p.astype(BF16) for p in _load_slabs(xring.at[i % X_RING], ROW_TILE)],
                            axis=1)
        bgu = bgu_ref[pl.ds(e, 1), :]
        bd = bd_ref[pl.ds(e, 1), :]
        cw = D_FF // FF_CHUNKS
        for c in range(FF_CHUNKS):
            gate = jnp.dot(x, wgu_bf[:, c * cw:(c + 1) * cw], preferred_element_type=F32)
            gate = jnp.minimum(gate + bgu[:, c * cw:(c + 1) * cw], SWIGLU_LIMIT)
            up = jnp.dot(x, wgu_bf[:, D_FF + c * cw:D_FF + (c + 1) * cw], preferred_element_type=F32)
            up = jnp.clip(up + bgu[:, D_FF + c * cw:D_FF + (c + 1) * cw], -SWIGLU_LIMIT, SWIGLU_LIMIT)
            act = gate * jax.nn.sigmoid(SWIGLU_ALPHA * gate) * (up + 1.0)
            act_bf[:, c * cw:(c + 1) * cw] = act.astype(BF16)
        ow = D_MODEL // FF_CHUNKS
        for c in range(FF_CHUNKS):
            out = jnp.dot(act_bf[...], wd_bf[:, c * ow:(c + 1) * ow], preferred_element_type=F32)
            out = out + bd[:, c * ow:(c + 1) * ow]
            for j in range(ow // LANES):
                yring[i % 2, pl.ds(c * (ow // LANES) + j, ROW_TILE, stride=SLAB), :] = (
                    out[:, LANES * j:LANES * (j + 1)])
        y_tile(i).start()
        return carry

    lax.fori_loop(0, used, step, 0)

    @pl.when(used >= 2)
    def _():
        y_tile(used - 2).wait()
    y_tile(used - 1).wait()

    yring[0] = jnp.zeros(yring.shape[1:], F32)

    def clear(fn):
        def body(t, carry):
            fn(pltpu.make_async_copy(yring.at[0], ys_ref.at[tile_rows_of(t)], ysem.at[0]))
            return carry
        lax.fori_loop(used, n_tiles, body, 0)

    clear(lambda cp: cp.start())
    clear(lambda cp: cp.wait())


def _mlp_loop(xs, tile_expert, used, grp, nxt, w_gu, b_gu, w_down, b_down):
    rows = xs.shape[0] // SLAB
    nt = rows // ROW_TILE
    whole = lambda shape: pl.BlockSpec(shape, lambda i, *_: (0,) * len(shape))
    grid_spec = pltpu.PrefetchScalarGridSpec(
        num_scalar_prefetch=4,
        grid=(1,),
        in_specs=[pl.BlockSpec(memory_space=pl.ANY),
                  pl.BlockSpec(memory_space=pl.ANY),
                  whole((N_EXPERTS, 2 * D_FF)),
                  pl.BlockSpec(memory_space=pl.ANY),
                  whole((N_EXPERTS, D_MODEL))],
        out_specs=pl.BlockSpec(memory_space=pl.ANY),
        scratch_shapes=[pltpu.VMEM((2, D_MODEL, 2 * D_FF), F32), pltpu.VMEM((2, D_FF, D_MODEL), F32),
                        pltpu.VMEM((D_MODEL, 2 * D_FF), BF16), pltpu.VMEM((D_FF, D_MODEL), BF16),
                        pltpu.VMEM((ROW_TILE, D_FF), BF16),
                        pltpu.VMEM((X_RING, ROW_TILE * SLAB, LANES), F32),
                        pltpu.VMEM((2, ROW_TILE * SLAB, LANES), F32),
                        pltpu.SemaphoreType.DMA((2, 2)), pltpu.SemaphoreType.DMA((X_RING,)),
                        pltpu.SemaphoreType.DMA((2,))],
    )
    return pl.pallas_call(
        functools.partial(_mlp_loop_kernel, n_tiles=nt),
        grid_spec=grid_spec,
        out_shape=jax.ShapeDtypeStruct((rows * SLAB, LANES), F32),
        compiler_params=_cparams(("arbitrary",)),
        name="experts",
    )(tile_expert, used, grp, nxt, xs, w_gu, b_gu, w_down, b_down)


def _combine_kernel(dest_ref, next_ref, ys_ref, x1_ref, p_ref, g2_ref, fn_ref, y_ref, buf_ref, sem,
                    *, last):
    i = pl.program_id(0)
    slot = i % 2

    def issue(d_ref, sl):
        def body(r, carry):
            for k in range(TOP_K):
                d = d_ref[r * TOP_K + k]
                pltpu.make_async_copy(ys_ref.at[_slab_rows(d)], buf_ref.at[sl, k, _slab_rows(r)],
                                      sem.at[sl]).start(priority=k % 2)
            return carry
        lax.fori_loop(0, TOK_TILE, body, 0, unroll=2)

    @pl.when(i == 0)
    def _():
        issue(dest_ref, 0)

    @pl.when(i + 1 < pl.num_programs(0))
    def _():
        issue(next_ref, 1 - slot)

    for k in range(TOP_K):
        pltpu.make_async_copy(ys_ref.at[pl.ds(0, TOK_TILE * SLAB)], buf_ref.at[slot, k],
                              sem.at[slot]).wait()

    p = p_ref[...]
    groups = None
    for k in range(TOP_K):
        part = [p[:, k:k + 1] * g for g in _load_slabs(buf_ref.at[slot, k], TOK_TILE)]
        groups = part if groups is None else [a + b for a, b in zip(groups, part)]
    moe = jnp.concatenate(groups, axis=1)
    x2 = x1_ref[...] + g2_ref[...] * moe
    if last:
        x2 = x2 * lax.rsqrt(jnp.mean(x2 * x2, axis=-1, keepdims=True) + EPS) * fn_ref[...]
    y_ref[...] = x2


def _combine(ys, dest_flat, x1, probs, mod, per_token, seq, tile0, final_norm, last):
    n = x1.shape[0]
    tm = TOK_TILE
    tps = max(seq // tm, 1)
    nt = n // tm
    return pl.pallas_call(
        functools.partial(_combine_kernel, last=last),
        grid=(nt,),
        in_specs=[pl.BlockSpec((tm * TOP_K,), lambda i: (i + tile0,), memory_space=pltpu.SMEM),
                  pl.BlockSpec((tm * TOP_K,), lambda i: (jnp.minimum(i + 1, nt - 1) + tile0,),
                               memory_space=pltpu.SMEM),
                  pl.BlockSpec(memory_space=pl.ANY),
                  pl.BlockSpec((tm, D_MODEL), lambda i: (i, 0)),
                  pl.BlockSpec((tm, TOP_K), lambda i: (i + tile0, 0)),
                  _mod_spec(per_token, tm, tps, 5),
                  pl.BlockSpec((1, D_MODEL), lambda i: (0, 0))],
        out_specs=pl.BlockSpec((tm, D_MODEL), lambda i: (i, 0)),
        out_shape=jax.ShapeDtypeStruct((n, D_MODEL), F32),
        scratch_shapes=[pltpu.VMEM((2, TOP_K, tm * SLAB, LANES), F32),
                        pltpu.SemaphoreType.DMA((2,))],
        compiler_params=_cparams(("arbitrary",)),
        name="combine",
    )(dest_flat, dest_flat, ys, x1, probs, mod, final_norm.reshape(1, D_MODEL))


def _invert_kernel(dest_ref, src0_ref, dst0_ref, src_ref, dst_ref, sem, *, n_tok):
    i = pl.program_id(0)
    blk = dest_ref.shape[0]

    @pl.when(i == 0)
    def _():
        copies = [pltpu.make_async_copy(src0_ref, src_ref, sem.at[0]),
                  pltpu.make_async_copy(dst0_ref, dst_ref, sem.at[1])]
        for cp in copies:
            cp.start()
        for cp in copies:
            cp.wait()

    def body(a, carry):
        g = i * blk + a
        d = dest_ref[a]
        tok = g >> (TOP_K.bit_length() - 1)
        src_ref[d] = tok
        dst_ref[d] = (g & (TOP_K - 1)) * n_tok + tok
        return carry

    lax.fori_loop(0, blk, body, 0, unroll=8)


def _invert(dest_flat, n_tok, n_rows):
    blk = TOK_TILE * TOP_K
    whole = pl.BlockSpec(memory_space=pltpu.SMEM)
    src0 = jnp.zeros((n_rows,), I32)
    dst0 = n_tok * TOP_K + jnp.arange(n_rows, dtype=I32)
    return pl.pallas_call(
        functools.partial(_invert_kernel, n_tok=n_tok),
        grid=(dest_flat.shape[0] // blk,),
        in_specs=[pl.BlockSpec((blk,), lambda i: (i,), memory_space=pltpu.SMEM),
                  pl.BlockSpec(memory_space=pl.ANY), pl.BlockSpec(memory_space=pl.ANY)],
        out_specs=[whole, whole],
        out_shape=[jax.ShapeDtypeStruct((n_rows,), I32), jax.ShapeDtypeStruct((n_rows,), I32)],
        scratch_shapes=[pltpu.SemaphoreType.DMA((2,))],
        compiler_params=_cparams(("arbitrary",)),
        name="invert",
    )(dest_flat, src0, dst0)


FF_CHUNKS = 4


def _moe_kernel(src_ref, dst_ref, te_ref, used_ref, h_ref, wgu_ref, bgu_ref, wd_ref, bd_ref, y_ref,
                wgu_bf, wd_bf, act_bf, buf, gsem, ssem):
    i = pl.program_id(0)
    used = used_ref[0]
    slot = i % 2
    tile_rows = ROW_TILE * SLAB
    n_tiles = te_ref.shape[0]

    def gather(tile, sl, rows):
        for r in rows:
            tok = src_ref[tile * ROW_TILE + r]
            pltpu.make_async_copy(h_ref.at[_slab_rows(tok)], buf.at[sl, _slab_rows(r)],
                                  gsem.at[sl]).start(priority=1)

    def scatter(tile, sl, rows):
        for r in rows:
            out = dst_ref[tile * ROW_TILE + r]
            pltpu.make_async_copy(buf.at[2 + sl, _slab_rows(r)], y_ref.at[_slab_rows(out)],
                                  ssem.at[sl]).start()

    def wait_gather(sl):
        pltpu.make_async_copy(h_ref.at[pl.ds(0, tile_rows)], buf.at[sl], gsem.at[sl]).wait()

    def wait_scatter(sl):
        pltpu.make_async_copy(buf.at[2 + sl], y_ref.at[pl.ds(0, tile_rows)], ssem.at[sl]).wait()

    @pl.when(i == 0)
    def _():
        buf[2] = jnp.zeros(buf.shape[1:], F32)
        buf[3] = jnp.zeros(buf.shape[1:], F32)
        tail = y_ref.shape[0] - tile_rows
        pltpu.make_async_copy(buf.at[2], y_ref.at[pl.ds(tail, tile_rows)], ssem.at[0]).start()
        gather(0, 0, range(ROW_TILE))

    prev = te_ref[jnp.maximum(i - 1, 0)]
    fresh = (i == 0) | (te_ref[jnp.minimum(i, n_tiles - 1)] != prev)

    @pl.when(fresh & (i < used))
    def _():
        wgu_bf[...] = wgu_ref[...].astype(BF16)
        wd_bf[...] = wd_ref[...].astype(BF16)

    per = ROW_TILE // FF_CHUNKS
    cw = D_FF // FF_CHUNKS

    @pl.when(i < used)
    def _():
        wait_gather(slot)
        nxt = jnp.minimum(i + 1, n_tiles - 1)
        prv = jnp.where(i >= 1, i - 1, n_tiles)
        half = per // 2
        for c in range(FF_CHUNKS):
            gather(nxt, 1 - slot, range(c * per, (c + 1) * per))
            scatter(prv, 1 - slot, range(c * half, (c + 1) * half))
            x = jnp.concatenate([p.astype(BF16) for p in _load_slabs(buf.at[slot], ROW_TILE)], axis=1)
            gate = jnp.dot(x, wgu_bf[:, c * cw:(c + 1) * cw], preferred_element_type=F32)
            gate = jnp.minimum(gate + bgu_ref[:, c * cw:(c + 1) * cw], SWIGLU_LIMIT)
            up = jnp.dot(x, wgu_bf[:, D_FF + c * cw:D_FF + (c + 1) * cw], preferred_element_type=F32)
            up = jnp.clip(up + bgu_ref[:, D_FF + c * cw:D_FF + (c + 1) * cw], -SWIGLU_LIMIT, SWIGLU_LIMIT)
            act = gate * jax.nn.sigmoid(SWIGLU_ALPHA * gate) * (up + 1.0)
            act_bf[:, c * cw:(c + 1) * cw] = act.astype(BF16)
        ow = D_MODEL // FF_CHUNKS
        base = FF_CHUNKS * half
        wait_scatter(slot)
        for c in range(FF_CHUNKS):
            scatter(prv, 1 - slot, range(base + c * half, base + (c + 1) * half))
            out = jnp.dot(act_bf[...], wd_bf[:, c * ow:(c + 1) * ow], preferred_element_type=F32)
            out = out + bd_ref[:, c * ow:(c + 1) * ow]
            for j in range(ow // LANES):
                buf[2 + slot, pl.ds(c * (ow // LANES) + j, ROW_TILE, stride=SLAB), :] = (
                    out[:, LANES * j:LANES * (j + 1)])

    @pl.when(i == used)
    def _():
        wait_gather(slot)
        wait_scatter(slot)
        scatter(i - 1, 1 - slot, range(ROW_TILE))
        wait_scatter(1 - slot)


def _moe(h2, src, dst, tile_expert, used, w_gu, b_gu, w_down, b_down, n_tok):
    n_tiles = tile_expert.shape[0]
    n_rows = n_tiles * ROW_TILE
    ex = lambda i, src, dst, te, used: (te[jnp.minimum(i, n_tiles - 1)], 0, 0)
    grid_spec = pltpu.PrefetchScalarGridSpec(
        num_scalar_prefetch=4,
        grid=(n_tiles + 1,),
        in_specs=[pl.BlockSpec(memory_space=pl.ANY),
                  pl.BlockSpec((None, D_MODEL, 2 * D_FF), ex),
                  pl.BlockSpec((None, 1, 2 * D_FF), ex),
                  pl.BlockSpec((None, D_FF, D_MODEL), ex),
                  pl.BlockSpec((None, 1, D_MODEL), ex)],
        out_specs=pl.BlockSpec(memory_space=pl.ANY),
        scratch_shapes=[pltpu.VMEM((D_MODEL, 2 * D_FF), BF16), pltpu.VMEM((D_FF, D_MODEL), BF16),
                        pltpu.VMEM((ROW_TILE, D_FF), BF16),
                        pltpu.VMEM((4, ROW_TILE * SLAB, LANES), F32),
                        pltpu.SemaphoreType.DMA((2,)), pltpu.SemaphoreType.DMA((2,))],
    )
    return pl.pallas_call(
        _moe_kernel,
        grid_spec=grid_spec,
        out_shape=jax.ShapeDtypeStruct(((n_tok * TOP_K + n_rows + 2 * ROW_TILE) * SLAB, LANES), F32),
        compiler_params=_cparams(("arbitrary",)),
        name="moe",
    )(src, dst, tile_expert, used, h2, w_gu, b_gu.reshape(N_EXPERTS, 1, 2 * D_FF), w_down,
      b_down.reshape(N_EXPERTS, 1, D_MODEL))


def _merge_kernel(y0_ref, y1_ref, y2_ref, y3_ref, x1_ref, p_ref, g2_ref, fn_ref, o_ref, *, last):
    p = p_ref[...]
    tm = p.shape[0]
    groups = None
    for k, ref in enumerate((y0_ref, y1_ref, y2_ref, y3_ref)):
        part = [p[:, k:k + 1] * g for g in _load_slabs(ref, tm)]
        groups = part if groups is None else [a + b for a, b in zip(groups, part)]
    x2 = x1_ref[...] + g2_ref[...] * jnp.concatenate(groups, axis=1)
    if last:
        x2 = x2 * lax.rsqrt(jnp.mean(x2 * x2, axis=-1, keepdims=True) + EPS) * fn_ref[...]
    o_ref[...] = x2


def _merge(y, x1, probs, mod, per_token, seq, tile0, n_tok, final_norm, last):
    n = x1.shape[0]
    tm = TOK_TILE
    tps = max(seq // tm, 1)
    choice = lambda k: pl.BlockSpec((tm * SLAB, LANES), lambda i: (k * (n_tok // tm) + tile0 + i, 0))
    return pl.pallas_call(
        functools.partial(_merge_kernel, last=last),
        grid=(n // tm,),
        in_specs=[choice(0), choice(1), choice(2), choice(3),
                  pl.BlockSpec((tm, D_MODEL), lambda i: (i, 0)),
                  pl.BlockSpec((tm, TOP_K), lambda i: (i + tile0, 0)),
                  _mod_spec(per_token, tm, tps, 5),
                  pl.BlockSpec((1, D_MODEL), lambda i: (0, 0))],
        out_specs=pl.BlockSpec((tm, D_MODEL), lambda i: (i, 0)),
        out_shape=jax.ShapeDtypeStruct((n, D_MODEL), F32),
        compiler_params=_cparams(("arbitrary",)),
        name="merge",
    )(y, y, y, y, x1, probs, mod, final_norm.reshape(1, D_MODEL))


def kernel(x_prompt, x_sample, state_hgrn, state_pool, c_prompt, c_sample, w_ada, b_ada, norm1,
           norm2, w_in, lower_bounds, hgrn_norm, w_pool, pool_scale, w_out, w_router, b_router,
           w_gu, b_gu, w_down, b_down, final_norm):
    bp, seq_p, _ = x_prompt.shape
    bs, seq_s, _ = x_sample.shape
    np_, ns = bp * seq_p, bs * seq_s
    depth = w_ada.shape[0]
    past_len = 16384

    lbs = jnp.cumsum(jax.nn.softmax(lower_bounds.astype(F32), axis=0), axis=0)
    xp = x_prompt.reshape(np_, D_MODEL)
    xs_ = x_sample.reshape(ns, D_MODEL)
    c_all = jnp.concatenate([c_prompt, c_sample], axis=0)

    sp_l, hp_l, ss_l, hs_l = [], [], [], []
    for l in range(depth):
        w_in_bf = w_in[l].astype(BF16)
        w_out_bf = w_out[l].astype(BF16)
        w_pool_bf = w_pool[l].astype(BF16)
        wr_hi = w_router[l].astype(BF16)
        wr_lo = (w_router[l] - wr_hi.astype(F32)).astype(BF16)
        lb = lbs[l].reshape(1, D_A)
        hn = hgrn_norm[l].reshape(1, D_A)
        psc = pool_scale[l].reshape(1, D_B)

        mod = _ada(c_all, w_ada[l], b_ada[l])
        mod_p = mod[:bp].reshape(bp, 1, 6 * D_MODEL)
        mod_s = jnp.repeat(mod[bp:], seq_s, axis=0)

        up = _inproj(xp, mod_p, False, seq_p, norm1[l], w_in_bf)
        us = _inproj(xs_, mod_s, True, seq_s, norm1[l], w_in_bf)
        us3 = us.reshape(bs, seq_s, D_IN)

        oa_p, s_p = _hgrn(up.reshape(bp, seq_p, D_IN), None, lb, hn)
        oa_p = oa_p.reshape(np_, D_A)
        oa_s, s_s = _hgrn(us3, state_hgrn[l], lb, hn)
        ob_p, h_p = _pool_prompt(up, bp, seq_p, w_pool_bf, psc)
        ob_t, h_t = _pool_sample(us3.transpose(1, 0, 2), state_pool[l].transpose(1, 0, 2), past_len,
                                 w_pool_bf, psc)
        ob_s, h_s = ob_t.transpose(1, 0, 2), h_t.transpose(1, 0, 2)

        n_tok = np_ + ns
        n_tiles = (n_tok * TOP_K) // ROW_TILE + N_EXPERTS
        x1p, *joint = _outproj(xp, oa_p, ob_p, mod_p, False, seq_p, norm2[l], w_out_bf,
                               wr_hi, wr_lo, b_router[l], n_tok, 0)
        x1s, h2, top_idx, probs = _outproj(
            xs_, oa_s.reshape(ns, D_A), ob_s.reshape(ns, D_B), mod_s, True, seq_s, norm2[l],
            w_out_bf, wr_hi, wr_lo, b_router[l], n_tok, np_ // TOK_TILE, joint)

        dest, counts = _rank(top_idx)
        dest_flat = dest.reshape(n_tok * TOP_K)
        tiles_e = (counts.reshape(N_EXPERTS) + ROW_TILE - 1) // ROW_TILE
        ends = jnp.cumsum(tiles_e)
        used = ends[-1:].astype(I32)
        tile_expert = jnp.minimum(
            jnp.sum(jnp.arange(n_tiles, dtype=I32)[:, None] >= ends[None, :], axis=1),
            N_EXPERTS - 1).astype(I32)

        last_tile = (ends - 1).astype(I32)

        xsort = _dispatch(h2, dest_flat, last_tile, tiles_e.astype(I32), used, n_tiles)
        grp = jnp.cumsum(jnp.concatenate(
            [jnp.zeros((1,), I32), (tile_expert[1:] != tile_expert[:-1]).astype(I32)]))
        eids = jnp.arange(N_EXPERTS, dtype=I32)
        later = (eids[None, :] > eids[:, None]) & (tiles_e > 0)[None, :]
        nxt_e = jnp.min(jnp.where(later, eids[None, :], N_EXPERTS), axis=1)
        nxt = jnp.where(nxt_e < N_EXPERTS, nxt_e, -1)[tile_expert].astype(I32)
        ysort = _mlp_loop(xsort, tile_expert, used, grp.astype(I32), nxt, w_gu[l], b_gu[l],
                          w_down[l], b_down[l])

        last = l == depth - 1
        xp = _combine(ysort, dest_flat, x1p, probs, mod_p, False, seq_p, 0, final_norm, last)
        xs_ = _combine(ysort, dest_flat, x1s, probs, mod_s, True, seq_s, np_ // TOK_TILE,
                       final_norm, last)

        sp_l.append(s_p)
        hp_l.append(h_p)
        ss_l.append(s_s)
        hs_l.append(h_s)

    return (xp.reshape(bp, seq_p, D_MODEL), xs_.reshape(bs, seq_s, D_MODEL),
            jnp.stack(sp_l), jnp.stack(hp_l), jnp.stack(ss_l), jnp.stack(hs_l))
```

```python
import functools

import jax
import jax.numpy as jnp
from jax import lax
from jax.experimental import pallas as pl
from jax.experimental.pallas import tpu as pltpu

F32, BF16, I32 = jnp.float32, jnp.bfloat16, jnp.int32

D_MODEL = 1024
D_A = 512
H_A = 4
DK = 128
DV = 128
D_B = 512
POOL_WINDOWS = (2, 4, 8, 16)
POOL_GROUP = 128
POOL_HIST = 15
D_IN = 4 * D_A + D_B
N_EXPERTS = 32
TOP_K = 4
D_FF = 1024
SWIGLU_LIMIT = 7.0
SWIGLU_ALPHA = 1.702
EPS = 1e-6

SUB = 16
ROW_TILE = 256
TOK_TILE = 256
VMEM_LIMIT = 56 * 1024 * 1024


def _cparams(sem):
    return pltpu.CompilerParams(dimension_semantics=sem, vmem_limit_bytes=VMEM_LIMIT)


def _silu(x):
    return x * jax.nn.sigmoid(x)


def _mod_spec(per_token, tm, tiles_per_seq, col, last_tile=None):
    tile = (lambda i: i) if last_tile is None else (lambda i: jnp.minimum(i, last_tile))
    if per_token:
        return pl.BlockSpec((tm, D_MODEL), lambda i: (tile(i), col))
    return pl.BlockSpec((None, 1, D_MODEL), lambda i: (tile(i) // tiles_per_seq, 0, col))


def _ada_kernel(c_ref, w_ref, b_ref, o_ref):
    c = c_ref[...]
    o_ref[...] = jnp.dot(_silu(c).astype(BF16), w_ref[...].astype(BF16),
                         preferred_element_type=F32) + b_ref[...]


def _ada(c_all, w_ada, b_ada):
    rows = c_all.shape[0]
    n = w_ada.shape[1]
    return pl.pallas_call(
        _ada_kernel,
        grid=(n // D_MODEL,),
        in_specs=[pl.BlockSpec((rows, D_MODEL), lambda j: (0, 0)),
                  pl.BlockSpec((D_MODEL, D_MODEL), lambda j: (0, j)),
                  pl.BlockSpec((1, D_MODEL), lambda j: (0, j))],
        out_specs=pl.BlockSpec((rows, D_MODEL), lambda j: (0, j)),
        out_shape=jax.ShapeDtypeStruct((rows, n), F32),
        compiler_params=_cparams(("arbitrary",)),
        name="ada",
    )(c_all, w_ada, b_ada.reshape(1, n))


def _inproj_kernel(x_ref, sh_ref, sc_ref, n1_ref, w_ref, u_ref):
    x = x_ref[...]
    h = x * lax.rsqrt(jnp.mean(x * x, axis=-1, keepdims=True) + EPS) * n1_ref[...]
    h = h * (1.0 + sc_ref[...]) + sh_ref[...]
    u_ref[...] = jnp.dot(h.astype(BF16), w_ref[...], preferred_element_type=F32)


def _inproj(x, mod, per_token, seq, norm1, w_in_bf):
    n = x.shape[0]
    tm = min(512, n)
    tps = max(seq // tm, 1)
    return pl.pallas_call(
        _inproj_kernel,
        grid=(n // tm,),
        in_specs=[pl.BlockSpec((tm, D_MODEL), lambda i: (i, 0)),
                  _mod_spec(per_token, tm, tps, 0),
                  _mod_spec(per_token, tm, tps, 1),
                  pl.BlockSpec((1, D_MODEL), lambda i: (0, 0)),
                  pl.BlockSpec((D_MODEL, D_IN), lambda i: (0, 0))],
        out_specs=pl.BlockSpec((tm, D_IN), lambda i: (i, 0)),
        out_shape=jax.ShapeDtypeStruct((n, D_IN), F32),
        compiler_params=_cparams(("arbitrary",)),
        name="inproj",
    )(x, mod, mod, norm1.reshape(1, D_MODEL), w_in_bf)


def _block_select():
    r = lax.broadcasted_iota(I32, (SUB, SUB * DK), 0)
    c = lax.broadcasted_iota(I32, (SUB, SUB * DK), 1) >> (DK.bit_length() - 1)
    return (r == c).astype(BF16)


def _hgrn_block(q, fp, iv, g, lb, hn, st_ref, sel_ref, valid):
    f = lb + (1.0 - lb) * jax.nn.sigmoid(fp)
    logf = jnp.log(f)
    kk = 1.0 - f
    if valid < SUB:
        live = lax.broadcasted_iota(I32, (SUB, D_A), 0) < valid
        logf = jnp.where(live, logf, 0.0)
        kk = jnp.where(live, kk, 0.0)
    qa = _silu(q)

    r = lax.broadcasted_iota(I32, (SUB, SUB), 0)
    c = lax.broadcasted_iota(I32, (SUB, SUB), 1)
    tri = (r >= c).astype(BF16)
    hi = logf.astype(BF16)
    lo = (logf - hi.astype(F32)).astype(BF16)
    b = (jnp.dot(tri, hi, preferred_element_type=F32)
         + jnp.dot(tri, lo, preferred_element_type=F32))
    bend = b[SUB - 1:SUB, :]
    qt = (qa * jnp.exp(b)).astype(BF16)
    kh = (kk * jnp.exp(bend - b)).astype(BF16)
    dec = jnp.exp(bend)
    ivb = iv.astype(BF16)

    rows = []
    for h in range(H_A):
        sl = slice(DK * h, DK * (h + 1))
        bh, qh, kkh = b[:, sl], qa[:, sl], kk[:, sl]
        ps = []
        for s in range(SUB):
            e = jnp.exp(jnp.minimum(bh - bh[s:s + 1, :], 0.0))
            ps.append((e * (qh * kkh[s:s + 1, :])).astype(BF16))
        rows.append(jnp.concatenate(ps, axis=1))
    pcat = jnp.concatenate(rows, axis=0)
    a = lax.dot_general(pcat, sel_ref[...], (((1,), (1,)), ((), ())),
                        preferred_element_type=F32)
    tr = lax.broadcasted_iota(I32, (H_A * SUB, SUB), 0) & (SUB - 1)
    tc = lax.broadcasted_iota(I32, (H_A * SUB, SUB), 1)
    a = jnp.where(tr >= tc, a, 0.0).astype(BF16)

    outs = []
    for h in range(H_A):
        sl = slice(DK * h, DK * (h + 1))
        st = st_ref[h]
        o = lax.dot_general(qt[:, sl], st.astype(BF16), (((1,), (1,)), ((), ())),
                            preferred_element_type=F32)
        o = o + jnp.dot(a[SUB * h:SUB * (h + 1), :], ivb[:, sl], preferred_element_type=F32)
        ut = lax.dot_general(ivb[:, sl], kh[:, sl], (((0,), (0,)), ((), ())),
                             preferred_element_type=F32)
        st_ref[h] = st * dec[:, sl] + ut
        o = o * lax.rsqrt(jnp.mean(o * o, axis=-1, keepdims=True) + EPS) * hn[:, sl]
        outs.append(o)
    return jnp.concatenate(outs, axis=1) * _silu(g)


def _hgrn_prompt_kernel(q_ref, f_ref, i_ref, g_ref, lb_ref, hn_ref, o_ref, s_ref, st_ref, sel_ref,
                        *, tt):
    t = pl.program_id(1)

    @pl.when(t == 0)
    def _():
        st_ref[...] = jnp.zeros_like(st_ref)
        sel_ref[...] = _block_select()

    lb = lb_ref[...]
    hn = hn_ref[...]

    def body(j, carry):
        rs = pl.ds(pl.multiple_of(j * SUB, SUB), SUB)
        o_ref[rs, :] = _hgrn_block(q_ref[rs, :], f_ref[rs, :], i_ref[rs, :], g_ref[rs, :],
                                   lb, hn, st_ref, sel_ref, SUB)
        return carry

    lax.fori_loop(0, tt // SUB, body, 0)

    @pl.when(t == pl.num_programs(1) - 1)
    def _():
        for h in range(H_A):
            s_ref[h] = st_ref[h].T


def _hgrn_prompt(u, batch, seq, lb, hn):
    tt = 256
    nt = seq // tt

    def col(cb):
        return pl.BlockSpec((tt, D_A), lambda b, t: (b * nt + t, cb))

    return pl.pallas_call(
        functools.partial(_hgrn_prompt_kernel, tt=tt),
        grid=(batch, nt),
        in_specs=[col(0), col(1), col(2), col(3),
                  pl.BlockSpec((1, D_A), lambda b, t: (0, 0)),
                  pl.BlockSpec((1, D_A), lambda b, t: (0, 0))],
        out_specs=[pl.BlockSpec((tt, D_A), lambda b, t: (b * nt + t, 0)),
                   pl.BlockSpec((None, H_A, DK, DV), lambda b, t: (b, 0, 0, 0))],
        out_shape=[jax.ShapeDtypeStruct((batch * seq, D_A), F32),
                   jax.ShapeDtypeStruct((batch, H_A, DK, DV), F32)],
        scratch_shapes=[pltpu.VMEM((H_A, DV, DK), F32), pltpu.VMEM((SUB, SUB * DK), BF16)],
        compiler_params=_cparams(("arbitrary", "arbitrary")),
        name="hgrn_prompt",
    )(u, u, u, u, lb, hn)


def _hgrn_sample_kernel(q_ref, f_ref, i_ref, g_ref, lb_ref, hn_ref, s0_ref, o_ref, s_ref,
                        st_ref, sel_ref, pad_ref, *, seq):
    for h in range(H_A):
        st_ref[h] = s0_ref[h].T
    sel_ref[...] = _block_select()
    pad_ref[...] = jnp.zeros_like(pad_ref)
    for n, ref in enumerate((q_ref, f_ref, i_ref, g_ref)):
        pad_ref[n, 0:seq, :] = ref[...]
    o = _hgrn_block(pad_ref[0], pad_ref[1], pad_ref[2], pad_ref[3], lb_ref[...], hn_ref[...],
                    st_ref, sel_ref, seq)
    o_ref[...] = o[0:seq, :]
    for h in range(H_A):
        s_ref[h] = st_ref[h].T


def _hgrn_sample(u3, s0, lb, hn):
    batch, seq, _ = u3.shape

    def col(cb):
        return pl.BlockSpec((None, seq, D_A), lambda b: (b, 0, cb))

    return pl.pallas_call(
        functools.partial(_hgrn_sample_kernel, seq=seq),
        grid=(batch,),
        in_specs=[col(0), col(1), col(2), col(3),
                  pl.BlockSpec((1, D_A), lambda b: (0, 0)),
                  pl.BlockSpec((1, D_A), lambda b: (0, 0)),
                  pl.BlockSpec((None, H_A, DK, DV), lambda b: (b, 0, 0, 0))],
        out_specs=[pl.BlockSpec((None, seq, D_A), lambda b: (b, 0, 0)),
                   pl.BlockSpec((None, H_A, DK, DV), lambda b: (b, 0, 0, 0))],
        out_shape=[jax.ShapeDtypeStruct((batch, seq, D_A), F32),
                   jax.ShapeDtypeStruct((batch, H_A, DK, DV), F32)],
        scratch_shapes=[pltpu.VMEM((H_A, DV, DK), F32), pltpu.VMEM((SUB, SUB * DK), BF16),
                        pltpu.VMEM((4, SUB, D_A), F32)],
        compiler_params=_cparams(("arbitrary",)),
        name="hgrn_sample",
    )(u3, u3, u3, u3, lb, hn, s0)


HGRN_SEQS = 8
HALF = SUB // 2


def _hgrn_step(q, fp, iv, g, lb, hn, st_ref, sel_ref, cm_ref, b_ref, k_ref, valid):
    nrow = HGRN_SEQS * SUB
    f = lb + (1.0 - lb) * jax.nn.sigmoid(fp)
    lf = jnp.log2(f)
    kk = 1.0 - f
    if valid < SUB:
        live = (lax.broadcasted_iota(I32, (nrow, D_A), 0) & (SUB - 1)) < valid
        lf = jnp.where(live, lf, 0.0)
        kk = jnp.where(live, kk, 0.0)
    qa = _silu(q)

    hi = lf.astype(BF16)
    lo = (lf - hi.astype(F32)).astype(BF16)
    cum, tot = cm_ref[0], cm_ref[1]
    b = jnp.dot(cum, hi, preferred_element_type=F32) + jnp.dot(cum, lo, preferred_element_type=F32)
    be = jnp.dot(tot, hi, preferred_element_type=F32) + jnp.dot(tot, lo, preferred_element_type=F32)
    qt = (qa * jnp.exp2(b)).astype(BF16)
    kh = (kk * jnp.exp2(be - b)).astype(BF16)
    dec = jnp.exp2(be)
    ivb = iv.astype(BF16)
    for h in range(H_A):
        b_ref[h] = b[:, DK * h:DK * (h + 1)]
        k_ref[h] = kk[:, DK * h:DK * (h + 1)]

    zero_half = jnp.zeros((HALF, D_A), F32)
    tr = lax.broadcasted_iota(I32, (H_A * SUB, SUB), 0) & (SUB - 1)
    tc = lax.broadcasted_iota(I32, (H_A * SUB, SUB), 1)
    sg = _silu(g)
    inter = []
    for n in range(HGRN_SEQS):
        rows = slice(n * SUB, (n + 1) * SUB)
        for h in range(H_A):
            sl = slice(DK * h, DK * (h + 1))
            st = st_ref[n, h]
            inter.append(lax.dot_general(qt[rows, sl], st.astype(BF16), (((1,), (1,)), ((), ())),
                                         preferred_element_type=F32))
            ut = lax.dot_general(ivb[rows, sl], kh[rows, sl], (((0,), (0,)), ((), ())),
                                 preferred_element_type=F32)
            dtile = dec[n * SUB:n * SUB + HALF, sl]
            st_ref[n, h] = (st.reshape(DV // HALF, HALF, DK) * dtile[None]).reshape(DV, DK) + ut

    scores = []
    for n in range(HGRN_SEQS):
        per_s = []
        for s in range(SUB):
            row = n * SUB + s
            b_s = jnp.concatenate([b_ref[h, pl.ds(row, HALF, stride=0), :] for h in range(H_A)], axis=1)
            k_s = jnp.concatenate([k_ref[h, pl.ds(row, HALF, stride=0), :] for h in range(H_A)], axis=1)
            halves = []
            for hf in range(2):
                if hf == 0 and s >= HALF:
                    halves.append(zero_half)
                    continue
                r0 = n * SUB + hf * HALF
                e = jnp.exp2(jnp.minimum(b[r0:r0 + HALF] - b_s, 0.0))
                halves.append(e * (qa[r0:r0 + HALF] * k_s))
            per_s.append(jnp.concatenate(halves, axis=0).astype(BF16))
        pcat = jnp.concatenate(
            [jnp.concatenate([p[:, DK * h:DK * (h + 1)] for p in per_s], axis=1) for h in range(H_A)],
            axis=0)
        a = lax.dot_general(pcat, sel_ref[...], (((1,), (1,)), ((), ())),
                            preferred_element_type=F32)
        scores.append(jnp.where(tr >= tc, a, 0.0).astype(BF16))

    outs = []
    for n in range(HGRN_SEQS):
        rows = slice(n * SUB, (n + 1) * SUB)
        heads = []
        for h in range(H_A):
            sl = slice(DK * h, DK * (h + 1))
            o = inter[n * H_A + h] + jnp.dot(scores[n][h * SUB:(h + 1) * SUB, :], ivb[rows, sl],
                                             preferred_element_type=F32)
            o = o * lax.rsqrt(jnp.mean(o * o, axis=-1, keepdims=True) + EPS) * hn[:, sl]
            heads.append(o)
        outs.append(jnp.concatenate(heads, axis=1) * sg[rows])
    return jnp.concatenate(outs, axis=0)


def _hgrn_kernel(*refs, rows, has_s0):
    if has_s0:
        (q_ref, f_ref, i_ref, g_ref, lb_ref, hn_ref, s0_ref, o_ref, s_ref,
         st_ref, sel_ref, cm_ref, b_ref, k_ref, pad_ref) = refs
    else:
        (q_ref, f_ref, i_ref, g_ref, lb_ref, hn_ref, o_ref, s_ref,
         st_ref, sel_ref, cm_ref, b_ref, k_ref) = refs
    t = pl.program_id(1)
    nrow = HGRN_SEQS * SUB

    @pl.when((t == 0) & (pl.program_id(0) == 0))
    def _():
        sel_ref[...] = _block_select()
        r = lax.broadcasted_iota(I32, (nrow, nrow), 0)
        c = lax.broadcasted_iota(I32, (nrow, nrow), 1)
        shift = SUB.bit_length() - 1
        same = (r >> shift) == (c >> shift)
        cm_ref[0] = (same & (r >= c)).astype(BF16)
        cm_ref[1] = same.astype(BF16)

    lb = lb_ref[...]
    hn = hn_ref[...]
    if has_s0:
        for n in range(HGRN_SEQS):
            for h in range(H_A):
                st_ref[n, h] = s0_ref[n, h].T
        pad_ref[...] = jnp.zeros_like(pad_ref)
        for j, ref in enumerate((q_ref, f_ref, i_ref, g_ref)):
            for n in range(HGRN_SEQS):
                pad_ref[j, n, 0:rows, :] = ref[n * rows:(n + 1) * rows, :]
        o = _hgrn_step(*(pad_ref[j].reshape(nrow, D_A) for j in range(4)), lb, hn,
                       st_ref, sel_ref, cm_ref, b_ref, k_ref, rows)
        for n in range(HGRN_SEQS):
            o_ref[n * rows:(n + 1) * rows, :] = o[n * SUB:n * SUB + rows, :]
    else:
        @pl.when(t == 0)
        def _():
            st_ref[...] = jnp.zeros_like(st_ref)

        def body(j, carry):
            rs = pl.ds(pl.multiple_of(j * SUB, SUB), SUB)
            o = _hgrn_step(*(ref[:, rs, :].reshape(nrow, D_A) for ref in (q_ref, f_ref, i_ref, g_ref)),
                           lb, hn, st_ref, sel_ref, cm_ref, b_ref, k_ref, SUB)
            o_ref[:, rs, :] = o.reshape(HGRN_SEQS, SUB, D_A)
            return carry

        lax.fori_loop(0, rows // SUB, body, 0)

    def emit_state():
        for n in range(HGRN_SEQS):
            for h in range(H_A):
                s_ref[n, h] = st_ref[n, h].T

    if has_s0:
        emit_state()
    else:
        pl.when(t == pl.num_programs(1) - 1)(emit_state)


def _hgrn(u3, s0, lb, hn, seq=None):
    has_s0 = s0 is not None
    if has_s0:
        batch = u3.shape[0] // seq
        rows = seq
        grid = (batch // HGRN_SEQS, 1)
        blk = lambda cb: pl.BlockSpec((HGRN_SEQS * rows, D_A), lambda b, t: (b, cb))
        o_shape = (batch * seq, D_A)
    else:
        batch, seq, _ = u3.shape
        rows = 128
        grid = (batch // HGRN_SEQS, seq // rows)
        blk = lambda cb: pl.BlockSpec((HGRN_SEQS, rows, D_A), lambda b, t: (b, t, cb))
        o_shape = (batch, seq, D_A)
    sblk = pl.BlockSpec((HGRN_SEQS, H_A, DK, DV), lambda b, t: (b, 0, 0, 0))
    vec = pl.BlockSpec((1, D_A), lambda b, t: (0, 0))
    nrow = HGRN_SEQS * SUB
    scratch = [pltpu.VMEM((HGRN_SEQS, H_A, DV, DK), F32), pltpu.VMEM((SUB, SUB * DK), BF16),
               pltpu.VMEM((2, nrow, nrow), BF16), pltpu.VMEM((H_A, nrow, DK), F32),
               pltpu.VMEM((H_A, nrow, DK), F32)]
    if has_s0:
        scratch.append(pltpu.VMEM((4, HGRN_SEQS, SUB, D_A), F32))
    args = (u3, u3, u3, u3, lb, hn) + ((s0,) if has_s0 else ())
    return pl.pallas_call(
        functools.partial(_hgrn_kernel, rows=rows, has_s0=has_s0),
        grid=grid,
        in_specs=[blk(0), blk(1), blk(2), blk(3), vec, vec] + ([sblk] if has_s0 else []),
        out_specs=[blk(0), sblk],
        out_shape=[jax.ShapeDtypeStruct(o_shape, F32),
                   jax.ShapeDtypeStruct((batch, H_A, DK, DV), F32)],
        scratch_shapes=scratch,
        compiler_params=_cparams(("arbitrary", "arbitrary")),
        name="hgrn_state" if has_s0 else "hgrn_scan",
    )(*args)


def _pool_groups(full_ref, base, tm, cnt_fn, wp_ref, ps_ref, step=1):
    outs = []
    for gi, w in enumerate(POOL_WINDOWS):
        ch = slice(POOL_GROUP * gi, POOL_GROUP * (gi + 1))
        v = full_ref[base:base + tm, ch]
        s = v
        for j in range(1, w):
            s = s + full_ref[base - j * step:base - j * step + tm, ch]
        d = s / cnt_fn(w) - v
        outs.append(jnp.dot(d.astype(BF16), wp_ref[gi], preferred_element_type=F32))
    return jnp.concatenate(outs, axis=1) * ps_ref[...]


def _pool_prompt_kernel(v_ref, prev_ref, wp_ref, ps_ref, o_ref, hist_ref, full_ref, *, tm):
    t = pl.program_id(1)
    prev = prev_ref[...]
    full_ref[0:16, :] = jnp.where(t == 0, 0.0, prev)
    full_ref[16:16 + tm, :] = v_ref[...]
    pos = (lax.broadcasted_iota(I32, (tm, 1), 0) + t * tm + 1).astype(F32)
    o_ref[...] = _pool_groups(full_ref, 16, tm, lambda w: jnp.minimum(pos, float(w)),
                              wp_ref, ps_ref)

    @pl.when(t == pl.num_programs(1) - 1)
    def _():
        hist_ref[...] = full_ref[16 + tm - POOL_HIST:16 + tm, :]


def _pool_prompt(u, batch, seq, w_pool_bf, pool_scale):
    tm = min(1024, seq)
    nt = seq // tm
    vcol = (4 * D_A) // D_B
    return pl.pallas_call(
        functools.partial(_pool_prompt_kernel, tm=tm),
        grid=(batch, nt),
        in_specs=[pl.BlockSpec((tm, D_B), lambda b, t: (b * nt + t, vcol)),
                  pl.BlockSpec((16, D_B),
                               lambda b, t: (jnp.maximum((b * nt + t) * (tm // 16) - 1, 0), vcol)),
                  pl.BlockSpec((len(POOL_WINDOWS), POOL_GROUP, POOL_GROUP), lambda b, t: (0, 0, 0)),
                  pl.BlockSpec((1, D_B), lambda b, t: (0, 0))],
        out_specs=[pl.BlockSpec((tm, D_B), lambda b, t: (b * nt + t, 0)),
                   pl.BlockSpec((None, POOL_HIST, D_B), lambda b, t: (b, 0, 0))],
        out_shape=[jax.ShapeDtypeStruct((batch * seq, D_B), F32),
                   jax.ShapeDtypeStruct((batch, POOL_HIST, D_B), F32)],
        scratch_shapes=[pltpu.VMEM((16 + tm, D_B), F32)],
        compiler_params=_cparams(("arbitrary", "arbitrary")),
        name="pool_prompt",
    )(u, u, w_pool_bf, pool_scale)


def _pool_sample_kernel(v_ref, hist_ref, wp_ref, ps_ref, o_ref, nh_ref, full_ref, *,
                        seq, batch, start):
    full_ref[0:batch, :] = jnp.zeros((batch, D_B), F32)
    full_ref[batch:16 * batch, :] = hist_ref[...].reshape(POOL_HIST * batch, D_B)
    full_ref[16 * batch:(16 + seq) * batch, :] = v_ref[...].reshape(seq * batch, D_B)

    def cnt(w):
        rows = [jnp.full((batch, 1), float(min(start + t + 1, w)), F32) for t in range(seq)]
        return jnp.concatenate(rows, axis=0)

    o = _pool_groups(full_ref, 16 * batch, seq * batch, cnt, wp_ref, ps_ref, step=batch)
    o_ref[...] = o.reshape(seq, batch, D_B)
    nh_ref[...] = full_ref[(16 + seq - POOL_HIST) * batch:(16 + seq) * batch, :].reshape(
        POOL_HIST, batch, D_B)


def _pool_sample(u_t, hist_t, start, w_pool_bf, pool_scale):
    seq, batch, _ = u_t.shape
    vcol = (4 * D_A) // D_B
    return pl.pallas_call(
        functools.partial(_pool_sample_kernel, seq=seq, batch=batch, start=start),
        grid=(1,),
        in_specs=[pl.BlockSpec((seq, batch, D_B), lambda i: (0, 0, vcol)),
                  pl.BlockSpec((POOL_HIST, batch, D_B), lambda i: (0, 0, 0)),
                  pl.BlockSpec((len(POOL_WINDOWS), POOL_GROUP, POOL_GROUP), lambda i: (0, 0, 0)),
                  pl.BlockSpec((1, D_B), lambda i: (0, 0))],
        out_specs=[pl.BlockSpec((seq, batch, D_B), lambda i: (0, 0, 0)),
                   pl.BlockSpec((POOL_HIST, batch, D_B), lambda i: (0, 0, 0))],
        out_shape=[jax.ShapeDtypeStruct((seq, batch, D_B), F32),
                   jax.ShapeDtypeStruct((POOL_HIST, batch, D_B), F32)],
        scratch_shapes=[pltpu.VMEM(((16 + seq) * batch, D_B), F32)],
        compiler_params=_cparams(("arbitrary",)),
        name="pool_sample",
    )(u_t, hist_t, w_pool_bf, pool_scale)


LANES = 128
SLAB = D_MODEL // LANES


def _store_slabs(ref, value):
    n = value.shape[0]
    for j in range(SLAB):
        ref[pl.ds(j, n, stride=SLAB), :] = value[:, LANES * j:LANES * (j + 1)]


def _load_slabs(ref, n):
    return [ref[pl.ds(j, n, stride=SLAB), :] for j in range(SLAB)]


def _slab_rows(row):
    return pl.ds(pl.multiple_of(row * SLAB, SLAB), SLAB)


def _split_dot(a, w_hi, w_lo):
    a_hi = a.astype(BF16)
    a_lo = (a - a_hi.astype(F32)).astype(BF16)
    return (jnp.dot(a_hi, w_hi, preferred_element_type=F32)
            + jnp.dot(a_hi, w_lo, preferred_element_type=F32)
            + jnp.dot(a_lo, w_hi, preferred_element_type=F32))


def _outproj_kernel(x_ref, oa_ref, ob_ref, g1_ref, sh2_ref, sc2_ref, n2_ref, woa_ref, wob_ref,
                    wrh_ref, wrl_ref, br_ref, *rest):
    x1_ref, h2_ref, idx_ref, prob_ref, lg_ref = rest[-5:]
    i = pl.program_id(0)
    slot = i % 2

    @pl.when(i == 0)
    def _():
        lg_ref[...] = jnp.zeros_like(lg_ref)

    logits = lg_ref[1 - slot]

    mix = (jnp.dot(oa_ref[...].astype(BF16), woa_ref[...], preferred_element_type=F32)
           + jnp.dot(ob_ref[...].astype(BF16), wob_ref[...], preferred_element_type=F32))
    x1 = x_ref[...] + g1_ref[...] * mix
    x1_ref[...] = x1
    h2 = x1 * lax.rsqrt(jnp.mean(x1 * x1, axis=-1, keepdims=True) + EPS) * n2_ref[...]
    h2 = h2 * (1.0 + sc2_ref[...]) + sh2_ref[...]
    _store_slabs(h2_ref, h2)
    lg_ref[slot] = _split_dot(h2, wrh_ref[...], wrl_ref[...]) + br_ref[...]

    tm = logits.shape[0]
    lane = lax.broadcasted_iota(I32, (tm, N_EXPERTS), 1)
    kcol = lax.broadcasted_iota(I32, (tm, TOP_K), 1)
    vals = jnp.zeros((tm, TOP_K), F32)
    idxs = jnp.zeros((tm, TOP_K), I32)
    for k in range(TOP_K):
        m = jnp.max(logits, axis=-1, keepdims=True)
        am = jnp.min(jnp.where(logits == m, lane, N_EXPERTS), axis=-1, keepdims=True)
        vals = jnp.where(kcol == k, m, vals)
        idxs = jnp.where(kcol == k, am, idxs)
        logits = jnp.where(lane == am, -jnp.inf, logits)
    e = jnp.exp(vals - vals[:, 0:1])
    prob_ref[...] = e / jnp.sum(e, axis=-1, keepdims=True)
    idx_ref[...] = idxs


def _outproj(x, o_a, o_b, mod, per_token, seq, norm2, w_out_bf, wr_hi, wr_lo, b_router,
             n_total, tile0, joint=None):
    n = x.shape[0]
    tm = min(256, n)
    tps = max(seq // tm, 1)
    nt = n // tm
    cur = lambda i: jnp.minimum(i, nt - 1)
    prv = lambda i: jnp.maximum(i - 1, 0)
    row = lambda width: pl.BlockSpec((tm, width), lambda i: (cur(i), 0))
    jrow = lambda width: pl.BlockSpec((tm, width), lambda i: (prv(i) + tile0, 0))
    const = lambda shape: pl.BlockSpec(shape, lambda i: (0,) * len(shape))
    joint = () if joint is None else tuple(joint)
    n_in = 12
    return pl.pallas_call(
        _outproj_kernel,
        grid=(nt + 1,),
        in_specs=[row(D_MODEL), row(D_A), row(D_B),
                  _mod_spec(per_token, tm, tps, 2, nt - 1),
                  _mod_spec(per_token, tm, tps, 3, nt - 1),
                  _mod_spec(per_token, tm, tps, 4, nt - 1),
                  const((1, D_MODEL)),
                  pl.BlockSpec((D_A, D_MODEL), lambda i: (0, 0)),
                  pl.BlockSpec((D_B, D_MODEL), lambda i: (1, 0)),
                  const((D_MODEL, N_EXPERTS)), const((D_MODEL, N_EXPERTS)), const((1, N_EXPERTS))]
                 + [pl.BlockSpec(memory_space=pl.ANY)] * len(joint),
        out_specs=[row(D_MODEL), pl.BlockSpec((tm * SLAB, LANES), lambda i: (cur(i) + tile0, 0)),
                   jrow(TOP_K), jrow(TOP_K)],
        out_shape=[jax.ShapeDtypeStruct((n, D_MODEL), F32),
                   jax.ShapeDtypeStruct((n_total * SLAB, LANES), F32),
                   jax.ShapeDtypeStruct((n_total, TOP_K), I32),
                   jax.ShapeDtypeStruct((n_total, TOP_K), F32)],
        input_output_aliases={n_in + j: 1 + j for j in range(len(joint))},
        scratch_shapes=[pltpu.VMEM((2, tm, N_EXPERTS), F32)],
        compiler_params=_cparams(("arbitrary",)),
        name="outproj",
    )(x, o_a, o_b, mod, mod, mod, norm2.reshape(1, D_MODEL), w_out_bf, w_out_bf,
      wr_hi, wr_lo, b_router.reshape(1, N_EXPERTS), *joint)


def _rank_kernel(idx_ref, dest_ref, cnt_ref, carry_ref, *, te):
    phase = pl.program_id(0)
    i = pl.program_id(1)
    idx = idx_ref[...]
    lane = lax.broadcasted_iota(I32, (te, N_EXPERTS), 1)
    onehots = [(idx[:, k:k + 1] == lane) for k in range(TOP_K)]
    member = jnp.zeros((te, N_EXPERTS), F32)
    for oh in onehots:
        member = member + oh.astype(F32)
    colsum = jnp.sum(member, axis=0, keepdims=True)

    @pl.when((phase == 0) & (i == 0))
    def _():
        carry_ref[...] = jnp.zeros_like(carry_ref)

    @pl.when(phase == 0)
    def _():
        carry_ref[...] += colsum

    @pl.when((phase == 1) & (i == 0))
    def _():
        cnt = carry_ref[...]
        cnt_ref[...] = cnt.astype(I32)
        tiles = jnp.floor((cnt + (ROW_TILE - 1)) * (1.0 / ROW_TILE))
        r = lax.broadcasted_iota(I32, (N_EXPERTS, N_EXPERTS), 0)
        c = lax.broadcasted_iota(I32, (N_EXPERTS, N_EXPERTS), 1)
        before = (r < c).astype(BF16)
        carry_ref[...] = ROW_TILE * jnp.dot(tiles.astype(BF16), before, preferred_element_type=F32)

    @pl.when(phase == 1)
    def _():
        r = lax.broadcasted_iota(I32, (te, te), 0)
        c = lax.broadcasted_iota(I32, (te, te), 1)
        earlier = (r > c).astype(BF16)
        base = carry_ref[...] + jnp.dot(earlier, member.astype(BF16), preferred_element_type=F32)
        kcol = lax.broadcasted_iota(I32, (te, TOP_K), 1)
        dest = jnp.zeros((te, TOP_K), F32)
        for k, oh in enumerate(onehots):
            dk = jnp.sum(jnp.where(oh, base, 0.0), axis=-1, keepdims=True)
            dest = jnp.where(kcol == k, dk, dest)
        dest_ref[...] = dest.astype(I32)
        carry_ref[...] += colsum


def _rank(top_idx):
    n = top_idx.shape[0]
    te = 512
    return pl.pallas_call(
        functools.partial(_rank_kernel, te=te),
        grid=(2, n // te),
        in_specs=[pl.BlockSpec((te, TOP_K), lambda p, i: (i, 0))],
        out_specs=[pl.BlockSpec((te, TOP_K), lambda p, i: (p * i, 0)),
                   pl.BlockSpec((1, N_EXPERTS), lambda p, i: (0, 0))],
        out_shape=[jax.ShapeDtypeStruct((n, TOP_K), I32),
                   jax.ShapeDtypeStruct((1, N_EXPERTS), I32)],
        scratch_shapes=[pltpu.VMEM((1, N_EXPERTS), F32)],
        compiler_params=_cparams(("arbitrary", "arbitrary")),
        name="rank",
    )(top_idx)


def _dispatch_kernel(last_ref, ntile_ref, used_ref, dest_ref, h_ref, xs_ref, zero_ref, sem, zsem,
                     *, n_tiles):
    i = pl.program_id(0)

    @pl.when(i == 0)
    def _():
        zero_ref[...] = jnp.zeros_like(zero_ref)

        def clear(tile):
            rows = pl.ds(pl.multiple_of(tile * (ROW_TILE * SLAB), ROW_TILE * SLAB), ROW_TILE * SLAB)
            return pltpu.make_async_copy(zero_ref, xs_ref.at[rows], zsem)

        def over_tail(fn):
            def body(t, carry):
                fn(clear(t))
                return carry
            lax.fori_loop(used_ref[0], n_tiles, body, 0)

        for fn in (lambda c: c.start(), lambda c: c.wait()):
            for e in range(N_EXPERTS):
                @pl.when(ntile_ref[e] > 0)
                def _():
                    fn(clear(last_ref[e]))
            over_tail(fn)

    def body(r, carry):
        for k in range(TOP_K):
            d = dest_ref[r * TOP_K + k]
            pltpu.make_async_copy(h_ref.at[_slab_rows(r)], xs_ref.at[_slab_rows(d)], sem).start(
                priority=k % 2)
        return carry

    lax.fori_loop(0, TOK_TILE, body, 0, unroll=2)
    for k in range(TOP_K):
        pltpu.make_async_copy(h_ref, xs_ref.at[pl.ds(0, TOK_TILE * SLAB)], sem).wait()


def _dispatch(h2, dest_flat, last_tile, ntiles, used, n_tiles):
    n = h2.shape[0] // SLAB
    grid_spec = pltpu.PrefetchScalarGridSpec(
        num_scalar_prefetch=3,
        grid=(n // TOK_TILE,),
        in_specs=[pl.BlockSpec((TOK_TILE * TOP_K,), lambda i, *_: (i,), memory_space=pltpu.SMEM),
                  pl.BlockSpec((TOK_TILE * SLAB, LANES), lambda i, *_: (i, 0))],
        out_specs=pl.BlockSpec(memory_space=pl.ANY),
        scratch_shapes=[pltpu.VMEM((ROW_TILE * SLAB, LANES), F32),
                        pltpu.SemaphoreType.DMA(()), pltpu.SemaphoreType.DMA(())],
    )
    return pl.pallas_call(
        functools.partial(_dispatch_kernel, n_tiles=n_tiles),
        grid_spec=grid_spec,
        out_shape=jax.ShapeDtypeStruct((n_tiles * ROW_TILE * SLAB, LANES), F32),
        compiler_params=_cparams(("arbitrary",)),
        name="dispatch",
    )(last_tile, ntiles, used, dest_flat, h2)


def _experts_kernel(te_ref, used_ref, xs_ref, wgu_ref, bgu_ref, wd_ref, bd_ref, ys_ref,
                    wgu_bf, wd_bf, act_bf):
    i = pl.program_id(0)
    prev = te_ref[jnp.maximum(i - 1, 0)]
    fresh = (i == 0) | (te_ref[i] != prev)

    @pl.when(fresh)
    def _():
        wgu_bf[...] = wgu_ref[...].astype(BF16)
        wd_bf[...] = wd_ref[...].astype(BF16)

    @pl.when(i < used_ref[0])
    def _():
        x = jnp.concatenate([p.astype(BF16) for p in _load_slabs(xs_ref, ROW_TILE)], axis=1)
        cw = D_FF // FF_CHUNKS
        for c in range(FF_CHUNKS):
            gate = jnp.dot(x, wgu_bf[:, c * cw:(c + 1) * cw], preferred_element_type=F32)
            gate = jnp.minimum(gate + bgu_ref[:, c * cw:(c + 1) * cw], SWIGLU_LIMIT)
            up = jnp.dot(x, wgu_bf[:, D_FF + c * cw:D_FF + (c + 1) * cw], preferred_element_type=F32)
            up = jnp.clip(up + bgu_ref[:, D_FF + c * cw:D_FF + (c + 1) * cw], -SWIGLU_LIMIT, SWIGLU_LIMIT)
            act = gate * jax.nn.sigmoid(SWIGLU_ALPHA * gate) * (up + 1.0)
            act_bf[:, c * cw:(c + 1) * cw] = act.astype(BF16)
        ow = D_MODEL // FF_CHUNKS
        for c in range(FF_CHUNKS):
            out = jnp.dot(act_bf[...], wd_bf[:, c * ow:(c + 1) * ow], preferred_element_type=F32)
            out = out + bd_ref[:, c * ow:(c + 1) * ow]
            for j in range(ow // LANES):
                ys_ref[pl.ds(c * (ow // LANES) + j, ROW_TILE, stride=SLAB), :] = (
                    out[:, LANES * j:LANES * (j + 1)])

    @pl.when(i >= used_ref[0])
    def _():
        ys_ref[...] = jnp.zeros_like(ys_ref)


def _experts(xs, tile_expert, used, w_gu, b_gu, w_down, b_down):
    rows = xs.shape[0] // SLAB
    nt = rows // ROW_TILE

    def tile(i, te, used):
        return (jnp.minimum(i, used[0] - 1), 0)

    grid_spec = pltpu.PrefetchScalarGridSpec(
        num_scalar_prefetch=2,
        grid=(nt,),
        in_specs=[pl.BlockSpec((ROW_TILE * SLAB, LANES), tile),
                  pl.BlockSpec((None, D_MODEL, 2 * D_FF), lambda i, te, used: (te[i], 0, 0)),
                  pl.BlockSpec((None, 1, 2 * D_FF), lambda i, te, used: (te[i], 0, 0)),
                  pl.BlockSpec((None, D_FF, D_MODEL), lambda i, te, used: (te[i], 0, 0)),
                  pl.BlockSpec((None, 1, D_MODEL), lambda i, te, used: (te[i], 0, 0))],
        out_specs=pl.BlockSpec((ROW_TILE * SLAB, LANES), lambda i, te, used: (i, 0)),
        scratch_shapes=[pltpu.VMEM((D_MODEL, 2 * D_FF), BF16), pltpu.VMEM((D_FF, D_MODEL), BF16),
                        pltpu.VMEM((ROW_TILE, D_FF), BF16)],
    )
    return pl.pallas_call(
        _experts_kernel,
        grid_spec=grid_spec,
        out_shape=jax.ShapeDtypeStruct((rows * SLAB, LANES), F32),
        compiler_params=_cparams(("arbitrary",)),
        name="experts",
    )(tile_expert, used, xs, w_gu, b_gu.reshape(N_EXPERTS, 1, 2 * D_FF), w_down,
      b_down.reshape(N_EXPERTS, 1, D_MODEL))


X_RING = 3


def _mlp_kernel(te_ref, used_ref, grp_ref, nxt_ref, xs_ref, wgu_ref, bgu_ref, wd_ref, bd_ref, ys_ref,
                wgu_f, wd_f, wgu_bf, wd_bf, act_bf, xring, wsem, xsem):
    i = pl.program_id(0)
    used = used_ref[0]
    tile_rows = ROW_TILE * SLAB

    def weights(e, ws):
        return (pltpu.make_async_copy(wgu_ref.at[e], wgu_f.at[ws], wsem.at[ws, 0]),
                pltpu.make_async_copy(wd_ref.at[e], wd_f.at[ws], wsem.at[ws, 1]))

    def x_tile(t):
        t = jnp.asarray(t, I32)
        rows = pl.ds(pl.multiple_of(t * tile_rows, tile_rows), tile_rows)
        return pltpu.make_async_copy(xs_ref.at[rows], xring.at[t % X_RING], xsem.at[t % X_RING])

    @pl.when(i == 0)
    def _():
        for cp in weights(te_ref[0], 0):
            cp.start()
        for t in range(X_RING - 1):
            @pl.when(t < used)
            def _():
                x_tile(t).start()

    @pl.when(i + X_RING - 1 < used)
    def _():
        x_tile(i + X_RING - 1).start()

    ws = grp_ref[i] % 2
    fresh = (i == 0) | (te_ref[i] != te_ref[jnp.maximum(i - 1, 0)])

    @pl.when(fresh & (i < used))
    def _():
        for cp in weights(te_ref[i], ws):
            cp.wait()

        @pl.when(nxt_ref[i] >= 0)
        def _():
            for cp in weights(nxt_ref[i], 1 - ws):
                cp.start()

        wgu_bf[...] = wgu_f[ws].astype(BF16)
        wd_bf[...] = wd_f[ws].astype(BF16)

    @pl.when(i < used)
    def _():
        x_tile(i).wait()
        x = jnp.concatenate([p.astype(BF16) for p in _load_slabs(xring.at[i % X_RING], ROW_TILE)],
                            axis=1)
        cw = D_FF // FF_CHUNKS
        for c in range(FF_CHUNKS):
            gate = jnp.dot(x, wgu_bf[:, c * cw:(c + 1) * cw], preferred_element_type=F32)
            gate = jnp.minimum(gate + bgu_ref[:, c * cw:(c + 1) * cw], SWIGLU_LIMIT)
            up = jnp.dot(x, wgu_bf[:, D_FF + c * cw:D_FF + (c + 1) * cw], preferred_element_type=F32)
            up = jnp.clip(up + bgu_ref[:, D_FF + c * cw:D_FF + (c + 1) * cw], -SWIGLU_LIMIT, SWIGLU_LIMIT)
            act = gate * jax.nn.sigmoid(SWIGLU_ALPHA * gate) * (up + 1.0)
            act_bf[:, c * cw:(c + 1) * cw] = act.astype(BF16)
        ow = D_MODEL // FF_CHUNKS
        for c in range(FF_CHUNKS):
            out = jnp.dot(act_bf[...], wd_bf[:, c * ow:(c + 1) * ow], preferred_element_type=F32)
            out = out + bd_ref[:, c * ow:(c + 1) * ow]
            for j in range(ow // LANES):
                ys_ref[pl.ds(c * (ow // LANES) + j, ROW_TILE, stride=SLAB), :] = (
                    out[:, LANES * j:LANES * (j + 1)])

    @pl.when(i >= used)
    def _():
        ys_ref[...] = jnp.zeros_like(ys_ref)


def _mlp(xs, tile_expert, used, grp, nxt, w_gu, b_gu, w_down, b_down):
    rows = xs.shape[0] // SLAB
    nt = rows // ROW_TILE
    ex = lambda i, te, *_: (te[i], 0, 0)
    grid_spec = pltpu.PrefetchScalarGridSpec(
        num_scalar_prefetch=4,
        grid=(nt,),
        in_specs=[pl.BlockSpec(memory_space=pl.ANY),
                  pl.BlockSpec(memory_space=pl.ANY),
                  pl.BlockSpec((None, 1, 2 * D_FF), ex),
                  pl.BlockSpec(memory_space=pl.ANY),
                  pl.BlockSpec((None, 1, D_MODEL), ex)],
        out_specs=pl.BlockSpec((ROW_TILE * SLAB, LANES), lambda i, *_: (i, 0)),
        scratch_shapes=[pltpu.VMEM((2, D_MODEL, 2 * D_FF), F32), pltpu.VMEM((2, D_FF, D_MODEL), F32),
                        pltpu.VMEM((D_MODEL, 2 * D_FF), BF16), pltpu.VMEM((D_FF, D_MODEL), BF16),
                        pltpu.VMEM((ROW_TILE, D_FF), BF16),
                        pltpu.VMEM((X_RING, ROW_TILE * SLAB, LANES), F32),
                        pltpu.SemaphoreType.DMA((2, 2)), pltpu.SemaphoreType.DMA((X_RING,))],
    )
    return pl.pallas_call(
        _mlp_kernel,
        grid_spec=grid_spec,
        out_shape=jax.ShapeDtypeStruct((rows * SLAB, LANES), F32),
        compiler_params=_cparams(("arbitrary",)),
        name="experts",
    )(tile_expert, used, grp, nxt, xs, w_gu, b_gu.reshape(N_EXPERTS, 1, 2 * D_FF), w_down,
      b_down.reshape(N_EXPERTS, 1, D_MODEL))


def _mlp_loop_kernel(te_ref, used_ref, grp_ref, nxt_ref, xs_ref, wgu_ref, bgu_ref, wd_ref, bd_ref,
                     ys_ref, wgu_f, wd_f, wgu_bf, wd_bf, act_bf, xring, yring, wsem, xsem, ysem,
                     *, n_tiles):
    used = used_ref[0]
    tile_rows = ROW_TILE * SLAB

    def weights(e, ws):
        return (pltpu.make_async_copy(wgu_ref.at[e], wgu_f.at[ws], wsem.at[ws, 0]),
                pltpu.make_async_copy(wd_ref.at[e], wd_f.at[ws], wsem.at[ws, 1]))

    def tile_rows_of(t):
        return pl.ds(pl.multiple_of(t * tile_rows, tile_rows), tile_rows)

    def x_tile(t):
        t = jnp.asarray(t, I32)
        return pltpu.make_async_copy(xs_ref.at[tile_rows_of(t)], xring.at[t % X_RING],
                                     xsem.at[t % X_RING])

    def y_tile(t):
        return pltpu.make_async_copy(yring.at[t % 2], ys_ref.at[tile_rows_of(t)], ysem.at[t % 2])

    for cp in weights(te_ref[0], 0):
        cp.start()
    for t in range(X_RING - 1):
        @pl.when(t < used)
        def _():
            x_tile(t).start()

    def step(i, carry):
        @pl.when(i + X_RING - 1 < used)
        def _():
            x_tile(i + X_RING - 1).start()

        e = te_ref[i]
        ws = grp_ref[i] % 2
        fresh = (i == 0) | (e != te_ref[jnp.maximum(i - 1, 0)])

        @pl.when(fresh)
        def _():
            for cp in weights(e, ws):
                cp.wait()

            @pl.when(nxt_ref[i] >= 0)
            def _():
                for cp in weights(nxt_ref[i], 1 - ws):
                    cp.start()

            wgu_bf[...] = wgu_f[ws].astype(BF16)
            wd_bf[...] = wd_f[ws].astype(BF16)

        x_tile(i).wait()

        @pl.when(i >= 2)
        def _():
            y_tile(i - 2).wait()

        x = jnp.concatenate([p.astype(BF16) for p in _load_slabs(xring.at[i % X_RING], ROW_TILE)],
                            axis=1)
        bgu = bgu_ref[pl.ds(e, 1), :]
        bd = bd_ref[pl.ds(e, 1), :]
        cw = D_FF // FF_CHUNKS
        for c in range(FF_CHUNKS):
            gate = jnp.dot(x, wgu_bf[:, c * cw:(c + 1) * cw], preferred_element_type=F32)
            gate = jnp.minimum(gate + bgu[:, c * cw:(c + 1) * cw], SWIGLU_LIMIT)
            up = jnp.dot(x, wgu_bf[:, D_FF + c * cw:D_FF + (c + 1) * cw], preferred_element_type=F32)
            up = jnp.clip(up + bgu[:, D_FF + c * cw:D_FF + (c + 1) * cw], -SWIGLU_LIMIT, SWIGLU_LIMIT)
            act = gate * jax.nn.sigmoid(SWIGLU_ALPHA * gate) * (up + 1.0)
            act_bf[:, c * cw:(c + 1) * cw] = act.astype(BF16)
        ow = D_MODEL // FF_CHUNKS
        for c in range(FF_CHUNKS):
            out = jnp.dot(act_bf[...], wd_bf[:, c * ow:(c + 1) * ow], preferred_element_type=F32)
            out = out + bd[:, c * ow:(c + 1) * ow]
            for j in range(ow // LANES):
                yring[i % 2, pl.ds(c * (ow // LANES) + j, ROW_TILE, stride=SLAB), :] = (
                    out[:, LANES * j:LANES * (j + 1)])
        y_tile(i).start()
        return carry

    lax.fori_loop(0, used, step, 0)

    @pl.when(used >= 2)
    def _():
        y_tile(used - 2).wait()
    y_tile(used - 1).wait()

    yring[0] = jnp.zeros(yring.shape[1:], F32)

    def clear(fn):
        def body(t, carry):
            fn(pltpu.make_async_copy(yring.at[0], ys_ref.at[tile_rows_of(t)], ysem.at[0]))
            return carry
        lax.fori_loop(used, n_tiles, body, 0)

    clear(lambda cp: cp.start())
    clear(lambda cp: cp.wait())


def _mlp_loop(xs, tile_expert, used, grp, nxt, w_gu, b_gu, w_down, b_down):
    rows = xs.shape[0] // SLAB
    nt = rows // ROW_TILE
    whole = lambda shape: pl.BlockSpec(shape, lambda i, *_: (0,) * len(shape))
    grid_spec = pltpu.PrefetchScalarGridSpec(
        num_scalar_prefetch=4,
        grid=(1,),
        in_specs=[pl.BlockSpec(memory_space=pl.ANY),
                  pl.BlockSpec(memory_space=pl.ANY),
                  whole((N_EXPERTS, 2 * D_FF)),
                  pl.BlockSpec(memory_space=pl.ANY),
                  whole((N_EXPERTS, D_MODEL))],
        out_specs=pl.BlockSpec(memory_space=pl.ANY),
        scratch_shapes=[pltpu.VMEM((2, D_MODEL, 2 * D_FF), F32), pltpu.VMEM((2, D_FF, D_MODEL), F32),
                        pltpu.VMEM((D_MODEL, 2 * D_FF), BF16), pltpu.VMEM((D_FF, D_MODEL), BF16),
                        pltpu.VMEM((ROW_TILE, D_FF), BF16),
                        pltpu.VMEM((X_RING, ROW_TILE * SLAB, LANES), F32),
                        pltpu.VMEM((2, ROW_TILE * SLAB, LANES), F32),
                        pltpu.SemaphoreType.DMA((2, 2)), pltpu.SemaphoreType.DMA((X_RING,)),
                        pltpu.SemaphoreType.DMA((2,))],
    )
    return pl.pallas_call(
        functools.partial(_mlp_loop_kernel, n_tiles=nt),
        grid_spec=grid_spec,
        out_shape=jax.ShapeDtypeStruct((rows * SLAB, LANES), F32),
        compiler_params=_cparams(("arbitrary",)),
        name="experts",
    )(tile_expert, used, grp, nxt, xs, w_gu, b_gu, w_down, b_down)


def _combine_kernel(dest_ref, next_ref, ys_ref, x1_ref, p_ref, g2_ref, fn_ref, y_ref, buf_ref, sem,
                    *, last):
    i = pl.program_id(0)
    slot = i % 2

    def issue(d_ref, sl):
        def body(r, carry):
            for k in range(TOP_K):
                d = d_ref[r * TOP_K + k]
                pltpu.make_async_copy(ys_ref.at[_slab_rows(d)], buf_ref.at[sl, k, _slab_rows(r)],
                                      sem.at[sl]).start(priority=k % 2)
            return carry
        lax.fori_loop(0, TOK_TILE, body, 0, unroll=2)

    @pl.when(i == 0)
    def _():
        issue(dest_ref, 0)

    @pl.when(i + 1 < pl.num_programs(0))
    def _():
        issue(next_ref, 1 - slot)

    for k in range(TOP_K):
        pltpu.make_async_copy(ys_ref.at[pl.ds(0, TOK_TILE * SLAB)], buf_ref.at[slot, k],
                              sem.at[slot]).wait()

    p = p_ref[...]
    groups = None
    for k in range(TOP_K):
        part = [p[:, k:k + 1] * g for g in _load_slabs(buf_ref.at[slot, k], TOK_TILE)]
        groups = part if groups is None else [a + b for a, b in zip(groups, part)]
    moe = jnp.concatenate(groups, axis=1)
    x2 = x1_ref[...] + g2_ref[...] * moe
    if last:
        x2 = x2 * lax.rsqrt(jnp.mean(x2 * x2, axis=-1, keepdims=True) + EPS) * fn_ref[...]
    y_ref[...] = x2


def _combine(ys, dest_flat, x1, probs, mod, per_token, seq, tile0, final_norm, last):
    n = x1.shape[0]
    tm = TOK_TILE
    tps = max(seq // tm, 1)
    nt = n // tm
    return pl.pallas_call(
        functools.partial(_combine_kernel, last=last),
        grid=(nt,),
        in_specs=[pl.BlockSpec((tm * TOP_K,), lambda i: (i + tile0,), memory_space=pltpu.SMEM),
                  pl.BlockSpec((tm * TOP_K,), lambda i: (jnp.minimum(i + 1, nt - 1) + tile0,),
                               memory_space=pltpu.SMEM),
                  pl.BlockSpec(memory_space=pl.ANY),
                  pl.BlockSpec((tm, D_MODEL), lambda i: (i, 0)),
                  pl.BlockSpec((tm, TOP_K), lambda i: (i + tile0, 0)),
                  _mod_spec(per_token, tm, tps, 5),
                  pl.BlockSpec((1, D_MODEL), lambda i: (0, 0))],
        out_specs=pl.BlockSpec((tm, D_MODEL), lambda i: (i, 0)),
        out_shape=jax.ShapeDtypeStruct((n, D_MODEL), F32),
        scratch_shapes=[pltpu.VMEM((2, TOP_K, tm * SLAB, LANES), F32),
                        pltpu.SemaphoreType.DMA((2,))],
        compiler_params=_cparams(("arbitrary",)),
        name="combine",
    )(dest_flat, dest_flat, ys, x1, probs, mod, final_norm.reshape(1, D_MODEL))


def _invert_kernel(dest_ref, src0_ref, dst0_ref, src_ref, dst_ref, sem, *, n_tok):
    i = pl.program_id(0)
    blk = dest_ref.shape[0]

    @pl.when(i == 0)
    def _():
        copies = [pltpu.make_async_copy(src0_ref, src_ref, sem.at[0]),
                  pltpu.make_async_copy(dst0_ref, dst_ref, sem.at[1])]
        for cp in copies:
            cp.start()
        for cp in copies:
            cp.wait()

    def body(a, carry):
        g = i * blk + a
        d = dest_ref[a]
        tok = g >> (TOP_K.bit_length() - 1)
        src_ref[d] = tok
        dst_ref[d] = (g & (TOP_K - 1)) * n_tok + tok
        return carry

    lax.fori_loop(0, blk, body, 0, unroll=8)


def _invert(dest_flat, n_tok, n_rows):
    blk = TOK_TILE * TOP_K
    whole = pl.BlockSpec(memory_space=pltpu.SMEM)
    src0 = jnp.zeros((n_rows,), I32)
    dst0 = n_tok * TOP_K + jnp.arange(n_rows, dtype=I32)
    return pl.pallas_call(
        functools.partial(_invert_kernel, n_tok=n_tok),
        grid=(dest_flat.shape[0] // blk,),
        in_specs=[pl.BlockSpec((blk,), lambda i: (i,), memory_space=pltpu.SMEM),
                  pl.BlockSpec(memory_space=pl.ANY), pl.BlockSpec(memory_space=pl.ANY)],
        out_specs=[whole, whole],
        out_shape=[jax.ShapeDtypeStruct((n_rows,), I32), jax.ShapeDtypeStruct((n_rows,), I32)],
        scratch_shapes=[pltpu.SemaphoreType.DMA((2,))],
        compiler_params=_cparams(("arbitrary",)),
        name="invert",
    )(dest_flat, src0, dst0)


FF_CHUNKS = 4


def _moe_kernel(src_ref, dst_ref, te_ref, used_ref, h_ref, wgu_ref, bgu_ref, wd_ref, bd_ref, y_ref,
                wgu_bf, wd_bf, act_bf, buf, gsem, ssem):
    i = pl.program_id(0)
    used = used_ref[0]
    slot = i % 2
    tile_rows = ROW_TILE * SLAB
    n_tiles = te_ref.shape[0]

    def gather(tile, sl, rows):
        for r in rows:
            tok = src_ref[tile * ROW_TILE + r]
            pltpu.make_async_copy(h_ref.at[_slab_rows(tok)], buf.at[sl, _slab_rows(r)],
                                  gsem.at[sl]).start(priority=1)

    def scatter(tile, sl, rows):
        for r in rows:
            out = dst_ref[tile * ROW_TILE + r]
            pltpu.make_async_copy(buf.at[2 + sl, _slab_rows(r)], y_ref.at[_slab_rows(out)],
                                  ssem.at[sl]).start()

    def wait_gather(sl):
        pltpu.make_async_copy(h_ref.at[pl.ds(0, tile_rows)], buf.at[sl], gsem.at[sl]).wait()

    def wait_scatter(sl):
        pltpu.make_async_copy(buf.at[2 + sl], y_ref.at[pl.ds(0, tile_rows)], ssem.at[sl]).wait()

    @pl.when(i == 0)
    def _():
        buf[2] = jnp.zeros(buf.shape[1:], F32)
        buf[3] = jnp.zeros(buf.shape[1:], F32)
        tail = y_ref.shape[0] - tile_rows
        pltpu.make_async_copy(buf.at[2], y_ref.at[pl.ds(tail, tile_rows)], ssem.at[0]).start()
        gather(0, 0, range(ROW_TILE))

    prev = te_ref[jnp.maximum(i - 1, 0)]
    fresh = (i == 0) | (te_ref[jnp.minimum(i, n_tiles - 1)] != prev)

    @pl.when(fresh & (i < used))
    def _():
        wgu_bf[...] = wgu_ref[...].astype(BF16)
        wd_bf[...] = wd_ref[...].astype(BF16)

    per = ROW_TILE // FF_CHUNKS
    cw = D_FF // FF_CHUNKS

    @pl.when(i < used)
    def _():
        wait_gather(slot)
        nxt = jnp.minimum(i + 1, n_tiles - 1)
        prv = jnp.where(i >= 1, i - 1, n_tiles)
        half = per // 2
        for c in range(FF_CHUNKS):
            gather(nxt, 1 - slot, range(c * per, (c + 1) * per))
            scatter(prv, 1 - slot, range(c * half, (c + 1) * half))
            x = jnp.concatenate([p.astype(BF16) for p in _load_slabs(buf.at[slot], ROW_TILE)], axis=1)
            gate = jnp.dot(x, wgu_bf[:, c * cw:(c + 1) * cw], preferred_element_type=F32)
            gate = jnp.minimum(gate + bgu_ref[:, c * cw:(c + 1) * cw], SWIGLU_LIMIT)
            up = jnp.dot(x, wgu_bf[:, D_FF + c * cw:D_FF + (c + 1) * cw], preferred_element_type=F32)
            up = jnp.clip(up + bgu_ref[:, D_FF + c * cw:D_FF + (c + 1) * cw], -SWIGLU_LIMIT, SWIGLU_LIMIT)
            act = gate * jax.nn.sigmoid(SWIGLU_ALPHA * gate) * (up + 1.0)
            act_bf[:, c * cw:(c + 1) * cw] = act.astype(BF16)
        ow = D_MODEL // FF_CHUNKS
        base = FF_CHUNKS * half
        wait_scatter(slot)
        for c in range(FF_CHUNKS):
            scatter(prv, 1 - slot, range(base + c * half, base + (c + 1) * half))
            out = jnp.dot(act_bf[...], wd_bf[:, c * ow:(c + 1) * ow], preferred_element_type=F32)
            out = out + bd_ref[:, c * ow:(c + 1) * ow]
            for j in range(ow // LANES):
                buf[2 + slot, pl.ds(c * (ow // LANES) + j, ROW_TILE, stride=SLAB), :] = (
                    out[:, LANES * j:LANES * (j + 1)])

    @pl.when(i == used)
    def _():
        wait_gather(slot)
        wait_scatter(slot)
        scatter(i - 1, 1 - slot, range(ROW_TILE))
        wait_scatter(1 - slot)


def _moe(h2, src, dst, tile_expert, used, w_gu, b_gu, w_down, b_down, n_tok):
    n_tiles = tile_expert.shape[0]
    n_rows = n_tiles * ROW_TILE
    ex = lambda i, src, dst, te, used: (te[jnp.minimum(i, n_tiles - 1)], 0, 0)
    grid_spec = pltpu.PrefetchScalarGridSpec(
        num_scalar_prefetch=4,
        grid=(n_tiles + 1,),
        in_specs=[pl.BlockSpec(memory_space=pl.ANY),
                  pl.BlockSpec((None, D_MODEL, 2 * D_FF), ex),
                  pl.BlockSpec((None, 1, 2 * D_FF), ex),
                  pl.BlockSpec((None, D_FF, D_MODEL), ex),
                  pl.BlockSpec((None, 1, D_MODEL), ex)],
        out_specs=pl.BlockSpec(memory_space=pl.ANY),
        scratch_shapes=[pltpu.VMEM((D_MODEL, 2 * D_FF), BF16), pltpu.VMEM((D_FF, D_MODEL), BF16),
                        pltpu.VMEM((ROW_TILE, D_FF), BF16),
                        pltpu.VMEM((4, ROW_TILE * SLAB, LANES), F32),
                        pltpu.SemaphoreType.DMA((2,)), pltpu.SemaphoreType.DMA((2,))],
    )
    return pl.pallas_call(
        _moe_kernel,
        grid_spec=grid_spec,
        out_shape=jax.ShapeDtypeStruct(((n_tok * TOP_K + n_rows + 2 * ROW_TILE) * SLAB, LANES), F32),
        compiler_params=_cparams(("arbitrary",)),
        name="moe",
    )(src, dst, tile_expert, used, h2, w_gu, b_gu.reshape(N_EXPERTS, 1, 2 * D_FF), w_down,
      b_down.reshape(N_EXPERTS, 1, D_MODEL))


def _merge_kernel(y0_ref, y1_ref, y2_ref, y3_ref, x1_ref, p_ref, g2_ref, fn_ref, o_ref, *, last):
    p = p_ref[...]
    tm = p.shape[0]
    groups = None
    for k, ref in enumerate((y0_ref, y1_ref, y2_ref, y3_ref)):
        part = [p[:, k:k + 1] * g for g in _load_slabs(ref, tm)]
        groups = part if groups is None else [a + b for a, b in zip(groups, part)]
    x2 = x1_ref[...] + g2_ref[...] * jnp.concatenate(groups, axis=1)
    if last:
        x2 = x2 * lax.rsqrt(jnp.mean(x2 * x2, axis=-1, keepdims=True) + EPS) * fn_ref[...]
    o_ref[...] = x2


def _merge(y, x1, probs, mod, per_token, seq, tile0, n_tok, final_norm, last):
    n = x1.shape[0]
    tm = TOK_TILE
    tps = max(seq // tm, 1)
    choice = lambda k: pl.BlockSpec((tm * SLAB, LANES), lambda i: (k * (n_tok // tm) + tile0 + i, 0))
    return pl.pallas_call(
        functools.partial(_merge_kernel, last=last),
        grid=(n // tm,),
        in_specs=[choice(0), choice(1), choice(2), choice(3),
                  pl.BlockSpec((tm, D_MODEL), lambda i: (i, 0)),
                  pl.BlockSpec((tm, TOP_K), lambda i: (i + tile0, 0)),
                  _mod_spec(per_token, tm, tps, 5),
                  pl.BlockSpec((1, D_MODEL), lambda i: (0, 0))],
        out_specs=pl.BlockSpec((tm, D_MODEL), lambda i: (i, 0)),
        out_shape=jax.ShapeDtypeStruct((n, D_MODEL), F32),
        compiler_params=_cparams(("arbitrary",)),
        name="merge",
    )(y, y, y, y, x1, probs, mod, final_norm.reshape(1, D_MODEL))


def kernel(x_prompt, x_sample, state_hgrn, state_pool, c_prompt, c_sample, w_ada, b_ada, norm1,
           norm2, w_in, lower_bounds, hgrn_norm, w_pool, pool_scale, w_out, w_router, b_router,
           w_gu, b_gu, w_down, b_down, final_norm):
    bp, seq_p, _ = x_prompt.shape
    bs, seq_s, _ = x_sample.shape
    np_, ns = bp * seq_p, bs * seq_s
    depth = w_ada.shape[0]
    past_len = 16384

    lbs = jnp.cumsum(jax.nn.softmax(lower_bounds.astype(F32), axis=0), axis=0)
    xp = x_prompt.reshape(np_, D_MODEL)
    xs_ = x_sample.reshape(ns, D_MODEL)
    c_all = jnp.concatenate([c_prompt, c_sample], axis=0)

    sp_l, hp_l, ss_l, hs_l = [], [], [], []
    for l in range(depth):
        w_in_bf = w_in[l].astype(BF16)
        w_out_bf = w_out[l].astype(BF16)
        w_pool_bf = w_pool[l].astype(BF16)
        wr_hi = w_router[l].astype(BF16)
        wr_lo = (w_router[l] - wr_hi.astype(F32)).astype(BF16)
        lb = lbs[l].reshape(1, D_A)
        hn = hgrn_norm[l].reshape(1, D_A)
        psc = pool_scale[l].reshape(1, D_B)

        mod = _ada(c_all, w_ada[l], b_ada[l])
        mod_p = mod[:bp].reshape(bp, 1, 6 * D_MODEL)
        mod_s = jnp.repeat(mod[bp:], seq_s, axis=0)

        up = _inproj(xp, mod_p, False, seq_p, norm1[l], w_in_bf)
        us = _inproj(xs_, mod_s, True, seq_s, norm1[l], w_in_bf)
        us3 = us.reshape(bs, seq_s, D_IN)

        oa_p, s_p = _hgrn(up.reshape(bp, seq_p, D_IN), None, lb, hn)
        oa_p = oa_p.reshape(np_, D_A)
        oa_s, s_s = _hgrn(us, state_hgrn[l], lb, hn, seq=seq_s)
        ob_p, h_p = _pool_prompt(up, bp, seq_p, w_pool_bf, psc)
        ob_t, h_t = _pool_sample(us3.transpose(1, 0, 2), state_pool[l].transpose(1, 0, 2), past_len,
                                 w_pool_bf, psc)
        ob_s, h_s = ob_t.transpose(1, 0, 2), h_t.transpose(1, 0, 2)

        n_tok = np_ + ns
        n_tiles = (n_tok * TOP_K) // ROW_TILE + N_EXPERTS
        x1p, *joint = _outproj(xp, oa_p, ob_p, mod_p, False, seq_p, norm2[l], w_out_bf,
                               wr_hi, wr_lo, b_router[l], n_tok, 0)
        x1s, h2, top_idx, probs = _outproj(
            xs_, oa_s.reshape(ns, D_A), ob_s.reshape(ns, D_B), mod_s, True, seq_s, norm2[l],
            w_out_bf, wr_hi, wr_lo, b_router[l], n_tok, np_ // TOK_TILE, joint)

        dest, counts = _rank(top_idx)
        dest_flat = dest.reshape(n_tok * TOP_K)
        tiles_e = (counts.reshape(N_EXPERTS) + ROW_TILE - 1) // ROW_TILE
        ends = jnp.cumsum(tiles_e)
        used = ends[-1:].astype(I32)
        tile_expert = jnp.minimum(
            jnp.sum(jnp.arange(n_tiles, dtype=I32)[:, None] >= ends[None, :], axis=1),
            N_EXPERTS - 1).astype(I32)

        last_tile = (ends - 1).astype(I32)

        xsort = _dispatch(h2, dest_flat, last_tile, tiles_e.astype(I32), used, n_tiles)
        grp = jnp.cumsum(jnp.concatenate(
            [jnp.zeros((1,), I32), (tile_expert[1:] != tile_expert[:-1]).astype(I32)]))
        eids = jnp.arange(N_EXPERTS, dtype=I32)
        later = (eids[None, :] > eids[:, None]) & (tiles_e > 0)[None, :]
        nxt_e = jnp.min(jnp.where(later, eids[None, :], N_EXPERTS), axis=1)
        nxt = jnp.where(nxt_e < N_EXPERTS, nxt_e, -1)[tile_expert].astype(I32)
        ysort = _mlp_loop(xsort, tile_expert, used, grp.astype(I32), nxt, w_gu[l], b_gu[l],
                          w_down[l], b_down[l])

        last = l == depth - 1
        xp = _combine(ysort, dest_flat, x1p, probs, mod_p, False, seq_p, 0, final_norm, last)
        xs_ = _combine(ysort, dest_flat, x1s, probs, mod_s, True, seq_s, np_ // TOK_TILE,
                       final_norm, last)

        sp_l.append(s_p)
        hp_l.append(h_p)
        ss_l.append(s_s)
        hs_l.append(h_s)

    return (xp.reshape(bp, seq_p, D_MODEL), xs_.reshape(bs, seq_s, D_MODEL),
            jnp.stack(sp_l), jnp.stack(hp_l), jnp.stack(ss_l), jnp.stack(hs_l))
```

```python
import functools

import jax
import jax.numpy as jnp
from jax import lax
from jax.experimental import pallas as pl
from jax.experimental.pallas import tpu as pltpu

F32, BF16, I32 = jnp.float32, jnp.bfloat16, jnp.int32

D_MODEL = 1024
D_A = 512
H_A = 4
DK = 128
DV = 128
D_B = 512
POOL_WINDOWS = (2, 4, 8, 16)
POOL_GROUP = 128
POOL_HIST = 15
D_IN = 4 * D_A + D_B
N_EXPERTS = 32
TOP_K = 4
D_FF = 1024
SWIGLU_LIMIT = 7.0
SWIGLU_ALPHA = 1.702
EPS = 1e-6

SUB = 16
ROW_TILE = 256
TOK_TILE = 256
VMEM_LIMIT = 56 * 1024 * 1024


def _cparams(sem):
    return pltpu.CompilerParams(dimension_semantics=sem, vmem_limit_bytes=VMEM_LIMIT)


def _silu(x):
    return x * jax.nn.sigmoid(x)


def _mod_spec(per_token, tm, tiles_per_seq, col, last_tile=None):
    tile = (lambda i: i) if last_tile is None else (lambda i: jnp.minimum(i, last_tile))
    if per_token:
        return pl.BlockSpec((tm, D_MODEL), lambda i: (tile(i), col))
    return pl.BlockSpec((None, 1, D_MODEL), lambda i: (tile(i) // tiles_per_seq, 0, col))


def _ada_kernel(c_ref, w_ref, b_ref, o_ref):
    c = c_ref[...]
    o_ref[...] = jnp.dot(_silu(c).astype(BF16), w_ref[...].astype(BF16),
                         preferred_element_type=F32) + b_ref[...]


def _ada(c_all, w_ada, b_ada):
    rows = c_all.shape[0]
    n = w_ada.shape[1]
    return pl.pallas_call(
        _ada_kernel,
        grid=(n // D_MODEL,),
        in_specs=[pl.BlockSpec((rows, D_MODEL), lambda j: (0, 0)),
                  pl.BlockSpec((D_MODEL, D_MODEL), lambda j: (0, j)),
                  pl.BlockSpec((1, D_MODEL), lambda j: (0, j))],
        out_specs=pl.BlockSpec((rows, D_MODEL), lambda j: (0, j)),
        out_shape=jax.ShapeDtypeStruct((rows, n), F32),
        compiler_params=_cparams(("arbitrary",)),
        name="ada",
    )(c_all, w_ada, b_ada.reshape(1, n))


def _inproj_kernel(x_ref, sh_ref, sc_ref, n1_ref, w_ref, u_ref):
    x = x_ref[...]
    h = x * lax.rsqrt(jnp.mean(x * x, axis=-1, keepdims=True) + EPS) * n1_ref[...]
    h = h * (1.0 + sc_ref[...]) + sh_ref[...]
    u_ref[...] = jnp.dot(h.astype(BF16), w_ref[...], preferred_element_type=F32)


def _inproj(x, mod, per_token, seq, norm1, w_in_bf):
    n = x.shape[0]
    tm = min(512, n)
    tps = max(seq // tm, 1)
    return pl.pallas_call(
        _inproj_kernel,
        grid=(n // tm,),
        in_specs=[pl.BlockSpec((tm, D_MODEL), lambda i: (i, 0)),
                  _mod_spec(per_token, tm, tps, 0),
                  _mod_spec(per_token, tm, tps, 1),
                  pl.BlockSpec((1, D_MODEL), lambda i: (0, 0)),
                  pl.BlockSpec((D_MODEL, D_IN), lambda i: (0, 0))],
        out_specs=pl.BlockSpec((tm, D_IN), lambda i: (i, 0)),
        out_shape=jax.ShapeDtypeStruct((n, D_IN), F32),
        compiler_params=_cparams(("arbitrary",)),
        name="inproj",
    )(x, mod, mod, norm1.reshape(1, D_MODEL), w_in_bf)


def _block_select():
    r = lax.broadcasted_iota(I32, (SUB, SUB * DK), 0)
    c = lax.broadcasted_iota(I32, (SUB, SUB * DK), 1) >> (DK.bit_length() - 1)
    return (r == c).astype(BF16)


def _hgrn_block(q, fp, iv, g, lb, hn, st_ref, sel_ref, valid):
    f = lb + (1.0 - lb) * jax.nn.sigmoid(fp)
    logf = jnp.log(f)
    kk = 1.0 - f
    if valid < SUB:
        live = lax.broadcasted_iota(I32, (SUB, D_A), 0) < valid
        logf = jnp.where(live, logf, 0.0)
        kk = jnp.where(live, kk, 0.0)
    qa = _silu(q)

    r = lax.broadcasted_iota(I32, (SUB, SUB), 0)
    c = lax.broadcasted_iota(I32, (SUB, SUB), 1)
    tri = (r >= c).astype(BF16)
    hi = logf.astype(BF16)
    lo = (logf - hi.astype(F32)).astype(BF16)
    b = (jnp.dot(tri, hi, preferred_element_type=F32)
         + jnp.dot(tri, lo, preferred_element_type=F32))
    bend = b[SUB - 1:SUB, :]
    qt = (qa * jnp.exp(b)).astype(BF16)
    kh = (kk * jnp.exp(bend - b)).astype(BF16)
    dec = jnp.exp(bend)
    ivb = iv.astype(BF16)

    rows = []
    for h in range(H_A):
        sl = slice(DK * h, DK * (h + 1))
        bh, qh, kkh = b[:, sl], qa[:, sl], kk[:, sl]
        ps = []
        for s in range(SUB):
            e = jnp.exp(jnp.minimum(bh - bh[s:s + 1, :], 0.0))
            ps.append((e * (qh * kkh[s:s + 1, :])).astype(BF16))
        rows.append(jnp.concatenate(ps, axis=1))
    pcat = jnp.concatenate(rows, axis=0)
    a = lax.dot_general(pcat, sel_ref[...], (((1,), (1,)), ((), ())),
                        preferred_element_type=F32)
    tr = lax.broadcasted_iota(I32, (H_A * SUB, SUB), 0) & (SUB - 1)
    tc = lax.broadcasted_iota(I32, (H_A * SUB, SUB), 1)
    a = jnp.where(tr >= tc, a, 0.0).astype(BF16)

    outs = []
    for h in range(H_A):
        sl = slice(DK * h, DK * (h + 1))
        st = st_ref[h]
        o = lax.dot_general(qt[:, sl], st.astype(BF16), (((1,), (1,)), ((), ())),
                            preferred_element_type=F32)
        o = o + jnp.dot(a[SUB * h:SUB * (h + 1), :], ivb[:, sl], preferred_element_type=F32)
        ut = lax.dot_general(ivb[:, sl], kh[:, sl], (((0,), (0,)), ((), ())),
                             preferred_element_type=F32)
        st_ref[h] = st * dec[:, sl] + ut
        o = o * lax.rsqrt(jnp.mean(o * o, axis=-1, keepdims=True) + EPS) * hn[:, sl]
        outs.append(o)
    return jnp.concatenate(outs, axis=1) * _silu(g)


def _hgrn_prompt_kernel(q_ref, f_ref, i_ref, g_ref, lb_ref, hn_ref, o_ref, s_ref, st_ref, sel_ref,
                        *, tt):
    t = pl.program_id(1)

    @pl.when(t == 0)
    def _():
        st_ref[...] = jnp.zeros_like(st_ref)
        sel_ref[...] = _block_select()

    lb = lb_ref[...]
    hn = hn_ref[...]

    def body(j, carry):
        rs = pl.ds(pl.multiple_of(j * SUB, SUB), SUB)
        o_ref[rs, :] = _hgrn_block(q_ref[rs, :], f_ref[rs, :], i_ref[rs, :], g_ref[rs, :],
                                   lb, hn, st_ref, sel_ref, SUB)
        return carry

    lax.fori_loop(0, tt // SUB, body, 0)

    @pl.when(t == pl.num_programs(1) - 1)
    def _():
        for h in range(H_A):
            s_ref[h] = st_ref[h].T


def _hgrn_prompt(u, batch, seq, lb, hn):
    tt = 256
    nt = seq // tt

    def col(cb):
        return pl.BlockSpec((tt, D_A), lambda b, t: (b * nt + t, cb))

    return pl.pallas_call(
        functools.partial(_hgrn_prompt_kernel, tt=tt),
        grid=(batch, nt),
        in_specs=[col(0), col(1), col(2), col(3),
                  pl.BlockSpec((1, D_A), lambda b, t: (0, 0)),
                  pl.BlockSpec((1, D_A), lambda b, t: (0, 0))],
        out_specs=[pl.BlockSpec((tt, D_A), lambda b, t: (b * nt + t, 0)),
                   pl.BlockSpec((None, H_A, DK, DV), lambda b, t: (b, 0, 0, 0))],
        out_shape=[jax.ShapeDtypeStruct((batch * seq, D_A), F32),
                   jax.ShapeDtypeStruct((batch, H_A, DK, DV), F32)],
        scratch_shapes=[pltpu.VMEM((H_A, DV, DK), F32), pltpu.VMEM((SUB, SUB * DK), BF16)],
        compiler_params=_cparams(("arbitrary", "arbitrary")),
        name="hgrn_prompt",
    )(u, u, u, u, lb, hn)


def _hgrn_sample_kernel(q_ref, f_ref, i_ref, g_ref, lb_ref, hn_ref, s0_ref, o_ref, s_ref,
                        st_ref, sel_ref, pad_ref, *, seq):
    for h in range(H_A):
        st_ref[h] = s0_ref[h].T
    sel_ref[...] = _block_select()
    pad_ref[...] = jnp.zeros_like(pad_ref)
    for n, ref in enumerate((q_ref, f_ref, i_ref, g_ref)):
        pad_ref[n, 0:seq, :] = ref[...]
    o = _hgrn_block(pad_ref[0], pad_ref[1], pad_ref[2], pad_ref[3], lb_ref[...], hn_ref[...],
                    st_ref, sel_ref, seq)
    o_ref[...] = o[0:seq, :]
    for h in range(H_A):
        s_ref[h] = st_ref[h].T


def _hgrn_sample(u3, s0, lb, hn):
    batch, seq, _ = u3.shape

    def col(cb):
        return pl.BlockSpec((None, seq, D_A), lambda b: (b, 0, cb))

    return pl.pallas_call(
        functools.partial(_hgrn_sample_kernel, seq=seq),
        grid=(batch,),
        in_specs=[col(0), col(1), col(2), col(3),
                  pl.BlockSpec((1, D_A), lambda b: (0, 0)),
                  pl.BlockSpec((1, D_A), lambda b: (0, 0)),
                  pl.BlockSpec((None, H_A, DK, DV), lambda b: (b, 0, 0, 0))],
        out_specs=[pl.BlockSpec((None, seq, D_A), lambda b: (b, 0, 0)),
                   pl.BlockSpec((None, H_A, DK, DV), lambda b: (b, 0, 0, 0))],
        out_shape=[jax.ShapeDtypeStruct((batch, seq, D_A), F32),
                   jax.ShapeDtypeStruct((batch, H_A, DK, DV), F32)],
        scratch_shapes=[pltpu.VMEM((H_A, DV, DK), F32), pltpu.VMEM((SUB, SUB * DK), BF16),
                        pltpu.VMEM((4, SUB, D_A), F32)],
        compiler_params=_cparams(("arbitrary",)),
        name="hgrn_sample",
    )(u3, u3, u3, u3, lb, hn, s0)


HGRN_SEQS = 8
HALF = SUB // 2


def _hgrn_step(q, fp, iv, g, lb, hn, st_ref, sel_ref, cm_ref, b_ref, k_ref, valid):
    nrow = HGRN_SEQS * SUB
    f = lb + (1.0 - lb) * jax.nn.sigmoid(fp)
    lf = jnp.log2(f)
    kk = 1.0 - f
    if valid < SUB:
        live = (lax.broadcasted_iota(I32, (nrow, D_A), 0) & (SUB - 1)) < valid
        lf = jnp.where(live, lf, 0.0)
        kk = jnp.where(live, kk, 0.0)
    qa = _silu(q)

    hi = lf.astype(BF16)
    lo = (lf - hi.astype(F32)).astype(BF16)
    cum, tot = cm_ref[0], cm_ref[1]
    b = jnp.dot(cum, hi, preferred_element_type=F32) + jnp.dot(cum, lo, preferred_element_type=F32)
    be = jnp.dot(tot, hi, preferred_element_type=F32) + jnp.dot(tot, lo, preferred_element_type=F32)
    qt = (qa * jnp.exp2(b)).astype(BF16)
    kh = (kk * jnp.exp2(be - b)).astype(BF16)
    dec = jnp.exp2(be)
    ivb = iv.astype(BF16)
    for h in range(H_A):
        b_ref[h] = b[:, DK * h:DK * (h + 1)]
        k_ref[h] = kk[:, DK * h:DK * (h + 1)]

    zero_half = jnp.zeros((HALF, D_A), F32)
    tr = lax.broadcasted_iota(I32, (H_A * SUB, SUB), 0) & (SUB - 1)
    tc = lax.broadcasted_iota(I32, (H_A * SUB, SUB), 1)
    sg = _silu(g)
    inter = []
    for n in range(HGRN_SEQS):
        rows = slice(n * SUB, (n + 1) * SUB)
        for h in range(H_A):
            sl = slice(DK * h, DK * (h + 1))
            st = st_ref[n, h]
            inter.append(lax.dot_general(qt[rows, sl], st.astype(BF16), (((1,), (1,)), ((), ())),
                                         preferred_element_type=F32))
            ut = lax.dot_general(ivb[rows, sl], kh[rows, sl], (((0,), (0,)), ((), ())),
                                 preferred_element_type=F32)
            dtile = dec[n * SUB:n * SUB + HALF, sl]
            st_ref[n, h] = (st.reshape(DV // HALF, HALF, DK) * dtile[None]).reshape(DV, DK) + ut

    scores = []
    for n in range(HGRN_SEQS):
        per_s = []
        for s in range(SUB):
            row = n * SUB + s
            b_s = jnp.concatenate([b_ref[h, pl.ds(row, HALF, stride=0), :] for h in range(H_A)], axis=1)
            k_s = jnp.concatenate([k_ref[h, pl.ds(row, HALF, stride=0), :] for h in range(H_A)], axis=1)
            halves = []
            for hf in range(2):
                if hf == 0 and s >= HALF:
                    halves.append(zero_half)
                    continue
                r0 = n * SUB + hf * HALF
                e = jnp.exp2(jnp.minimum(b[r0:r0 + HALF] - b_s, 0.0))
                halves.append(e * (qa[r0:r0 + HALF] * k_s))
            per_s.append(jnp.concatenate(halves, axis=0).astype(BF16))
        pcat = jnp.concatenate(
            [jnp.concatenate([p[:, DK * h:DK * (h + 1)] for p in per_s], axis=1) for h in range(H_A)],
            axis=0)
        a = lax.dot_general(pcat, sel_ref[...], (((1,), (1,)), ((), ())),
                            preferred_element_type=F32)
        scores.append(jnp.where(tr >= tc, a, 0.0).astype(BF16))

    outs = []
    for n in range(HGRN_SEQS):
        rows = slice(n * SUB, (n + 1) * SUB)
        heads = []
        for h in range(H_A):
            sl = slice(DK * h, DK * (h + 1))
            o = inter[n * H_A + h] + jnp.dot(scores[n][h * SUB:(h + 1) * SUB, :], ivb[rows, sl],
                                             preferred_element_type=F32)
            o = o * lax.rsqrt(jnp.mean(o * o, axis=-1, keepdims=True) + EPS) * hn[:, sl]
            heads.append(o)
        outs.append(jnp.concatenate(heads, axis=1) * sg[rows])
    return jnp.concatenate(outs, axis=0)


def _hgrn_kernel(*refs, rows, has_s0):
    if has_s0:
        (q_ref, f_ref, i_ref, g_ref, lb_ref, hn_ref, s0_ref, o_ref, s_ref,
         st_ref, sel_ref, cm_ref, b_ref, k_ref, pad_ref) = refs
    else:
        (q_ref, f_ref, i_ref, g_ref, lb_ref, hn_ref, o_ref, s_ref,
         st_ref, sel_ref, cm_ref, b_ref, k_ref) = refs
    t = pl.program_id(1)
    nrow = HGRN_SEQS * SUB

    @pl.when((t == 0) & (pl.program_id(0) == 0))
    def _():
        sel_ref[...] = _block_select()
        r = lax.broadcasted_iota(I32, (nrow, nrow), 0)
        c = lax.broadcasted_iota(I32, (nrow, nrow), 1)
        shift = SUB.bit_length() - 1
        same = (r >> shift) == (c >> shift)
        cm_ref[0] = (same & (r >= c)).astype(BF16)
        cm_ref[1] = same.astype(BF16)

    lb = lb_ref[...]
    hn = hn_ref[...]
    if has_s0:
        for n in range(HGRN_SEQS):
            for h in range(H_A):
                st_ref[n, h] = s0_ref[n, h].T
        pad_ref[...] = jnp.zeros_like(pad_ref)
        for j, ref in enumerate((q_ref, f_ref, i_ref, g_ref)):
            for n in range(HGRN_SEQS):
                pad_ref[j, n, 0:rows, :] = ref[n * rows:(n + 1) * rows, :]
        o = _hgrn_step(*(pad_ref[j].reshape(nrow, D_A) for j in range(4)), lb, hn,
                       st_ref, sel_ref, cm_ref, b_ref, k_ref, rows)
        for n in range(HGRN_SEQS):
            o_ref[n * rows:(n + 1) * rows, :] = o[n * SUB:n * SUB + rows, :]
    else:
        @pl.when(t == 0)
        def _():
            st_ref[...] = jnp.zeros_like(st_ref)

        def body(j, carry):
            rs = pl.ds(pl.multiple_of(j * SUB, SUB), SUB)
            o = _hgrn_step(*(ref[:, rs, :].reshape(nrow, D_A) for ref in (q_ref, f_ref, i_ref, g_ref)),
                           lb, hn, st_ref, sel_ref, cm_ref, b_ref, k_ref, SUB)
            o_ref[:, rs, :] = o.reshape(HGRN_SEQS, SUB, D_A)
            return carry

        lax.fori_loop(0, rows // SUB, body, 0)

    def emit_state():
        for n in range(HGRN_SEQS):
            for h in range(H_A):
                s_ref[n, h] = st_ref[n, h].T

    if has_s0:
        emit_state()
    else:
        pl.when(t == pl.num_programs(1) - 1)(emit_state)


def _hgrn(u3, s0, lb, hn, seq=None):
    has_s0 = s0 is not None
    if has_s0:
        batch = u3.shape[0] // seq
        rows = seq
        grid = (batch // HGRN_SEQS, 1)
        blk = lambda cb: pl.BlockSpec((HGRN_SEQS * rows, D_A), lambda b, t: (b, cb))
        o_shape = (batch * seq, D_A)
    else:
        batch, seq, _ = u3.shape
        rows = 128
        grid = (batch // HGRN_SEQS, seq // rows)
        blk = lambda cb: pl.BlockSpec((HGRN_SEQS, rows, D_A), lambda b, t: (b, t, cb))
        o_shape = (batch, seq, D_A)
    sblk = pl.BlockSpec((HGRN_SEQS, H_A, DK, DV), lambda b, t: (b, 0, 0, 0))
    vec = pl.BlockSpec((1, D_A), lambda b, t: (0, 0))
    nrow = HGRN_SEQS * SUB
    scratch = [pltpu.VMEM((HGRN_SEQS, H_A, DV, DK), F32), pltpu.VMEM((SUB, SUB * DK), BF16),
               pltpu.VMEM((2, nrow, nrow), BF16), pltpu.VMEM((H_A, nrow, DK), F32),
               pltpu.VMEM((H_A, nrow, DK), F32)]
    if has_s0:
        scratch.append(pltpu.VMEM((4, HGRN_SEQS, SUB, D_A), F32))
    args = (u3, u3, u3, u3, lb, hn) + ((s0,) if has_s0 else ())
    return pl.pallas_call(
        functools.partial(_hgrn_kernel, rows=rows, has_s0=has_s0),
        grid=grid,
        in_specs=[blk(0), blk(1), blk(2), blk(3), vec, vec] + ([sblk] if has_s0 else []),
        out_specs=[blk(0), sblk],
        out_shape=[jax.ShapeDtypeStruct(o_shape, F32),
                   jax.ShapeDtypeStruct((batch, H_A, DK, DV), F32)],
        scratch_shapes=scratch,
        compiler_params=_cparams(("arbitrary", "arbitrary")),
        name="hgrn_state" if has_s0 else "hgrn_scan",
    )(*args)


def _pool_groups(full_ref, base, tm, cnt_fn, wp_ref, ps_ref, step=1):
    outs = []
    for gi, w in enumerate(POOL_WINDOWS):
        ch = slice(POOL_GROUP * gi, POOL_GROUP * (gi + 1))
        v = full_ref[base:base + tm, ch]
        s = v
        for j in range(1, w):
            s = s + full_ref[base - j * step:base - j * step + tm, ch]
        d = s / cnt_fn(w) - v
        outs.append(jnp.dot(d.astype(BF16), wp_ref[gi], preferred_element_type=F32))
    return jnp.concatenate(outs, axis=1) * ps_ref[...]


def _pool_prompt_kernel(v_ref, prev_ref, wp_ref, ps_ref, o_ref, hist_ref, full_ref, *, tm):
    t = pl.program_id(1)
    prev = prev_ref[...]
    full_ref[0:16, :] = jnp.where(t == 0, 0.0, prev)
    full_ref[16:16 + tm, :] = v_ref[...]
    pos = (lax.broadcasted_iota(I32, (tm, 1), 0) + t * tm + 1).astype(F32)
    o_ref[...] = _pool_groups(full_ref, 16, tm, lambda w: jnp.minimum(pos, float(w)),
                              wp_ref, ps_ref)

    @pl.when(t == pl.num_programs(1) - 1)
    def _():
        hist_ref[...] = full_ref[16 + tm - POOL_HIST:16 + tm, :]


def _pool_prompt(u, batch, seq, w_pool_bf, pool_scale):
    tm = min(1024, seq)
    nt = seq // tm
    vcol = (4 * D_A) // D_B
    return pl.pallas_call(
        functools.partial(_pool_prompt_kernel, tm=tm),
        grid=(batch, nt),
        in_specs=[pl.BlockSpec((tm, D_B), lambda b, t: (b * nt + t, vcol)),
                  pl.BlockSpec((16, D_B),
                               lambda b, t: (jnp.maximum((b * nt + t) * (tm // 16) - 1, 0), vcol)),
                  pl.BlockSpec((len(POOL_WINDOWS), POOL_GROUP, POOL_GROUP), lambda b, t: (0, 0, 0)),
                  pl.BlockSpec((1, D_B), lambda b, t: (0, 0))],
        out_specs=[pl.BlockSpec((tm, D_B), lambda b, t: (b * nt + t, 0)),
                   pl.BlockSpec((None, POOL_HIST, D_B), lambda b, t: (b, 0, 0))],
        out_shape=[jax.ShapeDtypeStruct((batch * seq, D_B), F32),
                   jax.ShapeDtypeStruct((batch, POOL_HIST, D_B), F32)],
        scratch_shapes=[pltpu.VMEM((16 + tm, D_B), F32)],
        compiler_params=_cparams(("arbitrary", "arbitrary")),
        name="pool_prompt",
    )(u, u, w_pool_bf, pool_scale)


def _pool_sample_kernel(v_ref, hist_ref, wp_ref, ps_ref, o_ref, nh_ref, full_ref, *,
                        seq, batch, start):
    full_ref[0:batch, :] = jnp.zeros((batch, D_B), F32)
    full_ref[batch:16 * batch, :] = hist_ref[...].reshape(POOL_HIST * batch, D_B)
    full_ref[16 * batch:(16 + seq) * batch, :] = v_ref[...].reshape(seq * batch, D_B)

    def cnt(w):
        rows = [jnp.full((batch, 1), float(min(start + t + 1, w)), F32) for t in range(seq)]
        return jnp.concatenate(rows, axis=0)

    o = _pool_groups(full_ref, 16 * batch, seq * batch, cnt, wp_ref, ps_ref, step=batch)
    o_ref[...] = o.reshape(seq, batch, D_B)
    nh_ref[...] = full_ref[(16 + seq - POOL_HIST) * batch:(16 + seq) * batch, :].reshape(
        POOL_HIST, batch, D_B)


def _pool_sample(u_t, hist_t, start, w_pool_bf, pool_scale):
    seq, batch, _ = u_t.shape
    vcol = (4 * D_A) // D_B
    return pl.pallas_call(
        functools.partial(_pool_sample_kernel, seq=seq, batch=batch, start=start),
        grid=(1,),
        in_specs=[pl.BlockSpec((seq, batch, D_B), lambda i: (0, 0, vcol)),
                  pl.BlockSpec((POOL_HIST, batch, D_B), lambda i: (0, 0, 0)),
                  pl.BlockSpec((len(POOL_WINDOWS), POOL_GROUP, POOL_GROUP), lambda i: (0, 0, 0)),
                  pl.BlockSpec((1, D_B), lambda i: (0, 0))],
        out_specs=[pl.BlockSpec((seq, batch, D_B), lambda i: (0, 0, 0)),
                   pl.BlockSpec((POOL_HIST, batch, D_B), lambda i: (0, 0, 0))],
        out_shape=[jax.ShapeDtypeStruct((seq, batch, D_B), F32),
                   jax.ShapeDtypeStruct((POOL_HIST, batch, D_B), F32)],
        scratch_shapes=[pltpu.VMEM(((16 + seq) * batch, D_B), F32)],
        compiler_params=_cparams(("arbitrary",)),
        name="pool_sample",
    )(u_t, hist_t, w_pool_bf, pool_scale)


LANES = 128
SLAB = D_MODEL // LANES


def _store_slabs(ref, value):
    n = value.shape[0]
    for j in range(SLAB):
        ref[pl.ds(j, n, stride=SLAB), :] = value[:, LANES * j:LANES * (j + 1)]


def _load_slabs(ref, n):
    return [ref[pl.ds(j, n, stride=SLAB), :] for j in range(SLAB)]


def _slab_rows(row):
    return pl.ds(pl.multiple_of(row * SLAB, SLAB), SLAB)


def _split_dot(a, w_hi, w_lo):
    a_hi = a.astype(BF16)
    a_lo = (a - a_hi.astype(F32)).astype(BF16)
    return (jnp.dot(a_hi, w_hi, preferred_element_type=F32)
            + jnp.dot(a_hi, w_lo, preferred_element_type=F32)
            + jnp.dot(a_lo, w_hi, preferred_element_type=F32))


def _outproj_kernel(x_ref, oa_ref, ob_ref, g1_ref, sh2_ref, sc2_ref, n2_ref, woa_ref, wob_ref,
                    wrh_ref, wrl_ref, br_ref, *rest):
    x1_ref, h2_ref, idx_ref, prob_ref, lg_ref = rest[-5:]
    i = pl.program_id(0)
    slot = i % 2

    @pl.when(i == 0)
    def _():
        lg_ref[...] = jnp.zeros_like(lg_ref)

    logits = lg_ref[1 - slot]

    mix = (jnp.dot(oa_ref[...].astype(BF16), woa_ref[...], preferred_element_type=F32)
           + jnp.dot(ob_ref[...].astype(BF16), wob_ref[...], preferred_element_type=F32))
    x1 = x_ref[...] + g1_ref[...] * mix
    x1_ref[...] = x1
    h2 = x1 * lax.rsqrt(jnp.mean(x1 * x1, axis=-1, keepdims=True) + EPS) * n2_ref[...]
    h2 = h2 * (1.0 + sc2_ref[...]) + sh2_ref[...]
    _store_slabs(h2_ref, h2)
    lg_ref[slot] = jnp.dot(h2.astype(BF16), wrh_ref[...], preferred_element_type=F32) + br_ref[...]

    tm = logits.shape[0]
    lane = lax.broadcasted_iota(I32, (tm, N_EXPERTS), 1)
    kcol = lax.broadcasted_iota(I32, (tm, TOP_K), 1)
    vals = jnp.zeros((tm, TOP_K), F32)
    idxs = jnp.zeros((tm, TOP_K), I32)
    for k in range(TOP_K):
        m = jnp.max(logits, axis=-1, keepdims=True)
        am = jnp.min(jnp.where(logits == m, lane, N_EXPERTS), axis=-1, keepdims=True)
        vals = jnp.where(kcol == k, m, vals)
        idxs = jnp.where(kcol == k, am, idxs)
        logits = jnp.where(lane == am, -jnp.inf, logits)
    e = jnp.exp(vals - vals[:, 0:1])
    prob_ref[...] = e / jnp.sum(e, axis=-1, keepdims=True)
    idx_ref[...] = idxs


def _outproj(x, o_a, o_b, mod, per_token, seq, norm2, w_out_bf, wr_hi, wr_lo, b_router,
             n_total, tile0, joint=None):
    n = x.shape[0]
    tm = min(256, n)
    tps = max(seq // tm, 1)
    nt = n // tm
    cur = lambda i: jnp.minimum(i, nt - 1)
    prv = lambda i: jnp.maximum(i - 1, 0)
    row = lambda width: pl.BlockSpec((tm, width), lambda i: (cur(i), 0))
    jrow = lambda width: pl.BlockSpec((tm, width), lambda i: (prv(i) + tile0, 0))
    const = lambda shape: pl.BlockSpec(shape, lambda i: (0,) * len(shape))
    joint = () if joint is None else tuple(joint)
    n_in = 12
    return pl.pallas_call(
        _outproj_kernel,
        grid=(nt + 1,),
        in_specs=[row(D_MODEL), row(D_A), row(D_B),
                  _mod_spec(per_token, tm, tps, 2, nt - 1),
                  _mod_spec(per_token, tm, tps, 3, nt - 1),
                  _mod_spec(per_token, tm, tps, 4, nt - 1),
                  const((1, D_MODEL)),
                  pl.BlockSpec((D_A, D_MODEL), lambda i: (0, 0)),
                  pl.BlockSpec((D_B, D_MODEL), lambda i: (1, 0)),
                  const((D_MODEL, N_EXPERTS)), const((D_MODEL, N_EXPERTS)), const((1, N_EXPERTS))]
                 + [pl.BlockSpec(memory_space=pl.ANY)] * len(joint),
        out_specs=[row(D_MODEL), pl.BlockSpec((tm * SLAB, LANES), lambda i: (cur(i) + tile0, 0)),
                   jrow(TOP_K), jrow(TOP_K)],
        out_shape=[jax.ShapeDtypeStruct((n, D_MODEL), F32),
                   jax.ShapeDtypeStruct((n_total * SLAB, LANES), F32),
                   jax.ShapeDtypeStruct((n_total, TOP_K), I32),
                   jax.ShapeDtypeStruct((n_total, TOP_K), F32)],
        input_output_aliases={n_in + j: 1 + j for j in range(len(joint))},
        scratch_shapes=[pltpu.VMEM((2, tm, N_EXPERTS), F32)],
        compiler_params=_cparams(("arbitrary",)),
        name="outproj",
    )(x, o_a, o_b, mod, mod, mod, norm2.reshape(1, D_MODEL), w_out_bf, w_out_bf,
      wr_hi, wr_lo, b_router.reshape(1, N_EXPERTS), *joint)


def _rank_kernel(idx_ref, dest_ref, cnt_ref, carry_ref, *, te):
    phase = pl.program_id(0)
    i = pl.program_id(1)
    idx = idx_ref[...]
    lane = lax.broadcasted_iota(I32, (te, N_EXPERTS), 1)
    onehots = [(idx[:, k:k + 1] == lane) for k in range(TOP_K)]
    member = jnp.zeros((te, N_EXPERTS), F32)
    for oh in onehots:
        member = member + oh.astype(F32)
    colsum = jnp.sum(member, axis=0, keepdims=True)

    @pl.when((phase == 0) & (i == 0))
    def _():
        carry_ref[...] = jnp.zeros_like(carry_ref)

    @pl.when(phase == 0)
    def _():
        carry_ref[...] += colsum

    @pl.when((phase == 1) & (i == 0))
    def _():
        cnt = carry_ref[...]
        cnt_ref[...] = cnt.astype(I32)
        tiles = jnp.floor((cnt + (ROW_TILE - 1)) * (1.0 / ROW_TILE))
        r = lax.broadcasted_iota(I32, (N_EXPERTS, N_EXPERTS), 0)
        c = lax.broadcasted_iota(I32, (N_EXPERTS, N_EXPERTS), 1)
        before = (r < c).astype(BF16)
        carry_ref[...] = ROW_TILE * jnp.dot(tiles.astype(BF16), before, preferred_element_type=F32)

    @pl.when(phase == 1)
    def _():
        r = lax.broadcasted_iota(I32, (te, te), 0)
        c = lax.broadcasted_iota(I32, (te, te), 1)
        earlier = (r > c).astype(BF16)
        base = carry_ref[...] + jnp.dot(earlier, member.astype(BF16), preferred_element_type=F32)
        kcol = lax.broadcasted_iota(I32, (te, TOP_K), 1)
        dest = jnp.zeros((te, TOP_K), F32)
        for k, oh in enumerate(onehots):
            dk = jnp.sum(jnp.where(oh, base, 0.0), axis=-1, keepdims=True)
            dest = jnp.where(kcol == k, dk, dest)
        dest_ref[...] = dest.astype(I32)
        carry_ref[...] += colsum


def _rank(top_idx):
    n = top_idx.shape[0]
    te = 512
    return pl.pallas_call(
        functools.partial(_rank_kernel, te=te),
        grid=(2, n // te),
        in_specs=[pl.BlockSpec((te, TOP_K), lambda p, i: (i, 0))],
        out_specs=[pl.BlockSpec((te, TOP_K), lambda p, i: (p * i, 0)),
                   pl.BlockSpec((1, N_EXPERTS), lambda p, i: (0, 0))],
        out_shape=[jax.ShapeDtypeStruct((n, TOP_K), I32),
                   jax.ShapeDtypeStruct((1, N_EXPERTS), I32)],
        scratch_shapes=[pltpu.VMEM((1, N_EXPERTS), F32)],
        compiler_params=_cparams(("arbitrary", "arbitrary")),
        name="rank",
    )(top_idx)


def _dispatch_kernel(last_ref, ntile_ref, used_ref, dest_ref, h_ref, xs_ref, zero_ref, sem, zsem,
                     *, n_tiles):
    i = pl.program_id(0)

    @pl.when(i == 0)
    def _():
        zero_ref[...] = jnp.zeros_like(zero_ref)

        def clear(tile):
            rows = pl.ds(pl.multiple_of(tile * (ROW_TILE * SLAB), ROW_TILE * SLAB), ROW_TILE * SLAB)
            return pltpu.make_async_copy(zero_ref, xs_ref.at[rows], zsem)

        def over_tail(fn):
            def body(t, carry):
                fn(clear(t))
                return carry
            lax.fori_loop(used_ref[0], n_tiles, body, 0)

        for fn in (lambda c: c.start(), lambda c: c.wait()):
            for e in range(N_EXPERTS):
                @pl.when(ntile_ref[e] > 0)
                def _():
                    fn(clear(last_ref[e]))
            over_tail(fn)

    def body(r, carry):
        for k in range(TOP_K):
            d = dest_ref[r * TOP_K + k]
            pltpu.make_async_copy(h_ref.at[_slab_rows(r)], xs_ref.at[_slab_rows(d)], sem).start(
                priority=k % 2)
        return carry

    lax.fori_loop(0, TOK_TILE, body, 0, unroll=2)
    for k in range(TOP_K):
        pltpu.make_async_copy(h_ref, xs_ref.at[pl.ds(0, TOK_TILE * SLAB)], sem).wait()


def _dispatch(h2, dest_flat, last_tile, ntiles, used, n_tiles):
    n = h2.shape[0] // SLAB
    grid_spec = pltpu.PrefetchScalarGridSpec(
        num_scalar_prefetch=3,
        grid=(n // TOK_TILE,),
        in_specs=[pl.BlockSpec((TOK_TILE * TOP_K,), lambda i, *_: (i,), memory_space=pltpu.SMEM),
                  pl.BlockSpec((TOK_TILE * SLAB, LANES), lambda i, *_: (i, 0))],
        out_specs=pl.BlockSpec(memory_space=pl.ANY),
        scratch_shapes=[pltpu.VMEM((ROW_TILE * SLAB, LANES), F32),
                        pltpu.SemaphoreType.DMA(()), pltpu.SemaphoreType.DMA(())],
    )
    return pl.pallas_call(
        functools.partial(_dispatch_kernel, n_tiles=n_tiles),
        grid_spec=grid_spec,
        out_shape=jax.ShapeDtypeStruct((n_tiles * ROW_TILE * SLAB, LANES), F32),
        compiler_params=_cparams(("arbitrary",)),
        name="dispatch",
    )(last_tile, ntiles, used, dest_flat, h2)


def _experts_kernel(te_ref, used_ref, xs_ref, wgu_ref, bgu_ref, wd_ref, bd_ref, ys_ref,
                    wgu_bf, wd_bf, act_bf):
    i = pl.program_id(0)
    prev = te_ref[jnp.maximum(i - 1, 0)]
    fresh = (i == 0) | (te_ref[i] != prev)

    @pl.when(fresh)
    def _():
        wgu_bf[...] = wgu_ref[...].astype(BF16)
        wd_bf[...] = wd_ref[...].astype(BF16)

    @pl.when(i < used_ref[0])
    def _():
        x = jnp.concatenate([p.astype(BF16) for p in _load_slabs(xs_ref, ROW_TILE)], axis=1)
        cw = D_FF // FF_CHUNKS
        for c in range(FF_CHUNKS):
            gate = jnp.dot(x, wgu_bf[:, c * cw:(c + 1) * cw], preferred_element_type=F32)
            gate = jnp.minimum(gate + bgu_ref[:, c * cw:(c + 1) * cw], SWIGLU_LIMIT)
            up = jnp.dot(x, wgu_bf[:, D_FF + c * cw:D_FF + (c + 1) * cw], preferred_element_type=F32)
            up = jnp.clip(up + bgu_ref[:, D_FF + c * cw:D_FF + (c + 1) * cw], -SWIGLU_LIMIT, SWIGLU_LIMIT)
            act = gate * jax.nn.sigmoid(SWIGLU_ALPHA * gate) * (up + 1.0)
            act_bf[:, c * cw:(c + 1) * cw] = act.astype(BF16)
        ow = D_MODEL // FF_CHUNKS
        for c in range(FF_CHUNKS):
            out = jnp.dot(act_bf[...], wd_bf[:, c * ow:(c + 1) * ow], preferred_element_type=F32)
            out = out + bd_ref[:, c * ow:(c + 1) * ow]
            for j in range(ow // LANES):
                ys_ref[pl.ds(c * (ow // LANES) + j, ROW_TILE, stride=SLAB), :] = (
                    out[:, LANES * j:LANES * (j + 1)])

    @pl.when(i >= used_ref[0])
    def _():
        ys_ref[...] = jnp.zeros_like(ys_ref)


def _experts(xs, tile_expert, used, w_gu, b_gu, w_down, b_down):
    rows = xs.shape[0] // SLAB
    nt = rows // ROW_TILE

    def tile(i, te, used):
        return (jnp.minimum(i, used[0] - 1), 0)

    grid_spec = pltpu.PrefetchScalarGridSpec(
        num_scalar_prefetch=2,
        grid=(nt,),
        in_specs=[pl.BlockSpec((ROW_TILE * SLAB, LANES), tile),
                  pl.BlockSpec((None, D_MODEL, 2 * D_FF), lambda i, te, used: (te[i], 0, 0)),
                  pl.BlockSpec((None, 1, 2 * D_FF), lambda i, te, used: (te[i], 0, 0)),
                  pl.BlockSpec((None, D_FF, D_MODEL), lambda i, te, used: (te[i], 0, 0)),
                  pl.BlockSpec((None, 1, D_MODEL), lambda i, te, used: (te[i], 0, 0))],
        out_specs=pl.BlockSpec((ROW_TILE * SLAB, LANES), lambda i, te, used: (i, 0)),
        scratch_shapes=[pltpu.VMEM((D_MODEL, 2 * D_FF), BF16), pltpu.VMEM((D_FF, D_MODEL), BF16),
                        pltpu.VMEM((ROW_TILE, D_FF), BF16)],
    )
    return pl.pallas_call(
        _experts_kernel,
        grid_spec=grid_spec,
        out_shape=jax.ShapeDtypeStruct((rows * SLAB, LANES), F32),
        compiler_params=_cparams(("arbitrary",)),
        name="experts",
    )(tile_expert, used, xs, w_gu, b_gu.reshape(N_EXPERTS, 1, 2 * D_FF), w_down,
      b_down.reshape(N_EXPERTS, 1, D_MODEL))


X_RING = 3


def _mlp_kernel(te_ref, used_ref, grp_ref, nxt_ref, xs_ref, wgu_ref, bgu_ref, wd_ref, bd_ref, ys_ref,
                wgu_f, wd_f, wgu_bf, wd_bf, act_bf, xring, wsem, xsem):
    i = pl.program_id(0)
    used = used_ref[0]
    tile_rows = ROW_TILE * SLAB

    def weights(e, ws):
        return (pltpu.make_async_copy(wgu_ref.at[e], wgu_f.at[ws], wsem.at[ws, 0]),
                pltpu.make_async_copy(wd_ref.at[e], wd_f.at[ws], wsem.at[ws, 1]))

    def x_tile(t):
        t = jnp.asarray(t, I32)
        rows = pl.ds(pl.multiple_of(t * tile_rows, tile_rows), tile_rows)
        return pltpu.make_async_copy(xs_ref.at[rows], xring.at[t % X_RING], xsem.at[t % X_RING])

    @pl.when(i == 0)
    def _():
        for cp in weights(te_ref[0], 0):
            cp.start()
        for t in range(X_RING - 1):
            @pl.when(t < used)
            def _():
                x_tile(t).start()

    @pl.when(i + X_RING - 1 < used)
    def _():
        x_tile(i + X_RING - 1).start()

    ws = grp_ref[i] % 2
    fresh = (i == 0) | (te_ref[i] != te_ref[jnp.maximum(i - 1, 0)])

    @pl.when(fresh & (i < used))
    def _():
        for cp in weights(te_ref[i], ws):
            cp.wait()

        @pl.when(nxt_ref[i] >= 0)
        def _():
            for cp in weights(nxt_ref[i], 1 - ws):
                cp.start()

        wgu_bf[...] = wgu_f[ws].astype(BF16)
        wd_bf[...] = wd_f[ws].astype(BF16)

    @pl.when(i < used)
    def _():
        x_tile(i).wait()
        x = jnp.concatenate([p.astype(BF16) for p in _load_slabs(xring.at[i % X_RING], ROW_TILE)],
                            axis=1)
        cw = D_FF // FF_CHUNKS
        for c in range(FF_CHUNKS):
            gate = jnp.dot(x, wgu_bf[:, c * cw:(c + 1) * cw], preferred_element_type=F32)
            gate = jnp.minimum(gate + bgu_ref[:, c * cw:(c + 1) * cw], SWIGLU_LIMIT)
            up = jnp.dot(x, wgu_bf[:, D_FF + c * cw:D_FF + (c + 1) * cw], preferred_element_type=F32)
            up = jnp.clip(up + bgu_ref[:, D_FF + c * cw:D_FF + (c + 1) * cw], -SWIGLU_LIMIT, SWIGLU_LIMIT)
            act = gate * jax.nn.sigmoid(SWIGLU_ALPHA * gate) * (up + 1.0)
            act_bf[:, c * cw:(c + 1) * cw] = act.astype(BF16)
        ow = D_MODEL // FF_CHUNKS
        for c in range(FF_CHUNKS):
            out = jnp.dot(act_bf[...], wd_bf[:, c * ow:(c + 1) * ow], preferred_element_type=F32)
            out = out + bd_ref[:, c * ow:(c + 1) * ow]
            for j in range(ow // LANES):
                ys_ref[pl.ds(c * (ow // LANES) + j, ROW_TILE, stride=SLAB), :] = (
                    out[:, LANES * j:LANES * (j + 1)])

    @pl.when(i >= used)
    def _():
        ys_ref[...] = jnp.zeros_like(ys_ref)


def _mlp(xs, tile_expert, used, grp, nxt, w_gu, b_gu, w_down, b_down):
    rows = xs.shape[0] // SLAB
    nt = rows // ROW_TILE
    ex = lambda i, te, *_: (te[i], 0, 0)
    grid_spec = pltpu.PrefetchScalarGridSpec(
        num_scalar_prefetch=4,
        grid=(nt,),
        in_specs=[pl.BlockSpec(memory_space=pl.ANY),
                  pl.BlockSpec(memory_space=pl.ANY),
                  pl.BlockSpec((None, 1, 2 * D_FF), ex),
                  pl.BlockSpec(memory_space=pl.ANY),
                  pl.BlockSpec((None, 1, D_MODEL), ex)],
        out_specs=pl.BlockSpec((ROW_TILE * SLAB, LANES), lambda i, *_: (i, 0)),
        scratch_shapes=[pltpu.VMEM((2, D_MODEL, 2 * D_FF), F32), pltpu.VMEM((2, D_FF, D_MODEL), F32),
                        pltpu.VMEM((D_MODEL, 2 * D_FF), BF16), pltpu.VMEM((D_FF, D_MODEL), BF16),
                        pltpu.VMEM((ROW_TILE, D_FF), BF16),
                        pltpu.VMEM((X_RING, ROW_TILE * SLAB, LANES), F32),
                        pltpu.SemaphoreType.DMA((2, 2)), pltpu.SemaphoreType.DMA((X_RING,))],
    )
    return pl.pallas_call(
        _mlp_kernel,
        grid_spec=grid_spec,
        out_shape=jax.ShapeDtypeStruct((rows * SLAB, LANES), F32),
        compiler_params=_cparams(("arbitrary",)),
        name="experts",
    )(tile_expert, used, grp, nxt, xs, w_gu, b_gu.reshape(N_EXPERTS, 1, 2 * D_FF), w_down,
      b_down.reshape(N_EXPERTS, 1, D_MODEL))


def _mlp_loop_kernel(te_ref, used_ref, grp_ref, nxt_ref, xs_ref, wgu_ref, bgu_ref, wd_ref, bd_ref,
                     ys_ref, wgu_f, wd_f, wgu_bf, wd_bf, act_bf, xring, yring, wsem, xsem, ysem,
                     *, n_tiles):
    used = used_ref[0]
    tile_rows = ROW_TILE * SLAB

    def weights(e, ws):
        return (pltpu.make_async_copy(wgu_ref.at[e], wgu_f.at[ws], wsem.at[ws, 0]),
                pltpu.make_async_copy(wd_ref.at[e], wd_f.at[ws], wsem.at[ws, 1]))

    def tile_rows_of(t):
        return pl.ds(pl.multiple_of(t * tile_rows, tile_rows), tile_rows)

    def x_tile(t):
        t = jnp.asarray(t, I32)
        return pltpu.make_async_copy(xs_ref.at[tile_rows_of(t)], xring.at[t % X_RING],
                                     xsem.at[t % X_RING])

    def y_tile(t):
        return pltpu.make_async_copy(yring.at[t % 2], ys_ref.at[tile_rows_of(t)], ysem.at[t % 2])

    for cp in weights(te_ref[0], 0):
        cp.start()
    for t in range(X_RING - 1):
        @pl.when(t < used)
        def _():
            x_tile(t).start()

    def step(i, carry):
        @pl.when(i + X_RING - 1 < used)
        def _():
            x_tile(i + X_RING - 1).start()

        e = te_ref[i]
        ws = grp_ref[i] % 2
        fresh = (i == 0) | (e != te_ref[jnp.maximum(i - 1, 0)])

        @pl.when(fresh)
        def _():
            for cp in weights(e, ws):
                cp.wait()

            @pl.when(nxt_ref[i] >= 0)
            def _():
                for cp in weights(nxt_ref[i], 1 - ws):
                    cp.start()

            wgu_bf[...] = wgu_f[ws].astype(BF16)
            wd_bf[...] = wd_f[ws].astype(BF16)

        x_tile(i).wait()

        @pl.when(i >= 2)
        def _():
            y_tile(i - 2).wait()

        x = jnp.concatenate([p.astype(BF16) for p in _load_slabs(xring.at[i % X_RING], ROW_TILE)],
                            axis=1)
        bgu = bgu_ref[pl.ds(e, 1), :]
        bd = bd_ref[pl.ds(e, 1), :]
        cw = D_FF // FF_CHUNKS
        for c in range(FF_CHUNKS):
            gate = jnp.dot(x, wgu_bf[:, c * cw:(c + 1) * cw], preferred_element_type=F32)
            gate = jnp.minimum(gate + bgu[:, c * cw:(c + 1) * cw], SWIGLU_LIMIT)
            up = jnp.dot(x, wgu_bf[:, D_FF + c * cw:D_FF + (c + 1) * cw], preferred_element_type=F32)
            up = jnp.clip(up + bgu[:, D_FF + c * cw:D_FF + (c + 1) * cw], -SWIGLU_LIMIT, SWIGLU_LIMIT)
            act = gate * jax.nn.sigmoid(SWIGLU_ALPHA * gate) * (up + 1.0)
            act_bf[:, c * cw:(c + 1) * cw] = act.astype(BF16)
        ow = D_MODEL // FF_CHUNKS
        for c in range(FF_CHUNKS):
            out = jnp.dot(act_bf[...], wd_bf[:, c * ow:(c + 1) * ow], preferred_element_type=F32)
            out = out + bd[:, c * ow:(c + 1) * ow]
            for j in range(ow // LANES):
                yring[i % 2, pl.ds(c * (ow // LANES) + j, ROW_TILE, stride=SLAB), :] = (
                    out[:, LANES * j:LANES * (j + 1)])
        y_tile(i).start()
        return carry

    lax.fori_loop(0, used, step, 0)

    @pl.when(used >= 2)
    def _():
        y_tile(used - 2).wait()
    y_tile(used - 1).wait()

    yring[0] = jnp.zeros(yring.shape[1:], F32)

    def clear(fn):
        def body(t, carry):
            fn(pltpu.make_async_copy(yring.at[0], ys_ref.at[tile_rows_of(t)], ysem.at[0]))
            return carry
        lax.fori_loop(used, n_tiles, body, 0)

    clear(lambda cp: cp.start())
    clear(lambda cp: cp.wait())


def _mlp_loop(xs, tile_expert, used, grp, nxt, w_gu, b_gu, w_down, b_down):
    rows = xs.shape[0] // SLAB
    nt = rows // ROW_TILE
    whole = lambda shape: pl.BlockSpec(shape, lambda i, *_: (0,) * len(shape))
    grid_spec = pltpu.PrefetchScalarGridSpec(
        num_scalar_prefetch=4,
        grid=(1,),
        in_specs=[pl.BlockSpec(memory_space=pl.ANY),
                  pl.BlockSpec(memory_space=pl.ANY),
                  whole((N_EXPERTS, 2 * D_FF)),
                  pl.BlockSpec(memory_space=pl.ANY),
                  whole((N_EXPERTS, D_MODEL))],
        out_specs=pl.BlockSpec(memory_space=pl.ANY),
        scratch_shapes=[pltpu.VMEM((2, D_MODEL, 2 * D_FF), F32), pltpu.VMEM((2, D_FF, D_MODEL), F32),
                        pltpu.VMEM((D_MODEL, 2 * D_FF), BF16), pltpu.VMEM((D_FF, D_MODEL), BF16),
                        pltpu.VMEM((ROW_TILE, D_FF), BF16),
                        pltpu.VMEM((X_RING, ROW_TILE * SLAB, LANES), F32),
                        pltpu.VMEM((2, ROW_TILE * SLAB, LANES), F32),
                        pltpu.SemaphoreType.DMA((2, 2)), pltpu.SemaphoreType.DMA((X_RING,)),
                        pltpu.SemaphoreType.DMA((2,))],
    )
    return pl.pallas_call(
        functools.partial(_mlp_loop_kernel, n_tiles=nt),
        grid_spec=grid_spec,
        out_shape=jax.ShapeDtypeStruct((rows * SLAB, LANES), F32),
        compiler_params=_cparams(("arbitrary",)),
        name="experts",
    )(tile_expert, used, grp, nxt, xs, w_gu, b_gu, w_down, b_down)


def _combine_kernel(dest_ref, next_ref, ys_ref, x1_ref, p_ref, g2_ref, fn_ref, y_ref, buf_ref, sem,
                    *, last):
    i = pl.program_id(0)
    slot = i % 2

    def issue(d_ref, sl):
        def body(r, carry):
            for k in range(TOP_K):
                d = d_ref[r * TOP_K + k]
                pltpu.make_async_copy(ys_ref.at[_slab_rows(d)], buf_ref.at[sl, k, _slab_rows(r)],
                                      sem.at[sl]).start(priority=k % 2)
            return carry
        lax.fori_loop(0, TOK_TILE, body, 0, unroll=2)

    @pl.when(i == 0)
    def _():
        issue(dest_ref, 0)

    @pl.when(i + 1 < pl.num_programs(0))
    def _():
        issue(next_ref, 1 - slot)

    for k in range(TOP_K):
        pltpu.make_async_copy(ys_ref.at[pl.ds(0, TOK_TILE * SLAB)], buf_ref.at[slot, k],
                              sem.at[slot]).wait()

    p = p_ref[...]
    groups = None
    for k in range(TOP_K):
        part = [p[:, k:k + 1] * g for g in _load_slabs(buf_ref.at[slot, k], TOK_TILE)]
        groups = part if groups is None else [a + b for a, b in zip(groups, part)]
    moe = jnp.concatenate(groups, axis=1)
    x2 = x1_ref[...] + g2_ref[...] * moe
    if last:
        x2 = x2 * lax.rsqrt(jnp.mean(x2 * x2, axis=-1, keepdims=True) + EPS) * fn_ref[...]
    y_ref[...] = x2


def _combine(ys, dest_flat, x1, probs, mod, per_token, seq, tile0, final_norm, last):
    n = x1.shape[0]
    tm = TOK_TILE
    tps = max(seq // tm, 1)
    nt = n // tm
    return pl.pallas_call(
        functools.partial(_combine_kernel, last=last),
        grid=(nt,),
        in_specs=[pl.BlockSpec((tm * TOP_K,), lambda i: (i + tile0,), memory_space=pltpu.SMEM),
                  pl.BlockSpec((tm * TOP_K,), lambda i: (jnp.minimum(i + 1, nt - 1) + tile0,),
                               memory_space=pltpu.SMEM),
                  pl.BlockSpec(memory_space=pl.ANY),
                  pl.BlockSpec((tm, D_MODEL), lambda i: (i, 0)),
                  pl.BlockSpec((tm, TOP_K), lambda i: (i + tile0, 0)),
                  _mod_spec(per_token, tm, tps, 5),
                  pl.BlockSpec((1, D_MODEL), lambda i: (0, 0))],
        out_specs=pl.BlockSpec((tm, D_MODEL), lambda i: (i, 0)),
        out_shape=jax.ShapeDtypeStruct((n, D_MODEL), F32),
        scratch_shapes=[pltpu.VMEM((2, TOP_K, tm * SLAB, LANES), F32),
                        pltpu.SemaphoreType.DMA((2,))],
        compiler_params=_cparams(("arbitrary",)),
        name="combine",
    )(dest_flat, dest_flat, ys, x1, probs, mod, final_norm.reshape(1, D_MODEL))


def _invert_kernel(dest_ref, src0_ref, dst0_ref, src_ref, dst_ref, sem, *, n_tok):
    i = pl.program_id(0)
    blk = dest_ref.shape[0]

    @pl.when(i == 0)
    def _():
        copies = [pltpu.make_async_copy(src0_ref, src_ref, sem.at[0]),
                  pltpu.make_async_copy(dst0_ref, dst_ref, sem.at[1])]
        for cp in copies:
            cp.start()
        for cp in copies:
            cp.wait()

    def body(a, carry):
        g = i * blk + a
        d = dest_ref[a]
        tok = g >> (TOP_K.bit_length() - 1)
        src_ref[d] = tok
        dst_ref[d] = (g & (TOP_K - 1)) * n_tok + tok
        return carry

    lax.fori_loop(0, blk, body, 0, unroll=8)


def _invert(dest_flat, n_tok, n_rows):
    blk = TOK_TILE * TOP_K
    whole = pl.BlockSpec(memory_space=pltpu.SMEM)
    src0 = jnp.zeros((n_rows,), I32)
    dst0 = n_tok * TOP_K + jnp.arange(n_rows, dtype=I32)
    return pl.pallas_call(
        functools.partial(_invert_kernel, n_tok=n_tok),
        grid=(dest_flat.shape[0] // blk,),
        in_specs=[pl.BlockSpec((blk,), lambda i: (i,), memory_space=pltpu.SMEM),
                  pl.BlockSpec(memory_space=pl.ANY), pl.BlockSpec(memory_space=pl.ANY)],
        out_specs=[whole, whole],
        out_shape=[jax.ShapeDtypeStruct((n_rows,), I32), jax.ShapeDtypeStruct((n_rows,), I32)],
        scratch_shapes=[pltpu.SemaphoreType.DMA((2,))],
        compiler_params=_cparams(("arbitrary",)),
        name="invert",
    )(dest_flat, src0, dst0)


FF_CHUNKS = 4


def _moe_kernel(src_ref, dst_ref, te_ref, used_ref, h_ref, wgu_ref, bgu_ref, wd_ref, bd_ref, y_ref,
                wgu_bf, wd_bf, act_bf, buf, gsem, ssem):
    i = pl.program_id(0)
    used = used_ref[0]
    slot = i % 2
    tile_rows = ROW_TILE * SLAB
    n_tiles = te_ref.shape[0]

    def gather(tile, sl, rows):
        for r in rows:
            tok = src_ref[tile * ROW_TILE + r]
            pltpu.make_async_copy(h_ref.at[_slab_rows(tok)], buf.at[sl, _slab_rows(r)],
                                  gsem.at[sl]).start(priority=1)

    def scatter(tile, sl, rows):
        for r in rows:
            out = dst_ref[tile * ROW_TILE + r]
            pltpu.make_async_copy(buf.at[2 + sl, _slab_rows(r)], y_ref.at[_slab_rows(out)],
                                  ssem.at[sl]).start()

    def wait_gather(sl):
        pltpu.make_async_copy(h_ref.at[pl.ds(0, tile_rows)], buf.at[sl], gsem.at[sl]).wait()

    def wait_scatter(sl):
        pltpu.make_async_copy(buf.at[2 + sl], y_ref.at[pl.ds(0, tile_rows)], ssem.at[sl]).wait()

    @pl.when(i == 0)
    def _():
        buf[2] = jnp.zeros(buf.shape[1:], F32)
        buf[3] = jnp.zeros(buf.shape[1:], F32)
        tail = y_ref.shape[0] - tile_rows
        pltpu.make_async_copy(buf.at[2], y_ref.at[pl.ds(tail, tile_rows)], ssem.at[0]).start()
        gather(0, 0, range(ROW_TILE))

    prev = te_ref[jnp.maximum(i - 1, 0)]
    fresh = (i == 0) | (te_ref[jnp.minimum(i, n_tiles - 1)] != prev)

    @pl.when(fresh & (i < used))
    def _():
        wgu_bf[...] = wgu_ref[...].astype(BF16)
        wd_bf[...] = wd_ref[...].astype(BF16)

    per = ROW_TILE // FF_CHUNKS
    cw = D_FF // FF_CHUNKS

    @pl.when(i < used)
    def _():
        wait_gather(slot)
        nxt = jnp.minimum(i + 1, n_tiles - 1)
        prv = jnp.where(i >= 1, i - 1, n_tiles)
        half = per // 2
        for c in range(FF_CHUNKS):
            gather(nxt, 1 - slot, range(c * per, (c + 1) * per))
            scatter(prv, 1 - slot, range(c * half, (c + 1) * half))
            x = jnp.concatenate([p.astype(BF16) for p in _load_slabs(buf.at[slot], ROW_TILE)], axis=1)
            gate = jnp.dot(x, wgu_bf[:, c * cw:(c + 1) * cw], preferred_element_type=F32)
            gate = jnp.minimum(gate + bgu_ref[:, c * cw:(c + 1) * cw], SWIGLU_LIMIT)
            up = jnp.dot(x, wgu_bf[:, D_FF + c * cw:D_FF + (c + 1) * cw], preferred_element_type=F32)
            up = jnp.clip(up + bgu_ref[:, D_FF + c * cw:D_FF + (c + 1) * cw], -SWIGLU_LIMIT, SWIGLU_LIMIT)
            act = gate * jax.nn.sigmoid(SWIGLU_ALPHA * gate) * (up + 1.0)
            act_bf[:, c * cw:(c + 1) * cw] = act.astype(BF16)
        ow = D_MODEL // FF_CHUNKS
        base = FF_CHUNKS * half
        wait_scatter(slot)
        for c in range(FF_CHUNKS):
            scatter(prv, 1 - slot, range(base + c * half, base + (c + 1) * half))
            out = jnp.dot(act_bf[...], wd_bf[:, c * ow:(c + 1) * ow], preferred_element_type=F32)
            out = out + bd_ref[:, c * ow:(c + 1) * ow]
            for j in range(ow // LANES):
                buf[2 + slot, pl.ds(c * (ow // LANES) + j, ROW_TILE, stride=SLAB), :] = (
                    out[:, LANES * j:LANES * (j + 1)])

    @pl.when(i == used)
    def _():
        wait_gather(slot)
        wait_scatter(slot)
        scatter(i - 1, 1 - slot, range(ROW_TILE))
        wait_scatter(1 - slot)


def _moe(h2, src, dst, tile_expert, used, w_gu, b_gu, w_down, b_down, n_tok):
    n_tiles = tile_expert.shape[0]
    n_rows = n_tiles * ROW_TILE
    ex = lambda i, src, dst, te, used: (te[jnp.minimum(i, n_tiles - 1)], 0, 0)
    grid_spec = pltpu.PrefetchScalarGridSpec(
        num_scalar_prefetch=4,
        grid=(n_tiles + 1,),
        in_specs=[pl.BlockSpec(memory_space=pl.ANY),
                  pl.BlockSpec((None, D_MODEL, 2 * D_FF), ex),
                  pl.BlockSpec((None, 1, 2 * D_FF), ex),
                  pl.BlockSpec((None, D_FF, D_MODEL), ex),
                  pl.BlockSpec((None, 1, D_MODEL), ex)],
        out_specs=pl.BlockSpec(memory_space=pl.ANY),
        scratch_shapes=[pltpu.VMEM((D_MODEL, 2 * D_FF), BF16), pltpu.VMEM((D_FF, D_MODEL), BF16),
                        pltpu.VMEM((ROW_TILE, D_FF), BF16),
                        pltpu.VMEM((4, ROW_TILE * SLAB, LANES), F32),
                        pltpu.SemaphoreType.DMA((2,)), pltpu.SemaphoreType.DMA((2,))],
    )
    return pl.pallas_call(
        _moe_kernel,
        grid_spec=grid_spec,
        out_shape=jax.ShapeDtypeStruct(((n_tok * TOP_K + n_rows + 2 * ROW_TILE) * SLAB, LANES), F32),
        compiler_params=_cparams(("arbitrary",)),
        name="moe",
    )(src, dst, tile_expert, used, h2, w_gu, b_gu.reshape(N_EXPERTS, 1, 2 * D_FF), w_down,
      b_down.reshape(N_EXPERTS, 1, D_MODEL))


def _merge_kernel(y0_ref, y1_ref, y2_ref, y3_ref, x1_ref, p_ref, g2_ref, fn_ref, o_ref, *, last):
    p = p_ref[...]
    tm = p.shape[0]
    groups = None
    for k, ref in enumerate((y0_ref, y1_ref, y2_ref, y3_ref)):
        part = [p[:, k:k + 1] * g for g in _load_slabs(ref, tm)]
        groups = part if groups is None else [a + b for a, b in zip(groups, part)]
    x2 = x1_ref[...] + g2_ref[...] * jnp.concatenate(groups, axis=1)
    if last:
        x2 = x2 * lax.rsqrt(jnp.mean(x2 * x2, axis=-1, keepdims=True) + EPS) * fn_ref[...]
    o_ref[...] = x2


def _merge(y, x1, probs, mod, per_token, seq, tile0, n_tok, final_norm, last):
    n = x1.shape[0]
    tm = TOK_TILE
    tps = max(seq // tm, 1)
    choice = lambda k: pl.BlockSpec((tm * SLAB, LANES), lambda i: (k * (n_tok // tm) + tile0 + i, 0))
    return pl.pallas_call(
        functools.partial(_merge_kernel, last=last),
        grid=(n // tm,),
        in_specs=[choice(0), choice(1), choice(2), choice(3),
                  pl.BlockSpec((tm, D_MODEL), lambda i: (i, 0)),
                  pl.BlockSpec((tm, TOP_K), lambda i: (i + tile0, 0)),
                  _mod_spec(per_token, tm, tps, 5),
                  pl.BlockSpec((1, D_MODEL), lambda i: (0, 0))],
        out_specs=pl.BlockSpec((tm, D_MODEL), lambda i: (i, 0)),
        out_shape=jax.ShapeDtypeStruct((n, D_MODEL), F32),
        compiler_params=_cparams(("arbitrary",)),
        name="merge",
    )(y, y, y, y, x1, probs, mod, final_norm.reshape(1, D_MODEL))


def kernel(x_prompt, x_sample, state_hgrn, state_pool, c_prompt, c_sample, w_ada, b_ada, norm1,
           norm2, w_in, lower_bounds, hgrn_norm, w_pool, pool_scale, w_out, w_router, b_router,
           w_gu, b_gu, w_down, b_down, final_norm):
    bp, seq_p, _ = x_prompt.shape
    bs, seq_s, _ = x_sample.shape
    np_, ns = bp * seq_p, bs * seq_s
    depth = w_ada.shape[0]
    past_len = 16384

    lbs = jnp.cumsum(jax.nn.softmax(lower_bounds.astype(F32), axis=0), axis=0)
    xp = x_prompt.reshape(np_, D_MODEL)
    xs_ = x_sample.reshape(ns, D_MODEL)
    c_all = jnp.concatenate([c_prompt, c_sample], axis=0)

    sp_l, hp_l, ss_l, hs_l = [], [], [], []
    for l in range(depth):
        w_in_bf = w_in[l].astype(BF16)
        w_out_bf = w_out[l].astype(BF16)
        w_pool_bf = w_pool[l].astype(BF16)
        wr_hi = w_router[l].astype(BF16)
        wr_lo = (w_router[l] - wr_hi.astype(F32)).astype(BF16)
        lb = lbs[l].reshape(1, D_A)
        hn = hgrn_norm[l].reshape(1, D_A)
        psc = pool_scale[l].reshape(1, D_B)

        mod = _ada(c_all, w_ada[l], b_ada[l])
        mod_p = mod[:bp].reshape(bp, 1, 6 * D_MODEL)
        mod_s = jnp.repeat(mod[bp:], seq_s, axis=0)

        up = _inproj(xp, mod_p, False, seq_p, norm1[l], w_in_bf)
        us = _inproj(xs_, mod_s, True, seq_s, norm1[l], w_in_bf)
        us3 = us.reshape(bs, seq_s, D_IN)

        oa_p, s_p = _hgrn(up.reshape(bp, seq_p, D_IN), None, lb, hn)
        oa_p = oa_p.reshape(np_, D_A)
        oa_s, s_s = _hgrn(us, state_hgrn[l], lb, hn, seq=seq_s)
        ob_p, h_p = _pool_prompt(up, bp, seq_p, w_pool_bf, psc)
        ob_t, h_t = _pool_sample(us3.transpose(1, 0, 2), state_pool[l].transpose(1, 0, 2), past_len,
                                 w_pool_bf, psc)
        ob_s, h_s = ob_t.transpose(1, 0, 2), h_t.transpose(1, 0, 2)

        n_tok = np_ + ns
        n_tiles = (n_tok * TOP_K) // ROW_TILE + N_EXPERTS
        x1p, *joint = _outproj(xp, oa_p, ob_p, mod_p, False, seq_p, norm2[l], w_out_bf,
                               wr_hi, wr_lo, b_router[l], n_tok, 0)
        x1s, h2, top_idx, probs = _outproj(
            xs_, oa_s.reshape(ns, D_A), ob_s.reshape(ns, D_B), mod_s, True, seq_s, norm2[l],
            w_out_bf, wr_hi, wr_lo, b_router[l], n_tok, np_ // TOK_TILE, joint)

        dest, counts = _rank(top_idx)
        dest_flat = dest.reshape(n_tok * TOP_K)
        tiles_e = (counts.reshape(N_EXPERTS) + ROW_TILE - 1) // ROW_TILE
        ends = jnp.cumsum(tiles_e)
        used = ends[-1:].astype(I32)
        tile_expert = jnp.minimum(
            jnp.sum(jnp.arange(n_tiles, dtype=I32)[:, None] >= ends[None, :], axis=1),
            N_EXPERTS - 1).astype(I32)

        last_tile = (ends - 1).astype(I32)

        xsort = _dispatch(h2, dest_flat, last_tile, tiles_e.astype(I32), used, n_tiles)
        grp = jnp.cumsum(jnp.concatenate(
            [jnp.zeros((1,), I32), (tile_expert[1:] != tile_expert[:-1]).astype(I32)]))
        eids = jnp.arange(N_EXPERTS, dtype=I32)
        later = (eids[None, :] > eids[:, None]) & (tiles_e > 0)[None, :]
        nxt_e = jnp.min(jnp.where(later, eids[None, :], N_EXPERTS), axis=1)
        nxt = jnp.where(nxt_e < N_EXPERTS, nxt_e, -1)[tile_expert].astype(I32)
        ysort = _mlp_loop(xsort, tile_expert, used, grp.astype(I32), nxt, w_gu[l], b_gu[l],
                          w_down[l], b_down[l])

        last = l == depth - 1
        xp = _combine(ysort, dest_flat, x1p, probs, mod_p, False, seq_p, 0, final_norm, last)
        xs_ = _combine(ysort, dest_flat, x1s, probs, mod_s, True, seq_s, np_ // TOK_TILE,
                       final_norm, last)

        sp_l.append(s_p)
        hp_l.append(h_p)
        ss_l.append(s_s)
        hs_l.append(h_s)

    return (xp.reshape(bp, seq_p, D_MODEL), xs_.reshape(bs, seq_s, D_MODEL),
            jnp.stack(sp_l), jnp.stack(hp_l), jnp.stack(ss_l), jnp.stack(hs_l))
```

```python
import functools

import jax
import jax.numpy as jnp
from jax import lax
from jax.experimental import pallas as pl
from jax.experimental.pallas import tpu as pltpu

F32, BF16, I32 = jnp.float32, jnp.bfloat16, jnp.int32

D_MODEL = 1024
D_A = 512
H_A = 4
DK = 128
DV = 128
D_B = 512
POOL_WINDOWS = (2, 4, 8, 16)
POOL_GROUP = 128
POOL_HIST = 15
D_IN = 4 * D_A + D_B
N_EXPERTS = 32
TOP_K = 4
D_FF = 1024
SWIGLU_LIMIT = 7.0
SWIGLU_ALPHA = 1.702
EPS = 1e-6
PAST_LEN = 16384

SUB = 16
HALF = SUB // 2
HGRN_SEQS = 8
ROW_TILE = 256
TOK_TILE = 256
FF_CHUNKS = 4
X_RING = 3
LANES = 128
SLAB = D_MODEL // LANES
VMEM_LIMIT = 56 * 1024 * 1024


def _cparams(sem):
    return pltpu.CompilerParams(dimension_semantics=sem, vmem_limit_bytes=VMEM_LIMIT)


def _silu(x):
    return x * jax.nn.sigmoid(x)


def _mod_spec(per_token, tm, tiles_per_seq, col, last_tile=None):
    tile = (lambda i: i) if last_tile is None else (lambda i: jnp.minimum(i, last_tile))
    if per_token:
        return pl.BlockSpec((tm, D_MODEL), lambda i: (tile(i), col))
    return pl.BlockSpec((None, 1, D_MODEL), lambda i: (tile(i) // tiles_per_seq, 0, col))


def _ada_kernel(c_ref, w_ref, b_ref, o_ref):
    c = c_ref[...]
    o_ref[...] = jnp.dot(_silu(c).astype(BF16), w_ref[...].astype(BF16),
                         preferred_element_type=F32) + b_ref[...]


def _ada(c_all, w_ada, b_ada):
    rows = c_all.shape[0]
    n = w_ada.shape[1]
    return pl.pallas_call(
        _ada_kernel,
        grid=(n // D_MODEL,),
        in_specs=[pl.BlockSpec((rows, D_MODEL), lambda j: (0, 0)),
                  pl.BlockSpec((D_MODEL, D_MODEL), lambda j: (0, j)),
                  pl.BlockSpec((1, D_MODEL), lambda j: (0, j))],
        out_specs=pl.BlockSpec((rows, D_MODEL), lambda j: (0, j)),
        out_shape=jax.ShapeDtypeStruct((rows, n), F32),
        compiler_params=_cparams(("arbitrary",)),
        name="ada",
    )(c_all, w_ada, b_ada.reshape(1, n))


def _inproj_kernel(x_ref, sh_ref, sc_ref, n1_ref, w_ref, u_ref):
    x = x_ref[...]
    h = x * lax.rsqrt(jnp.mean(x * x, axis=-1, keepdims=True) + EPS) * n1_ref[...]
    h = h * (1.0 + sc_ref[...]) + sh_ref[...]
    u_ref[...] = jnp.dot(h.astype(BF16), w_ref[...], preferred_element_type=F32)


def _inproj(x, mod, per_token, seq, norm1, w_in_bf):
    n = x.shape[0]
    tm = min(512, n)
    tps = max(seq // tm, 1)
    return pl.pallas_call(
        _inproj_kernel,
        grid=(n // tm,),
        in_specs=[pl.BlockSpec((tm, D_MODEL), lambda i: (i, 0)),
                  _mod_spec(per_token, tm, tps, 0),
                  _mod_spec(per_token, tm, tps, 1),
                  pl.BlockSpec((1, D_MODEL), lambda i: (0, 0)),
                  pl.BlockSpec((D_MODEL, D_IN), lambda i: (0, 0))],
        out_specs=pl.BlockSpec((tm, D_IN), lambda i: (i, 0)),
        out_shape=jax.ShapeDtypeStruct((n, D_IN), F32),
        compiler_params=_cparams(("arbitrary",)),
        name="inproj",
    )(x, mod, mod, norm1.reshape(1, D_MODEL), w_in_bf)


def _block_select():
    r = lax.broadcasted_iota(I32, (SUB, SUB * DK), 0)
    c = lax.broadcasted_iota(I32, (SUB, SUB * DK), 1) >> (DK.bit_length() - 1)
    return (r == c).astype(BF16)


def _hgrn_step(q, fp, iv, g, lb, hn, st_ref, sel_ref, cm_ref, b_ref, k_ref, valid):
    nrow = HGRN_SEQS * SUB
    f = lb + (1.0 - lb) * jax.nn.sigmoid(fp)
    lf = jnp.log2(f)
    kk = 1.0 - f
    if valid < SUB:
        live = (lax.broadcasted_iota(I32, (nrow, D_A), 0) & (SUB - 1)) < valid
        lf = jnp.where(live, lf, 0.0)
        kk = jnp.where(live, kk, 0.0)
    qa = _silu(q)

    hi = lf.astype(BF16)
    lo = (lf - hi.astype(F32)).astype(BF16)
    cum, tot = cm_ref[0], cm_ref[1]
    b = jnp.dot(cum, hi, preferred_element_type=F32) + jnp.dot(cum, lo, preferred_element_type=F32)
    be = jnp.dot(tot, hi, preferred_element_type=F32) + jnp.dot(tot, lo, preferred_element_type=F32)
    qt = (qa * jnp.exp2(b)).astype(BF16)
    kh = (kk * jnp.exp2(be - b)).astype(BF16)
    dec = jnp.exp2(be)
    ivb = iv.astype(BF16)
    for h in range(H_A):
        b_ref[h] = b[:, DK * h:DK * (h + 1)]
        k_ref[h] = kk[:, DK * h:DK * (h + 1)]

    zero_half = jnp.zeros((HALF, D_A), F32)
    tr = lax.broadcasted_iota(I32, (H_A * SUB, SUB), 0) & (SUB - 1)
    tc = lax.broadcasted_iota(I32, (H_A * SUB, SUB), 1)
    sg = _silu(g)

    own = ((lax.broadcasted_iota(I32, (nrow, HGRN_SEQS * DK), 0) >> (SUB.bit_length() - 1))
           == (lax.broadcasted_iota(I32, (nrow, HGRN_SEQS * DK), 1) >> (DK.bit_length() - 1)))
    inter = [None] * (HGRN_SEQS * H_A)
    for h in range(H_A):
        sl = slice(DK * h, DK * (h + 1))
        kbd = jnp.where(own, jnp.concatenate([kh[:, sl]] * HGRN_SEQS, axis=1), jnp.zeros((), BF16))
        ut_all = lax.dot_general(ivb[:, sl], kbd, (((0,), (0,)), ((), ())),
                                 preferred_element_type=F32)
        for n in range(HGRN_SEQS):
            rows = slice(n * SUB, (n + 1) * SUB)
            st = st_ref[n, h]
            inter[n * H_A + h] = lax.dot_general(qt[rows, sl], st.astype(BF16),
                                                 (((1,), (1,)), ((), ())),
                                                 preferred_element_type=F32)
            dtile = dec[n * SUB:n * SUB + HALF, sl]
            st_ref[n, h] = ((st.reshape(DV // HALF, HALF, DK) * dtile[None]).reshape(DV, DK)
                            + ut_all[:, DK * n:DK * (n + 1)])

    scores = []
    for n in range(HGRN_SEQS):
        per_s = []
        for s in range(SUB):
            row = n * SUB + s
            b_s = jnp.concatenate([b_ref[h, pl.ds(row, HALF, stride=0), :] for h in range(H_A)], axis=1)
            k_s = jnp.concatenate([k_ref[h, pl.ds(row, HALF, stride=0), :] for h in range(H_A)], axis=1)
            halves = []
            for hf in range(2):
                if hf == 0 and s >= HALF:
                    halves.append(zero_half)
                    continue
                r0 = n * SUB + hf * HALF
                e = jnp.exp2(jnp.minimum(b[r0:r0 + HALF] - b_s, 0.0))
                halves.append(e * (qa[r0:r0 + HALF] * k_s))
            per_s.append(jnp.concatenate(halves, axis=0).astype(BF16))
        pcat = jnp.concatenate(
            [jnp.concatenate([p[:, DK * h:DK * (h + 1)] for p in per_s], axis=1) for h in range(H_A)],
            axis=0)
        a = lax.dot_general(pcat, sel_ref[...], (((1,), (1,)), ((), ())),
                            preferred_element_type=F32)
        scores.append(jnp.where(tr >= tc, a, 0.0).astype(BF16))

    outs = []
    for n in range(HGRN_SEQS):
        rows = slice(n * SUB, (n + 1) * SUB)
        heads = []
        for h in range(H_A):
            sl = slice(DK * h, DK * (h + 1))
            o = inter[n * H_A + h] + jnp.dot(scores[n][h * SUB:(h + 1) * SUB, :], ivb[rows, sl],
                                             preferred_element_type=F32)
            o = o * lax.rsqrt(jnp.mean(o * o, axis=-1, keepdims=True) + EPS) * hn[:, sl]
            heads.append(o)
        outs.append(jnp.concatenate(heads, axis=1) * sg[rows])
    return jnp.concatenate(outs, axis=0)


def _hgrn_kernel(*refs, rows, has_s0):
    if has_s0:
        (q_ref, f_ref, i_ref, g_ref, lb_ref, hn_ref, s0_ref, o_ref, s_ref,
         st_ref, sel_ref, cm_ref, b_ref, k_ref, pad_ref) = refs
    else:
        (q_ref, f_ref, i_ref, g_ref, lb_ref, hn_ref, o_ref, s_ref,
         st_ref, sel_ref, cm_ref, b_ref, k_ref) = refs
    t = pl.program_id(1)
    nrow = HGRN_SEQS * SUB

    @pl.when((t == 0) & (pl.program_id(0) == 0))
    def _():
        sel_ref[...] = _block_select()
        r = lax.broadcasted_iota(I32, (nrow, nrow), 0)
        c = lax.broadcasted_iota(I32, (nrow, nrow), 1)
        shift = SUB.bit_length() - 1
        same = (r >> shift) == (c >> shift)
        cm_ref[0] = (same & (r >= c)).astype(BF16)
        cm_ref[1] = same.astype(BF16)

    lb = lb_ref[...]
    hn = hn_ref[...]
    if has_s0:
        for n in range(HGRN_SEQS):
            for h in range(H_A):
                st_ref[n, h] = s0_ref[n, h].T
        pad_ref[...] = jnp.zeros_like(pad_ref)
        for j, ref in enumerate((q_ref, f_ref, i_ref, g_ref)):
            for n in range(HGRN_SEQS):
                pad_ref[j, n, 0:rows, :] = ref[n * rows:(n + 1) * rows, :]
        o = _hgrn_step(*(pad_ref[j].reshape(nrow, D_A) for j in range(4)), lb, hn,
                       st_ref, sel_ref, cm_ref, b_ref, k_ref, rows)
        for n in range(HGRN_SEQS):
            o_ref[n * rows:(n + 1) * rows, :] = o[n * SUB:n * SUB + rows, :]
    else:
        @pl.when(t == 0)
        def _():
            st_ref[...] = jnp.zeros_like(st_ref)

        def body(j, carry):
            rs = pl.ds(pl.multiple_of(j * SUB, SUB), SUB)
            o = _hgrn_step(*(ref[:, rs, :].reshape(nrow, D_A) for ref in (q_ref, f_ref, i_ref, g_ref)),
                           lb, hn, st_ref, sel_ref, cm_ref, b_ref, k_ref, SUB)
            o_ref[:, rs, :] = o.reshape(HGRN_SEQS, SUB, D_A)
            return carry

        lax.fori_loop(0, rows // SUB, body, 0)

    def emit_state():
        for n in range(HGRN_SEQS):
            for h in range(H_A):
                s_ref[n, h] = st_ref[n, h].T

    if has_s0:
        emit_state()
    else:
        pl.when(t == pl.num_programs(1) - 1)(emit_state)


def _hgrn(u3, s0, lb, hn, seq=None):
    has_s0 = s0 is not None
    if has_s0:
        batch = u3.shape[0] // seq
        rows = seq
        grid = (batch // HGRN_SEQS, 1)
        blk = lambda cb: pl.BlockSpec((HGRN_SEQS * rows, D_A), lambda b, t: (b, cb))
        o_shape = (batch * seq, D_A)
    else:
        batch, seq, _ = u3.shape
        rows = 128
        grid = (batch // HGRN_SEQS, seq // rows)
        blk = lambda cb: pl.BlockSpec((HGRN_SEQS, rows, D_A), lambda b, t: (b, t, cb))
        o_shape = (batch, seq, D_A)
    sblk = pl.BlockSpec((HGRN_SEQS, H_A, DK, DV), lambda b, t: (b, 0, 0, 0))
    vec = pl.BlockSpec((1, D_A), lambda b, t: (0, 0))
    nrow = HGRN_SEQS * SUB
    scratch = [pltpu.VMEM((HGRN_SEQS, H_A, DV, DK), F32), pltpu.VMEM((SUB, SUB * DK), BF16),
               pltpu.VMEM((2, nrow, nrow), BF16), pltpu.VMEM((H_A, nrow, DK), F32),
               pltpu.VMEM((H_A, nrow, DK), F32)]
    if has_s0:
        scratch.append(pltpu.VMEM((4, HGRN_SEQS, SUB, D_A), F32))
    args = (u3, u3, u3, u3, lb, hn) + ((s0,) if has_s0 else ())
    return pl.pallas_call(
        functools.partial(_hgrn_kernel, rows=rows, has_s0=has_s0),
        grid=grid,
        in_specs=[blk(0), blk(1), blk(2), blk(3), vec, vec] + ([sblk] if has_s0 else []),
        out_specs=[blk(0), sblk],
        out_shape=[jax.ShapeDtypeStruct(o_shape, F32),
                   jax.ShapeDtypeStruct((batch, H_A, DK, DV), F32)],
        scratch_shapes=scratch,
        compiler_params=_cparams(("arbitrary", "arbitrary")),
        name="hgrn_state" if has_s0 else "hgrn_scan",
    )(*args)


def _pool_groups(full_ref, base, tm, cnt_fn, wp_ref, ps_ref, step=1):
    outs = []
    for gi, w in enumerate(POOL_WINDOWS):
        ch = slice(POOL_GROUP * gi, POOL_GROUP * (gi + 1))
        v = full_ref[base:base + tm, ch]
        s = v
        for j in range(1, w):
            s = s + full_ref[base - j * step:base - j * step + tm, ch]
        d = s / cnt_fn(w) - v
        outs.append(jnp.dot(d.astype(BF16), wp_ref[gi], preferred_element_type=F32))
    return jnp.concatenate(outs, axis=1) * ps_ref[...]


def _pool_prompt_kernel(v_ref, prev_ref, wp_ref, ps_ref, o_ref, hist_ref, full_ref, *, tm):
    t = pl.program_id(1)
    prev = prev_ref[...]
    full_ref[0:16, :] = jnp.where(t == 0, 0.0, prev)
    full_ref[16:16 + tm, :] = v_ref[...]
    pos = (lax.broadcasted_iota(I32, (tm, 1), 0) + t * tm + 1).astype(F32)
    o_ref[...] = _pool_groups(full_ref, 16, tm, lambda w: jnp.minimum(pos, float(w)),
                              wp_ref, ps_ref)

    @pl.when(t == pl.num_programs(1) - 1)
    def _():
        hist_ref[...] = full_ref[16 + tm - POOL_HIST:16 + tm, :]


def _pool_prompt(u, batch, seq, w_pool_bf, pool_scale):
    tm = min(1024, seq)
    nt = seq // tm
    vcol = (4 * D_A) // D_B
    return pl.pallas_call(
        functools.partial(_pool_prompt_kernel, tm=tm),
        grid=(batch, nt),
        in_specs=[pl.BlockSpec((tm, D_B), lambda b, t: (b * nt + t, vcol)),
                  pl.BlockSpec((16, D_B),
                               lambda b, t: (jnp.maximum((b * nt + t) * (tm // 16) - 1, 0), vcol)),
                  pl.BlockSpec((len(POOL_WINDOWS), POOL_GROUP, POOL_GROUP), lambda b, t: (0, 0, 0)),
                  pl.BlockSpec((1, D_B), lambda b, t: (0, 0))],
        out_specs=[pl.BlockSpec((tm, D_B), lambda b, t: (b * nt + t, 0)),
                   pl.BlockSpec((None, POOL_HIST, D_B), lambda b, t: (b, 0, 0))],
        out_shape=[jax.ShapeDtypeStruct((batch * seq, D_B), F32),
                   jax.ShapeDtypeStruct((batch, POOL_HIST, D_B), F32)],
        scratch_shapes=[pltpu.VMEM((16 + tm, D_B), F32)],
        compiler_params=_cparams(("arbitrary", "arbitrary")),
        name="pool_prompt",
    )(u, u, w_pool_bf, pool_scale)


def _pool_sample_kernel(v_ref, hist_ref, wp_ref, ps_ref, o_ref, nh_ref, full_ref, *,
                        seq, batch, start):
    full_ref[0:batch, :] = jnp.zeros((batch, D_B), F32)
    full_ref[batch:16 * batch, :] = hist_ref[...].reshape(POOL_HIST * batch, D_B)
    full_ref[16 * batch:(16 + seq) * batch, :] = v_ref[...].reshape(seq * batch, D_B)

    def cnt(w):
        rows = [jnp.full((batch, 1), float(min(start + t + 1, w)), F32) for t in range(seq)]
        return jnp.concatenate(rows, axis=0)

    o = _pool_groups(full_ref, 16 * batch, seq * batch, cnt, wp_ref, ps_ref, step=batch)
    o_ref[...] = o.reshape(seq, batch, D_B)
    nh_ref[...] = full_ref[(16 + seq - POOL_HIST) * batch:(16 + seq) * batch, :].reshape(
        POOL_HIST, batch, D_B)


def _pool_sample(u_t, hist_t, start, w_pool_bf, pool_scale):
    seq, batch, _ = u_t.shape
    vcol = (4 * D_A) // D_B
    return pl.pallas_call(
        functools.partial(_pool_sample_kernel, seq=seq, batch=batch, start=start),
        grid=(1,),
        in_specs=[pl.BlockSpec((seq, batch, D_B), lambda i: (0, 0, vcol)),
                  pl.BlockSpec((POOL_HIST, batch, D_B), lambda i: (0, 0, 0)),
                  pl.BlockSpec((len(POOL_WINDOWS), POOL_GROUP, POOL_GROUP), lambda i: (0, 0, 0)),
                  pl.BlockSpec((1, D_B), lambda i: (0, 0))],
        out_specs=[pl.BlockSpec((seq, batch, D_B), lambda i: (0, 0, 0)),
                   pl.BlockSpec((POOL_HIST, batch, D_B), lambda i: (0, 0, 0))],
        out_shape=[jax.ShapeDtypeStruct((seq, batch, D_B), F32),
                   jax.ShapeDtypeStruct((POOL_HIST, batch, D_B), F32)],
        scratch_shapes=[pltpu.VMEM(((16 + seq) * batch, D_B), F32)],
        compiler_params=_cparams(("arbitrary",)),
        name="pool_sample",
    )(u_t, hist_t, w_pool_bf, pool_scale)


def _store_slabs(ref, value):
    n = value.shape[0]
    for j in range(SLAB):
        ref[pl.ds(j, n, stride=SLAB), :] = value[:, LANES * j:LANES * (j + 1)]


def _load_slabs(ref, n):
    return [ref[pl.ds(j, n, stride=SLAB), :] for j in range(SLAB)]


def _slab_rows(row):
    return pl.ds(pl.multiple_of(row * SLAB, SLAB), SLAB)


def _outproj_kernel(x_ref, oa_ref, ob_ref, g1_ref, sh2_ref, sc2_ref, n2_ref, woa_ref, wob_ref,
                    wr_ref, br_ref, *rest):
    x1_ref, h2_ref, idx_ref, prob_ref, lg_ref = rest[-5:]
    i = pl.program_id(0)
    slot = i % 2

    @pl.when(i == 0)
    def _():
        lg_ref[...] = jnp.zeros_like(lg_ref)

    logits = lg_ref[1 - slot]

    mix = (jnp.dot(oa_ref[...].astype(BF16), woa_ref[...], preferred_element_type=F32)
           + jnp.dot(ob_ref[...].astype(BF16), wob_ref[...], preferred_element_type=F32))
    x1 = x_ref[...] + g1_ref[...] * mix
    x1_ref[...] = x1
    h2 = x1 * lax.rsqrt(jnp.mean(x1 * x1, axis=-1, keepdims=True) + EPS) * n2_ref[...]
    h2 = h2 * (1.0 + sc2_ref[...]) + sh2_ref[...]
    _store_slabs(h2_ref, h2)
    lg_ref[slot] = jnp.dot(h2.astype(BF16), wr_ref[...], preferred_element_type=F32) + br_ref[...]

    tm = logits.shape[0]
    lane = lax.broadcasted_iota(I32, (tm, N_EXPERTS), 1)
    kcol = lax.broadcasted_iota(I32, (tm, TOP_K), 1)
    vals = jnp.zeros((tm, TOP_K), F32)
    idxs = jnp.zeros((tm, TOP_K), I32)
    for k in range(TOP_K):
        m = jnp.max(logits, axis=-1, keepdims=True)
        am = jnp.min(jnp.where(logits == m, lane, N_EXPERTS), axis=-1, keepdims=True)
        vals = jnp.where(kcol == k, m, vals)
        idxs = jnp.where(kcol == k, am, idxs)
        logits = jnp.where(lane == am, -jnp.inf, logits)
    e = jnp.exp(vals - vals[:, 0:1])
    prob_ref[...] = e / jnp.sum(e, axis=-1, keepdims=True)
    idx_ref[...] = idxs


def _outproj(x, o_a, o_b, mod, per_token, seq, norm2, w_out_bf, w_router_bf, b_router,
             n_total, tile0, joint=None):
    n = x.shape[0]
    tm = min(256, n)
    tps = max(seq // tm, 1)
    nt = n // tm
    cur = lambda i: jnp.minimum(i, nt - 1)
    prv = lambda i: jnp.maximum(i - 1, 0)
    row = lambda width: pl.BlockSpec((tm, width), lambda i: (cur(i), 0))
    jrow = lambda width: pl.BlockSpec((tm, width), lambda i: (prv(i) + tile0, 0))
    const = lambda shape: pl.BlockSpec(shape, lambda i: (0,) * len(shape))
    joint = () if joint is None else tuple(joint)
    n_in = 11
    return pl.pallas_call(
        _outproj_kernel,
        grid=(nt + 1,),
        in_specs=[row(D_MODEL), row(D_A), row(D_B),
                  _mod_spec(per_token, tm, tps, 2, nt - 1),
                  _mod_spec(per_token, tm, tps, 3, nt - 1),
                  _mod_spec(per_token, tm, tps, 4, nt - 1),
                  const((1, D_MODEL)),
                  pl.BlockSpec((D_A, D_MODEL), lambda i: (0, 0)),
                  pl.BlockSpec((D_B, D_MODEL), lambda i: (1, 0)),
                  const((D_MODEL, N_EXPERTS)), const((1, N_EXPERTS))]
                 + [pl.BlockSpec(memory_space=pl.ANY)] * len(joint),
        out_specs=[row(D_MODEL), pl.BlockSpec((tm * SLAB, LANES), lambda i: (cur(i) + tile0, 0)),
                   jrow(TOP_K), jrow(TOP_K)],
        out_shape=[jax.ShapeDtypeStruct((n, D_MODEL), F32),
                   jax.ShapeDtypeStruct((n_total * SLAB, LANES), F32),
                   jax.ShapeDtypeStruct((n_total, TOP_K), I32),
                   jax.ShapeDtypeStruct((n_total, TOP_K), F32)],
        input_output_aliases={n_in + j: 1 + j for j in range(len(joint))},
        scratch_shapes=[pltpu.VMEM((2, tm, N_EXPERTS), F32)],
        compiler_params=_cparams(("arbitrary",)),
        name="outproj",
    )(x, o_a, o_b, mod, mod, mod, norm2.reshape(1, D_MODEL), w_out_bf, w_out_bf,
      w_router_bf, b_router.reshape(1, N_EXPERTS), *joint)


def _rank_kernel(idx_ref, dest_ref, cnt_ref, carry_ref, *, te):
    phase = pl.program_id(0)
    i = pl.program_id(1)
    idx = idx_ref[...]
    lane = lax.broadcasted_iota(I32, (te, N_EXPERTS), 1)
    onehots = [(idx[:, k:k + 1] == lane) for k in range(TOP_K)]
    member = jnp.zeros((te, N_EXPERTS), F32)
    for oh in onehots:
        member = member + oh.astype(F32)
    colsum = jnp.sum(member, axis=0, keepdims=True)

    @pl.when((phase == 0) & (i == 0))
    def _():
        carry_ref[...] = jnp.zeros_like(carry_ref)

    @pl.when(phase == 0)
    def _():
        carry_ref[...] += colsum

    @pl.when((phase == 1) & (i == 0))
    def _():
        cnt = carry_ref[...]
        cnt_ref[...] = cnt.astype(I32)
        tiles = jnp.floor((cnt + (ROW_TILE - 1)) * (1.0 / ROW_TILE))
        r = lax.broadcasted_iota(I32, (N_EXPERTS, N_EXPERTS), 0)
        c = lax.broadcasted_iota(I32, (N_EXPERTS, N_EXPERTS), 1)
        before = (r < c).astype(BF16)
        carry_ref[...] = ROW_TILE * jnp.dot(tiles.astype(BF16), before, preferred_element_type=F32)

    @pl.when(phase == 1)
    def _():
        r = lax.broadcasted_iota(I32, (te, te), 0)
        c = lax.broadcasted_iota(I32, (te, te), 1)
        earlier = (r > c).astype(BF16)
        base = carry_ref[...] + jnp.dot(earlier, member.astype(BF16), preferred_element_type=F32)
        kcol = lax.broadcasted_iota(I32, (te, TOP_K), 1)
        dest = jnp.zeros((te, TOP_K), F32)
        for k, oh in enumerate(onehots):
            dk = jnp.sum(jnp.where(oh, base, 0.0), axis=-1, keepdims=True)
            dest = jnp.where(kcol == k, dk, dest)
        dest_ref[...] = dest.astype(I32)
        carry_ref[...] += colsum


def _rank(top_idx):
    n = top_idx.shape[0]
    te = 512
    return pl.pallas_call(
        functools.partial(_rank_kernel, te=te),
        grid=(2, n // te),
        in_specs=[pl.BlockSpec((te, TOP_K), lambda p, i: (i, 0))],
        out_specs=[pl.BlockSpec((te, TOP_K), lambda p, i: (p * i, 0)),
                   pl.BlockSpec((1, N_EXPERTS), lambda p, i: (0, 0))],
        out_shape=[jax.ShapeDtypeStruct((n, TOP_K), I32),
                   jax.ShapeDtypeStruct((1, N_EXPERTS), I32)],
        scratch_shapes=[pltpu.VMEM((1, N_EXPERTS), F32)],
        compiler_params=_cparams(("arbitrary", "arbitrary")),
        name="rank",
    )(top_idx)


def _dispatch_kernel(last_ref, ntile_ref, used_ref, dest_ref, h_ref, xs_ref, zero_ref, sem, zsem,
                     *, n_tiles):
    i = pl.program_id(0)

    @pl.when(i == 0)
    def _():
        zero_ref[...] = jnp.zeros_like(zero_ref)

        def clear(tile):
            rows = pl.ds(pl.multiple_of(tile * (ROW_TILE * SLAB), ROW_TILE * SLAB), ROW_TILE * SLAB)
            return pltpu.make_async_copy(zero_ref, xs_ref.at[rows], zsem)

        def over_tail(fn):
            def body(t, carry):
                fn(clear(t))
                return carry
            lax.fori_loop(used_ref[0], n_tiles, body, 0)

        for fn in (lambda c: c.start(), lambda c: c.wait()):
            for e in range(N_EXPERTS):
                @pl.when(ntile_ref[e] > 0)
                def _():
                    fn(clear(last_ref[e]))
            over_tail(fn)

    def body(r, carry):
        for k in range(TOP_K):
            d = dest_ref[r * TOP_K + k]
            pltpu.make_async_copy(h_ref.at[_slab_rows(r)], xs_ref.at[_slab_rows(d)], sem).start(
                priority=k % 2)
        return carry

    lax.fori_loop(0, TOK_TILE, body, 0, unroll=2)
    for k in range(TOP_K):
        pltpu.make_async_copy(h_ref, xs_ref.at[pl.ds(0, TOK_TILE * SLAB)], sem).wait()


def _dispatch(h2, dest_flat, last_tile, ntiles, used, n_tiles):
    n = h2.shape[0] // SLAB
    grid_spec = pltpu.PrefetchScalarGridSpec(
        num_scalar_prefetch=3,
        grid=(n // TOK_TILE,),
        in_specs=[pl.BlockSpec((TOK_TILE * TOP_K,), lambda i, *_: (i,), memory_space=pltpu.SMEM),
                  pl.BlockSpec((TOK_TILE * SLAB, LANES), lambda i, *_: (i, 0))],
        out_specs=pl.BlockSpec(memory_space=pl.ANY),
        scratch_shapes=[pltpu.VMEM((ROW_TILE * SLAB, LANES), F32),
                        pltpu.SemaphoreType.DMA(()), pltpu.SemaphoreType.DMA(())],
    )
    return pl.pallas_call(
        functools.partial(_dispatch_kernel, n_tiles=n_tiles),
        grid_spec=grid_spec,
        out_shape=jax.ShapeDtypeStruct((n_tiles * ROW_TILE * SLAB, LANES), F32),
        compiler_params=_cparams(("arbitrary",)),
        name="dispatch",
    )(last_tile, ntiles, used, dest_flat, h2)


def _experts_kernel(te_ref, used_ref, grp_ref, nxt_ref, xs_ref, wgu_ref, bgu_ref, wd_ref, bd_ref,
                    ys_ref, wgu_f, wd_f, wgu_bf, wd_bf, act_bf, xring, yring, wsem, xsem, ysem,
                    *, n_tiles):
    used = used_ref[0]
    tile_rows = ROW_TILE * SLAB

    def weights(e, ws):
        return (pltpu.make_async_copy(wgu_ref.at[e], wgu_f.at[ws], wsem.at[ws, 0]),
                pltpu.make_async_copy(wd_ref.at[e], wd_f.at[ws], wsem.at[ws, 1]))

    def tile_rows_of(t):
        return pl.ds(pl.multiple_of(t * tile_rows, tile_rows), tile_rows)

    def x_tile(t):
        t = jnp.asarray(t, I32)
        return pltpu.make_async_copy(xs_ref.at[tile_rows_of(t)], xring.at[t % X_RING],
                                     xsem.at[t % X_RING])

    def y_tile(t):
        return pltpu.make_async_copy(yring.at[t % 2], ys_ref.at[tile_rows_of(t)], ysem.at[t % 2])

    for cp in weights(te_ref[0], 0):
        cp.start()
    for t in range(X_RING - 1):
        @pl.when(t < used)
        def _():
            x_tile(t).start()

    def step(i, carry):
        @pl.when(i + X_RING - 1 < used)
        def _():
            x_tile(i + X_RING - 1).start()

        e = te_ref[i]
        ws = grp_ref[i] % 2
        fresh = (i == 0) | (e != te_ref[jnp.maximum(i - 1, 0)])

        @pl.when(fresh)
        def _():
            for cp in weights(e, ws):
                cp.wait()

            @pl.when(nxt_ref[i] >= 0)
            def _():
                for cp in weights(nxt_ref[i], 1 - ws):
                    cp.start()

            wgu_bf[...] = wgu_f[ws].astype(BF16)
            wd_bf[...] = wd_f[ws].astype(BF16)

        x_tile(i).wait()

        @pl.when(i >= 2)
        def _():
            y_tile(i - 2).wait()

        x = jnp.concatenate([p.astype(BF16) for p in _load_slabs(xring.at[i % X_RING], ROW_TILE)],
                            axis=1)
        bgu = bgu_ref[pl.ds(e, 1), :]
        bd = bd_ref[pl.ds(e, 1), :]
        cw = D_FF // FF_CHUNKS
        for c in range(FF_CHUNKS):
            gate = jnp.dot(x, wgu_bf[:, c * cw:(c + 1) * cw], preferred_element_type=F32)
            gate = jnp.minimum(gate + bgu[:, c * cw:(c + 1) * cw], SWIGLU_LIMIT)
            up = jnp.dot(x, wgu_bf[:, D_FF + c * cw:D_FF + (c + 1) * cw], preferred_element_type=F32)
            up = jnp.clip(up + bgu[:, D_FF + c * cw:D_FF + (c + 1) * cw], -SWIGLU_LIMIT, SWIGLU_LIMIT)
            act = gate * jax.nn.sigmoid(SWIGLU_ALPHA * gate) * (up + 1.0)
            act_bf[:, c * cw:(c + 1) * cw] = act.astype(BF16)
        ow = D_MODEL // FF_CHUNKS
        for c in range(FF_CHUNKS):
            out = jnp.dot(act_bf[...], wd_bf[:, c * ow:(c + 1) * ow], preferred_element_type=F32)
            out = out + bd[:, c * ow:(c + 1) * ow]
            for j in range(ow // LANES):
                yring[i % 2, pl.ds(c * (ow // LANES) + j, ROW_TILE, stride=SLAB), :] = (
                    out[:, LANES * j:LANES * (j + 1)])
        y_tile(i).start()
        return carry

    lax.fori_loop(0, used, step, 0)

    @pl.when(used >= 2)
    def _():
        y_tile(used - 2).wait()
    y_tile(used - 1).wait()

    yring[0] = jnp.zeros(yring.shape[1:], F32)

    def clear(fn):
        def body(t, carry):
            fn(pltpu.make_async_copy(yring.at[0], ys_ref.at[tile_rows_of(t)], ysem.at[0]))
            return carry
        lax.fori_loop(used, n_tiles, body, 0)

    clear(lambda cp: cp.start())
    clear(lambda cp: cp.wait())


def _experts(xs, tile_expert, used, grp, nxt, w_gu, b_gu, w_down, b_down):
    rows = xs.shape[0] // SLAB
    nt = rows // ROW_TILE
    whole = lambda shape: pl.BlockSpec(shape, lambda i, *_: (0,) * len(shape))
    grid_spec = pltpu.PrefetchScalarGridSpec(
        num_scalar_prefetch=4,
        grid=(1,),
        in_specs=[pl.BlockSpec(memory_space=pl.ANY),
                  pl.BlockSpec(memory_space=pl.ANY),
                  whole((N_EXPERTS, 2 * D_FF)),
                  pl.BlockSpec(memory_space=pl.ANY),
                  whole((N_EXPERTS, D_MODEL))],
        out_specs=pl.BlockSpec(memory_space=pl.ANY),
        scratch_shapes=[pltpu.VMEM((2, D_MODEL, 2 * D_FF), F32), pltpu.VMEM((2, D_FF, D_MODEL), F32),
                        pltpu.VMEM((D_MODEL, 2 * D_FF), BF16), pltpu.VMEM((D_FF, D_MODEL), BF16),
                        pltpu.VMEM((ROW_TILE, D_FF), BF16),
                        pltpu.VMEM((X_RING, ROW_TILE * SLAB, LANES), F32),
                        pltpu.VMEM((2, ROW_TILE * SLAB, LANES), F32),
                        pltpu.SemaphoreType.DMA((2, 2)), pltpu.SemaphoreType.DMA((X_RING,)),
                        pltpu.SemaphoreType.DMA((2,))],
    )
    return pl.pallas_call(
        functools.partial(_experts_kernel, n_tiles=nt),
        grid_spec=grid_spec,
        out_shape=jax.ShapeDtypeStruct((rows * SLAB, LANES), F32),
        compiler_params=_cparams(("arbitrary",)),
        name="experts",
    )(tile_expert, used, grp, nxt, xs, w_gu, b_gu, w_down, b_down)


def _combine_kernel(dest_ref, next_ref, ys_ref, x1_ref, p_ref, g2_ref, fn_ref, y_ref, buf_ref, sem,
                    *, last):
    i = pl.program_id(0)
    slot = i % 2

    def issue(d_ref, sl):
        def body(r, carry):
            for k in range(TOP_K):
                d = d_ref[r * TOP_K + k]
                pltpu.make_async_copy(ys_ref.at[_slab_rows(d)], buf_ref.at[sl, k, _slab_rows(r)],
                                      sem.at[sl]).start(priority=k % 2)
            return carry
        lax.fori_loop(0, TOK_TILE, body, 0, unroll=2)

    @pl.when(i == 0)
    def _():
        issue(dest_ref, 0)

    @pl.when(i + 1 < pl.num_programs(0))
    def _():
        issue(next_ref, 1 - slot)

    for k in range(TOP_K):
        pltpu.make_async_copy(ys_ref.at[pl.ds(0, TOK_TILE * SLAB)], buf_ref.at[slot, k],
                              sem.at[slot]).wait()

    p = p_ref[...]
    groups = None
    for k in range(TOP_K):
        part = [p[:, k:k + 1] * g for g in _load_slabs(buf_ref.at[slot, k], TOK_TILE)]
        groups = part if groups is None else [a + b for a, b in zip(groups, part)]
    moe = jnp.concatenate(groups, axis=1)
    x2 = x1_ref[...] + g2_ref[...] * moe
    if last:
        x2 = x2 * lax.rsqrt(jnp.mean(x2 * x2, axis=-1, keepdims=True) + EPS) * fn_ref[...]
    y_ref[...] = x2


def _combine(ys, dest_flat, x1, probs, mod, per_token, seq, tile0, final_norm, last):
    n = x1.shape[0]
    tm = TOK_TILE
    tps = max(seq // tm, 1)
    nt = n // tm
    return pl.pallas_call(
        functools.partial(_combine_kernel, last=last),
        grid=(nt,),
        in_specs=[pl.BlockSpec((tm * TOP_K,), lambda i: (i + tile0,), memory_space=pltpu.SMEM),
                  pl.BlockSpec((tm * TOP_K,), lambda i: (jnp.minimum(i + 1, nt - 1) + tile0,),
                               memory_space=pltpu.SMEM),
                  pl.BlockSpec(memory_space=pl.ANY),
                  pl.BlockSpec((tm, D_MODEL), lambda i: (i, 0)),
                  pl.BlockSpec((tm, TOP_K), lambda i: (i + tile0, 0)),
                  _mod_spec(per_token, tm, tps, 5),
                  pl.BlockSpec((1, D_MODEL), lambda i: (0, 0))],
        out_specs=pl.BlockSpec((tm, D_MODEL), lambda i: (i, 0)),
        out_shape=jax.ShapeDtypeStruct((n, D_MODEL), F32),
        scratch_shapes=[pltpu.VMEM((2, TOP_K, tm * SLAB, LANES), F32),
                        pltpu.SemaphoreType.DMA((2,))],
        compiler_params=_cparams(("arbitrary",)),
        name="combine",
    )(dest_flat, dest_flat, ys, x1, probs, mod, final_norm.reshape(1, D_MODEL))


def kernel(x_prompt, x_sample, state_hgrn, state_pool, c_prompt, c_sample, w_ada, b_ada, norm1,
           norm2, w_in, lower_bounds, hgrn_norm, w_pool, pool_scale, w_out, w_router, b_router,
           w_gu, b_gu, w_down, b_down, final_norm):
    bp, seq_p, _ = x_prompt.shape
    bs, seq_s, _ = x_sample.shape
    np_, ns = bp * seq_p, bs * seq_s
    depth = w_ada.shape[0]

    lbs = jnp.cumsum(jax.nn.softmax(lower_bounds.astype(F32), axis=0), axis=0)
    xp = x_prompt.reshape(np_, D_MODEL)
    xs_ = x_sample.reshape(ns, D_MODEL)
    c_all = jnp.concatenate([c_prompt, c_sample], axis=0)

    sp_l, hp_l, ss_l, hs_l = [], [], [], []
    for l in range(depth):
        w_in_bf = w_in[l].astype(BF16)
        w_out_bf = w_out[l].astype(BF16)
        w_pool_bf = w_pool[l].astype(BF16)
        w_router_bf = w_router[l].astype(BF16)
        lb = lbs[l].reshape(1, D_A)
        hn = hgrn_norm[l].reshape(1, D_A)
        psc = pool_scale[l].reshape(1, D_B)

        mod = _ada(c_all, w_ada[l], b_ada[l])
        mod_p = mod[:bp].reshape(bp, 1, 6 * D_MODEL)
        mod_s = jnp.repeat(mod[bp:], seq_s, axis=0)

        up = _inproj(xp, mod_p, False, seq_p, norm1[l], w_in_bf)
        us = _inproj(xs_, mod_s, True, seq_s, norm1[l], w_in_bf)

        oa_p, s_p = _hgrn(up.reshape(bp, seq_p, D_IN), None, lb, hn)
        oa_p = oa_p.reshape(np_, D_A)
        oa_s, s_s = _hgrn(us, state_hgrn[l], lb, hn, seq=seq_s)
        ob_p, h_p = _pool_prompt(up, bp, seq_p, w_pool_bf, psc)
        ob_t, h_t = _pool_sample(us.reshape(bs, seq_s, D_IN).transpose(1, 0, 2),
                                 state_pool[l].transpose(1, 0, 2), PAST_LEN, w_pool_bf, psc)
        ob_s, h_s = ob_t.transpose(1, 0, 2).reshape(ns, D_B), h_t.transpose(1, 0, 2)

        n_tok = np_ + ns
        n_tiles = (n_tok * TOP_K) // ROW_TILE + N_EXPERTS
        x1p, *joint = _outproj(xp, oa_p, ob_p, mod_p, False, seq_p, norm2[l], w_out_bf,
                               w_router_bf, b_router[l], n_tok, 0)
        x1s, h2, top_idx, probs = _outproj(xs_, oa_s, ob_s, mod_s, True, seq_s, norm2[l], w_out_bf,
                                           w_router_bf, b_router[l], n_tok, np_ // TOK_TILE, joint)

        dest, counts = _rank(top_idx)
        dest_flat = dest.reshape(n_tok * TOP_K)
        tiles_e = (counts.reshape(N_EXPERTS) + ROW_TILE - 1) // ROW_TILE
        ends = jnp.cumsum(tiles_e)
        used = ends[-1:].astype(I32)
        tile_expert = jnp.minimum(
            jnp.sum(jnp.arange(n_tiles, dtype=I32)[:, None] >= ends[None, :], axis=1),
            N_EXPERTS - 1).astype(I32)
        last_tile = (ends - 1).astype(I32)
        grp = jnp.cumsum(jnp.concatenate(
            [jnp.zeros((1,), I32), (tile_expert[1:] != tile_expert[:-1]).astype(I32)])).astype(I32)
        eids = jnp.arange(N_EXPERTS, dtype=I32)
        later = (eids[None, :] > eids[:, None]) & (tiles_e > 0)[None, :]
        nxt_e = jnp.min(jnp.where(later, eids[None, :], N_EXPERTS), axis=1)
        nxt = jnp.where(nxt_e < N_EXPERTS, nxt_e, -1)[tile_expert].astype(I32)

        xsort = _dispatch(h2, dest_flat, last_tile, tiles_e.astype(I32), used, n_tiles)
        ysort = _experts(xsort, tile_expert, used, grp, nxt, w_gu[l], b_gu[l], w_down[l], b_down[l])

        last = l == depth - 1
        xp = _combine(ysort, dest_flat, x1p, probs, mod_p, False, seq_p, 0, final_norm, last)
        xs_ = _combine(ysort, dest_flat, x1s, probs, mod_s, True, seq_s, np_ // TOK_TILE,
                       final_norm, last)

        sp_l.append(s_p)
        hp_l.append(h_p)
        ss_l.append(s_s)
        hs_l.append(h_s)

    return (xp.reshape(bp, seq_p, D_MODEL), xs_.reshape(bs, seq_s, D_MODEL),
            jnp.stack(sp_l), jnp.stack(hp_l), jnp.stack(ss_l), jnp.stack(hs_l))
```

```python
import functools

import jax
import jax.numpy as jnp
from jax import lax
from jax.experimental import pallas as pl
from jax.experimental.pallas import tpu as pltpu

F32, BF16, I32 = jnp.float32, jnp.bfloat16, jnp.int32

D_MODEL = 1024
D_A = 512
H_A = 4
DK = 128
DV = 128
D_B = 512
POOL_WINDOWS = (2, 4, 8, 16)
POOL_GROUP = 128
POOL_HIST = 15
D_IN = 4 * D_A + D_B
N_EXPERTS = 32
TOP_K = 4
D_FF = 1024
SWIGLU_LIMIT = 7.0
SWIGLU_ALPHA = 1.702
EPS = 1e-6
PAST_LEN = 16384

SUB = 16
HALF = SUB // 2
HGRN_SEQS = 8
ROW_TILE = 256
TOK_TILE = 256
FF_CHUNKS = 4
X_RING = 3
LANES = 128
SLAB = D_MODEL // LANES
VMEM_LIMIT = 56 * 1024 * 1024


def _cparams(sem):
    return pltpu.CompilerParams(dimension_semantics=sem, vmem_limit_bytes=VMEM_LIMIT)


def _silu(x):
    return x * jax.nn.sigmoid(x)


def _mod_spec(per_token, tm, tiles_per_seq, col, last_tile=None):
    tile = (lambda i: i) if last_tile is None else (lambda i: jnp.minimum(i, last_tile))
    if per_token:
        return pl.BlockSpec((tm, D_MODEL), lambda i: (tile(i), col))
    return pl.BlockSpec((None, 1, D_MODEL), lambda i: (tile(i) // tiles_per_seq, 0, col))


def _ada_kernel(c_ref, w_ref, b_ref, o_ref):
    c = c_ref[...]
    o_ref[...] = jnp.dot(_silu(c).astype(BF16), w_ref[...].astype(BF16),
                         preferred_element_type=F32) + b_ref[...]


def _ada(c_all, w_ada, b_ada):
    rows = c_all.shape[0]
    n = w_ada.shape[1]
    return pl.pallas_call(
        _ada_kernel,
        grid=(n // D_MODEL,),
        in_specs=[pl.BlockSpec((rows, D_MODEL), lambda j: (0, 0)),
                  pl.BlockSpec((D_MODEL, D_MODEL), lambda j: (0, j)),
                  pl.BlockSpec((1, D_MODEL), lambda j: (0, j))],
        out_specs=pl.BlockSpec((rows, D_MODEL), lambda j: (0, j)),
        out_shape=jax.ShapeDtypeStruct((rows, n), F32),
        compiler_params=_cparams(("arbitrary",)),
        name="ada",
    )(c_all, w_ada, b_ada.reshape(1, n))


def _inproj_kernel(x_ref, sh_ref, sc_ref, n1_ref, w_ref, u_ref):
    x = x_ref[...]
    h = x * lax.rsqrt(jnp.mean(x * x, axis=-1, keepdims=True) + EPS) * n1_ref[...]
    h = h * (1.0 + sc_ref[...]) + sh_ref[...]
    u_ref[...] = jnp.dot(h.astype(BF16), w_ref[...], preferred_element_type=F32)


def _inproj(x, mod, per_token, seq, norm1, w_in_bf):
    n = x.shape[0]
    tm = min(512, n)
    tps = max(seq // tm, 1)
    return pl.pallas_call(
        _inproj_kernel,
        grid=(n // tm,),
        in_specs=[pl.BlockSpec((tm, D_MODEL), lambda i: (i, 0)),
                  _mod_spec(per_token, tm, tps, 0),
                  _mod_spec(per_token, tm, tps, 1),
                  pl.BlockSpec((1, D_MODEL), lambda i: (0, 0)),
                  pl.BlockSpec((D_MODEL, D_IN), lambda i: (0, 0))],
        out_specs=pl.BlockSpec((tm, D_IN), lambda i: (i, 0)),
        out_shape=jax.ShapeDtypeStruct((n, D_IN), F32),
        compiler_params=_cparams(("arbitrary",)),
        name="inproj",
    )(x, mod, mod, norm1.reshape(1, D_MODEL), w_in_bf)


def _block_select():
    r = lax.broadcasted_iota(I32, (SUB, SUB * DK), 0)
    c = lax.broadcasted_iota(I32, (SUB, SUB * DK), 1) >> (DK.bit_length() - 1)
    return (r == c).astype(BF16)


def _hgrn_step(q, fp, iv, g, lb, hn, st_ref, sel_ref, cm_ref, b_ref, k_ref, valid):
    nrow = HGRN_SEQS * SUB
    f = lb + (1.0 - lb) * jax.nn.sigmoid(fp)
    lf = jnp.log2(f)
    kk = 1.0 - f
    if valid < SUB:
        live = (lax.broadcasted_iota(I32, (nrow, D_A), 0) & (SUB - 1)) < valid
        lf = jnp.where(live, lf, 0.0)
        kk = jnp.where(live, kk, 0.0)
    qa = _silu(q)

    hi = lf.astype(BF16)
    lo = (lf - hi.astype(F32)).astype(BF16)
    cum, tot = cm_ref[0], cm_ref[1]
    b = jnp.dot(cum, hi, preferred_element_type=F32) + jnp.dot(cum, lo, preferred_element_type=F32)
    be = jnp.dot(tot, hi, preferred_element_type=F32) + jnp.dot(tot, lo, preferred_element_type=F32)
    qt = (qa * jnp.exp2(b)).astype(BF16)
    kh = (kk * jnp.exp2(be - b)).astype(BF16)
    dec = jnp.exp2(be)
    ivb = iv.astype(BF16)
    for h in range(H_A):
        b_ref[h] = b[:, DK * h:DK * (h + 1)]
        k_ref[h] = kk[:, DK * h:DK * (h + 1)]

    zero_half = jnp.zeros((HALF, D_A), F32)
    tr = lax.broadcasted_iota(I32, (H_A * SUB, SUB), 0) & (SUB - 1)
    tc = lax.broadcasted_iota(I32, (H_A * SUB, SUB), 1)
    sg = _silu(g)

    own = ((lax.broadcasted_iota(I32, (nrow, HGRN_SEQS * DK), 0) >> (SUB.bit_length() - 1))
           == (lax.broadcasted_iota(I32, (nrow, HGRN_SEQS * DK), 1) >> (DK.bit_length() - 1)))
    inter = [None] * (HGRN_SEQS * H_A)
    for h in range(H_A):
        sl = slice(DK * h, DK * (h + 1))
        kbd = jnp.where(own, jnp.concatenate([kh[:, sl]] * HGRN_SEQS, axis=1), jnp.zeros((), BF16))
        ut_all = lax.dot_general(ivb[:, sl], kbd, (((0,), (0,)), ((), ())),
                                 preferred_element_type=F32)
        for n in range(HGRN_SEQS):
            rows = slice(n * SUB, (n + 1) * SUB)
            st = st_ref[n, h]
            inter[n * H_A + h] = lax.dot_general(qt[rows, sl], st.astype(BF16),
                                                 (((1,), (1,)), ((), ())),
                                                 preferred_element_type=F32)
            dtile = dec[n * SUB:n * SUB + HALF, sl]
            st_ref[n, h] = ((st.reshape(DV // HALF, HALF, DK) * dtile[None]).reshape(DV, DK)
                            + ut_all[:, DK * n:DK * (n + 1)])

    scores = []
    for n in range(HGRN_SEQS):
        per_s = []
        for s in range(SUB):
            row = n * SUB + s
            b_s = jnp.concatenate([b_ref[h, pl.ds(row, HALF, stride=0), :] for h in range(H_A)], axis=1)
            k_s = jnp.concatenate([k_ref[h, pl.ds(row, HALF, stride=0), :] for h in range(H_A)], axis=1)
            halves = []
            for hf in range(2):
                if hf == 0 and s >= HALF:
                    halves.append(zero_half)
                    continue
                r0 = n * SUB + hf * HALF
                e = jnp.exp2(jnp.minimum(b[r0:r0 + HALF] - b_s, 0.0))
                halves.append(e * (qa[r0:r0 + HALF] * k_s))
            per_s.append(jnp.concatenate(halves, axis=0).astype(BF16))
        pcat = jnp.concatenate(
            [jnp.concatenate([p[:, DK * h:DK * (h + 1)] for p in per_s], axis=1) for h in range(H_A)],
            axis=0)
        a = lax.dot_general(pcat, sel_ref[...], (((1,), (1,)), ((), ())),
                            preferred_element_type=F32)
        scores.append(jnp.where(tr >= tc, a, 0.0).astype(BF16))

    outs = []
    for n in range(HGRN_SEQS):
        rows = slice(n * SUB, (n + 1) * SUB)
        heads = []
        for h in range(H_A):
            sl = slice(DK * h, DK * (h + 1))
            o = inter[n * H_A + h] + jnp.dot(scores[n][h * SUB:(h + 1) * SUB, :], ivb[rows, sl],
                                             preferred_element_type=F32)
            o = o * lax.rsqrt(jnp.mean(o * o, axis=-1, keepdims=True) + EPS) * hn[:, sl]
            heads.append(o)
        outs.append(jnp.concatenate(heads, axis=1) * sg[rows])
    return jnp.concatenate(outs, axis=0)


def _hgrn_kernel(*refs, rows, has_s0):
    if has_s0:
        (q_ref, f_ref, i_ref, g_ref, lb_ref, hn_ref, s0_ref, o_ref, s_ref,
         st_ref, sel_ref, cm_ref, b_ref, k_ref, pad_ref) = refs
    else:
        (q_ref, f_ref, i_ref, g_ref, lb_ref, hn_ref, o_ref, s_ref,
         st_ref, sel_ref, cm_ref, b_ref, k_ref) = refs
    t = pl.program_id(1)
    nrow = HGRN_SEQS * SUB

    @pl.when((t == 0) & (pl.program_id(0) == 0))
    def _():
        sel_ref[...] = _block_select()
        r = lax.broadcasted_iota(I32, (nrow, nrow), 0)
        c = lax.broadcasted_iota(I32, (nrow, nrow), 1)
        shift = SUB.bit_length() - 1
        same = (r >> shift) == (c >> shift)
        cm_ref[0] = (same & (r >= c)).astype(BF16)
        cm_ref[1] = same.astype(BF16)

    lb = lb_ref[...]
    hn = hn_ref[...]
    if has_s0:
        for n in range(HGRN_SEQS):
            for h in range(H_A):
                st_ref[n, h] = s0_ref[n, h].T
        pad_ref[...] = jnp.zeros_like(pad_ref)
        for j, ref in enumerate((q_ref, f_ref, i_ref, g_ref)):
            for n in range(HGRN_SEQS):
                pad_ref[j, n, 0:rows, :] = ref[n * rows:(n + 1) * rows, :]
        o = _hgrn_step(*(pad_ref[j].reshape(nrow, D_A) for j in range(4)), lb, hn,
                       st_ref, sel_ref, cm_ref, b_ref, k_ref, rows)
        for n in range(HGRN_SEQS):
            o_ref[n * rows:(n + 1) * rows, :] = o[n * SUB:n * SUB + rows, :]
    else:
        @pl.when(t == 0)
        def _():
            st_ref[...] = jnp.zeros_like(st_ref)

        def body(j, carry):
            rs = pl.ds(pl.multiple_of(j * SUB, SUB), SUB)
            o = _hgrn_step(*(ref[:, rs, :].reshape(nrow, D_A) for ref in (q_ref, f_ref, i_ref, g_ref)),
                           lb, hn, st_ref, sel_ref, cm_ref, b_ref, k_ref, SUB)
            o_ref[:, rs, :] = o.reshape(HGRN_SEQS, SUB, D_A)
            return carry

        lax.fori_loop(0, rows // SUB, body, 0)

    def emit_state():
        for n in range(HGRN_SEQS):
            for h in range(H_A):
                s_ref[n, h] = st_ref[n, h].T

    if has_s0:
        emit_state()
    else:
        pl.when(t == pl.num_programs(1) - 1)(emit_state)


def _hgrn(u3, s0, lb, hn, seq=None):
    has_s0 = s0 is not None
    if has_s0:
        batch = u3.shape[0] // seq
        rows = seq
        grid = (batch // HGRN_SEQS, 1)
        blk = lambda cb: pl.BlockSpec((HGRN_SEQS * rows, D_A), lambda b, t: (b, cb))
        o_shape = (batch * seq, D_A)
    else:
        batch, seq, _ = u3.shape
        rows = 128
        grid = (batch // HGRN_SEQS, seq // rows)
        blk = lambda cb: pl.BlockSpec((HGRN_SEQS, rows, D_A), lambda b, t: (b, t, cb))
        o_shape = (batch, seq, D_A)
    sblk = pl.BlockSpec((HGRN_SEQS, H_A, DK, DV), lambda b, t: (b, 0, 0, 0))
    vec = pl.BlockSpec((1, D_A), lambda b, t: (0, 0))
    nrow = HGRN_SEQS * SUB
    scratch = [pltpu.VMEM((HGRN_SEQS, H_A, DV, DK), F32), pltpu.VMEM((SUB, SUB * DK), BF16),
               pltpu.VMEM((2, nrow, nrow), BF16), pltpu.VMEM((H_A, nrow, DK), F32),
               pltpu.VMEM((H_A, nrow, DK), F32)]
    if has_s0:
        scratch.append(pltpu.VMEM((4, HGRN_SEQS, SUB, D_A), F32))
    args = (u3, u3, u3, u3, lb, hn) + ((s0,) if has_s0 else ())
    return pl.pallas_call(
        functools.partial(_hgrn_kernel, rows=rows, has_s0=has_s0),
        grid=grid,
        in_specs=[blk(0), blk(1), blk(2), blk(3), vec, vec] + ([sblk] if has_s0 else []),
        out_specs=[blk(0), sblk],
        out_shape=[jax.ShapeDtypeStruct(o_shape, F32),
                   jax.ShapeDtypeStruct((batch, H_A, DK, DV), F32)],
        scratch_shapes=scratch,
        compiler_params=_cparams(("arbitrary", "arbitrary")),
        name="hgrn_state" if has_s0 else "hgrn_scan",
    )(*args)


def _pool_groups(full_ref, base, tm, cnt_fn, wp_ref, ps_ref, step=1):
    outs = []
    for gi, w in enumerate(POOL_WINDOWS):
        ch = slice(POOL_GROUP * gi, POOL_GROUP * (gi + 1))
        v = full_ref[base:base + tm, ch]
        s = v
        for j in range(1, w):
            s = s + full_ref[base - j * step:base - j * step + tm, ch]
        d = s / cnt_fn(w) - v
        outs.append(jnp.dot(d.astype(BF16), wp_ref[gi], preferred_element_type=F32))
    return jnp.concatenate(outs, axis=1) * ps_ref[...]


def _pool_prompt_kernel(v_ref, prev_ref, wp_ref, ps_ref, o_ref, hist_ref, full_ref, *, tm):
    t = pl.program_id(1)
    prev = prev_ref[...]
    full_ref[0:16, :] = jnp.where(t == 0, 0.0, prev)
    full_ref[16:16 + tm, :] = v_ref[...]
    pos = (lax.broadcasted_iota(I32, (tm, 1), 0) + t * tm + 1).astype(F32)
    o_ref[...] = _pool_groups(full_ref, 16, tm, lambda w: jnp.minimum(pos, float(w)),
                              wp_ref, ps_ref)

    @pl.when(t == pl.num_programs(1) - 1)
    def _():
        hist_ref[...] = full_ref[16 + tm - POOL_HIST:16 + tm, :]


def _pool_prompt(u, batch, seq, w_pool_bf, pool_scale):
    tm = min(1024, seq)
    nt = seq // tm
    vcol = (4 * D_A) // D_B
    return pl.pallas_call(
        functools.partial(_pool_prompt_kernel, tm=tm),
        grid=(batch, nt),
        in_specs=[pl.BlockSpec((tm, D_B), lambda b, t: (b * nt + t, vcol)),
                  pl.BlockSpec((16, D_B),
                               lambda b, t: (jnp.maximum((b * nt + t) * (tm // 16) - 1, 0), vcol)),
                  pl.BlockSpec((len(POOL_WINDOWS), POOL_GROUP, POOL_GROUP), lambda b, t: (0, 0, 0)),
                  pl.BlockSpec((1, D_B), lambda b, t: (0, 0))],
        out_specs=[pl.BlockSpec((tm, D_B), lambda b, t: (b * nt + t, 0)),
                   pl.BlockSpec((None, POOL_HIST, D_B), lambda b, t: (b, 0, 0))],
        out_shape=[jax.ShapeDtypeStruct((batch * seq, D_B), F32),
                   jax.ShapeDtypeStruct((batch, POOL_HIST, D_B), F32)],
        scratch_shapes=[pltpu.VMEM((16 + tm, D_B), F32)],
        compiler_params=_cparams(("arbitrary", "arbitrary")),
        name="pool_prompt",
    )(u, u, w_pool_bf, pool_scale)


def _pool_sample_kernel(v_ref, hist_ref, wp_ref, ps_ref, o_ref, nh_ref, full_ref, *,
                        seq, batch, start):
    full_ref[0:batch, :] = jnp.zeros((batch, D_B), F32)
    full_ref[batch:16 * batch, :] = hist_ref[...].reshape(POOL_HIST * batch, D_B)
    full_ref[16 * batch:(16 + seq) * batch, :] = v_ref[...].reshape(seq * batch, D_B)

    def cnt(w):
        rows = [jnp.full((batch, 1), float(min(start + t + 1, w)), F32) for t in range(seq)]
        return jnp.concatenate(rows, axis=0)

    o = _pool_groups(full_ref, 16 * batch, seq * batch, cnt, wp_ref, ps_ref, step=batch)
    o_ref[...] = o.reshape(seq, batch, D_B)
    nh_ref[...] = full_ref[(16 + seq - POOL_HIST) * batch:(16 + seq) * batch, :].reshape(
        POOL_HIST, batch, D_B)


def _pool_sample(u_t, hist_t, start, w_pool_bf, pool_scale):
    seq, batch, _ = u_t.shape
    vcol = (4 * D_A) // D_B
    return pl.pallas_call(
        functools.partial(_pool_sample_kernel, seq=seq, batch=batch, start=start),
        grid=(1,),
        in_specs=[pl.BlockSpec((seq, batch, D_B), lambda i: (0, 0, vcol)),
                  pl.BlockSpec((POOL_HIST, batch, D_B), lambda i: (0, 0, 0)),
                  pl.BlockSpec((len(POOL_WINDOWS), POOL_GROUP, POOL_GROUP), lambda i: (0, 0, 0)),
                  pl.BlockSpec((1, D_B), lambda i: (0, 0))],
        out_specs=[pl.BlockSpec((seq, batch, D_B), lambda i: (0, 0, 0)),
                   pl.BlockSpec((POOL_HIST, batch, D_B), lambda i: (0, 0, 0))],
        out_shape=[jax.ShapeDtypeStruct((seq, batch, D_B), F32),
                   jax.ShapeDtypeStruct((POOL_HIST, batch, D_B), F32)],
        scratch_shapes=[pltpu.VMEM(((16 + seq) * batch, D_B), F32)],
        compiler_params=_cparams(("arbitrary",)),
        name="pool_sample",
    )(u_t, hist_t, w_pool_bf, pool_scale)


def _store_slabs(ref, value):
    n = value.shape[0]
    for j in range(SLAB):
        ref[pl.ds(j, n, stride=SLAB), :] = value[:, LANES * j:LANES * (j + 1)]


def _load_slabs(ref, n):
    return [ref[pl.ds(j, n, stride=SLAB), :] for j in range(SLAB)]


def _slab_rows(row):
    return pl.ds(pl.multiple_of(row * SLAB, SLAB), SLAB)


def _outproj_kernel(x_ref, oa_ref, ob_ref, g1_ref, sh2_ref, sc2_ref, n2_ref, woa_ref, wob_ref,
                    wr_ref, br_ref, *rest):
    x1_ref, h2_ref, idx_ref, prob_ref, lg_ref = rest[-5:]
    i = pl.program_id(0)
    slot = i % 2

    @pl.when(i == 0)
    def _():
        lg_ref[...] = jnp.zeros_like(lg_ref)

    logits = lg_ref[1 - slot]

    mix = (jnp.dot(oa_ref[...].astype(BF16), woa_ref[...], preferred_element_type=F32)
           + jnp.dot(ob_ref[...].astype(BF16), wob_ref[...], preferred_element_type=F32))
    x1 = x_ref[...] + g1_ref[...] * mix
    x1_ref[...] = x1
    h2 = x1 * lax.rsqrt(jnp.mean(x1 * x1, axis=-1, keepdims=True) + EPS) * n2_ref[...]
    h2 = h2 * (1.0 + sc2_ref[...]) + sh2_ref[...]
    _store_slabs(h2_ref, h2)
    lg_ref[slot] = jnp.dot(h2.astype(BF16), wr_ref[...], preferred_element_type=F32) + br_ref[...]

    tm = logits.shape[0]
    lane = lax.broadcasted_iota(I32, (tm, N_EXPERTS), 1)
    kcol = lax.broadcasted_iota(I32, (tm, TOP_K), 1)
    vals = jnp.zeros((tm, TOP_K), F32)
    idxs = jnp.zeros((tm, TOP_K), I32)
    for k in range(TOP_K):
        m = jnp.max(logits, axis=-1, keepdims=True)
        am = jnp.min(jnp.where(logits == m, lane, N_EXPERTS), axis=-1, keepdims=True)
        vals = jnp.where(kcol == k, m, vals)
        idxs = jnp.where(kcol == k, am, idxs)
        logits = jnp.where(lane == am, -jnp.inf, logits)
    e = jnp.exp(vals - vals[:, 0:1])
    prob_ref[...] = e / jnp.sum(e, axis=-1, keepdims=True)
    idx_ref[...] = idxs


def _outproj(x, o_a, o_b, mod, per_token, seq, norm2, w_out_bf, w_router_bf, b_router,
             n_total, tile0, joint=None):
    n = x.shape[0]
    tm = min(256, n)
    tps = max(seq // tm, 1)
    nt = n // tm
    cur = lambda i: jnp.minimum(i, nt - 1)
    prv = lambda i: jnp.maximum(i - 1, 0)
    row = lambda width: pl.BlockSpec((tm, width), lambda i: (cur(i), 0))
    jrow = lambda width: pl.BlockSpec((tm, width), lambda i: (prv(i) + tile0, 0))
    const = lambda shape: pl.BlockSpec(shape, lambda i: (0,) * len(shape))
    joint = () if joint is None else tuple(joint)
    n_in = 11
    return pl.pallas_call(
        _outproj_kernel,
        grid=(nt + 1,),
        in_specs=[row(D_MODEL), row(D_A), row(D_B),
                  _mod_spec(per_token, tm, tps, 2, nt - 1),
                  _mod_spec(per_token, tm, tps, 3, nt - 1),
                  _mod_spec(per_token, tm, tps, 4, nt - 1),
                  const((1, D_MODEL)),
                  pl.BlockSpec((D_A, D_MODEL), lambda i: (0, 0)),
                  pl.BlockSpec((D_B, D_MODEL), lambda i: (1, 0)),
                  const((D_MODEL, N_EXPERTS)), const((1, N_EXPERTS))]
                 + [pl.BlockSpec(memory_space=pl.ANY)] * len(joint),
        out_specs=[row(D_MODEL), pl.BlockSpec((tm * SLAB, LANES), lambda i: (cur(i) + tile0, 0)),
                   jrow(TOP_K), jrow(TOP_K)],
        out_shape=[jax.ShapeDtypeStruct((n, D_MODEL), F32),
                   jax.ShapeDtypeStruct((n_total * SLAB, LANES), F32),
                   jax.ShapeDtypeStruct((n_total, TOP_K), I32),
                   jax.ShapeDtypeStruct((n_total, TOP_K), F32)],
        input_output_aliases={n_in + j: 1 + j for j in range(len(joint))},
        scratch_shapes=[pltpu.VMEM((2, tm, N_EXPERTS), F32)],
        compiler_params=_cparams(("arbitrary",)),
        name="outproj",
    )(x, o_a, o_b, mod, mod, mod, norm2.reshape(1, D_MODEL), w_out_bf, w_out_bf,
      w_router_bf, b_router.reshape(1, N_EXPERTS), *joint)


def _rank_kernel(idx_ref, dest_ref, cnt_ref, carry_ref, *, te):
    phase = pl.program_id(0)
    i = pl.program_id(1)
    idx = idx_ref[...]
    lane = lax.broadcasted_iota(I32, (te, N_EXPERTS), 1)
    onehots = [(idx[:, k:k + 1] == lane) for k in range(TOP_K)]
    member = jnp.zeros((te, N_EXPERTS), F32)
    for oh in onehots:
        member = member + oh.astype(F32)
    colsum = jnp.sum(member, axis=0, keepdims=True)

    @pl.when((phase == 0) & (i == 0))
    def _():
        carry_ref[...] = jnp.zeros_like(carry_ref)

    @pl.when(phase == 0)
    def _():
        carry_ref[...] += colsum

    @pl.when((phase == 1) & (i == 0))
    def _():
        cnt = carry_ref[...]
        cnt_ref[...] = cnt.astype(I32)
        tiles = jnp.floor((cnt + (ROW_TILE - 1)) * (1.0 / ROW_TILE))
        r = lax.broadcasted_iota(I32, (N_EXPERTS, N_EXPERTS), 0)
        c = lax.broadcasted_iota(I32, (N_EXPERTS, N_EXPERTS), 1)
        before = (r < c).astype(BF16)
        carry_ref[...] = ROW_TILE * jnp.dot(tiles.astype(BF16), before, preferred_element_type=F32)

    @pl.when(phase == 1)
    def _():
        r = lax.broadcasted_iota(I32, (te, te), 0)
        c = lax.broadcasted_iota(I32, (te, te), 1)
        earlier = (r > c).astype(BF16)
        base = carry_ref[...] + jnp.dot(earlier, member.astype(BF16), preferred_element_type=F32)
        kcol = lax.broadcasted_iota(I32, (te, TOP_K), 1)
        dest = jnp.zeros((te, TOP_K), F32)
        for k, oh in enumerate(onehots):
            dk = jnp.sum(jnp.where(oh, base, 0.0), axis=-1, keepdims=True)
            dest = jnp.where(kcol == k, dk, dest)
        dest_ref[...] = dest.astype(I32)
        carry_ref[...] += colsum


def _rank(top_idx):
    n = top_idx.shape[0]
    te = 512
    return pl.pallas_call(
        functools.partial(_rank_kernel, te=te),
        grid=(2, n // te),
        in_specs=[pl.BlockSpec((te, TOP_K), lambda p, i: (i, 0))],
        out_specs=[pl.BlockSpec((te, TOP_K), lambda p, i: (p * i, 0)),
                   pl.BlockSpec((1, N_EXPERTS), lambda p, i: (0, 0))],
        out_shape=[jax.ShapeDtypeStruct((n, TOP_K), I32),
                   jax.ShapeDtypeStruct((1, N_EXPERTS), I32)],
        scratch_shapes=[pltpu.VMEM((1, N_EXPERTS), F32)],
        compiler_params=_cparams(("arbitrary", "arbitrary")),
        name="rank",
    )(top_idx)


def _dispatch_kernel(last_ref, ntile_ref, used_ref, dest_ref, h_ref, xs_ref, zero_ref, sem, zsem,
                     *, n_tiles):
    i = pl.program_id(0)

    @pl.when(i == 0)
    def _():
        zero_ref[...] = jnp.zeros_like(zero_ref)

        def clear(tile):
            rows = pl.ds(pl.multiple_of(tile * (ROW_TILE * SLAB), ROW_TILE * SLAB), ROW_TILE * SLAB)
            return pltpu.make_async_copy(zero_ref, xs_ref.at[rows], zsem)

        def over_tail(fn):
            def body(t, carry):
                fn(clear(t))
                return carry
            lax.fori_loop(used_ref[0], n_tiles, body, 0)

        for fn in (lambda c: c.start(), lambda c: c.wait()):
            for e in range(N_EXPERTS):
                @pl.when(ntile_ref[e] > 0)
                def _():
                    fn(clear(last_ref[e]))
            over_tail(fn)

    def body(r, carry):
        for k in range(TOP_K):
            d = dest_ref[r, k]
            pltpu.make_async_copy(h_ref.at[_slab_rows(r)], xs_ref.at[_slab_rows(d)], sem).start(
                priority=k % 2)
        return carry

    lax.fori_loop(0, TOK_TILE, body, 0, unroll=2)
    for k in range(TOP_K):
        pltpu.make_async_copy(h_ref, xs_ref.at[pl.ds(0, TOK_TILE * SLAB)], sem).wait()


def _dispatch(h2, dest_flat, last_tile, ntiles, used, n_tiles):
    n = h2.shape[0] // SLAB
    grid_spec = pltpu.PrefetchScalarGridSpec(
        num_scalar_prefetch=3,
        grid=(n // TOK_TILE,),
        in_specs=[pl.BlockSpec((TOK_TILE, TOP_K), lambda i, *_: (i, 0), memory_space=pltpu.SMEM),
                  pl.BlockSpec((TOK_TILE * SLAB, LANES), lambda i, *_: (i, 0))],
        out_specs=pl.BlockSpec(memory_space=pl.ANY),
        scratch_shapes=[pltpu.VMEM((ROW_TILE * SLAB, LANES), F32),
                        pltpu.SemaphoreType.DMA(()), pltpu.SemaphoreType.DMA(())],
    )
    return pl.pallas_call(
        functools.partial(_dispatch_kernel, n_tiles=n_tiles),
        grid_spec=grid_spec,
        out_shape=jax.ShapeDtypeStruct((n_tiles * ROW_TILE * SLAB, LANES), F32),
        compiler_params=_cparams(("arbitrary",)),
        name="dispatch",
    )(last_tile, ntiles, used, dest_flat, h2)


def _experts_kernel(te_ref, used_ref, grp_ref, nxt_ref, xs_ref, wgu_ref, bgu_ref, wd_ref, bd_ref,
                    ys_ref, wgu_f, wd_f, wgu_bf, wd_bf, act_bf, xring, yring, wsem, xsem, ysem,
                    *, n_tiles):
    used = used_ref[0]
    tile_rows = ROW_TILE * SLAB

    def weights(e, ws):
        return (pltpu.make_async_copy(wgu_ref.at[e], wgu_f.at[ws], wsem.at[ws, 0]),
                pltpu.make_async_copy(wd_ref.at[e], wd_f.at[ws], wsem.at[ws, 1]))

    def tile_rows_of(t):
        return pl.ds(pl.multiple_of(t * tile_rows, tile_rows), tile_rows)

    def x_tile(t):
        t = jnp.asarray(t, I32)
        return pltpu.make_async_copy(xs_ref.at[tile_rows_of(t)], xring.at[t % X_RING],
                                     xsem.at[t % X_RING])

    def y_tile(t):
        return pltpu.make_async_copy(yring.at[t % 2], ys_ref.at[tile_rows_of(t)], ysem.at[t % 2])

    for cp in weights(te_ref[0], 0):
        cp.start()
    for t in range(X_RING - 1):
        @pl.when(t < used)
        def _():
            x_tile(t).start()

    def step(i, carry):
        @pl.when(i + X_RING - 1 < used)
        def _():
            x_tile(i + X_RING - 1).start()

        e = te_ref[i]
        ws = grp_ref[i] % 2
        fresh = (i == 0) | (e != te_ref[jnp.maximum(i - 1, 0)])

        @pl.when(fresh)
        def _():
            for cp in weights(e, ws):
                cp.wait()

            @pl.when(nxt_ref[i] >= 0)
            def _():
                for cp in weights(nxt_ref[i], 1 - ws):
                    cp.start()

            wgu_bf[...] = wgu_f[ws].astype(BF16)
            wd_bf[...] = wd_f[ws].astype(BF16)

        x_tile(i).wait()

        @pl.when(i >= 2)
        def _():
            y_tile(i - 2).wait()

        x = jnp.concatenate([p.astype(BF16) for p in _load_slabs(xring.at[i % X_RING], ROW_TILE)],
                            axis=1)
        bgu = bgu_ref[pl.ds(e, 1), :]
        bd = bd_ref[pl.ds(e, 1), :]
        cw = D_FF // FF_CHUNKS
        for c in range(FF_CHUNKS):
            gate = jnp.dot(x, wgu_bf[:, c * cw:(c + 1) * cw], preferred_element_type=F32)
            gate = jnp.minimum(gate + bgu[:, c * cw:(c + 1) * cw], SWIGLU_LIMIT)
            up = jnp.dot(x, wgu_bf[:, D_FF + c * cw:D_FF + (c + 1) * cw], preferred_element_type=F32)
            up = jnp.clip(up + bgu[:, D_FF + c * cw:D_FF + (c + 1) * cw], -SWIGLU_LIMIT, SWIGLU_LIMIT)
            act = gate * jax.nn.sigmoid(SWIGLU_ALPHA * gate) * (up + 1.0)
            act_bf[:, c * cw:(c + 1) * cw] = act.astype(BF16)
        ow = D_MODEL // FF_CHUNKS
        for c in range(FF_CHUNKS):
            out = jnp.dot(act_bf[...], wd_bf[:, c * ow:(c + 1) * ow], preferred_element_type=F32)
            out = out + bd[:, c * ow:(c + 1) * ow]
            for j in range(ow // LANES):
                yring[i % 2, pl.ds(c * (ow // LANES) + j, ROW_TILE, stride=SLAB), :] = (
                    out[:, LANES * j:LANES * (j + 1)])
        y_tile(i).start()
        return carry

    lax.fori_loop(0, used, step, 0)

    @pl.when(used >= 2)
    def _():
        y_tile(used - 2).wait()
    y_tile(used - 1).wait()

    yring[0] = jnp.zeros(yring.shape[1:], F32)

    def clear(fn):
        def body(t, carry):
            fn(pltpu.make_async_copy(yring.at[0], ys_ref.at[tile_rows_of(t)], ysem.at[0]))
            return carry
        lax.fori_loop(used, n_tiles, body, 0)

    clear(lambda cp: cp.start())
    clear(lambda cp: cp.wait())


def _experts(xs, tile_expert, used, grp, nxt, w_gu, b_gu, w_down, b_down):
    rows = xs.shape[0] // SLAB
    nt = rows // ROW_TILE
    whole = lambda shape: pl.BlockSpec(shape, lambda i, *_: (0,) * len(shape))
    grid_spec = pltpu.PrefetchScalarGridSpec(
        num_scalar_prefetch=4,
        grid=(1,),
        in_specs=[pl.BlockSpec(memory_space=pl.ANY),
                  pl.BlockSpec(memory_space=pl.ANY),
                  whole((N_EXPERTS, 2 * D_FF)),
                  pl.BlockSpec(memory_space=pl.ANY),
                  whole((N_EXPERTS, D_MODEL))],
        out_specs=pl.BlockSpec(memory_space=pl.ANY),
        scratch_shapes=[pltpu.VMEM((2, D_MODEL, 2 * D_FF), F32), pltpu.VMEM((2, D_FF, D_MODEL), F32),
                        pltpu.VMEM((D_MODEL, 2 * D_FF), BF16), pltpu.VMEM((D_FF, D_MODEL), BF16),
                        pltpu.VMEM((ROW_TILE, D_FF), BF16),
                        pltpu.VMEM((X_RING, ROW_TILE * SLAB, LANES), F32),
                        pltpu.VMEM((2, ROW_TILE * SLAB, LANES), F32),
                        pltpu.SemaphoreType.DMA((2, 2)), pltpu.SemaphoreType.DMA((X_RING,)),
                        pltpu.SemaphoreType.DMA((2,))],
    )
    return pl.pallas_call(
        functools.partial(_experts_kernel, n_tiles=nt),
        grid_spec=grid_spec,
        out_shape=jax.ShapeDtypeStruct((rows * SLAB, LANES), F32),
        compiler_params=_cparams(("arbitrary",)),
        name="experts",
    )(tile_expert, used, grp, nxt, xs, w_gu, b_gu, w_down, b_down)


def _combine_kernel(dest_ref, next_ref, ys_ref, x1_ref, p_ref, g2_ref, fn_ref, y_ref, buf_ref, sem,
                    *, last):
    i = pl.program_id(0)
    slot = i % 2

    def issue(d_ref, sl):
        def body(r, carry):
            for k in range(TOP_K):
                d = d_ref[r, k]
                pltpu.make_async_copy(ys_ref.at[_slab_rows(d)], buf_ref.at[sl, k, _slab_rows(r)],
                                      sem.at[sl]).start(priority=k % 2)
            return carry
        lax.fori_loop(0, TOK_TILE, body, 0, unroll=2)

    @pl.when(i == 0)
    def _():
        issue(dest_ref, 0)

    @pl.when(i + 1 < pl.num_programs(0))
    def _():
        issue(next_ref, 1 - slot)

    for k in range(TOP_K):
        pltpu.make_async_copy(ys_ref.at[pl.ds(0, TOK_TILE * SLAB)], buf_ref.at[slot, k],
                              sem.at[slot]).wait()

    p = p_ref[...]
    groups = None
    for k in range(TOP_K):
        part = [p[:, k:k + 1] * g for g in _load_slabs(buf_ref.at[slot, k], TOK_TILE)]
        groups = part if groups is None else [a + b for a, b in zip(groups, part)]
    moe = jnp.concatenate(groups, axis=1)
    x2 = x1_ref[...] + g2_ref[...] * moe
    if last:
        x2 = x2 * lax.rsqrt(jnp.mean(x2 * x2, axis=-1, keepdims=True) + EPS) * fn_ref[...]
    y_ref[...] = x2


def _combine(ys, dest_flat, x1, probs, mod, per_token, seq, tile0, final_norm, last):
    n = x1.shape[0]
    tm = TOK_TILE
    tps = max(seq // tm, 1)
    nt = n // tm
    return pl.pallas_call(
        functools.partial(_combine_kernel, last=last),
        grid=(nt,),
        in_specs=[pl.BlockSpec((tm, TOP_K), lambda i: (i + tile0, 0), memory_space=pltpu.SMEM),
                  pl.BlockSpec((tm, TOP_K), lambda i: (jnp.minimum(i + 1, nt - 1) + tile0, 0),
                               memory_space=pltpu.SMEM),
                  pl.BlockSpec(memory_space=pl.ANY),
                  pl.BlockSpec((tm, D_MODEL), lambda i: (i, 0)),
                  pl.BlockSpec((tm, TOP_K), lambda i: (i + tile0, 0)),
                  _mod_spec(per_token, tm, tps, 5),
                  pl.BlockSpec((1, D_MODEL), lambda i: (0, 0))],
        out_specs=pl.BlockSpec((tm, D_MODEL), lambda i: (i, 0)),
        out_shape=jax.ShapeDtypeStruct((n, D_MODEL), F32),
        scratch_shapes=[pltpu.VMEM((2, TOP_K, tm * SLAB, LANES), F32),
                        pltpu.SemaphoreType.DMA((2,))],
        compiler_params=_cparams(("arbitrary",)),
        name="combine",
    )(dest_flat, dest_flat, ys, x1, probs, mod, final_norm.reshape(1, D_MODEL))


def kernel(x_prompt, x_sample, state_hgrn, state_pool, c_prompt, c_sample, w_ada, b_ada, norm1,
           norm2, w_in, lower_bounds, hgrn_norm, w_pool, pool_scale, w_out, w_router, b_router,
           w_gu, b_gu, w_down, b_down, final_norm):
    bp, seq_p, _ = x_prompt.shape
    bs, seq_s, _ = x_sample.shape
    np_, ns = bp * seq_p, bs * seq_s
    depth = w_ada.shape[0]

    lbs = jnp.cumsum(jax.nn.softmax(lower_bounds.astype(F32), axis=0), axis=0)
    xp = x_prompt.reshape(np_, D_MODEL)
    xs_ = x_sample.reshape(ns, D_MODEL)
    c_all = jnp.concatenate([c_prompt, c_sample], axis=0)

    sp_l, hp_l, ss_l, hs_l = [], [], [], []
    for l in range(depth):
        w_in_bf = w_in[l].astype(BF16)
        w_out_bf = w_out[l].astype(BF16)
        w_pool_bf = w_pool[l].astype(BF16)
        w_router_bf = w_router[l].astype(BF16)
        lb = lbs[l].reshape(1, D_A)
        hn = hgrn_norm[l].reshape(1, D_A)
        psc = pool_scale[l].reshape(1, D_B)

        mod = _ada(c_all, w_ada[l], b_ada[l])
        mod_p = mod[:bp].reshape(bp, 1, 6 * D_MODEL)
        mod_s = jnp.repeat(mod[bp:], seq_s, axis=0)

        up = _inproj(xp, mod_p, False, seq_p, norm1[l], w_in_bf)
        us = _inproj(xs_, mod_s, True, seq_s, norm1[l], w_in_bf)

        oa_p, s_p = _hgrn(up.reshape(bp, seq_p, D_IN), None, lb, hn)
        oa_p = oa_p.reshape(np_, D_A)
        oa_s, s_s = _hgrn(us, state_hgrn[l], lb, hn, seq=seq_s)
        ob_p, h_p = _pool_prompt(up, bp, seq_p, w_pool_bf, psc)
        ob_t, h_t = _pool_sample(us.reshape(bs, seq_s, D_IN).transpose(1, 0, 2),
                                 state_pool[l].transpose(1, 0, 2), PAST_LEN, w_pool_bf, psc)
        ob_s, h_s = ob_t.transpose(1, 0, 2).reshape(ns, D_B), h_t.transpose(1, 0, 2)

        n_tok = np_ + ns
        n_tiles = (n_tok * TOP_K) // ROW_TILE + N_EXPERTS
        x1p, *joint = _outproj(xp, oa_p, ob_p, mod_p, False, seq_p, norm2[l], w_out_bf,
                               w_router_bf, b_router[l], n_tok, 0)
        x1s, h2, top_idx, probs = _outproj(xs_, oa_s, ob_s, mod_s, True, seq_s, norm2[l], w_out_bf,
                                           w_router_bf, b_router[l], n_tok, np_ // TOK_TILE, joint)

        dest, counts = _rank(top_idx)
        dest_flat = dest
        tiles_e = (counts.reshape(N_EXPERTS) + ROW_TILE - 1) // ROW_TILE
        ends = jnp.cumsum(tiles_e)
        used = ends[-1:].astype(I32)
        tile_expert = jnp.minimum(
            jnp.sum(jnp.arange(n_tiles, dtype=I32)[:, None] >= ends[None, :], axis=1),
            N_EXPERTS - 1).astype(I32)
        last_tile = (ends - 1).astype(I32)
        grp = jnp.cumsum(jnp.concatenate(
            [jnp.zeros((1,), I32), (tile_expert[1:] != tile_expert[:-1]).astype(I32)])).astype(I32)
        eids = jnp.arange(N_EXPERTS, dtype=I32)
        later = (eids[None, :] > eids[:, None]) & (tiles_e > 0)[None, :]
        nxt_e = jnp.min(jnp.where(later, eids[None, :], N_EXPERTS), axis=1)
        nxt = jnp.where(nxt_e < N_EXPERTS, nxt_e, -1)[tile_expert].astype(I32)

        xsort = _dispatch(h2, dest_flat, last_tile, tiles_e.astype(I32), used, n_tiles)
        ysort = _experts(xsort, tile_expert, used, grp, nxt, w_gu[l], b_gu[l], w_down[l], b_down[l])

        last = l == depth - 1
        xp = _combine(ysort, dest_flat, x1p, probs, mod_p, False, seq_p, 0, final_norm, last)
        xs_ = _combine(ysort, dest_flat, x1s, probs, mod_s, True, seq_s, np_ // TOK_TILE,
                       final_norm, last)

        sp_l.append(s_p)
        hp_l.append(h_p)
        ss_l.append(s_s)
        hs_l.append(h_s)

    return (xp.reshape(bp, seq_p, D_MODEL), xs_.reshape(bs, seq_s, D_MODEL),
            jnp.stack(sp_l), jnp.stack(hp_l), jnp.stack(ss_l), jnp.stack(hs_l))
```

```python
import functools

import jax
import jax.numpy as jnp
from jax import lax
from jax.experimental import pallas as pl
from jax.experimental.pallas import tpu as pltpu

F32, BF16, I32 = jnp.float32, jnp.bfloat16, jnp.int32

D_MODEL = 1024
D_A = 512
H_A = 4
DK = 128
DV = 128
D_B = 512
POOL_WINDOWS = (2, 4, 8, 16)
POOL_GROUP = 128
POOL_HIST = 15
D_IN = 4 * D_A + D_B
N_EXPERTS = 32
TOP_K = 4
D_FF = 1024
SWIGLU_LIMIT = 7.0
SWIGLU_ALPHA = 1.702
EPS = 1e-6
PAST_LEN = 16384

SUB = 16
HALF = SUB // 2
HGRN_SEQS = 8
ROW_TILE = 256
TOK_TILE = 256
FF_CHUNKS = 4
X_RING = 3
LANES = 128
SLAB = D_MODEL // LANES
VMEM_LIMIT = 56 * 1024 * 1024


def _cparams(sem):
    return pltpu.CompilerParams(dimension_semantics=sem, vmem_limit_bytes=VMEM_LIMIT)


def _silu(x):
    return x * jax.nn.sigmoid(x)


def _mod_spec(per_token, tm, tiles_per_seq, col, last_tile=None):
    tile = (lambda i: i) if last_tile is None else (lambda i: jnp.minimum(i, last_tile))
    if per_token:
        return pl.BlockSpec((tm, D_MODEL), lambda i: (tile(i), col))
    return pl.BlockSpec((None, 1, D_MODEL), lambda i: (tile(i) // tiles_per_seq, 0, col))


def _ada_kernel(c_ref, w_ref, b_ref, o_ref):
    c = c_ref[...]
    o_ref[...] = jnp.dot(_silu(c).astype(BF16), w_ref[...].astype(BF16),
                         preferred_element_type=F32) + b_ref[...]


def _ada(c_all, w_ada, b_ada):
    rows = c_all.shape[0]
    n = w_ada.shape[1]
    return pl.pallas_call(
        _ada_kernel,
        grid=(n // D_MODEL,),
        in_specs=[pl.BlockSpec((rows, D_MODEL), lambda j: (0, 0)),
                  pl.BlockSpec((D_MODEL, D_MODEL), lambda j: (0, j)),
                  pl.BlockSpec((1, D_MODEL), lambda j: (0, j))],
        out_specs=pl.BlockSpec((rows, D_MODEL), lambda j: (0, j)),
        out_shape=jax.ShapeDtypeStruct((rows, n), F32),
        compiler_params=_cparams(("arbitrary",)),
        name="ada",
    )(c_all, w_ada, b_ada.reshape(1, n))


def _inproj_kernel(x_ref, sh_ref, sc_ref, n1_ref, w_ref, u_ref):
    x = x_ref[...]
    h = x * lax.rsqrt(jnp.mean(x * x, axis=-1, keepdims=True) + EPS) * n1_ref[...]
    h = h * (1.0 + sc_ref[...]) + sh_ref[...]
    u_ref[...] = jnp.dot(h.astype(BF16), w_ref[...], preferred_element_type=F32)


def _inproj(x, mod, per_token, seq, norm1, w_in_bf):
    n = x.shape[0]
    tm = min(512, n)
    tps = max(seq // tm, 1)
    return pl.pallas_call(
        _inproj_kernel,
        grid=(n // tm,),
        in_specs=[pl.BlockSpec((tm, D_MODEL), lambda i: (i, 0)),
                  _mod_spec(per_token, tm, tps, 0),
                  _mod_spec(per_token, tm, tps, 1),
                  pl.BlockSpec((1, D_MODEL), lambda i: (0, 0)),
                  pl.BlockSpec((D_MODEL, D_IN), lambda i: (0, 0))],
        out_specs=pl.BlockSpec((tm, D_IN), lambda i: (i, 0)),
        out_shape=jax.ShapeDtypeStruct((n, D_IN), F32),
        compiler_params=_cparams(("arbitrary",)),
        name="inproj",
    )(x, mod, mod, norm1.reshape(1, D_MODEL), w_in_bf)


def _block_select():
    r = lax.broadcasted_iota(I32, (SUB, SUB * DK), 0)
    c = lax.broadcasted_iota(I32, (SUB, SUB * DK), 1) >> (DK.bit_length() - 1)
    return (r == c).astype(BF16)


def _hgrn_step(q, fp, iv, g, lb, hn, st_ref, sel_ref, cm_ref, b_ref, k_ref, valid):
    nrow = HGRN_SEQS * SUB
    f = lb + (1.0 - lb) * jax.nn.sigmoid(fp)
    lf = jnp.log2(f)
    kk = 1.0 - f
    if valid < SUB:
        live = (lax.broadcasted_iota(I32, (nrow, D_A), 0) & (SUB - 1)) < valid
        lf = jnp.where(live, lf, 0.0)
        kk = jnp.where(live, kk, 0.0)
    qa = _silu(q)

    hi = lf.astype(BF16)
    lo = (lf - hi.astype(F32)).astype(BF16)
    cum, tot = cm_ref[0], cm_ref[1]
    b = jnp.dot(cum, hi, preferred_element_type=F32) + jnp.dot(cum, lo, preferred_element_type=F32)
    be = jnp.dot(tot, hi, preferred_element_type=F32) + jnp.dot(tot, lo, preferred_element_type=F32)
    qt = (qa * jnp.exp2(b)).astype(BF16)
    kh = (kk * jnp.exp2(be - b)).astype(BF16)
    dec = jnp.exp2(be)
    ivb = iv.astype(BF16)
    for h in range(H_A):
        b_ref[h] = b[:, DK * h:DK * (h + 1)]
        k_ref[h] = kk[:, DK * h:DK * (h + 1)]

    zero_half = jnp.zeros((HALF, D_A), F32)
    tr = lax.broadcasted_iota(I32, (H_A * SUB, SUB), 0) & (SUB - 1)
    tc = lax.broadcasted_iota(I32, (H_A * SUB, SUB), 1)
    sg = _silu(g)

    own = ((lax.broadcasted_iota(I32, (nrow, HGRN_SEQS * DK), 0) >> (SUB.bit_length() - 1))
           == (lax.broadcasted_iota(I32, (nrow, HGRN_SEQS * DK), 1) >> (DK.bit_length() - 1)))
    inter = [None] * (HGRN_SEQS * H_A)
    for h in range(H_A):
        sl = slice(DK * h, DK * (h + 1))
        kbd = jnp.where(own, jnp.concatenate([kh[:, sl]] * HGRN_SEQS, axis=1), jnp.zeros((), BF16))
        ut_all = lax.dot_general(ivb[:, sl], kbd, (((0,), (0,)), ((), ())),
                                 preferred_element_type=F32)
        for n in range(HGRN_SEQS):
            rows = slice(n * SUB, (n + 1) * SUB)
            st = st_ref[n, h]
            inter[n * H_A + h] = lax.dot_general(qt[rows, sl], st.astype(BF16),
                                                 (((1,), (1,)), ((), ())),
                                                 preferred_element_type=F32)
            dtile = dec[n * SUB:n * SUB + HALF, sl]
            st_ref[n, h] = ((st.reshape(DV // HALF, HALF, DK) * dtile[None]).reshape(DV, DK)
                            + ut_all[:, DK * n:DK * (n + 1)])

    scores = []
    for n in range(HGRN_SEQS):
        per_s = []
        for s in range(SUB):
            row = n * SUB + s
            b_s = jnp.concatenate([b_ref[h, pl.ds(row, HALF, stride=0), :] for h in range(H_A)], axis=1)
            k_s = jnp.concatenate([k_ref[h, pl.ds(row, HALF, stride=0), :] for h in range(H_A)], axis=1)
            halves = []
            for hf in range(2):
                if hf == 0 and s >= HALF:
                    halves.append(zero_half)
                    continue
                r0 = n * SUB + hf * HALF
                e = jnp.exp2(jnp.minimum(b[r0:r0 + HALF] - b_s, 0.0))
                halves.append(e * (qa[r0:r0 + HALF] * k_s))
            per_s.append(jnp.concatenate(halves, axis=0).astype(BF16))
        pcat = jnp.concatenate(
            [jnp.concatenate([p[:, DK * h:DK * (h + 1)] for p in per_s], axis=1) for h in range(H_A)],
            axis=0)
        a = lax.dot_general(pcat, sel_ref[...], (((1,), (1,)), ((), ())),
                            preferred_element_type=F32)
        scores.append(jnp.where(tr >= tc, a, 0.0).astype(BF16))

    outs = []
    for n in range(HGRN_SEQS):
        rows = slice(n * SUB, (n + 1) * SUB)
        heads = []
        for h in range(H_A):
            sl = slice(DK * h, DK * (h + 1))
            o = inter[n * H_A + h] + jnp.dot(scores[n][h * SUB:(h + 1) * SUB, :], ivb[rows, sl],
                                             preferred_element_type=F32)
            o = o * lax.rsqrt(jnp.mean(o * o, axis=-1, keepdims=True) + EPS) * hn[:, sl]
            heads.append(o)
        outs.append(jnp.concatenate(heads, axis=1) * sg[rows])
    return jnp.concatenate(outs, axis=0)


def _hgrn_kernel(*refs, rows, has_s0):
    if has_s0:
        (q_ref, f_ref, i_ref, g_ref, lb_ref, hn_ref, s0_ref, o_ref, s_ref,
         st_ref, sel_ref, cm_ref, b_ref, k_ref, pad_ref) = refs
    else:
        (q_ref, f_ref, i_ref, g_ref, lb_ref, hn_ref, o_ref, s_ref,
         st_ref, sel_ref, cm_ref, b_ref, k_ref) = refs
    t = pl.program_id(1)
    nrow = HGRN_SEQS * SUB

    @pl.when((t == 0) & (pl.program_id(0) == 0))
    def _():
        sel_ref[...] = _block_select()
        r = lax.broadcasted_iota(I32, (nrow, nrow), 0)
        c = lax.broadcasted_iota(I32, (nrow, nrow), 1)
        shift = SUB.bit_length() - 1
        same = (r >> shift) == (c >> shift)
        cm_ref[0] = (same & (r >= c)).astype(BF16)
        cm_ref[1] = same.astype(BF16)

    lb = lb_ref[...]
    hn = hn_ref[...]
    if has_s0:
        for n in range(HGRN_SEQS):
            for h in range(H_A):
                st_ref[n, h] = s0_ref[n, h].T
        pad_ref[...] = jnp.zeros_like(pad_ref)
        for j, ref in enumerate((q_ref, f_ref, i_ref, g_ref)):
            for n in range(HGRN_SEQS):
                pad_ref[j, n, 0:rows, :] = ref[n * rows:(n + 1) * rows, :]
        o = _hgrn_step(*(pad_ref[j].reshape(nrow, D_A) for j in range(4)), lb, hn,
                       st_ref, sel_ref, cm_ref, b_ref, k_ref, rows)
        for n in range(HGRN_SEQS):
            o_ref[n * rows:(n + 1) * rows, :] = o[n * SUB:n * SUB + rows, :]
    else:
        @pl.when(t == 0)
        def _():
            st_ref[...] = jnp.zeros_like(st_ref)

        def body(j, carry):
            rs = pl.ds(pl.multiple_of(j * SUB, SUB), SUB)
            o = _hgrn_step(*(ref[:, rs, :].reshape(nrow, D_A) for ref in (q_ref, f_ref, i_ref, g_ref)),
                           lb, hn, st_ref, sel_ref, cm_ref, b_ref, k_ref, SUB)
            o_ref[:, rs, :] = o.reshape(HGRN_SEQS, SUB, D_A)
            return carry

        lax.fori_loop(0, rows // SUB, body, 0)

    def emit_state():
        for n in range(HGRN_SEQS):
            for h in range(H_A):
                s_ref[n, h] = st_ref[n, h].T

    if has_s0:
        emit_state()
    else:
        pl.when(t == pl.num_programs(1) - 1)(emit_state)


def _hgrn(u3, s0, lb, hn, seq=None):
    has_s0 = s0 is not None
    if has_s0:
        batch = u3.shape[0] // seq
        rows = seq
        grid = (batch // HGRN_SEQS, 1)
        blk = lambda cb: pl.BlockSpec((HGRN_SEQS * rows, D_A), lambda b, t: (b, cb))
        o_shape = (batch * seq, D_A)
    else:
        batch, seq, _ = u3.shape
        rows = 128
        grid = (batch // HGRN_SEQS, seq // rows)
        blk = lambda cb: pl.BlockSpec((HGRN_SEQS, rows, D_A), lambda b, t: (b, t, cb))
        o_shape = (batch, seq, D_A)
    sblk = pl.BlockSpec((HGRN_SEQS, H_A, DK, DV), lambda b, t: (b, 0, 0, 0))
    vec = pl.BlockSpec((1, D_A), lambda b, t: (0, 0))
    nrow = HGRN_SEQS * SUB
    scratch = [pltpu.VMEM((HGRN_SEQS, H_A, DV, DK), F32), pltpu.VMEM((SUB, SUB * DK), BF16),
               pltpu.VMEM((2, nrow, nrow), BF16), pltpu.VMEM((H_A, nrow, DK), F32),
               pltpu.VMEM((H_A, nrow, DK), F32)]
    if has_s0:
        scratch.append(pltpu.VMEM((4, HGRN_SEQS, SUB, D_A), F32))
    args = (u3, u3, u3, u3, lb, hn) + ((s0,) if has_s0 else ())
    return pl.pallas_call(
        functools.partial(_hgrn_kernel, rows=rows, has_s0=has_s0),
        grid=grid,
        in_specs=[blk(0), blk(1), blk(2), blk(3), vec, vec] + ([sblk] if has_s0 else []),
        out_specs=[blk(0), sblk],
        out_shape=[jax.ShapeDtypeStruct(o_shape, F32),
                   jax.ShapeDtypeStruct((batch, H_A, DK, DV), F32)],
        scratch_shapes=scratch,
        compiler_params=_cparams(("arbitrary", "arbitrary")),
        name="hgrn_state" if has_s0 else "hgrn_scan",
    )(*args)


def _pool_groups(full_ref, base, tm, cnt_fn, wp_ref, ps_ref, step=1):
    outs = []
    for gi, w in enumerate(POOL_WINDOWS):
        ch = slice(POOL_GROUP * gi, POOL_GROUP * (gi + 1))
        v = full_ref[base:base + tm, ch]
        s = v
        for j in range(1, w):
            s = s + full_ref[base - j * step:base - j * step + tm, ch]
        d = s / cnt_fn(w) - v
        outs.append(jnp.dot(d.astype(BF16), wp_ref[gi], preferred_element_type=F32))
    return jnp.concatenate(outs, axis=1) * ps_ref[...]


def _pool_prompt_kernel(v_ref, prev_ref, wp_ref, ps_ref, o_ref, hist_ref, full_ref, *, tm):
    t = pl.program_id(1)
    prev = prev_ref[...]
    full_ref[0:16, :] = jnp.where(t == 0, 0.0, prev)
    full_ref[16:16 + tm, :] = v_ref[...]
    pos = (lax.broadcasted_iota(I32, (tm, 1), 0) + t * tm + 1).astype(F32)
    o_ref[...] = _pool_groups(full_ref, 16, tm, lambda w: jnp.minimum(pos, float(w)),
                              wp_ref, ps_ref)

    @pl.when(t == pl.num_programs(1) - 1)
    def _():
        hist_ref[...] = full_ref[16 + tm - POOL_HIST:16 + tm, :]


def _pool_prompt(u, batch, seq, w_pool_bf, pool_scale):
    tm = min(1024, seq)
    nt = seq // tm
    vcol = (4 * D_A) // D_B
    return pl.pallas_call(
        functools.partial(_pool_prompt_kernel, tm=tm),
        grid=(batch, nt),
        in_specs=[pl.BlockSpec((tm, D_B), lambda b, t: (b * nt + t, vcol)),
                  pl.BlockSpec((16, D_B),
                               lambda b, t: (jnp.maximum((b * nt + t) * (tm // 16) - 1, 0), vcol)),
                  pl.BlockSpec((len(POOL_WINDOWS), POOL_GROUP, POOL_GROUP), lambda b, t: (0, 0, 0)),
                  pl.BlockSpec((1, D_B), lambda b, t: (0, 0))],
        out_specs=[pl.BlockSpec((tm, D_B), lambda b, t: (b * nt + t, 0)),
                   pl.BlockSpec((None, POOL_HIST, D_B), lambda b, t: (b, 0, 0))],
        out_shape=[jax.ShapeDtypeStruct((batch * seq, D_B), F32),
                   jax.ShapeDtypeStruct((batch, POOL_HIST, D_B), F32)],
        scratch_shapes=[pltpu.VMEM((16 + tm, D_B), F32)],
        compiler_params=_cparams(("arbitrary", "arbitrary")),
        name="pool_prompt",
    )(u, u, w_pool_bf, pool_scale)


def _pool_sample_kernel(v_ref, hist_ref, wp_ref, ps_ref, o_ref, nh_ref, full_ref, *,
                        seq, batch, start):
    full_ref[0:batch, :] = jnp.zeros((batch, D_B), F32)
    full_ref[batch:16 * batch, :] = hist_ref[...].reshape(POOL_HIST * batch, D_B)
    full_ref[16 * batch:(16 + seq) * batch, :] = v_ref[...].reshape(seq * batch, D_B)

    def cnt(w):
        rows = [jnp.full((batch, 1), float(min(start + t + 1, w)), F32) for t in range(seq)]
        return jnp.concatenate(rows, axis=0)

    o = _pool_groups(full_ref, 16 * batch, seq * batch, cnt, wp_ref, ps_ref, step=batch)
    o_ref[...] = o.reshape(seq, batch, D_B)
    nh_ref[...] = full_ref[(16 + seq - POOL_HIST) * batch:(16 + seq) * batch, :].reshape(
        POOL_HIST, batch, D_B)


def _pool_sample(u_t, hist_t, start, w_pool_bf, pool_scale):
    seq, batch, _ = u_t.shape
    vcol = (4 * D_A) // D_B
    return pl.pallas_call(
        functools.partial(_pool_sample_kernel, seq=seq, batch=batch, start=start),
        grid=(1,),
        in_specs=[pl.BlockSpec((seq, batch, D_B), lambda i: (0, 0, vcol)),
                  pl.BlockSpec((POOL_HIST, batch, D_B), lambda i: (0, 0, 0)),
                  pl.BlockSpec((len(POOL_WINDOWS), POOL_GROUP, POOL_GROUP), lambda i: (0, 0, 0)),
                  pl.BlockSpec((1, D_B), lambda i: (0, 0))],
        out_specs=[pl.BlockSpec((seq, batch, D_B), lambda i: (0, 0, 0)),
                   pl.BlockSpec((POOL_HIST, batch, D_B), lambda i: (0, 0, 0))],
        out_shape=[jax.ShapeDtypeStruct((seq, batch, D_B), F32),
                   jax.ShapeDtypeStruct((POOL_HIST, batch, D_B), F32)],
        scratch_shapes=[pltpu.VMEM(((16 + seq) * batch, D_B), F32)],
        compiler_params=_cparams(("arbitrary",)),
        name="pool_sample",
    )(u_t, hist_t, w_pool_bf, pool_scale)


def _store_slabs(ref, value):
    n = value.shape[0]
    for j in range(SLAB):
        ref[pl.ds(j, n, stride=SLAB), :] = value[:, LANES * j:LANES * (j + 1)]


def _load_slabs(ref, n):
    return [ref[pl.ds(j, n, stride=SLAB), :] for j in range(SLAB)]


def _slab_rows(row):
    return pl.ds(pl.multiple_of(row * SLAB, SLAB), SLAB)


def _outproj_kernel(x_ref, oa_ref, ob_ref, g1_ref, sh2_ref, sc2_ref, n2_ref, woa_ref, wob_ref,
                    wr_ref, br_ref, *rest):
    x1_ref, h2_ref, idx_ref, prob_ref, lg_ref = rest[-5:]
    i = pl.program_id(0)
    slot = i % 2

    @pl.when(i == 0)
    def _():
        lg_ref[...] = jnp.zeros_like(lg_ref)

    logits = lg_ref[1 - slot]

    mix = (jnp.dot(oa_ref[...].astype(BF16), woa_ref[...], preferred_element_type=F32)
           + jnp.dot(ob_ref[...].astype(BF16), wob_ref[...], preferred_element_type=F32))
    x1 = x_ref[...] + g1_ref[...] * mix
    x1_ref[...] = x1
    h2 = x1 * lax.rsqrt(jnp.mean(x1 * x1, axis=-1, keepdims=True) + EPS) * n2_ref[...]
    h2 = h2 * (1.0 + sc2_ref[...]) + sh2_ref[...]
    _store_slabs(h2_ref, h2)
    lg_ref[slot] = jnp.dot(h2.astype(BF16), wr_ref[...], preferred_element_type=F32) + br_ref[...]

    tm = logits.shape[0]
    lane = lax.broadcasted_iota(I32, (tm, N_EXPERTS), 1)
    kcol = lax.broadcasted_iota(I32, (tm, TOP_K), 1)
    vals = jnp.zeros((tm, TOP_K), F32)
    idxs = jnp.zeros((tm, TOP_K), I32)
    for k in range(TOP_K):
        m = jnp.max(logits, axis=-1, keepdims=True)
        am = jnp.min(jnp.where(logits == m, lane, N_EXPERTS), axis=-1, keepdims=True)
        vals = jnp.where(kcol == k, m, vals)
        idxs = jnp.where(kcol == k, am, idxs)
        logits = jnp.where(lane == am, -jnp.inf, logits)
    e = jnp.exp(vals - vals[:, 0:1])
    prob_ref[...] = e / jnp.sum(e, axis=-1, keepdims=True)
    idx_ref[...] = idxs


def _outproj(x, o_a, o_b, mod, per_token, seq, norm2, w_out_bf, w_router_bf, b_router,
             n_total, tile0, joint=None):
    n = x.shape[0]
    tm = min(256, n)
    tps = max(seq // tm, 1)
    nt = n // tm
    cur = lambda i: jnp.minimum(i, nt - 1)
    prv = lambda i: jnp.maximum(i - 1, 0)
    row = lambda width: pl.BlockSpec((tm, width), lambda i: (cur(i), 0))
    jrow = lambda width: pl.BlockSpec((tm, width), lambda i: (prv(i) + tile0, 0))
    const = lambda shape: pl.BlockSpec(shape, lambda i: (0,) * len(shape))
    joint = () if joint is None else tuple(joint)
    n_in = 11
    return pl.pallas_call(
        _outproj_kernel,
        grid=(nt + 1,),
        in_specs=[row(D_MODEL), row(D_A), row(D_B),
                  _mod_spec(per_token, tm, tps, 2, nt - 1),
                  _mod_spec(per_token, tm, tps, 3, nt - 1),
                  _mod_spec(per_token, tm, tps, 4, nt - 1),
                  const((1, D_MODEL)),
                  pl.BlockSpec((D_A, D_MODEL), lambda i: (0, 0)),
                  pl.BlockSpec((D_B, D_MODEL), lambda i: (1, 0)),
                  const((D_MODEL, N_EXPERTS)), const((1, N_EXPERTS))]
                 + [pl.BlockSpec(memory_space=pl.ANY)] * len(joint),
        out_specs=[row(D_MODEL), pl.BlockSpec((tm * SLAB, LANES), lambda i: (cur(i) + tile0, 0)),
                   jrow(TOP_K), jrow(TOP_K)],
        out_shape=[jax.ShapeDtypeStruct((n, D_MODEL), F32),
                   jax.ShapeDtypeStruct((n_total * SLAB, LANES), F32),
                   jax.ShapeDtypeStruct((n_total, TOP_K), I32),
                   jax.ShapeDtypeStruct((n_total, TOP_K), F32)],
        input_output_aliases={n_in + j: 1 + j for j in range(len(joint))},
        scratch_shapes=[pltpu.VMEM((2, tm, N_EXPERTS), F32)],
        compiler_params=_cparams(("arbitrary",)),
        name="outproj",
    )(x, o_a, o_b, mod, mod, mod, norm2.reshape(1, D_MODEL), w_out_bf, w_out_bf,
      w_router_bf, b_router.reshape(1, N_EXPERTS), *joint)


def _rank_kernel(idx_ref, dest_ref, cnt_ref, carry_ref, *, te):
    phase = pl.program_id(0)
    i = pl.program_id(1)
    idx = idx_ref[...]
    lane = lax.broadcasted_iota(I32, (te, N_EXPERTS), 1)
    onehots = [(idx[:, k:k + 1] == lane) for k in range(TOP_K)]
    member = jnp.zeros((te, N_EXPERTS), F32)
    for oh in onehots:
        member = member + oh.astype(F32)
    colsum = jnp.sum(member, axis=0, keepdims=True)

    @pl.when((phase == 0) & (i == 0))
    def _():
        carry_ref[...] = jnp.zeros_like(carry_ref)

    @pl.when(phase == 0)
    def _():
        carry_ref[...] += colsum

    @pl.when((phase == 1) & (i == 0))
    def _():
        cnt = carry_ref[...]
        cnt_ref[...] = cnt.astype(I32)
        tiles = jnp.floor((cnt + (ROW_TILE - 1)) * (1.0 / ROW_TILE))
        r = lax.broadcasted_iota(I32, (N_EXPERTS, N_EXPERTS), 0)
        c = lax.broadcasted_iota(I32, (N_EXPERTS, N_EXPERTS), 1)
        before = (r < c).astype(BF16)
        carry_ref[...] = ROW_TILE * jnp.dot(tiles.astype(BF16), before, preferred_element_type=F32)

    @pl.when(phase == 1)
    def _():
        r = lax.broadcasted_iota(I32, (te, te), 0)
        c = lax.broadcasted_iota(I32, (te, te), 1)
        earlier = (r > c).astype(BF16)
        base = carry_ref[...] + jnp.dot(earlier, member.astype(BF16), preferred_element_type=F32)
        kcol = lax.broadcasted_iota(I32, (te, TOP_K), 1)
        dest = jnp.zeros((te, TOP_K), F32)
        for k, oh in enumerate(onehots):
            dk = jnp.sum(jnp.where(oh, base, 0.0), axis=-1, keepdims=True)
            dest = jnp.where(kcol == k, dk, dest)
        dest_ref[...] = dest.astype(I32)
        carry_ref[...] += colsum


def _rank(top_idx):
    n = top_idx.shape[0]
    te = 512
    return pl.pallas_call(
        functools.partial(_rank_kernel, te=te),
        grid=(2, n // te),
        in_specs=[pl.BlockSpec((te, TOP_K), lambda p, i: (i, 0))],
        out_specs=[pl.BlockSpec((te, TOP_K), lambda p, i: (p * i, 0)),
                   pl.BlockSpec((1, N_EXPERTS), lambda p, i: (0, 0))],
        out_shape=[jax.ShapeDtypeStruct((n, TOP_K), I32),
                   jax.ShapeDtypeStruct((1, N_EXPERTS), I32)],
        scratch_shapes=[pltpu.VMEM((1, N_EXPERTS), F32)],
        compiler_params=_cparams(("arbitrary", "arbitrary")),
        name="rank",
    )(top_idx)


def _dispatch_kernel(last_ref, ntile_ref, used_ref, dest_ref, h_ref, xs_ref, zero_ref, sem, zsem,
                     *, n_tiles):
    i = pl.program_id(0)

    @pl.when(i == 0)
    def _():
        zero_ref[...] = jnp.zeros_like(zero_ref)

        def clear(tile):
            rows = pl.ds(pl.multiple_of(tile * (ROW_TILE * SLAB), ROW_TILE * SLAB), ROW_TILE * SLAB)
            return pltpu.make_async_copy(zero_ref, xs_ref.at[rows], zsem)

        def over_tail(fn):
            def body(t, carry):
                fn(clear(t))
                return carry
            lax.fori_loop(used_ref[0], n_tiles, body, 0)

        for fn in (lambda c: c.start(), lambda c: c.wait()):
            for e in range(N_EXPERTS):
                @pl.when(ntile_ref[e] > 0)
                def _():
                    fn(clear(last_ref[e]))
            over_tail(fn)

    def body(r, carry):
        for k in range(TOP_K):
            d = dest_ref[r * TOP_K + k]
            pltpu.make_async_copy(h_ref.at[_slab_rows(r)], xs_ref.at[_slab_rows(d)], sem).start(
                priority=k % 2)
        return carry

    lax.fori_loop(0, TOK_TILE, body, 0, unroll=2)
    for k in range(TOP_K):
        pltpu.make_async_copy(h_ref, xs_ref.at[pl.ds(0, TOK_TILE * SLAB)], sem).wait()


def _dispatch(h2, dest_flat, last_tile, ntiles, used, n_tiles):
    n = h2.shape[0] // SLAB
    grid_spec = pltpu.PrefetchScalarGridSpec(
        num_scalar_prefetch=3,
        grid=(n // TOK_TILE,),
        in_specs=[pl.BlockSpec((TOK_TILE * TOP_K,), lambda i, *_: (i,), memory_space=pltpu.SMEM),
                  pl.BlockSpec((TOK_TILE * SLAB, LANES), lambda i, *_: (i, 0))],
        out_specs=pl.BlockSpec(memory_space=pl.ANY),
        scratch_shapes=[pltpu.VMEM((ROW_TILE * SLAB, LANES), F32),
                        pltpu.SemaphoreType.DMA(()), pltpu.SemaphoreType.DMA(())],
    )
    return pl.pallas_call(
        functools.partial(_dispatch_kernel, n_tiles=n_tiles),
        grid_spec=grid_spec,
        out_shape=jax.ShapeDtypeStruct((n_tiles * ROW_TILE * SLAB, LANES), F32),
        compiler_params=_cparams(("arbitrary",)),
        name="dispatch",
    )(last_tile, ntiles, used, dest_flat, h2)


def _experts_kernel(te_ref, used_ref, grp_ref, nxt_ref, xs_ref, wgu_ref, bgu_ref, wd_ref, bd_ref,
                    ys_ref, wgu_f, wd_f, wgu_bf, wd_bf, act_bf, xring, yring, wsem, xsem, ysem,
                    *, n_tiles):
    used = used_ref[0]
    tile_rows = ROW_TILE * SLAB

    def weights(e, ws):
        return (pltpu.make_async_copy(wgu_ref.at[e], wgu_f.at[ws], wsem.at[ws, 0]),
                pltpu.make_async_copy(wd_ref.at[e], wd_f.at[ws], wsem.at[ws, 1]))

    def tile_rows_of(t):
        return pl.ds(pl.multiple_of(t * tile_rows, tile_rows), tile_rows)

    def x_tile(t):
        t = jnp.asarray(t, I32)
        return pltpu.make_async_copy(xs_ref.at[tile_rows_of(t)], xring.at[t % X_RING],
                                     xsem.at[t % X_RING])

    def y_tile(t):
        return pltpu.make_async_copy(yring.at[t % 2], ys_ref.at[tile_rows_of(t)], ysem.at[t % 2])

    for cp in weights(te_ref[0], 0):
        cp.start()
    for t in range(X_RING - 1):
        @pl.when(t < used)
        def _():
            x_tile(t).start()

    def step(i, carry):
        @pl.when(i + X_RING - 1 < used)
        def _():
            x_tile(i + X_RING - 1).start()

        e = te_ref[i]
        ws = grp_ref[i] % 2
        fresh = (i == 0) | (e != te_ref[jnp.maximum(i - 1, 0)])

        @pl.when(fresh)
        def _():
            for cp in weights(e, ws):
                cp.wait()

            @pl.when(nxt_ref[i] >= 0)
            def _():
                for cp in weights(nxt_ref[i], 1 - ws):
                    cp.start()

            wgu_bf[...] = wgu_f[ws].astype(BF16)
            wd_bf[...] = wd_f[ws].astype(BF16)

        x_tile(i).wait()

        @pl.when(i >= 2)
        def _():
            y_tile(i - 2).wait()

        x = jnp.concatenate([p.astype(BF16) for p in _load_slabs(xring.at[i % X_RING], ROW_TILE)],
                            axis=1)
        bgu = bgu_ref[pl.ds(e, 1), :]
        bd = bd_ref[pl.ds(e, 1), :]
        cw = D_FF // FF_CHUNKS
        for c in range(FF_CHUNKS):
            gate = jnp.dot(x, wgu_bf[:, c * cw:(c + 1) * cw], preferred_element_type=F32)
            gate = jnp.minimum(gate + bgu[:, c * cw:(c + 1) * cw], SWIGLU_LIMIT)
            up = jnp.dot(x, wgu_bf[:, D_FF + c * cw:D_FF + (c + 1) * cw], preferred_element_type=F32)
            up = jnp.clip(up + bgu[:, D_FF + c * cw:D_FF + (c + 1) * cw], -SWIGLU_LIMIT, SWIGLU_LIMIT)
            act = gate * jax.nn.sigmoid(SWIGLU_ALPHA * gate) * (up + 1.0)
            act_bf[:, c * cw:(c + 1) * cw] = act.astype(BF16)
        ow = D_MODEL // FF_CHUNKS
        for c in range(FF_CHUNKS):
            out = jnp.dot(act_bf[...], wd_bf[:, c * ow:(c + 1) * ow], preferred_element_type=F32)
            out = out + bd[:, c * ow:(c + 1) * ow]
            for j in range(ow // LANES):
                yring[i % 2, pl.ds(c * (ow // LANES) + j, ROW_TILE, stride=SLAB), :] = (
                    out[:, LANES * j:LANES * (j + 1)])
        y_tile(i).start()
        return carry

    lax.fori_loop(0, used, step, 0)

    @pl.when(used >= 2)
    def _():
        y_tile(used - 2).wait()
    y_tile(used - 1).wait()

    yring[0] = jnp.zeros(yring.shape[1:], F32)

    def clear(fn):
        def body(t, carry):
            fn(pltpu.make_async_copy(yring.at[0], ys_ref.at[tile_rows_of(t)], ysem.at[0]))
            return carry
        lax.fori_loop(used, n_tiles, body, 0)

    clear(lambda cp: cp.start())
    clear(lambda cp: cp.wait())


def _experts(xs, tile_expert, used, grp, nxt, w_gu, b_gu, w_down, b_down):
    rows = xs.shape[0] // SLAB
    nt = rows // ROW_TILE
    whole = lambda shape: pl.BlockSpec(shape, lambda i, *_: (0,) * len(shape))
    grid_spec = pltpu.PrefetchScalarGridSpec(
        num_scalar_prefetch=4,
        grid=(1,),
        in_specs=[pl.BlockSpec(memory_space=pl.ANY),
                  pl.BlockSpec(memory_space=pl.ANY),
                  whole((N_EXPERTS, 2 * D_FF)),
                  pl.BlockSpec(memory_space=pl.ANY),
                  whole((N_EXPERTS, D_MODEL))],
        out_specs=pl.BlockSpec(memory_space=pl.ANY),
        scratch_shapes=[pltpu.VMEM((2, D_MODEL, 2 * D_FF), F32), pltpu.VMEM((2, D_FF, D_MODEL), F32),
                        pltpu.VMEM((D_MODEL, 2 * D_FF), BF16), pltpu.VMEM((D_FF, D_MODEL), BF16),
                        pltpu.VMEM((ROW_TILE, D_FF), BF16),
                        pltpu.VMEM((X_RING, ROW_TILE * SLAB, LANES), F32),
                        pltpu.VMEM((2, ROW_TILE * SLAB, LANES), F32),
                        pltpu.SemaphoreType.DMA((2, 2)), pltpu.SemaphoreType.DMA((X_RING,)),
                        pltpu.SemaphoreType.DMA((2,))],
    )
    return pl.pallas_call(
        functools.partial(_experts_kernel, n_tiles=nt),
        grid_spec=grid_spec,
        out_shape=jax.ShapeDtypeStruct((rows * SLAB, LANES), F32),
        compiler_params=_cparams(("arbitrary",)),
        name="experts",
    )(tile_expert, used, grp, nxt, xs, w_gu, b_gu, w_down, b_down)


def _combine_kernel(dest_ref, next_ref, ys_ref, x1_ref, p_ref, g2_ref, fn_ref, y_ref, buf_ref, sem,
                    *, last):
    i = pl.program_id(0)
    slot = i % 2

    def issue(d_ref, sl):
        def body(r, carry):
            for k in range(TOP_K):
                d = d_ref[r * TOP_K + k]
                pltpu.make_async_copy(ys_ref.at[_slab_rows(d)], buf_ref.at[sl, k, _slab_rows(r)],
                                      sem.at[sl]).start(priority=k % 2)
            return carry
        lax.fori_loop(0, TOK_TILE, body, 0, unroll=2)

    @pl.when(i == 0)
    def _():
        issue(dest_ref, 0)

    @pl.when(i + 1 < pl.num_programs(0))
    def _():
        issue(next_ref, 1 - slot)

    for k in range(TOP_K):
        pltpu.make_async_copy(ys_ref.at[pl.ds(0, TOK_TILE * SLAB)], buf_ref.at[slot, k],
                              sem.at[slot]).wait()

    p = p_ref[...]
    groups = None
    for k in range(TOP_K):
        part = [p[:, k:k + 1] * g for g in _load_slabs(buf_ref.at[slot, k], TOK_TILE)]
        groups = part if groups is None else [a + b for a, b in zip(groups, part)]
    moe = jnp.concatenate(groups, axis=1)
    x2 = x1_ref[...] + g2_ref[...] * moe
    if last:
        x2 = x2 * lax.rsqrt(jnp.mean(x2 * x2, axis=-1, keepdims=True) + EPS) * fn_ref[...]
    y_ref[...] = x2


def _combine(ys, dest_flat, x1, probs, mod, per_token, seq, tile0, final_norm, last):
    n = x1.shape[0]
    tm = TOK_TILE
    tps = max(seq // tm, 1)
    nt = n // tm
    return pl.pallas_call(
        functools.partial(_combine_kernel, last=last),
        grid=(nt,),
        in_specs=[pl.BlockSpec((tm * TOP_K,), lambda i: (i + tile0,), memory_space=pltpu.SMEM),
                  pl.BlockSpec((tm * TOP_K,), lambda i: (jnp.minimum(i + 1, nt - 1) + tile0,),
                               memory_space=pltpu.SMEM),
                  pl.BlockSpec(memory_space=pl.ANY),
                  pl.BlockSpec((tm, D_MODEL), lambda i: (i, 0)),
                  pl.BlockSpec((tm, TOP_K), lambda i: (i + tile0, 0)),
                  _mod_spec(per_token, tm, tps, 5),
                  pl.BlockSpec((1, D_MODEL), lambda i: (0, 0))],
        out_specs=pl.BlockSpec((tm, D_MODEL), lambda i: (i, 0)),
        out_shape=jax.ShapeDtypeStruct((n, D_MODEL), F32),
        scratch_shapes=[pltpu.VMEM((2, TOP_K, tm * SLAB, LANES), F32),
                        pltpu.SemaphoreType.DMA((2,))],
        compiler_params=_cparams(("arbitrary",)),
        name="combine",
    )(dest_flat, dest_flat, ys, x1, probs, mod, final_norm.reshape(1, D_MODEL))


def kernel(x_prompt, x_sample, state_hgrn, state_pool, c_prompt, c_sample, w_ada, b_ada, norm1,
           norm2, w_in, lower_bounds, hgrn_norm, w_pool, pool_scale, w_out, w_router, b_router,
           w_gu, b_gu, w_down, b_down, final_norm):
    bp, seq_p, _ = x_prompt.shape
    bs, seq_s, _ = x_sample.shape
    np_, ns = bp * seq_p, bs * seq_s
    depth = w_ada.shape[0]

    lbs = jnp.cumsum(jax.nn.softmax(lower_bounds.astype(F32), axis=0), axis=0)
    xp = x_prompt.reshape(np_, D_MODEL)
    xs_ = x_sample.reshape(ns, D_MODEL)
    c_all = jnp.concatenate([jnp.repeat(c_sample, seq_s, axis=0), c_prompt], axis=0)

    sp_l, hp_l, ss_l, hs_l = [], [], [], []
    for l in range(depth):
        w_in_bf = w_in[l].astype(BF16)
        w_out_bf = w_out[l].astype(BF16)
        w_pool_bf = w_pool[l].astype(BF16)
        w_router_bf = w_router[l].astype(BF16)
        lb = lbs[l].reshape(1, D_A)
        hn = hgrn_norm[l].reshape(1, D_A)
        psc = pool_scale[l].reshape(1, D_B)

        mod = _ada(c_all, w_ada[l], b_ada[l])
        mod_p = mod[ns:].reshape(bp, 1, 6 * D_MODEL)
        mod_s = mod

        up = _inproj(xp, mod_p, False, seq_p, norm1[l], w_in_bf)
        us = _inproj(xs_, mod_s, True, seq_s, norm1[l], w_in_bf)

        oa_p, s_p = _hgrn(up.reshape(bp, seq_p, D_IN), None, lb, hn)
        oa_p = oa_p.reshape(np_, D_A)
        oa_s, s_s = _hgrn(us, state_hgrn[l], lb, hn, seq=seq_s)
        ob_p, h_p = _pool_prompt(up, bp, seq_p, w_pool_bf, psc)
        ob_t, h_t = _pool_sample(us.reshape(bs, seq_s, D_IN).transpose(1, 0, 2),
                                 state_pool[l].transpose(1, 0, 2), PAST_LEN, w_pool_bf, psc)
        ob_s, h_s = ob_t.transpose(1, 0, 2).reshape(ns, D_B), h_t.transpose(1, 0, 2)

        n_tok = np_ + ns
        n_tiles = (n_tok * TOP_K) // ROW_TILE + N_EXPERTS
        x1p, *joint = _outproj(xp, oa_p, ob_p, mod_p, False, seq_p, norm2[l], w_out_bf,
                               w_router_bf, b_router[l], n_tok, 0)
        x1s, h2, top_idx, probs = _outproj(xs_, oa_s, ob_s, mod_s, True, seq_s, norm2[l], w_out_bf,
                                           w_router_bf, b_router[l], n_tok, np_ // TOK_TILE, joint)

        dest, counts = _rank(top_idx)
        dest_flat = dest.reshape(n_tok * TOP_K)
        tiles_e = (counts.reshape(N_EXPERTS) + ROW_TILE - 1) // ROW_TILE
        ends = jnp.cumsum(tiles_e)
        used = ends[-1:].astype(I32)
        tile_expert = jnp.minimum(
            jnp.sum(jnp.arange(n_tiles, dtype=I32)[:, None] >= ends[None, :], axis=1),
            N_EXPERTS - 1).astype(I32)
        last_tile = (ends - 1).astype(I32)
        grp = jnp.cumsum(jnp.concatenate(
            [jnp.zeros((1,), I32), (tile_expert[1:] != tile_expert[:-1]).astype(I32)])).astype(I32)
        eids = jnp.arange(N_EXPERTS, dtype=I32)
        later = (eids[None, :] > eids[:, None]) & (tiles_e > 0)[None, :]
        nxt_e = jnp.min(jnp.where(later, eids[None, :], N_EXPERTS), axis=1)
        nxt = jnp.where(nxt_e < N_EXPERTS, nxt_e, -1)[tile_expert].astype(I32)

        xsort = _dispatch(h2, dest_flat, last_tile, tiles_e.astype(I32), used, n_tiles)
        ysort = _experts(xsort, tile_expert, used, grp, nxt, w_gu[l], b_gu[l], w_down[l], b_down[l])

        last = l == depth - 1
        xp = _combine(ysort, dest_flat, x1p, probs, mod_p, False, seq_p, 0, final_norm, last)
        xs_ = _combine(ysort, dest_flat, x1s, probs, mod_s, True, seq_s, np_ // TOK_TILE,
                       final_norm, last)

        sp_l.append(s_p)
        hp_l.append(h_p)
        ss_l.append(s_s)
        hs_l.append(h_s)

    return (xp.reshape(bp, seq_p, D_MODEL), xs_.reshape(bs, seq_s, D_MODEL),
            jnp.stack(sp_l), jnp.stack(hp_l), jnp.stack(ss_l), jnp.stack(hs_l))
```

```python
import functools

import jax
import jax.numpy as jnp
from jax import lax
from jax.experimental import pallas as pl
from jax.experimental.pallas import tpu as pltpu

F32, BF16, I32 = jnp.float32, jnp.bfloat16, jnp.int32

D_MODEL = 1024
D_A = 512
H_A = 4
DK = 128
DV = 128
D_B = 512
POOL_WINDOWS = (2, 4, 8, 16)
POOL_GROUP = 128
POOL_HIST = 15
D_IN = 4 * D_A + D_B
N_EXPERTS = 32
TOP_K = 4
D_FF = 1024
SWIGLU_LIMIT = 7.0
SWIGLU_ALPHA = 1.702
EPS = 1e-6
PAST_LEN = 16384

SUB = 16
HALF = SUB // 2
HGRN_SEQS = 8
ROW_TILE = 256
TOK_TILE = 256
FF_CHUNKS = 4
X_RING = 3
LANES = 128
SLAB = D_MODEL // LANES
VMEM_LIMIT = 56 * 1024 * 1024


def _cparams(sem):
    return pltpu.CompilerParams(dimension_semantics=sem, vmem_limit_bytes=VMEM_LIMIT)


def _silu(x):
    return x * jax.nn.sigmoid(x)


def _mod_spec(per_token, tm, tiles_per_seq, col, last_tile=None):
    tile = (lambda i: i) if last_tile is None else (lambda i: jnp.minimum(i, last_tile))
    if per_token:
        return pl.BlockSpec((tm, D_MODEL), lambda i: (tile(i), col))
    return pl.BlockSpec((None, 1, D_MODEL), lambda i: (tile(i) // tiles_per_seq, 0, col))


def _ada_kernel(c_ref, w_ref, b_ref, o_ref):
    c = c_ref[...]
    o_ref[...] = jnp.dot(_silu(c).astype(BF16), w_ref[...].astype(BF16),
                         preferred_element_type=F32) + b_ref[...]


def _ada(c_all, w_ada, b_ada):
    rows = c_all.shape[0]
    n = w_ada.shape[1]
    return pl.pallas_call(
        _ada_kernel,
        grid=(n // D_MODEL,),
        in_specs=[pl.BlockSpec((rows, D_MODEL), lambda j: (0, 0)),
                  pl.BlockSpec((D_MODEL, D_MODEL), lambda j: (0, j)),
                  pl.BlockSpec((1, D_MODEL), lambda j: (0, j))],
        out_specs=pl.BlockSpec((rows, D_MODEL), lambda j: (0, j)),
        out_shape=jax.ShapeDtypeStruct((rows, n), F32),
        compiler_params=_cparams(("arbitrary",)),
        name="ada",
    )(c_all, w_ada, b_ada.reshape(1, n))


def _inproj_kernel(x_ref, sh_ref, sc_ref, n1_ref, w_ref, u_ref):
    x = x_ref[...]
    h = x * lax.rsqrt(jnp.mean(x * x, axis=-1, keepdims=True) + EPS) * n1_ref[...]
    h = h * (1.0 + sc_ref[...]) + sh_ref[...]
    u_ref[...] = jnp.dot(h.astype(BF16), w_ref[...], preferred_element_type=F32)


def _inproj(x, mod, per_token, seq, norm1, w_in_bf):
    n = x.shape[0]
    tm = min(512, n)
    tps = max(seq // tm, 1)
    return pl.pallas_call(
        _inproj_kernel,
        grid=(n // tm,),
        in_specs=[pl.BlockSpec((tm, D_MODEL), lambda i: (i, 0)),
                  _mod_spec(per_token, tm, tps, 0),
                  _mod_spec(per_token, tm, tps, 1),
                  pl.BlockSpec((1, D_MODEL), lambda i: (0, 0)),
                  pl.BlockSpec((D_MODEL, D_IN), lambda i: (0, 0))],
        out_specs=pl.BlockSpec((tm, D_IN), lambda i: (i, 0)),
        out_shape=jax.ShapeDtypeStruct((n, D_IN), F32),
        compiler_params=_cparams(("arbitrary",)),
        name="inproj",
    )(x, mod, mod, norm1.reshape(1, D_MODEL), w_in_bf)


def _block_select():
    r = lax.broadcasted_iota(I32, (SUB, SUB * DK), 0)
    c = lax.broadcasted_iota(I32, (SUB, SUB * DK), 1) >> (DK.bit_length() - 1)
    return (r == c).astype(BF16)


def _hgrn_step(q, fp, iv, g, lb, hn, st_ref, sel_ref, cm_ref, b_ref, k_ref, valid):
    nrow = HGRN_SEQS * SUB
    f = lb + (1.0 - lb) * jax.nn.sigmoid(fp)
    lf = jnp.log2(f)
    kk = 1.0 - f
    if valid < SUB:
        live = (lax.broadcasted_iota(I32, (nrow, D_A), 0) & (SUB - 1)) < valid
        lf = jnp.where(live, lf, 0.0)
        kk = jnp.where(live, kk, 0.0)
    qa = _silu(q)

    hi = lf.astype(BF16)
    lo = (lf - hi.astype(F32)).astype(BF16)
    cum, tot = cm_ref[0], cm_ref[1]
    b = jnp.dot(cum, hi, preferred_element_type=F32) + jnp.dot(cum, lo, preferred_element_type=F32)
    be = jnp.dot(tot, hi, preferred_element_type=F32) + jnp.dot(tot, lo, preferred_element_type=F32)
    qt = (qa * jnp.exp2(b)).astype(BF16)
    kh = (kk * jnp.exp2(be - b)).astype(BF16)
    dec = jnp.exp2(be)
    ivb = iv.astype(BF16)
    for h in range(H_A):
        b_ref[h] = b[:, DK * h:DK * (h + 1)]
        k_ref[h] = kk[:, DK * h:DK * (h + 1)]

    zero_half = jnp.zeros((HALF, D_A), F32)
    tr = lax.broadcasted_iota(I32, (H_A * SUB, SUB), 0) & (SUB - 1)
    tc = lax.broadcasted_iota(I32, (H_A * SUB, SUB), 1)
    sg = _silu(g)

    own = ((lax.broadcasted_iota(I32, (nrow, HGRN_SEQS * DK), 0) >> (SUB.bit_length() - 1))
           == (lax.broadcasted_iota(I32, (nrow, HGRN_SEQS * DK), 1) >> (DK.bit_length() - 1)))
    inter = [None] * (HGRN_SEQS * H_A)
    for h in range(H_A):
        sl = slice(DK * h, DK * (h + 1))
        kbd = jnp.where(own, jnp.concatenate([kh[:, sl]] * HGRN_SEQS, axis=1), jnp.zeros((), BF16))
        ut_all = lax.dot_general(ivb[:, sl], kbd, (((0,), (0,)), ((), ())),
                                 preferred_element_type=F32)
        for n in range(HGRN_SEQS):
            rows = slice(n * SUB, (n + 1) * SUB)
            st = st_ref[n, h]
            inter[n * H_A + h] = lax.dot_general(qt[rows, sl], st.astype(BF16),
                                                 (((1,), (1,)), ((), ())),
                                                 preferred_element_type=F32)
            dtile = dec[n * SUB:n * SUB + HALF, sl]
            st_ref[n, h] = ((st.reshape(DV // HALF, HALF, DK) * dtile[None]).reshape(DV, DK)
                            + ut_all[:, DK * n:DK * (n + 1)])

    scores = []
    for n in range(HGRN_SEQS):
        per_s = []
        for s in range(SUB):
            row = n * SUB + s
            b_s = jnp.concatenate([b_ref[h, pl.ds(row, HALF, stride=0), :] for h in range(H_A)], axis=1)
            k_s = jnp.concatenate([k_ref[h, pl.ds(row, HALF, stride=0), :] for h in range(H_A)], axis=1)
            halves = []
            for hf in range(2):
                if hf == 0 and s >= HALF:
                    halves.append(zero_half)
                    continue
                r0 = n * SUB + hf * HALF
                e = jnp.exp2(jnp.minimum(b[r0:r0 + HALF] - b_s, 0.0))
                halves.append(e * (qa[r0:r0 + HALF] * k_s))
            per_s.append(jnp.concatenate(halves, axis=0).astype(BF16))
        pcat = jnp.concatenate(
            [jnp.concatenate([p[:, DK * h:DK * (h + 1)] for p in per_s], axis=1) for h in range(H_A)],
            axis=0)
        a = lax.dot_general(pcat, sel_ref[...], (((1,), (1,)), ((), ())),
                            preferred_element_type=F32)
        scores.append(jnp.where(tr >= tc, a, 0.0).astype(BF16))

    outs = []
    for n in range(HGRN_SEQS):
        rows = slice(n * SUB, (n + 1) * SUB)
        heads = []
        for h in range(H_A):
            sl = slice(DK * h, DK * (h + 1))
            o = inter[n * H_A + h] + jnp.dot(scores[n][h * SUB:(h + 1) * SUB, :], ivb[rows, sl],
                                             preferred_element_type=F32)
            o = o * lax.rsqrt(jnp.mean(o * o, axis=-1, keepdims=True) + EPS) * hn[:, sl]
            heads.append(o)
        outs.append(jnp.concatenate(heads, axis=1) * sg[rows])
    return jnp.concatenate(outs, axis=0)


def _hgrn_kernel(*refs, rows, has_s0):
    if has_s0:
        (q_ref, f_ref, i_ref, g_ref, lb_ref, hn_ref, s0_ref, o_ref, s_ref,
         st_ref, sel_ref, cm_ref, b_ref, k_ref, pad_ref) = refs
    else:
        (q_ref, f_ref, i_ref, g_ref, lb_ref, hn_ref, o_ref, s_ref,
         st_ref, sel_ref, cm_ref, b_ref, k_ref) = refs
    t = pl.program_id(1)
    nrow = HGRN_SEQS * SUB

    @pl.when((t == 0) & (pl.program_id(0) == 0))
    def _():
        sel_ref[...] = _block_select()
        r = lax.broadcasted_iota(I32, (nrow, nrow), 0)
        c = lax.broadcasted_iota(I32, (nrow, nrow), 1)
        shift = SUB.bit_length() - 1
        same = (r >> shift) == (c >> shift)
        cm_ref[0] = (same & (r >= c)).astype(BF16)
        cm_ref[1] = same.astype(BF16)

    lb = lb_ref[...]
    hn = hn_ref[...]
    if has_s0:
        for n in range(HGRN_SEQS):
            for h in range(H_A):
                st_ref[n, h] = s0_ref[n, h].T
        pad_ref[...] = jnp.zeros_like(pad_ref)
        for j, ref in enumerate((q_ref, f_ref, i_ref, g_ref)):
            for n in range(HGRN_SEQS):
                pad_ref[j, n, 0:rows, :] = ref[n * rows:(n + 1) * rows, :]
        o = _hgrn_step(*(pad_ref[j].reshape(nrow, D_A) for j in range(4)), lb, hn,
                       st_ref, sel_ref, cm_ref, b_ref, k_ref, rows)
        for n in range(HGRN_SEQS):
            o_ref[n * rows:(n + 1) * rows, :] = o[n * SUB:n * SUB + rows, :]
    else:
        @pl.when(t == 0)
        def _():
            st_ref[...] = jnp.zeros_like(st_ref)

        def body(j, carry):
            rs = pl.ds(pl.multiple_of(j * SUB, SUB), SUB)
            o = _hgrn_step(*(ref[:, rs, :].reshape(nrow, D_A) for ref in (q_ref, f_ref, i_ref, g_ref)),
                           lb, hn, st_ref, sel_ref, cm_ref, b_ref, k_ref, SUB)
            o_ref[:, rs, :] = o.reshape(HGRN_SEQS, SUB, D_A)
            return carry

        lax.fori_loop(0, rows // SUB, body, 0)

    def emit_state():
        for n in range(HGRN_SEQS):
            for h in range(H_A):
                s_ref[n, h] = st_ref[n, h].T

    if has_s0:
        emit_state()
    else:
        pl.when(t == pl.num_programs(1) - 1)(emit_state)


def _hgrn(u3, s0, lb, hn, seq=None):
    has_s0 = s0 is not None
    if has_s0:
        batch = u3.shape[0] // seq
        rows = seq
        grid = (batch // HGRN_SEQS, 1)
        blk = lambda cb: pl.BlockSpec((HGRN_SEQS * rows, D_A), lambda b, t: (b, cb))
        o_shape = (batch * seq, D_A)
    else:
        batch, seq, _ = u3.shape
        rows = 128
        grid = (batch // HGRN_SEQS, seq // rows)
        blk = lambda cb: pl.BlockSpec((HGRN_SEQS, rows, D_A), lambda b, t: (b, t, cb))
        o_shape = (batch, seq, D_A)
    sblk = pl.BlockSpec((HGRN_SEQS, H_A, DK, DV), lambda b, t: (b, 0, 0, 0))
    vec = pl.BlockSpec((1, D_A), lambda b, t: (0, 0))
    nrow = HGRN_SEQS * SUB
    scratch = [pltpu.VMEM((HGRN_SEQS, H_A, DV, DK), F32), pltpu.VMEM((SUB, SUB * DK), BF16),
               pltpu.VMEM((2, nrow, nrow), BF16), pltpu.VMEM((H_A, nrow, DK), F32),
               pltpu.VMEM((H_A, nrow, DK), F32)]
    if has_s0:
        scratch.append(pltpu.VMEM((4, HGRN_SEQS, SUB, D_A), F32))
    args = (u3, u3, u3, u3, lb, hn) + ((s0,) if has_s0 else ())
    return pl.pallas_call(
        functools.partial(_hgrn_kernel, rows=rows, has_s0=has_s0),
        grid=grid,
        in_specs=[blk(0), blk(1), blk(2), blk(3), vec, vec] + ([sblk] if has_s0 else []),
        out_specs=[blk(0), sblk],
        out_shape=[jax.ShapeDtypeStruct(o_shape, F32),
                   jax.ShapeDtypeStruct((batch, H_A, DK, DV), F32)],
        scratch_shapes=scratch,
        compiler_params=_cparams(("arbitrary", "arbitrary")),
        name="hgrn_state" if has_s0 else "hgrn_scan",
    )(*args)


def _pool_groups(full_ref, base, tm, cnt_fn, wp_ref, ps_ref, step=1):
    outs = []
    for gi, w in enumerate(POOL_WINDOWS):
        ch = slice(POOL_GROUP * gi, POOL_GROUP * (gi + 1))
        v = full_ref[base:base + tm, ch]
        s = v
        for j in range(1, w):
            s = s + full_ref[base - j * step:base - j * step + tm, ch]
        d = s / cnt_fn(w) - v
        outs.append(jnp.dot(d.astype(BF16), wp_ref[gi], preferred_element_type=F32))
    return jnp.concatenate(outs, axis=1) * ps_ref[...]


def _pool_prompt_kernel(v_ref, prev_ref, wp_ref, ps_ref, o_ref, hist_ref, full_ref, *, tm):
    t = pl.program_id(1)
    prev = prev_ref[...]
    full_ref[0:16, :] = jnp.where(t == 0, 0.0, prev)
    full_ref[16:16 + tm, :] = v_ref[...]
    pos = (lax.broadcasted_iota(I32, (tm, 1), 0) + t * tm + 1).astype(F32)
    o_ref[...] = _pool_groups(full_ref, 16, tm, lambda w: jnp.minimum(pos, float(w)),
                              wp_ref, ps_ref)

    @pl.when(t == pl.num_programs(1) - 1)
    def _():
        hist_ref[...] = full_ref[16 + tm - POOL_HIST:16 + tm, :]


def _pool_prompt(u, batch, seq, w_pool_bf, pool_scale):
    tm = min(1024, seq)
    nt = seq // tm
    vcol = (4 * D_A) // D_B
    return pl.pallas_call(
        functools.partial(_pool_prompt_kernel, tm=tm),
        grid=(batch, nt),
        in_specs=[pl.BlockSpec((tm, D_B), lambda b, t: (b * nt + t, vcol)),
                  pl.BlockSpec((16, D_B),
                               lambda b, t: (jnp.maximum((b * nt + t) * (tm // 16) - 1, 0), vcol)),
                  pl.BlockSpec((len(POOL_WINDOWS), POOL_GROUP, POOL_GROUP), lambda b, t: (0, 0, 0)),
                  pl.BlockSpec((1, D_B), lambda b, t: (0, 0))],
        out_specs=[pl.BlockSpec((tm, D_B), lambda b, t: (b * nt + t, 0)),
                   pl.BlockSpec((None, POOL_HIST, D_B), lambda b, t: (b, 0, 0))],
        out_shape=[jax.ShapeDtypeStruct((batch * seq, D_B), F32),
                   jax.ShapeDtypeStruct((batch, POOL_HIST, D_B), F32)],
        scratch_shapes=[pltpu.VMEM((16 + tm, D_B), F32)],
        compiler_params=_cparams(("arbitrary", "arbitrary")),
        name="pool_prompt",
    )(u, u, w_pool_bf, pool_scale)


def _pool_sample_kernel(v_ref, hist_ref, wp_ref, ps_ref, o_ref, nh_ref, full_ref, *,
                        seq, batch, start):
    full_ref[0:batch, :] = jnp.zeros((batch, D_B), F32)
    full_ref[batch:16 * batch, :] = hist_ref[...].reshape(POOL_HIST * batch, D_B)
    full_ref[16 * batch:(16 + seq) * batch, :] = v_ref[...].reshape(seq * batch, D_B)

    def cnt(w):
        rows = [jnp.full((batch, 1), float(min(start + t + 1, w)), F32) for t in range(seq)]
        return jnp.concatenate(rows, axis=0)

    o = _pool_groups(full_ref, 16 * batch, seq * batch, cnt, wp_ref, ps_ref, step=batch)
    o_ref[...] = o.reshape(seq, batch, D_B)
    nh_ref[...] = full_ref[(16 + seq - POOL_HIST) * batch:(16 + seq) * batch, :].reshape(
        POOL_HIST, batch, D_B)


def _pool_sample(u_t, hist_t, start, w_pool_bf, pool_scale):
    seq, batch, _ = u_t.shape
    vcol = (4 * D_A) // D_B
    return pl.pallas_call(
        functools.partial(_pool_sample_kernel, seq=seq, batch=batch, start=start),
        grid=(1,),
        in_specs=[pl.BlockSpec((seq, batch, D_B), lambda i: (0, 0, vcol)),
                  pl.BlockSpec((POOL_HIST, batch, D_B), lambda i: (0, 0, 0)),
                  pl.BlockSpec((len(POOL_WINDOWS), POOL_GROUP, POOL_GROUP), lambda i: (0, 0, 0)),
                  pl.BlockSpec((1, D_B), lambda i: (0, 0))],
        out_specs=[pl.BlockSpec((seq, batch, D_B), lambda i: (0, 0, 0)),
                   pl.BlockSpec((POOL_HIST, batch, D_B), lambda i: (0, 0, 0))],
        out_shape=[jax.ShapeDtypeStruct((seq, batch, D_B), F32),
                   jax.ShapeDtypeStruct((POOL_HIST, batch, D_B), F32)],
        scratch_shapes=[pltpu.VMEM(((16 + seq) * batch, D_B), F32)],
        compiler_params=_cparams(("arbitrary",)),
        name="pool_sample",
    )(u_t, hist_t, w_pool_bf, pool_scale)


def _store_slabs(ref, value):
    n = value.shape[0]
    for j in range(SLAB):
        ref[pl.ds(j, n, stride=SLAB), :] = value[:, LANES * j:LANES * (j + 1)]


def _load_slabs(ref, n):
    return [ref[pl.ds(j, n, stride=SLAB), :] for j in range(SLAB)]


def _slab_rows(row):
    return pl.ds(pl.multiple_of(row * SLAB, SLAB), SLAB)


def _outproj_kernel(x_ref, oa_ref, ob_ref, g1_ref, sh2_ref, sc2_ref, n2_ref, woa_ref, wob_ref,
                    wr_ref, br_ref, *rest):
    x1_ref, h2_ref, idx_ref, prob_ref, cnt_ref, lg_ref = rest[-6:]
    i = pl.program_id(0)
    slot = i % 2

    @pl.when(i == 0)
    def _():
        lg_ref[...] = jnp.zeros_like(lg_ref)

    logits = lg_ref[1 - slot]

    mix = (jnp.dot(oa_ref[...].astype(BF16), woa_ref[...], preferred_element_type=F32)
           + jnp.dot(ob_ref[...].astype(BF16), wob_ref[...], preferred_element_type=F32))
    x1 = x_ref[...] + g1_ref[...] * mix
    x1_ref[...] = x1
    h2 = x1 * lax.rsqrt(jnp.mean(x1 * x1, axis=-1, keepdims=True) + EPS) * n2_ref[...]
    h2 = h2 * (1.0 + sc2_ref[...]) + sh2_ref[...]
    _store_slabs(h2_ref, h2)
    lg_ref[slot] = jnp.dot(h2.astype(BF16), wr_ref[...], preferred_element_type=F32) + br_ref[...]

    tm = logits.shape[0]
    lane = lax.broadcasted_iota(I32, (tm, N_EXPERTS), 1)
    kcol = lax.broadcasted_iota(I32, (tm, TOP_K), 1)
    vals = jnp.zeros((tm, TOP_K), F32)
    idxs = jnp.zeros((tm, TOP_K), I32)
    for k in range(TOP_K):
        m = jnp.max(logits, axis=-1, keepdims=True)
        am = jnp.min(jnp.where(logits == m, lane, N_EXPERTS), axis=-1, keepdims=True)
        vals = jnp.where(kcol == k, m, vals)
        idxs = jnp.where(kcol == k, am, idxs)
        logits = jnp.where(lane == am, -jnp.inf, logits)
    e = jnp.exp(vals - vals[:, 0:1])
    prob_ref[...] = e / jnp.sum(e, axis=-1, keepdims=True)
    idx_ref[...] = idxs
    member = jnp.zeros((tm, N_EXPERTS), F32)
    for k in range(TOP_K):
        member = member + (idxs[:, k:k + 1] == lane).astype(F32)
    cnt_ref[...] = jnp.sum(member, axis=0, keepdims=True)


def _outproj(x, o_a, o_b, mod, per_token, seq, norm2, w_out_bf, w_router_bf, b_router,
             n_total, tile0, joint=None):
    n = x.shape[0]
    tm = min(256, n)
    tps = max(seq // tm, 1)
    nt = n // tm
    cur = lambda i: jnp.minimum(i, nt - 1)
    prv = lambda i: jnp.maximum(i - 1, 0)
    row = lambda width: pl.BlockSpec((tm, width), lambda i: (cur(i), 0))
    jrow = lambda width: pl.BlockSpec((tm, width), lambda i: (prv(i) + tile0, 0))
    const = lambda shape: pl.BlockSpec(shape, lambda i: (0,) * len(shape))
    joint = () if joint is None else tuple(joint)
    n_in = 11
    return pl.pallas_call(
        _outproj_kernel,
        grid=(nt + 1,),
        in_specs=[row(D_MODEL), row(D_A), row(D_B),
                  _mod_spec(per_token, tm, tps, 2, nt - 1),
                  _mod_spec(per_token, tm, tps, 3, nt - 1),
                  _mod_spec(per_token, tm, tps, 4, nt - 1),
                  const((1, D_MODEL)),
                  pl.BlockSpec((D_A, D_MODEL), lambda i: (0, 0)),
                  pl.BlockSpec((D_B, D_MODEL), lambda i: (1, 0)),
                  const((D_MODEL, N_EXPERTS)), const((1, N_EXPERTS))]
                 + [pl.BlockSpec(memory_space=pl.ANY)] * len(joint),
        out_specs=[row(D_MODEL), pl.BlockSpec((tm * SLAB, LANES), lambda i: (cur(i) + tile0, 0)),
                   jrow(TOP_K), jrow(TOP_K),
                   pl.BlockSpec((None, 1, N_EXPERTS), lambda i: (prv(i) + tile0, 0, 0))],
        out_shape=[jax.ShapeDtypeStruct((n, D_MODEL), F32),
                   jax.ShapeDtypeStruct((n_total * SLAB, LANES), F32),
                   jax.ShapeDtypeStruct((n_total, TOP_K), I32),
                   jax.ShapeDtypeStruct((n_total, TOP_K), F32),
                   jax.ShapeDtypeStruct((n_total // tm, 1, N_EXPERTS), F32)],
        input_output_aliases={n_in + j: 1 + j for j in range(len(joint))},
        scratch_shapes=[pltpu.VMEM((2, tm, N_EXPERTS), F32)],
        compiler_params=_cparams(("arbitrary",)),
        name="outproj",
    )(x, o_a, o_b, mod, mod, mod, norm2.reshape(1, D_MODEL), w_out_bf, w_out_bf,
      w_router_bf, b_router.reshape(1, N_EXPERTS), *joint)


def _rank_kernel(idx_ref, start_ref, dest_ref, carry_ref, *, te):
    i = pl.program_id(0)
    idx = idx_ref[...]
    lane = lax.broadcasted_iota(I32, (te, N_EXPERTS), 1)
    onehots = [(idx[:, k:k + 1] == lane) for k in range(TOP_K)]
    member = jnp.zeros((te, N_EXPERTS), F32)
    for oh in onehots:
        member = member + oh.astype(F32)

    @pl.when(i == 0)
    def _():
        carry_ref[...] = start_ref[...]

    r = lax.broadcasted_iota(I32, (te, te), 0)
    c = lax.broadcasted_iota(I32, (te, te), 1)
    earlier = (r > c).astype(BF16)
    base = carry_ref[...] + jnp.dot(earlier, member.astype(BF16), preferred_element_type=F32)
    kcol = lax.broadcasted_iota(I32, (te, TOP_K), 1)
    dest = jnp.zeros((te, TOP_K), F32)
    for k, oh in enumerate(onehots):
        dk = jnp.sum(jnp.where(oh, base, 0.0), axis=-1, keepdims=True)
        dest = jnp.where(kcol == k, dk, dest)
    dest_ref[...] = dest.astype(I32)
    carry_ref[...] += jnp.sum(member, axis=0, keepdims=True)


def _rank(top_idx, starts):
    n = top_idx.shape[0]
    te = 512
    return pl.pallas_call(
        functools.partial(_rank_kernel, te=te),
        grid=(n // te,),
        in_specs=[pl.BlockSpec((te, TOP_K), lambda i: (i, 0)),
                  pl.BlockSpec((1, N_EXPERTS), lambda i: (0, 0))],
        out_specs=pl.BlockSpec((te, TOP_K), lambda i: (i, 0)),
        out_shape=jax.ShapeDtypeStruct((n, TOP_K), I32),
        scratch_shapes=[pltpu.VMEM((1, N_EXPERTS), F32)],
        compiler_params=_cparams(("arbitrary",)),
        name="rank",
    )(top_idx, starts)


def _dispatch_kernel(last_ref, ntile_ref, used_ref, dest_ref, h_ref, xs_ref, zero_ref, sem, zsem,
                     *, n_tiles):
    i = pl.program_id(0)

    @pl.when(i == 0)
    def _():
        zero_ref[...] = jnp.zeros_like(zero_ref)

        def clear(tile):
            rows = pl.ds(pl.multiple_of(tile * (ROW_TILE * SLAB), ROW_TILE * SLAB), ROW_TILE * SLAB)
            return pltpu.make_async_copy(zero_ref, xs_ref.at[rows], zsem)

        def over_tail(fn):
            def body(t, carry):
                fn(clear(t))
                return carry
            lax.fori_loop(used_ref[0], n_tiles, body, 0)

        for fn in (lambda c: c.start(), lambda c: c.wait()):
            for e in range(N_EXPERTS):
                @pl.when(ntile_ref[e] > 0)
                def _():
                    fn(clear(last_ref[e]))
            over_tail(fn)

    def body(r, carry):
        for k in range(TOP_K):
            d = dest_ref[r * TOP_K + k]
            pltpu.make_async_copy(h_ref.at[_slab_rows(r)], xs_ref.at[_slab_rows(d)], sem).start(
                priority=k % 2)
        return carry

    lax.fori_loop(0, TOK_TILE, body, 0, unroll=2)
    for k in range(TOP_K):
        pltpu.make_async_copy(h_ref, xs_ref.at[pl.ds(0, TOK_TILE * SLAB)], sem).wait()


def _dispatch(h2, dest_flat, last_tile, ntiles, used, n_tiles):
    n = h2.shape[0] // SLAB
    grid_spec = pltpu.PrefetchScalarGridSpec(
        num_scalar_prefetch=3,
        grid=(n // TOK_TILE,),
        in_specs=[pl.BlockSpec((TOK_TILE * TOP_K,), lambda i, *_: (i,), memory_space=pltpu.SMEM),
                  pl.BlockSpec((TOK_TILE * SLAB, LANES), lambda i, *_: (i, 0))],
        out_specs=pl.BlockSpec(memory_space=pl.ANY),
        scratch_shapes=[pltpu.VMEM((ROW_TILE * SLAB, LANES), F32),
                        pltpu.SemaphoreType.DMA(()), pltpu.SemaphoreType.DMA(())],
    )
    return pl.pallas_call(
        functools.partial(_dispatch_kernel, n_tiles=n_tiles),
        grid_spec=grid_spec,
        out_shape=jax.ShapeDtypeStruct((n_tiles * ROW_TILE * SLAB, LANES), F32),
        compiler_params=_cparams(("arbitrary",)),
        name="dispatch",
    )(last_tile, ntiles, used, dest_flat, h2)


def _experts_kernel(te_ref, used_ref, grp_ref, nxt_ref, xs_ref, wgu_ref, bgu_ref, wd_ref, bd_ref,
                    ys_ref, wgu_f, wd_f, wgu_bf, wd_bf, act_bf, xring, yring, wsem, xsem, ysem,
                    *, n_tiles):
    used = used_ref[0]
    tile_rows = ROW_TILE * SLAB

    def weights(e, ws):
        return (pltpu.make_async_copy(wgu_ref.at[e], wgu_f.at[ws], wsem.at[ws, 0]),
                pltpu.make_async_copy(wd_ref.at[e], wd_f.at[ws], wsem.at[ws, 1]))

    def tile_rows_of(t):
        return pl.ds(pl.multiple_of(t * tile_rows, tile_rows), tile_rows)

    def x_tile(t):
        t = jnp.asarray(t, I32)
        return pltpu.make_async_copy(xs_ref.at[tile_rows_of(t)], xring.at[t % X_RING],
                                     xsem.at[t % X_RING])

    def y_tile(t):
        return pltpu.make_async_copy(yring.at[t % 2], ys_ref.at[tile_rows_of(t)], ysem.at[t % 2])

    for cp in weights(te_ref[0], 0):
        cp.start()
    for t in range(X_RING - 1):
        @pl.when(t < used)
        def _():
            x_tile(t).start()

    def step(i, carry):
        @pl.when(i + X_RING - 1 < used)
        def _():
            x_tile(i + X_RING - 1).start()

        e = te_ref[i]
        ws = grp_ref[i] % 2
        fresh = (i == 0) | (e != te_ref[jnp.maximum(i - 1, 0)])

        @pl.when(fresh)
        def _():
            for cp in weights(e, ws):
                cp.wait()

            @pl.when(nxt_ref[i] >= 0)
            def _():
                for cp in weights(nxt_ref[i], 1 - ws):
                    cp.start()

            wgu_bf[...] = wgu_f[ws].astype(BF16)
            wd_bf[...] = wd_f[ws].astype(BF16)

        x_tile(i).wait()

        @pl.when(i >= 2)
        def _():
            y_tile(i - 2).wait()

        x = jnp.concatenate([p.astype(BF16) for p in _load_slabs(xring.at[i % X_RING], ROW_TILE)],
                            axis=1)
        bgu = bgu_ref[pl.ds(e, 1), :]
        bd = bd_ref[pl.ds(e, 1), :]
        cw = D_FF // FF_CHUNKS
        for c in range(FF_CHUNKS):
            gate = jnp.dot(x, wgu_bf[:, c * cw:(c + 1) * cw], preferred_element_type=F32)
            gate = jnp.minimum(gate + bgu[:, c * cw:(c + 1) * cw], SWIGLU_LIMIT)
            up = jnp.dot(x, wgu_bf[:, D_FF + c * cw:D_FF + (c + 1) * cw], preferred_element_type=F32)
            up = jnp.clip(up + bgu[:, D_FF + c * cw:D_FF + (c + 1) * cw], -SWIGLU_LIMIT, SWIGLU_LIMIT)
            act = gate * jax.nn.sigmoid(SWIGLU_ALPHA * gate) * (up + 1.0)
            act_bf[:, c * cw:(c + 1) * cw] = act.astype(BF16)
        ow = D_MODEL // FF_CHUNKS
        for c in range(FF_CHUNKS):
            out = jnp.dot(act_bf[...], wd_bf[:, c * ow:(c + 1) * ow], preferred_element_type=F32)
            out = out + bd[:, c * ow:(c + 1) * ow]
            for j in range(ow // LANES):
                yring[i % 2, pl.ds(c * (ow // LANES) + j, ROW_TILE, stride=SLAB), :] = (
                    out[:, LANES * j:LANES * (j + 1)])
        y_tile(i).start()
        return carry

    lax.fori_loop(0, used, step, 0)

    @pl.when(used >= 2)
    def _():
        y_tile(used - 2).wait()
    y_tile(used - 1).wait()

    yring[0] = jnp.zeros(yring.shape[1:], F32)

    def clear(fn):
        def body(t, carry):
            fn(pltpu.make_async_copy(yring.at[0], ys_ref.at[tile_rows_of(t)], ysem.at[0]))
            return carry
        lax.fori_loop(used, n_tiles, body, 0)

    clear(lambda cp: cp.start())
    clear(lambda cp: cp.wait())


def _experts(xs, tile_expert, used, grp, nxt, w_gu, b_gu, w_down, b_down):
    rows = xs.shape[0] // SLAB
    nt = rows // ROW_TILE
    whole = lambda shape: pl.BlockSpec(shape, lambda i, *_: (0,) * len(shape))
    grid_spec = pltpu.PrefetchScalarGridSpec(
        num_scalar_prefetch=4,
        grid=(1,),
        in_specs=[pl.BlockSpec(memory_space=pl.ANY),
                  pl.BlockSpec(memory_space=pl.ANY),
                  whole((N_EXPERTS, 2 * D_FF)),
                  pl.BlockSpec(memory_space=pl.ANY),
                  whole((N_EXPERTS, D_MODEL))],
        out_specs=pl.BlockSpec(memory_space=pl.ANY),
        scratch_shapes=[pltpu.VMEM((2, D_MODEL, 2 * D_FF), F32), pltpu.VMEM((2, D_FF, D_MODEL), F32),
                        pltpu.VMEM((D_MODEL, 2 * D_FF), BF16), pltpu.VMEM((D_FF, D_MODEL), BF16),
                        pltpu.VMEM((ROW_TILE, D_FF), BF16),
                        pltpu.VMEM((X_RING, ROW_TILE * SLAB, LANES), F32),
                        pltpu.VMEM((2, ROW_TILE * SLAB, LANES), F32),
                        pltpu.SemaphoreType.DMA((2, 2)), pltpu.SemaphoreType.DMA((X_RING,)),
                        pltpu.SemaphoreType.DMA((2,))],
    )
    return pl.pallas_call(
        functools.partial(_experts_kernel, n_tiles=nt),
        grid_spec=grid_spec,
        out_shape=jax.ShapeDtypeStruct((rows * SLAB, LANES), F32),
        compiler_params=_cparams(("arbitrary",)),
        name="experts",
    )(tile_expert, used, grp, nxt, xs, w_gu, b_gu, w_down, b_down)


def _combine_kernel(dest_ref, next_ref, ys_ref, x1_ref, p_ref, g2_ref, fn_ref, y_ref, buf_ref, sem,
                    *, last):
    i = pl.program_id(0)
    slot = i % 2

    def issue(d_ref, sl):
        def body(r, carry):
            for k in range(TOP_K):
                d = d_ref[r * TOP_K + k]
                pltpu.make_async_copy(ys_ref.at[_slab_rows(d)], buf_ref.at[sl, k, _slab_rows(r)],
                                      sem.at[sl]).start(priority=k % 2)
            return carry
        lax.fori_loop(0, TOK_TILE, body, 0, unroll=2)

    @pl.when(i == 0)
    def _():
        issue(dest_ref, 0)

    @pl.when(i + 1 < pl.num_programs(0))
    def _():
        issue(next_ref, 1 - slot)

    for k in range(TOP_K):
        pltpu.make_async_copy(ys_ref.at[pl.ds(0, TOK_TILE * SLAB)], buf_ref.at[slot, k],
                              sem.at[slot]).wait()

    p = p_ref[...]
    groups = None
    for k in range(TOP_K):
        part = [p[:, k:k + 1] * g for g in _load_slabs(buf_ref.at[slot, k], TOK_TILE)]
        groups = part if groups is None else [a + b for a, b in zip(groups, part)]
    moe = jnp.concatenate(groups, axis=1)
    x2 = x1_ref[...] + g2_ref[...] * moe
    if last:
        x2 = x2 * lax.rsqrt(jnp.mean(x2 * x2, axis=-1, keepdims=True) + EPS) * fn_ref[...]
    y_ref[...] = x2


def _combine(ys, dest_flat, x1, probs, mod, per_token, seq, tile0, final_norm, last):
    n = x1.shape[0]
    tm = TOK_TILE
    tps = max(seq // tm, 1)
    nt = n // tm
    return pl.pallas_call(
        functools.partial(_combine_kernel, last=last),
        grid=(nt,),
        in_specs=[pl.BlockSpec((tm * TOP_K,), lambda i: (i + tile0,), memory_space=pltpu.SMEM),
                  pl.BlockSpec((tm * TOP_K,), lambda i: (jnp.minimum(i + 1, nt - 1) + tile0,),
                               memory_space=pltpu.SMEM),
                  pl.BlockSpec(memory_space=pl.ANY),
                  pl.BlockSpec((tm, D_MODEL), lambda i: (i, 0)),
                  pl.BlockSpec((tm, TOP_K), lambda i: (i + tile0, 0)),
                  _mod_spec(per_token, tm, tps, 5),
                  pl.BlockSpec((1, D_MODEL), lambda i: (0, 0))],
        out_specs=pl.BlockSpec((tm, D_MODEL), lambda i: (i, 0)),
        out_shape=jax.ShapeDtypeStruct((n, D_MODEL), F32),
        scratch_shapes=[pltpu.VMEM((2, TOP_K, tm * SLAB, LANES), F32),
                        pltpu.SemaphoreType.DMA((2,))],
        compiler_params=_cparams(("arbitrary",)),
        name="combine",
    )(dest_flat, dest_flat, ys, x1, probs, mod, final_norm.reshape(1, D_MODEL))


def kernel(x_prompt, x_sample, state_hgrn, state_pool, c_prompt, c_sample, w_ada, b_ada, norm1,
           norm2, w_in, lower_bounds, hgrn_norm, w_pool, pool_scale, w_out, w_router, b_router,
           w_gu, b_gu, w_down, b_down, final_norm):
    bp, seq_p, _ = x_prompt.shape
    bs, seq_s, _ = x_sample.shape
    np_, ns = bp * seq_p, bs * seq_s
    depth = w_ada.shape[0]

    lbs = jnp.cumsum(jax.nn.softmax(lower_bounds.astype(F32), axis=0), axis=0)
    xp = x_prompt.reshape(np_, D_MODEL)
    xs_ = x_sample.reshape(ns, D_MODEL)
    c_all = jnp.concatenate([jnp.repeat(c_sample, seq_s, axis=0), c_prompt], axis=0)

    sp_l, hp_l, ss_l, hs_l = [], [], [], []
    for l in range(depth):
        w_in_bf = w_in[l].astype(BF16)
        w_out_bf = w_out[l].astype(BF16)
        w_pool_bf = w_pool[l].astype(BF16)
        w_router_bf = w_router[l].astype(BF16)
        lb = lbs[l].reshape(1, D_A)
        hn = hgrn_norm[l].reshape(1, D_A)
        psc = pool_scale[l].reshape(1, D_B)

        mod = _ada(c_all, w_ada[l], b_ada[l])
        mod_p = mod[ns:].reshape(bp, 1, 6 * D_MODEL)
        mod_s = mod

        up = _inproj(xp, mod_p, False, seq_p, norm1[l], w_in_bf)
        us = _inproj(xs_, mod_s, True, seq_s, norm1[l], w_in_bf)

        oa_p, s_p = _hgrn(up.reshape(bp, seq_p, D_IN), None, lb, hn)
        oa_p = oa_p.reshape(np_, D_A)
        oa_s, s_s = _hgrn(us, state_hgrn[l], lb, hn, seq=seq_s)
        ob_p, h_p = _pool_prompt(up, bp, seq_p, w_pool_bf, psc)
        ob_t, h_t = _pool_sample(us.reshape(bs, seq_s, D_IN).transpose(1, 0, 2),
                                 state_pool[l].transpose(1, 0, 2), PAST_LEN, w_pool_bf, psc)
        ob_s, h_s = ob_t.transpose(1, 0, 2).reshape(ns, D_B), h_t.transpose(1, 0, 2)

        n_tok = np_ + ns
        n_tiles = (n_tok * TOP_K) // ROW_TILE + N_EXPERTS
        x1p, *joint = _outproj(xp, oa_p, ob_p, mod_p, False, seq_p, norm2[l], w_out_bf,
                               w_router_bf, b_router[l], n_tok, 0)
        x1s, h2, top_idx, probs, tile_counts = _outproj(
            xs_, oa_s, ob_s, mod_s, True, seq_s, norm2[l], w_out_bf, w_router_bf, b_router[l],
            n_tok, np_ // TOK_TILE, joint)

        counts = jnp.sum(tile_counts, axis=(0, 1)).astype(I32)
        tiles_e = (counts + ROW_TILE - 1) // ROW_TILE
        ends = jnp.cumsum(tiles_e)
        starts = ((ends - tiles_e) * ROW_TILE).astype(F32).reshape(1, N_EXPERTS)
        dest = _rank(top_idx, starts)
        dest_flat = dest.reshape(n_tok * TOP_K)
        used = ends[-1:].astype(I32)
        tile_expert = jnp.minimum(
            jnp.sum(jnp.arange(n_tiles, dtype=I32)[:, None] >= ends[None, :], axis=1),
            N_EXPERTS - 1).astype(I32)
        last_tile = (ends - 1).astype(I32)
        grp = jnp.cumsum(jnp.concatenate(
            [jnp.zeros((1,), I32), (tile_expert[1:] != tile_expert[:-1]).astype(I32)])).astype(I32)
        eids = jnp.arange(N_EXPERTS, dtype=I32)
        later = (eids[None, :] > eids[:, None]) & (tiles_e > 0)[None, :]
        nxt_e = jnp.min(jnp.where(later, eids[None, :], N_EXPERTS), axis=1)
        nxt = jnp.where(nxt_e < N_EXPERTS, nxt_e, -1)[tile_expert].astype(I32)

        xsort = _dispatch(h2, dest_flat, last_tile, tiles_e.astype(I32), used, n_tiles)
        ysort = _experts(xsort, tile_expert, used, grp, nxt, w_gu[l], b_gu[l], w_down[l], b_down[l])

        last = l == depth - 1
        xp = _combine(ysort, dest_flat, x1p, probs, mod_p, False, seq_p, 0, final_norm, last)
        xs_ = _combine(ysort, dest_flat, x1s, probs, mod_s, True, seq_s, np_ // TOK_TILE,
                       final_norm, last)

        sp_l.append(s_p)
        hp_l.append(h_p)
        ss_l.append(s_s)
        hs_l.append(h_s)

    return (xp.reshape(bp, seq_p, D_MODEL), xs_.reshape(bs, seq_s, D_MODEL),
            jnp.stack(sp_l), jnp.stack(hp_l), jnp.stack(ss_l), jnp.stack(hs_l))
```

```python
import functools

import jax
import jax.numpy as jnp
from jax import lax
from jax.experimental import pallas as pl
from jax.experimental.pallas import tpu as pltpu

F32, BF16, I32 = jnp.float32, jnp.bfloat16, jnp.int32

D_MODEL = 1024
D_A = 512
H_A = 4
DK = 128
DV = 128
D_B = 512
POOL_WINDOWS = (2, 4, 8, 16)
POOL_GROUP = 128
POOL_HIST = 15
D_IN = 4 * D_A + D_B
N_EXPERTS = 32
TOP_K = 4
D_FF = 1024
SWIGLU_LIMIT = 7.0
SWIGLU_ALPHA = 1.702
EPS = 1e-6
PAST_LEN = 16384

SUB = 16
HALF = SUB // 2
HGRN_SEQS = 8
ROW_TILE = 256
TOK_TILE = 256
FF_CHUNKS = 4
X_RING = 3
LANES = 128
SLAB = D_MODEL // LANES
VMEM_LIMIT = 56 * 1024 * 1024


def _cparams(sem):
    return pltpu.CompilerParams(dimension_semantics=sem, vmem_limit_bytes=VMEM_LIMIT)


def _silu(x):
    return x * jax.nn.sigmoid(x)


def _mod_spec(per_token, tm, tiles_per_seq, col, last_tile=None):
    tile = (lambda i: i) if last_tile is None else (lambda i: jnp.minimum(i, last_tile))
    if per_token:
        return pl.BlockSpec((tm, D_MODEL), lambda i: (tile(i), col))
    return pl.BlockSpec((None, 1, D_MODEL), lambda i: (tile(i) // tiles_per_seq, 0, col))


def _ada_kernel(c_ref, w_ref, b_ref, o_ref):
    c = c_ref[...]
    o_ref[...] = jnp.dot(_silu(c).astype(BF16), w_ref[...].astype(BF16),
                         preferred_element_type=F32) + b_ref[...]


def _ada(c_all, w_ada, b_ada):
    rows = c_all.shape[0]
    n = w_ada.shape[1]
    return pl.pallas_call(
        _ada_kernel,
        grid=(n // D_MODEL,),
        in_specs=[pl.BlockSpec((rows, D_MODEL), lambda j: (0, 0)),
                  pl.BlockSpec((D_MODEL, D_MODEL), lambda j: (0, j)),
                  pl.BlockSpec((1, D_MODEL), lambda j: (0, j))],
        out_specs=pl.BlockSpec((rows, D_MODEL), lambda j: (0, j)),
        out_shape=jax.ShapeDtypeStruct((rows, n), F32),
        compiler_params=_cparams(("arbitrary",)),
        name="ada",
    )(c_all, w_ada, b_ada.reshape(1, n))


def _inproj_kernel(x_ref, sh_ref, sc_ref, n1_ref, w_ref, u_ref):
    x = x_ref[...]
    h = x * lax.rsqrt(jnp.mean(x * x, axis=-1, keepdims=True) + EPS) * n1_ref[...]
    h = h * (1.0 + sc_ref[...]) + sh_ref[...]
    u_ref[...] = jnp.dot(h.astype(BF16), w_ref[...], preferred_element_type=F32)


def _inproj(x, mod, per_token, seq, norm1, w_in_bf):
    n = x.shape[0]
    tm = min(512, n)
    tps = max(seq // tm, 1)
    return pl.pallas_call(
        _inproj_kernel,
        grid=(n // tm,),
        in_specs=[pl.BlockSpec((tm, D_MODEL), lambda i: (i, 0)),
                  _mod_spec(per_token, tm, tps, 0),
                  _mod_spec(per_token, tm, tps, 1),
                  pl.BlockSpec((1, D_MODEL), lambda i: (0, 0)),
                  pl.BlockSpec((D_MODEL, D_IN), lambda i: (0, 0))],
        out_specs=pl.BlockSpec((tm, D_IN), lambda i: (i, 0)),
        out_shape=jax.ShapeDtypeStruct((n, D_IN), F32),
        compiler_params=_cparams(("arbitrary",)),
        name="inproj",
    )(x, mod, mod, norm1.reshape(1, D_MODEL), w_in_bf)


def _block_select():
    r = lax.broadcasted_iota(I32, (SUB, SUB * DK), 0)
    c = lax.broadcasted_iota(I32, (SUB, SUB * DK), 1) >> (DK.bit_length() - 1)
    return (r == c).astype(BF16)


def _hgrn_step(q, fp, iv, g, lb, hn, st_ref, sel_ref, cm_ref, b_ref, k_ref, valid):
    nrow = HGRN_SEQS * SUB
    f = lb + (1.0 - lb) * jax.nn.sigmoid(fp)
    lf = jnp.log2(f)
    kk = 1.0 - f
    if valid < SUB:
        live = (lax.broadcasted_iota(I32, (nrow, D_A), 0) & (SUB - 1)) < valid
        lf = jnp.where(live, lf, 0.0)
        kk = jnp.where(live, kk, 0.0)
    qa = _silu(q)

    hi = lf.astype(BF16)
    lo = (lf - hi.astype(F32)).astype(BF16)
    cum, tot = cm_ref[0], cm_ref[1]
    b = jnp.dot(cum, hi, preferred_element_type=F32) + jnp.dot(cum, lo, preferred_element_type=F32)
    be = jnp.dot(tot, hi, preferred_element_type=F32) + jnp.dot(tot, lo, preferred_element_type=F32)
    qt = (qa * jnp.exp2(b)).astype(BF16)
    kh = (kk * jnp.exp2(be - b)).astype(BF16)
    dec = jnp.exp2(be)
    ivb = iv.astype(BF16)
    for h in range(H_A):
        b_ref[h] = b[:, DK * h:DK * (h + 1)]
        k_ref[h] = kk[:, DK * h:DK * (h + 1)]

    zero_half = jnp.zeros((HALF, D_A), F32)
    tr = lax.broadcasted_iota(I32, (H_A * SUB, SUB), 0) & (SUB - 1)
    tc = lax.broadcasted_iota(I32, (H_A * SUB, SUB), 1)
    sg = _silu(g)

    own = ((lax.broadcasted_iota(I32, (nrow, HGRN_SEQS * DK), 0) >> (SUB.bit_length() - 1))
           == (lax.broadcasted_iota(I32, (nrow, HGRN_SEQS * DK), 1) >> (DK.bit_length() - 1)))
    inter = [None] * (HGRN_SEQS * H_A)
    for h in range(H_A):
        sl = slice(DK * h, DK * (h + 1))
        kbd = jnp.where(own, jnp.concatenate([kh[:, sl]] * HGRN_SEQS, axis=1), jnp.zeros((), BF16))
        ut_all = lax.dot_general(ivb[:, sl], kbd, (((0,), (0,)), ((), ())),
                                 preferred_element_type=F32)
        for n in range(HGRN_SEQS):
            rows = slice(n * SUB, (n + 1) * SUB)
            st = st_ref[n, h]
            inter[n * H_A + h] = lax.dot_general(qt[rows, sl], st.astype(BF16),
                                                 (((1,), (1,)), ((), ())),
                                                 preferred_element_type=F32)
            dtile = dec[n * SUB:n * SUB + HALF, sl]
            st_ref[n, h] = ((st.reshape(DV // HALF, HALF, DK) * dtile[None]).reshape(DV, DK)
                            + ut_all[:, DK * n:DK * (n + 1)])

    scores = []
    for n in range(HGRN_SEQS):
        per_s = []
        for s in range(SUB):
            if s >= valid:
                per_s.append(jnp.zeros((SUB, D_A), BF16))
                continue
            row = n * SUB + s
            b_s = jnp.concatenate([b_ref[h, pl.ds(row, HALF, stride=0), :] for h in range(H_A)], axis=1)
            k_s = jnp.concatenate([k_ref[h, pl.ds(row, HALF, stride=0), :] for h in range(H_A)], axis=1)
            halves = []
            for hf in range(2):
                if (hf == 0 and s >= HALF) or hf * HALF >= valid:
                    halves.append(zero_half)
                    continue
                r0 = n * SUB + hf * HALF
                e = jnp.exp2(jnp.minimum(b[r0:r0 + HALF] - b_s, 0.0))
                halves.append(e * (qa[r0:r0 + HALF] * k_s))
            per_s.append(jnp.concatenate(halves, axis=0).astype(BF16))
        pcat = jnp.concatenate(
            [jnp.concatenate([p[:, DK * h:DK * (h + 1)] for p in per_s], axis=1) for h in range(H_A)],
            axis=0)
        a = lax.dot_general(pcat, sel_ref[...], (((1,), (1,)), ((), ())),
                            preferred_element_type=F32)
        scores.append(jnp.where(tr >= tc, a, 0.0).astype(BF16))

    outs = []
    for n in range(HGRN_SEQS):
        rows = slice(n * SUB, (n + 1) * SUB)
        heads = []
        for h in range(H_A):
            sl = slice(DK * h, DK * (h + 1))
            o = inter[n * H_A + h] + jnp.dot(scores[n][h * SUB:(h + 1) * SUB, :], ivb[rows, sl],
                                             preferred_element_type=F32)
            o = o * lax.rsqrt(jnp.mean(o * o, axis=-1, keepdims=True) + EPS) * hn[:, sl]
            heads.append(o)
        outs.append(jnp.concatenate(heads, axis=1) * sg[rows])
    return jnp.concatenate(outs, axis=0)


def _hgrn_kernel(*refs, rows, has_s0):
    if has_s0:
        (q_ref, f_ref, i_ref, g_ref, lb_ref, hn_ref, s0_ref, o_ref, s_ref,
         st_ref, sel_ref, cm_ref, b_ref, k_ref, pad_ref) = refs
    else:
        (q_ref, f_ref, i_ref, g_ref, lb_ref, hn_ref, o_ref, s_ref,
         st_ref, sel_ref, cm_ref, b_ref, k_ref) = refs
    t = pl.program_id(1)
    nrow = HGRN_SEQS * SUB

    @pl.when((t == 0) & (pl.program_id(0) == 0))
    def _():
        sel_ref[...] = _block_select()
        r = lax.broadcasted_iota(I32, (nrow, nrow), 0)
        c = lax.broadcasted_iota(I32, (nrow, nrow), 1)
        shift = SUB.bit_length() - 1
        same = (r >> shift) == (c >> shift)
        cm_ref[0] = (same & (r >= c)).astype(BF16)
        cm_ref[1] = same.astype(BF16)

    lb = lb_ref[...]
    hn = hn_ref[...]
    if has_s0:
        for n in range(HGRN_SEQS):
            for h in range(H_A):
                st_ref[n, h] = s0_ref[n, h].T
        pad_ref[...] = jnp.zeros_like(pad_ref)
        for j, ref in enumerate((q_ref, f_ref, i_ref, g_ref)):
            for n in range(HGRN_SEQS):
                pad_ref[j, n, 0:rows, :] = ref[n * rows:(n + 1) * rows, :]
        o = _hgrn_step(*(pad_ref[j].reshape(nrow, D_A) for j in range(4)), lb, hn,
                       st_ref, sel_ref, cm_ref, b_ref, k_ref, rows)
        for n in range(HGRN_SEQS):
            o_ref[n * rows:(n + 1) * rows, :] = o[n * SUB:n * SUB + rows, :]
    else:
        @pl.when(t == 0)
        def _():
            st_ref[...] = jnp.zeros_like(st_ref)

        def body(j, carry):
            rs = pl.ds(pl.multiple_of(j * SUB, SUB), SUB)
            o = _hgrn_step(*(ref[:, rs, :].reshape(nrow, D_A) for ref in (q_ref, f_ref, i_ref, g_ref)),
                           lb, hn, st_ref, sel_ref, cm_ref, b_ref, k_ref, SUB)
            o_ref[:, rs, :] = o.reshape(HGRN_SEQS, SUB, D_A)
            return carry

        lax.fori_loop(0, rows // SUB, body, 0)

    def emit_state():
        for n in range(HGRN_SEQS):
            for h in range(H_A):
                s_ref[n, h] = st_ref[n, h].T

    if has_s0:
        emit_state()
    else:
        pl.when(t == pl.num_programs(1) - 1)(emit_state)


def _hgrn(u3, s0, lb, hn, seq=None):
    has_s0 = s0 is not None
    if has_s0:
        batch = u3.shape[0] // seq
        rows = seq
        grid = (batch // HGRN_SEQS, 1)
        blk = lambda cb: pl.BlockSpec((HGRN_SEQS * rows, D_A), lambda b, t: (b, cb))
        o_shape = (batch * seq, D_A)
    else:
        batch, seq, _ = u3.shape
        rows = 128
        grid = (batch // HGRN_SEQS, seq // rows)
        blk = lambda cb: pl.BlockSpec((HGRN_SEQS, rows, D_A), lambda b, t: (b, t, cb))
        o_shape = (batch, seq, D_A)
    sblk = pl.BlockSpec((HGRN_SEQS, H_A, DK, DV), lambda b, t: (b, 0, 0, 0))
    vec = pl.BlockSpec((1, D_A), lambda b, t: (0, 0))
    nrow = HGRN_SEQS * SUB
    scratch = [pltpu.VMEM((HGRN_SEQS, H_A, DV, DK), F32), pltpu.VMEM((SUB, SUB * DK), BF16),
               pltpu.VMEM((2, nrow, nrow), BF16), pltpu.VMEM((H_A, nrow, DK), F32),
               pltpu.VMEM((H_A, nrow, DK), F32)]
    if has_s0:
        scratch.append(pltpu.VMEM((4, HGRN_SEQS, SUB, D_A), F32))
    args = (u3, u3, u3, u3, lb, hn) + ((s0,) if has_s0 else ())
    return pl.pallas_call(
        functools.partial(_hgrn_kernel, rows=rows, has_s0=has_s0),
        grid=grid,
        in_specs=[blk(0), blk(1), blk(2), blk(3), vec, vec] + ([sblk] if has_s0 else []),
        out_specs=[blk(0), sblk],
        out_shape=[jax.ShapeDtypeStruct(o_shape, F32),
                   jax.ShapeDtypeStruct((batch, H_A, DK, DV), F32)],
        scratch_shapes=scratch,
        compiler_params=_cparams(("arbitrary", "arbitrary")),
        name="hgrn_state" if has_s0 else "hgrn_scan",
    )(*args)


def _pool_groups(full_ref, base, tm, cnt_fn, wp_ref, ps_ref, step=1):
    outs = []
    for gi, w in enumerate(POOL_WINDOWS):
        ch = slice(POOL_GROUP * gi, POOL_GROUP * (gi + 1))
        v = full_ref[base:base + tm, ch]
        s = v
        for j in range(1, w):
            s = s + full_ref[base - j * step:base - j * step + tm, ch]
        d = s / cnt_fn(w) - v
        outs.append(jnp.dot(d.astype(BF16), wp_ref[gi], preferred_element_type=F32))
    return jnp.concatenate(outs, axis=1) * ps_ref[...]


def _pool_prompt_kernel(v_ref, prev_ref, wp_ref, ps_ref, o_ref, hist_ref, full_ref, *, tm):
    t = pl.program_id(1)
    prev = prev_ref[...]
    full_ref[0:16, :] = jnp.where(t == 0, 0.0, prev)
    full_ref[16:16 + tm, :] = v_ref[...]
    pos = (lax.broadcasted_iota(I32, (tm, 1), 0) + t * tm + 1).astype(F32)
    o_ref[...] = _pool_groups(full_ref, 16, tm, lambda w: jnp.minimum(pos, float(w)),
                              wp_ref, ps_ref)

    @pl.when(t == pl.num_programs(1) - 1)
    def _():
        hist_ref[...] = full_ref[16 + tm - POOL_HIST:16 + tm, :]


def _pool_prompt(u, batch, seq, w_pool_bf, pool_scale):
    tm = min(1024, seq)
    nt = seq // tm
    vcol = (4 * D_A) // D_B
    return pl.pallas_call(
        functools.partial(_pool_prompt_kernel, tm=tm),
        grid=(batch, nt),
        in_specs=[pl.BlockSpec((tm, D_B), lambda b, t: (b * nt + t, vcol)),
                  pl.BlockSpec((16, D_B),
                               lambda b, t: (jnp.maximum((b * nt + t) * (tm // 16) - 1, 0), vcol)),
                  pl.BlockSpec((len(POOL_WINDOWS), POOL_GROUP, POOL_GROUP), lambda b, t: (0, 0, 0)),
                  pl.BlockSpec((1, D_B), lambda b, t: (0, 0))],
        out_specs=[pl.BlockSpec((tm, D_B), lambda b, t: (b * nt + t, 0)),
                   pl.BlockSpec((None, POOL_HIST, D_B), lambda b, t: (b, 0, 0))],
        out_shape=[jax.ShapeDtypeStruct((batch * seq, D_B), F32),
                   jax.ShapeDtypeStruct((batch, POOL_HIST, D_B), F32)],
        scratch_shapes=[pltpu.VMEM((16 + tm, D_B), F32)],
        compiler_params=_cparams(("arbitrary", "arbitrary")),
        name="pool_prompt",
    )(u, u, w_pool_bf, pool_scale)


def _pool_sample_kernel(v_ref, hist_ref, wp_ref, ps_ref, o_ref, nh_ref, full_ref, *,
                        seq, batch, start):
    full_ref[0:batch, :] = jnp.zeros((batch, D_B), F32)
    full_ref[batch:16 * batch, :] = hist_ref[...].reshape(POOL_HIST * batch, D_B)
    full_ref[16 * batch:(16 + seq) * batch, :] = v_ref[...].reshape(seq * batch, D_B)

    def cnt(w):
        rows = [jnp.full((batch, 1), float(min(start + t + 1, w)), F32) for t in range(seq)]
        return jnp.concatenate(rows, axis=0)

    o = _pool_groups(full_ref, 16 * batch, seq * batch, cnt, wp_ref, ps_ref, step=batch)
    o_ref[...] = o.reshape(seq, batch, D_B)
    nh_ref[...] = full_ref[(16 + seq - POOL_HIST) * batch:(16 + seq) * batch, :].reshape(
        POOL_HIST, batch, D_B)


def _pool_sample(u_t, hist_t, start, w_pool_bf, pool_scale):
    seq, batch, _ = u_t.shape
    vcol = (4 * D_A) // D_B
    return pl.pallas_call(
        functools.partial(_pool_sample_kernel, seq=seq, batch=batch, start=start),
        grid=(1,),
        in_specs=[pl.BlockSpec((seq, batch, D_B), lambda i: (0, 0, vcol)),
                  pl.BlockSpec((POOL_HIST, batch, D_B), lambda i: (0, 0, 0)),
                  pl.BlockSpec((len(POOL_WINDOWS), POOL_GROUP, POOL_GROUP), lambda i: (0, 0, 0)),
                  pl.BlockSpec((1, D_B), lambda i: (0, 0))],
        out_specs=[pl.BlockSpec((seq, batch, D_B), lambda i: (0, 0, 0)),
                   pl.BlockSpec((POOL_HIST, batch, D_B), lambda i: (0, 0, 0))],
        out_shape=[jax.ShapeDtypeStruct((seq, batch, D_B), F32),
                   jax.ShapeDtypeStruct((POOL_HIST, batch, D_B), F32)],
        scratch_shapes=[pltpu.VMEM(((16 + seq) * batch, D_B), F32)],
        compiler_params=_cparams(("arbitrary",)),
        name="pool_sample",
    )(u_t, hist_t, w_pool_bf, pool_scale)


def _store_slabs(ref, value):
    n = value.shape[0]
    for j in range(SLAB):
        ref[pl.ds(j, n, stride=SLAB), :] = value[:, LANES * j:LANES * (j + 1)]


def _load_slabs(ref, n):
    return [ref[pl.ds(j, n, stride=SLAB), :] for j in range(SLAB)]


def _slab_rows(row):
    return pl.ds(pl.multiple_of(row * SLAB, SLAB), SLAB)


def _outproj_kernel(x_ref, oa_ref, ob_ref, g1_ref, sh2_ref, sc2_ref, n2_ref, woa_ref, wob_ref,
                    wr_ref, br_ref, *rest):
    x1_ref, h2_ref, idx_ref, prob_ref, cnt_ref, lg_ref = rest[-6:]
    i = pl.program_id(0)
    slot = i % 2

    @pl.when(i == 0)
    def _():
        lg_ref[...] = jnp.zeros_like(lg_ref)

    logits = lg_ref[1 - slot]

    mix = (jnp.dot(oa_ref[...].astype(BF16), woa_ref[...], preferred_element_type=F32)
           + jnp.dot(ob_ref[...].astype(BF16), wob_ref[...], preferred_element_type=F32))
    x1 = x_ref[...] + g1_ref[...] * mix
    x1_ref[...] = x1
    h2 = x1 * lax.rsqrt(jnp.mean(x1 * x1, axis=-1, keepdims=True) + EPS) * n2_ref[...]
    h2 = h2 * (1.0 + sc2_ref[...]) + sh2_ref[...]
    _store_slabs(h2_ref, h2)
    lg_ref[slot] = jnp.dot(h2.astype(BF16), wr_ref[...], preferred_element_type=F32) + br_ref[...]

    tm = logits.shape[0]
    lane = lax.broadcasted_iota(I32, (tm, N_EXPERTS), 1)
    kcol = lax.broadcasted_iota(I32, (tm, TOP_K), 1)
    vals = jnp.zeros((tm, TOP_K), F32)
    idxs = jnp.zeros((tm, TOP_K), I32)
    for k in range(TOP_K):
        m = jnp.max(logits, axis=-1, keepdims=True)
        am = jnp.min(jnp.where(logits == m, lane, N_EXPERTS), axis=-1, keepdims=True)
        vals = jnp.where(kcol == k, m, vals)
        idxs = jnp.where(kcol == k, am, idxs)
        logits = jnp.where(lane == am, -jnp.inf, logits)
    e = jnp.exp(vals - vals[:, 0:1])
    prob_ref[...] = e / jnp.sum(e, axis=-1, keepdims=True)
    idx_ref[...] = idxs
    member = jnp.zeros((tm, N_EXPERTS), F32)
    for k in range(TOP_K):
        member = member + (idxs[:, k:k + 1] == lane).astype(F32)
    cnt_ref[...] = jnp.sum(member, axis=0, keepdims=True)


def _outproj(x, o_a, o_b, mod, per_token, seq, norm2, w_out_bf, w_router_bf, b_router,
             n_total, tile0, joint=None):
    n = x.shape[0]
    tm = min(256, n)
    tps = max(seq // tm, 1)
    nt = n // tm
    cur = lambda i: jnp.minimum(i, nt - 1)
    prv = lambda i: jnp.maximum(i - 1, 0)
    row = lambda width: pl.BlockSpec((tm, width), lambda i: (cur(i), 0))
    jrow = lambda width: pl.BlockSpec((tm, width), lambda i: (prv(i) + tile0, 0))
    const = lambda shape: pl.BlockSpec(shape, lambda i: (0,) * len(shape))
    joint = () if joint is None else tuple(joint)
    n_in = 11
    return pl.pallas_call(
        _outproj_kernel,
        grid=(nt + 1,),
        in_specs=[row(D_MODEL), row(D_A), row(D_B),
                  _mod_spec(per_token, tm, tps, 2, nt - 1),
                  _mod_spec(per_token, tm, tps, 3, nt - 1),
                  _mod_spec(per_token, tm, tps, 4, nt - 1),
                  const((1, D_MODEL)),
                  pl.BlockSpec((D_A, D_MODEL), lambda i: (0, 0)),
                  pl.BlockSpec((D_B, D_MODEL), lambda i: (1, 0)),
                  const((D_MODEL, N_EXPERTS)), const((1, N_EXPERTS))]
                 + [pl.BlockSpec(memory_space=pl.ANY)] * len(joint),
        out_specs=[row(D_MODEL), pl.BlockSpec((tm * SLAB, LANES), lambda i: (cur(i) + tile0, 0)),
                   jrow(TOP_K), jrow(TOP_K),
                   pl.BlockSpec((None, 1, N_EXPERTS), lambda i: (prv(i) + tile0, 0, 0))],
        out_shape=[jax.ShapeDtypeStruct((n, D_MODEL), F32),
                   jax.ShapeDtypeStruct((n_total * SLAB, LANES), F32),
                   jax.ShapeDtypeStruct((n_total, TOP_K), I32),
                   jax.ShapeDtypeStruct((n_total, TOP_K), F32),
                   jax.ShapeDtypeStruct((n_total // tm, 1, N_EXPERTS), F32)],
        input_output_aliases={n_in + j: 1 + j for j in range(len(joint))},
        scratch_shapes=[pltpu.VMEM((2, tm, N_EXPERTS), F32)],
        compiler_params=_cparams(("arbitrary",)),
        name="outproj",
    )(x, o_a, o_b, mod, mod, mod, norm2.reshape(1, D_MODEL), w_out_bf, w_out_bf,
      w_router_bf, b_router.reshape(1, N_EXPERTS), *joint)


def _rank_kernel(idx_ref, start_ref, dest_ref, carry_ref, *, te):
    i = pl.program_id(0)
    idx = idx_ref[...]
    lane = lax.broadcasted_iota(I32, (te, N_EXPERTS), 1)
    onehots = [(idx[:, k:k + 1] == lane) for k in range(TOP_K)]
    member = jnp.zeros((te, N_EXPERTS), F32)
    for oh in onehots:
        member = member + oh.astype(F32)

    @pl.when(i == 0)
    def _():
        carry_ref[...] = start_ref[...]

    r = lax.broadcasted_iota(I32, (te, te), 0)
    c = lax.broadcasted_iota(I32, (te, te), 1)
    earlier = (r > c).astype(BF16)
    base = carry_ref[...] + jnp.dot(earlier, member.astype(BF16), preferred_element_type=F32)
    kcol = lax.broadcasted_iota(I32, (te, TOP_K), 1)
    dest = jnp.zeros((te, TOP_K), F32)
    for k, oh in enumerate(onehots):
        dk = jnp.sum(jnp.where(oh, base, 0.0), axis=-1, keepdims=True)
        dest = jnp.where(kcol == k, dk, dest)
    dest_ref[...] = dest.astype(I32)
    carry_ref[...] += jnp.sum(member, axis=0, keepdims=True)


def _rank(top_idx, starts):
    n = top_idx.shape[0]
    te = 768 if n % 768 == 0 else 512
    return pl.pallas_call(
        functools.partial(_rank_kernel, te=te),
        grid=(n // te,),
        in_specs=[pl.BlockSpec((te, TOP_K), lambda i: (i, 0)),
                  pl.BlockSpec((1, N_EXPERTS), lambda i: (0, 0))],
        out_specs=pl.BlockSpec((te, TOP_K), lambda i: (i, 0)),
        out_shape=jax.ShapeDtypeStruct((n, TOP_K), I32),
        scratch_shapes=[pltpu.VMEM((1, N_EXPERTS), F32)],
        compiler_params=_cparams(("arbitrary",)),
        name="rank",
    )(top_idx, starts)


def _dispatch_kernel(last_ref, ntile_ref, used_ref, dest_ref, h_ref, xs_ref, zero_ref, sem, zsem,
                     *, n_tiles):
    i = pl.program_id(0)

    @pl.when(i == 0)
    def _():
        zero_ref[...] = jnp.zeros_like(zero_ref)

        def clear(tile):
            rows = pl.ds(pl.multiple_of(tile * (ROW_TILE * SLAB), ROW_TILE * SLAB), ROW_TILE * SLAB)
            return pltpu.make_async_copy(zero_ref, xs_ref.at[rows], zsem)

        def over_tail(fn):
            def body(t, carry):
                fn(clear(t))
                return carry
            lax.fori_loop(used_ref[0], n_tiles, body, 0)

        for fn in (lambda c: c.start(), lambda c: c.wait()):
            for e in range(N_EXPERTS):
                @pl.when(ntile_ref[e] > 0)
                def _():
                    fn(clear(last_ref[e]))
            over_tail(fn)

    def body(r, carry):
        for k in range(TOP_K):
            d = dest_ref[r * TOP_K + k]
            pltpu.make_async_copy(h_ref.at[_slab_rows(r)], xs_ref.at[_slab_rows(d)], sem).start(
                priority=k % 2)
        return carry

    lax.fori_loop(0, TOK_TILE, body, 0, unroll=2)
    for k in range(TOP_K):
        pltpu.make_async_copy(h_ref, xs_ref.at[pl.ds(0, TOK_TILE * SLAB)], sem).wait()


def _dispatch(h2, dest_flat, last_tile, ntiles, used, n_tiles):
    n = h2.shape[0] // SLAB
    grid_spec = pltpu.PrefetchScalarGridSpec(
        num_scalar_prefetch=3,
        grid=(n // TOK_TILE,),
        in_specs=[pl.BlockSpec((TOK_TILE * TOP_K,), lambda i, *_: (i,), memory_space=pltpu.SMEM),
                  pl.BlockSpec((TOK_TILE * SLAB, LANES), lambda i, *_: (i, 0))],
        out_specs=pl.BlockSpec(memory_space=pl.ANY),
        scratch_shapes=[pltpu.VMEM((ROW_TILE * SLAB, LANES), F32),
                        pltpu.SemaphoreType.DMA(()), pltpu.SemaphoreType.DMA(())],
    )
    return pl.pallas_call(
        functools.partial(_dispatch_kernel, n_tiles=n_tiles),
        grid_spec=grid_spec,
        out_shape=jax.ShapeDtypeStruct((n_tiles * ROW_TILE * SLAB, LANES), F32),
        compiler_params=_cparams(("arbitrary",)),
        name="dispatch",
    )(last_tile, ntiles, used, dest_flat, h2)


def _experts_kernel(te_ref, used_ref, grp_ref, nxt_ref, xs_ref, wgu_ref, bgu_ref, wd_ref, bd_ref,
                    ys_ref, wgu_f, wd_f, wgu_bf, wd_bf, act_bf, xring, yring, wsem, xsem, ysem,
                    *, n_tiles):
    used = used_ref[0]
    tile_rows = ROW_TILE * SLAB

    def weights(e, ws):
        return (pltpu.make_async_copy(wgu_ref.at[e], wgu_f.at[ws], wsem.at[ws, 0]),
                pltpu.make_async_copy(wd_ref.at[e], wd_f.at[ws], wsem.at[ws, 1]))

    def tile_rows_of(t):
        return pl.ds(pl.multiple_of(t * tile_rows, tile_rows), tile_rows)

    def x_tile(t):
        t = jnp.asarray(t, I32)
        return pltpu.make_async_copy(xs_ref.at[tile_rows_of(t)], xring.at[t % X_RING],
                                     xsem.at[t % X_RING])

    def y_tile(t):
        return pltpu.make_async_copy(yring.at[t % 2], ys_ref.at[tile_rows_of(t)], ysem.at[t % 2])

    for cp in weights(te_ref[0], 0):
        cp.start()
    for t in range(X_RING - 1):
        @pl.when(t < used)
        def _():
            x_tile(t).start()

    def step(i, carry):
        @pl.when(i + X_RING - 1 < used)
        def _():
            x_tile(i + X_RING - 1).start()

        e = te_ref[i]
        ws = grp_ref[i] % 2
        fresh = (i == 0) | (e != te_ref[jnp.maximum(i - 1, 0)])

        @pl.when(fresh)
        def _():
            for cp in weights(e, ws):
                cp.wait()

            @pl.when(nxt_ref[i] >= 0)
            def _():
                for cp in weights(nxt_ref[i], 1 - ws):
                    cp.start()

            wgu_bf[...] = wgu_f[ws].astype(BF16)
            wd_bf[...] = wd_f[ws].astype(BF16)

        x_tile(i).wait()

        @pl.when(i >= 2)
        def _():
            y_tile(i - 2).wait()

        x = jnp.concatenate([p.astype(BF16) for p in _load_slabs(xring.at[i % X_RING], ROW_TILE)],
                            axis=1)
        bgu = bgu_ref[pl.ds(e, 1), :]
        bd = bd_ref[pl.ds(e, 1), :]
        cw = D_FF // FF_CHUNKS
        for c in range(FF_CHUNKS):
            gate = jnp.dot(x, wgu_bf[:, c * cw:(c + 1) * cw], preferred_element_type=F32)
            gate = jnp.minimum(gate + bgu[:, c * cw:(c + 1) * cw], SWIGLU_LIMIT)
            up = jnp.dot(x, wgu_bf[:, D_FF + c * cw:D_FF + (c + 1) * cw], preferred_element_type=F32)
            up = jnp.clip(up + bgu[:, D_FF + c * cw:D_FF + (c + 1) * cw], -SWIGLU_LIMIT, SWIGLU_LIMIT)
            act = gate * jax.nn.sigmoid(SWIGLU_ALPHA * gate) * (up + 1.0)
            act_bf[:, c * cw:(c + 1) * cw] = act.astype(BF16)
        ow = D_MODEL // FF_CHUNKS
        for c in range(FF_CHUNKS):
            out = jnp.dot(act_bf[...], wd_bf[:, c * ow:(c + 1) * ow], preferred_element_type=F32)
            out = out + bd[:, c * ow:(c + 1) * ow]
            for j in range(ow // LANES):
                yring[i % 2, pl.ds(c * (ow // LANES) + j, ROW_TILE, stride=SLAB), :] = (
                    out[:, LANES * j:LANES * (j + 1)])
        y_tile(i).start()
        return carry

    lax.fori_loop(0, used, step, 0)

    @pl.when(used >= 2)
    def _():
        y_tile(used - 2).wait()
    y_tile(used - 1).wait()

    yring[0] = jnp.zeros(yring.shape[1:], F32)

    def clear(fn):
        def body(t, carry):
            fn(pltpu.make_async_copy(yring.at[0], ys_ref.at[tile_rows_of(t)], ysem.at[0]))
            return carry
        lax.fori_loop(used, n_tiles, body, 0)

    clear(lambda cp: cp.start())
    clear(lambda cp: cp.wait())


def _experts(xs, tile_expert, used, grp, nxt, w_gu, b_gu, w_down, b_down):
    rows = xs.shape[0] // SLAB
    nt = rows // ROW_TILE
    whole = lambda shape: pl.BlockSpec(shape, lambda i, *_: (0,) * len(shape))
    grid_spec = pltpu.PrefetchScalarGridSpec(
        num_scalar_prefetch=4,
        grid=(1,),
        in_specs=[pl.BlockSpec(memory_space=pl.ANY),
                  pl.BlockSpec(memory_space=pl.ANY),
                  whole((N_EXPERTS, 2 * D_FF)),
                  pl.BlockSpec(memory_space=pl.ANY),
                  whole((N_EXPERTS, D_MODEL))],
        out_specs=pl.BlockSpec(memory_space=pl.ANY),
        scratch_shapes=[pltpu.VMEM((2, D_MODEL, 2 * D_FF), F32), pltpu.VMEM((2, D_FF, D_MODEL), F32),
                        pltpu.VMEM((D_MODEL, 2 * D_FF), BF16), pltpu.VMEM((D_FF, D_MODEL), BF16),
                        pltpu.VMEM((ROW_TILE, D_FF), BF16),
                        pltpu.VMEM((X_RING, ROW_TILE * SLAB, LANES), F32),
                        pltpu.VMEM((2, ROW_TILE * SLAB, LANES), F32),
                        pltpu.SemaphoreType.DMA((2, 2)), pltpu.SemaphoreType.DMA((X_RING,)),
                        pltpu.SemaphoreType.DMA((2,))],
    )
    return pl.pallas_call(
        functools.partial(_experts_kernel, n_tiles=nt),
        grid_spec=grid_spec,
        out_shape=jax.ShapeDtypeStruct((rows * SLAB, LANES), F32),
        compiler_params=_cparams(("arbitrary",)),
        name="experts",
    )(tile_expert, used, grp, nxt, xs, w_gu, b_gu, w_down, b_down)


def _combine_kernel(dest_ref, next_ref, ys_ref, x1_ref, p_ref, g2_ref, fn_ref, y_ref, buf_ref, sem,
                    *, last):
    i = pl.program_id(0)
    slot = i % 2

    def issue(d_ref, sl):
        def body(r, carry):
            for k in range(TOP_K):
                d = d_ref[r * TOP_K + k]
                pltpu.make_async_copy(ys_ref.at[_slab_rows(d)], buf_ref.at[sl, k, _slab_rows(r)],
                                      sem.at[sl]).start(priority=k % 2)
            return carry
        lax.fori_loop(0, TOK_TILE, body, 0, unroll=2)

    @pl.when(i == 0)
    def _():
        issue(dest_ref, 0)

    @pl.when(i + 1 < pl.num_programs(0))
    def _():
        issue(next_ref, 1 - slot)

    for k in range(TOP_K):
        pltpu.make_async_copy(ys_ref.at[pl.ds(0, TOK_TILE * SLAB)], buf_ref.at[slot, k],
                              sem.at[slot]).wait()

    p = p_ref[...]
    groups = None
    for k in range(TOP_K):
        part = [p[:, k:k + 1] * g for g in _load_slabs(buf_ref.at[slot, k], TOK_TILE)]
        groups = part if groups is None else [a + b for a, b in zip(groups, part)]
    moe = jnp.concatenate(groups, axis=1)
    x2 = x1_ref[...] + g2_ref[...] * moe
    if last:
        x2 = x2 * lax.rsqrt(jnp.mean(x2 * x2, axis=-1, keepdims=True) + EPS) * fn_ref[...]
    y_ref[...] = x2


def _combine(ys, dest_flat, x1, probs, mod, per_token, seq, tile0, final_norm, last):
    n = x1.shape[0]
    tm = TOK_TILE
    tps = max(seq // tm, 1)
    nt = n // tm
    return pl.pallas_call(
        functools.partial(_combine_kernel, last=last),
        grid=(nt,),
        in_specs=[pl.BlockSpec((tm * TOP_K,), lambda i: (i + tile0,), memory_space=pltpu.SMEM),
                  pl.BlockSpec((tm * TOP_K,), lambda i: (jnp.minimum(i + 1, nt - 1) + tile0,),
                               memory_space=pltpu.SMEM),
                  pl.BlockSpec(memory_space=pl.ANY),
                  pl.BlockSpec((tm, D_MODEL), lambda i: (i, 0)),
                  pl.BlockSpec((tm, TOP_K), lambda i: (i + tile0, 0)),
                  _mod_spec(per_token, tm, tps, 5),
                  pl.BlockSpec((1, D_MODEL), lambda i: (0, 0))],
        out_specs=pl.BlockSpec((tm, D_MODEL), lambda i: (i, 0)),
        out_shape=jax.ShapeDtypeStruct((n, D_MODEL), F32),
        scratch_shapes=[pltpu.VMEM((2, TOP_K, tm * SLAB, LANES), F32),
                        pltpu.SemaphoreType.DMA((2,))],
        compiler_params=_cparams(("arbitrary",)),
        name="combine",
    )(dest_flat, dest_flat, ys, x1, probs, mod, final_norm.reshape(1, D_MODEL))


def kernel(x_prompt, x_sample, state_hgrn, state_pool, c_prompt, c_sample, w_ada, b_ada, norm1,
           norm2, w_in, lower_bounds, hgrn_norm, w_pool, pool_scale, w_out, w_router, b_router,
           w_gu, b_gu, w_down, b_down, final_norm):
    bp, seq_p, _ = x_prompt.shape
    bs, seq_s, _ = x_sample.shape
    np_, ns = bp * seq_p, bs * seq_s
    depth = w_ada.shape[0]

    lbs = jnp.cumsum(jax.nn.softmax(lower_bounds.astype(F32), axis=0), axis=0)
    xp = x_prompt.reshape(np_, D_MODEL)
    xs_ = x_sample.reshape(ns, D_MODEL)
    c_all = jnp.concatenate([jnp.repeat(c_sample, seq_s, axis=0), c_prompt], axis=0)

    sp_l, hp_l, ss_l, hs_l = [], [], [], []
    for l in range(depth):
        w_in_bf = w_in[l].astype(BF16)
        w_out_bf = w_out[l].astype(BF16)
        w_pool_bf = w_pool[l].astype(BF16)
        w_router_bf = w_router[l].astype(BF16)
        lb = lbs[l].reshape(1, D_A)
        hn = hgrn_norm[l].reshape(1, D_A)
        psc = pool_scale[l].reshape(1, D_B)

        mod = _ada(c_all, w_ada[l], b_ada[l])
        mod_p = mod[ns:].reshape(bp, 1, 6 * D_MODEL)
        mod_s = mod

        up = _inproj(xp, mod_p, False, seq_p, norm1[l], w_in_bf)
        us = _inproj(xs_, mod_s, True, seq_s, norm1[l], w_in_bf)

        oa_p, s_p = _hgrn(up.reshape(bp, seq_p, D_IN), None, lb, hn)
        oa_p = oa_p.reshape(np_, D_A)
        oa_s, s_s = _hgrn(us, state_hgrn[l], lb, hn, seq=seq_s)
        ob_p, h_p = _pool_prompt(up, bp, seq_p, w_pool_bf, psc)
        ob_t, h_t = _pool_sample(us.reshape(bs, seq_s, D_IN).transpose(1, 0, 2),
                                 state_pool[l].transpose(1, 0, 2), PAST_LEN, w_pool_bf, psc)
        ob_s, h_s = ob_t.transpose(1, 0, 2).reshape(ns, D_B), h_t.transpose(1, 0, 2)

        n_tok = np_ + ns
        n_tiles = (n_tok * TOP_K) // ROW_TILE + N_EXPERTS
        x1p, *joint = _outproj(xp, oa_p, ob_p, mod_p, False, seq_p, norm2[l], w_out_bf,
                               w_router_bf, b_router[l], n_tok, 0)
        x1s, h2, top_idx, probs, tile_counts = _outproj(
            xs_, oa_s, ob_s, mod_s, True, seq_s, norm2[l], w_out_bf, w_router_bf, b_router[l],
            n_tok, np_ // TOK_TILE, joint)

        counts = jnp.sum(tile_counts, axis=(0, 1)).astype(I32)
        tiles_e = (counts + ROW_TILE - 1) // ROW_TILE
        ends = jnp.cumsum(tiles_e)
        starts = ((ends - tiles_e) * ROW_TILE).astype(F32).reshape(1, N_EXPERTS)
        dest = _rank(top_idx, starts)
        dest_flat = dest.reshape(n_tok * TOP_K)
        used = ends[-1:].astype(I32)
        tile_expert = jnp.minimum(
            jnp.sum(jnp.arange(n_tiles, dtype=I32)[:, None] >= ends[None, :], axis=1),
            N_EXPERTS - 1).astype(I32)
        last_tile = (ends - 1).astype(I32)
        grp = jnp.cumsum(jnp.concatenate(
            [jnp.zeros((1,), I32), (tile_expert[1:] != tile_expert[:-1]).astype(I32)])).astype(I32)
        eids = jnp.arange(N_EXPERTS, dtype=I32)
        later = (eids[None, :] > eids[:, None]) & (tiles_e > 0)[None, :]
        nxt_e = jnp.min(jnp.where(later, eids[None, :], N_EXPERTS), axis=1)
        nxt = jnp.where(nxt_e < N_EXPERTS, nxt_e, -1)[tile_expert].astype(I32)

        xsort = _dispatch(h2, dest_flat, last_tile, tiles_e.astype(I32), used, n_tiles)
        ysort = _experts(xsort, tile_expert, used, grp, nxt, w_gu[l], b_gu[l], w_down[l], b_down[l])

        last = l == depth - 1
        xp = _combine(ysort, dest_flat, x1p, probs, mod_p, False, seq_p, 0, final_norm, last)
        xs_ = _combine(ysort, dest_flat, x1s, probs, mod_s, True, seq_s, np_ // TOK_TILE,
                       final_norm, last)

        sp_l.append(s_p)
        hp_l.append(h_p)
        ss_l.append(s_s)
        hs_l.append(h_s)

    return (xp.reshape(bp, seq_p, D_MODEL), xs_.reshape(bs, seq_s, D_MODEL),
            jnp.stack(sp_l), jnp.stack(hp_l), jnp.stack(ss_l), jnp.stack(hs_l))
```

```python
import functools

import jax
import jax.numpy as jnp
from jax import lax
from jax.experimental import pallas as pl
from jax.experimental.pallas import tpu as pltpu

F32, BF16, I32 = jnp.float32, jnp.bfloat16, jnp.int32

D_MODEL = 1024
D_A = 512
H_A = 4
DK = 128
DV = 128
D_B = 512
POOL_WINDOWS = (2, 4, 8, 16)
POOL_GROUP = 128
POOL_HIST = 15
D_IN = 4 * D_A + D_B
N_EXPERTS = 32
TOP_K = 4
D_FF = 1024
SWIGLU_LIMIT = 7.0
SWIGLU_ALPHA = 1.702
EPS = 1e-6
PAST_LEN = 16384

SUB = 16
HALF = SUB // 2
HGRN_SEQS = 8
ROW_TILE = 256
TOK_TILE = 256
OUT_TILE = 512
FF_CHUNKS = 4
X_RING = 3
LANES = 128
SLAB = D_MODEL // LANES
VMEM_LIMIT = 56 * 1024 * 1024


def _cparams(sem):
    return pltpu.CompilerParams(dimension_semantics=sem, vmem_limit_bytes=VMEM_LIMIT)


def _silu(x):
    return x * jax.nn.sigmoid(x)


def _mod_spec(per_token, tm, tiles_per_seq, col, last_tile=None):
    tile = (lambda i: i) if last_tile is None else (lambda i: jnp.minimum(i, last_tile))
    if per_token:
        return pl.BlockSpec((tm, D_MODEL), lambda i: (tile(i), col))
    return pl.BlockSpec((None, 1, D_MODEL), lambda i: (tile(i) // tiles_per_seq, 0, col))


def _ada_kernel(c_ref, w_ref, b_ref, o_ref):
    c = c_ref[...]
    o_ref[...] = jnp.dot(_silu(c).astype(BF16), w_ref[...].astype(BF16),
                         preferred_element_type=F32) + b_ref[...]


def _ada(c_all, w_ada, b_ada):
    rows = c_all.shape[0]
    n = w_ada.shape[1]
    return pl.pallas_call(
        _ada_kernel,
        grid=(n // D_MODEL,),
        in_specs=[pl.BlockSpec((rows, D_MODEL), lambda j: (0, 0)),
                  pl.BlockSpec((D_MODEL, D_MODEL), lambda j: (0, j)),
                  pl.BlockSpec((1, D_MODEL), lambda j: (0, j))],
        out_specs=pl.BlockSpec((rows, D_MODEL), lambda j: (0, j)),
        out_shape=jax.ShapeDtypeStruct((rows, n), F32),
        compiler_params=_cparams(("arbitrary",)),
        name="ada",
    )(c_all, w_ada, b_ada.reshape(1, n))


def _inproj_kernel(x_ref, sh_ref, sc_ref, n1_ref, w_ref, u_ref):
    x = x_ref[...]
    h = x * lax.rsqrt(jnp.mean(x * x, axis=-1, keepdims=True) + EPS) * n1_ref[...]
    h = h * (1.0 + sc_ref[...]) + sh_ref[...]
    u_ref[...] = jnp.dot(h.astype(BF16), w_ref[...], preferred_element_type=F32)


def _inproj(x, mod, per_token, seq, norm1, w_in_bf):
    n = x.shape[0]
    tm = min(1024, n)
    tps = max(seq // tm, 1)
    return pl.pallas_call(
        _inproj_kernel,
        grid=(n // tm,),
        in_specs=[pl.BlockSpec((tm, D_MODEL), lambda i: (i, 0)),
                  _mod_spec(per_token, tm, tps, 0),
                  _mod_spec(per_token, tm, tps, 1),
                  pl.BlockSpec((1, D_MODEL), lambda i: (0, 0)),
                  pl.BlockSpec((D_MODEL, D_IN), lambda i: (0, 0))],
        out_specs=pl.BlockSpec((tm, D_IN), lambda i: (i, 0)),
        out_shape=jax.ShapeDtypeStruct((n, D_IN), F32),
        compiler_params=_cparams(("arbitrary",)),
        name="inproj",
    )(x, mod, mod, norm1.reshape(1, D_MODEL), w_in_bf)


def _block_select():
    r = lax.broadcasted_iota(I32, (SUB, SUB * DK), 0)
    c = lax.broadcasted_iota(I32, (SUB, SUB * DK), 1) >> (DK.bit_length() - 1)
    return (r == c).astype(BF16)


def _hgrn_step(q, fp, iv, g, lb, hn, st_ref, sel_ref, cm_ref, b_ref, k_ref, valid):
    nrow = HGRN_SEQS * SUB
    f = lb + (1.0 - lb) * jax.nn.sigmoid(fp)
    lf = jnp.log2(f)
    kk = 1.0 - f
    if valid < SUB:
        live = (lax.broadcasted_iota(I32, (nrow, D_A), 0) & (SUB - 1)) < valid
        lf = jnp.where(live, lf, 0.0)
        kk = jnp.where(live, kk, 0.0)
    qa = _silu(q)

    hi = lf.astype(BF16)
    lo = (lf - hi.astype(F32)).astype(BF16)
    cum, tot = cm_ref[0], cm_ref[1]
    b = jnp.dot(cum, hi, preferred_element_type=F32) + jnp.dot(cum, lo, preferred_element_type=F32)
    be = jnp.dot(tot, hi, preferred_element_type=F32) + jnp.dot(tot, lo, preferred_element_type=F32)
    qt = (qa * jnp.exp2(b)).astype(BF16)
    kh = (kk * jnp.exp2(be - b)).astype(BF16)
    dec = jnp.exp2(be)
    ivb = iv.astype(BF16)
    for h in range(H_A):
        b_ref[h] = b[:, DK * h:DK * (h + 1)]
        k_ref[h] = kk[:, DK * h:DK * (h + 1)]

    zero_half = jnp.zeros((HALF, D_A), F32)
    tr = lax.broadcasted_iota(I32, (H_A * SUB, SUB), 0) & (SUB - 1)
    tc = lax.broadcasted_iota(I32, (H_A * SUB, SUB), 1)
    sg = _silu(g)

    own = ((lax.broadcasted_iota(I32, (nrow, HGRN_SEQS * DK), 0) >> (SUB.bit_length() - 1))
           == (lax.broadcasted_iota(I32, (nrow, HGRN_SEQS * DK), 1) >> (DK.bit_length() - 1)))
    inter = [None] * (HGRN_SEQS * H_A)
    for h in range(H_A):
        sl = slice(DK * h, DK * (h + 1))
        kbd = jnp.where(own, jnp.concatenate([kh[:, sl]] * HGRN_SEQS, axis=1), jnp.zeros((), BF16))
        ut_all = lax.dot_general(ivb[:, sl], kbd, (((0,), (0,)), ((), ())),
                                 preferred_element_type=F32)
        for n in range(HGRN_SEQS):
            rows = slice(n * SUB, (n + 1) * SUB)
            st = st_ref[n, h]
            inter[n * H_A + h] = lax.dot_general(qt[rows, sl], st.astype(BF16),
                                                 (((1,), (1,)), ((), ())),
                                                 preferred_element_type=F32)
            dtile = dec[n * SUB:n * SUB + HALF, sl]
            st_ref[n, h] = ((st.reshape(DV // HALF, HALF, DK) * dtile[None]).reshape(DV, DK)
                            + ut_all[:, DK * n:DK * (n + 1)])

    scores = []
    for n in range(HGRN_SEQS):
        per_s = []
        for s in range(SUB):
            if s >= valid:
                per_s.append(jnp.zeros((SUB, D_A), BF16))
                continue
            row = n * SUB + s
            b_s = jnp.concatenate([b_ref[h, pl.ds(row, HALF, stride=0), :] for h in range(H_A)], axis=1)
            k_s = jnp.concatenate([k_ref[h, pl.ds(row, HALF, stride=0), :] for h in range(H_A)], axis=1)
            halves = []
            for hf in range(2):
                if (hf == 0 and s >= HALF) or hf * HALF >= valid:
                    halves.append(zero_half)
                    continue
                r0 = n * SUB + hf * HALF
                e = jnp.exp2(jnp.minimum(b[r0:r0 + HALF] - b_s, 0.0))
                halves.append(e * (qa[r0:r0 + HALF] * k_s))
            per_s.append(jnp.concatenate(halves, axis=0).astype(BF16))
        pcat = jnp.concatenate(
            [jnp.concatenate([p[:, DK * h:DK * (h + 1)] for p in per_s], axis=1) for h in range(H_A)],
            axis=0)
        a = lax.dot_general(pcat, sel_ref[...], (((1,), (1,)), ((), ())),
                            preferred_element_type=F32)
        scores.append(jnp.where(tr >= tc, a, 0.0).astype(BF16))

    outs = []
    for n in range(HGRN_SEQS):
        rows = slice(n * SUB, (n + 1) * SUB)
        heads = []
        for h in range(H_A):
            sl = slice(DK * h, DK * (h + 1))
            o = inter[n * H_A + h] + jnp.dot(scores[n][h * SUB:(h + 1) * SUB, :], ivb[rows, sl],
                                             preferred_element_type=F32)
            o = o * lax.rsqrt(jnp.mean(o * o, axis=-1, keepdims=True) + EPS) * hn[:, sl]
            heads.append(o)
        outs.append(jnp.concatenate(heads, axis=1) * sg[rows])
    return jnp.concatenate(outs, axis=0)


def _hgrn_kernel(*refs, rows, has_s0):
    if has_s0:
        (q_ref, f_ref, i_ref, g_ref, lb_ref, hn_ref, s0_ref, o_ref, s_ref,
         st_ref, sel_ref, cm_ref, b_ref, k_ref, pad_ref) = refs
    else:
        (q_ref, f_ref, i_ref, g_ref, lb_ref, hn_ref, o_ref, s_ref,
         st_ref, sel_ref, cm_ref, b_ref, k_ref) = refs
    t = pl.program_id(1)
    nrow = HGRN_SEQS * SUB

    @pl.when((t == 0) & (pl.program_id(0) == 0))
    def _():
        sel_ref[...] = _block_select()
        r = lax.broadcasted_iota(I32, (nrow, nrow), 0)
        c = lax.broadcasted_iota(I32, (nrow, nrow), 1)
        shift = SUB.bit_length() - 1
        same = (r >> shift) == (c >> shift)
        cm_ref[0] = (same & (r >= c)).astype(BF16)
        cm_ref[1] = same.astype(BF16)

    lb = lb_ref[...]
    hn = hn_ref[...]
    if has_s0:
        for n in range(HGRN_SEQS):
            for h in range(H_A):
                st_ref[n, h] = s0_ref[n, h].T
        pad_ref[...] = jnp.zeros_like(pad_ref)
        for j, ref in enumerate((q_ref, f_ref, i_ref, g_ref)):
            for n in range(HGRN_SEQS):
                pad_ref[j, n, 0:rows, :] = ref[n * rows:(n + 1) * rows, :]
        o = _hgrn_step(*(pad_ref[j].reshape(nrow, D_A) for j in range(4)), lb, hn,
                       st_ref, sel_ref, cm_ref, b_ref, k_ref, rows)
        for n in range(HGRN_SEQS):
            o_ref[n * rows:(n + 1) * rows, :] = o[n * SUB:n * SUB + rows, :]
    else:
        @pl.when(t == 0)
        def _():
            st_ref[...] = jnp.zeros_like(st_ref)

        def body(j, carry):
            rs = pl.ds(pl.multiple_of(j * SUB, SUB), SUB)
            o = _hgrn_step(*(ref[:, rs, :].reshape(nrow, D_A) for ref in (q_ref, f_ref, i_ref, g_ref)),
                           lb, hn, st_ref, sel_ref, cm_ref, b_ref, k_ref, SUB)
            o_ref[:, rs, :] = o.reshape(HGRN_SEQS, SUB, D_A)
            return carry

        lax.fori_loop(0, rows // SUB, body, 0)

    def emit_state():
        for n in range(HGRN_SEQS):
            for h in range(H_A):
                s_ref[n, h] = st_ref[n, h].T

    if has_s0:
        emit_state()
    else:
        pl.when(t == pl.num_programs(1) - 1)(emit_state)


def _hgrn(u3, s0, lb, hn, seq=None):
    has_s0 = s0 is not None
    if has_s0:
        batch = u3.shape[0] // seq
        rows = seq
        grid = (batch // HGRN_SEQS, 1)
        blk = lambda cb: pl.BlockSpec((HGRN_SEQS * rows, D_A), lambda b, t: (b, cb))
        o_shape = (batch * seq, D_A)
    else:
        batch, seq, _ = u3.shape
        rows = 256
        grid = (batch // HGRN_SEQS, seq // rows)
        blk = lambda cb: pl.BlockSpec((HGRN_SEQS, rows, D_A), lambda b, t: (b, t, cb))
        o_shape = (batch, seq, D_A)
    sblk = pl.BlockSpec((HGRN_SEQS, H_A, DK, DV), lambda b, t: (b, 0, 0, 0))
    vec = pl.BlockSpec((1, D_A), lambda b, t: (0, 0))
    nrow = HGRN_SEQS * SUB
    scratch = [pltpu.VMEM((HGRN_SEQS, H_A, DV, DK), F32), pltpu.VMEM((SUB, SUB * DK), BF16),
               pltpu.VMEM((2, nrow, nrow), BF16), pltpu.VMEM((H_A, nrow, DK), F32),
               pltpu.VMEM((H_A, nrow, DK), F32)]
    if has_s0:
        scratch.append(pltpu.VMEM((4, HGRN_SEQS, SUB, D_A), F32))
    args = (u3, u3, u3, u3, lb, hn) + ((s0,) if has_s0 else ())
    return pl.pallas_call(
        functools.partial(_hgrn_kernel, rows=rows, has_s0=has_s0),
        grid=grid,
        in_specs=[blk(0), blk(1), blk(2), blk(3), vec, vec] + ([sblk] if has_s0 else []),
        out_specs=[blk(0), sblk],
        out_shape=[jax.ShapeDtypeStruct(o_shape, F32),
                   jax.ShapeDtypeStruct((batch, H_A, DK, DV), F32)],
        scratch_shapes=scratch,
        compiler_params=_cparams(("arbitrary", "arbitrary")),
        name="hgrn_state" if has_s0 else "hgrn_scan",
    )(*args)


def _pool_groups(full_ref, base, tm, cnt_fn, wp_ref, ps_ref, step=1):
    outs = []
    for gi, w in enumerate(POOL_WINDOWS):
        ch = slice(POOL_GROUP * gi, POOL_GROUP * (gi + 1))
        v = full_ref[base:base + tm, ch]
        s = v
        for j in range(1, w):
            s = s + full_ref[base - j * step:base - j * step + tm, ch]
        d = s / cnt_fn(w) - v
        outs.append(jnp.dot(d.astype(BF16), wp_ref[gi], preferred_element_type=F32))
    return jnp.concatenate(outs, axis=1) * ps_ref[...]


def _pool_prompt_kernel(v_ref, prev_ref, wp_ref, ps_ref, o_ref, hist_ref, full_ref, *, tm):
    t = pl.program_id(1)
    prev = prev_ref[...]
    full_ref[0:16, :] = jnp.where(t == 0, 0.0, prev)
    full_ref[16:16 + tm, :] = v_ref[...]
    pos = (lax.broadcasted_iota(I32, (tm, 1), 0) + t * tm + 1).astype(F32)
    o_ref[...] = _pool_groups(full_ref, 16, tm, lambda w: jnp.minimum(pos, float(w)),
                              wp_ref, ps_ref)

    @pl.when(t == pl.num_programs(1) - 1)
    def _():
        hist_ref[...] = full_ref[16 + tm - POOL_HIST:16 + tm, :]


def _pool_prompt(u, batch, seq, w_pool_bf, pool_scale):
    tm = min(2048, seq)
    nt = seq // tm
    vcol = (4 * D_A) // D_B
    return pl.pallas_call(
        functools.partial(_pool_prompt_kernel, tm=tm),
        grid=(batch, nt),
        in_specs=[pl.BlockSpec((tm, D_B), lambda b, t: (b * nt + t, vcol)),
                  pl.BlockSpec((16, D_B),
                               lambda b, t: (jnp.maximum((b * nt + t) * (tm // 16) - 1, 0), vcol)),
                  pl.BlockSpec((len(POOL_WINDOWS), POOL_GROUP, POOL_GROUP), lambda b, t: (0, 0, 0)),
                  pl.BlockSpec((1, D_B), lambda b, t: (0, 0))],
        out_specs=[pl.BlockSpec((tm, D_B), lambda b, t: (b * nt + t, 0)),
                   pl.BlockSpec((None, POOL_HIST, D_B), lambda b, t: (b, 0, 0))],
        out_shape=[jax.ShapeDtypeStruct((batch * seq, D_B), F32),
                   jax.ShapeDtypeStruct((batch, POOL_HIST, D_B), F32)],
        scratch_shapes=[pltpu.VMEM((16 + tm, D_B), F32)],
        compiler_params=_cparams(("arbitrary", "arbitrary")),
        name="pool_prompt",
    )(u, u, w_pool_bf, pool_scale)


def _pool_sample_kernel(v_ref, hist_ref, wp_ref, ps_ref, o_ref, nh_ref, full_ref, *,
                        seq, batch, start):
    full_ref[0:batch, :] = jnp.zeros((batch, D_B), F32)
    full_ref[batch:16 * batch, :] = hist_ref[...].reshape(POOL_HIST * batch, D_B)
    full_ref[16 * batch:(16 + seq) * batch, :] = v_ref[...].reshape(seq * batch, D_B)

    def cnt(w):
        rows = [jnp.full((batch, 1), float(min(start + t + 1, w)), F32) for t in range(seq)]
        return jnp.concatenate(rows, axis=0)

    o = _pool_groups(full_ref, 16 * batch, seq * batch, cnt, wp_ref, ps_ref, step=batch)
    o_ref[...] = o.reshape(seq, batch, D_B)
    nh_ref[...] = full_ref[(16 + seq - POOL_HIST) * batch:(16 + seq) * batch, :].reshape(
        POOL_HIST, batch, D_B)


def _pool_sample(u_t, hist_t, start, w_pool_bf, pool_scale):
    seq, batch, _ = u_t.shape
    vcol = (4 * D_A) // D_B
    return pl.pallas_call(
        functools.partial(_pool_sample_kernel, seq=seq, batch=batch, start=start),
        grid=(1,),
        in_specs=[pl.BlockSpec((seq, batch, D_B), lambda i: (0, 0, vcol)),
                  pl.BlockSpec((POOL_HIST, batch, D_B), lambda i: (0, 0, 0)),
                  pl.BlockSpec((len(POOL_WINDOWS), POOL_GROUP, POOL_GROUP), lambda i: (0, 0, 0)),
                  pl.BlockSpec((1, D_B), lambda i: (0, 0))],
        out_specs=[pl.BlockSpec((seq, batch, D_B), lambda i: (0, 0, 0)),
                   pl.BlockSpec((POOL_HIST, batch, D_B), lambda i: (0, 0, 0))],
        out_shape=[jax.ShapeDtypeStruct((seq, batch, D_B), F32),
                   jax.ShapeDtypeStruct((POOL_HIST, batch, D_B), F32)],
        scratch_shapes=[pltpu.VMEM(((16 + seq) * batch, D_B), F32)],
        compiler_params=_cparams(("arbitrary",)),
        name="pool_sample",
    )(u_t, hist_t, w_pool_bf, pool_scale)


def _store_slabs(ref, value):
    n = value.shape[0]
    for j in range(SLAB):
        ref[pl.ds(j, n, stride=SLAB), :] = value[:, LANES * j:LANES * (j + 1)]


def _load_slabs(ref, n):
    return [ref[pl.ds(j, n, stride=SLAB), :] for j in range(SLAB)]


def _slab_rows(row):
    return pl.ds(pl.multiple_of(row * SLAB, SLAB), SLAB)


def _outproj_kernel(x_ref, oa_ref, ob_ref, g1_ref, sh2_ref, sc2_ref, n2_ref, woa_ref, wob_ref,
                    wr_ref, br_ref, *rest):
    x1_ref, h2_ref, idx_ref, prob_ref, cnt_ref, lg_ref = rest[-6:]
    i = pl.program_id(0)
    slot = i % 2

    @pl.when(i == 0)
    def _():
        lg_ref[...] = jnp.zeros_like(lg_ref)

    logits = lg_ref[1 - slot]

    mix = (jnp.dot(oa_ref[...].astype(BF16), woa_ref[...], preferred_element_type=F32)
           + jnp.dot(ob_ref[...].astype(BF16), wob_ref[...], preferred_element_type=F32))
    x1 = x_ref[...] + g1_ref[...] * mix
    x1_ref[...] = x1
    h2 = x1 * lax.rsqrt(jnp.mean(x1 * x1, axis=-1, keepdims=True) + EPS) * n2_ref[...]
    h2 = h2 * (1.0 + sc2_ref[...]) + sh2_ref[...]
    _store_slabs(h2_ref, h2)
    lg_ref[slot] = jnp.dot(h2.astype(BF16), wr_ref[...], preferred_element_type=F32) + br_ref[...]

    tm = logits.shape[0]
    lane = lax.broadcasted_iota(I32, (tm, N_EXPERTS), 1)
    kcol = lax.broadcasted_iota(I32, (tm, TOP_K), 1)
    vals = jnp.zeros((tm, TOP_K), F32)
    idxs = jnp.zeros((tm, TOP_K), I32)
    for k in range(TOP_K):
        m = jnp.max(logits, axis=-1, keepdims=True)
        am = jnp.min(jnp.where(logits == m, lane, N_EXPERTS), axis=-1, keepdims=True)
        vals = jnp.where(kcol == k, m, vals)
        idxs = jnp.where(kcol == k, am, idxs)
        logits = jnp.where(lane == am, -jnp.inf, logits)
    e = jnp.exp(vals - vals[:, 0:1])
    prob_ref[...] = e / jnp.sum(e, axis=-1, keepdims=True)
    idx_ref[...] = idxs
    member = jnp.zeros((tm, N_EXPERTS), F32)
    for k in range(TOP_K):
        member = member + (idxs[:, k:k + 1] == lane).astype(F32)
    cnt_ref[...] = jnp.sum(member, axis=0, keepdims=True)


def _outproj(x, o_a, o_b, mod, per_token, seq, norm2, w_out_bf, w_router_bf, b_router,
             n_total, first, joint=None):
    n = x.shape[0]
    tm = min(OUT_TILE, n)
    tile0 = first // tm
    tps = max(seq // tm, 1)
    nt = n // tm
    cur = lambda i: jnp.minimum(i, nt - 1)
    prv = lambda i: jnp.maximum(i - 1, 0)
    row = lambda width: pl.BlockSpec((tm, width), lambda i: (cur(i), 0))
    jrow = lambda width: pl.BlockSpec((tm, width), lambda i: (prv(i) + tile0, 0))
    const = lambda shape: pl.BlockSpec(shape, lambda i: (0,) * len(shape))
    joint = () if joint is None else tuple(joint)
    n_in = 11
    return pl.pallas_call(
        _outproj_kernel,
        grid=(nt + 1,),
        in_specs=[row(D_MODEL), row(D_A), row(D_B),
                  _mod_spec(per_token, tm, tps, 2, nt - 1),
                  _mod_spec(per_token, tm, tps, 3, nt - 1),
                  _mod_spec(per_token, tm, tps, 4, nt - 1),
                  const((1, D_MODEL)),
                  pl.BlockSpec((D_A, D_MODEL), lambda i: (0, 0)),
                  pl.BlockSpec((D_B, D_MODEL), lambda i: (1, 0)),
                  const((D_MODEL, N_EXPERTS)), const((1, N_EXPERTS))]
                 + [pl.BlockSpec(memory_space=pl.ANY)] * len(joint),
        out_specs=[row(D_MODEL), pl.BlockSpec((tm * SLAB, LANES), lambda i: (cur(i) + tile0, 0)),
                   jrow(TOP_K), jrow(TOP_K),
                   pl.BlockSpec((None, 1, N_EXPERTS), lambda i: (prv(i) + tile0, 0, 0))],
        out_shape=[jax.ShapeDtypeStruct((n, D_MODEL), F32),
                   jax.ShapeDtypeStruct((n_total * SLAB, LANES), F32),
                   jax.ShapeDtypeStruct((n_total, TOP_K), I32),
                   jax.ShapeDtypeStruct((n_total, TOP_K), F32),
                   jax.ShapeDtypeStruct((n_total // tm, 1, N_EXPERTS), F32)],
        input_output_aliases={n_in + j: 1 + j for j in range(len(joint))},
        scratch_shapes=[pltpu.VMEM((2, tm, N_EXPERTS), F32)],
        compiler_params=_cparams(("arbitrary",)),
        name="outproj",
    )(x, o_a, o_b, mod, mod, mod, norm2.reshape(1, D_MODEL), w_out_bf, w_out_bf,
      w_router_bf, b_router.reshape(1, N_EXPERTS), *joint)


def _rank_kernel(idx_ref, start_ref, dest_ref, carry_ref, *, te):
    i = pl.program_id(0)
    idx = idx_ref[...]
    lane = lax.broadcasted_iota(I32, (te, N_EXPERTS), 1)
    onehots = [(idx[:, k:k + 1] == lane) for k in range(TOP_K)]
    member = jnp.zeros((te, N_EXPERTS), F32)
    for oh in onehots:
        member = member + oh.astype(F32)

    @pl.when(i == 0)
    def _():
        carry_ref[...] = start_ref[...]

    r = lax.broadcasted_iota(I32, (te, te), 0)
    c = lax.broadcasted_iota(I32, (te, te), 1)
    earlier = (r > c).astype(BF16)
    base = carry_ref[...] + jnp.dot(earlier, member.astype(BF16), preferred_element_type=F32)
    kcol = lax.broadcasted_iota(I32, (te, TOP_K), 1)
    dest = jnp.zeros((te, TOP_K), F32)
    for k, oh in enumerate(onehots):
        dk = jnp.sum(jnp.where(oh, base, 0.0), axis=-1, keepdims=True)
        dest = jnp.where(kcol == k, dk, dest)
    dest_ref[...] = dest.astype(I32)
    carry_ref[...] += jnp.sum(member, axis=0, keepdims=True)


def _rank(top_idx, starts):
    n = top_idx.shape[0]
    te = 768 if n % 768 == 0 else 512
    return pl.pallas_call(
        functools.partial(_rank_kernel, te=te),
        grid=(n // te,),
        in_specs=[pl.BlockSpec((te, TOP_K), lambda i: (i, 0)),
                  pl.BlockSpec((1, N_EXPERTS), lambda i: (0, 0))],
        out_specs=pl.BlockSpec((te, TOP_K), lambda i: (i, 0)),
        out_shape=jax.ShapeDtypeStruct((n, TOP_K), I32),
        scratch_shapes=[pltpu.VMEM((1, N_EXPERTS), F32)],
        compiler_params=_cparams(("arbitrary",)),
        name="rank",
    )(top_idx, starts)


def _dispatch_kernel(last_ref, ntile_ref, used_ref, dest_ref, h_ref, xs_ref, zero_ref, sem, zsem,
                     *, n_tiles):
    i = pl.program_id(0)

    @pl.when(i == 0)
    def _():
        zero_ref[...] = jnp.zeros_like(zero_ref)

        def clear(tile):
            rows = pl.ds(pl.multiple_of(tile * (ROW_TILE * SLAB), ROW_TILE * SLAB), ROW_TILE * SLAB)
            return pltpu.make_async_copy(zero_ref, xs_ref.at[rows], zsem)

        def over_tail(fn):
            def body(t, carry):
                fn(clear(t))
                return carry
            lax.fori_loop(used_ref[0], n_tiles, body, 0)

        for fn in (lambda c: c.start(), lambda c: c.wait()):
            for e in range(N_EXPERTS):
                @pl.when(ntile_ref[e] > 0)
                def _():
                    fn(clear(last_ref[e]))
            over_tail(fn)

    def body(r, carry):
        for k in range(TOP_K):
            d = dest_ref[r * TOP_K + k]
            pltpu.make_async_copy(h_ref.at[_slab_rows(r)], xs_ref.at[_slab_rows(d)], sem).start(
                priority=k % 2)
        return carry

    lax.fori_loop(0, TOK_TILE, body, 0, unroll=2)
    for k in range(TOP_K):
        pltpu.make_async_copy(h_ref, xs_ref.at[pl.ds(0, TOK_TILE * SLAB)], sem).wait()


def _dispatch(h2, dest_flat, last_tile, ntiles, used, n_tiles):
    n = h2.shape[0] // SLAB
    grid_spec = pltpu.PrefetchScalarGridSpec(
        num_scalar_prefetch=3,
        grid=(n // TOK_TILE,),
        in_specs=[pl.BlockSpec((TOK_TILE * TOP_K,), lambda i, *_: (i,), memory_space=pltpu.SMEM),
                  pl.BlockSpec((TOK_TILE * SLAB, LANES), lambda i, *_: (i, 0))],
        out_specs=pl.BlockSpec(memory_space=pl.ANY),
        scratch_shapes=[pltpu.VMEM((ROW_TILE * SLAB, LANES), F32),
                        pltpu.SemaphoreType.DMA(()), pltpu.SemaphoreType.DMA(())],
    )
    return pl.pallas_call(
        functools.partial(_dispatch_kernel, n_tiles=n_tiles),
        grid_spec=grid_spec,
        out_shape=jax.ShapeDtypeStruct((n_tiles * ROW_TILE * SLAB, LANES), F32),
        compiler_params=_cparams(("arbitrary",)),
        name="dispatch",
    )(last_tile, ntiles, used, dest_flat, h2)


def _experts_kernel(te_ref, used_ref, grp_ref, nxt_ref, xs_ref, wgu_ref, bgu_ref, wd_ref, bd_ref,
                    ys_ref, wgu_f, wd_f, wgu_bf, wd_bf, act_bf, xring, yring, wsem, xsem, ysem,
                    *, n_tiles):
    used = used_ref[0]
    tile_rows = ROW_TILE * SLAB

    def weights(e, ws):
        return (pltpu.make_async_copy(wgu_ref.at[e], wgu_f.at[ws], wsem.at[ws, 0]),
                pltpu.make_async_copy(wd_ref.at[e], wd_f.at[ws], wsem.at[ws, 1]))

    def tile_rows_of(t):
        return pl.ds(pl.multiple_of(t * tile_rows, tile_rows), tile_rows)

    def x_tile(t):
        t = jnp.asarray(t, I32)
        return pltpu.make_async_copy(xs_ref.at[tile_rows_of(t)], xring.at[t % X_RING],
                                     xsem.at[t % X_RING])

    def y_tile(t):
        return pltpu.make_async_copy(yring.at[t % 2], ys_ref.at[tile_rows_of(t)], ysem.at[t % 2])

    for cp in weights(te_ref[0], 0):
        cp.start()
    for t in range(X_RING - 1):
        @pl.when(t < used)
        def _():
            x_tile(t).start()

    def step(i, carry):
        @pl.when(i + X_RING - 1 < used)
        def _():
            x_tile(i + X_RING - 1).start()

        e = te_ref[i]
        ws = grp_ref[i] % 2
        fresh = (i == 0) | (e != te_ref[jnp.maximum(i - 1, 0)])

        @pl.when(fresh)
        def _():
            for cp in weights(e, ws):
                cp.wait()

            @pl.when(nxt_ref[i] >= 0)
            def _():
                for cp in weights(nxt_ref[i], 1 - ws):
                    cp.start()

            wgu_bf[...] = wgu_f[ws].astype(BF16)
            wd_bf[...] = wd_f[ws].astype(BF16)

        x_tile(i).wait()

        @pl.when(i >= 2)
        def _():
            y_tile(i - 2).wait()

        x = jnp.concatenate([p.astype(BF16) for p in _load_slabs(xring.at[i % X_RING], ROW_TILE)],
                            axis=1)
        bgu = bgu_ref[pl.ds(e, 1), :]
        bd = bd_ref[pl.ds(e, 1), :]
        cw = D_FF // FF_CHUNKS
        for c in range(FF_CHUNKS):
            gate = jnp.dot(x, wgu_bf[:, c * cw:(c + 1) * cw], preferred_element_type=F32)
            gate = jnp.minimum(gate + bgu[:, c * cw:(c + 1) * cw], SWIGLU_LIMIT)
            up = jnp.dot(x, wgu_bf[:, D_FF + c * cw:D_FF + (c + 1) * cw], preferred_element_type=F32)
            up = jnp.clip(up + bgu[:, D_FF + c * cw:D_FF + (c + 1) * cw], -SWIGLU_LIMIT, SWIGLU_LIMIT)
            act = gate * jax.nn.sigmoid(SWIGLU_ALPHA * gate) * (up + 1.0)
            act_bf[:, c * cw:(c + 1) * cw] = act.astype(BF16)
        ow = D_MODEL // FF_CHUNKS
        for c in range(FF_CHUNKS):
            out = jnp.dot(act_bf[...], wd_bf[:, c * ow:(c + 1) * ow], preferred_element_type=F32)
            out = out + bd[:, c * ow:(c + 1) * ow]
            for j in range(ow // LANES):
                yring[i % 2, pl.ds(c * (ow // LANES) + j, ROW_TILE, stride=SLAB), :] = (
                    out[:, LANES * j:LANES * (j + 1)])
        y_tile(i).start()
        return carry

    lax.fori_loop(0, used, step, 0)

    @pl.when(used >= 2)
    def _():
        y_tile(used - 2).wait()
    y_tile(used - 1).wait()

    yring[0] = jnp.zeros(yring.shape[1:], F32)

    def clear(fn):
        def body(t, carry):
            fn(pltpu.make_async_copy(yring.at[0], ys_ref.at[tile_rows_of(t)], ysem.at[0]))
            return carry
        lax.fori_loop(used, n_tiles, body, 0)

    clear(lambda cp: cp.start())
    clear(lambda cp: cp.wait())


def _experts(xs, tile_expert, used, grp, nxt, w_gu, b_gu, w_down, b_down):
    rows = xs.shape[0] // SLAB
    nt = rows // ROW_TILE
    whole = lambda shape: pl.BlockSpec(shape, lambda i, *_: (0,) * len(shape))
    grid_spec = pltpu.PrefetchScalarGridSpec(
        num_scalar_prefetch=4,
        grid=(1,),
        in_specs=[pl.BlockSpec(memory_space=pl.ANY),
                  pl.BlockSpec(memory_space=pl.ANY),
                  whole((N_EXPERTS, 2 * D_FF)),
                  pl.BlockSpec(memory_space=pl.ANY),
                  whole((N_EXPERTS, D_MODEL))],
        out_specs=pl.BlockSpec(memory_space=pl.ANY),
        scratch_shapes=[pltpu.VMEM((2, D_MODEL, 2 * D_FF), F32), pltpu.VMEM((2, D_FF, D_MODEL), F32),
                        pltpu.VMEM((D_MODEL, 2 * D_FF), BF16), pltpu.VMEM((D_FF, D_MODEL), BF16),
                        pltpu.VMEM((ROW_TILE, D_FF), BF16),
                        pltpu.VMEM((X_RING, ROW_TILE * SLAB, LANES), F32),
                        pltpu.VMEM((2, ROW_TILE * SLAB, LANES), F32),
                        pltpu.SemaphoreType.DMA((2, 2)), pltpu.SemaphoreType.DMA((X_RING,)),
                        pltpu.SemaphoreType.DMA((2,))],
    )
    return pl.pallas_call(
        functools.partial(_experts_kernel, n_tiles=nt),
        grid_spec=grid_spec,
        out_shape=jax.ShapeDtypeStruct((rows * SLAB, LANES), F32),
        compiler_params=_cparams(("arbitrary",)),
        name="experts",
    )(tile_expert, used, grp, nxt, xs, w_gu, b_gu, w_down, b_down)


def _combine_kernel(dest_ref, next_ref, ys_ref, x1_ref, p_ref, g2_ref, fn_ref, y_ref, buf_ref, sem,
                    *, last):
    i = pl.program_id(0)
    slot = i % 2

    def issue(d_ref, sl):
        def body(r, carry):
            for k in range(TOP_K):
                d = d_ref[r * TOP_K + k]
                pltpu.make_async_copy(ys_ref.at[_slab_rows(d)], buf_ref.at[sl, k, _slab_rows(r)],
                                      sem.at[sl]).start(priority=k % 2)
            return carry
        lax.fori_loop(0, TOK_TILE, body, 0, unroll=2)

    @pl.when(i == 0)
    def _():
        issue(dest_ref, 0)

    @pl.when(i + 1 < pl.num_programs(0))
    def _():
        issue(next_ref, 1 - slot)

    for k in range(TOP_K):
        pltpu.make_async_copy(ys_ref.at[pl.ds(0, TOK_TILE * SLAB)], buf_ref.at[slot, k],
                              sem.at[slot]).wait()

    p = p_ref[...]
    groups = None
    for k in range(TOP_K):
        part = [p[:, k:k + 1] * g for g in _load_slabs(buf_ref.at[slot, k], TOK_TILE)]
        groups = part if groups is None else [a + b for a, b in zip(groups, part)]
    moe = jnp.concatenate(groups, axis=1)
    x2 = x1_ref[...] + g2_ref[...] * moe
    if last:
        x2 = x2 * lax.rsqrt(jnp.mean(x2 * x2, axis=-1, keepdims=True) + EPS) * fn_ref[...]
    y_ref[...] = x2


def _combine(ys, dest_flat, x1, probs, mod, per_token, seq, tile0, final_norm, last):
    n = x1.shape[0]
    tm = TOK_TILE
    tps = max(seq // tm, 1)
    nt = n // tm
    return pl.pallas_call(
        functools.partial(_combine_kernel, last=last),
        grid=(nt,),
        in_specs=[pl.BlockSpec((tm * TOP_K,), lambda i: (i + tile0,), memory_space=pltpu.SMEM),
                  pl.BlockSpec((tm * TOP_K,), lambda i: (jnp.minimum(i + 1, nt - 1) + tile0,),
                               memory_space=pltpu.SMEM),
                  pl.BlockSpec(memory_space=pl.ANY),
                  pl.BlockSpec((tm, D_MODEL), lambda i: (i, 0)),
                  pl.BlockSpec((tm, TOP_K), lambda i: (i + tile0, 0)),
                  _mod_spec(per_token, tm, tps, 5),
                  pl.BlockSpec((1, D_MODEL), lambda i: (0, 0))],
        out_specs=pl.BlockSpec((tm, D_MODEL), lambda i: (i, 0)),
        out_shape=jax.ShapeDtypeStruct((n, D_MODEL), F32),
        scratch_shapes=[pltpu.VMEM((2, TOP_K, tm * SLAB, LANES), F32),
                        pltpu.SemaphoreType.DMA((2,))],
        compiler_params=_cparams(("arbitrary",)),
        name="combine",
    )(dest_flat, dest_flat, ys, x1, probs, mod, final_norm.reshape(1, D_MODEL))


def kernel(x_prompt, x_sample, state_hgrn, state_pool, c_prompt, c_sample, w_ada, b_ada, norm1,
           norm2, w_in, lower_bounds, hgrn_norm, w_pool, pool_scale, w_out, w_router, b_router,
           w_gu, b_gu, w_down, b_down, final_norm):
    bp, seq_p, _ = x_prompt.shape
    bs, seq_s, _ = x_sample.shape
    np_, ns = bp * seq_p, bs * seq_s
    depth = w_ada.shape[0]

    lbs = jnp.cumsum(jax.nn.softmax(lower_bounds.astype(F32), axis=0), axis=0)
    xp = x_prompt.reshape(np_, D_MODEL)
    xs_ = x_sample.reshape(ns, D_MODEL)
    c_all = jnp.concatenate([jnp.repeat(c_sample, seq_s, axis=0), c_prompt], axis=0)

    sp_l, hp_l, ss_l, hs_l = [], [], [], []
    for l in range(depth):
        w_in_bf = w_in[l].astype(BF16)
        w_out_bf = w_out[l].astype(BF16)
        w_pool_bf = w_pool[l].astype(BF16)
        w_router_bf = w_router[l].astype(BF16)
        lb = lbs[l].reshape(1, D_A)
        hn = hgrn_norm[l].reshape(1, D_A)
        psc = pool_scale[l].reshape(1, D_B)

        mod = _ada(c_all, w_ada[l], b_ada[l])
        mod_p = mod[ns:].reshape(bp, 1, 6 * D_MODEL)
        mod_s = mod

        up = _inproj(xp, mod_p, False, seq_p, norm1[l], w_in_bf)
        us = _inproj(xs_, mod_s, True, seq_s, norm1[l], w_in_bf)

        oa_p, s_p = _hgrn(up.reshape(bp, seq_p, D_IN), None, lb, hn)
        oa_p = oa_p.reshape(np_, D_A)
        oa_s, s_s = _hgrn(us, state_hgrn[l], lb, hn, seq=seq_s)
        ob_p, h_p = _pool_prompt(up, bp, seq_p, w_pool_bf, psc)
        ob_t, h_t = _pool_sample(us.reshape(bs, seq_s, D_IN).transpose(1, 0, 2),
                                 state_pool[l].transpose(1, 0, 2), PAST_LEN, w_pool_bf, psc)
        ob_s, h_s = ob_t.transpose(1, 0, 2).reshape(ns, D_B), h_t.transpose(1, 0, 2)

        n_tok = np_ + ns
        n_tiles = (n_tok * TOP_K) // ROW_TILE + N_EXPERTS
        x1p, *joint = _outproj(xp, oa_p, ob_p, mod_p, False, seq_p, norm2[l], w_out_bf,
                               w_router_bf, b_router[l], n_tok, 0)
        x1s, h2, top_idx, probs, tile_counts = _outproj(
            xs_, oa_s, ob_s, mod_s, True, seq_s, norm2[l], w_out_bf, w_router_bf, b_router[l],
            n_tok, np_, joint)

        counts = jnp.sum(tile_counts, axis=(0, 1)).astype(I32)
        tiles_e = (counts + ROW_TILE - 1) // ROW_TILE
        ends = jnp.cumsum(tiles_e)
        starts = ((ends - tiles_e) * ROW_TILE).astype(F32).reshape(1, N_EXPERTS)
        dest = _rank(top_idx, starts)
        dest_flat = dest.reshape(n_tok * TOP_K)
        used = ends[-1:].astype(I32)
        tile_expert = jnp.minimum(
            jnp.sum(jnp.arange(n_tiles, dtype=I32)[:, None] >= ends[None, :], axis=1),
            N_EXPERTS - 1).astype(I32)
        last_tile = (ends - 1).astype(I32)
        grp = jnp.cumsum(jnp.concatenate(
            [jnp.zeros((1,), I32), (tile_expert[1:] != tile_expert[:-1]).astype(I32)])).astype(I32)
        eids = jnp.arange(N_EXPERTS, dtype=I32)
        later = (eids[None, :] > eids[:, None]) & (tiles_e > 0)[None, :]
        nxt_e = jnp.min(jnp.where(later, eids[None, :], N_EXPERTS), axis=1)
        nxt = jnp.where(nxt_e < N_EXPERTS, nxt_e, -1)[tile_expert].astype(I32)

        xsort = _dispatch(h2, dest_flat, last_tile, tiles_e.astype(I32), used, n_tiles)
        ysort = _experts(xsort, tile_expert, used, grp, nxt, w_gu[l], b_gu[l], w_down[l], b_down[l])

        last = l == depth - 1
        xp = _combine(ysort, dest_flat, x1p, probs, mod_p, False, seq_p, 0, final_norm, last)
        xs_ = _combine(ysort, dest_flat, x1s, probs, mod_s, True, seq_s, np_ // TOK_TILE,
                       final_norm, last)

        sp_l.append(s_p)
        hp_l.append(h_p)
        ss_l.append(s_s)
        hs_l.append(h_s)

    return (xp.reshape(bp, seq_p, D_MODEL), xs_.reshape(bs, seq_s, D_MODEL),
            jnp.stack(sp_l), jnp.stack(hp_l), jnp.stack(ss_l), jnp.stack(hs_l))
```
